```python
import jax, jax.numpy as jnp
from jax import lax
import numpy as np

D_MODEL = 1024
BATCH = 8
SEQ = 4096
DEPTH = 1

C_CONV = D_MODEL
CONV_WIDTH = 31
POOL_WINDOWS = (2, 4, 8, 16)
N_POOL_GROUPS = len(POOL_WINDOWS)
C_POOL = D_MODEL
POOL_GROUP = C_POOL // N_POOL_GROUPS
N_BRANCHES = 2
D_FF = 4 * D_MODEL
IN_COLS = 2 * C_CONV + C_POOL + N_BRANCHES * D_MODEL
RMS_EPS = 1e-6
LN_EPS = 1e-5

kernel_name = "hybrid_conv_pool_gated_block"


def rms_norm(x, g):
    xf = x.astype(jnp.float32)
    y = xf * lax.rsqrt(jnp.mean(xf * xf, axis=-1, keepdims=True) + RMS_EPS)
    return (y * g.astype(jnp.float32)).astype(x.dtype)


def layer_norm(x, g, b):
    xf = x.astype(jnp.float32)
    mu = jnp.mean(xf, axis=-1, keepdims=True)
    var = jnp.mean(jnp.square(xf - mu), axis=-1, keepdims=True)
    y = (xf - mu) * lax.rsqrt(var + LN_EPS)
    return (y * g.astype(jnp.float32) + b.astype(jnp.float32)).astype(x.dtype)


def depthwise_causal_conv(u, k, b):
    out = lax.conv_general_dilated(
        u, k[:, None, :].astype(u.dtype), window_strides=(1,),
        padding=[(CONV_WIDTH - 1, 0)],
        dimension_numbers=("NWC", "WIO", "NWC"),
        feature_group_count=u.shape[-1])
    return out + b.astype(u.dtype)


def conformer_conv_branch(u_glu, dw_kernel, dw_bias, ln_g, ln_b, w_out):
    a, gate = jnp.split(u_glu, 2, axis=-1)
    u = a * jax.nn.sigmoid(gate)
    u = depthwise_causal_conv(u, dw_kernel, dw_bias)
    u = layer_norm(u, ln_g, ln_b)
    u = jax.nn.swish(u)
    return u @ w_out


def causal_multiscale_pool(p):
    B, S, _ = p.shape
    pg = p.reshape(B, S, N_POOL_GROUPS, POOL_GROUP)
    cs = jnp.cumsum(pg.astype(jnp.float32), axis=1)
    pos = jnp.arange(1, S + 1, dtype=jnp.int32)
    outs = []
    for g, w in enumerate(POOL_WINDOWS):
        c = cs[:, :, g]
        prev = jnp.pad(c[:, :-w], ((0, 0), (w, 0), (0, 0)))
        cnt = jnp.minimum(pos, w).astype(jnp.float32)[None, :, None]
        outs.append((c - prev) / cnt)
    pooled = jnp.stack(outs, axis=2) - pg.astype(jnp.float32)
    return pooled.astype(p.dtype)


def pooling_branch(p, pool_w, pool_scale, w_out):
    B, S, _ = p.shape
    z = causal_multiscale_pool(p)
    z = jnp.einsum("bsgc,gcd->bsgd", z, pool_w).reshape(B, S, C_POOL)
    z = z * pool_scale
    return z @ w_out


def _fwd_setup_inputs(seed: int = 0) -> dict:
    key = jax.random.key(seed)
    ks = jax.random.split(key, 20)
    f32 = jnp.float32
    nrm = lambda k, shape, s: jax.random.normal(k, shape, f32) * s
    return {
        "x": jax.random.normal(ks[0], (BATCH, SEQ, D_MODEL), f32),
        "mix_pre_g": 1.0 + nrm(ks[1], (D_MODEL,), 0.05),
        "w_in": nrm(ks[2], (D_MODEL, IN_COLS), D_MODEL ** -0.5),
        "dw_kernel": nrm(ks[3], (CONV_WIDTH, C_CONV), CONV_WIDTH ** -0.5),
        "dw_bias": nrm(ks[4], (C_CONV,), 0.02),
        "conv_ln_g": 1.0 + nrm(ks[5], (C_CONV,), 0.05),
        "conv_ln_b": nrm(ks[6], (C_CONV,), 0.02),
        "w_conv_out": nrm(ks[7], (C_CONV, D_MODEL), C_CONV ** -0.5),
        "pool_w": nrm(ks[8], (N_POOL_GROUPS, POOL_GROUP, POOL_GROUP), POOL_GROUP ** -0.5),
        "pool_scale": 1.0 + nrm(ks[9], (C_POOL,), 0.1),
        "w_pool_out": nrm(ks[10], (C_POOL, D_MODEL), C_POOL ** -0.5),
        "w_o": nrm(ks[11], (D_MODEL, D_MODEL), D_MODEL ** -0.5),
        "mix_post_g": 1.0 + nrm(ks[12], (D_MODEL,), 0.05),
        "mlp_pre_g": 1.0 + nrm(ks[13], (D_MODEL,), 0.05),
        "w_ff1": nrm(ks[14], (D_MODEL, D_FF), D_MODEL ** -0.5),
        "w_ff2": nrm(ks[15], (D_FF, D_MODEL), D_FF ** -0.5),
        "mlp_post_g": 1.0 + nrm(ks[16], (D_MODEL,), 0.05),
    }


def _fwd_reference(x, mix_pre_g, w_in, dw_kernel, dw_bias, conv_ln_g, conv_ln_b,
              w_conv_out, pool_w, pool_scale, w_pool_out, w_o, mix_post_g,
              mlp_pre_g, w_ff1, w_ff2, mlp_post_g):
    h = x
    for _ in range(DEPTH):
        u = rms_norm(h, mix_pre_g)
        proj = u @ w_in
        u_glu = proj[..., :2 * C_CONV]
        p = proj[..., 2 * C_CONV:2 * C_CONV + C_POOL]
        gates = jax.nn.sigmoid(proj[..., 2 * C_CONV + C_POOL:])
        g_conv, g_pool = jnp.split(gates, N_BRANCHES, axis=-1)
        y_conv = conformer_conv_branch(u_glu, dw_kernel, dw_bias, conv_ln_g, conv_ln_b, w_conv_out)
        y_pool = pooling_branch(p, pool_w, pool_scale, w_pool_out)
        merged = g_conv * y_conv + g_pool * y_pool
        h = h + rms_norm(merged @ w_o, mix_post_g)
        v = rms_norm(h, mlp_pre_g)
        v = jnp.square(jax.nn.relu(v @ w_ff1)) @ w_ff2
        h = h + rms_norm(v, mlp_post_g)
    return h


import jax as _jax
import jax.numpy as _jnp

TWIN_FORMAT = 'train_step'
FWD_PARAMS = ['x', 'mix_pre_g', 'w_in', 'dw_kernel', 'dw_bias', 'conv_ln_g', 'conv_ln_b', 'w_conv_out', 'pool_w', 'pool_scale', 'w_pool_out', 'w_o', 'mix_post_g', 'mlp_pre_g', 'w_ff1', 'w_ff2', 'mlp_post_g']
TWIN_WEIGHTS = ['mix_pre_g', 'w_in', 'dw_kernel', 'dw_bias', 'conv_ln_g', 'conv_ln_b', 'w_conv_out', 'pool_w', 'pool_scale', 'w_pool_out', 'w_o', 'mix_post_g', 'mlp_pre_g', 'w_ff1', 'w_ff2', 'mlp_post_g']
TWIN_DIFF_INPUT = 'x'
TWIN_INPUTS = ['x', 'mix_pre_g', 'w_in', 'dw_kernel', 'dw_bias', 'conv_ln_g', 'conv_ln_b', 'w_conv_out', 'pool_w', 'pool_scale', 'w_pool_out', 'w_o', 'mix_post_g', 'mlp_pre_g', 'w_ff1', 'w_ff2', 'mlp_post_g', 'loss_target', 'm_mix_pre_g', 'm_w_in', 'm_dw_kernel', 'm_dw_bias', 'm_conv_ln_g', 'm_conv_ln_b', 'm_w_conv_out', 'm_pool_w', 'm_pool_scale', 'm_w_pool_out', 'm_w_o', 'm_mix_post_g', 'm_mlp_pre_g', 'm_w_ff1', 'm_w_ff2', 'm_mlp_post_g', 'v_mix_pre_g', 'v_w_in', 'v_dw_kernel', 'v_dw_bias', 'v_conv_ln_g', 'v_conv_ln_b', 'v_w_conv_out', 'v_pool_w', 'v_pool_scale', 'v_w_pool_out', 'v_w_o', 'v_mix_post_g', 'v_mlp_pre_g', 'v_w_ff1', 'v_w_ff2', 'v_mlp_post_g']
TWIN_OUTPUTS = ['loss', 'grad_x', 'grad_mix_pre_g', 'grad_w_in', 'grad_dw_kernel', 'grad_dw_bias', 'grad_conv_ln_g', 'grad_conv_ln_b', 'grad_w_conv_out', 'grad_pool_w', 'grad_pool_scale', 'grad_w_pool_out', 'grad_w_o', 'grad_mix_post_g', 'grad_mlp_pre_g', 'grad_w_ff1', 'grad_w_ff2', 'grad_mlp_post_g', 'delta_mix_pre_g', 'delta_w_in', 'delta_dw_kernel', 'delta_dw_bias', 'delta_conv_ln_g', 'delta_conv_ln_b', 'delta_w_conv_out', 'delta_pool_w', 'delta_pool_scale', 'delta_w_pool_out', 'delta_w_o', 'delta_mix_post_g', 'delta_mlp_pre_g', 'delta_w_ff1', 'delta_w_ff2', 'delta_mlp_post_g', 'new_m_mix_pre_g', 'new_m_w_in', 'new_m_dw_kernel', 'new_m_dw_bias', 'new_m_conv_ln_g', 'new_m_conv_ln_b', 'new_m_w_conv_out', 'new_m_pool_w', 'new_m_pool_scale', 'new_m_w_pool_out', 'new_m_w_o', 'new_m_mix_post_g', 'new_m_mlp_pre_g', 'new_m_w_ff1', 'new_m_w_ff2', 'new_m_mlp_post_g', 'new_v_mix_pre_g', 'new_v_w_in', 'new_v_dw_kernel', 'new_v_dw_bias', 'new_v_conv_ln_g', 'new_v_conv_ln_b', 'new_v_w_conv_out', 'new_v_pool_w', 'new_v_pool_scale', 'new_v_w_pool_out', 'new_v_w_o', 'new_v_mix_post_g', 'new_v_mlp_pre_g', 'new_v_w_ff1', 'new_v_w_ff2', 'new_v_mlp_post_g']
TWIN_LEAF_KINDS = {'loss': 'loss', 'grad_x': 'grad_x', 'grad_mix_pre_g': 'grad_w', 'grad_w_in': 'grad_w', 'grad_dw_kernel': 'grad_w', 'grad_dw_bias': 'grad_w', 'grad_conv_ln_g': 'grad_w', 'grad_conv_ln_b': 'grad_w', 'grad_w_conv_out': 'grad_w', 'grad_pool_w': 'grad_w', 'grad_pool_scale': 'grad_w', 'grad_w_pool_out': 'grad_w', 'grad_w_o': 'grad_w', 'grad_mix_post_g': 'grad_w', 'grad_mlp_pre_g': 'grad_w', 'grad_w_ff1': 'grad_w', 'grad_w_ff2': 'grad_w', 'grad_mlp_post_g': 'grad_w', 'delta_mix_pre_g': 'delta_w', 'delta_w_in': 'delta_w', 'delta_dw_kernel': 'delta_w', 'delta_dw_bias': 'delta_w', 'delta_conv_ln_g': 'delta_w', 'delta_conv_ln_b': 'delta_w', 'delta_w_conv_out': 'delta_w', 'delta_pool_w': 'delta_w', 'delta_pool_scale': 'delta_w', 'delta_w_pool_out': 'delta_w', 'delta_w_o': 'delta_w', 'delta_mix_post_g': 'delta_w', 'delta_mlp_pre_g': 'delta_w', 'delta_w_ff1': 'delta_w', 'delta_w_ff2': 'delta_w', 'delta_mlp_post_g': 'delta_w', 'new_m_mix_pre_g': 'new_m', 'new_m_w_in': 'new_m', 'new_m_dw_kernel': 'new_m', 'new_m_dw_bias': 'new_m', 'new_m_conv_ln_g': 'new_m', 'new_m_conv_ln_b': 'new_m', 'new_m_w_conv_out': 'new_m', 'new_m_pool_w': 'new_m', 'new_m_pool_scale': 'new_m', 'new_m_w_pool_out': 'new_m', 'new_m_w_o': 'new_m', 'new_m_mix_post_g': 'new_m', 'new_m_mlp_pre_g': 'new_m', 'new_m_w_ff1': 'new_m', 'new_m_w_ff2': 'new_m', 'new_m_mlp_post_g': 'new_m', 'new_v_mix_pre_g': 'new_v', 'new_v_w_in': 'new_v', 'new_v_dw_kernel': 'new_v', 'new_v_dw_bias': 'new_v', 'new_v_conv_ln_g': 'new_v', 'new_v_conv_ln_b': 'new_v', 'new_v_w_conv_out': 'new_v', 'new_v_pool_w': 'new_v', 'new_v_pool_scale': 'new_v', 'new_v_w_pool_out': 'new_v', 'new_v_w_o': 'new_v', 'new_v_mix_post_g': 'new_v', 'new_v_mlp_pre_g': 'new_v', 'new_v_w_ff1': 'new_v', 'new_v_w_ff2': 'new_v', 'new_v_mlp_post_g': 'new_v'}


def _forward(args):
    return _fwd_reference(*[args[k] for k in FWD_PARAMS])


def _output_shape():
    def fwd():
        inp = _fwd_setup_inputs(0)
        return _fwd_reference(*[inp[k] for k in FWD_PARAMS])
    out = _jax.eval_shape(fwd)
    return out.shape, out.dtype

N_MICROBATCH = 1
ADAM_LR = 0.001
ADAM_B1 = 0.9
ADAM_B2 = 0.999
ADAM_EPS = 1e-08
ADAM_WD = 0.01
ADAM_STEP = 10
PER_EXAMPLE_BATCH_AXIS = {'x': 0, 'loss_target': 0}
SHARED_INPUTS = []
_WEIGHT_DTYPES = {'mix_pre_g': _jnp.float32, 'w_in': _jnp.float32, 'dw_kernel': _jnp.float32, 'dw_bias': _jnp.float32, 'conv_ln_g': _jnp.float32, 'conv_ln_b': _jnp.float32, 'w_conv_out': _jnp.float32, 'pool_w': _jnp.float32, 'pool_scale': _jnp.float32, 'w_pool_out': _jnp.float32, 'w_o': _jnp.float32, 'mix_post_g': _jnp.float32, 'mlp_pre_g': _jnp.float32, 'w_ff1': _jnp.float32, 'w_ff2': _jnp.float32, 'mlp_post_g': _jnp.float32}
MOMENT_SCALE = {'mix_pre_g': 8.565605e-01, 'w_in': 3.708979e-01, 'dw_kernel': 6.790922e-01, 'dw_bias': 1.101635e+01, 'conv_ln_g': 4.287112e+00, 'conv_ln_b': 6.231756e+00, 'w_conv_out': 2.514686e+00, 'pool_w': 8.589098e-01, 'pool_scale': 9.721936e-01, 'w_pool_out': 9.291635e-01, 'w_o': 2.763486e+00, 'mix_post_g': 3.231103e+01, 'mlp_pre_g': 1.291600e+00, 'w_ff1': 6.072062e-01, 'w_ff2': 2.733634e+00, 'mlp_post_g': 3.312035e+01}


def _to_microbatches(a, axis):
    t = _jnp.moveaxis(a, axis, 0)
    t = t.reshape((N_MICROBATCH, t.shape[0] // N_MICROBATCH) + t.shape[1:])
    return _jnp.moveaxis(t, 1, axis + 1)


def setup_inputs(seed: int = 0) -> dict:
    inp = _fwd_setup_inputs(seed)
    key = _jax.random.fold_in(_jax.random.key(seed), 7919)
    shape, _ = _output_shape()
    out = dict(inp)
    out["loss_target"] = _jax.random.normal(_jax.random.fold_in(key, 0), shape, _jnp.float32)
    for i, name in enumerate(TWIN_WEIGHTS):
        w = inp[name].astype(_jnp.float32)
        if MOMENT_SCALE is None:
            s = _jnp.sqrt(_jnp.mean(_jnp.square(w)) + 1e-30)
        else:
            s = MOMENT_SCALE[name]
        km, kv = _jax.random.split(_jax.random.fold_in(key, i + 1))
        out[name] = w
        out["m_" + name] = s * _jax.random.normal(km, w.shape, _jnp.float32)
        out["v_" + name] = (s * s) * _jax.random.uniform(kv, w.shape, _jnp.float32, 0.5, 1.5)
    if N_MICROBATCH > 1:
        for name, axis in PER_EXAMPLE_BATCH_AXIS.items():
            out[name] = _to_microbatches(out[name], axis)
    return {'x': out['x'], 'mix_pre_g': out['mix_pre_g'], 'w_in': out['w_in'], 'dw_kernel': out['dw_kernel'], 'dw_bias': out['dw_bias'], 'conv_ln_g': out['conv_ln_g'], 'conv_ln_b': out['conv_ln_b'], 'w_conv_out': out['w_conv_out'], 'pool_w': out['pool_w'], 'pool_scale': out['pool_scale'], 'w_pool_out': out['w_pool_out'], 'w_o': out['w_o'], 'mix_post_g': out['mix_post_g'], 'mlp_pre_g': out['mlp_pre_g'], 'w_ff1': out['w_ff1'], 'w_ff2': out['w_ff2'], 'mlp_post_g': out['mlp_post_g'], 'loss_target': out['loss_target'], 'm_mix_pre_g': out['m_mix_pre_g'], 'm_w_in': out['m_w_in'], 'm_dw_kernel': out['m_dw_kernel'], 'm_dw_bias': out['m_dw_bias'], 'm_conv_ln_g': out['m_conv_ln_g'], 'm_conv_ln_b': out['m_conv_ln_b'], 'm_w_conv_out': out['m_w_conv_out'], 'm_pool_w': out['m_pool_w'], 'm_pool_scale': out['m_pool_scale'], 'm_w_pool_out': out['m_w_pool_out'], 'm_w_o': out['m_w_o'], 'm_mix_post_g': out['m_mix_post_g'], 'm_mlp_pre_g': out['m_mlp_pre_g'], 'm_w_ff1': out['m_w_ff1'], 'm_w_ff2': out['m_w_ff2'], 'm_mlp_post_g': out['m_mlp_post_g'], 'v_mix_pre_g': out['v_mix_pre_g'], 'v_w_in': out['v_w_in'], 'v_dw_kernel': out['v_dw_kernel'], 'v_dw_bias': out['v_dw_bias'], 'v_conv_ln_g': out['v_conv_ln_g'], 'v_conv_ln_b': out['v_conv_ln_b'], 'v_w_conv_out': out['v_w_conv_out'], 'v_pool_w': out['v_pool_w'], 'v_pool_scale': out['v_pool_scale'], 'v_w_pool_out': out['v_w_pool_out'], 'v_w_o': out['v_w_o'], 'v_mix_post_g': out['v_mix_post_g'], 'v_mlp_pre_g': out['v_mlp_pre_g'], 'v_w_ff1': out['v_w_ff1'], 'v_w_ff2': out['v_w_ff2'], 'v_mlp_post_g': out['v_mlp_post_g']}


def _loss(weights, diff, rest, loss_target):
    with _jax.named_scope("forward"):
        args = {**rest, TWIN_DIFF_INPUT: diff, **{k: w.astype(_WEIGHT_DTYPES[k]) for k, w in weights.items()}}
        y = _forward(args)
    with _jax.named_scope("loss_head"):
        err = _jnp.square(y.astype(_jnp.float32) - loss_target)
        return 0.5 * _jnp.sum(_jnp.mean(err, axis=-1)) if err.ndim else 0.5 * err


def _adamw(w, g, m, v):
    m = ADAM_B1 * m + (1.0 - ADAM_B1) * g
    v = ADAM_B2 * v + (1.0 - ADAM_B2) * _jnp.square(g)
    m_hat = m / (1.0 - ADAM_B1 ** ADAM_STEP)
    v_hat = v / (1.0 - ADAM_B2 ** ADAM_STEP)
    delta = -ADAM_LR * (m_hat / (_jnp.sqrt(v_hat) + ADAM_EPS) + ADAM_WD * w)
    return delta, m, v


def reference(x, mix_pre_g, w_in, dw_kernel, dw_bias, conv_ln_g, conv_ln_b, w_conv_out, pool_w, pool_scale, w_pool_out, w_o, mix_post_g, mlp_pre_g, w_ff1, w_ff2, mlp_post_g, loss_target, m_mix_pre_g, m_w_in, m_dw_kernel, m_dw_bias, m_conv_ln_g, m_conv_ln_b, m_w_conv_out, m_pool_w, m_pool_scale, m_w_pool_out, m_w_o, m_mix_post_g, m_mlp_pre_g, m_w_ff1, m_w_ff2, m_mlp_post_g, v_mix_pre_g, v_w_in, v_dw_kernel, v_dw_bias, v_conv_ln_g, v_conv_ln_b, v_w_conv_out, v_pool_w, v_pool_scale, v_w_pool_out, v_w_o, v_mix_post_g, v_mlp_pre_g, v_w_ff1, v_w_ff2, v_mlp_post_g):
    given = dict(x=x, mix_pre_g=mix_pre_g, w_in=w_in, dw_kernel=dw_kernel, dw_bias=dw_bias, conv_ln_g=conv_ln_g, conv_ln_b=conv_ln_b, w_conv_out=w_conv_out, pool_w=pool_w, pool_scale=pool_scale, w_pool_out=w_pool_out, w_o=w_o, mix_post_g=mix_post_g, mlp_pre_g=mlp_pre_g, w_ff1=w_ff1, w_ff2=w_ff2, mlp_post_g=mlp_post_g, loss_target=loss_target, m_mix_pre_g=m_mix_pre_g, m_w_in=m_w_in, m_dw_kernel=m_dw_kernel, m_dw_bias=m_dw_bias, m_conv_ln_g=m_conv_ln_g, m_conv_ln_b=m_conv_ln_b, m_w_conv_out=m_w_conv_out, m_pool_w=m_pool_w, m_pool_scale=m_pool_scale, m_w_pool_out=m_w_pool_out, m_w_o=m_w_o, m_mix_post_g=m_mix_post_g, m_mlp_pre_g=m_mlp_pre_g, m_w_ff1=m_w_ff1, m_w_ff2=m_w_ff2, m_mlp_post_g=m_mlp_post_g, v_mix_pre_g=v_mix_pre_g, v_w_in=v_w_in, v_dw_kernel=v_dw_kernel, v_dw_bias=v_dw_bias, v_conv_ln_g=v_conv_ln_g, v_conv_ln_b=v_conv_ln_b, v_w_conv_out=v_w_conv_out, v_pool_w=v_pool_w, v_pool_scale=v_pool_scale, v_w_pool_out=v_w_pool_out, v_w_o=v_w_o, v_mix_post_g=v_mix_post_g, v_mlp_pre_g=v_mlp_pre_g, v_w_ff1=v_w_ff1, v_w_ff2=v_w_ff2, v_mlp_post_g=v_mlp_post_g)
    weights = {n: given[n] for n in TWIN_WEIGHTS}
    shared = {n: given[n] for n in SHARED_INPUTS}
    per_example = {n: given[n] for n in ['x']}
    grad_fn = _jax.value_and_grad(_loss, argnums=(0, 1))

    def one_microbatch(ex, loss_target):
        ex = dict(ex)
        diff = ex.pop(TWIN_DIFF_INPUT)
        return grad_fn(weights, diff, {**shared, **ex}, loss_target)

    if N_MICROBATCH == 1:
        loss, (grad_w, grad_x) = one_microbatch(per_example, given["loss_target"])
    else:
        def body(carry, xs):
            loss_sum, grad_sum = carry
            l_k, (gw_k, gx_k) = one_microbatch(xs[0], xs[1])
            with _jax.named_scope("update"):
                return (loss_sum + l_k, _jax.tree.map(_jnp.add, grad_sum, gw_k)), gx_k

        init = (_jnp.zeros((), _jnp.float32), _jax.tree.map(_jnp.zeros_like, weights))
        (loss, grad_w), grad_x = _jax.lax.scan(body, init, (per_example, given["loss_target"]))
    with _jax.named_scope("update"):
        delta_w, new_m, new_v = {}, {}, {}
        for n in TWIN_WEIGHTS:
            delta_w[n], new_m[n], new_v[n] = _adamw(weights[n], grad_w[n], given["m_" + n], given["v_" + n])
    return (loss, grad_x, *[grad_w[n] for n in TWIN_WEIGHTS], *[delta_w[n] for n in TWIN_WEIGHTS],
            *[new_m[n] for n in TWIN_WEIGHTS], *[new_v[n] for n in TWIN_WEIGHTS])
```

```python
import functools

import jax
import jax.numpy as jnp
from jax import lax
from jax.experimental import pallas as pl
from jax.experimental.pallas import tpu as pltpu

D = 1024
D_FF = 4 * D
IN_COLS = 5 * D
TAPS = 31
HALO = 32
POOL_WINDOWS = (2, 4, 8, 16)
PG = D // 4
N_DEV = 8
RMS_EPS = 1e-6
LN_EPS = 1e-5
ADAM_LR, ADAM_B1, ADAM_B2, ADAM_EPS, ADAM_WD, ADAM_STEP = 0.001, 0.9, 0.999, 1e-08, 0.01, 10

BF16 = jnp.bfloat16
F32 = jnp.float32
MIB = 1 << 20
MESH = pl.DeviceIdType.MESH


def _params(sem, vmem_mib):
    return pltpu.CompilerParams(dimension_semantics=sem, vmem_limit_bytes=vmem_mib * MIB)


def _dot(a, b):
    return jnp.dot(a, b, preferred_element_type=F32)


def _dot_nt(a, b):
    return lax.dot_general(a, b, (((1,), (1,)), ((), ())), preferred_element_type=F32)


def _dot_tn(a, b):
    return lax.dot_general(a, b, (((0,), (0,)), ((), ())), preferred_element_type=F32)


def _rms_scale(x):
    return lax.rsqrt(jnp.mean(x * x, axis=-1, keepdims=True) + RMS_EPS)


def _rms_bwd(dy, x, g):
    xn = x * _rms_scale(x)
    dn = dy * g
    dx = _rms_scale(x) * (dn - xn * jnp.mean(dn * xn, axis=-1, keepdims=True))
    return dx, jnp.sum(dy * xn, axis=0, keepdims=True)


def _sigmoid(x):
    return jax.nn.sigmoid(x)


def _acc_out(ref, first, value):
    @pl.when(first)
    def _():
        ref[...] = value

    @pl.when(jnp.logical_not(first))
    def _():
        ref[...] += value


def _my_slot():
    return 4 * lax.axis_index("x") + 2 * lax.axis_index("y") + lax.axis_index("c")


def _peer(mask):
    x, y, c = lax.axis_index("x"), lax.axis_index("y"), lax.axis_index("c")
    return (x ^ ((mask >> 2) & 1), y ^ ((mask >> 1) & 1), c ^ (mask & 1))


def _cols(width):
    return lambda ref, slot: ref.at[:, pl.ds(pl.multiple_of(slot * width, 128), width)]


def _rows(height):
    return lambda ref, slot: ref.at[pl.ds(pl.multiple_of(slot * height, 8), height), :]


def _lead(ref, slot):
    return ref.at[slot]


def _pool_rows(ref, slot):
    return ref.at[:, pl.ds(pl.multiple_of(slot * (PG // N_DEV), 8), PG // N_DEV), :]


SHARDED = (
    ("w_in", (D, IN_COLS), (D, IN_COLS // N_DEV), _cols(IN_COLS // N_DEV)),
    ("w_ff1", (D, D_FF), (D, D_FF // N_DEV), _cols(D_FF // N_DEV)),
    ("w_ff2", (D_FF, D), (D_FF // N_DEV, D), _rows(D_FF // N_DEV)),
    ("w_conv_out", (D, D), (D // N_DEV, D), _rows(D // N_DEV)),
    ("w_pool_out", (D, D), (D // N_DEV, D), _rows(D // N_DEV)),
    ("w_o", (D, D), (D // N_DEV, D), _rows(D // N_DEV)),
    ("pool_w", (4, PG, PG), (4, PG // N_DEV, PG), _pool_rows),
    ("dw_kernel", (N_DEV, TAPS, D // N_DEV), (TAPS, D // N_DEV), _lead),
)
N_SHARDED = len(SHARDED)


def _gather_weights(shards):
    dtypes = [F32 if name == "dw_kernel" else BF16 for name, *_ in SHARDED]

    def body(*refs):
        ins = refs[:N_SHARDED]
        outs = refs[N_SHARDED:2 * N_SHARDED]
        stage = refs[2 * N_SHARDED:3 * N_SHARDED]
        send_sems, recv_sems, local_sems = refs[3 * N_SHARDED:]
        me = _my_slot()
        for a in range(N_SHARDED):
            stage[a][...] = ins[a][...].astype(dtypes[a])
        local = []
        for a, (_, _, _, at) in enumerate(SHARDED):
            cp = pltpu.make_async_copy(stage[a], at(outs[a], me), local_sems.at[a])
            cp.start()
            local.append(cp)
        sends = []
        for mask in range(1, N_DEV):
            for a, (_, _, _, at) in enumerate(SHARDED):
                cp = pltpu.make_async_remote_copy(
                    src_ref=stage[a], dst_ref=at(outs[a], me), send_sem=send_sems.at[a, mask - 1],
                    recv_sem=recv_sems.at[a, mask - 1], device_id=_peer(mask), device_id_type=MESH)
                cp.start()
                sends.append(cp)
        for mask in range(1, N_DEV):
            for a, (_, _, _, at) in enumerate(SHARDED):
                pltpu.make_async_remote_copy(
                    src_ref=stage[a], dst_ref=at(outs[a], me ^ mask), send_sem=send_sems.at[a, mask - 1],
                    recv_sem=recv_sems.at[a, mask - 1], device_id=_peer(mask), device_id_type=MESH).wait_recv()
        for cp in sends:
            cp.wait_send()
        for cp in local:
            cp.wait()

    return pl.pallas_call(
        body, name="gather_weights",
        out_shape=[jax.ShapeDtypeStruct(full, dt) for (_, full, _, _), dt in zip(SHARDED, dtypes)],
        in_specs=[pl.BlockSpec(memory_space=pltpu.VMEM)] * N_SHARDED,
        out_specs=[pl.BlockSpec(memory_space=pl.ANY)] * N_SHARDED,
        scratch_shapes=[pltpu.VMEM(shard, dt) for (_, _, shard, _), dt in zip(SHARDED, dtypes)]
        + [pltpu.SemaphoreType.DMA((N_SHARDED, N_DEV - 1)), pltpu.SemaphoreType.DMA((N_SHARDED, N_DEV - 1)),
           pltpu.SemaphoreType.DMA((N_SHARDED,))],
        compiler_params=pltpu.CompilerParams(vmem_limit_bytes=40 * MIB),
    )(*shards)


def _scatter_grads(grads, vec_grads):
    dtypes = [g.dtype for g in grads]

    def body(*refs):
        gs = refs[:N_SHARDED]
        vec = refs[N_SHARDED]
        outs = refs[N_SHARDED + 1:2 * N_SHARDED + 1]
        vec_out = refs[2 * N_SHARDED + 1]
        send_sems, recv_sems, local_sems = refs[2 * N_SHARDED + 2:]
        me = _my_slot()
        local = []
        for a, (_, _, _, at) in enumerate(SHARDED):
            cp = pltpu.make_async_copy(at(gs[a], me), outs[a].at[me], local_sems.at[a])
            cp.start()
            local.append(cp)
        cp = pltpu.make_async_copy(vec, vec_out.at[me], local_sems.at[N_SHARDED])
        cp.start()
        local.append(cp)
        sends = []
        for mask in range(1, N_DEV):
            for a, (_, _, _, at) in enumerate(SHARDED):
                cp = pltpu.make_async_remote_copy(
                    src_ref=at(gs[a], me ^ mask), dst_ref=outs[a].at[me], send_sem=send_sems.at[a, mask - 1],
                    recv_sem=recv_sems.at[a, mask - 1], device_id=_peer(mask), device_id_type=MESH)
                cp.start()
                sends.append(cp)
            cp = pltpu.make_async_remote_copy(
                src_ref=vec, dst_ref=vec_out.at[me], send_sem=send_sems.at[N_SHARDED, mask - 1],
                recv_sem=recv_sems.at[N_SHARDED, mask - 1], device_id=_peer(mask), device_id_type=MESH)
            cp.start()
            sends.append(cp)
        for mask in range(1, N_DEV):
            for a, (_, _, _, at) in enumerate(SHARDED):
                pltpu.make_async_remote_copy(
                    src_ref=at(gs[a], me), dst_ref=outs[a].at[me ^ mask], send_sem=send_sems.at[a, mask - 1],
                    recv_sem=recv_sems.at[a, mask - 1], device_id=_peer(mask), device_id_type=MESH).wait_recv()
            pltpu.make_async_remote_copy(
                src_ref=vec, dst_ref=vec_out.at[me ^ mask], send_sem=send_sems.at[N_SHARDED, mask - 1],
                recv_sem=recv_sems.at[N_SHARDED, mask - 1], device_id=_peer(mask), device_id_type=MESH).wait_recv()
        for cp in sends:
            cp.wait_send()
        for cp in local:
            cp.wait()

    n = N_SHARDED + 1
    return pl.pallas_call(
        body, name="scatter_grads",
        out_shape=[jax.ShapeDtypeStruct((N_DEV,) + shard, dt) for (_, _, shard, _), dt in zip(SHARDED, dtypes)]
        + [jax.ShapeDtypeStruct((N_DEV,) + vec_grads.shape, F32)],
        in_specs=[pl.BlockSpec(memory_space=pl.ANY)] * n,
        out_specs=[pl.BlockSpec(memory_space=pl.ANY)] * n,
        scratch_shapes=[pltpu.SemaphoreType.DMA((n, N_DEV - 1)), pltpu.SemaphoreType.DMA((n, N_DEV - 1)),
                        pltpu.SemaphoreType.DMA((n,))],
    )(*grads, vec_grads)


def _adamw(name, parts, w, m, v):
    rows, cols = w.shape
    tr = rows
    for cand in (256, 128, 64, 32, 31):
        if rows % cand == 0 and rows >= cand:
            tr = cand
            break

    def body(p_ref, w_ref, m_ref, v_ref, g_out, d_out, m_out, v_out):
        g = p_ref[0].astype(F32)
        for s in range(1, N_DEV):
            g = g + p_ref[s].astype(F32)
        m_new = ADAM_B1 * m_ref[...] + (1.0 - ADAM_B1) * g
        v_new = ADAM_B2 * v_ref[...] + (1.0 - ADAM_B2) * (g * g)
        m_hat = m_new / (1.0 - ADAM_B1 ** ADAM_STEP)
        v_hat = v_new / (1.0 - ADAM_B2 ** ADAM_STEP)
        g_out[...] = g
        d_out[...] = -ADAM_LR * (m_hat / (jnp.sqrt(v_hat) + ADAM_EPS) + ADAM_WD * w_ref[...])
        m_out[...] = m_new
        v_out[...] = v_new

    blk = pl.BlockSpec((tr, cols), lambda i: (i, 0))
    return pl.pallas_call(
        body, name="adamw_" + name, grid=(rows // tr,),
        in_specs=[pl.BlockSpec((N_DEV, tr, cols), lambda i: (0, i, 0)), blk, blk, blk],
        out_specs=[blk] * 4,
        out_shape=[jax.ShapeDtypeStruct((rows, cols), F32)] * 4,
        compiler_params=_params(("arbitrary",), 32),
    )(parts, w, m, v)


def _in_proj_fwd(x, g_pre, w_in):
    seq = x.shape[0]
    tm, tn = 512, IN_COLS // N_DEV

    def body(x_ref, g_ref, w_ref, proj_ref, u_ref):
        @pl.when(pl.program_id(1) == 0)
        def _():
            xf = x_ref[...]
            u_ref[...] = (xf * _rms_scale(xf) * g_ref[...]).astype(BF16)
        proj_ref[...] = _dot(u_ref[...], w_ref[...])

    return pl.pallas_call(
        body, name="in_proj_fwd", grid=(seq // tm, N_DEV),
        in_specs=[pl.BlockSpec((tm, D), lambda i, j: (i, 0)), pl.BlockSpec((1, D), lambda i, j: (0, 0)),
                  pl.BlockSpec((D, tn), lambda i, j: (0, j))],
        out_specs=[pl.BlockSpec((tm, tn), lambda i, j: (i, j)), pl.BlockSpec((tm, D), lambda i, j: (i, 0))],
        out_shape=[jax.ShapeDtypeStruct((seq, IN_COLS), F32), jax.ShapeDtypeStruct((seq, D), BF16)],
        compiler_params=_params(("arbitrary", "arbitrary"), 32),
    )(x, g_pre, w_in)


CONV_TM = 256
CONV_RS = 64


def _layer_norm_parts(cv):
    mu = jnp.mean(cv, axis=-1, keepdims=True)
    cen = cv - mu
    rstd = lax.rsqrt(jnp.mean(cen * cen, axis=-1, keepdims=True) + LN_EPS)
    return cen * rstd, rstd


def _conv_fwd(proj, dw, dw_bias, ln_g, ln_b, w_conv_out):
    seq = proj.shape[0]
    tm = CONV_TM

    def body(a_ref, gate_ref, dw_ref, bias_ref, lg_ref, lb_ref, w_ref, cv_ref, y_ref, ug):
        i = pl.program_id(0)

        @pl.when(i == 0)
        def _():
            ug[0:HALO, :] = jnp.zeros((HALO, D), F32)

        @pl.when(i > 0)
        def _():
            ug[0:HALO, :] = ug[tm:tm + HALO, :]
        ug[HALO:HALO + tm, :] = a_ref[...] * _sigmoid(gate_ref[...])

        def channel_block(cb, carry):
            lanes = pl.ds(pl.multiple_of(cb * 128, 128), 128)
            for r0 in range(0, tm, CONV_RS):
                acc = jnp.broadcast_to(bias_ref[:, lanes], (CONV_RS, 128))
                for k in range(TAPS):
                    acc = acc + dw_ref[cb, k:k + 1, :] * ug[pl.ds(r0 + HALO - (TAPS - 1) + k, CONV_RS), lanes]
                cv_ref[pl.ds(r0, CONV_RS), lanes] = acc
            return carry
        lax.fori_loop(0, D // 128, channel_block, 0)

        n, _ = _layer_norm_parts(cv_ref[...])
        ln = n * lg_ref[...] + lb_ref[...]
        y_ref[...] = _dot((ln * _sigmoid(ln)).astype(BF16), w_ref[...])

    vec = pl.BlockSpec((1, D), lambda i: (0, 0))
    tile = pl.BlockSpec((tm, D), lambda i: (i, 0))
    return pl.pallas_call(
        body, name="conv_fwd", grid=(seq // tm,),
        in_specs=[pl.BlockSpec((tm, D), lambda i: (i, 0)), pl.BlockSpec((tm, D), lambda i: (i, 1)),
                  pl.BlockSpec((N_DEV, TAPS, 128), lambda i: (0, 0, 0)), vec, vec, vec,
                  pl.BlockSpec((D, D), lambda i: (0, 0))],
        out_specs=[tile, tile],
        out_shape=[jax.ShapeDtypeStruct((seq, D), F32)] * 2,
        scratch_shapes=[pltpu.VMEM((HALO + tm, D), F32)],
        compiler_params=_params(("arbitrary",), 32),
    )(proj, proj, dw, dw_bias, ln_g, ln_b, w_conv_out)


def _pool_counts(tile_index, tm, window):
    t = tile_index * tm + lax.broadcasted_iota(jnp.int32, (tm, 1), 0)
    return 1.0 / jnp.minimum(t + 1, window).astype(F32)


def _pool_merge_fwd(proj, y_conv, x, pool_w, pool_scale, w_pool_out, w_o, g_post):
    seq = proj.shape[0]
    tm = CONV_TM

    def body(p_ref, gc_ref, gp_ref, yc_ref, x_ref, pw_ref, ps_ref, wpo_ref, wo_ref, g_ref,
             z_ref, zl_ref, yp_ref, mg_ref, o_ref, h1_ref, pbuf):
        i = pl.program_id(0)

        @pl.when(i == 0)
        def _():
            pbuf[0:HALO, :] = jnp.zeros((HALO, D), F32)

        @pl.when(i > 0)
        def _():
            pbuf[0:HALO, :] = pbuf[tm:tm + HALO, :]
        pbuf[HALO:HALO + tm, :] = p_ref[...]

        for g, window in enumerate(POOL_WINDOWS):
            lanes = pl.ds(g * PG, PG)
            acc = pbuf[pl.ds(HALO, tm), lanes]
            for j in range(1, window):
                acc = acc + pbuf[pl.ds(HALO - j, tm), lanes]
            zg = acc * _pool_counts(i, tm, window) - pbuf[pl.ds(HALO, tm), lanes]
            z_ref[:, lanes] = zg.astype(BF16)
            zl_ref[:, lanes] = _dot(zg.astype(BF16), pw_ref[g])
        zl = zl_ref[...]
        y_pool = _dot((zl * ps_ref[...]).astype(BF16), wpo_ref[...])
        yp_ref[...] = y_pool
        merged = (_sigmoid(gc_ref[...]) * yc_ref[...] + _sigmoid(gp_ref[...]) * y_pool).astype(BF16)
        mg_ref[...] = merged
        o = _dot(merged, wo_ref[...])
        o_ref[...] = o
        h1_ref[...] = x_ref[...] + o * _rms_scale(o) * g_ref[...]

    vec = pl.BlockSpec((1, D), lambda i: (0, 0))
    tile = pl.BlockSpec((tm, D), lambda i: (i, 0))
    mat = pl.BlockSpec((D, D), lambda i: (0, 0))
    return pl.pallas_call(
        body, name="pool_merge_fwd", grid=(seq // tm,),
        in_specs=[pl.BlockSpec((tm, D), lambda i: (i, 2)), pl.BlockSpec((tm, D), lambda i: (i, 3)),
                  pl.BlockSpec((tm, D), lambda i: (i, 4)), tile, tile,
                  pl.BlockSpec((4, PG, PG), lambda i: (0, 0, 0)), vec, mat, mat, vec],
        out_specs=[tile] * 6,
        out_shape=[jax.ShapeDtypeStruct((seq, D), dt) for dt in (BF16, F32, F32, BF16, F32, F32)],
        scratch_shapes=[pltpu.VMEM((HALO + tm, D), F32)],
        compiler_params=_params(("arbitrary",), 48),
    )(proj, proj, proj, y_conv, x, pool_w, pool_scale, w_pool_out, w_o, g_post)


def _mlp_fwd(h1, g_pre, w_ff1, w_ff2, g_post, target):
    seq = h1.shape[0]
    tm, tf = 512, D_FF // N_DEV
    n_f = D_FF // tf

    def body(h1_ref, gpre_ref, w1_ref, w2_ref, gpost_ref, tgt_ref, v_ref, dm_ref, dh2_ref, sse_ref, ggrad_ref, macc):
        i, j = pl.program_id(0), pl.program_id(1)

        @pl.when(j == 0)
        def _():
            h = h1_ref[...]
            v_ref[...] = (h * _rms_scale(h) * gpre_ref[...]).astype(BF16)
        f = jnp.maximum(_dot(v_ref[...], w1_ref[...]), 0.0)
        part = _dot((f * f).astype(BF16), w2_ref[...])

        @pl.when(j == 0)
        def _():
            macc[...] = part

        @pl.when(j > 0)
        def _():
            macc[...] += part

        @pl.when(j == n_f - 1)
        def _():
            mo = macc[...]
            err = h1_ref[...] + mo * _rms_scale(mo) * gpost_ref[...] - tgt_ref[...]
            dh2 = err * (1.0 / D)
            dh2_ref[...] = dh2
            dm, ggrad = _rms_bwd(dh2, mo, gpost_ref[...])
            dm_ref[...] = dm.astype(BF16)
            _acc_out(ggrad_ref, i == 0, ggrad)
            _acc_out(sse_ref, i == 0, jnp.sum(jnp.sum(err * err, axis=1, keepdims=True), axis=0, keepdims=True))

    vec = pl.BlockSpec((1, D), lambda i, j: (0, 0))
    tile = pl.BlockSpec((tm, D), lambda i, j: (i, 0))
    return pl.pallas_call(
        body, name="mlp_fwd", grid=(seq // tm, n_f),
        in_specs=[tile, vec, pl.BlockSpec((D, tf), lambda i, j: (0, j)), pl.BlockSpec((tf, D), lambda i, j: (j, 0)), vec, tile],
        out_specs=[tile, tile, tile, pl.BlockSpec((1, 1), lambda i, j: (0, 0)), vec],
        out_shape=[jax.ShapeDtypeStruct((seq, D), BF16), jax.ShapeDtypeStruct((seq, D), BF16),
                   jax.ShapeDtypeStruct((seq, D), F32), jax.ShapeDtypeStruct((1, 1), F32),
                   jax.ShapeDtypeStruct((1, D), F32)],
        scratch_shapes=[pltpu.VMEM((tm, D), F32)],
        compiler_params=_params(("arbitrary", "arbitrary"), 40),
    )(h1, g_pre, w_ff1, w_ff2, g_post, target)


def _mlp_bwd(v, dm, w_ff1, w_ff2):
    seq = v.shape[0]
    tm, tf = 512, D_FF // N_DEV
    n_t = seq // tm

    def body(v_ref, dm_ref, w1_ref, w2_ref, dv_hbm, g1_ref, g2_ref, dv_acc, g1_acc, g2_acc, sem):
        j, i = pl.program_id(0), pl.program_id(1)
        vt, dmt = v_ref[...], dm_ref[...]
        f = jnp.maximum(_dot(vt, w1_ref[...]), 0.0)
        df = (_dot_nt(dmt, w2_ref[...]) * (2.0 * f)).astype(BF16)
        rows = pl.ds(pl.multiple_of(i * tm, tm), tm)
        dv_part = _dot_nt(df, w1_ref[...])

        @pl.when(j == 0)
        def _():
            dv_acc[rows, :] = dv_part

        @pl.when(j > 0)
        def _():
            dv_acc[rows, :] += dv_part
        g1_part = _dot_tn(vt, df)
        g2_part = _dot_tn((f * f).astype(BF16), dmt)

        @pl.when(i == 0)
        def _():
            g1_acc[...] = g1_part
            g2_acc[...] = g2_part

        @pl.when(i > 0)
        def _():
            g1_acc[...] += g1_part
            g2_acc[...] += g2_part

        @pl.when(i == n_t - 1)
        def _():
            g1_ref[...] = g1_acc[...].astype(BF16)
            g2_ref[...] = g2_acc[...].astype(BF16)

        @pl.when(jnp.logical_and(i == n_t - 1, j == D_FF // tf - 1))
        def _():
            cp = pltpu.make_async_copy(dv_acc, dv_hbm, sem)
            cp.start()
            cp.wait()

    tile = pl.BlockSpec((tm, D), lambda j, i: (i, 0))
    return pl.pallas_call(
        body, name="mlp_bwd", grid=(D_FF // tf, n_t),
        in_specs=[tile, tile, pl.BlockSpec((D, tf), lambda j, i: (0, j)), pl.BlockSpec((tf, D), lambda j, i: (j, 0))],
        out_specs=[pl.BlockSpec(memory_space=pl.ANY), pl.BlockSpec((D, tf), lambda j, i: (0, j)),
                   pl.BlockSpec((tf, D), lambda j, i: (j, 0))],
        out_shape=[jax.ShapeDtypeStruct((seq, D), F32), jax.ShapeDtypeStruct((D, D_FF), BF16),
                   jax.ShapeDtypeStruct((D_FF, D), BF16)],
        scratch_shapes=[pltpu.VMEM((seq, D), F32), pltpu.VMEM((D, tf), F32), pltpu.VMEM((tf, D), F32),
                        pltpu.SemaphoreType.DMA],
        compiler_params=_params(("arbitrary", "arbitrary"), 52),
    )(v, dm, w_ff1, w_ff2)


def _merge_bwd(dh2, dv, h1, g_mlp_pre, o, g_mix_post, w_o, merged, proj, y_conv, y_pool):
    seq = dh2.shape[0]
    tm = 256
    n_t = seq // tm

    def body(dh2_ref, dv_ref, h1_ref, gpre_ref, o_ref, gpost_ref, wo_ref, mg_ref, gc_ref, gp_ref, yc_ref, yp_ref,
             dh1_ref, dyc_ref, dyp_ref, dg_ref, gwo_ref, ggpre_ref, ggpost_ref, gwo_acc):
        i = pl.program_id(0)
        dnorm, ggpre = _rms_bwd(dv_ref[...], h1_ref[...], gpre_ref[...])
        dh1 = dh2_ref[...] + dnorm
        dh1_ref[...] = dh1
        do, ggpost = _rms_bwd(dh1, o_ref[...], gpost_ref[...])
        do = do.astype(BF16)
        _acc_out(ggpre_ref, i == 0, ggpre)
        _acc_out(ggpost_ref, i == 0, ggpost)
        _acc_out(gwo_acc, i == 0, _dot_tn(mg_ref[...], do))
        dmerged = _dot_nt(do, wo_ref[...])
        sc, sp = _sigmoid(gc_ref[...]), _sigmoid(gp_ref[...])
        dyc_ref[...] = (dmerged * sc).astype(BF16)
        dyp_ref[...] = (dmerged * sp).astype(BF16)
        dg_ref[:, 0:D] = (dmerged * yc_ref[...] * (sc * (1.0 - sc))).astype(BF16)
        dg_ref[:, D:2 * D] = (dmerged * yp_ref[...] * (sp * (1.0 - sp))).astype(BF16)

        @pl.when(i == n_t - 1)
        def _():
            gwo_ref[...] = gwo_acc[...].astype(BF16)

    vec = pl.BlockSpec((1, D), lambda i: (0, 0))
    tile = pl.BlockSpec((tm, D), lambda i: (i, 0))
    mat = pl.BlockSpec((D, D), lambda i: (0, 0))
    return pl.pallas_call(
        body, name="merge_bwd", grid=(n_t,),
        in_specs=[tile, tile, tile, vec, tile, vec, mat, tile,
                  pl.BlockSpec((tm, D), lambda i: (i, 3)), pl.BlockSpec((tm, D), lambda i: (i, 4)), tile, tile],
        out_specs=[tile, tile, tile, pl.BlockSpec((tm, 2 * D), lambda i: (i, 0)), mat, vec, vec],
        out_shape=[jax.ShapeDtypeStruct((seq, D), F32), jax.ShapeDtypeStruct((seq, D), BF16),
                   jax.ShapeDtypeStruct((seq, D), BF16), jax.ShapeDtypeStruct((seq, 2 * D), BF16),
                   jax.ShapeDtypeStruct((D, D), BF16), jax.ShapeDtypeStruct((1, D), F32),
                   jax.ShapeDtypeStruct((1, D), F32)],
        scratch_shapes=[pltpu.VMEM((D, D), F32)],
        compiler_params=_params(("arbitrary",), 48),
    )(dh2, dv, h1, g_mlp_pre, o, g_mix_post, w_o, merged, proj, proj, y_conv, y_pool)


def _pool_bwd(dy_pool, zl, z, pool_w, pool_scale, w_pool_out):
    seq = dy_pool.shape[0]
    tm = CONV_TM
    n_t = seq // tm

    def body(dy_ref, zl_ref, z_ref, pw_ref, ps_ref, wpo_ref, dp_ref, gwpo_ref, gpw_ref, gps_ref, qbuf, gwpo_acc, gpw_acc):
        i = pl.program_id(0)
        tile_index = n_t - 1 - i
        first = i == 0
        dy = dy_ref[...]
        zl = zl_ref[...]
        dzs = _dot_nt(dy, wpo_ref[...])
        _acc_out(gwpo_acc, first, _dot_tn((zl * ps_ref[...]).astype(BF16), dy))
        _acc_out(gps_ref, first, jnp.sum(dzs * zl, axis=0, keepdims=True))
        dzl = (dzs * ps_ref[...]).astype(BF16)

        @pl.when(first)
        def _():
            qbuf[tm:tm + HALO, :] = jnp.zeros((HALO, D), F32)

        @pl.when(jnp.logical_not(first))
        def _():
            qbuf[tm:tm + HALO, :] = qbuf[0:HALO, :]

        dzs_list = []
        for g, window in enumerate(POOL_WINDOWS):
            lanes = pl.ds(g * PG, PG)
            dzl_g = dzl[:, g * PG:(g + 1) * PG]
            dz = _dot_nt(dzl_g, pw_ref[g])
            _acc_out(gpw_acc.at[g], first, _dot_tn(z_ref[:, lanes], dzl_g))
            qbuf[pl.ds(0, tm), lanes] = dz * _pool_counts(tile_index, tm, window)
            dzs_list.append(dz)
        for g, window in enumerate(POOL_WINDOWS):
            lanes = pl.ds(g * PG, PG)
            acc = qbuf[pl.ds(0, tm), lanes]
            for j in range(1, window):
                acc = acc + qbuf[pl.ds(j, tm), lanes]
            dp_ref[:, lanes] = (acc - dzs_list[g]).astype(BF16)

        @pl.when(i == n_t - 1)
        def _():
            gwpo_ref[...] = gwpo_acc[...].astype(BF16)
            gpw_ref[...] = gpw_acc[...].astype(BF16)

    vec = pl.BlockSpec((1, D), lambda i: (0, 0))
    tile = pl.BlockSpec((tm, D), lambda i: (n_t - 1 - i, 0))
    mat = pl.BlockSpec((D, D), lambda i: (0, 0))
    pw = pl.BlockSpec((4, PG, PG), lambda i: (0, 0, 0))
    return pl.pallas_call(
        body, name="pool_bwd", grid=(n_t,),
        in_specs=[tile, tile, tile, pw, vec, mat],
        out_specs=[tile, mat, pw, vec],
        out_shape=[jax.ShapeDtypeStruct((seq, D), BF16), jax.ShapeDtypeStruct((D, D), BF16),
                   jax.ShapeDtypeStruct((4, PG, PG), BF16), jax.ShapeDtypeStruct((1, D), F32)],
        scratch_shapes=[pltpu.VMEM((tm + HALO, D), F32), pltpu.VMEM((D, D), F32), pltpu.VMEM((4, PG, PG), F32)],
        compiler_params=_params(("arbitrary",), 40),
    )(dy_pool, zl, z, pool_w, pool_scale, w_pool_out)


def _conv_bwd(dy_conv, cv, proj, dw, ln_g, ln_b, w_conv_out):
    seq = dy_conv.shape[0]
    tm = CONV_TM
    n_t = seq // tm
    halo_blocks = tm // HALO

    def body(dy_ref, cv_ref, a_ref, gate_ref, ah_ref, gh_ref, dw_ref, lg_ref, lb_ref, w_ref,
             dglu_ref, gw_ref, gdw_ref, gbias_ref, glg_ref, glb_ref, ug, dcv, dug, gw_acc):
        i = pl.program_id(0)
        tile_index = n_t - 1 - i
        first = i == 0
        dy = dy_ref[...]
        n, rstd = _layer_norm_parts(cv_ref[...])
        ln = n * lg_ref[...] + lb_ref[...]
        sg = _sigmoid(ln)
        _acc_out(gw_acc, first, _dot_tn((ln * sg).astype(BF16), dy))
        dln = _dot_nt(dy, w_ref[...]) * (sg * (1.0 + ln * (1.0 - sg)))
        _acc_out(glg_ref, first, jnp.sum(dln * n, axis=0, keepdims=True))
        _acc_out(glb_ref, first, jnp.sum(dln, axis=0, keepdims=True))
        dn = dln * lg_ref[...]
        dcv_tile = rstd * (dn - jnp.mean(dn, axis=-1, keepdims=True) - n * jnp.mean(dn * n, axis=-1, keepdims=True))
        _acc_out(gbias_ref, first, jnp.sum(dcv_tile, axis=0, keepdims=True))

        @pl.when(first)
        def _():
            dcv[tm:tm + HALO, :] = jnp.zeros((HALO, D), F32)

        @pl.when(jnp.logical_not(first))
        def _():
            dcv[tm:tm + HALO, :] = dcv[0:HALO, :]
        dcv[0:tm, :] = dcv_tile

        a, gate = a_ref[...], gate_ref[...]
        sgate = _sigmoid(gate)
        ug[HALO:HALO + tm, :] = a * sgate
        before = jnp.where(tile_index > 0, 1.0, 0.0)
        ug[0:HALO, :] = ah_ref[...] * _sigmoid(gh_ref[...]) * before

        @pl.when(first)
        def _():
            gdw_ref[...] = jnp.zeros((N_DEV, TAPS + 1, 128), F32)

        def channel_block(cb, carry):
            lanes = pl.ds(pl.multiple_of(cb * 128, 128), 128)
            for r0 in range(0, tm, CONV_RS):
                acc = jnp.zeros((CONV_RS, 128), F32)
                for k in range(TAPS):
                    acc = acc + dw_ref[cb, k:k + 1, :] * dcv[pl.ds(r0 + (TAPS - 1) - k, CONV_RS), lanes]
                dug[pl.ds(r0, CONV_RS), lanes] = acc
            for k in range(TAPS):
                part = jnp.zeros((8, 128), F32)
                for r0 in range(0, tm, CONV_RS):
                    prod = dcv[pl.ds(r0, CONV_RS), lanes] * ug[pl.ds(r0 + HALO - (TAPS - 1) + k, CONV_RS), lanes]
                    part = part + jnp.sum(prod.reshape(CONV_RS // 8, 8, 128), axis=0)
                gdw_ref[cb, k:k + 1, :] += jnp.sum(part, axis=0, keepdims=True)
            return carry
        lax.fori_loop(0, D // 128, channel_block, 0)

        d_ug = dug[...]
        dglu_ref[:, 0:D] = (d_ug * sgate).astype(BF16)
        dglu_ref[:, D:2 * D] = (d_ug * a * (sgate * (1.0 - sgate))).astype(BF16)

        @pl.when(i == n_t - 1)
        def _():
            gw_ref[...] = gw_acc[...].astype(BF16)

    def halo_index(col):
        return lambda i: (jnp.maximum((n_t - 1 - i) * halo_blocks - 1, 0), col)

    vec = pl.BlockSpec((1, D), lambda i: (0, 0))
    tile = pl.BlockSpec((tm, D), lambda i: (n_t - 1 - i, 0))
    mat = pl.BlockSpec((D, D), lambda i: (0, 0))
    dwspec = pl.BlockSpec((N_DEV, TAPS, 128), lambda i: (0, 0, 0))
    return pl.pallas_call(
        body, name="conv_bwd", grid=(n_t,),
        in_specs=[tile, tile, pl.BlockSpec((tm, D), lambda i: (n_t - 1 - i, 0)), pl.BlockSpec((tm, D), lambda i: (n_t - 1 - i, 1)),
                  pl.BlockSpec((HALO, D), halo_index(0)), pl.BlockSpec((HALO, D), halo_index(1)), dwspec, vec, vec, mat],
        out_specs=[pl.BlockSpec((tm, 2 * D), lambda i: (n_t - 1 - i, 0)), mat,
                   pl.BlockSpec((N_DEV, TAPS + 1, 128), lambda i: (0, 0, 0)), vec, vec, vec],
        out_shape=[jax.ShapeDtypeStruct((seq, 2 * D), BF16), jax.ShapeDtypeStruct((D, D), BF16),
                   jax.ShapeDtypeStruct((N_DEV, TAPS + 1, 128), F32), jax.ShapeDtypeStruct((1, D), F32),
                   jax.ShapeDtypeStruct((1, D), F32), jax.ShapeDtypeStruct((1, D), F32)],
        scratch_shapes=[pltpu.VMEM((HALO + tm, D), F32), pltpu.VMEM((tm + HALO, D), F32), pltpu.VMEM((tm, D), F32),
                        pltpu.VMEM((D, D), F32)],
        compiler_params=_params(("arbitrary",), 48),
    )(dy_conv, cv, proj, proj, proj, proj, dw, ln_g, ln_b, w_conv_out)


def _in_proj_bwd_x(d_glu, dp, dgates, w_in, x, g_pre, dh1):
    seq = x.shape[0]
    tm = 256

    def body(dglu_ref, dp_ref, dg_ref, w_ref, x_ref, g_ref, dh1_ref, dx_ref, gg_ref):
        du = _dot_nt(dglu_ref[...], w_ref[:, 0:2 * D])
        du += _dot_nt(dp_ref[...], w_ref[:, 2 * D:3 * D])
        du += _dot_nt(dg_ref[...], w_ref[:, 3 * D:5 * D])
        dnorm, gg = _rms_bwd(du, x_ref[...], g_ref[...])
        dx_ref[...] = dh1_ref[...] + dnorm
        _acc_out(gg_ref, pl.program_id(0) == 0, gg)

    vec = pl.BlockSpec((1, D), lambda i: (0, 0))
    tile = pl.BlockSpec((tm, D), lambda i: (i, 0))
    wide = pl.BlockSpec((tm, 2 * D), lambda i: (i, 0))
    return pl.pallas_call(
        body, name="in_proj_bwd_x", grid=(seq // tm,),
        in_specs=[wide, tile, wide, pl.BlockSpec((D, IN_COLS), lambda i: (0, 0)), tile, vec, tile],
        out_specs=[tile, vec],
        out_shape=[jax.ShapeDtypeStruct((seq, D), F32), jax.ShapeDtypeStruct((1, D), F32)],
        compiler_params=_params(("arbitrary",), 48),
    )(d_glu, dp, dgates, w_in, x, g_pre, dh1)


def _in_proj_bwd_w(u, d_glu, dp, dgates):
    seq = u.shape[0]
    tm = 512
    n_t = seq // tm

    def body(u_ref, dglu_ref, dp_ref, dg_ref, out_ref, acc):
        b, i = pl.program_id(0), pl.program_id(1)
        ut = u_ref[...]

        def add(d_ref):
            _acc_out(acc, i == 0, _dot_tn(ut, d_ref[...]))

        pl.when(b < 2)(lambda: add(dglu_ref))
        pl.when(b == 2)(lambda: add(dp_ref))
        pl.when(b > 2)(lambda: add(dg_ref))

        @pl.when(i == n_t - 1)
        def _():
            out_ref[...] = acc[...].astype(BF16)

    return pl.pallas_call(
        body, name="in_proj_bwd_w", grid=(IN_COLS // D, n_t),
        in_specs=[pl.BlockSpec((tm, D), lambda b, i: (i, 0)),
                  pl.BlockSpec((tm, D), lambda b, i: (jnp.where(b < 2, i, 0), jnp.minimum(b, 1))),
                  pl.BlockSpec((tm, D), lambda b, i: (jnp.where(b == 2, i, 0), 0)),
                  pl.BlockSpec((tm, D), lambda b, i: (jnp.where(b > 2, i, 0), jnp.maximum(b - 3, 0)))],
        out_specs=pl.BlockSpec((D, D), lambda b, i: (0, b)),
        out_shape=jax.ShapeDtypeStruct((D, IN_COLS), BF16),
        scratch_shapes=[pltpu.VMEM((D, D), F32)],
        compiler_params=_params(("arbitrary", "arbitrary"), 40),
    )(u, d_glu, dp, dgates)


VEC_NAMES = ("mix_pre_g", "dw_bias", "conv_ln_g", "conv_ln_b", "pool_scale", "mix_post_g", "mlp_pre_g", "mlp_post_g")
WEIGHT_ORDER = ("mix_pre_g", "w_in", "dw_kernel", "dw_bias", "conv_ln_g", "conv_ln_b", "w_conv_out", "pool_w",
                "pool_scale", "w_pool_out", "w_o", "mix_post_g", "mlp_pre_g", "w_ff1", "w_ff2", "mlp_post_g")


def _step(x, loss_target, w, m, v):
    row = lambda a: a.reshape(1, D)
    full = dict(zip([s[0] for s in SHARDED], _gather_weights([w[s[0]] for s in SHARDED])))

    proj, u = _in_proj_fwd(x, row(w["mix_pre_g"]), full["w_in"])
    cv, y_conv = _conv_fwd(proj, full["dw_kernel"], row(w["dw_bias"]), row(w["conv_ln_g"]), row(w["conv_ln_b"]),
                           full["w_conv_out"])
    z, zl, y_pool, merged, o, h1 = _pool_merge_fwd(proj, y_conv, x, full["pool_w"], row(w["pool_scale"]),
                                                   full["w_pool_out"], full["w_o"], row(w["mix_post_g"]))
    vv, dm, dh2, sse, g_mlp_post = _mlp_fwd(h1, row(w["mlp_pre_g"]), full["w_ff1"], full["w_ff2"],
                                            row(w["mlp_post_g"]), loss_target)

    dv, g_ff1, g_ff2 = _mlp_bwd(vv, dm, full["w_ff1"], full["w_ff2"])
    dh1, dy_conv, dy_pool, dgates, g_wo, g_mlp_pre, g_mix_post = _merge_bwd(
        dh2, dv, h1, row(w["mlp_pre_g"]), o, row(w["mix_post_g"]), full["w_o"], merged, proj, y_conv, y_pool)
    dp, g_wpo, g_pw, g_pool_scale = _pool_bwd(dy_pool, zl, z, full["pool_w"], row(w["pool_scale"]), full["w_pool_out"])
    d_glu, g_wco, g_dw, g_bias, g_ln_g, g_ln_b = _conv_bwd(dy_conv, cv, proj, full["dw_kernel"], row(w["conv_ln_g"]),
                                                            row(w["conv_ln_b"]), full["w_conv_out"])
    grad_x, g_mix_pre = _in_proj_bwd_x(d_glu, dp, dgates, full["w_in"], x, row(w["mix_pre_g"]), dh1)
    g_win = _in_proj_bwd_w(u, d_glu, dp, dgates)

    vec_grads = jnp.concatenate(
        [jnp.pad(g, ((0, 7), (0, 0))) for g in
         (g_mix_pre, g_bias, g_ln_g, g_ln_b, g_pool_scale, g_mix_post, g_mlp_pre, g_mlp_post)], axis=0)
    grads = {"w_in": g_win, "w_ff1": g_ff1, "w_ff2": g_ff2, "w_conv_out": g_wco, "w_pool_out": g_wpo, "w_o": g_wo,
             "pool_w": g_pw, "dw_kernel": g_dw[:, :TAPS, :]}
    *parts, vec_parts = _scatter_grads([grads[s[0]] for s in SHARDED], vec_grads)
    parts = dict(zip([s[0] for s in SHARDED], parts))

    out = {}
    for name, _, shard, _ in SHARDED:
        as2d = (lambda a: a.reshape(-1, a.shape[-1]))
        p = parts[name]
        res = _adamw(name, p.reshape(N_DEV, -1, p.shape[-1]), as2d(w[name]), as2d(m[name]), as2d(v[name]))
        out[name] = [r.reshape(shard) for r in res]
    stack = lambda d: jnp.stack([d[n] for n in VEC_NAMES], axis=0)
    vec_parts = vec_parts.reshape(N_DEV, 8, 8, D)[:, :, 0, :]
    res = _adamw("vectors", vec_parts, stack(w), stack(m), stack(v))
    for k, name in enumerate(VEC_NAMES):
        out[name] = [r[k] for r in res]

    loss = lax.psum(sse[0, 0] * (0.5 / D), ("x", "y", "c"))
    return loss, grad_x, out


def kernel(x, mix_pre_g, w_in, dw_kernel, dw_bias, conv_ln_g, conv_ln_b, w_conv_out, pool_w, pool_scale, w_pool_out, w_o, mix_post_g, mlp_pre_g, w_ff1, w_ff2, mlp_post_g, loss_target, m_mix_pre_g, m_w_in, m_dw_kernel, m_dw_bias, m_conv_ln_g, m_conv_ln_b, m_w_conv_out, m_pool_w, m_pool_scale, m_w_pool_out, m_w_o, m_mix_post_g, m_mlp_pre_g, m_w_ff1, m_w_ff2, m_mlp_post_g, v_mix_pre_g, v_w_in, v_dw_kernel, v_dw_bias, v_conv_ln_g, v_conv_ln_b, v_w_conv_out, v_pool_w, v_pool_scale, v_w_pool_out, v_w_o, v_mix_post_g, v_mlp_pre_g, v_w_ff1, v_w_ff2, v_mlp_post_g):
    w = dict(mix_pre_g=mix_pre_g, w_in=w_in, dw_kernel=dw_kernel, dw_bias=dw_bias, conv_ln_g=conv_ln_g, conv_ln_b=conv_ln_b,
             w_conv_out=w_conv_out, pool_w=pool_w, pool_scale=pool_scale, w_pool_out=w_pool_out, w_o=w_o,
             mix_post_g=mix_post_g, mlp_pre_g=mlp_pre_g, w_ff1=w_ff1, w_ff2=w_ff2, mlp_post_g=mlp_post_g)
    m = dict(mix_pre_g=m_mix_pre_g, w_in=m_w_in, dw_kernel=m_dw_kernel, dw_bias=m_dw_bias, conv_ln_g=m_conv_ln_g,
             conv_ln_b=m_conv_ln_b, w_conv_out=m_w_conv_out, pool_w=m_pool_w, pool_scale=m_pool_scale,
             w_pool_out=m_w_pool_out, w_o=m_w_o, mix_post_g=m_mix_post_g, mlp_pre_g=m_mlp_pre_g, w_ff1=m_w_ff1,
             w_ff2=m_w_ff2, mlp_post_g=m_mlp_post_g)
    v = dict(mix_pre_g=v_mix_pre_g, w_in=v_w_in, dw_kernel=v_dw_kernel, dw_bias=v_dw_bias, conv_ln_g=v_conv_ln_g,
             conv_ln_b=v_conv_ln_b, w_conv_out=v_w_conv_out, pool_w=v_pool_w, pool_scale=v_pool_scale,
             w_pool_out=v_w_pool_out, w_o=v_w_o, mix_post_g=v_mix_post_g, mlp_pre_g=v_mlp_pre_g, w_ff1=v_w_ff1,
             w_ff2=v_w_ff2, mlp_post_g=v_mlp_post_g)
    seq = x.shape[1]
    loss, grad_x, out = _step(x.reshape(seq, D), loss_target.reshape(seq, D), w, m, v)
    grads, deltas, new_m, new_v = ([out[n][k] for n in WEIGHT_ORDER] for k in range(4))
    return (loss, grad_x.reshape(x.shape), *grads, *deltas, *new_m, *new_v)
```

```python
import functools

import jax
import jax.numpy as jnp
from jax import lax
from jax.experimental import pallas as pl
from jax.experimental.pallas import tpu as pltpu

D = 1024
D_FF = 4 * D
IN_COLS = 5 * D
TAPS = 31
HALO = 32
POOL_WINDOWS = (2, 4, 8, 16)
PG = D // 4
N_DEV = 8
RMS_EPS = 1e-6
LN_EPS = 1e-5
ADAM_LR, ADAM_B1, ADAM_B2, ADAM_EPS, ADAM_WD, ADAM_STEP = 0.001, 0.9, 0.999, 1e-08, 0.01, 10

BF16 = jnp.bfloat16
F32 = jnp.float32
MIB = 1 << 20
MESH = pl.DeviceIdType.MESH


def _params(sem, vmem_mib):
    return pltpu.CompilerParams(dimension_semantics=sem, vmem_limit_bytes=vmem_mib * MIB)


def _dot(a, b):
    return jnp.dot(a, b, preferred_element_type=F32)


def _dot_nt(a, b):
    return lax.dot_general(a, b, (((1,), (1,)), ((), ())), preferred_element_type=F32)


def _dot_tn(a, b):
    return lax.dot_general(a, b, (((0,), (0,)), ((), ())), preferred_element_type=F32)


def _rms_scale(x):
    return lax.rsqrt(jnp.mean(x * x, axis=-1, keepdims=True) + RMS_EPS)


def _rms_bwd(dy, x, g):
    xn = x * _rms_scale(x)
    dn = dy * g
    dx = _rms_scale(x) * (dn - xn * jnp.mean(dn * xn, axis=-1, keepdims=True))
    return dx, jnp.sum(dy * xn, axis=0, keepdims=True)


def _sigmoid(x):
    return jax.nn.sigmoid(x)


def _acc_out(ref, first, value):
    @pl.when(first)
    def _():
        ref[...] = value

    @pl.when(jnp.logical_not(first))
    def _():
        ref[...] += value


def _my_slot():
    return 4 * lax.axis_index("x") + 2 * lax.axis_index("y") + lax.axis_index("c")


def _peer(mask):
    x, y, c = lax.axis_index("x"), lax.axis_index("y"), lax.axis_index("c")
    return (x ^ ((mask >> 2) & 1), y ^ ((mask >> 1) & 1), c ^ (mask & 1))


def _cols(width):
    return lambda ref, slot: ref.at[:, pl.ds(pl.multiple_of(slot * width, 128), width)]


def _rows(height):
    return lambda ref, slot: ref.at[pl.ds(pl.multiple_of(slot * height, 8), height), :]


def _lead(ref, slot):
    return ref.at[slot]


def _pool_rows(ref, slot):
    return ref.at[:, pl.ds(pl.multiple_of(slot * (PG // N_DEV), 8), PG // N_DEV), :]


SHARDED = (
    ("w_in", (D, IN_COLS), (D, IN_COLS // N_DEV), _cols(IN_COLS // N_DEV)),
    ("w_ff1", (D, D_FF), (D, D_FF // N_DEV), _cols(D_FF // N_DEV)),
    ("w_ff2", (D_FF, D), (D_FF // N_DEV, D), _rows(D_FF // N_DEV)),
    ("w_conv_out", (D, D), (D // N_DEV, D), _rows(D // N_DEV)),
    ("w_pool_out", (D, D), (D // N_DEV, D), _rows(D // N_DEV)),
    ("w_o", (D, D), (D // N_DEV, D), _rows(D // N_DEV)),
    ("pool_w", (4, PG, PG), (4, PG // N_DEV, PG), _pool_rows),
    ("dw_kernel", (N_DEV, TAPS, D // N_DEV), (TAPS, D // N_DEV), _lead),
)
N_SHARDED = len(SHARDED)


SHARD_AT = {name: at for name, _, _, at in SHARDED}
HBM = pl.BlockSpec(memory_space=pltpu.HBM)
SEM = pl.BlockSpec(memory_space=pltpu.SEMAPHORE)
EFFECT = pltpu.SideEffectType.DATAFLOW_SIDE_EFFECTING


def _in_hbm(a):
    return pltpu.with_memory_space_constraint(a, pltpu.HBM)


def _stage_shards(shards):
    dtypes = [F32 if name == "dw_kernel" else BF16 for name, *_ in SHARDED]

    def body(*refs):
        ins = refs[:N_SHARDED]
        stage = refs[N_SHARDED:2 * N_SHARDED]
        fulls = refs[2 * N_SHARDED:3 * N_SHARDED]
        sems = refs[3 * N_SHARDED]
        me = _my_slot()
        copies = []
        for a, (_, _, _, at) in enumerate(SHARDED):
            stage[a][...] = ins[a][...].astype(dtypes[a])
            cp = pltpu.make_async_copy(stage[a], at(fulls[a], me), sems.at[a])
            cp.start()
            copies.append(cp)
        for cp in copies:
            cp.wait()

    outs = pl.pallas_call(
        body, name="stage_shards",
        out_shape=[jax.ShapeDtypeStruct(shard, dt) for (_, _, shard, _), dt in zip(SHARDED, dtypes)]
        + [jax.ShapeDtypeStruct(full, dt) for (_, full, _, _), dt in zip(SHARDED, dtypes)],
        in_specs=[pl.BlockSpec(memory_space=pltpu.VMEM)] * N_SHARDED,
        out_specs=[pl.BlockSpec(memory_space=pltpu.VMEM)] * N_SHARDED + [pl.BlockSpec(memory_space=pl.ANY)] * N_SHARDED,
        scratch_shapes=[pltpu.SemaphoreType.DMA((N_SHARDED,))],
        compiler_params=pltpu.CompilerParams(vmem_limit_bytes=40 * MIB),
    )(*shards)
    return outs[:N_SHARDED], outs[N_SHARDED:]


def _own_slices(names, grads):
    n = len(names)
    shards = {name: shard for name, _, shard, _ in SHARDED}

    def body(*refs):
        gs, parts, sems = refs[:n], refs[n:2 * n], refs[2 * n]
        me = _my_slot()
        copies = []
        for a, name in enumerate(names):
            cp = pltpu.make_async_copy(SHARD_AT[name](gs[a], me), parts[a].at[me], sems.at[a])
            cp.start()
            copies.append(cp)
        for cp in copies:
            cp.wait()

    return pl.pallas_call(
        body, name="own_slices_" + names[0],
        out_shape=[jax.ShapeDtypeStruct((N_DEV,) + shards[name], g.dtype) for name, g in zip(names, grads)],
        in_specs=[pl.BlockSpec(memory_space=pl.ANY)] * n,
        out_specs=[pl.BlockSpec(memory_space=pl.ANY)] * n,
        scratch_shapes=[pltpu.SemaphoreType.DMA((n,))],
    )(*grads)


def _sem_index(k, mask):
    return k * (N_DEV - 1) + mask - 1


def _exchange_start(name, groups, src_of, dst_of):
    sizes = [len(g[0]) for g in groups]
    names = [nm for g in groups for nm in g[0]]
    srcs = [s for g in groups for s in g[1]]
    lands = [l for g in groups for l in g[2]]
    n, n_g = len(names), len(groups)

    def body(*refs):
        src_refs, land_refs = refs[:n], refs[n:2 * n]
        sems = refs[2 * n:2 * n + 2 * n_g]
        token = refs[-1]
        me = _my_slot()
        first = 0
        for g, size in enumerate(sizes):
            for mask in range(1, N_DEV):
                for k in range(size):
                    a = first + k
                    pltpu.make_async_remote_copy(
                        src_ref=src_of(names[a], src_refs[a], me, mask), dst_ref=dst_of(names[a], land_refs[a], me),
                        send_sem=sems[2 * g].at[_sem_index(k, mask)], recv_sem=sems[2 * g + 1].at[_sem_index(k, mask)],
                        device_id=_peer(mask), device_id_type=MESH).start()
            first += size
        token[...] = jnp.zeros_like(token)

    sem_shapes = [pltpu.SemaphoreType.DMA((size * (N_DEV - 1),)) for size in sizes for _ in range(2)]
    outs = pl.pallas_call(
        body, name=name,
        out_shape=sem_shapes + [pltpu.HBM(a.shape, a.dtype) for a in srcs + lands] + [jax.ShapeDtypeStruct((8, 128), F32)],
        in_specs=[HBM] * (2 * n),
        out_specs=[SEM] * (2 * n_g) + [HBM] * (2 * n) + [pl.BlockSpec(memory_space=pltpu.VMEM)],
        input_output_aliases={k: 2 * n_g + k for k in range(2 * n)},
        compiler_params=pltpu.CompilerParams(has_side_effects=EFFECT),
    )(*[_in_hbm(a) for a in srcs + lands])
    sems, thru, token = outs[:2 * n_g], outs[2 * n_g:-1], outs[-1]
    handles, a = [], 0
    for g, size in enumerate(sizes):
        handles.append((groups[g][0], sems[2 * g], sems[2 * g + 1], thru[a:a + size], thru[n + a:n + a + size]))
        a += size
    return handles, token


def _exchange_wait(name, handle, src_of, dst_of, after):
    names, send_sems, recv_sems, srcs, lands = handle
    n = len(names)

    def body(*refs):
        src_refs, land_refs = refs[:n], refs[n:2 * n]
        send_ref, recv_ref = refs[2 * n], refs[2 * n + 1]
        me = _my_slot()
        for mask in range(1, N_DEV):
            for a in range(n):
                cp = pltpu.make_async_remote_copy(
                    src_ref=src_of(names[a], src_refs[a], me, mask), dst_ref=dst_of(names[a], land_refs[a], me ^ mask),
                    send_sem=send_ref.at[_sem_index(a, mask)], recv_sem=recv_ref.at[_sem_index(a, mask)],
                    device_id=_peer(mask), device_id_type=MESH)
                cp.wait_send()
                cp.wait_recv()

    outs = pl.pallas_call(
        body, name=name,
        out_shape=[pltpu.HBM(a.shape, a.dtype) for a in list(srcs) + list(lands)],
        in_specs=[HBM] * (2 * n) + [SEM, SEM, pl.BlockSpec(memory_space=pl.ANY)],
        out_specs=[HBM] * (2 * n),
        input_output_aliases={k: k for k in range(2 * n)},
        compiler_params=pltpu.CompilerParams(has_side_effects=EFFECT),
    )(*srcs, *lands, send_sems, recv_sems, after)
    return outs[n:]


def _gather_src(name, ref, me, mask):
    return ref


def _gather_dst(name, ref, slot):
    return SHARD_AT[name](ref, slot)


def _scatter_src(name, ref, me, mask):
    return SHARD_AT[name](ref, me ^ mask)


def _scatter_dst(name, ref, slot):
    return ref.at[slot]


def _vec_exchange(vec_grads):
    def body(vec, vec_out, send_sems, recv_sems, local_sem):
        me = _my_slot()
        local = pltpu.make_async_copy(vec, vec_out.at[me], local_sem)
        local.start()
        sends = []
        for mask in range(1, N_DEV):
            cp = pltpu.make_async_remote_copy(
                src_ref=vec, dst_ref=vec_out.at[me], send_sem=send_sems.at[mask - 1],
                recv_sem=recv_sems.at[mask - 1], device_id=_peer(mask), device_id_type=MESH)
            cp.start()
            sends.append(cp)
        for mask in range(1, N_DEV):
            pltpu.make_async_remote_copy(
                src_ref=vec, dst_ref=vec_out.at[me ^ mask], send_sem=send_sems.at[mask - 1],
                recv_sem=recv_sems.at[mask - 1], device_id=_peer(mask), device_id_type=MESH).wait_recv()
        for cp in sends:
            cp.wait_send()
        local.wait()

    return pl.pallas_call(
        body, name="vec_exchange",
        out_shape=jax.ShapeDtypeStruct((N_DEV,) + vec_grads.shape, F32),
        in_specs=[pl.BlockSpec(memory_space=pl.ANY)],
        out_specs=pl.BlockSpec(memory_space=pl.ANY),
        scratch_shapes=[pltpu.SemaphoreType.DMA((N_DEV - 1,)), pltpu.SemaphoreType.DMA((N_DEV - 1,)),
                        pltpu.SemaphoreType.DMA],
    )(vec_grads)


def _adamw(name, parts, w, m, v):
    rows, cols = w.shape
    tr = rows
    for cand in (256, 128, 64, 32, 31):
        if rows % cand == 0 and rows >= cand:
            tr = cand
            break

    def body(p_ref, w_ref, m_ref, v_ref, g_out, d_out, m_out, v_out):
        g = p_ref[0].astype(F32)
        for s in range(1, N_DEV):
            g = g + p_ref[s].astype(F32)
        m_new = ADAM_B1 * m_ref[...] + (1.0 - ADAM_B1) * g
        v_new = ADAM_B2 * v_ref[...] + (1.0 - ADAM_B2) * (g * g)
        m_hat = m_new / (1.0 - ADAM_B1 ** ADAM_STEP)
        v_hat = v_new / (1.0 - ADAM_B2 ** ADAM_STEP)
        g_out[...] = g
        d_out[...] = -ADAM_LR * (m_hat / (jnp.sqrt(v_hat) + ADAM_EPS) + ADAM_WD * w_ref[...])
        m_out[...] = m_new
        v_out[...] = v_new

    blk = pl.BlockSpec((tr, cols), lambda i: (i, 0))
    return pl.pallas_call(
        body, name="adamw_" + name, grid=(rows // tr,),
        in_specs=[pl.BlockSpec((N_DEV, tr, cols), lambda i: (0, i, 0)), blk, blk, blk],
        out_specs=[blk] * 4,
        out_shape=[jax.ShapeDtypeStruct((rows, cols), F32)] * 4,
        compiler_params=_params(("arbitrary",), 32),
    )(parts, w, m, v)


def _in_proj_fwd(x, g_pre, w_in):
    seq = x.shape[0]
    tm, tn = 512, IN_COLS // N_DEV

    def body(x_ref, g_ref, w_ref, proj_ref, u_ref):
        @pl.when(pl.program_id(1) == 0)
        def _():
            xf = x_ref[...]
            u_ref[...] = (xf * _rms_scale(xf) * g_ref[...]).astype(BF16)
        proj_ref[...] = _dot(u_ref[...], w_ref[...])

    return pl.pallas_call(
        body, name="in_proj_fwd", grid=(seq // tm, N_DEV),
        in_specs=[pl.BlockSpec((tm, D), lambda i, j: (i, 0)), pl.BlockSpec((1, D), lambda i, j: (0, 0)),
                  pl.BlockSpec((D, tn), lambda i, j: (0, j))],
        out_specs=[pl.BlockSpec((tm, tn), lambda i, j: (i, j)), pl.BlockSpec((tm, D), lambda i, j: (i, 0))],
        out_shape=[jax.ShapeDtypeStruct((seq, IN_COLS), F32), jax.ShapeDtypeStruct((seq, D), BF16)],
        compiler_params=_params(("arbitrary", "arbitrary"), 32),
    )(x, g_pre, w_in)


CONV_TM = 256
CONV_RS = 64


def _layer_norm_parts(cv):
    mu = jnp.mean(cv, axis=-1, keepdims=True)
    cen = cv - mu
    rstd = lax.rsqrt(jnp.mean(cen * cen, axis=-1, keepdims=True) + LN_EPS)
    return cen * rstd, rstd


def _conv_fwd(proj, dw, dw_bias, ln_g, ln_b, w_conv_out):
    seq = proj.shape[0]
    tm = CONV_TM

    def body(a_ref, gate_ref, dw_ref, bias_ref, lg_ref, lb_ref, w_ref, cv_ref, y_ref, ug):
        i = pl.program_id(0)

        @pl.when(i == 0)
        def _():
            ug[0:HALO, :] = jnp.zeros((HALO, D), F32)

        @pl.when(i > 0)
        def _():
            ug[0:HALO, :] = ug[tm:tm + HALO, :]
        ug[HALO:HALO + tm, :] = a_ref[...] * _sigmoid(gate_ref[...])

        def channel_block(cb, carry):
            lanes = pl.ds(pl.multiple_of(cb * 128, 128), 128)
            for r0 in range(0, tm, CONV_RS):
                acc = jnp.broadcast_to(bias_ref[:, lanes], (CONV_RS, 128))
                for k in range(TAPS):
                    acc = acc + dw_ref[cb, k:k + 1, :] * ug[pl.ds(r0 + HALO - (TAPS - 1) + k, CONV_RS), lanes]
                cv_ref[pl.ds(r0, CONV_RS), lanes] = acc
            return carry
        lax.fori_loop(0, D // 128, channel_block, 0)

        n, _ = _layer_norm_parts(cv_ref[...])
        ln = n * lg_ref[...] + lb_ref[...]
        y_ref[...] = _dot((ln * _sigmoid(ln)).astype(BF16), w_ref[...])

    vec = pl.BlockSpec((1, D), lambda i: (0, 0))
    tile = pl.BlockSpec((tm, D), lambda i: (i, 0))
    return pl.pallas_call(
        body, name="conv_fwd", grid=(seq // tm,),
        in_specs=[pl.BlockSpec((tm, D), lambda i: (i, 0)), pl.BlockSpec((tm, D), lambda i: (i, 1)),
                  pl.BlockSpec((N_DEV, TAPS, 128), lambda i: (0, 0, 0)), vec, vec, vec,
                  pl.BlockSpec((D, D), lambda i: (0, 0))],
        out_specs=[tile, tile],
        out_shape=[jax.ShapeDtypeStruct((seq, D), F32)] * 2,
        scratch_shapes=[pltpu.VMEM((HALO + tm, D), F32)],
        compiler_params=_params(("arbitrary",), 32),
    )(proj, proj, dw, dw_bias, ln_g, ln_b, w_conv_out)


def _pool_counts(tile_index, tm, window):
    t = tile_index * tm + lax.broadcasted_iota(jnp.int32, (tm, 1), 0)
    return 1.0 / jnp.minimum(t + 1, window).astype(F32)


def _pool_merge_fwd(proj, y_conv, x, pool_w, pool_scale, w_pool_out, w_o, g_post):
    seq = proj.shape[0]
    tm = CONV_TM

    def body(p_ref, gc_ref, gp_ref, yc_ref, x_ref, pw_ref, ps_ref, wpo_ref, wo_ref, g_ref,
             z_ref, zl_ref, yp_ref, mg_ref, o_ref, h1_ref, pbuf):
        i = pl.program_id(0)

        @pl.when(i == 0)
        def _():
            pbuf[0:HALO, :] = jnp.zeros((HALO, D), F32)

        @pl.when(i > 0)
        def _():
            pbuf[0:HALO, :] = pbuf[tm:tm + HALO, :]
        pbuf[HALO:HALO + tm, :] = p_ref[...]

        for g, window in enumerate(POOL_WINDOWS):
            lanes = pl.ds(g * PG, PG)
            acc = pbuf[pl.ds(HALO, tm), lanes]
            for j in range(1, window):
                acc = acc + pbuf[pl.ds(HALO - j, tm), lanes]
            zg = acc * _pool_counts(i, tm, window) - pbuf[pl.ds(HALO, tm), lanes]
            z_ref[:, lanes] = zg.astype(BF16)
            zl_ref[:, lanes] = _dot(zg.astype(BF16), pw_ref[g])
        zl = zl_ref[...]
        y_pool = _dot((zl * ps_ref[...]).astype(BF16), wpo_ref[...])
        yp_ref[...] = y_pool
        merged = (_sigmoid(gc_ref[...]) * yc_ref[...] + _sigmoid(gp_ref[...]) * y_pool).astype(BF16)
        mg_ref[...] = merged
        o = _dot(merged, wo_ref[...])
        o_ref[...] = o
        h1_ref[...] = x_ref[...] + o * _rms_scale(o) * g_ref[...]

    vec = pl.BlockSpec((1, D), lambda i: (0, 0))
    tile = pl.BlockSpec((tm, D), lambda i: (i, 0))
    mat = pl.BlockSpec((D, D), lambda i: (0, 0))
    return pl.pallas_call(
        body, name="pool_merge_fwd", grid=(seq // tm,),
        in_specs=[pl.BlockSpec((tm, D), lambda i: (i, 2)), pl.BlockSpec((tm, D), lambda i: (i, 3)),
                  pl.BlockSpec((tm, D), lambda i: (i, 4)), tile, tile,
                  pl.BlockSpec((4, PG, PG), lambda i: (0, 0, 0)), vec, mat, mat, vec],
        out_specs=[tile] * 6,
        out_shape=[jax.ShapeDtypeStruct((seq, D), dt) for dt in (BF16, F32, F32, BF16, F32, F32)],
        scratch_shapes=[pltpu.VMEM((HALO + tm, D), F32)],
        compiler_params=_params(("arbitrary",), 48),
    )(proj, proj, proj, y_conv, x, pool_w, pool_scale, w_pool_out, w_o, g_post)


def _mlp_fwd(h1, g_pre, w_ff1, w_ff2, g_post, target):
    seq = h1.shape[0]
    tm, tf = 512, D_FF // N_DEV
    n_f = D_FF // tf

    def body(h1_ref, gpre_ref, w1_ref, w2_ref, gpost_ref, tgt_ref, v_ref, dm_ref, dh2_ref, sse_ref, ggrad_ref, macc):
        i, j = pl.program_id(0), pl.program_id(1)

        @pl.when(j == 0)
        def _():
            h = h1_ref[...]
            v_ref[...] = (h * _rms_scale(h) * gpre_ref[...]).astype(BF16)
        f = jnp.maximum(_dot(v_ref[...], w1_ref[...]), 0.0)
        part = _dot((f * f).astype(BF16), w2_ref[...])

        @pl.when(j == 0)
        def _():
            macc[...] = part

        @pl.when(j > 0)
        def _():
            macc[...] += part

        @pl.when(j == n_f - 1)
        def _():
            mo = macc[...]
            err = h1_ref[...] + mo * _rms_scale(mo) * gpost_ref[...] - tgt_ref[...]
            dh2 = err * (1.0 / D)
            dh2_ref[...] = dh2
            dm, ggrad = _rms_bwd(dh2, mo, gpost_ref[...])
            dm_ref[...] = dm.astype(BF16)
            _acc_out(ggrad_ref, i == 0, ggrad)
            _acc_out(sse_ref, i == 0, jnp.sum(jnp.sum(err * err, axis=1, keepdims=True), axis=0, keepdims=True))

    vec = pl.BlockSpec((1, D), lambda i, j: (0, 0))
    tile = pl.BlockSpec((tm, D), lambda i, j: (i, 0))
    return pl.pallas_call(
        body, name="mlp_fwd", grid=(seq // tm, n_f),
        in_specs=[tile, vec, pl.BlockSpec((D, tf), lambda i, j: (0, j)), pl.BlockSpec((tf, D), lambda i, j: (j, 0)), vec, tile],
        out_specs=[tile, tile, tile, pl.BlockSpec((1, 1), lambda i, j: (0, 0)), vec],
        out_shape=[jax.ShapeDtypeStruct((seq, D), BF16), jax.ShapeDtypeStruct((seq, D), BF16),
                   jax.ShapeDtypeStruct((seq, D), F32), jax.ShapeDtypeStruct((1, 1), F32),
                   jax.ShapeDtypeStruct((1, D), F32)],
        scratch_shapes=[pltpu.VMEM((tm, D), F32)],
        compiler_params=_params(("arbitrary", "arbitrary"), 40),
    )(h1, g_pre, w_ff1, w_ff2, g_post, target)


def _mlp_bwd(v, dm, w_ff1, w_ff2):
    seq = v.shape[0]
    tm, tf = 512, D_FF // N_DEV
    n_t = seq // tm

    def body(v_ref, dm_ref, w1_ref, w2_ref, dv_hbm, g1_ref, g2_ref, dv_acc, g1_acc, g2_acc, sem):
        j, i = pl.program_id(0), pl.program_id(1)
        vt, dmt = v_ref[...], dm_ref[...]
        f = jnp.maximum(_dot(vt, w1_ref[...]), 0.0)
        df = (_dot_nt(dmt, w2_ref[...]) * (2.0 * f)).astype(BF16)
        rows = pl.ds(pl.multiple_of(i * tm, tm), tm)
        dv_part = _dot_nt(df, w1_ref[...])

        @pl.when(j == 0)
        def _():
            dv_acc[rows, :] = dv_part

        @pl.when(j > 0)
        def _():
            dv_acc[rows, :] += dv_part
        g1_part = _dot_tn(vt, df)
        g2_part = _dot_tn((f * f).astype(BF16), dmt)

        @pl.when(i == 0)
        def _():
            g1_acc[...] = g1_part
            g2_acc[...] = g2_part

        @pl.when(i > 0)
        def _():
            g1_acc[...] += g1_part
            g2_acc[...] += g2_part

        @pl.when(i == n_t - 1)
        def _():
            g1_ref[...] = g1_acc[...].astype(BF16)
            g2_ref[...] = g2_acc[...].astype(BF16)

        @pl.when(jnp.logical_and(i == n_t - 1, j == D_FF // tf - 1))
        def _():
            cp = pltpu.make_async_copy(dv_acc, dv_hbm, sem)
            cp.start()
            cp.wait()

    tile = pl.BlockSpec((tm, D), lambda j, i: (i, 0))
    return pl.pallas_call(
        body, name="mlp_bwd", grid=(D_FF // tf, n_t),
        in_specs=[tile, tile, pl.BlockSpec((D, tf), lambda j, i: (0, j)), pl.BlockSpec((tf, D), lambda j, i: (j, 0))],
        out_specs=[pl.BlockSpec(memory_space=pl.ANY), pl.BlockSpec((D, tf), lambda j, i: (0, j)),
                   pl.BlockSpec((tf, D), lambda j, i: (j, 0))],
        out_shape=[jax.ShapeDtypeStruct((seq, D), F32), jax.ShapeDtypeStruct((D, D_FF), BF16),
                   jax.ShapeDtypeStruct((D_FF, D), BF16)],
        scratch_shapes=[pltpu.VMEM((seq, D), F32), pltpu.VMEM((D, tf), F32), pltpu.VMEM((tf, D), F32),
                        pltpu.SemaphoreType.DMA],
        compiler_params=_params(("arbitrary", "arbitrary"), 52),
    )(v, dm, w_ff1, w_ff2)


ANY = pl.BlockSpec(memory_space=pl.ANY)


def _merge_bwd(dh2, dv, h1, g_mlp_pre, o, g_mix_post, w_o, merged, proj, y_conv, y_pool, token):
    seq = dh2.shape[0]
    tm = 256
    n_t = seq // tm

    def body(dh2_ref, dv_ref, h1_ref, gpre_ref, o_ref, gpost_ref, wo_ref, mg_ref, gc_ref, gp_ref, yc_ref, yp_ref, _token,
             dh1_ref, dyc_ref, dyp_ref, dg_ref, gwo_ref, ggpre_ref, ggpost_ref, gwo_acc):
        i = pl.program_id(0)
        dnorm, ggpre = _rms_bwd(dv_ref[...], h1_ref[...], gpre_ref[...])
        dh1 = dh2_ref[...] + dnorm
        dh1_ref[...] = dh1
        do, ggpost = _rms_bwd(dh1, o_ref[...], gpost_ref[...])
        do = do.astype(BF16)
        _acc_out(ggpre_ref, i == 0, ggpre)
        _acc_out(ggpost_ref, i == 0, ggpost)
        _acc_out(gwo_acc, i == 0, _dot_tn(mg_ref[...], do))
        dmerged = _dot_nt(do, wo_ref[...])
        sc, sp = _sigmoid(gc_ref[...]), _sigmoid(gp_ref[...])
        dyc_ref[...] = (dmerged * sc).astype(BF16)
        dyp_ref[...] = (dmerged * sp).astype(BF16)
        dg_ref[:, 0:D] = (dmerged * yc_ref[...] * (sc * (1.0 - sc))).astype(BF16)
        dg_ref[:, D:2 * D] = (dmerged * yp_ref[...] * (sp * (1.0 - sp))).astype(BF16)

        @pl.when(i == n_t - 1)
        def _():
            gwo_ref[...] = gwo_acc[...].astype(BF16)

    vec = pl.BlockSpec((1, D), lambda i: (0, 0))
    tile = pl.BlockSpec((tm, D), lambda i: (i, 0))
    mat = pl.BlockSpec((D, D), lambda i: (0, 0))
    return pl.pallas_call(
        body, name="merge_bwd", grid=(n_t,),
        in_specs=[tile, tile, tile, vec, tile, vec, mat, tile,
                  pl.BlockSpec((tm, D), lambda i: (i, 3)), pl.BlockSpec((tm, D), lambda i: (i, 4)), tile, tile, ANY],
        out_specs=[tile, tile, tile, pl.BlockSpec((tm, 2 * D), lambda i: (i, 0)), mat, vec, vec],
        out_shape=[jax.ShapeDtypeStruct((seq, D), F32), jax.ShapeDtypeStruct((seq, D), BF16),
                   jax.ShapeDtypeStruct((seq, D), BF16), jax.ShapeDtypeStruct((seq, 2 * D), BF16),
                   jax.ShapeDtypeStruct((D, D), BF16), jax.ShapeDtypeStruct((1, D), F32),
                   jax.ShapeDtypeStruct((1, D), F32)],
        scratch_shapes=[pltpu.VMEM((D, D), F32)],
        compiler_params=_params(("arbitrary",), 48),
    )(dh2, dv, h1, g_mlp_pre, o, g_mix_post, w_o, merged, proj, proj, y_conv, y_pool, token)


def _pool_bwd(dy_pool, zl, z, pool_w, pool_scale, w_pool_out):
    seq = dy_pool.shape[0]
    tm = CONV_TM
    n_t = seq // tm

    def body(dy_ref, zl_ref, z_ref, pw_ref, ps_ref, wpo_ref, dp_ref, gwpo_ref, gpw_ref, gps_ref, qbuf, gwpo_acc, gpw_acc):
        i = pl.program_id(0)
        tile_index = n_t - 1 - i
        first = i == 0
        dy = dy_ref[...]
        zl = zl_ref[...]
        dzs = _dot_nt(dy, wpo_ref[...])
        _acc_out(gwpo_acc, first, _dot_tn((zl * ps_ref[...]).astype(BF16), dy))
        _acc_out(gps_ref, first, jnp.sum(dzs * zl, axis=0, keepdims=True))
        dzl = (dzs * ps_ref[...]).astype(BF16)

        @pl.when(first)
        def _():
            qbuf[tm:tm + HALO, :] = jnp.zeros((HALO, D), F32)

        @pl.when(jnp.logical_not(first))
        def _():
            qbuf[tm:tm + HALO, :] = qbuf[0:HALO, :]

        dzs_list = []
        for g, window in enumerate(POOL_WINDOWS):
            lanes = pl.ds(g * PG, PG)
            dzl_g = dzl[:, g * PG:(g + 1) * PG]
            dz = _dot_nt(dzl_g, pw_ref[g])
            _acc_out(gpw_acc.at[g], first, _dot_tn(z_ref[:, lanes], dzl_g))
            qbuf[pl.ds(0, tm), lanes] = dz * _pool_counts(tile_index, tm, window)
            dzs_list.append(dz)
        for g, window in enumerate(POOL_WINDOWS):
            lanes = pl.ds(g * PG, PG)
            acc = qbuf[pl.ds(0, tm), lanes]
            for j in range(1, window):
                acc = acc + qbuf[pl.ds(j, tm), lanes]
            dp_ref[:, lanes] = (acc - dzs_list[g]).astype(BF16)

        @pl.when(i == n_t - 1)
        def _():
            gwpo_ref[...] = gwpo_acc[...].astype(BF16)
            gpw_ref[...] = gpw_acc[...].astype(BF16)

    vec = pl.BlockSpec((1, D), lambda i: (0, 0))
    tile = pl.BlockSpec((tm, D), lambda i: (n_t - 1 - i, 0))
    mat = pl.BlockSpec((D, D), lambda i: (0, 0))
    pw = pl.BlockSpec((4, PG, PG), lambda i: (0, 0, 0))
    return pl.pallas_call(
        body, name="pool_bwd", grid=(n_t,),
        in_specs=[tile, tile, tile, pw, vec, mat],
        out_specs=[tile, mat, pw, vec],
        out_shape=[jax.ShapeDtypeStruct((seq, D), BF16), jax.ShapeDtypeStruct((D, D), BF16),
                   jax.ShapeDtypeStruct((4, PG, PG), BF16), jax.ShapeDtypeStruct((1, D), F32)],
        scratch_shapes=[pltpu.VMEM((tm + HALO, D), F32), pltpu.VMEM((D, D), F32), pltpu.VMEM((4, PG, PG), F32)],
        compiler_params=_params(("arbitrary",), 40),
    )(dy_pool, zl, z, pool_w, pool_scale, w_pool_out)


def _conv_bwd(dy_conv, cv, proj, dw, ln_g, ln_b, w_conv_out):
    seq = dy_conv.shape[0]
    tm = CONV_TM
    n_t = seq // tm
    halo_blocks = tm // HALO

    def body(dy_ref, cv_ref, a_ref, gate_ref, ah_ref, gh_ref, dw_ref, lg_ref, lb_ref, w_ref,
             dglu_ref, gw_ref, gdw_ref, gbias_ref, glg_ref, glb_ref, ug, dcv, dug, gw_acc):
        i = pl.program_id(0)
        tile_index = n_t - 1 - i
        first = i == 0
        dy = dy_ref[...]
        n, rstd = _layer_norm_parts(cv_ref[...])
        ln = n * lg_ref[...] + lb_ref[...]
        sg = _sigmoid(ln)
        _acc_out(gw_acc, first, _dot_tn((ln * sg).astype(BF16), dy))
        dln = _dot_nt(dy, w_ref[...]) * (sg * (1.0 + ln * (1.0 - sg)))
        _acc_out(glg_ref, first, jnp.sum(dln * n, axis=0, keepdims=True))
        _acc_out(glb_ref, first, jnp.sum(dln, axis=0, keepdims=True))
        dn = dln * lg_ref[...]
        dcv_tile = rstd * (dn - jnp.mean(dn, axis=-1, keepdims=True) - n * jnp.mean(dn * n, axis=-1, keepdims=True))
        _acc_out(gbias_ref, first, jnp.sum(dcv_tile, axis=0, keepdims=True))

        @pl.when(first)
        def _():
            dcv[tm:tm + HALO, :] = jnp.zeros((HALO, D), F32)

        @pl.when(jnp.logical_not(first))
        def _():
            dcv[tm:tm + HALO, :] = dcv[0:HALO, :]
        dcv[0:tm, :] = dcv_tile

        a, gate = a_ref[...], gate_ref[...]
        sgate = _sigmoid(gate)
        ug[HALO:HALO + tm, :] = a * sgate
        before = jnp.where(tile_index > 0, 1.0, 0.0)
        ug[0:HALO, :] = ah_ref[...] * _sigmoid(gh_ref[...]) * before

        @pl.when(first)
        def _():
            gdw_ref[...] = jnp.zeros((N_DEV, TAPS + 1, 128), F32)

        def channel_block(cb, carry):
            lanes = pl.ds(pl.multiple_of(cb * 128, 128), 128)
            for r0 in range(0, tm, CONV_RS):
                acc = jnp.zeros((CONV_RS, 128), F32)
                for k in range(TAPS):
                    acc = acc + dw_ref[cb, k:k + 1, :] * dcv[pl.ds(r0 + (TAPS - 1) - k, CONV_RS), lanes]
                dug[pl.ds(r0, CONV_RS), lanes] = acc
            for k in range(TAPS):
                part = jnp.zeros((8, 128), F32)
                for r0 in range(0, tm, CONV_RS):
                    prod = dcv[pl.ds(r0, CONV_RS), lanes] * ug[pl.ds(r0 + HALO - (TAPS - 1) + k, CONV_RS), lanes]
                    part = part + jnp.sum(prod.reshape(CONV_RS // 8, 8, 128), axis=0)
                gdw_ref[cb, k:k + 1, :] += jnp.sum(part, axis=0, keepdims=True)
            return carry
        lax.fori_loop(0, D // 128, channel_block, 0)

        d_ug = dug[...]
        dglu_ref[:, 0:D] = (d_ug * sgate).astype(BF16)
        dglu_ref[:, D:2 * D] = (d_ug * a * (sgate * (1.0 - sgate))).astype(BF16)

        @pl.when(i == n_t - 1)
        def _():
            gw_ref[...] = gw_acc[...].astype(BF16)

    def halo_index(col):
        return lambda i: (jnp.maximum((n_t - 1 - i) * halo_blocks - 1, 0), col)

    vec = pl.BlockSpec((1, D), lambda i: (0, 0))
    tile = pl.BlockSpec((tm, D), lambda i: (n_t - 1 - i, 0))
    mat = pl.BlockSpec((D, D), lambda i: (0, 0))
    dwspec = pl.BlockSpec((N_DEV, TAPS, 128), lambda i: (0, 0, 0))
    return pl.pallas_call(
        body, name="conv_bwd", grid=(n_t,),
        in_specs=[tile, tile, pl.BlockSpec((tm, D), lambda i: (n_t - 1 - i, 0)), pl.BlockSpec((tm, D), lambda i: (n_t - 1 - i, 1)),
                  pl.BlockSpec((HALO, D), halo_index(0)), pl.BlockSpec((HALO, D), halo_index(1)), dwspec, vec, vec, mat],
        out_specs=[pl.BlockSpec((tm, 2 * D), lambda i: (n_t - 1 - i, 0)), mat,
                   pl.BlockSpec((N_DEV, TAPS + 1, 128), lambda i: (0, 0, 0)), vec, vec, vec],
        out_shape=[jax.ShapeDtypeStruct((seq, 2 * D), BF16), jax.ShapeDtypeStruct((D, D), BF16),
                   jax.ShapeDtypeStruct((N_DEV, TAPS + 1, 128), F32), jax.ShapeDtypeStruct((1, D), F32),
                   jax.ShapeDtypeStruct((1, D), F32), jax.ShapeDtypeStruct((1, D), F32)],
        scratch_shapes=[pltpu.VMEM((HALO + tm, D), F32), pltpu.VMEM((tm + HALO, D), F32), pltpu.VMEM((tm, D), F32),
                        pltpu.VMEM((D, D), F32)],
        compiler_params=_params(("arbitrary",), 48),
    )(dy_conv, cv, proj, proj, proj, proj, dw, ln_g, ln_b, w_conv_out)


def _in_proj_bwd_x(d_glu, dp, dgates, w_in, x, g_pre, dh1, token):
    seq = x.shape[0]
    tm = 256

    def body(dglu_ref, dp_ref, dg_ref, w_ref, x_ref, g_ref, dh1_ref, _token, dx_ref, gg_ref):
        du = _dot_nt(dglu_ref[...], w_ref[:, 0:2 * D])
        du += _dot_nt(dp_ref[...], w_ref[:, 2 * D:3 * D])
        du += _dot_nt(dg_ref[...], w_ref[:, 3 * D:5 * D])
        dnorm, gg = _rms_bwd(du, x_ref[...], g_ref[...])
        dx_ref[...] = dh1_ref[...] + dnorm
        _acc_out(gg_ref, pl.program_id(0) == 0, gg)

    vec = pl.BlockSpec((1, D), lambda i: (0, 0))
    tile = pl.BlockSpec((tm, D), lambda i: (i, 0))
    wide = pl.BlockSpec((tm, 2 * D), lambda i: (i, 0))
    return pl.pallas_call(
        body, name="in_proj_bwd_x", grid=(seq // tm,),
        in_specs=[wide, tile, wide, pl.BlockSpec((D, IN_COLS), lambda i: (0, 0)), tile, vec, tile, ANY],
        out_specs=[tile, vec],
        out_shape=[jax.ShapeDtypeStruct((seq, D), F32), jax.ShapeDtypeStruct((1, D), F32)],
        compiler_params=_params(("arbitrary",), 48),
    )(d_glu, dp, dgates, w_in, x, g_pre, dh1, token)


def _in_proj_bwd_w(u, d_glu, dp, dgates, token):
    seq = u.shape[0]
    tm = 512
    n_t = seq // tm

    def body(u_ref, dglu_ref, dp_ref, dg_ref, _token, out_ref, acc):
        b, i = pl.program_id(0), pl.program_id(1)
        ut = u_ref[...]

        def add(d_ref):
            _acc_out(acc, i == 0, _dot_tn(ut, d_ref[...]))

        pl.when(b < 2)(lambda: add(dglu_ref))
        pl.when(b == 2)(lambda: add(dp_ref))
        pl.when(b > 2)(lambda: add(dg_ref))

        @pl.when(i == n_t - 1)
        def _():
            out_ref[...] = acc[...].astype(BF16)

    return pl.pallas_call(
        body, name="in_proj_bwd_w", grid=(IN_COLS // D, n_t),
        in_specs=[pl.BlockSpec((tm, D), lambda b, i: (i, 0)),
                  pl.BlockSpec((tm, D), lambda b, i: (jnp.where(b < 2, i, 0), jnp.minimum(b, 1))),
                  pl.BlockSpec((tm, D), lambda b, i: (jnp.where(b == 2, i, 0), 0)),
                  pl.BlockSpec((tm, D), lambda b, i: (jnp.where(b > 2, i, 0), jnp.maximum(b - 3, 0))), ANY],
        out_specs=pl.BlockSpec((D, D), lambda b, i: (0, b)),
        out_shape=jax.ShapeDtypeStruct((D, IN_COLS), BF16),
        scratch_shapes=[pltpu.VMEM((D, D), F32)],
        compiler_params=_params(("arbitrary", "arbitrary"), 40),
    )(u, d_glu, dp, dgates, token)


VEC_NAMES = ("mix_pre_g", "dw_bias", "conv_ln_g", "conv_ln_b", "pool_scale", "mix_post_g", "mlp_pre_g", "mlp_post_g")
WEIGHT_ORDER = ("mix_pre_g", "w_in", "dw_kernel", "dw_bias", "conv_ln_g", "conv_ln_b", "w_conv_out", "pool_w",
                "pool_scale", "w_pool_out", "w_o", "mix_post_g", "mlp_pre_g", "w_ff1", "w_ff2", "mlp_post_g")


def _step(x, loss_target, w, m, v):
    row = lambda a: a.reshape(1, D)
    names = [s[0] for s in SHARDED]

    stage, seeded = _stage_shards([w[n] for n in names])
    stage, seeded = dict(zip(names, stage)), dict(zip(names, seeded))
    gather_groups = (("w_in",), ("w_conv_out", "dw_kernel"), ("pool_w", "w_pool_out", "w_o"), ("w_ff1", "w_ff2"))
    handles, token = _exchange_start(
        "gather_start", [(g, [stage[n] for n in g], [seeded[n] for n in g]) for g in gather_groups], _gather_src, _gather_dst)
    full = {}

    def gathered(k, after):
        arrays = _exchange_wait("gather_wait_" + gather_groups[k][0], handles[k], _gather_src, _gather_dst, after)
        full.update(zip(gather_groups[k], arrays))

    gathered(0, token)
    proj, u = _in_proj_fwd(x, row(w["mix_pre_g"]), full["w_in"])
    gathered(1, proj)
    cv, y_conv = _conv_fwd(proj, full["dw_kernel"], row(w["dw_bias"]), row(w["conv_ln_g"]), row(w["conv_ln_b"]),
                           full["w_conv_out"])
    gathered(2, y_conv)
    z, zl, y_pool, merged, o, h1 = _pool_merge_fwd(proj, y_conv, x, full["pool_w"], row(w["pool_scale"]),
                                                   full["w_pool_out"], full["w_o"], row(w["mix_post_g"]))
    gathered(3, h1)
    vv, dm, dh2, sse, g_mlp_post = _mlp_fwd(h1, row(w["mlp_pre_g"]), full["w_ff1"], full["w_ff2"],
                                            row(w["mlp_post_g"]), loss_target)

    def scatter_start(tag, group, grads):
        (handle,), tok = _exchange_start("scatter_start_" + tag, [(group, grads, _own_slices(group, grads))],
                                         _scatter_src, _scatter_dst)
        return handle, tok

    dv, g_ff1, g_ff2 = _mlp_bwd(vv, dm, full["w_ff1"], full["w_ff2"])
    h_ff, tok_ff = scatter_start("ff", ("w_ff1", "w_ff2"), [g_ff1, g_ff2])
    dh1, dy_conv, dy_pool, dgates, g_wo, g_mlp_pre, g_mix_post = _merge_bwd(
        dh2, dv, h1, row(w["mlp_pre_g"]), o, row(w["mix_post_g"]), full["w_o"], merged, proj, y_conv, y_pool, tok_ff)
    dp, g_wpo, g_pw, g_pool_scale = _pool_bwd(dy_pool, zl, z, full["pool_w"], row(w["pool_scale"]), full["w_pool_out"])
    d_glu, g_wco, g_dw, g_bias, g_ln_g, g_ln_b = _conv_bwd(dy_conv, cv, proj, full["dw_kernel"], row(w["conv_ln_g"]),
                                                            row(w["conv_ln_b"]), full["w_conv_out"])
    h_mix, tok_mix = scatter_start("mix", ("w_o", "w_pool_out", "pool_w", "w_conv_out", "dw_kernel"),
                                   [g_wo, g_wpo, g_pw, g_wco, g_dw[:, :TAPS, :]])
    g_win = _in_proj_bwd_w(u, d_glu, dp, dgates, tok_mix)
    h_in, tok_in = scatter_start("in", ("w_in",), [g_win])
    grad_x, g_mix_pre = _in_proj_bwd_x(d_glu, dp, dgates, full["w_in"], x, row(w["mix_pre_g"]), dh1, tok_in)

    parts = {}
    for tag, handle in (("ff", h_ff), ("mix", h_mix), ("in", h_in)):
        parts.update(zip(handle[0], _exchange_wait("scatter_wait_" + tag, handle, _scatter_src, _scatter_dst, grad_x)))
    vec_grads = jnp.concatenate(
        [jnp.pad(g, ((0, 7), (0, 0))) for g in
         (g_mix_pre, g_bias, g_ln_g, g_ln_b, g_pool_scale, g_mix_post, g_mlp_pre, g_mlp_post)], axis=0)
    vec_parts = _vec_exchange(vec_grads)

    out = {}
    for name, _, shard, _ in SHARDED:
        as2d = (lambda a: a.reshape(-1, a.shape[-1]))
        p = parts[name]
        res = _adamw(name, p.reshape(N_DEV, -1, p.shape[-1]), as2d(w[name]), as2d(m[name]), as2d(v[name]))
        out[name] = [r.reshape(shard) for r in res]
    stack = lambda d: jnp.stack([d[n] for n in VEC_NAMES], axis=0)
    vec_parts = vec_parts.reshape(N_DEV, 8, 8, D)[:, :, 0, :]
    res = _adamw("vectors", vec_parts, stack(w), stack(m), stack(v))
    for k, name in enumerate(VEC_NAMES):
        out[name] = [r[k] for r in res]

    loss = lax.psum(sse[0, 0] * (0.5 / D), ("x", "y", "c"))
    return loss, grad_x, out


def kernel(x, mix_pre_g, w_in, dw_kernel, dw_bias, conv_ln_g, conv_ln_b, w_conv_out, pool_w, pool_scale, w_pool_out, w_o, mix_post_g, mlp_pre_g, w_ff1, w_ff2, mlp_post_g, loss_target, m_mix_pre_g, m_w_in, m_dw_kernel, m_dw_bias, m_conv_ln_g, m_conv_ln_b, m_w_conv_out, m_pool_w, m_pool_scale, m_w_pool_out, m_w_o, m_mix_post_g, m_mlp_pre_g, m_w_ff1, m_w_ff2, m_mlp_post_g, v_mix_pre_g, v_w_in, v_dw_kernel, v_dw_bias, v_conv_ln_g, v_conv_ln_b, v_w_conv_out, v_pool_w, v_pool_scale, v_w_pool_out, v_w_o, v_mix_post_g, v_mlp_pre_g, v_w_ff1, v_w_ff2, v_mlp_post_g):
    w = dict(mix_pre_g=mix_pre_g, w_in=w_in, dw_kernel=dw_kernel, dw_bias=dw_bias, conv_ln_g=conv_ln_g, conv_ln_b=conv_ln_b,
             w_conv_out=w_conv_out, pool_w=pool_w, pool_scale=pool_scale, w_pool_out=w_pool_out, w_o=w_o,
             mix_post_g=mix_post_g, mlp_pre_g=mlp_pre_g, w_ff1=w_ff1, w_ff2=w_ff2, mlp_post_g=mlp_post_g)
    m = dict(mix_pre_g=m_mix_pre_g, w_in=m_w_in, dw_kernel=m_dw_kernel, dw_bias=m_dw_bias, conv_ln_g=m_conv_ln_g,
             conv_ln_b=m_conv_ln_b, w_conv_out=m_w_conv_out, pool_w=m_pool_w, pool_scale=m_pool_scale,
             w_pool_out=m_w_pool_out, w_o=m_w_o, mix_post_g=m_mix_post_g, mlp_pre_g=m_mlp_pre_g, w_ff1=m_w_ff1,
             w_ff2=m_w_ff2, mlp_post_g=m_mlp_post_g)
    v = dict(mix_pre_g=v_mix_pre_g, w_in=v_w_in, dw_kernel=v_dw_kernel, dw_bias=v_dw_bias, conv_ln_g=v_conv_ln_g,
             conv_ln_b=v_conv_ln_b, w_conv_out=v_w_conv_out, pool_w=v_pool_w, pool_scale=v_pool_scale,
             w_pool_out=v_w_pool_out, w_o=v_w_o, mix_post_g=v_mix_post_g, mlp_pre_g=v_mlp_pre_g, w_ff1=v_w_ff1,
             w_ff2=v_w_ff2, mlp_post_g=v_mlp_post_g)
    seq = x.shape[1]
    loss, grad_x, out = _step(x.reshape(seq, D), loss_target.reshape(seq, D), w, m, v)
    grads, deltas, new_m, new_v = ([out[n][k] for n in WEIGHT_ORDER] for k in range(4))
    return (loss, grad_x.reshape(x.shape), *grads, *deltas, *new_m, *new_v)
```

```python
import functools

import jax
import jax.numpy as jnp
from jax import lax
from jax.experimental import pallas as pl
from jax.experimental.pallas import tpu as pltpu

D = 1024
D_FF = 4 * D
IN_COLS = 5 * D
TAPS = 31
HALO = 32
POOL_WINDOWS = (2, 4, 8, 16)
PG = D // 4
N_DEV = 8
RMS_EPS = 1e-6
LN_EPS = 1e-5
ADAM_LR, ADAM_B1, ADAM_B2, ADAM_EPS, ADAM_WD, ADAM_STEP = 0.001, 0.9, 0.999, 1e-08, 0.01, 10

BF16 = jnp.bfloat16
F32 = jnp.float32
MIB = 1 << 20
MESH = pl.DeviceIdType.MESH


def _params(sem, vmem_mib):
    return pltpu.CompilerParams(dimension_semantics=sem, vmem_limit_bytes=vmem_mib * MIB)


def _dot(a, b):
    return jnp.dot(a, b, preferred_element_type=F32)


def _dot_nt(a, b):
    return lax.dot_general(a, b, (((1,), (1,)), ((), ())), preferred_element_type=F32)


def _dot_tn(a, b):
    return lax.dot_general(a, b, (((0,), (0,)), ((), ())), preferred_element_type=F32)


def _rms_scale(x):
    return lax.rsqrt(jnp.mean(x * x, axis=-1, keepdims=True) + RMS_EPS)


def _rms_bwd(dy, x, g):
    xn = x * _rms_scale(x)
    dn = dy * g
    dx = _rms_scale(x) * (dn - xn * jnp.mean(dn * xn, axis=-1, keepdims=True))
    return dx, jnp.sum(dy * xn, axis=0, keepdims=True)


def _sigmoid(x):
    return jax.nn.sigmoid(x)


def _acc_out(ref, first, value):
    @pl.when(first)
    def _():
        ref[...] = value

    @pl.when(jnp.logical_not(first))
    def _():
        ref[...] += value


def _my_slot():
    return 4 * lax.axis_index("x") + 2 * lax.axis_index("y") + lax.axis_index("c")


def _peer(mask):
    x, y, c = lax.axis_index("x"), lax.axis_index("y"), lax.axis_index("c")
    return (x ^ ((mask >> 2) & 1), y ^ ((mask >> 1) & 1), c ^ (mask & 1))


def _cols(width):
    return lambda ref, slot: ref.at[:, pl.ds(pl.multiple_of(slot * width, 128), width)]


def _rows(height):
    return lambda ref, slot: ref.at[pl.ds(pl.multiple_of(slot * height, 8), height), :]


def _lead(ref, slot):
    return ref.at[slot]


def _pool_rows(ref, slot):
    return ref.at[:, pl.ds(pl.multiple_of(slot * (PG // N_DEV), 8), PG // N_DEV), :]


SHARDED = (
    ("w_in", (D, IN_COLS), (D, IN_COLS // N_DEV), _cols(IN_COLS // N_DEV)),
    ("w_ff1", (D, D_FF), (D, D_FF // N_DEV), _cols(D_FF // N_DEV)),
    ("w_ff2", (D_FF, D), (D_FF // N_DEV, D), _rows(D_FF // N_DEV)),
    ("w_conv_out", (D, D), (D // N_DEV, D), _rows(D // N_DEV)),
    ("w_pool_out", (D, D), (D // N_DEV, D), _rows(D // N_DEV)),
    ("w_o", (D, D), (D // N_DEV, D), _rows(D // N_DEV)),
    ("pool_w", (4, PG, PG), (4, PG // N_DEV, PG), _pool_rows),
    ("dw_kernel", (N_DEV, TAPS, D // N_DEV), (TAPS, D // N_DEV), _lead),
)
N_SHARDED = len(SHARDED)


SHARD_AT = {name: at for name, _, _, at in SHARDED}
HBM = pl.BlockSpec(memory_space=pltpu.HBM)
SEM = pl.BlockSpec(memory_space=pltpu.SEMAPHORE)
EFFECT = pltpu.SideEffectType.DATAFLOW_SIDE_EFFECTING


def _in_hbm(a):
    return pltpu.with_memory_space_constraint(a, pltpu.HBM)


def _stage_shards(shards):
    dtypes = [F32 if name == "dw_kernel" else BF16 for name, *_ in SHARDED]

    def body(*refs):
        ins = refs[:N_SHARDED]
        stage = refs[N_SHARDED:2 * N_SHARDED]
        fulls = refs[2 * N_SHARDED:3 * N_SHARDED]
        sems = refs[3 * N_SHARDED]
        me = _my_slot()
        copies = []
        for a, (_, _, _, at) in enumerate(SHARDED):
            stage[a][...] = ins[a][...].astype(dtypes[a])
            cp = pltpu.make_async_copy(stage[a], at(fulls[a], me), sems.at[a])
            cp.start()
            copies.append(cp)
        for cp in copies:
            cp.wait()

    outs = pl.pallas_call(
        body, name="stage_shards",
        out_shape=[jax.ShapeDtypeStruct(shard, dt) for (_, _, shard, _), dt in zip(SHARDED, dtypes)]
        + [jax.ShapeDtypeStruct(full, dt) for (_, full, _, _), dt in zip(SHARDED, dtypes)],
        in_specs=[pl.BlockSpec(memory_space=pltpu.VMEM)] * N_SHARDED,
        out_specs=[pl.BlockSpec(memory_space=pltpu.VMEM)] * N_SHARDED + [pl.BlockSpec(memory_space=pl.ANY)] * N_SHARDED,
        scratch_shapes=[pltpu.SemaphoreType.DMA((N_SHARDED,))],
        compiler_params=pltpu.CompilerParams(vmem_limit_bytes=40 * MIB),
    )(*shards)
    return outs[:N_SHARDED], outs[N_SHARDED:]


def _sem_index(k, mask):
    return k * (N_DEV - 1) + mask - 1


def _exchange_start(name, groups, src_of, dst_of):
    sizes = [len(g[0]) for g in groups]
    names = [nm for g in groups for nm in g[0]]
    srcs = [s for g in groups for s in g[1]]
    lands = [l for g in groups for l in g[2]]
    n, n_g = len(names), len(groups)

    def body(*refs):
        src_refs, land_refs = refs[:n], refs[n:2 * n]
        sems = refs[2 * n:2 * n + 2 * n_g]
        token = refs[-1]
        me = _my_slot()
        first = 0
        for g, size in enumerate(sizes):
            for mask in range(1, N_DEV):
                for k in range(size):
                    a = first + k
                    pltpu.make_async_remote_copy(
                        src_ref=src_of(names[a], src_refs[a], me, mask), dst_ref=dst_of(names[a], land_refs[a], me, mask),
                        send_sem=sems[2 * g].at[_sem_index(k, mask)], recv_sem=sems[2 * g + 1].at[_sem_index(k, mask)],
                        device_id=_peer(mask), device_id_type=MESH).start()
            first += size
        token[...] = jnp.zeros_like(token)

    sem_shapes = [pltpu.SemaphoreType.DMA((size * (N_DEV - 1),)) for size in sizes for _ in range(2)]
    outs = pl.pallas_call(
        body, name=name,
        out_shape=sem_shapes + [pltpu.HBM(a.shape, a.dtype) for a in srcs + lands] + [jax.ShapeDtypeStruct((8, 128), F32)],
        in_specs=[HBM] * (2 * n),
        out_specs=[SEM] * (2 * n_g) + [HBM] * (2 * n) + [pl.BlockSpec(memory_space=pltpu.VMEM)],
        input_output_aliases={k: 2 * n_g + k for k in range(2 * n)},
        compiler_params=pltpu.CompilerParams(has_side_effects=EFFECT),
    )(*[_in_hbm(a) for a in srcs + lands])
    sems, thru, token = outs[:2 * n_g], outs[2 * n_g:-1], outs[-1]
    handles, a = [], 0
    for g, size in enumerate(sizes):
        handles.append((groups[g][0], sems[2 * g], sems[2 * g + 1], thru[a:a + size], thru[n + a:n + a + size]))
        a += size
    return handles, token


def _exchange_wait(name, handle, src_of, dst_of, after):
    names, send_sems, recv_sems, srcs, lands = handle
    n = len(names)

    def body(*refs):
        src_refs, land_refs = refs[:n], refs[n:2 * n]
        send_ref, recv_ref = refs[2 * n], refs[2 * n + 1]
        me = _my_slot()
        for mask in range(1, N_DEV):
            for a in range(n):
                cp = pltpu.make_async_remote_copy(
                    src_ref=src_of(names[a], src_refs[a], me, mask), dst_ref=dst_of(names[a], land_refs[a], me ^ mask, mask),
                    send_sem=send_ref.at[_sem_index(a, mask)], recv_sem=recv_ref.at[_sem_index(a, mask)],
                    device_id=_peer(mask), device_id_type=MESH)
                cp.wait_send()
                cp.wait_recv()

    outs = pl.pallas_call(
        body, name=name,
        out_shape=[pltpu.HBM(a.shape, a.dtype) for a in list(srcs) + list(lands)],
        in_specs=[HBM] * (2 * n) + [SEM, SEM, pl.BlockSpec(memory_space=pl.ANY)],
        out_specs=[HBM] * (2 * n),
        input_output_aliases={k: k for k in range(2 * n)},
        compiler_params=pltpu.CompilerParams(has_side_effects=EFFECT),
    )(*srcs, *lands, send_sems, recv_sems, after)
    return outs[:n], outs[n:]


def _gather_src(name, ref, me, mask):
    return ref


def _gather_dst(name, ref, origin, mask):
    return SHARD_AT[name](ref, origin)


def _scatter_src(name, ref, me, mask):
    return SHARD_AT[name](ref, me ^ mask)


def _scatter_dst(name, ref, origin, mask):
    return ref.at[mask - 1]


def _vec_exchange(vec_grads):
    def body(vec, vec_out, send_sems, recv_sems, local_sem):
        me = _my_slot()
        local = pltpu.make_async_copy(vec, vec_out.at[me], local_sem)
        local.start()
        sends = []
        for mask in range(1, N_DEV):
            cp = pltpu.make_async_remote_copy(
                src_ref=vec, dst_ref=vec_out.at[me], send_sem=send_sems.at[mask - 1],
                recv_sem=recv_sems.at[mask - 1], device_id=_peer(mask), device_id_type=MESH)
            cp.start()
            sends.append(cp)
        for mask in range(1, N_DEV):
            pltpu.make_async_remote_copy(
                src_ref=vec, dst_ref=vec_out.at[me ^ mask], send_sem=send_sems.at[mask - 1],
                recv_sem=recv_sems.at[mask - 1], device_id=_peer(mask), device_id_type=MESH).wait_recv()
        for cp in sends:
            cp.wait_send()
        local.wait()

    return pl.pallas_call(
        body, name="vec_exchange",
        out_shape=jax.ShapeDtypeStruct((N_DEV,) + vec_grads.shape, F32),
        in_specs=[pl.BlockSpec(memory_space=pl.ANY)],
        out_specs=pl.BlockSpec(memory_space=pl.ANY),
        scratch_shapes=[pltpu.SemaphoreType.DMA((N_DEV - 1,)), pltpu.SemaphoreType.DMA((N_DEV - 1,)),
                        pltpu.SemaphoreType.DMA],
    )(vec_grads)


def _adamw_update(g, w_ref, m_ref, v_ref, g_out, d_out, m_out, v_out):
    m_new = ADAM_B1 * m_ref[...] + (1.0 - ADAM_B1) * g
    v_new = ADAM_B2 * v_ref[...] + (1.0 - ADAM_B2) * (g * g)
    m_hat = m_new / (1.0 - ADAM_B1 ** ADAM_STEP)
    v_hat = v_new / (1.0 - ADAM_B2 ** ADAM_STEP)
    g_out[...] = g
    d_out[...] = -ADAM_LR * (m_hat / (jnp.sqrt(v_hat) + ADAM_EPS) + ADAM_WD * w_ref[...])
    m_out[...] = m_new
    v_out[...] = v_new


ADAMW_ROWS = 256


def _adamw(name, own, parts, w, m, v, slot):
    shard = w.shape
    if name in ("w_in", "w_ff1"):
        tr = ADAMW_ROWS
        grid = (shard[0] // tr,)
        own_spec = pl.BlockSpec((tr, shard[1]), lambda i, s: (i, s[0]))
        blk = pl.BlockSpec((tr, shard[1]), lambda i, s: (i, 0))
        parts_spec = pl.BlockSpec((N_DEV - 1, tr, shard[1]), lambda i, s: (0, i, 0))
    elif name == "pool_w":
        grid = (shard[0],)
        own_spec = pl.BlockSpec((None,) + shard[1:], lambda g, s: (g, s[0], 0))
        blk = pl.BlockSpec((None,) + shard[1:], lambda g, s: (g, 0, 0))
        parts_spec = pl.BlockSpec((N_DEV - 1, None) + shard[1:], lambda g, s: (0, g, 0, 0))
    elif name == "dw_kernel":
        grid = (1,)
        own_spec = pl.BlockSpec((None,) + shard, lambda i, s: (s[0], 0, 0))
        blk = pl.BlockSpec(shard, lambda i, s: (0, 0))
        parts_spec = pl.BlockSpec((N_DEV - 1,) + shard, lambda i, s: (0, 0, 0))
    else:
        tr = min(ADAMW_ROWS, shard[0])
        grid = (shard[0] // tr,)
        own_spec = pl.BlockSpec((tr, shard[1]), lambda i, s: (s[0] * grid[0] + i, 0))
        blk = pl.BlockSpec((tr, shard[1]), lambda i, s: (i, 0))
        parts_spec = pl.BlockSpec((N_DEV - 1, tr, shard[1]), lambda i, s: (0, i, 0))

    def body(slot_ref, own_ref, p_ref, w_ref, m_ref, v_ref, g_out, d_out, m_out, v_out):
        g = own_ref[...].astype(F32)
        for k in range(N_DEV - 1):
            g = g + p_ref[k].astype(F32)
        _adamw_update(g, w_ref, m_ref, v_ref, g_out, d_out, m_out, v_out)

    return pl.pallas_call(
        body, name="adamw_" + name,
        grid_spec=pltpu.PrefetchScalarGridSpec(
            num_scalar_prefetch=1, grid=grid, in_specs=[own_spec, parts_spec, blk, blk, blk], out_specs=[blk] * 4),
        out_shape=[jax.ShapeDtypeStruct(shard, F32)] * 4,
        compiler_params=_params(("arbitrary",), 32),
    )(slot, own, parts, w, m, v)


def _adamw_vectors(parts, w, m, v):
    def body(p_ref, w_ref, m_ref, v_ref, g_out, d_out, m_out, v_out):
        g = p_ref[0]
        for s in range(1, N_DEV):
            g = g + p_ref[s]
        _adamw_update(g, w_ref, m_ref, v_ref, g_out, d_out, m_out, v_out)

    return pl.pallas_call(
        body, name="adamw_vectors",
        out_shape=[jax.ShapeDtypeStruct(w.shape, F32)] * 4,
    )(parts, w, m, v)


def _in_proj_fwd(x, g_pre, w_in):
    seq = x.shape[0]
    tm, tn = 1024, IN_COLS // N_DEV

    def body(x_ref, g_ref, w_ref, proj_ref, u_ref):
        @pl.when(pl.program_id(1) == 0)
        def _():
            xf = x_ref[...]
            u_ref[...] = (xf * _rms_scale(xf) * g_ref[...]).astype(BF16)
        proj_ref[...] = _dot(u_ref[...], w_ref[...])

    return pl.pallas_call(
        body, name="in_proj_fwd", grid=(seq // tm, N_DEV),
        in_specs=[pl.BlockSpec((tm, D), lambda i, j: (i, 0)), pl.BlockSpec((1, D), lambda i, j: (0, 0)),
                  pl.BlockSpec((D, tn), lambda i, j: (0, j))],
        out_specs=[pl.BlockSpec((tm, tn), lambda i, j: (i, j)), pl.BlockSpec((tm, D), lambda i, j: (i, 0))],
        out_shape=[jax.ShapeDtypeStruct((seq, IN_COLS), F32), jax.ShapeDtypeStruct((seq, D), BF16)],
        compiler_params=_params(("arbitrary", "arbitrary"), 40),
    )(x, g_pre, w_in)


CONV_TM = 256
CONV_RS = 64


def _layer_norm_parts(cv):
    mu = jnp.mean(cv, axis=-1, keepdims=True)
    cen = cv - mu
    rstd = lax.rsqrt(jnp.mean(cen * cen, axis=-1, keepdims=True) + LN_EPS)
    return cen * rstd, rstd


def _conv_fwd(proj, dw, dw_bias, ln_g, ln_b, w_conv_out):
    seq = proj.shape[0]
    tm = CONV_TM

    def body(a_ref, gate_ref, dw_ref, bias_ref, lg_ref, lb_ref, w_ref, cv_ref, y_ref, ug):
        i = pl.program_id(0)

        @pl.when(i == 0)
        def _():
            ug[0:HALO, :] = jnp.zeros((HALO, D), F32)

        @pl.when(i > 0)
        def _():
            ug[0:HALO, :] = ug[tm:tm + HALO, :]
        ug[HALO:HALO + tm, :] = a_ref[...] * _sigmoid(gate_ref[...])

        def channel_block(cb, carry):
            lanes = pl.ds(pl.multiple_of(cb * 128, 128), 128)
            for r0 in range(0, tm, CONV_RS):
                acc = jnp.broadcast_to(bias_ref[:, lanes], (CONV_RS, 128))
                for k in range(TAPS):
                    acc = acc + dw_ref[cb, k:k + 1, :] * ug[pl.ds(r0 + HALO - (TAPS - 1) + k, CONV_RS), lanes]
                cv_ref[pl.ds(r0, CONV_RS), lanes] = acc
            return carry
        lax.fori_loop(0, D // 128, channel_block, 0)

        n, _ = _layer_norm_parts(cv_ref[...])
        ln = n * lg_ref[...] + lb_ref[...]
        y_ref[...] = _dot((ln * _sigmoid(ln)).astype(BF16), w_ref[...])

    vec = pl.BlockSpec((1, D), lambda i: (0, 0))
    tile = pl.BlockSpec((tm, D), lambda i: (i, 0))
    return pl.pallas_call(
        body, name="conv_fwd", grid=(seq // tm,),
        in_specs=[pl.BlockSpec((tm, D), lambda i: (i, 0)), pl.BlockSpec((tm, D), lambda i: (i, 1)),
                  pl.BlockSpec((N_DEV, TAPS, 128), lambda i: (0, 0, 0)), vec, vec, vec,
                  pl.BlockSpec((D, D), lambda i: (0, 0))],
        out_specs=[tile, tile],
        out_shape=[jax.ShapeDtypeStruct((seq, D), F32)] * 2,
        scratch_shapes=[pltpu.VMEM((HALO + tm, D), F32)],
        compiler_params=_params(("arbitrary",), 32),
    )(proj, proj, dw, dw_bias, ln_g, ln_b, w_conv_out)


def _pool_counts(tile_index, tm, window):
    t = tile_index * tm + lax.broadcasted_iota(jnp.int32, (tm, 1), 0)
    return 1.0 / jnp.minimum(t + 1, window).astype(F32)


def _pool_merge_fwd(proj, y_conv, x, pool_w, pool_scale, w_pool_out, w_o, g_post):
    seq = proj.shape[0]
    tm = CONV_TM

    def body(p_ref, gc_ref, gp_ref, yc_ref, x_ref, pw_ref, ps_ref, wpo_ref, wo_ref, g_ref,
             z_ref, zl_ref, yp_ref, mg_ref, o_ref, h1_ref, pbuf):
        i = pl.program_id(0)

        @pl.when(i == 0)
        def _():
            pbuf[0:HALO, :] = jnp.zeros((HALO, D), F32)

        @pl.when(i > 0)
        def _():
            pbuf[0:HALO, :] = pbuf[tm:tm + HALO, :]
        pbuf[HALO:HALO + tm, :] = p_ref[...]

        for g, window in enumerate(POOL_WINDOWS):
            lanes = pl.ds(g * PG, PG)
            acc = pbuf[pl.ds(HALO, tm), lanes]
            for j in range(1, window):
                acc = acc + pbuf[pl.ds(HALO - j, tm), lanes]
            zg = acc * _pool_counts(i, tm, window) - pbuf[pl.ds(HALO, tm), lanes]
            z_ref[:, lanes] = zg.astype(BF16)
            zl_ref[:, lanes] = _dot(zg.astype(BF16), pw_ref[g])
        zl = zl_ref[...]
        y_pool = _dot((zl * ps_ref[...]).astype(BF16), wpo_ref[...])
        yp_ref[...] = y_pool
        merged = (_sigmoid(gc_ref[...]) * yc_ref[...] + _sigmoid(gp_ref[...]) * y_pool).astype(BF16)
        mg_ref[...] = merged
        o = _dot(merged, wo_ref[...])
        o_ref[...] = o
        h1_ref[...] = x_ref[...] + o * _rms_scale(o) * g_ref[...]

    vec = pl.BlockSpec((1, D), lambda i: (0, 0))
    tile = pl.BlockSpec((tm, D), lambda i: (i, 0))
    mat = pl.BlockSpec((D, D), lambda i: (0, 0))
    return pl.pallas_call(
        body, name="pool_merge_fwd", grid=(seq // tm,),
        in_specs=[pl.BlockSpec((tm, D), lambda i: (i, 2)), pl.BlockSpec((tm, D), lambda i: (i, 3)),
                  pl.BlockSpec((tm, D), lambda i: (i, 4)), tile, tile,
                  pl.BlockSpec((4, PG, PG), lambda i: (0, 0, 0)), vec, mat, mat, vec],
        out_specs=[tile] * 6,
        out_shape=[jax.ShapeDtypeStruct((seq, D), dt) for dt in (BF16, F32, F32, BF16, F32, F32)],
        scratch_shapes=[pltpu.VMEM((HALO + tm, D), F32)],
        compiler_params=_params(("arbitrary",), 48),
    )(proj, proj, proj, y_conv, x, pool_w, pool_scale, w_pool_out, w_o, g_post)


def _mlp_fwd(h1, g_pre, w_ff1, w_ff2, g_post, target):
    seq = h1.shape[0]
    tm, tf = 1024, D_FF // N_DEV
    n_f = D_FF // tf

    def body(h1_ref, gpre_ref, w1_ref, w2_ref, gpost_ref, tgt_ref, v_ref, dm_ref, dh2_ref, sse_ref, ggrad_ref, macc):
        i, j = pl.program_id(0), pl.program_id(1)

        @pl.when(j == 0)
        def _():
            h = h1_ref[...]
            v_ref[...] = (h * _rms_scale(h) * gpre_ref[...]).astype(BF16)
        f = jnp.maximum(_dot(v_ref[...], w1_ref[...]), 0.0)
        part = _dot((f * f).astype(BF16), w2_ref[...])

        @pl.when(j == 0)
        def _():
            macc[...] = part

        @pl.when(j > 0)
        def _():
            macc[...] += part

        @pl.when(j == n_f - 1)
        def _():
            mo = macc[...]
            err = h1_ref[...] + mo * _rms_scale(mo) * gpost_ref[...] - tgt_ref[...]
            dh2 = err * (1.0 / D)
            dh2_ref[...] = dh2
            dm, ggrad = _rms_bwd(dh2, mo, gpost_ref[...])
            dm_ref[...] = dm.astype(BF16)
            _acc_out(ggrad_ref, i == 0, ggrad)
            _acc_out(sse_ref, i == 0, jnp.sum(jnp.sum(err * err, axis=1, keepdims=True), axis=0, keepdims=True))

    vec = pl.BlockSpec((1, D), lambda i, j: (0, 0))
    tile = pl.BlockSpec((tm, D), lambda i, j: (i, 0))
    return pl.pallas_call(
        body, name="mlp_fwd", grid=(seq // tm, n_f),
        in_specs=[tile, vec, pl.BlockSpec((D, tf), lambda i, j: (0, j)), pl.BlockSpec((tf, D), lambda i, j: (j, 0)), vec, tile],
        out_specs=[tile, tile, tile, pl.BlockSpec((1, 1), lambda i, j: (0, 0)), vec],
        out_shape=[jax.ShapeDtypeStruct((seq, D), BF16), jax.ShapeDtypeStruct((seq, D), BF16),
                   jax.ShapeDtypeStruct((seq, D), F32), jax.ShapeDtypeStruct((1, 1), F32),
                   jax.ShapeDtypeStruct((1, D), F32)],
        scratch_shapes=[pltpu.VMEM((tm, D), F32)],
        compiler_params=_params(("arbitrary", "arbitrary"), 56),
    )(h1, g_pre, w_ff1, w_ff2, g_post, target)


def _mlp_bwd(v, dm, w_ff1, w_ff2):
    seq = v.shape[0]
    tm, tf = 512, D_FF // N_DEV
    n_t = seq // tm

    def body(v_ref, dm_ref, w1_ref, w2_ref, dv_hbm, g1_ref, g2_ref, dv_acc, g1_acc, g2_acc, sem):
        j, i = pl.program_id(0), pl.program_id(1)
        vt, dmt = v_ref[...], dm_ref[...]
        f = jnp.maximum(_dot(vt, w1_ref[...]), 0.0)
        df = (_dot_nt(dmt, w2_ref[...]) * (2.0 * f)).astype(BF16)
        rows = pl.ds(pl.multiple_of(i * tm, tm), tm)
        dv_part = _dot_nt(df, w1_ref[...])

        @pl.when(j == 0)
        def _():
            dv_acc[rows, :] = dv_part

        @pl.when(j > 0)
        def _():
            dv_acc[rows, :] += dv_part
        g1_part = _dot_tn(vt, df)
        g2_part = _dot_tn((f * f).astype(BF16), dmt)

        @pl.when(i == 0)
        def _():
            g1_acc[...] = g1_part
            g2_acc[...] = g2_part

        @pl.when(i > 0)
        def _():
            g1_acc[...] += g1_part
            g2_acc[...] += g2_part

        @pl.when(i == n_t - 1)
        def _():
            g1_ref[...] = g1_acc[...].astype(BF16)
            g2_ref[...] = g2_acc[...].astype(BF16)

        @pl.when(jnp.logical_and(i == n_t - 1, j == D_FF // tf - 1))
        def _():
            cp = pltpu.make_async_copy(dv_acc, dv_hbm, sem)
            cp.start()
            cp.wait()

    tile = pl.BlockSpec((tm, D), lambda j, i: (i, 0))
    return pl.pallas_call(
        body, name="mlp_bwd", grid=(D_FF // tf, n_t),
        in_specs=[tile, tile, pl.BlockSpec((D, tf), lambda j, i: (0, j)), pl.BlockSpec((tf, D), lambda j, i: (j, 0))],
        out_specs=[pl.BlockSpec(memory_space=pl.ANY), pl.BlockSpec((D, tf), lambda j, i: (0, j)),
                   pl.BlockSpec((tf, D), lambda j, i: (j, 0))],
        out_shape=[jax.ShapeDtypeStruct((seq, D), F32), jax.ShapeDtypeStruct((D, D_FF), BF16),
                   jax.ShapeDtypeStruct((D_FF, D), BF16)],
        scratch_shapes=[pltpu.VMEM((seq, D), F32), pltpu.VMEM((D, tf), F32), pltpu.VMEM((tf, D), F32),
                        pltpu.SemaphoreType.DMA],
        compiler_params=_params(("arbitrary", "arbitrary"), 52),
    )(v, dm, w_ff1, w_ff2)


ANY = pl.BlockSpec(memory_space=pl.ANY)


def _merge_bwd(dh2, dv, h1, g_mlp_pre, o, g_mix_post, w_o, merged, proj, y_conv, y_pool, token):
    seq = dh2.shape[0]
    tm = 256
    n_t = seq // tm

    def body(dh2_ref, dv_ref, h1_ref, gpre_ref, o_ref, gpost_ref, wo_ref, mg_ref, gc_ref, gp_ref, yc_ref, yp_ref, _token,
             dh1_ref, dyc_ref, dyp_ref, dg_ref, gwo_ref, ggpre_ref, ggpost_ref, gwo_acc):
        i = pl.program_id(0)
        dnorm, ggpre = _rms_bwd(dv_ref[...], h1_ref[...], gpre_ref[...])
        dh1 = dh2_ref[...] + dnorm
        dh1_ref[...] = dh1
        do, ggpost = _rms_bwd(dh1, o_ref[...], gpost_ref[...])
        do = do.astype(BF16)
        _acc_out(ggpre_ref, i == 0, ggpre)
        _acc_out(ggpost_ref, i == 0, ggpost)
        _acc_out(gwo_acc, i == 0, _dot_tn(mg_ref[...], do))
        dmerged = _dot_nt(do, wo_ref[...])
        sc, sp = _sigmoid(gc_ref[...]), _sigmoid(gp_ref[...])
        dyc_ref[...] = (dmerged * sc).astype(BF16)
        dyp_ref[...] = (dmerged * sp).astype(BF16)
        dg_ref[:, 0:D] = (dmerged * yc_ref[...] * (sc * (1.0 - sc))).astype(BF16)
        dg_ref[:, D:2 * D] = (dmerged * yp_ref[...] * (sp * (1.0 - sp))).astype(BF16)

        @pl.when(i == n_t - 1)
        def _():
            gwo_ref[...] = gwo_acc[...].astype(BF16)

    vec = pl.BlockSpec((1, D), lambda i: (0, 0))
    tile = pl.BlockSpec((tm, D), lambda i: (i, 0))
    mat = pl.BlockSpec((D, D), lambda i: (0, 0))
    return pl.pallas_call(
        body, name="merge_bwd", grid=(n_t,),
        in_specs=[tile, tile, tile, vec, tile, vec, mat, tile,
                  pl.BlockSpec((tm, D), lambda i: (i, 3)), pl.BlockSpec((tm, D), lambda i: (i, 4)), tile, tile, ANY],
        out_specs=[tile, tile, tile, pl.BlockSpec((tm, 2 * D), lambda i: (i, 0)), mat, vec, vec],
        out_shape=[jax.ShapeDtypeStruct((seq, D), F32), jax.ShapeDtypeStruct((seq, D), BF16),
                   jax.ShapeDtypeStruct((seq, D), BF16), jax.ShapeDtypeStruct((seq, 2 * D), BF16),
                   jax.ShapeDtypeStruct((D, D), BF16), jax.ShapeDtypeStruct((1, D), F32),
                   jax.ShapeDtypeStruct((1, D), F32)],
        scratch_shapes=[pltpu.VMEM((D, D), F32)],
        compiler_params=_params(("arbitrary",), 48),
    )(dh2, dv, h1, g_mlp_pre, o, g_mix_post, w_o, merged, proj, proj, y_conv, y_pool, token)


def _pool_bwd(dy_pool, zl, z, pool_w, pool_scale, w_pool_out):
    seq = dy_pool.shape[0]
    tm = CONV_TM
    n_t = seq // tm

    def body(dy_ref, zl_ref, z_ref, pw_ref, ps_ref, wpo_ref, dp_ref, gwpo_ref, gpw_ref, gps_ref, qbuf, gwpo_acc, gpw_acc):
        i = pl.program_id(0)
        tile_index = n_t - 1 - i
        first = i == 0
        dy = dy_ref[...]
        zl = zl_ref[...]
        dzs = _dot_nt(dy, wpo_ref[...])
        _acc_out(gwpo_acc, first, _dot_tn((zl * ps_ref[...]).astype(BF16), dy))
        _acc_out(gps_ref, first, jnp.sum(dzs * zl, axis=0, keepdims=True))
        dzl = (dzs * ps_ref[...]).astype(BF16)

        @pl.when(first)
        def _():
            qbuf[tm:tm + HALO, :] = jnp.zeros((HALO, D), F32)

        @pl.when(jnp.logical_not(first))
        def _():
            qbuf[tm:tm + HALO, :] = qbuf[0:HALO, :]

        dzs_list = []
        for g, window in enumerate(POOL_WINDOWS):
            lanes = pl.ds(g * PG, PG)
            dzl_g = dzl[:, g * PG:(g + 1) * PG]
            dz = _dot_nt(dzl_g, pw_ref[g])
            _acc_out(gpw_acc.at[g], first, _dot_tn(z_ref[:, lanes], dzl_g))
            qbuf[pl.ds(0, tm), lanes] = dz * _pool_counts(tile_index, tm, window)
            dzs_list.append(dz)
        for g, window in enumerate(POOL_WINDOWS):
            lanes = pl.ds(g * PG, PG)
            acc = qbuf[pl.ds(0, tm), lanes]
            for j in range(1, window):
                acc = acc + qbuf[pl.ds(j, tm), lanes]
            dp_ref[:, lanes] = (acc - dzs_list[g]).astype(BF16)

        @pl.when(i == n_t - 1)
        def _():
            gwpo_ref[...] = gwpo_acc[...].astype(BF16)
            gpw_ref[...] = gpw_acc[...].astype(BF16)

    vec = pl.BlockSpec((1, D), lambda i: (0, 0))
    tile = pl.BlockSpec((tm, D), lambda i: (n_t - 1 - i, 0))
    mat = pl.BlockSpec((D, D), lambda i: (0, 0))
    pw = pl.BlockSpec((4, PG, PG), lambda i: (0, 0, 0))
    return pl.pallas_call(
        body, name="pool_bwd", grid=(n_t,),
        in_specs=[tile, tile, tile, pw, vec, mat],
        out_specs=[tile, mat, pw, vec],
        out_shape=[jax.ShapeDtypeStruct((seq, D), BF16), jax.ShapeDtypeStruct((D, D), BF16),
                   jax.ShapeDtypeStruct((4, PG, PG), BF16), jax.ShapeDtypeStruct((1, D), F32)],
        scratch_shapes=[pltpu.VMEM((tm + HALO, D), F32), pltpu.VMEM((D, D), F32), pltpu.VMEM((4, PG, PG), F32)],
        compiler_params=_params(("arbitrary",), 40),
    )(dy_pool, zl, z, pool_w, pool_scale, w_pool_out)


def _conv_bwd(dy_conv, cv, proj, dw, ln_g, ln_b, w_conv_out):
    seq = dy_conv.shape[0]
    tm = CONV_TM
    n_t = seq // tm
    halo_blocks = tm // HALO

    def body(dy_ref, cv_ref, a_ref, gate_ref, ah_ref, gh_ref, dw_ref, lg_ref, lb_ref, w_ref,
             dglu_ref, gw_ref, gdw_ref, gbias_ref, glg_ref, glb_ref, ug, dcv, dug, gw_acc):
        i = pl.program_id(0)
        tile_index = n_t - 1 - i
        first = i == 0
        dy = dy_ref[...]
        n, rstd = _layer_norm_parts(cv_ref[...])
        ln = n * lg_ref[...] + lb_ref[...]
        sg = _sigmoid(ln)
        _acc_out(gw_acc, first, _dot_tn((ln * sg).astype(BF16), dy))
        dln = _dot_nt(dy, w_ref[...]) * (sg * (1.0 + ln * (1.0 - sg)))
        _acc_out(glg_ref, first, jnp.sum(dln * n, axis=0, keepdims=True))
        _acc_out(glb_ref, first, jnp.sum(dln, axis=0, keepdims=True))
        dn = dln * lg_ref[...]
        dcv_tile = rstd * (dn - jnp.mean(dn, axis=-1, keepdims=True) - n * jnp.mean(dn * n, axis=-1, keepdims=True))
        _acc_out(gbias_ref, first, jnp.sum(dcv_tile, axis=0, keepdims=True))

        @pl.when(first)
        def _():
            dcv[tm:tm + HALO, :] = jnp.zeros((HALO, D), F32)

        @pl.when(jnp.logical_not(first))
        def _():
            dcv[tm:tm + HALO, :] = dcv[0:HALO, :]
        dcv[0:tm, :] = dcv_tile

        a, gate = a_ref[...], gate_ref[...]
        sgate = _sigmoid(gate)
        ug[HALO:HALO + tm, :] = a * sgate
        before = jnp.where(tile_index > 0, 1.0, 0.0)
        ug[0:HALO, :] = ah_ref[...] * _sigmoid(gh_ref[...]) * before

        @pl.when(first)
        def _():
            gdw_ref[...] = jnp.zeros((N_DEV, TAPS + 1, 128), F32)

        def channel_block(cb, carry):
            lanes = pl.ds(pl.multiple_of(cb * 128, 128), 128)
            for r0 in range(0, tm, CONV_RS):
                acc = jnp.zeros((CONV_RS, 128), F32)
                for k in range(TAPS):
                    acc = acc + dw_ref[cb, k:k + 1, :] * dcv[pl.ds(r0 + (TAPS - 1) - k, CONV_RS), lanes]
                dug[pl.ds(r0, CONV_RS), lanes] = acc
            for k in range(TAPS):
                part = jnp.zeros((8, 128), F32)
                for r0 in range(0, tm, CONV_RS):
                    prod = dcv[pl.ds(r0, CONV_RS), lanes] * ug[pl.ds(r0 + HALO - (TAPS - 1) + k, CONV_RS), lanes]
                    part = part + jnp.sum(prod.reshape(CONV_RS // 8, 8, 128), axis=0)
                gdw_ref[cb, k:k + 1, :] += jnp.sum(part, axis=0, keepdims=True)
            return carry
        lax.fori_loop(0, D // 128, channel_block, 0)

        d_ug = dug[...]
        dglu_ref[:, 0:D] = (d_ug * sgate).astype(BF16)
        dglu_ref[:, D:2 * D] = (d_ug * a * (sgate * (1.0 - sgate))).astype(BF16)

        @pl.when(i == n_t - 1)
        def _():
            gw_ref[...] = gw_acc[...].astype(BF16)

    def halo_index(col):
        return lambda i: (jnp.maximum((n_t - 1 - i) * halo_blocks - 1, 0), col)

    vec = pl.BlockSpec((1, D), lambda i: (0, 0))
    tile = pl.BlockSpec((tm, D), lambda i: (n_t - 1 - i, 0))
    mat = pl.BlockSpec((D, D), lambda i: (0, 0))
    dwspec = pl.BlockSpec((N_DEV, TAPS, 128), lambda i: (0, 0, 0))
    return pl.pallas_call(
        body, name="conv_bwd", grid=(n_t,),
        in_specs=[tile, tile, pl.BlockSpec((tm, D), lambda i: (n_t - 1 - i, 0)), pl.BlockSpec((tm, D), lambda i: (n_t - 1 - i, 1)),
                  pl.BlockSpec((HALO, D), halo_index(0)), pl.BlockSpec((HALO, D), halo_index(1)), dwspec, vec, vec, mat],
        out_specs=[pl.BlockSpec((tm, 2 * D), lambda i: (n_t - 1 - i, 0)), mat,
                   pl.BlockSpec((N_DEV, TAPS + 1, 128), lambda i: (0, 0, 0)), vec, vec, vec],
        out_shape=[jax.ShapeDtypeStruct((seq, 2 * D), BF16), jax.ShapeDtypeStruct((D, D), BF16),
                   jax.ShapeDtypeStruct((N_DEV, TAPS + 1, 128), F32), jax.ShapeDtypeStruct((1, D), F32),
                   jax.ShapeDtypeStruct((1, D), F32), jax.ShapeDtypeStruct((1, D), F32)],
        scratch_shapes=[pltpu.VMEM((HALO + tm, D), F32), pltpu.VMEM((tm + HALO, D), F32), pltpu.VMEM((tm, D), F32),
                        pltpu.VMEM((D, D), F32)],
        compiler_params=_params(("arbitrary",), 48),
    )(dy_conv, cv, proj, proj, proj, proj, dw, ln_g, ln_b, w_conv_out)


def _in_proj_bwd_x(d_glu, dp, dgates, w_in, x, g_pre, dh1, token):
    seq = x.shape[0]
    tm = 256

    def body(dglu_ref, dp_ref, dg_ref, w_ref, x_ref, g_ref, dh1_ref, _token, dx_ref, gg_ref):
        du = _dot_nt(dglu_ref[...], w_ref[:, 0:2 * D])
        du += _dot_nt(dp_ref[...], w_ref[:, 2 * D:3 * D])
        du += _dot_nt(dg_ref[...], w_ref[:, 3 * D:5 * D])
        dnorm, gg = _rms_bwd(du, x_ref[...], g_ref[...])
        dx_ref[...] = dh1_ref[...] + dnorm
        _acc_out(gg_ref, pl.program_id(0) == 0, gg)

    vec = pl.BlockSpec((1, D), lambda i: (0, 0))
    tile = pl.BlockSpec((tm, D), lambda i: (i, 0))
    wide = pl.BlockSpec((tm, 2 * D), lambda i: (i, 0))
    return pl.pallas_call(
        body, name="in_proj_bwd_x", grid=(seq // tm,),
        in_specs=[wide, tile, wide, pl.BlockSpec((D, IN_COLS), lambda i: (0, 0)), tile, vec, tile, ANY],
        out_specs=[tile, vec],
        out_shape=[jax.ShapeDtypeStruct((seq, D), F32), jax.ShapeDtypeStruct((1, D), F32)],
        compiler_params=_params(("arbitrary",), 48),
    )(d_glu, dp, dgates, w_in, x, g_pre, dh1, token)


def _in_proj_bwd_w(u, d_glu, dp, dgates, token):
    seq = u.shape[0]
    tm = 512
    n_t = seq // tm

    def body(u_ref, dglu_ref, dp_ref, dg_ref, _token, out_ref, acc):
        b, i = pl.program_id(0), pl.program_id(1)
        ut = u_ref[...]

        def add(d_ref):
            _acc_out(acc, i == 0, _dot_tn(ut, d_ref[...]))

        pl.when(b < 2)(lambda: add(dglu_ref))
        pl.when(b == 2)(lambda: add(dp_ref))
        pl.when(b > 2)(lambda: add(dg_ref))

        @pl.when(i == n_t - 1)
        def _():
            out_ref[...] = acc[...].astype(BF16)

    return pl.pallas_call(
        body, name="in_proj_bwd_w", grid=(IN_COLS // D, n_t),
        in_specs=[pl.BlockSpec((tm, D), lambda b, i: (i, 0)),
                  pl.BlockSpec((tm, D), lambda b, i: (jnp.where(b < 2, i, 0), jnp.minimum(b, 1))),
                  pl.BlockSpec((tm, D), lambda b, i: (jnp.where(b == 2, i, 0), 0)),
                  pl.BlockSpec((tm, D), lambda b, i: (jnp.where(b > 2, i, 0), jnp.maximum(b - 3, 0))), ANY],
        out_specs=pl.BlockSpec((D, D), lambda b, i: (0, b)),
        out_shape=jax.ShapeDtypeStruct((D, IN_COLS), BF16),
        scratch_shapes=[pltpu.VMEM((D, D), F32)],
        compiler_params=_params(("arbitrary", "arbitrary"), 40),
    )(u, d_glu, dp, dgates, token)


VEC_NAMES = ("mix_pre_g", "dw_bias", "conv_ln_g", "conv_ln_b", "pool_scale", "mix_post_g", "mlp_pre_g", "mlp_post_g")
WEIGHT_ORDER = ("mix_pre_g", "w_in", "dw_kernel", "dw_bias", "conv_ln_g", "conv_ln_b", "w_conv_out", "pool_w",
                "pool_scale", "w_pool_out", "w_o", "mix_post_g", "mlp_pre_g", "w_ff1", "w_ff2", "mlp_post_g")


def _step(x, loss_target, w, m, v):
    row = lambda a: a.reshape(1, D)
    names = [s[0] for s in SHARDED]

    stage, seeded = _stage_shards([w[n] for n in names])
    stage, seeded = dict(zip(names, stage)), dict(zip(names, seeded))
    gather_groups = (("w_in",), ("w_conv_out", "dw_kernel"), ("pool_w", "w_pool_out", "w_o"), ("w_ff1", "w_ff2"))
    handles, token = _exchange_start(
        "gather_start", [(g, [stage[n] for n in g], [seeded[n] for n in g]) for g in gather_groups], _gather_src, _gather_dst)
    full = {}

    def gathered(k, after):
        _, arrays = _exchange_wait("gather_wait_" + gather_groups[k][0], handles[k], _gather_src, _gather_dst, after)
        full.update(zip(gather_groups[k], arrays))

    gathered(0, token)
    proj, u = _in_proj_fwd(x, row(w["mix_pre_g"]), full["w_in"])
    gathered(1, proj)
    cv, y_conv = _conv_fwd(proj, full["dw_kernel"], row(w["dw_bias"]), row(w["conv_ln_g"]), row(w["conv_ln_b"]),
                           full["w_conv_out"])
    gathered(2, y_conv)
    z, zl, y_pool, merged, o, h1 = _pool_merge_fwd(proj, y_conv, x, full["pool_w"], row(w["pool_scale"]),
                                                   full["w_pool_out"], full["w_o"], row(w["mix_post_g"]))
    gathered(3, h1)
    vv, dm, dh2, sse, g_mlp_post = _mlp_fwd(h1, row(w["mlp_pre_g"]), full["w_ff1"], full["w_ff2"],
                                            row(w["mlp_post_g"]), loss_target)

    shard_of = {name: shard for name, _, shard, _ in SHARDED}

    def scatter_start(tag, group, grads):
        landings = [lax.empty((N_DEV - 1,) + shard_of[n], g.dtype) for n, g in zip(group, grads)]
        (handle,), tok = _exchange_start("scatter_start_" + tag, [(group, grads, landings)], _scatter_src, _scatter_dst)
        return handle, tok

    dv, g_ff1, g_ff2 = _mlp_bwd(vv, dm, full["w_ff1"], full["w_ff2"])
    h_ff, tok_ff = scatter_start("ff", ("w_ff1", "w_ff2"), [g_ff1, g_ff2])
    dh1, dy_conv, dy_pool, dgates, g_wo, g_mlp_pre, g_mix_post = _merge_bwd(
        dh2, dv, h1, row(w["mlp_pre_g"]), o, row(w["mix_post_g"]), full["w_o"], merged, proj, y_conv, y_pool, tok_ff)
    dp, g_wpo, g_pw, g_pool_scale = _pool_bwd(dy_pool, zl, z, full["pool_w"], row(w["pool_scale"]), full["w_pool_out"])
    d_glu, g_wco, g_dw, g_bias, g_ln_g, g_ln_b = _conv_bwd(dy_conv, cv, proj, full["dw_kernel"], row(w["conv_ln_g"]),
                                                            row(w["conv_ln_b"]), full["w_conv_out"])
    h_mix, tok_mix = scatter_start("mix", ("w_o", "w_pool_out", "pool_w", "w_conv_out", "dw_kernel"),
                                   [g_wo, g_wpo, g_pw, g_wco, g_dw[:, :TAPS, :]])
    g_win = _in_proj_bwd_w(u, d_glu, dp, dgates, tok_mix)
    h_in, tok_in = scatter_start("in", ("w_in",), [g_win])
    grad_x, g_mix_pre = _in_proj_bwd_x(d_glu, dp, dgates, full["w_in"], x, row(w["mix_pre_g"]), dh1, tok_in)

    own, parts = {}, {}
    for tag, handle in (("ff", h_ff), ("mix", h_mix), ("in", h_in)):
        mine, landed = _exchange_wait("scatter_wait_" + tag, handle, _scatter_src, _scatter_dst, grad_x)
        own.update(zip(handle[0], mine))
        parts.update(zip(handle[0], landed))
    vec_grads = jnp.concatenate(
        [jnp.pad(g, ((0, 7), (0, 0))) for g in
         (g_mix_pre, g_bias, g_ln_g, g_ln_b, g_pool_scale, g_mix_post, g_mlp_pre, g_mlp_post)], axis=0)
    vec_parts = _vec_exchange(vec_grads)

    out = {}
    slot = jnp.reshape(_my_slot(), (1,)).astype(jnp.int32)
    for name in names:
        out[name] = _adamw(name, own[name], parts[name], w[name], m[name], v[name], slot)
    stack = lambda d: jnp.stack([d[n] for n in VEC_NAMES], axis=0)
    vec_parts = vec_parts.reshape(N_DEV, 8, 8, D)[:, :, 0, :]
    res = _adamw_vectors(vec_parts, stack(w), stack(m), stack(v))
    for k, name in enumerate(VEC_NAMES):
        out[name] = [r[k] for r in res]

    loss = lax.psum(sse[0, 0] * (0.5 / D), ("x", "y", "c"))
    return loss, grad_x, out


def kernel(x, mix_pre_g, w_in, dw_kernel, dw_bias, conv_ln_g, conv_ln_b, w_conv_out, pool_w, pool_scale, w_pool_out, w_o, mix_post_g, mlp_pre_g, w_ff1, w_ff2, mlp_post_g, loss_target, m_mix_pre_g, m_w_in, m_dw_kernel, m_dw_bias, m_conv_ln_g, m_conv_ln_b, m_w_conv_out, m_pool_w, m_pool_scale, m_w_pool_out, m_w_o, m_mix_post_g, m_mlp_pre_g, m_w_ff1, m_w_ff2, m_mlp_post_g, v_mix_pre_g, v_w_in, v_dw_kernel, v_dw_bias, v_conv_ln_g, v_conv_ln_b, v_w_conv_out, v_pool_w, v_pool_scale, v_w_pool_out, v_w_o, v_mix_post_g, v_mlp_pre_g, v_w_ff1, v_w_ff2, v_mlp_post_g):
    w = dict(mix_pre_g=mix_pre_g, w_in=w_in, dw_kernel=dw_kernel, dw_bias=dw_bias, conv_ln_g=conv_ln_g, conv_ln_b=conv_ln_b,
             w_conv_out=w_conv_out, pool_w=pool_w, pool_scale=pool_scale, w_pool_out=w_pool_out, w_o=w_o,
             mix_post_g=mix_post_g, mlp_pre_g=mlp_pre_g, w_ff1=w_ff1, w_ff2=w_ff2, mlp_post_g=mlp_post_g)
    m = dict(mix_pre_g=m_mix_pre_g, w_in=m_w_in, dw_kernel=m_dw_kernel, dw_bias=m_dw_bias, conv_ln_g=m_conv_ln_g,
             conv_ln_b=m_conv_ln_b, w_conv_out=m_w_conv_out, pool_w=m_pool_w, pool_scale=m_pool_scale,
             w_pool_out=m_w_pool_out, w_o=m_w_o, mix_post_g=m_mix_post_g, mlp_pre_g=m_mlp_pre_g, w_ff1=m_w_ff1,
             w_ff2=m_w_ff2, mlp_post_g=m_mlp_post_g)
    v = dict(mix_pre_g=v_mix_pre_g, w_in=v_w_in, dw_kernel=v_dw_kernel, dw_bias=v_dw_bias, conv_ln_g=v_conv_ln_g,
             conv_ln_b=v_conv_ln_b, w_conv_out=v_w_conv_out, pool_w=v_pool_w, pool_scale=v_pool_scale,
             w_pool_out=v_w_pool_out, w_o=v_w_o, mix_post_g=v_mix_post_g, mlp_pre_g=v_mlp_pre_g, w_ff1=v_w_ff1,
             w_ff2=v_w_ff2, mlp_post_g=v_mlp_post_g)
    seq = x.shape[1]
    loss, grad_x, out = _step(x.reshape(seq, D), loss_target.reshape(seq, D), w, m, v)
    grads, deltas, new_m, new_v = ([out[n][k] for n in WEIGHT_ORDER] for k in range(4))
    return (loss, grad_x.reshape(x.shape), *grads, *deltas, *new_m, *new_v)
```

```python
import functools

import jax
import jax.numpy as jnp
from jax import lax
from jax.experimental import pallas as pl
from jax.experimental.pallas import tpu as pltpu

D = 1024
D_FF = 4 * D
IN_COLS = 5 * D
TAPS = 31
HALO = 32
POOL_WINDOWS = (2, 4, 8, 16)
PG = D // 4
N_DEV = 8
RMS_EPS = 1e-6
LN_EPS = 1e-5
ADAM_LR, ADAM_B1, ADAM_B2, ADAM_EPS, ADAM_WD, ADAM_STEP = 0.001, 0.9, 0.999, 1e-08, 0.01, 10

BF16 = jnp.bfloat16
F32 = jnp.float32
MIB = 1 << 20
MESH = pl.DeviceIdType.MESH


def _params(sem, vmem_mib):
    return pltpu.CompilerParams(dimension_semantics=sem, vmem_limit_bytes=vmem_mib * MIB)


def _dot(a, b):
    return jnp.dot(a, b, preferred_element_type=F32)


def _dot_nt(a, b):
    return lax.dot_general(a, b, (((1,), (1,)), ((), ())), preferred_element_type=F32)


def _dot_tn(a, b):
    return lax.dot_general(a, b, (((0,), (0,)), ((), ())), preferred_element_type=F32)


def _rms_scale(x):
    return lax.rsqrt(jnp.mean(x * x, axis=-1, keepdims=True) + RMS_EPS)


def _rms_bwd(dy, x, g):
    xn = x * _rms_scale(x)
    dn = dy * g
    dx = _rms_scale(x) * (dn - xn * jnp.mean(dn * xn, axis=-1, keepdims=True))
    return dx, jnp.sum(dy * xn, axis=0, keepdims=True)


def _sigmoid(x):
    return jax.nn.sigmoid(x)


def _acc_out(ref, first, value):
    @pl.when(first)
    def _():
        ref[...] = value

    @pl.when(jnp.logical_not(first))
    def _():
        ref[...] += value


def _my_slot():
    return 4 * lax.axis_index("x") + 2 * lax.axis_index("y") + lax.axis_index("c")


def _peer(mask):
    x, y, c = lax.axis_index("x"), lax.axis_index("y"), lax.axis_index("c")
    return (x ^ ((mask >> 2) & 1), y ^ ((mask >> 1) & 1), c ^ (mask & 1))


def _cols(width):
    return lambda ref, slot: ref.at[:, pl.ds(pl.multiple_of(slot * width, 128), width)]


def _rows(height):
    return lambda ref, slot: ref.at[pl.ds(pl.multiple_of(slot * height, 8), height), :]


def _lead(ref, slot):
    return ref.at[slot]


def _pool_rows(ref, slot):
    return ref.at[:, pl.ds(pl.multiple_of(slot * (PG // N_DEV), 8), PG // N_DEV), :]


SHARDED = (
    ("w_in", (D, IN_COLS), (D, IN_COLS // N_DEV), _cols(IN_COLS // N_DEV)),
    ("w_ff1", (D, D_FF), (D, D_FF // N_DEV), _cols(D_FF // N_DEV)),
    ("w_ff2", (D_FF, D), (D_FF // N_DEV, D), _rows(D_FF // N_DEV)),
    ("w_conv_out", (D, D), (D // N_DEV, D), _rows(D // N_DEV)),
    ("w_pool_out", (D, D), (D // N_DEV, D), _rows(D // N_DEV)),
    ("w_o", (D, D), (D // N_DEV, D), _rows(D // N_DEV)),
    ("pool_w", (4, PG, PG), (4, PG // N_DEV, PG), _pool_rows),
    ("dw_kernel", (N_DEV, TAPS, D // N_DEV), (TAPS, D // N_DEV), _lead),
)
N_SHARDED = len(SHARDED)


SHARD_AT = {name: at for name, _, _, at in SHARDED}
HBM = pl.BlockSpec(memory_space=pltpu.HBM)
SEM = pl.BlockSpec(memory_space=pltpu.SEMAPHORE)
EFFECT = pltpu.SideEffectType.DATAFLOW_SIDE_EFFECTING


def _in_hbm(a):
    return pltpu.with_memory_space_constraint(a, pltpu.HBM)


def _stage_shards(shards):
    dtypes = [F32 if name == "dw_kernel" else BF16 for name, *_ in SHARDED]

    def body(*refs):
        ins = refs[:N_SHARDED]
        stage = refs[N_SHARDED:2 * N_SHARDED]
        fulls = refs[2 * N_SHARDED:3 * N_SHARDED]
        sems = refs[3 * N_SHARDED]
        me = _my_slot()
        copies = []
        for a, (_, _, _, at) in enumerate(SHARDED):
            stage[a][...] = ins[a][...].astype(dtypes[a])
            cp = pltpu.make_async_copy(stage[a], at(fulls[a], me), sems.at[a])
            cp.start()
            copies.append(cp)
        for cp in copies:
            cp.wait()

    outs = pl.pallas_call(
        body, name="stage_shards",
        out_shape=[jax.ShapeDtypeStruct(shard, dt) for (_, _, shard, _), dt in zip(SHARDED, dtypes)]
        + [jax.ShapeDtypeStruct(full, dt) for (_, full, _, _), dt in zip(SHARDED, dtypes)],
        in_specs=[pl.BlockSpec(memory_space=pltpu.VMEM)] * N_SHARDED,
        out_specs=[pl.BlockSpec(memory_space=pltpu.VMEM)] * N_SHARDED + [pl.BlockSpec(memory_space=pl.ANY)] * N_SHARDED,
        scratch_shapes=[pltpu.SemaphoreType.DMA((N_SHARDED,))],
        compiler_params=pltpu.CompilerParams(vmem_limit_bytes=40 * MIB),
    )(*shards)
    return outs[:N_SHARDED], outs[N_SHARDED:]


def _sem_index(k, mask):
    return k * (N_DEV - 1) + mask - 1


def _exchange_start(name, groups, src_of, dst_of):
    sizes = [len(g[0]) for g in groups]
    names = [nm for g in groups for nm in g[0]]
    srcs = [s for g in groups for s in g[1]]
    lands = [l for g in groups for l in g[2]]
    n, n_g = len(names), len(groups)

    def body(*refs):
        src_refs, land_refs = refs[:n], refs[n:2 * n]
        sems = refs[2 * n:2 * n + 2 * n_g]
        token = refs[-1]
        me = _my_slot()
        first = 0
        for g, size in enumerate(sizes):
            for mask in range(1, N_DEV):
                for k in range(size):
                    a = first + k
                    pltpu.make_async_remote_copy(
                        src_ref=src_of(names[a], src_refs[a], me, mask), dst_ref=dst_of(names[a], land_refs[a], me, mask),
                        send_sem=sems[2 * g].at[_sem_index(k, mask)], recv_sem=sems[2 * g + 1].at[_sem_index(k, mask)],
                        device_id=_peer(mask), device_id_type=MESH).start()
            first += size
        token[...] = jnp.zeros_like(token)

    sem_shapes = [pltpu.SemaphoreType.DMA((size * (N_DEV - 1),)) for size in sizes for _ in range(2)]
    outs = pl.pallas_call(
        body, name=name,
        out_shape=sem_shapes + [pltpu.HBM(a.shape, a.dtype) for a in srcs + lands] + [jax.ShapeDtypeStruct((8, 128), F32)],
        in_specs=[HBM] * (2 * n),
        out_specs=[SEM] * (2 * n_g) + [HBM] * (2 * n) + [pl.BlockSpec(memory_space=pltpu.VMEM)],
        input_output_aliases={k: 2 * n_g + k for k in range(2 * n)},
        compiler_params=pltpu.CompilerParams(has_side_effects=EFFECT),
    )(*[_in_hbm(a) for a in srcs + lands])
    sems, thru, token = outs[:2 * n_g], outs[2 * n_g:-1], outs[-1]
    handles, a = [], 0
    for g, size in enumerate(sizes):
        handles.append((groups[g][0], sems[2 * g], sems[2 * g + 1], thru[a:a + size], thru[n + a:n + a + size]))
        a += size
    return handles, token


def _exchange_wait(name, handle, src_of, dst_of, after):
    names, send_sems, recv_sems, srcs, lands = handle
    n = len(names)

    def body(*refs):
        src_refs, land_refs = refs[:n], refs[n:2 * n]
        send_ref, recv_ref = refs[2 * n], refs[2 * n + 1]
        me = _my_slot()
        for mask in range(1, N_DEV):
            for a in range(n):
                cp = pltpu.make_async_remote_copy(
                    src_ref=src_of(names[a], src_refs[a], me, mask), dst_ref=dst_of(names[a], land_refs[a], me ^ mask, mask),
                    send_sem=send_ref.at[_sem_index(a, mask)], recv_sem=recv_ref.at[_sem_index(a, mask)],
                    device_id=_peer(mask), device_id_type=MESH)
                cp.wait_send()
                cp.wait_recv()

    outs = pl.pallas_call(
        body, name=name,
        out_shape=[pltpu.HBM(a.shape, a.dtype) for a in list(srcs) + list(lands)],
        in_specs=[HBM] * (2 * n) + [SEM, SEM, pl.BlockSpec(memory_space=pl.ANY)],
        out_specs=[HBM] * (2 * n),
        input_output_aliases={k: k for k in range(2 * n)},
        compiler_params=pltpu.CompilerParams(has_side_effects=EFFECT),
    )(*srcs, *lands, send_sems, recv_sems, after)
    return outs[:n], outs[n:]


def _gather_src(name, ref, me, mask):
    return ref


def _gather_dst(name, ref, origin, mask):
    return SHARD_AT[name](ref, origin)


def _scatter_src(name, ref, me, mask):
    return SHARD_AT[name](ref, me ^ mask)


def _scatter_dst(name, ref, origin, mask):
    return ref.at[mask - 1]


def _vec_exchange(vec_grads):
    def body(vec, vec_out, send_sems, recv_sems, local_sem):
        me = _my_slot()
        local = pltpu.make_async_copy(vec, vec_out.at[me], local_sem)
        local.start()
        sends = []
        for mask in range(1, N_DEV):
            cp = pltpu.make_async_remote_copy(
                src_ref=vec, dst_ref=vec_out.at[me], send_sem=send_sems.at[mask - 1],
                recv_sem=recv_sems.at[mask - 1], device_id=_peer(mask), device_id_type=MESH)
            cp.start()
            sends.append(cp)
        for mask in range(1, N_DEV):
            pltpu.make_async_remote_copy(
                src_ref=vec, dst_ref=vec_out.at[me ^ mask], send_sem=send_sems.at[mask - 1],
                recv_sem=recv_sems.at[mask - 1], device_id=_peer(mask), device_id_type=MESH).wait_recv()
        for cp in sends:
            cp.wait_send()
        local.wait()

    return pl.pallas_call(
        body, name="vec_exchange",
        out_shape=jax.ShapeDtypeStruct((N_DEV,) + vec_grads.shape, F32),
        in_specs=[pl.BlockSpec(memory_space=pl.ANY)],
        out_specs=pl.BlockSpec(memory_space=pl.ANY),
        scratch_shapes=[pltpu.SemaphoreType.DMA((N_DEV - 1,)), pltpu.SemaphoreType.DMA((N_DEV - 1,)),
                        pltpu.SemaphoreType.DMA],
    )(vec_grads)


def _adamw_update(g, w_ref, m_ref, v_ref, g_out, d_out, m_out, v_out):
    m_new = ADAM_B1 * m_ref[...] + (1.0 - ADAM_B1) * g
    v_new = ADAM_B2 * v_ref[...] + (1.0 - ADAM_B2) * (g * g)
    m_hat = m_new / (1.0 - ADAM_B1 ** ADAM_STEP)
    v_hat = v_new / (1.0 - ADAM_B2 ** ADAM_STEP)
    g_out[...] = g
    d_out[...] = -ADAM_LR * (m_hat / (jnp.sqrt(v_hat) + ADAM_EPS) + ADAM_WD * w_ref[...])
    m_out[...] = m_new
    v_out[...] = v_new


ADAMW_ROWS = 256


def _adamw(name, own, parts, w, m, v, slot):
    shard = w.shape
    if name in ("w_in", "w_ff1"):
        tr = ADAMW_ROWS
        grid = (shard[0] // tr,)
        own_spec = pl.BlockSpec((tr, shard[1]), lambda i, s: (i, s[0]))
        blk = pl.BlockSpec((tr, shard[1]), lambda i, s: (i, 0))
        parts_spec = pl.BlockSpec((N_DEV - 1, tr, shard[1]), lambda i, s: (0, i, 0))
    elif name == "pool_w":
        grid = (shard[0],)
        own_spec = pl.BlockSpec((None,) + shard[1:], lambda g, s: (g, s[0], 0))
        blk = pl.BlockSpec((None,) + shard[1:], lambda g, s: (g, 0, 0))
        parts_spec = pl.BlockSpec((N_DEV - 1, None) + shard[1:], lambda g, s: (0, g, 0, 0))
    elif name == "dw_kernel":
        grid = (1,)
        own_spec = pl.BlockSpec((None,) + shard, lambda i, s: (s[0], 0, 0))
        blk = pl.BlockSpec(shard, lambda i, s: (0, 0))
        parts_spec = pl.BlockSpec((N_DEV - 1,) + shard, lambda i, s: (0, 0, 0))
    else:
        tr = min(ADAMW_ROWS, shard[0])
        grid = (shard[0] // tr,)
        own_spec = pl.BlockSpec((tr, shard[1]), lambda i, s: (s[0] * grid[0] + i, 0))
        blk = pl.BlockSpec((tr, shard[1]), lambda i, s: (i, 0))
        parts_spec = pl.BlockSpec((N_DEV - 1, tr, shard[1]), lambda i, s: (0, i, 0))

    def body(slot_ref, own_ref, p_ref, w_ref, m_ref, v_ref, g_out, d_out, m_out, v_out):
        g = own_ref[...].astype(F32)
        for k in range(N_DEV - 1):
            g = g + p_ref[k].astype(F32)
        _adamw_update(g, w_ref, m_ref, v_ref, g_out, d_out, m_out, v_out)

    return pl.pallas_call(
        body, name="adamw_" + name,
        grid_spec=pltpu.PrefetchScalarGridSpec(
            num_scalar_prefetch=1, grid=grid, in_specs=[own_spec, parts_spec, blk, blk, blk], out_specs=[blk] * 4),
        out_shape=[jax.ShapeDtypeStruct(shard, F32)] * 4,
        compiler_params=_params(("arbitrary",), 32),
    )(slot, own, parts, w, m, v)


def _adamw_vectors(parts, w, m, v):
    def body(p_ref, w_ref, m_ref, v_ref, g_out, d_out, m_out, v_out):
        g = p_ref[0]
        for s in range(1, N_DEV):
            g = g + p_ref[s]
        _adamw_update(g, w_ref, m_ref, v_ref, g_out, d_out, m_out, v_out)

    return pl.pallas_call(
        body, name="adamw_vectors",
        out_shape=[jax.ShapeDtypeStruct(w.shape, F32)] * 4,
    )(parts, w, m, v)


def _in_proj_fwd(x, g_pre, w_in):
    seq = x.shape[0]
    tm, tn = 1024, IN_COLS // N_DEV

    def body(x_ref, g_ref, w_ref, proj_ref, u_ref):
        @pl.when(pl.program_id(1) == 0)
        def _():
            xf = x_ref[...]
            u_ref[...] = (xf * _rms_scale(xf) * g_ref[...]).astype(BF16)
        proj_ref[...] = _dot(u_ref[...], w_ref[...])

    return pl.pallas_call(
        body, name="in_proj_fwd", grid=(seq // tm, N_DEV),
        in_specs=[pl.BlockSpec((tm, D), lambda i, j: (i, 0)), pl.BlockSpec((1, D), lambda i, j: (0, 0)),
                  pl.BlockSpec((D, tn), lambda i, j: (0, j))],
        out_specs=[pl.BlockSpec((tm, tn), lambda i, j: (i, j)), pl.BlockSpec((tm, D), lambda i, j: (i, 0))],
        out_shape=[jax.ShapeDtypeStruct((seq, IN_COLS), F32), jax.ShapeDtypeStruct((seq, D), BF16)],
        compiler_params=_params(("arbitrary", "arbitrary"), 40),
    )(x, g_pre, w_in)


CONV_TM = 256
CONV_RS = 64


def _shifts():
    return [(b, [(a, 8 * a + b) for a in range(4) if 8 * a + b < TAPS]) for b in range(8)]


def _taps_looking_back(buf, row0, lanes, weight):
    acc = None
    for b, group in _shifts():
        part = None
        for a, s in group:
            term = weight(TAPS - 1 - s) * buf[pl.ds(row0 - 8 - 8 * a, CONV_RS + 8), lanes]
            part = term if part is None else part + term
        if b:
            part = pltpu.roll(part, b, 0)
        acc = part[8:, :] if acc is None else acc + part[8:, :]
    return acc


def _taps_looking_ahead(buf, row0, lanes, weight):
    acc = None
    for b, group in _shifts():
        part = None
        for a, s in group:
            term = weight(TAPS - 1 - s) * buf[pl.ds(row0 + 8 * a, CONV_RS + 8), lanes]
            part = term if part is None else part + term
        if b:
            part = pltpu.roll(part, CONV_RS + 8 - b, 0)
        acc = part[:CONV_RS, :] if acc is None else acc + part[:CONV_RS, :]
    return acc


def _layer_norm_parts(cv):
    mu = jnp.mean(cv, axis=-1, keepdims=True)
    cen = cv - mu
    rstd = lax.rsqrt(jnp.mean(cen * cen, axis=-1, keepdims=True) + LN_EPS)
    return cen * rstd, rstd


def _conv_fwd(proj, dw, dw_bias, ln_g, ln_b, w_conv_out):
    seq = proj.shape[0]
    tm = CONV_TM

    def body(a_ref, gate_ref, dw_ref, bias_ref, lg_ref, lb_ref, w_ref, cv_ref, y_ref, ug):
        i = pl.program_id(0)

        @pl.when(i == 0)
        def _():
            ug[0:HALO, :] = jnp.zeros((HALO, D), F32)

        @pl.when(i > 0)
        def _():
            ug[0:HALO, :] = ug[tm:tm + HALO, :]
        ug[HALO:HALO + tm, :] = a_ref[...] * _sigmoid(gate_ref[...])

        def channel_block(cb, carry):
            lanes = pl.ds(pl.multiple_of(cb * 128, 128), 128)
            for r0 in range(0, tm, CONV_RS):
                taps = _taps_looking_back(ug, HALO + r0, lanes, lambda k: dw_ref[cb, k:k + 1, :])
                cv_ref[pl.ds(r0, CONV_RS), lanes] = taps + bias_ref[:, lanes]
            return carry
        lax.fori_loop(0, D // 128, channel_block, 0)

        n, _ = _layer_norm_parts(cv_ref[...])
        ln = n * lg_ref[...] + lb_ref[...]
        y_ref[...] = _dot((ln * _sigmoid(ln)).astype(BF16), w_ref[...])

    vec = pl.BlockSpec((1, D), lambda i: (0, 0))
    tile = pl.BlockSpec((tm, D), lambda i: (i, 0))
    return pl.pallas_call(
        body, name="conv_fwd", grid=(seq // tm,),
        in_specs=[pl.BlockSpec((tm, D), lambda i: (i, 0)), pl.BlockSpec((tm, D), lambda i: (i, 1)),
                  pl.BlockSpec((N_DEV, TAPS, 128), lambda i: (0, 0, 0)), vec, vec, vec,
                  pl.BlockSpec((D, D), lambda i: (0, 0))],
        out_specs=[tile, tile],
        out_shape=[jax.ShapeDtypeStruct((seq, D), F32)] * 2,
        scratch_shapes=[pltpu.VMEM((HALO + tm, D), F32)],
        compiler_params=_params(("arbitrary",), 32),
    )(proj, proj, dw, dw_bias, ln_g, ln_b, w_conv_out)


def _pool_counts(tile_index, tm, window):
    t = tile_index * tm + lax.broadcasted_iota(jnp.int32, (tm, 1), 0)
    return 1.0 / jnp.minimum(t + 1, window).astype(F32)


def _pool_merge_fwd(proj, y_conv, x, pool_w, pool_scale, w_pool_out, w_o, g_post):
    seq = proj.shape[0]
    tm = CONV_TM

    def body(p_ref, gc_ref, gp_ref, yc_ref, x_ref, pw_ref, ps_ref, wpo_ref, wo_ref, g_ref,
             z_ref, zl_ref, yp_ref, mg_ref, o_ref, h1_ref, pbuf):
        i = pl.program_id(0)

        @pl.when(i == 0)
        def _():
            pbuf[0:HALO, :] = jnp.zeros((HALO, D), F32)

        @pl.when(i > 0)
        def _():
            pbuf[0:HALO, :] = pbuf[tm:tm + HALO, :]
        pbuf[HALO:HALO + tm, :] = p_ref[...]

        for g, window in enumerate(POOL_WINDOWS):
            lanes = pl.ds(g * PG, PG)
            acc = pbuf[pl.ds(HALO, tm), lanes]
            for j in range(1, window):
                acc = acc + pbuf[pl.ds(HALO - j, tm), lanes]
            zg = acc * _pool_counts(i, tm, window) - pbuf[pl.ds(HALO, tm), lanes]
            z_ref[:, lanes] = zg.astype(BF16)
            zl_ref[:, lanes] = _dot(zg.astype(BF16), pw_ref[g])
        zl = zl_ref[...]
        y_pool = _dot((zl * ps_ref[...]).astype(BF16), wpo_ref[...])
        yp_ref[...] = y_pool
        merged = (_sigmoid(gc_ref[...]) * yc_ref[...] + _sigmoid(gp_ref[...]) * y_pool).astype(BF16)
        mg_ref[...] = merged
        o = _dot(merged, wo_ref[...])
        o_ref[...] = o
        h1_ref[...] = x_ref[...] + o * _rms_scale(o) * g_ref[...]

    vec = pl.BlockSpec((1, D), lambda i: (0, 0))
    tile = pl.BlockSpec((tm, D), lambda i: (i, 0))
    mat = pl.BlockSpec((D, D), lambda i: (0, 0))
    return pl.pallas_call(
        body, name="pool_merge_fwd", grid=(seq // tm,),
        in_specs=[pl.BlockSpec((tm, D), lambda i: (i, 2)), pl.BlockSpec((tm, D), lambda i: (i, 3)),
                  pl.BlockSpec((tm, D), lambda i: (i, 4)), tile, tile,
                  pl.BlockSpec((4, PG, PG), lambda i: (0, 0, 0)), vec, mat, mat, vec],
        out_specs=[tile] * 6,
        out_shape=[jax.ShapeDtypeStruct((seq, D), dt) for dt in (BF16, F32, F32, BF16, F32, F32)],
        scratch_shapes=[pltpu.VMEM((HALO + tm, D), F32)],
        compiler_params=_params(("arbitrary",), 48),
    )(proj, proj, proj, y_conv, x, pool_w, pool_scale, w_pool_out, w_o, g_post)


def _mlp_fwd(h1, g_pre, w_ff1, w_ff2, g_post, target):
    seq = h1.shape[0]
    tm, tf = 1024, D_FF // N_DEV
    n_f = D_FF // tf

    def body(h1_ref, gpre_ref, w1_ref, w2_ref, gpost_ref, tgt_ref, v_ref, dm_ref, dh2_ref, sse_ref, ggrad_ref, macc):
        i, j = pl.program_id(0), pl.program_id(1)

        @pl.when(j == 0)
        def _():
            h = h1_ref[...]
            v_ref[...] = (h * _rms_scale(h) * gpre_ref[...]).astype(BF16)
        f = jnp.maximum(_dot(v_ref[...], w1_ref[...]), 0.0)
        part = _dot((f * f).astype(BF16), w2_ref[...])

        @pl.when(j == 0)
        def _():
            macc[...] = part

        @pl.when(j > 0)
        def _():
            macc[...] += part

        @pl.when(j == n_f - 1)
        def _():
            mo = macc[...]
            err = h1_ref[...] + mo * _rms_scale(mo) * gpost_ref[...] - tgt_ref[...]
            dh2 = err * (1.0 / D)
            dh2_ref[...] = dh2
            dm, ggrad = _rms_bwd(dh2, mo, gpost_ref[...])
            dm_ref[...] = dm.astype(BF16)
            _acc_out(ggrad_ref, i == 0, ggrad)
            _acc_out(sse_ref, i == 0, jnp.sum(jnp.sum(err * err, axis=1, keepdims=True), axis=0, keepdims=True))

    vec = pl.BlockSpec((1, D), lambda i, j: (0, 0))
    tile = pl.BlockSpec((tm, D), lambda i, j: (i, 0))
    return pl.pallas_call(
        body, name="mlp_fwd", grid=(seq // tm, n_f),
        in_specs=[tile, vec, pl.BlockSpec((D, tf), lambda i, j: (0, j)), pl.BlockSpec((tf, D), lambda i, j: (j, 0)), vec, tile],
        out_specs=[tile, tile, tile, pl.BlockSpec((1, 1), lambda i, j: (0, 0)), vec],
        out_shape=[jax.ShapeDtypeStruct((seq, D), BF16), jax.ShapeDtypeStruct((seq, D), BF16),
                   jax.ShapeDtypeStruct((seq, D), F32), jax.ShapeDtypeStruct((1, 1), F32),
                   jax.ShapeDtypeStruct((1, D), F32)],
        scratch_shapes=[pltpu.VMEM((tm, D), F32)],
        compiler_params=_params(("arbitrary", "arbitrary"), 56),
    )(h1, g_pre, w_ff1, w_ff2, g_post, target)


def _mlp_bwd(v, dm, w_ff1, w_ff2):
    seq = v.shape[0]
    tm, tf = 512, D_FF // N_DEV
    n_t = seq // tm

    def body(v_ref, dm_ref, w1_ref, w2_ref, dv_hbm, g1_ref, g2_ref, dv_acc, g1_acc, g2_acc, sem):
        j, i = pl.program_id(0), pl.program_id(1)
        vt, dmt = v_ref[...], dm_ref[...]
        f = jnp.maximum(_dot(vt, w1_ref[...]), 0.0)
        df = (_dot_nt(dmt, w2_ref[...]) * (2.0 * f)).astype(BF16)
        rows = pl.ds(pl.multiple_of(i * tm, tm), tm)
        dv_part = _dot_nt(df, w1_ref[...])

        @pl.when(j == 0)
        def _():
            dv_acc[rows, :] = dv_part

        @pl.when(j > 0)
        def _():
            dv_acc[rows, :] += dv_part
        g1_part = _dot_tn(vt, df)
        g2_part = _dot_tn((f * f).astype(BF16), dmt)

        @pl.when(i == 0)
        def _():
            g1_acc[...] = g1_part
            g2_acc[...] = g2_part

        @pl.when(i > 0)
        def _():
            g1_acc[...] += g1_part
            g2_acc[...] += g2_part

        @pl.when(i == n_t - 1)
        def _():
            g1_ref[...] = g1_acc[...].astype(BF16)
            g2_ref[...] = g2_acc[...].astype(BF16)

        @pl.when(jnp.logical_and(i == n_t - 1, j == D_FF // tf - 1))
        def _():
            cp = pltpu.make_async_copy(dv_acc, dv_hbm, sem)
            cp.start()
            cp.wait()

    tile = pl.BlockSpec((tm, D), lambda j, i: (i, 0))
    return pl.pallas_call(
        body, name="mlp_bwd", grid=(D_FF // tf, n_t),
        in_specs=[tile, tile, pl.BlockSpec((D, tf), lambda j, i: (0, j)), pl.BlockSpec((tf, D), lambda j, i: (j, 0))],
        out_specs=[pl.BlockSpec(memory_space=pl.ANY), pl.BlockSpec((D, tf), lambda j, i: (0, j)),
                   pl.BlockSpec((tf, D), lambda j, i: (j, 0))],
        out_shape=[jax.ShapeDtypeStruct((seq, D), F32), jax.ShapeDtypeStruct((D, D_FF), BF16),
                   jax.ShapeDtypeStruct((D_FF, D), BF16)],
        scratch_shapes=[pltpu.VMEM((seq, D), F32), pltpu.VMEM((D, tf), F32), pltpu.VMEM((tf, D), F32),
                        pltpu.SemaphoreType.DMA],
        compiler_params=_params(("arbitrary", "arbitrary"), 52),
    )(v, dm, w_ff1, w_ff2)


ANY = pl.BlockSpec(memory_space=pl.ANY)


def _merge_bwd(dh2, dv, h1, g_mlp_pre, o, g_mix_post, w_o, merged, proj, y_conv, y_pool, token):
    seq = dh2.shape[0]
    tm = 256
    n_t = seq // tm

    def body(dh2_ref, dv_ref, h1_ref, gpre_ref, o_ref, gpost_ref, wo_ref, mg_ref, gc_ref, gp_ref, yc_ref, yp_ref, _token,
             dh1_ref, dyc_ref, dyp_ref, dg_ref, gwo_ref, ggpre_ref, ggpost_ref, gwo_acc):
        i = pl.program_id(0)
        dnorm, ggpre = _rms_bwd(dv_ref[...], h1_ref[...], gpre_ref[...])
        dh1 = dh2_ref[...] + dnorm
        dh1_ref[...] = dh1
        do, ggpost = _rms_bwd(dh1, o_ref[...], gpost_ref[...])
        do = do.astype(BF16)
        _acc_out(ggpre_ref, i == 0, ggpre)
        _acc_out(ggpost_ref, i == 0, ggpost)
        _acc_out(gwo_acc, i == 0, _dot_tn(mg_ref[...], do))
        dmerged = _dot_nt(do, wo_ref[...])
        sc, sp = _sigmoid(gc_ref[...]), _sigmoid(gp_ref[...])
        dyc_ref[...] = (dmerged * sc).astype(BF16)
        dyp_ref[...] = (dmerged * sp).astype(BF16)
        dg_ref[:, 0:D] = (dmerged * yc_ref[...] * (sc * (1.0 - sc))).astype(BF16)
        dg_ref[:, D:2 * D] = (dmerged * yp_ref[...] * (sp * (1.0 - sp))).astype(BF16)

        @pl.when(i == n_t - 1)
        def _():
            gwo_ref[...] = gwo_acc[...].astype(BF16)

    vec = pl.BlockSpec((1, D), lambda i: (0, 0))
    tile = pl.BlockSpec((tm, D), lambda i: (i, 0))
    mat = pl.BlockSpec((D, D), lambda i: (0, 0))
    return pl.pallas_call(
        body, name="merge_bwd", grid=(n_t,),
        in_specs=[tile, tile, tile, vec, tile, vec, mat, tile,
                  pl.BlockSpec((tm, D), lambda i: (i, 3)), pl.BlockSpec((tm, D), lambda i: (i, 4)), tile, tile, ANY],
        out_specs=[tile, tile, tile, pl.BlockSpec((tm, 2 * D), lambda i: (i, 0)), mat, vec, vec],
        out_shape=[jax.ShapeDtypeStruct((seq, D), F32), jax.ShapeDtypeStruct((seq, D), BF16),
                   jax.ShapeDtypeStruct((seq, D), BF16), jax.ShapeDtypeStruct((seq, 2 * D), BF16),
                   jax.ShapeDtypeStruct((D, D), BF16), jax.ShapeDtypeStruct((1, D), F32),
                   jax.ShapeDtypeStruct((1, D), F32)],
        scratch_shapes=[pltpu.VMEM((D, D), F32)],
        compiler_params=_params(("arbitrary",), 48),
    )(dh2, dv, h1, g_mlp_pre, o, g_mix_post, w_o, merged, proj, proj, y_conv, y_pool, token)


def _pool_bwd(dy_pool, zl, z, pool_w, pool_scale, w_pool_out):
    seq = dy_pool.shape[0]
    tm = CONV_TM
    n_t = seq // tm

    def body(dy_ref, zl_ref, z_ref, pw_ref, ps_ref, wpo_ref, dp_ref, gwpo_ref, gpw_ref, gps_ref, qbuf, gwpo_acc, gpw_acc):
        i = pl.program_id(0)
        tile_index = n_t - 1 - i
        first = i == 0
        dy = dy_ref[...]
        zl = zl_ref[...]
        dzs = _dot_nt(dy, wpo_ref[...])
        _acc_out(gwpo_acc, first, _dot_tn((zl * ps_ref[...]).astype(BF16), dy))
        _acc_out(gps_ref, first, jnp.sum(dzs * zl, axis=0, keepdims=True))
        dzl = (dzs * ps_ref[...]).astype(BF16)

        @pl.when(first)
        def _():
            qbuf[tm:tm + HALO, :] = jnp.zeros((HALO, D), F32)

        @pl.when(jnp.logical_not(first))
        def _():
            qbuf[tm:tm + HALO, :] = qbuf[0:HALO, :]

        dzs_list = []
        for g, window in enumerate(POOL_WINDOWS):
            lanes = pl.ds(g * PG, PG)
            dzl_g = dzl[:, g * PG:(g + 1) * PG]
            dz = _dot_nt(dzl_g, pw_ref[g])
            _acc_out(gpw_acc.at[g], first, _dot_tn(z_ref[:, lanes], dzl_g))
            qbuf[pl.ds(0, tm), lanes] = dz * _pool_counts(tile_index, tm, window)
            dzs_list.append(dz)
        for g, window in enumerate(POOL_WINDOWS):
            lanes = pl.ds(g * PG, PG)
            acc = qbuf[pl.ds(0, tm), lanes]
            for j in range(1, window):
                acc = acc + qbuf[pl.ds(j, tm), lanes]
            dp_ref[:, lanes] = (acc - dzs_list[g]).astype(BF16)

        @pl.when(i == n_t - 1)
        def _():
            gwpo_ref[...] = gwpo_acc[...].astype(BF16)
            gpw_ref[...] = gpw_acc[...].astype(BF16)

    vec = pl.BlockSpec((1, D), lambda i: (0, 0))
    tile = pl.BlockSpec((tm, D), lambda i: (n_t - 1 - i, 0))
    mat = pl.BlockSpec((D, D), lambda i: (0, 0))
    pw = pl.BlockSpec((4, PG, PG), lambda i: (0, 0, 0))
    return pl.pallas_call(
        body, name="pool_bwd", grid=(n_t,),
        in_specs=[tile, tile, tile, pw, vec, mat],
        out_specs=[tile, mat, pw, vec],
        out_shape=[jax.ShapeDtypeStruct((seq, D), BF16), jax.ShapeDtypeStruct((D, D), BF16),
                   jax.ShapeDtypeStruct((4, PG, PG), BF16), jax.ShapeDtypeStruct((1, D), F32)],
        scratch_shapes=[pltpu.VMEM((tm + HALO, D), F32), pltpu.VMEM((D, D), F32), pltpu.VMEM((4, PG, PG), F32)],
        compiler_params=_params(("arbitrary",), 40),
    )(dy_pool, zl, z, pool_w, pool_scale, w_pool_out)


def _conv_bwd(dy_conv, cv, proj, dw, ln_g, ln_b, w_conv_out):
    seq = dy_conv.shape[0]
    tm = CONV_TM
    n_t = seq // tm
    halo_blocks = tm // HALO

    def body(dy_ref, cv_ref, a_ref, gate_ref, ah_ref, gh_ref, dw_ref, lg_ref, lb_ref, w_ref,
             dglu_ref, gw_ref, gdw_ref, gbias_ref, glg_ref, glb_ref, ug, dcv, dug, gw_acc):
        i = pl.program_id(0)
        tile_index = n_t - 1 - i
        first = i == 0
        dy = dy_ref[...]
        n, rstd = _layer_norm_parts(cv_ref[...])
        ln = n * lg_ref[...] + lb_ref[...]
        sg = _sigmoid(ln)
        _acc_out(gw_acc, first, _dot_tn((ln * sg).astype(BF16), dy))
        dln = _dot_nt(dy, w_ref[...]) * (sg * (1.0 + ln * (1.0 - sg)))
        _acc_out(glg_ref, first, jnp.sum(dln * n, axis=0, keepdims=True))
        _acc_out(glb_ref, first, jnp.sum(dln, axis=0, keepdims=True))
        dn = dln * lg_ref[...]
        dcv_tile = rstd * (dn - jnp.mean(dn, axis=-1, keepdims=True) - n * jnp.mean(dn * n, axis=-1, keepdims=True))
        _acc_out(gbias_ref, first, jnp.sum(dcv_tile, axis=0, keepdims=True))

        @pl.when(first)
        def _():
            dcv[tm:tm + HALO, :] = jnp.zeros((HALO, D), F32)

        @pl.when(jnp.logical_not(first))
        def _():
            dcv[tm:tm + HALO, :] = dcv[0:HALO, :]
        dcv[0:tm, :] = dcv_tile

        a, gate = a_ref[...], gate_ref[...]
        sgate = _sigmoid(gate)
        ug[HALO:HALO + tm, :] = a * sgate
        before = jnp.where(tile_index > 0, 1.0, 0.0)
        ug[0:HALO, :] = ah_ref[...] * _sigmoid(gh_ref[...]) * before

        @pl.when(first)
        def _():
            gdw_ref[...] = jnp.zeros((N_DEV, TAPS + 1, 128), F32)

        def channel_block(cb, carry):
            lanes = pl.ds(pl.multiple_of(cb * 128, 128), 128)
            for r0 in range(0, tm, CONV_RS):
                dug[pl.ds(r0, CONV_RS), lanes] = _taps_looking_ahead(dcv, r0, lanes, lambda k: dw_ref[cb, k:k + 1, :])
            for b, group in _shifts():
                sums = [jnp.zeros((8, 128), F32) for _ in group]
                for r0 in range(0, tm, CONV_RS):
                    window = ug[pl.ds(r0, CONV_RS + HALO), lanes]
                    if b:
                        window = pltpu.roll(window, b, 0)
                    d = dcv[pl.ds(r0, CONV_RS), lanes]
                    for n_a, (a, s) in enumerate(group):
                        prod = d * window[HALO - 8 * a:HALO - 8 * a + CONV_RS, :]
                        sums[n_a] = sums[n_a] + jnp.sum(prod.reshape(CONV_RS // 8, 8, 128), axis=0)
                for n_a, (a, s) in enumerate(group):
                    k = TAPS - 1 - s
                    gdw_ref[cb, k:k + 1, :] += jnp.sum(sums[n_a], axis=0, keepdims=True)
            return carry
        lax.fori_loop(0, D // 128, channel_block, 0)

        d_ug = dug[...]
        dglu_ref[:, 0:D] = (d_ug * sgate).astype(BF16)
        dglu_ref[:, D:2 * D] = (d_ug * a * (sgate * (1.0 - sgate))).astype(BF16)

        @pl.when(i == n_t - 1)
        def _():
            gw_ref[...] = gw_acc[...].astype(BF16)

    def halo_index(col):
        return lambda i: (jnp.maximum((n_t - 1 - i) * halo_blocks - 1, 0), col)

    vec = pl.BlockSpec((1, D), lambda i: (0, 0))
    tile = pl.BlockSpec((tm, D), lambda i: (n_t - 1 - i, 0))
    mat = pl.BlockSpec((D, D), lambda i: (0, 0))
    dwspec = pl.BlockSpec((N_DEV, TAPS, 128), lambda i: (0, 0, 0))
    return pl.pallas_call(
        body, name="conv_bwd", grid=(n_t,),
        in_specs=[tile, tile, pl.BlockSpec((tm, D), lambda i: (n_t - 1 - i, 0)), pl.BlockSpec((tm, D), lambda i: (n_t - 1 - i, 1)),
                  pl.BlockSpec((HALO, D), halo_index(0)), pl.BlockSpec((HALO, D), halo_index(1)), dwspec, vec, vec, mat],
        out_specs=[pl.BlockSpec((tm, 2 * D), lambda i: (n_t - 1 - i, 0)), mat,
                   pl.BlockSpec((N_DEV, TAPS + 1, 128), lambda i: (0, 0, 0)), vec, vec, vec],
        out_shape=[jax.ShapeDtypeStruct((seq, 2 * D), BF16), jax.ShapeDtypeStruct((D, D), BF16),
                   jax.ShapeDtypeStruct((N_DEV, TAPS + 1, 128), F32), jax.ShapeDtypeStruct((1, D), F32),
                   jax.ShapeDtypeStruct((1, D), F32), jax.ShapeDtypeStruct((1, D), F32)],
        scratch_shapes=[pltpu.VMEM((HALO + tm, D), F32), pltpu.VMEM((tm + HALO, D), F32), pltpu.VMEM((tm, D), F32),
                        pltpu.VMEM((D, D), F32)],
        compiler_params=_params(("arbitrary",), 48),
    )(dy_conv, cv, proj, proj, proj, proj, dw, ln_g, ln_b, w_conv_out)


def _in_proj_bwd_x(d_glu, dp, dgates, w_in, x, g_pre, dh1, token):
    seq = x.shape[0]
    tm = 256

    def body(dglu_ref, dp_ref, dg_ref, w_ref, x_ref, g_ref, dh1_ref, _token, dx_ref, gg_ref):
        du = _dot_nt(dglu_ref[...], w_ref[:, 0:2 * D])
        du += _dot_nt(dp_ref[...], w_ref[:, 2 * D:3 * D])
        du += _dot_nt(dg_ref[...], w_ref[:, 3 * D:5 * D])
        dnorm, gg = _rms_bwd(du, x_ref[...], g_ref[...])
        dx_ref[...] = dh1_ref[...] + dnorm
        _acc_out(gg_ref, pl.program_id(0) == 0, gg)

    vec = pl.BlockSpec((1, D), lambda i: (0, 0))
    tile = pl.BlockSpec((tm, D), lambda i: (i, 0))
    wide = pl.BlockSpec((tm, 2 * D), lambda i: (i, 0))
    return pl.pallas_call(
        body, name="in_proj_bwd_x", grid=(seq // tm,),
        in_specs=[wide, tile, wide, pl.BlockSpec((D, IN_COLS), lambda i: (0, 0)), tile, vec, tile, ANY],
        out_specs=[tile, vec],
        out_shape=[jax.ShapeDtypeStruct((seq, D), F32), jax.ShapeDtypeStruct((1, D), F32)],
        compiler_params=_params(("arbitrary",), 48),
    )(d_glu, dp, dgates, w_in, x, g_pre, dh1, token)


def _in_proj_bwd_w(u, d_glu, dp, dgates, token):
    seq = u.shape[0]
    tm = 512
    n_t = seq // tm

    def body(u_ref, dglu_ref, dp_ref, dg_ref, _token, out_ref, acc):
        b, i = pl.program_id(0), pl.program_id(1)
        ut = u_ref[...]

        def add(d_ref):
            _acc_out(acc, i == 0, _dot_tn(ut, d_ref[...]))

        pl.when(b < 2)(lambda: add(dglu_ref))
        pl.when(b == 2)(lambda: add(dp_ref))
        pl.when(b > 2)(lambda: add(dg_ref))

        @pl.when(i == n_t - 1)
        def _():
            out_ref[...] = acc[...].astype(BF16)

    return pl.pallas_call(
        body, name="in_proj_bwd_w", grid=(IN_COLS // D, n_t),
        in_specs=[pl.BlockSpec((tm, D), lambda b, i: (i, 0)),
                  pl.BlockSpec((tm, D), lambda b, i: (jnp.where(b < 2, i, 0), jnp.minimum(b, 1))),
                  pl.BlockSpec((tm, D), lambda b, i: (jnp.where(b == 2, i, 0), 0)),
                  pl.BlockSpec((tm, D), lambda b, i: (jnp.where(b > 2, i, 0), jnp.maximum(b - 3, 0))), ANY],
        out_specs=pl.BlockSpec((D, D), lambda b, i: (0, b)),
        out_shape=jax.ShapeDtypeStruct((D, IN_COLS), BF16),
        scratch_shapes=[pltpu.VMEM((D, D), F32)],
        compiler_params=_params(("arbitrary", "arbitrary"), 40),
    )(u, d_glu, dp, dgates, token)


VEC_NAMES = ("mix_pre_g", "dw_bias", "conv_ln_g", "conv_ln_b", "pool_scale", "mix_post_g", "mlp_pre_g", "mlp_post_g")
WEIGHT_ORDER = ("mix_pre_g", "w_in", "dw_kernel", "dw_bias", "conv_ln_g", "conv_ln_b", "w_conv_out", "pool_w",
                "pool_scale", "w_pool_out", "w_o", "mix_post_g", "mlp_pre_g", "w_ff1", "w_ff2", "mlp_post_g")


def _step(x, loss_target, w, m, v):
    row = lambda a: a.reshape(1, D)
    names = [s[0] for s in SHARDED]

    stage, seeded = _stage_shards([w[n] for n in names])
    stage, seeded = dict(zip(names, stage)), dict(zip(names, seeded))
    gather_groups = (("w_in",), ("w_conv_out", "dw_kernel"), ("pool_w", "w_pool_out", "w_o"), ("w_ff1", "w_ff2"))
    handles, token = _exchange_start(
        "gather_start", [(g, [stage[n] for n in g], [seeded[n] for n in g]) for g in gather_groups], _gather_src, _gather_dst)
    full = {}

    def gathered(k, after):
        _, arrays = _exchange_wait("gather_wait_" + gather_groups[k][0], handles[k], _gather_src, _gather_dst, after)
        full.update(zip(gather_groups[k], arrays))

    gathered(0, token)
    proj, u = _in_proj_fwd(x, row(w["mix_pre_g"]), full["w_in"])
    gathered(1, proj)
    cv, y_conv = _conv_fwd(proj, full["dw_kernel"], row(w["dw_bias"]), row(w["conv_ln_g"]), row(w["conv_ln_b"]),
                           full["w_conv_out"])
    gathered(2, y_conv)
    z, zl, y_pool, merged, o, h1 = _pool_merge_fwd(proj, y_conv, x, full["pool_w"], row(w["pool_scale"]),
                                                   full["w_pool_out"], full["w_o"], row(w["mix_post_g"]))
    gathered(3, h1)
    vv, dm, dh2, sse, g_mlp_post = _mlp_fwd(h1, row(w["mlp_pre_g"]), full["w_ff1"], full["w_ff2"],
                                            row(w["mlp_post_g"]), loss_target)

    shard_of = {name: shard for name, _, shard, _ in SHARDED}

    def scatter_start(tag, group, grads):
        landings = [lax.empty((N_DEV - 1,) + shard_of[n], g.dtype) for n, g in zip(group, grads)]
        (handle,), tok = _exchange_start("scatter_start_" + tag, [(group, grads, landings)], _scatter_src, _scatter_dst)
        return handle, tok

    dv, g_ff1, g_ff2 = _mlp_bwd(vv, dm, full["w_ff1"], full["w_ff2"])
    h_ff, tok_ff = scatter_start("ff", ("w_ff1", "w_ff2"), [g_ff1, g_ff2])
    dh1, dy_conv, dy_pool, dgates, g_wo, g_mlp_pre, g_mix_post = _merge_bwd(
        dh2, dv, h1, row(w["mlp_pre_g"]), o, row(w["mix_post_g"]), full["w_o"], merged, proj, y_conv, y_pool, tok_ff)
    dp, g_wpo, g_pw, g_pool_scale = _pool_bwd(dy_pool, zl, z, full["pool_w"], row(w["pool_scale"]), full["w_pool_out"])
    d_glu, g_wco, g_dw, g_bias, g_ln_g, g_ln_b = _conv_bwd(dy_conv, cv, proj, full["dw_kernel"], row(w["conv_ln_g"]),
                                                            row(w["conv_ln_b"]), full["w_conv_out"])
    h_mix, tok_mix = scatter_start("mix", ("w_o", "w_pool_out", "pool_w", "w_conv_out", "dw_kernel"),
                                   [g_wo, g_wpo, g_pw, g_wco, g_dw[:, :TAPS, :]])
    g_win = _in_proj_bwd_w(u, d_glu, dp, dgates, tok_mix)
    h_in, tok_in = scatter_start("in", ("w_in",), [g_win])
    grad_x, g_mix_pre = _in_proj_bwd_x(d_glu, dp, dgates, full["w_in"], x, row(w["mix_pre_g"]), dh1, tok_in)

    own, parts = {}, {}
    for tag, handle in (("ff", h_ff), ("mix", h_mix), ("in", h_in)):
        mine, landed = _exchange_wait("scatter_wait_" + tag, handle, _scatter_src, _scatter_dst, grad_x)
        own.update(zip(handle[0], mine))
        parts.update(zip(handle[0], landed))
    vec_grads = jnp.concatenate(
        [jnp.pad(g, ((0, 7), (0, 0))) for g in
         (g_mix_pre, g_bias, g_ln_g, g_ln_b, g_pool_scale, g_mix_post, g_mlp_pre, g_mlp_post)], axis=0)
    vec_parts = _vec_exchange(vec_grads)

    out = {}
    slot = jnp.reshape(_my_slot(), (1,)).astype(jnp.int32)
    for name in names:
        out[name] = _adamw(name, own[name], parts[name], w[name], m[name], v[name], slot)
    stack = lambda d: jnp.stack([d[n] for n in VEC_NAMES], axis=0)
    vec_parts = vec_parts.reshape(N_DEV, 8, 8, D)[:, :, 0, :]
    res = _adamw_vectors(vec_parts, stack(w), stack(m), stack(v))
    for k, name in enumerate(VEC_NAMES):
        out[name] = [r[k] for r in res]

    loss = lax.psum(sse[0, 0] * (0.5 / D), ("x", "y", "c"))
    return loss, grad_x, out


def kernel(x, mix_pre_g, w_in, dw_kernel, dw_bias, conv_ln_g, conv_ln_b, w_conv_out, pool_w, pool_scale, w_pool_out, w_o, mix_post_g, mlp_pre_g, w_ff1, w_ff2, mlp_post_g, loss_target, m_mix_pre_g, m_w_in, m_dw_kernel, m_dw_bias, m_conv_ln_g, m_conv_ln_b, m_w_conv_out, m_pool_w, m_pool_scale, m_w_pool_out, m_w_o, m_mix_post_g, m_mlp_pre_g, m_w_ff1, m_w_ff2, m_mlp_post_g, v_mix_pre_g, v_w_in, v_dw_kernel, v_dw_bias, v_conv_ln_g, v_conv_ln_b, v_w_conv_out, v_pool_w, v_pool_scale, v_w_pool_out, v_w_o, v_mix_post_g, v_mlp_pre_g, v_w_ff1, v_w_ff2, v_mlp_post_g):
    w = dict(mix_pre_g=mix_pre_g, w_in=w_in, dw_kernel=dw_kernel, dw_bias=dw_bias, conv_ln_g=conv_ln_g, conv_ln_b=conv_ln_b,
             w_conv_out=w_conv_out, pool_w=pool_w, pool_scale=pool_scale, w_pool_out=w_pool_out, w_o=w_o,
             mix_post_g=mix_post_g, mlp_pre_g=mlp_pre_g, w_ff1=w_ff1, w_ff2=w_ff2, mlp_post_g=mlp_post_g)
    m = dict(mix_pre_g=m_mix_pre_g, w_in=m_w_in, dw_kernel=m_dw_kernel, dw_bias=m_dw_bias, conv_ln_g=m_conv_ln_g,
             conv_ln_b=m_conv_ln_b, w_conv_out=m_w_conv_out, pool_w=m_pool_w, pool_scale=m_pool_scale,
             w_pool_out=m_w_pool_out, w_o=m_w_o, mix_post_g=m_mix_post_g, mlp_pre_g=m_mlp_pre_g, w_ff1=m_w_ff1,
             w_ff2=m_w_ff2, mlp_post_g=m_mlp_post_g)
    v = dict(mix_pre_g=v_mix_pre_g, w_in=v_w_in, dw_kernel=v_dw_kernel, dw_bias=v_dw_bias, conv_ln_g=v_conv_ln_g,
             conv_ln_b=v_conv_ln_b, w_conv_out=v_w_conv_out, pool_w=v_pool_w, pool_scale=v_pool_scale,
             w_pool_out=v_w_pool_out, w_o=v_w_o, mix_post_g=v_mix_post_g, mlp_pre_g=v_mlp_pre_g, w_ff1=v_w_ff1,
             w_ff2=v_w_ff2, mlp_post_g=v_mlp_post_g)
    seq = x.shape[1]
    loss, grad_x, out = _step(x.reshape(seq, D), loss_target.reshape(seq, D), w, m, v)
    grads, deltas, new_m, new_v = ([out[n][k] for n in WEIGHT_ORDER] for k in range(4))
    return (loss, grad_x.reshape(x.shape), *grads, *deltas, *new_m, *new_v)
```

```python
import functools

import jax
import jax.numpy as jnp
from jax import lax
from jax.experimental import pallas as pl
from jax.experimental.pallas import tpu as pltpu

D = 1024
D_FF = 4 * D
IN_COLS = 5 * D
TAPS = 31
HALO = 32
POOL_WINDOWS = (2, 4, 8, 16)
PG = D // 4
N_DEV = 8
RMS_EPS = 1e-6
LN_EPS = 1e-5
ADAM_LR, ADAM_B1, ADAM_B2, ADAM_EPS, ADAM_WD, ADAM_STEP = 0.001, 0.9, 0.999, 1e-08, 0.01, 10

BF16 = jnp.bfloat16
F32 = jnp.float32
MIB = 1 << 20
MESH = pl.DeviceIdType.MESH


def _params(sem, vmem_mib):
    return pltpu.CompilerParams(dimension_semantics=sem, vmem_limit_bytes=vmem_mib * MIB)


def _dot(a, b):
    return jnp.dot(a, b, preferred_element_type=F32)


def _dot_nt(a, b):
    return lax.dot_general(a, b, (((1,), (1,)), ((), ())), preferred_element_type=F32)


def _dot_tn(a, b):
    return lax.dot_general(a, b, (((0,), (0,)), ((), ())), preferred_element_type=F32)


def _rms_scale(x):
    return lax.rsqrt(jnp.mean(x * x, axis=-1, keepdims=True) + RMS_EPS)


def _rms_bwd(dy, x, g):
    xn = x * _rms_scale(x)
    dn = dy * g
    dx = _rms_scale(x) * (dn - xn * jnp.mean(dn * xn, axis=-1, keepdims=True))
    return dx, jnp.sum(dy * xn, axis=0, keepdims=True)


def _sigmoid(x):
    return jax.nn.sigmoid(x)


def _acc_out(ref, first, value):
    @pl.when(first)
    def _():
        ref[...] = value

    @pl.when(jnp.logical_not(first))
    def _():
        ref[...] += value


def _my_slot():
    return 4 * lax.axis_index("x") + 2 * lax.axis_index("y") + lax.axis_index("c")


def _peer(mask):
    x, y, c = lax.axis_index("x"), lax.axis_index("y"), lax.axis_index("c")
    return (x ^ ((mask >> 2) & 1), y ^ ((mask >> 1) & 1), c ^ (mask & 1))


def _cols(width):
    return lambda ref, slot: ref.at[:, pl.ds(pl.multiple_of(slot * width, 128), width)]


def _rows(height):
    return lambda ref, slot: ref.at[pl.ds(pl.multiple_of(slot * height, 8), height), :]


def _lead(ref, slot):
    return ref.at[slot]


def _pool_rows(ref, slot):
    return ref.at[:, pl.ds(pl.multiple_of(slot * (PG // N_DEV), 8), PG // N_DEV), :]


SHARDED = (
    ("w_in", (D, IN_COLS), (D, IN_COLS // N_DEV), _cols(IN_COLS // N_DEV)),
    ("w_ff1", (D, D_FF), (D, D_FF // N_DEV), _cols(D_FF // N_DEV)),
    ("w_ff2", (D_FF, D), (D_FF // N_DEV, D), _rows(D_FF // N_DEV)),
    ("w_conv_out", (D, D), (D // N_DEV, D), _rows(D // N_DEV)),
    ("w_pool_out", (D, D), (D // N_DEV, D), _rows(D // N_DEV)),
    ("w_o", (D, D), (D // N_DEV, D), _rows(D // N_DEV)),
    ("pool_w", (4, PG, PG), (4, PG // N_DEV, PG), _pool_rows),
    ("dw_kernel", (N_DEV, TAPS, D // N_DEV), (TAPS, D // N_DEV), _lead),
)
N_SHARDED = len(SHARDED)


SHARD_AT = {name: at for name, _, _, at in SHARDED}
HBM = pl.BlockSpec(memory_space=pltpu.HBM)
SEM = pl.BlockSpec(memory_space=pltpu.SEMAPHORE)
EFFECT = pltpu.SideEffectType.DATAFLOW_SIDE_EFFECTING


def _in_hbm(a):
    return pltpu.with_memory_space_constraint(a, pltpu.HBM)


def _stage_shards(shards):
    dtypes = [F32 if name == "dw_kernel" else BF16 for name, *_ in SHARDED]

    def body(*refs):
        ins = refs[:N_SHARDED]
        stage = refs[N_SHARDED:2 * N_SHARDED]
        fulls = refs[2 * N_SHARDED:3 * N_SHARDED]
        sems = refs[3 * N_SHARDED]
        me = _my_slot()
        copies = []
        for a, (_, _, _, at) in enumerate(SHARDED):
            stage[a][...] = ins[a][...].astype(dtypes[a])
            cp = pltpu.make_async_copy(stage[a], at(fulls[a], me), sems.at[a])
            cp.start()
            copies.append(cp)
        for cp in copies:
            cp.wait()

    outs = pl.pallas_call(
        body, name="stage_shards",
        out_shape=[jax.ShapeDtypeStruct(shard, dt) for (_, _, shard, _), dt in zip(SHARDED, dtypes)]
        + [jax.ShapeDtypeStruct(full, dt) for (_, full, _, _), dt in zip(SHARDED, dtypes)],
        in_specs=[pl.BlockSpec(memory_space=pltpu.VMEM)] * N_SHARDED,
        out_specs=[pl.BlockSpec(memory_space=pltpu.VMEM)] * N_SHARDED + [pl.BlockSpec(memory_space=pl.ANY)] * N_SHARDED,
        scratch_shapes=[pltpu.SemaphoreType.DMA((N_SHARDED,))],
        compiler_params=pltpu.CompilerParams(vmem_limit_bytes=40 * MIB),
    )(*shards)
    return outs[:N_SHARDED], outs[N_SHARDED:]


def _sem_index(k, mask):
    return k * (N_DEV - 1) + mask - 1


def _exchange_start(name, groups, src_of, dst_of):
    sizes = [len(g[0]) for g in groups]
    names = [nm for g in groups for nm in g[0]]
    srcs = [s for g in groups for s in g[1]]
    lands = [l for g in groups for l in g[2]]
    n, n_g = len(names), len(groups)

    def body(*refs):
        src_refs, land_refs = refs[:n], refs[n:2 * n]
        sems = refs[2 * n:2 * n + 2 * n_g]
        token = refs[-1]
        me = _my_slot()
        first = 0
        for g, size in enumerate(sizes):
            for mask in range(1, N_DEV):
                for k in range(size):
                    a = first + k
                    pltpu.make_async_remote_copy(
                        src_ref=src_of(names[a], src_refs[a], me, mask), dst_ref=dst_of(names[a], land_refs[a], me, mask),
                        send_sem=sems[2 * g].at[_sem_index(k, mask)], recv_sem=sems[2 * g + 1].at[_sem_index(k, mask)],
                        device_id=_peer(mask), device_id_type=MESH).start()
            first += size
        token[...] = jnp.zeros_like(token)

    sem_shapes = [pltpu.SemaphoreType.DMA((size * (N_DEV - 1),)) for size in sizes for _ in range(2)]
    outs = pl.pallas_call(
        body, name=name,
        out_shape=sem_shapes + [pltpu.HBM(a.shape, a.dtype) for a in srcs + lands] + [jax.ShapeDtypeStruct((8, 128), F32)],
        in_specs=[HBM] * (2 * n),
        out_specs=[SEM] * (2 * n_g) + [HBM] * (2 * n) + [pl.BlockSpec(memory_space=pltpu.VMEM)],
        input_output_aliases={k: 2 * n_g + k for k in range(2 * n)},
        compiler_params=pltpu.CompilerParams(has_side_effects=EFFECT),
    )(*[_in_hbm(a) for a in srcs + lands])
    sems, thru, token = outs[:2 * n_g], outs[2 * n_g:-1], outs[-1]
    handles, a = [], 0
    for g, size in enumerate(sizes):
        handles.append((groups[g][0], sems[2 * g], sems[2 * g + 1], thru[a:a + size], thru[n + a:n + a + size]))
        a += size
    return handles, token


def _exchange_wait(name, handle, src_of, dst_of, after):
    names, send_sems, recv_sems, srcs, lands = handle
    n = len(names)

    def body(*refs):
        src_refs, land_refs = refs[:n], refs[n:2 * n]
        send_ref, recv_ref = refs[2 * n], refs[2 * n + 1]
        me = _my_slot()
        for mask in range(1, N_DEV):
            for a in range(n):
                cp = pltpu.make_async_remote_copy(
                    src_ref=src_of(names[a], src_refs[a], me, mask), dst_ref=dst_of(names[a], land_refs[a], me ^ mask, mask),
                    send_sem=send_ref.at[_sem_index(a, mask)], recv_sem=recv_ref.at[_sem_index(a, mask)],
                    device_id=_peer(mask), device_id_type=MESH)
                cp.wait_send()
                cp.wait_recv()

    outs = pl.pallas_call(
        body, name=name,
        out_shape=[pltpu.HBM(a.shape, a.dtype) for a in list(srcs) + list(lands)],
        in_specs=[HBM] * (2 * n) + [SEM, SEM, pl.BlockSpec(memory_space=pl.ANY)],
        out_specs=[HBM] * (2 * n),
        input_output_aliases={k: k for k in range(2 * n)},
        compiler_params=pltpu.CompilerParams(has_side_effects=EFFECT),
    )(*srcs, *lands, send_sems, recv_sems, after)
    return outs[:n], outs[n:]


def _gather_src(name, ref, me, mask):
    return ref


def _gather_dst(name, ref, origin, mask):
    return SHARD_AT[name](ref, origin)


def _scatter_src(name, ref, me, mask):
    return SHARD_AT[name](ref, me ^ mask)


def _scatter_dst(name, ref, origin, mask):
    return ref.at[mask - 1]


def _vec_exchange(vec_grads):
    def body(vec, vec_out, send_sems, recv_sems, local_sem):
        me = _my_slot()
        local = pltpu.make_async_copy(vec, vec_out.at[me], local_sem)
        local.start()
        sends = []
        for mask in range(1, N_DEV):
            cp = pltpu.make_async_remote_copy(
                src_ref=vec, dst_ref=vec_out.at[me], send_sem=send_sems.at[mask - 1],
                recv_sem=recv_sems.at[mask - 1], device_id=_peer(mask), device_id_type=MESH)
            cp.start()
            sends.append(cp)
        for mask in range(1, N_DEV):
            pltpu.make_async_remote_copy(
                src_ref=vec, dst_ref=vec_out.at[me ^ mask], send_sem=send_sems.at[mask - 1],
                recv_sem=recv_sems.at[mask - 1], device_id=_peer(mask), device_id_type=MESH).wait_recv()
        for cp in sends:
            cp.wait_send()
        local.wait()

    return pl.pallas_call(
        body, name="vec_exchange",
        out_shape=jax.ShapeDtypeStruct((N_DEV,) + vec_grads.shape, F32),
        in_specs=[pl.BlockSpec(memory_space=pl.ANY)],
        out_specs=pl.BlockSpec(memory_space=pl.ANY),
        scratch_shapes=[pltpu.SemaphoreType.DMA((N_DEV - 1,)), pltpu.SemaphoreType.DMA((N_DEV - 1,)),
                        pltpu.SemaphoreType.DMA],
    )(vec_grads)


def _adamw_update(g, w_ref, m_ref, v_ref, g_out, d_out, m_out, v_out):
    m_new = ADAM_B1 * m_ref[...] + (1.0 - ADAM_B1) * g
    v_new = ADAM_B2 * v_ref[...] + (1.0 - ADAM_B2) * (g * g)
    m_hat = m_new / (1.0 - ADAM_B1 ** ADAM_STEP)
    v_hat = v_new / (1.0 - ADAM_B2 ** ADAM_STEP)
    g_out[...] = g
    d_out[...] = -ADAM_LR * (m_hat / (jnp.sqrt(v_hat) + ADAM_EPS) + ADAM_WD * w_ref[...])
    m_out[...] = m_new
    v_out[...] = v_new


ADAMW_ROWS = 256


def _adamw(name, own, parts, w, m, v, slot):
    shard = w.shape
    if name in ("w_in", "w_ff1"):
        tr = ADAMW_ROWS
        grid = (shard[0] // tr,)
        own_spec = pl.BlockSpec((tr, shard[1]), lambda i, s: (i, s[0]))
        blk = pl.BlockSpec((tr, shard[1]), lambda i, s: (i, 0))
        parts_spec = pl.BlockSpec((N_DEV - 1, tr, shard[1]), lambda i, s: (0, i, 0))
    elif name == "pool_w":
        grid = (shard[0],)
        own_spec = pl.BlockSpec((None,) + shard[1:], lambda g, s: (g, s[0], 0))
        blk = pl.BlockSpec((None,) + shard[1:], lambda g, s: (g, 0, 0))
        parts_spec = pl.BlockSpec((N_DEV - 1, None) + shard[1:], lambda g, s: (0, g, 0, 0))
    elif name == "dw_kernel":
        grid = (1,)
        own_spec = pl.BlockSpec((None,) + shard, lambda i, s: (s[0], 0, 0))
        blk = pl.BlockSpec(shard, lambda i, s: (0, 0))
        parts_spec = pl.BlockSpec((N_DEV - 1,) + shard, lambda i, s: (0, 0, 0))
    else:
        tr = min(ADAMW_ROWS, shard[0])
        grid = (shard[0] // tr,)
        own_spec = pl.BlockSpec((tr, shard[1]), lambda i, s: (s[0] * grid[0] + i, 0))
        blk = pl.BlockSpec((tr, shard[1]), lambda i, s: (i, 0))
        parts_spec = pl.BlockSpec((N_DEV - 1, tr, shard[1]), lambda i, s: (0, i, 0))

    def body(slot_ref, own_ref, p_ref, w_ref, m_ref, v_ref, g_out, d_out, m_out, v_out):
        g = own_ref[...].astype(F32)
        for k in range(N_DEV - 1):
            g = g + p_ref[k].astype(F32)
        _adamw_update(g, w_ref, m_ref, v_ref, g_out, d_out, m_out, v_out)

    return pl.pallas_call(
        body, name="adamw_" + name,
        grid_spec=pltpu.PrefetchScalarGridSpec(
            num_scalar_prefetch=1, grid=grid, in_specs=[own_spec, parts_spec, blk, blk, blk], out_specs=[blk] * 4),
        out_shape=[jax.ShapeDtypeStruct(shard, F32)] * 4,
        compiler_params=_params(("arbitrary",), 32),
    )(slot, own, parts, w, m, v)


def _adamw_vectors(parts, w, m, v):
    def body(p_ref, w_ref, m_ref, v_ref, g_out, d_out, m_out, v_out):
        g = p_ref[0]
        for s in range(1, N_DEV):
            g = g + p_ref[s]
        _adamw_update(g, w_ref, m_ref, v_ref, g_out, d_out, m_out, v_out)

    return pl.pallas_call(
        body, name="adamw_vectors",
        out_shape=[jax.ShapeDtypeStruct(w.shape, F32)] * 4,
    )(parts, w, m, v)


def _in_proj_fwd(x, g_pre, w_in):
    seq = x.shape[0]
    tm, tn = 1024, IN_COLS // 4

    def body(x_ref, g_ref, w_ref, proj_ref, u_ref):
        @pl.when(pl.program_id(1) == 0)
        def _():
            xf = x_ref[...]
            u_ref[...] = (xf * _rms_scale(xf) * g_ref[...]).astype(BF16)
        proj_ref[...] = _dot(u_ref[...], w_ref[...])

    return pl.pallas_call(
        body, name="in_proj_fwd", grid=(seq // tm, IN_COLS // tn),
        in_specs=[pl.BlockSpec((tm, D), lambda i, j: (i, 0)), pl.BlockSpec((1, D), lambda i, j: (0, 0)),
                  pl.BlockSpec((D, tn), lambda i, j: (0, j))],
        out_specs=[pl.BlockSpec((tm, tn), lambda i, j: (i, j)), pl.BlockSpec((tm, D), lambda i, j: (i, 0))],
        out_shape=[jax.ShapeDtypeStruct((seq, IN_COLS), F32), jax.ShapeDtypeStruct((seq, D), BF16)],
        compiler_params=_params(("arbitrary", "arbitrary"), 40),
    )(x, g_pre, w_in)


CONV_TM = 256
CONV_RS = 64


def _shifts():
    return [(b, [(a, 8 * a + b) for a in range(4) if 8 * a + b < TAPS]) for b in range(8)]


def _taps_looking_back(buf, row0, lanes, weight):
    acc = None
    for b, group in _shifts():
        part = None
        for a, s in group:
            term = weight(TAPS - 1 - s) * buf[pl.ds(row0 - 8 - 8 * a, CONV_RS + 8), lanes]
            part = term if part is None else part + term
        if b:
            part = pltpu.roll(part, b, 0)
        acc = part[8:, :] if acc is None else acc + part[8:, :]
    return acc


def _taps_looking_ahead(buf, row0, lanes, weight):
    acc = None
    for b, group in _shifts():
        part = None
        for a, s in group:
            term = weight(TAPS - 1 - s) * buf[pl.ds(row0 + 8 * a, CONV_RS + 8), lanes]
            part = term if part is None else part + term
        if b:
            part = pltpu.roll(part, CONV_RS + 8 - b, 0)
        acc = part[:CONV_RS, :] if acc is None else acc + part[:CONV_RS, :]
    return acc


def _layer_norm_parts(cv):
    mu = jnp.mean(cv, axis=-1, keepdims=True)
    cen = cv - mu
    rstd = lax.rsqrt(jnp.mean(cen * cen, axis=-1, keepdims=True) + LN_EPS)
    return cen * rstd, rstd


def _conv_fwd(proj, dw, dw_bias, ln_g, ln_b, w_conv_out):
    seq = proj.shape[0]
    tm = CONV_TM

    def body(a_ref, gate_ref, dw_ref, bias_ref, lg_ref, lb_ref, w_ref, cv_ref, y_ref, ug):
        i = pl.program_id(0)

        @pl.when(i == 0)
        def _():
            ug[0:HALO, :] = jnp.zeros((HALO, D), F32)

        @pl.when(i > 0)
        def _():
            ug[0:HALO, :] = ug[tm:tm + HALO, :]
        ug[HALO:HALO + tm, :] = a_ref[...] * _sigmoid(gate_ref[...])

        def channel_block(cb, carry):
            lanes = pl.ds(pl.multiple_of(cb * 128, 128), 128)
            for r0 in range(0, tm, CONV_RS):
                taps = _taps_looking_back(ug, HALO + r0, lanes, lambda k: dw_ref[cb, k:k + 1, :])
                cv_ref[pl.ds(r0, CONV_RS), lanes] = taps + bias_ref[:, lanes]
            return carry
        lax.fori_loop(0, D // 128, channel_block, 0)

        n, _ = _layer_norm_parts(cv_ref[...])
        ln = n * lg_ref[...] + lb_ref[...]
        y_ref[...] = _dot((ln * _sigmoid(ln)).astype(BF16), w_ref[...])

    vec = pl.BlockSpec((1, D), lambda i: (0, 0))
    tile = pl.BlockSpec((tm, D), lambda i: (i, 0))
    return pl.pallas_call(
        body, name="conv_fwd", grid=(seq // tm,),
        in_specs=[pl.BlockSpec((tm, D), lambda i: (i, 0)), pl.BlockSpec((tm, D), lambda i: (i, 1)),
                  pl.BlockSpec((N_DEV, TAPS, 128), lambda i: (0, 0, 0)), vec, vec, vec,
                  pl.BlockSpec((D, D), lambda i: (0, 0))],
        out_specs=[tile, tile],
        out_shape=[jax.ShapeDtypeStruct((seq, D), F32)] * 2,
        scratch_shapes=[pltpu.VMEM((HALO + tm, D), F32)],
        compiler_params=_params(("arbitrary",), 32),
    )(proj, proj, dw, dw_bias, ln_g, ln_b, w_conv_out)


def _pool_counts(tile_index, tm, window):
    t = tile_index * tm + lax.broadcasted_iota(jnp.int32, (tm, 1), 0)
    return 1.0 / jnp.minimum(t + 1, window).astype(F32)


def _pool_merge_fwd(proj, y_conv, x, pool_w, pool_scale, w_pool_out, w_o, g_post):
    seq = proj.shape[0]
    tm = CONV_TM

    def body(p_ref, gc_ref, gp_ref, yc_ref, x_ref, pw_ref, ps_ref, wpo_ref, wo_ref, g_ref,
             z_ref, zl_ref, yp_ref, mg_ref, o_ref, h1_ref, pbuf):
        i = pl.program_id(0)

        @pl.when(i == 0)
        def _():
            pbuf[0:HALO, :] = jnp.zeros((HALO, D), F32)

        @pl.when(i > 0)
        def _():
            pbuf[0:HALO, :] = pbuf[tm:tm + HALO, :]
        pbuf[HALO:HALO + tm, :] = p_ref[...]

        for g, window in enumerate(POOL_WINDOWS):
            lanes = pl.ds(g * PG, PG)
            acc = pbuf[pl.ds(HALO, tm), lanes]
            for j in range(1, window):
                acc = acc + pbuf[pl.ds(HALO - j, tm), lanes]
            zg = acc * _pool_counts(i, tm, window) - pbuf[pl.ds(HALO, tm), lanes]
            z_ref[:, lanes] = zg.astype(BF16)
            zl_ref[:, lanes] = _dot(zg.astype(BF16), pw_ref[g])
        zl = zl_ref[...]
        y_pool = _dot((zl * ps_ref[...]).astype(BF16), wpo_ref[...])
        yp_ref[...] = y_pool
        merged = (_sigmoid(gc_ref[...]) * yc_ref[...] + _sigmoid(gp_ref[...]) * y_pool).astype(BF16)
        mg_ref[...] = merged
        o = _dot(merged, wo_ref[...])
        o_ref[...] = o
        h1_ref[...] = x_ref[...] + o * _rms_scale(o) * g_ref[...]

    vec = pl.BlockSpec((1, D), lambda i: (0, 0))
    tile = pl.BlockSpec((tm, D), lambda i: (i, 0))
    mat = pl.BlockSpec((D, D), lambda i: (0, 0))
    return pl.pallas_call(
        body, name="pool_merge_fwd", grid=(seq // tm,),
        in_specs=[pl.BlockSpec((tm, D), lambda i: (i, 2)), pl.BlockSpec((tm, D), lambda i: (i, 3)),
                  pl.BlockSpec((tm, D), lambda i: (i, 4)), tile, tile,
                  pl.BlockSpec((4, PG, PG), lambda i: (0, 0, 0)), vec, mat, mat, vec],
        out_specs=[tile] * 6,
        out_shape=[jax.ShapeDtypeStruct((seq, D), dt) for dt in (BF16, F32, F32, BF16, F32, F32)],
        scratch_shapes=[pltpu.VMEM((HALO + tm, D), F32)],
        compiler_params=_params(("arbitrary",), 48),
    )(proj, proj, proj, y_conv, x, pool_w, pool_scale, w_pool_out, w_o, g_post)


def _mlp_fwd(h1, g_pre, w_ff1, w_ff2, g_post, target):
    seq = h1.shape[0]
    tm, tf = 1024, D_FF // N_DEV
    n_f = D_FF // tf

    def body(h1_ref, gpre_ref, w1_ref, w2_ref, gpost_ref, tgt_ref, v_ref, dm_ref, dh2_ref, sse_ref, ggrad_ref, macc):
        i, j = pl.program_id(0), pl.program_id(1)

        @pl.when(j == 0)
        def _():
            h = h1_ref[...]
            v_ref[...] = (h * _rms_scale(h) * gpre_ref[...]).astype(BF16)
        f = jnp.maximum(_dot(v_ref[...], w1_ref[...]), 0.0)
        part = _dot((f * f).astype(BF16), w2_ref[...])

        @pl.when(j == 0)
        def _():
            macc[...] = part

        @pl.when(j > 0)
        def _():
            macc[...] += part

        @pl.when(j == n_f - 1)
        def _():
            mo = macc[...]
            err = h1_ref[...] + mo * _rms_scale(mo) * gpost_ref[...] - tgt_ref[...]
            dh2 = err * (1.0 / D)
            dh2_ref[...] = dh2
            dm, ggrad = _rms_bwd(dh2, mo, gpost_ref[...])
            dm_ref[...] = dm.astype(BF16)
            _acc_out(ggrad_ref, i == 0, ggrad)
            _acc_out(sse_ref, i == 0, jnp.sum(jnp.sum(err * err, axis=1, keepdims=True), axis=0, keepdims=True))

    vec = pl.BlockSpec((1, D), lambda i, j: (0, 0))
    tile = pl.BlockSpec((tm, D), lambda i, j: (i, 0))
    return pl.pallas_call(
        body, name="mlp_fwd", grid=(seq // tm, n_f),
        in_specs=[tile, vec, pl.BlockSpec((D, tf), lambda i, j: (0, j)), pl.BlockSpec((tf, D), lambda i, j: (j, 0)), vec, tile],
        out_specs=[tile, tile, tile, pl.BlockSpec((1, 1), lambda i, j: (0, 0)), vec],
        out_shape=[jax.ShapeDtypeStruct((seq, D), BF16), jax.ShapeDtypeStruct((seq, D), BF16),
                   jax.ShapeDtypeStruct((seq, D), F32), jax.ShapeDtypeStruct((1, 1), F32),
                   jax.ShapeDtypeStruct((1, D), F32)],
        scratch_shapes=[pltpu.VMEM((tm, D), F32)],
        compiler_params=_params(("arbitrary", "arbitrary"), 56),
    )(h1, g_pre, w_ff1, w_ff2, g_post, target)


def _mlp_bwd(v, dm, w_ff1, w_ff2):
    seq = v.shape[0]
    tm, tf = 1024, D_FF // N_DEV
    n_t = seq // tm

    def body(v_ref, dm_ref, w1_ref, w2_ref, dv_hbm, g1_ref, g2_ref, dv_acc, g1_acc, g2_acc, sem):
        j, i = pl.program_id(0), pl.program_id(1)
        vt, dmt = v_ref[...], dm_ref[...]
        f = jnp.maximum(_dot(vt, w1_ref[...]), 0.0)
        df = (_dot_nt(dmt, w2_ref[...]) * (2.0 * f)).astype(BF16)
        rows = pl.ds(pl.multiple_of(i * tm, tm), tm)
        dv_part = _dot_nt(df, w1_ref[...])

        @pl.when(j == 0)
        def _():
            dv_acc[rows, :] = dv_part

        @pl.when(j > 0)
        def _():
            dv_acc[rows, :] += dv_part
        g1_part = _dot_tn(vt, df)
        g2_part = _dot_tn((f * f).astype(BF16), dmt)

        @pl.when(i == 0)
        def _():
            g1_acc[...] = g1_part
            g2_acc[...] = g2_part

        @pl.when(i > 0)
        def _():
            g1_acc[...] += g1_part
            g2_acc[...] += g2_part

        @pl.when(i == n_t - 1)
        def _():
            g1_ref[...] = g1_acc[...].astype(BF16)
            g2_ref[...] = g2_acc[...].astype(BF16)

        @pl.when(jnp.logical_and(i == n_t - 1, j == D_FF // tf - 1))
        def _():
            cp = pltpu.make_async_copy(dv_acc, dv_hbm, sem)
            cp.start()
            cp.wait()

    tile = pl.BlockSpec((tm, D), lambda j, i: (i, 0))
    return pl.pallas_call(
        body, name="mlp_bwd", grid=(D_FF // tf, n_t),
        in_specs=[tile, tile, pl.BlockSpec((D, tf), lambda j, i: (0, j)), pl.BlockSpec((tf, D), lambda j, i: (j, 0))],
        out_specs=[pl.BlockSpec(memory_space=pl.ANY), pl.BlockSpec((D, tf), lambda j, i: (0, j)),
                   pl.BlockSpec((tf, D), lambda j, i: (j, 0))],
        out_shape=[jax.ShapeDtypeStruct((seq, D), F32), jax.ShapeDtypeStruct((D, D_FF), BF16),
                   jax.ShapeDtypeStruct((D_FF, D), BF16)],
        scratch_shapes=[pltpu.VMEM((seq, D), F32), pltpu.VMEM((D, tf), F32), pltpu.VMEM((tf, D), F32),
                        pltpu.SemaphoreType.DMA],
        compiler_params=_params(("arbitrary", "arbitrary"), 52),
    )(v, dm, w_ff1, w_ff2)


ANY = pl.BlockSpec(memory_space=pl.ANY)


def _merge_bwd(dh2, dv, h1, g_mlp_pre, o, g_mix_post, w_o, merged, proj, y_conv, y_pool, token):
    seq = dh2.shape[0]
    tm = 256
    n_t = seq // tm

    def body(dh2_ref, dv_ref, h1_ref, gpre_ref, o_ref, gpost_ref, wo_ref, mg_ref, gc_ref, gp_ref, yc_ref, yp_ref, _token,
             dh1_ref, dyc_ref, dyp_ref, dg_ref, gwo_ref, ggpre_ref, ggpost_ref, gwo_acc):
        i = pl.program_id(0)
        dnorm, ggpre = _rms_bwd(dv_ref[...], h1_ref[...], gpre_ref[...])
        dh1 = dh2_ref[...] + dnorm
        dh1_ref[...] = dh1
        do, ggpost = _rms_bwd(dh1, o_ref[...], gpost_ref[...])
        do = do.astype(BF16)
        _acc_out(ggpre_ref, i == 0, ggpre)
        _acc_out(ggpost_ref, i == 0, ggpost)
        _acc_out(gwo_acc, i == 0, _dot_tn(mg_ref[...], do))
        dmerged = _dot_nt(do, wo_ref[...])
        sc, sp = _sigmoid(gc_ref[...]), _sigmoid(gp_ref[...])
        dyc_ref[...] = (dmerged * sc).astype(BF16)
        dyp_ref[...] = (dmerged * sp).astype(BF16)
        dg_ref[:, 0:D] = (dmerged * yc_ref[...] * (sc * (1.0 - sc))).astype(BF16)
        dg_ref[:, D:2 * D] = (dmerged * yp_ref[...] * (sp * (1.0 - sp))).astype(BF16)

        @pl.when(i == n_t - 1)
        def _():
            gwo_ref[...] = gwo_acc[...].astype(BF16)

    vec = pl.BlockSpec((1, D), lambda i: (0, 0))
    tile = pl.BlockSpec((tm, D), lambda i: (i, 0))
    mat = pl.BlockSpec((D, D), lambda i: (0, 0))
    return pl.pallas_call(
        body, name="merge_bwd", grid=(n_t,),
        in_specs=[tile, tile, tile, vec, tile, vec, mat, tile,
                  pl.BlockSpec((tm, D), lambda i: (i, 3)), pl.BlockSpec((tm, D), lambda i: (i, 4)), tile, tile, ANY],
        out_specs=[tile, tile, tile, pl.BlockSpec((tm, 2 * D), lambda i: (i, 0)), mat, vec, vec],
        out_shape=[jax.ShapeDtypeStruct((seq, D), F32), jax.ShapeDtypeStruct((seq, D), BF16),
                   jax.ShapeDtypeStruct((seq, D), BF16), jax.ShapeDtypeStruct((seq, 2 * D), BF16),
                   jax.ShapeDtypeStruct((D, D), BF16), jax.ShapeDtypeStruct((1, D), F32),
                   jax.ShapeDtypeStruct((1, D), F32)],
        scratch_shapes=[pltpu.VMEM((D, D), F32)],
        compiler_params=_params(("arbitrary",), 48),
    )(dh2, dv, h1, g_mlp_pre, o, g_mix_post, w_o, merged, proj, proj, y_conv, y_pool, token)


def _pool_bwd(dy_pool, zl, z, pool_w, pool_scale, w_pool_out):
    seq = dy_pool.shape[0]
    tm = CONV_TM
    n_t = seq // tm

    def body(dy_ref, zl_ref, z_ref, pw_ref, ps_ref, wpo_ref, dp_ref, gwpo_ref, gpw_ref, gps_ref, qbuf, gwpo_acc, gpw_acc):
        i = pl.program_id(0)
        tile_index = n_t - 1 - i
        first = i == 0
        dy = dy_ref[...]
        zl = zl_ref[...]
        dzs = _dot_nt(dy, wpo_ref[...])
        _acc_out(gwpo_acc, first, _dot_tn((zl * ps_ref[...]).astype(BF16), dy))
        _acc_out(gps_ref, first, jnp.sum(dzs * zl, axis=0, keepdims=True))
        dzl = (dzs * ps_ref[...]).astype(BF16)

        @pl.when(first)
        def _():
            qbuf[tm:tm + HALO, :] = jnp.zeros((HALO, D), F32)

        @pl.when(jnp.logical_not(first))
        def _():
            qbuf[tm:tm + HALO, :] = qbuf[0:HALO, :]

        dzs_list = []
        for g, window in enumerate(POOL_WINDOWS):
            lanes = pl.ds(g * PG, PG)
            dzl_g = dzl[:, g * PG:(g + 1) * PG]
            dz = _dot_nt(dzl_g, pw_ref[g])
            _acc_out(gpw_acc.at[g], first, _dot_tn(z_ref[:, lanes], dzl_g))
            qbuf[pl.ds(0, tm), lanes] = dz * _pool_counts(tile_index, tm, window)
            dzs_list.append(dz)
        for g, window in enumerate(POOL_WINDOWS):
            lanes = pl.ds(g * PG, PG)
            acc = qbuf[pl.ds(0, tm), lanes]
            for j in range(1, window):
                acc = acc + qbuf[pl.ds(j, tm), lanes]
            dp_ref[:, lanes] = (acc - dzs_list[g]).astype(BF16)

        @pl.when(i == n_t - 1)
        def _():
            gwpo_ref[...] = gwpo_acc[...].astype(BF16)
            gpw_ref[...] = gpw_acc[...].astype(BF16)

    vec = pl.BlockSpec((1, D), lambda i: (0, 0))
    tile = pl.BlockSpec((tm, D), lambda i: (n_t - 1 - i, 0))
    mat = pl.BlockSpec((D, D), lambda i: (0, 0))
    pw = pl.BlockSpec((4, PG, PG), lambda i: (0, 0, 0))
    return pl.pallas_call(
        body, name="pool_bwd", grid=(n_t,),
        in_specs=[tile, tile, tile, pw, vec, mat],
        out_specs=[tile, mat, pw, vec],
        out_shape=[jax.ShapeDtypeStruct((seq, D), BF16), jax.ShapeDtypeStruct((D, D), BF16),
                   jax.ShapeDtypeStruct((4, PG, PG), BF16), jax.ShapeDtypeStruct((1, D), F32)],
        scratch_shapes=[pltpu.VMEM((tm + HALO, D), F32), pltpu.VMEM((D, D), F32), pltpu.VMEM((4, PG, PG), F32)],
        compiler_params=_params(("arbitrary",), 40),
    )(dy_pool, zl, z, pool_w, pool_scale, w_pool_out)


def _conv_bwd(dy_conv, cv, proj, dw, ln_g, ln_b, w_conv_out):
    seq = dy_conv.shape[0]
    tm = CONV_TM
    n_t = seq // tm
    halo_blocks = tm // HALO

    def body(dy_ref, cv_ref, a_ref, gate_ref, ah_ref, gh_ref, dw_ref, lg_ref, lb_ref, w_ref,
             dglu_ref, gw_ref, gdw_ref, gbias_ref, glg_ref, glb_ref, ug, dcv, dug, gw_acc):
        i = pl.program_id(0)
        tile_index = n_t - 1 - i
        first = i == 0
        dy = dy_ref[...]
        n, rstd = _layer_norm_parts(cv_ref[...])
        ln = n * lg_ref[...] + lb_ref[...]
        sg = _sigmoid(ln)
        _acc_out(gw_acc, first, _dot_tn((ln * sg).astype(BF16), dy))
        dln = _dot_nt(dy, w_ref[...]) * (sg * (1.0 + ln * (1.0 - sg)))
        _acc_out(glg_ref, first, jnp.sum(dln * n, axis=0, keepdims=True))
        _acc_out(glb_ref, first, jnp.sum(dln, axis=0, keepdims=True))
        dn = dln * lg_ref[...]
        dcv_tile = rstd * (dn - jnp.mean(dn, axis=-1, keepdims=True) - n * jnp.mean(dn * n, axis=-1, keepdims=True))
        _acc_out(gbias_ref, first, jnp.sum(dcv_tile, axis=0, keepdims=True))

        @pl.when(first)
        def _():
            dcv[tm:tm + HALO, :] = jnp.zeros((HALO, D), F32)

        @pl.when(jnp.logical_not(first))
        def _():
            dcv[tm:tm + HALO, :] = dcv[0:HALO, :]
        dcv[0:tm, :] = dcv_tile

        a, gate = a_ref[...], gate_ref[...]
        sgate = _sigmoid(gate)
        ug[HALO:HALO + tm, :] = a * sgate
        before = jnp.where(tile_index > 0, 1.0, 0.0)
        ug[0:HALO, :] = ah_ref[...] * _sigmoid(gh_ref[...]) * before

        @pl.when(first)
        def _():
            gdw_ref[...] = jnp.zeros((N_DEV, TAPS + 1, 128), F32)

        def channel_block(cb, carry):
            lanes = pl.ds(pl.multiple_of(cb * 128, 128), 128)
            for r0 in range(0, tm, CONV_RS):
                dug[pl.ds(r0, CONV_RS), lanes] = _taps_looking_ahead(dcv, r0, lanes, lambda k: dw_ref[cb, k:k + 1, :])
            for b, group in _shifts():
                sums = [jnp.zeros((8, 128), F32) for _ in group]
                for r0 in range(0, tm, CONV_RS):
                    window = ug[pl.ds(r0, CONV_RS + HALO), lanes]
                    if b:
                        window = pltpu.roll(window, b, 0)
                    d = dcv[pl.ds(r0, CONV_RS), lanes]
                    for n_a, (a, s) in enumerate(group):
                        prod = d * window[HALO - 8 * a:HALO - 8 * a + CONV_RS, :]
                        sums[n_a] = sums[n_a] + jnp.sum(prod.reshape(CONV_RS // 8, 8, 128), axis=0)
                for n_a, (a, s) in enumerate(group):
                    k = TAPS - 1 - s
                    gdw_ref[cb, k:k + 1, :] += jnp.sum(sums[n_a], axis=0, keepdims=True)
            return carry
        lax.fori_loop(0, D // 128, channel_block, 0)

        d_ug = dug[...]
        dglu_ref[:, 0:D] = (d_ug * sgate).astype(BF16)
        dglu_ref[:, D:2 * D] = (d_ug * a * (sgate * (1.0 - sgate))).astype(BF16)

        @pl.when(i == n_t - 1)
        def _():
            gw_ref[...] = gw_acc[...].astype(BF16)

    def halo_index(col):
        return lambda i: (jnp.maximum((n_t - 1 - i) * halo_blocks - 1, 0), col)

    vec = pl.BlockSpec((1, D), lambda i: (0, 0))
    tile = pl.BlockSpec((tm, D), lambda i: (n_t - 1 - i, 0))
    mat = pl.BlockSpec((D, D), lambda i: (0, 0))
    dwspec = pl.BlockSpec((N_DEV, TAPS, 128), lambda i: (0, 0, 0))
    return pl.pallas_call(
        body, name="conv_bwd", grid=(n_t,),
        in_specs=[tile, tile, pl.BlockSpec((tm, D), lambda i: (n_t - 1 - i, 0)), pl.BlockSpec((tm, D), lambda i: (n_t - 1 - i, 1)),
                  pl.BlockSpec((HALO, D), halo_index(0)), pl.BlockSpec((HALO, D), halo_index(1)), dwspec, vec, vec, mat],
        out_specs=[pl.BlockSpec((tm, 2 * D), lambda i: (n_t - 1 - i, 0)), mat,
                   pl.BlockSpec((N_DEV, TAPS + 1, 128), lambda i: (0, 0, 0)), vec, vec, vec],
        out_shape=[jax.ShapeDtypeStruct((seq, 2 * D), BF16), jax.ShapeDtypeStruct((D, D), BF16),
                   jax.ShapeDtypeStruct((N_DEV, TAPS + 1, 128), F32), jax.ShapeDtypeStruct((1, D), F32),
                   jax.ShapeDtypeStruct((1, D), F32), jax.ShapeDtypeStruct((1, D), F32)],
        scratch_shapes=[pltpu.VMEM((HALO + tm, D), F32), pltpu.VMEM((tm + HALO, D), F32), pltpu.VMEM((tm, D), F32),
                        pltpu.VMEM((D, D), F32)],
        compiler_params=_params(("arbitrary",), 48),
    )(dy_conv, cv, proj, proj, proj, proj, dw, ln_g, ln_b, w_conv_out)


def _in_proj_bwd_x(d_glu, dp, dgates, w_in, x, g_pre, dh1, token):
    seq = x.shape[0]
    tm = 256

    def body(dglu_ref, dp_ref, dg_ref, w_ref, x_ref, g_ref, dh1_ref, _token, dx_ref, gg_ref):
        du = _dot_nt(dglu_ref[...], w_ref[:, 0:2 * D])
        du += _dot_nt(dp_ref[...], w_ref[:, 2 * D:3 * D])
        du += _dot_nt(dg_ref[...], w_ref[:, 3 * D:5 * D])
        dnorm, gg = _rms_bwd(du, x_ref[...], g_ref[...])
        dx_ref[...] = dh1_ref[...] + dnorm
        _acc_out(gg_ref, pl.program_id(0) == 0, gg)

    vec = pl.BlockSpec((1, D), lambda i: (0, 0))
    tile = pl.BlockSpec((tm, D), lambda i: (i, 0))
    wide = pl.BlockSpec((tm, 2 * D), lambda i: (i, 0))
    return pl.pallas_call(
        body, name="in_proj_bwd_x", grid=(seq // tm,),
        in_specs=[wide, tile, wide, pl.BlockSpec((D, IN_COLS), lambda i: (0, 0)), tile, vec, tile, ANY],
        out_specs=[tile, vec],
        out_shape=[jax.ShapeDtypeStruct((seq, D), F32), jax.ShapeDtypeStruct((1, D), F32)],
        compiler_params=_params(("arbitrary",), 48),
    )(d_glu, dp, dgates, w_in, x, g_pre, dh1, token)


def _in_proj_bwd_w(u, d_glu, dp, dgates, token):
    seq = u.shape[0]
    tm = 1024
    n_t = seq // tm

    def body(u_ref, dglu_ref, dp_ref, dg_ref, _token, out_ref, acc):
        b, i = pl.program_id(0), pl.program_id(1)
        ut = u_ref[...]

        def add(d_ref):
            _acc_out(acc, i == 0, _dot_tn(ut, d_ref[...]))

        pl.when(b < 2)(lambda: add(dglu_ref))
        pl.when(b == 2)(lambda: add(dp_ref))
        pl.when(b > 2)(lambda: add(dg_ref))

        @pl.when(i == n_t - 1)
        def _():
            out_ref[...] = acc[...].astype(BF16)

    return pl.pallas_call(
        body, name="in_proj_bwd_w", grid=(IN_COLS // D, n_t),
        in_specs=[pl.BlockSpec((tm, D), lambda b, i: (i, 0)),
                  pl.BlockSpec((tm, D), lambda b, i: (jnp.where(b < 2, i, 0), jnp.minimum(b, 1))),
                  pl.BlockSpec((tm, D), lambda b, i: (jnp.where(b == 2, i, 0), 0)),
                  pl.BlockSpec((tm, D), lambda b, i: (jnp.where(b > 2, i, 0), jnp.maximum(b - 3, 0))), ANY],
        out_specs=pl.BlockSpec((D, D), lambda b, i: (0, b)),
        out_shape=jax.ShapeDtypeStruct((D, IN_COLS), BF16),
        scratch_shapes=[pltpu.VMEM((D, D), F32)],
        compiler_params=_params(("arbitrary", "arbitrary"), 40),
    )(u, d_glu, dp, dgates, token)


VEC_NAMES = ("mix_pre_g", "dw_bias", "conv_ln_g", "conv_ln_b", "pool_scale", "mix_post_g", "mlp_pre_g", "mlp_post_g")
WEIGHT_ORDER = ("mix_pre_g", "w_in", "dw_kernel", "dw_bias", "conv_ln_g", "conv_ln_b", "w_conv_out", "pool_w",
                "pool_scale", "w_pool_out", "w_o", "mix_post_g", "mlp_pre_g", "w_ff1", "w_ff2", "mlp_post_g")


def _step(x, loss_target, w, m, v):
    row = lambda a: a.reshape(1, D)
    names = [s[0] for s in SHARDED]

    stage, seeded = _stage_shards([w[n] for n in names])
    stage, seeded = dict(zip(names, stage)), dict(zip(names, seeded))
    gather_groups = (("w_in",), ("w_conv_out", "dw_kernel"), ("pool_w", "w_pool_out", "w_o"), ("w_ff1", "w_ff2"))
    handles, token = _exchange_start(
        "gather_start", [(g, [stage[n] for n in g], [seeded[n] for n in g]) for g in gather_groups], _gather_src, _gather_dst)
    full = {}

    def gathered(k, after):
        _, arrays = _exchange_wait("gather_wait_" + gather_groups[k][0], handles[k], _gather_src, _gather_dst, after)
        full.update(zip(gather_groups[k], arrays))

    gathered(0, token)
    proj, u = _in_proj_fwd(x, row(w["mix_pre_g"]), full["w_in"])
    gathered(1, proj)
    cv, y_conv = _conv_fwd(proj, full["dw_kernel"], row(w["dw_bias"]), row(w["conv_ln_g"]), row(w["conv_ln_b"]),
                           full["w_conv_out"])
    gathered(2, y_conv)
    z, zl, y_pool, merged, o, h1 = _pool_merge_fwd(proj, y_conv, x, full["pool_w"], row(w["pool_scale"]),
                                                   full["w_pool_out"], full["w_o"], row(w["mix_post_g"]))
    gathered(3, h1)
    vv, dm, dh2, sse, g_mlp_post = _mlp_fwd(h1, row(w["mlp_pre_g"]), full["w_ff1"], full["w_ff2"],
                                            row(w["mlp_post_g"]), loss_target)

    shard_of = {name: shard for name, _, shard, _ in SHARDED}

    def scatter_start(tag, group, grads):
        landings = [lax.empty((N_DEV - 1,) + shard_of[n], g.dtype) for n, g in zip(group, grads)]
        (handle,), tok = _exchange_start("scatter_start_" + tag, [(group, grads, landings)], _scatter_src, _scatter_dst)
        return handle, tok

    dv, g_ff1, g_ff2 = _mlp_bwd(vv, dm, full["w_ff1"], full["w_ff2"])
    h_ff, tok_ff = scatter_start("ff", ("w_ff1", "w_ff2"), [g_ff1, g_ff2])
    dh1, dy_conv, dy_pool, dgates, g_wo, g_mlp_pre, g_mix_post = _merge_bwd(
        dh2, dv, h1, row(w["mlp_pre_g"]), o, row(w["mix_post_g"]), full["w_o"], merged, proj, y_conv, y_pool, tok_ff)
    dp, g_wpo, g_pw, g_pool_scale = _pool_bwd(dy_pool, zl, z, full["pool_w"], row(w["pool_scale"]), full["w_pool_out"])
    d_glu, g_wco, g_dw, g_bias, g_ln_g, g_ln_b = _conv_bwd(dy_conv, cv, proj, full["dw_kernel"], row(w["conv_ln_g"]),
                                                            row(w["conv_ln_b"]), full["w_conv_out"])
    h_mix, tok_mix = scatter_start("mix", ("w_o", "w_pool_out", "pool_w", "w_conv_out", "dw_kernel"),
                                   [g_wo, g_wpo, g_pw, g_wco, g_dw[:, :TAPS, :]])
    g_win = _in_proj_bwd_w(u, d_glu, dp, dgates, tok_mix)
    h_in, tok_in = scatter_start("in", ("w_in",), [g_win])
    grad_x, g_mix_pre = _in_proj_bwd_x(d_glu, dp, dgates, full["w_in"], x, row(w["mix_pre_g"]), dh1, tok_in)

    out = {}
    slot = jnp.reshape(_my_slot(), (1,)).astype(jnp.int32)
    after = grad_x
    for tag, handle in (("ff", h_ff), ("mix", h_mix), ("in", h_in)):
        mine, landed = _exchange_wait("scatter_wait_" + tag, handle, _scatter_src, _scatter_dst, after)
        for name, own, parts in zip(handle[0], mine, landed):
            out[name] = _adamw(name, own, parts, w[name], m[name], v[name], slot)
            after = out[name][0]
    vec_grads = jnp.concatenate(
        [jnp.pad(g, ((0, 7), (0, 0))) for g in
         (g_mix_pre, g_bias, g_ln_g, g_ln_b, g_pool_scale, g_mix_post, g_mlp_pre, g_mlp_post)], axis=0)
    vec_parts = _vec_exchange(vec_grads)
    stack = lambda d: jnp.stack([d[n] for n in VEC_NAMES], axis=0)
    vec_parts = vec_parts.reshape(N_DEV, 8, 8, D)[:, :, 0, :]
    res = _adamw_vectors(vec_parts, stack(w), stack(m), stack(v))
    for k, name in enumerate(VEC_NAMES):
        out[name] = [r[k] for r in res]

    loss = lax.psum(sse[0, 0] * (0.5 / D), ("x", "y", "c"))
    return loss, grad_x, out


def kernel(x, mix_pre_g, w_in, dw_kernel, dw_bias, conv_ln_g, conv_ln_b, w_conv_out, pool_w, pool_scale, w_pool_out, w_o, mix_post_g, mlp_pre_g, w_ff1, w_ff2, mlp_post_g, loss_target, m_mix_pre_g, m_w_in, m_dw_kernel, m_dw_bias, m_conv_ln_g, m_conv_ln_b, m_w_conv_out, m_pool_w, m_pool_scale, m_w_pool_out, m_w_o, m_mix_post_g, m_mlp_pre_g, m_w_ff1, m_w_ff2, m_mlp_post_g, v_mix_pre_g, v_w_in, v_dw_kernel, v_dw_bias, v_conv_ln_g, v_conv_ln_b, v_w_conv_out, v_pool_w, v_pool_scale, v_w_pool_out, v_w_o, v_mix_post_g, v_mlp_pre_g, v_w_ff1, v_w_ff2, v_mlp_post_g):
    w = dict(mix_pre_g=mix_pre_g, w_in=w_in, dw_kernel=dw_kernel, dw_bias=dw_bias, conv_ln_g=conv_ln_g, conv_ln_b=conv_ln_b,
             w_conv_out=w_conv_out, pool_w=pool_w, pool_scale=pool_scale, w_pool_out=w_pool_out, w_o=w_o,
             mix_post_g=mix_post_g, mlp_pre_g=mlp_pre_g, w_ff1=w_ff1, w_ff2=w_ff2, mlp_post_g=mlp_post_g)
    m = dict(mix_pre_g=m_mix_pre_g, w_in=m_w_in, dw_kernel=m_dw_kernel, dw_bias=m_dw_bias, conv_ln_g=m_conv_ln_g,
             conv_ln_b=m_conv_ln_b, w_conv_out=m_w_conv_out, pool_w=m_pool_w, pool_scale=m_pool_scale,
             w_pool_out=m_w_pool_out, w_o=m_w_o, mix_post_g=m_mix_post_g, mlp_pre_g=m_mlp_pre_g, w_ff1=m_w_ff1,
             w_ff2=m_w_ff2, mlp_post_g=m_mlp_post_g)
    v = dict(mix_pre_g=v_mix_pre_g, w_in=v_w_in, dw_kernel=v_dw_kernel, dw_bias=v_dw_bias, conv_ln_g=v_conv_ln_g,
             conv_ln_b=v_conv_ln_b, w_conv_out=v_w_conv_out, pool_w=v_pool_w, pool_scale=v_pool_scale,
             w_pool_out=v_w_pool_out, w_o=v_w_o, mix_post_g=v_mix_post_g, mlp_pre_g=v_mlp_pre_g, w_ff1=v_w_ff1,
             w_ff2=v_w_ff2, mlp_post_g=v_mlp_post_g)
    seq = x.shape[1]
    loss, grad_x, out = _step(x.reshape(seq, D), loss_target.reshape(seq, D), w, m, v)
    grads, deltas, new_m, new_v = ([out[n][k] for n in WEIGHT_ORDER] for k in range(4))
    return (loss, grad_x.reshape(x.shape), *grads, *deltas, *new_m, *new_v)
```

```python
import collections

import jax
import jax.numpy as jnp
from jax import lax
from jax.experimental import pallas as pl
from jax.experimental.pallas import tpu as pltpu

D = 1024
D_FF = 4 * D
IN_COLS = 5 * D
TAPS = 31
HALO = 32
POOL_WINDOWS = (2, 4, 8, 16)
PG = D // 4
N_DEV = 8
RMS_EPS = 1e-6
LN_EPS = 1e-5
ADAM_LR, ADAM_B1, ADAM_B2, ADAM_EPS, ADAM_WD, ADAM_STEP = 0.001, 0.9, 0.999, 1e-08, 0.01, 10

BF16 = jnp.bfloat16
F32 = jnp.float32
MIB = 1 << 20
MESH = pl.DeviceIdType.MESH


def _params(sem, vmem_mib):
    return pltpu.CompilerParams(dimension_semantics=sem, vmem_limit_bytes=vmem_mib * MIB)


def _dot(a, b):
    return jnp.dot(a, b, preferred_element_type=F32)


def _dot_nt(a, b):
    return lax.dot_general(a, b, (((1,), (1,)), ((), ())), preferred_element_type=F32)


def _dot_tn(a, b):
    return lax.dot_general(a, b, (((0,), (0,)), ((), ())), preferred_element_type=F32)


def _rms_scale(x):
    return lax.rsqrt(jnp.mean(x * x, axis=-1, keepdims=True) + RMS_EPS)


def _rms_bwd(dy, x, g):
    xn = x * _rms_scale(x)
    dn = dy * g
    dx = _rms_scale(x) * (dn - xn * jnp.mean(dn * xn, axis=-1, keepdims=True))
    return dx, jnp.sum(dy * xn, axis=0, keepdims=True)


def _sigmoid(x):
    return jax.nn.sigmoid(x)


def _acc_out(ref, first, value):
    @pl.when(first)
    def _():
        ref[...] = value

    @pl.when(jnp.logical_not(first))
    def _():
        ref[...] += value


def _my_slot():
    return 4 * lax.axis_index("x") + 2 * lax.axis_index("y") + lax.axis_index("c")


def _peer(mask):
    x, y, c = lax.axis_index("x"), lax.axis_index("y"), lax.axis_index("c")
    return (x ^ ((mask >> 2) & 1), y ^ ((mask >> 1) & 1), c ^ (mask & 1))


def _cols(width):
    return lambda ref, slot: ref.at[:, pl.ds(pl.multiple_of(slot * width, 128), width)]


def _rows(height):
    return lambda ref, slot: ref.at[pl.ds(pl.multiple_of(slot * height, 8), height), :]


def _lead(ref, slot):
    return ref.at[slot]


def _pool_rows(ref, slot):
    return ref.at[:, pl.ds(pl.multiple_of(slot * (PG // N_DEV), 8), PG // N_DEV), :]


SHARDED = (
    ("w_in", (D, IN_COLS), (D, IN_COLS // N_DEV), _cols(IN_COLS // N_DEV)),
    ("w_ff1", (D, D_FF), (D, D_FF // N_DEV), _cols(D_FF // N_DEV)),
    ("w_ff2", (D_FF, D), (D_FF // N_DEV, D), _rows(D_FF // N_DEV)),
    ("w_conv_out", (D, D), (D // N_DEV, D), _rows(D // N_DEV)),
    ("w_pool_out", (D, D), (D // N_DEV, D), _rows(D // N_DEV)),
    ("w_o", (D, D), (D // N_DEV, D), _rows(D // N_DEV)),
    ("pool_w", (4, PG, PG), (4, PG // N_DEV, PG), _pool_rows),
    ("dw_kernel", (N_DEV, TAPS, D // N_DEV), (TAPS, D // N_DEV), _lead),
)
N_SHARDED = len(SHARDED)


SHARD_AT = {name: at for name, _, _, at in SHARDED}
HBM = pl.BlockSpec(memory_space=pltpu.HBM)
SEM = pl.BlockSpec(memory_space=pltpu.SEMAPHORE)
ANY = pl.BlockSpec(memory_space=pl.ANY)
EFFECT = pltpu.SideEffectType.DATAFLOW_SIDE_EFFECTING


def _in_hbm(a):
    return pltpu.with_memory_space_constraint(a, pltpu.HBM)


def _stage_shards(shards):
    dtypes = [F32 if name == "dw_kernel" else BF16 for name, *_ in SHARDED]

    def body(*refs):
        ins = refs[:N_SHARDED]
        stage = refs[N_SHARDED:2 * N_SHARDED]
        fulls = refs[2 * N_SHARDED:3 * N_SHARDED]
        sems = refs[3 * N_SHARDED]
        me = _my_slot()
        copies = []
        for a, (_, _, _, at) in enumerate(SHARDED):
            stage[a][...] = ins[a][...].astype(dtypes[a])
            cp = pltpu.make_async_copy(stage[a], at(fulls[a], me), sems.at[a])
            cp.start()
            copies.append(cp)
        for cp in copies:
            cp.wait()

    outs = pl.pallas_call(
        body, name="stage_shards",
        out_shape=[jax.ShapeDtypeStruct(shard, dt) for (_, _, shard, _), dt in zip(SHARDED, dtypes)]
        + [jax.ShapeDtypeStruct(full, dt) for (_, full, _, _), dt in zip(SHARDED, dtypes)],
        in_specs=[pl.BlockSpec(memory_space=pltpu.VMEM)] * N_SHARDED,
        out_specs=[pl.BlockSpec(memory_space=pltpu.VMEM)] * N_SHARDED + [pl.BlockSpec(memory_space=pl.ANY)] * N_SHARDED,
        scratch_shapes=[pltpu.SemaphoreType.DMA((N_SHARDED,))],
        compiler_params=pltpu.CompilerParams(vmem_limit_bytes=40 * MIB),
    )(*shards)
    return outs[:N_SHARDED], outs[N_SHARDED:]


Leg = collections.namedtuple("Leg", "routes src_of dst_of src_is_land")


def _sem_index(k, m):
    return k * (N_DEV - 1) + m - 1


def _exchange_start(name, groups, legs):
    sizes = [len(g[0]) for g in groups]
    names = [nm for g in groups for nm in g[0]]
    srcs = [s for g in groups if g[1] is not None for s in g[1]]
    lands = [l for g in groups for l in g[2]]
    n_src, n, n_g, n_l = len(srcs), len(names), len(groups), len(legs)

    def body(*refs):
        src_refs, land_refs = list(refs[:n_src]), refs[n_src:n_src + n]
        sems = refs[n_src + n:n_src + n + 2 * n_g * n_l]
        token = refs[-1]
        me = _my_slot()
        first = 0
        for g, size in enumerate(sizes):
            own_src = [src_refs.pop(0) for _ in range(size)] if groups[g][1] is not None else None
            for li, leg in enumerate(legs):
                send, recv = sems[2 * (g * n_l + li)], sems[2 * (g * n_l + li) + 1]
                for m, via in leg.routes:
                    for k in range(size):
                        land = land_refs[first + k]
                        src = land if leg.src_is_land else own_src[k]
                        pltpu.make_async_remote_copy(
                            src_ref=leg.src_of(names[first + k], src, me, m), dst_ref=leg.dst_of(names[first + k], land, me, m),
                            send_sem=send.at[_sem_index(k, m)], recv_sem=recv.at[_sem_index(k, m)],
                            device_id=_peer(via), device_id_type=MESH).start()
            first += size
        token[...] = jnp.zeros_like(token)

    sem_shapes = [pltpu.SemaphoreType.DMA((size * (N_DEV - 1),)) for size in sizes for _ in range(2 * n_l)]
    n_sem = len(sem_shapes)
    outs = pl.pallas_call(
        body, name=name,
        out_shape=sem_shapes + [pltpu.HBM(a.shape, a.dtype) for a in srcs + lands] + [jax.ShapeDtypeStruct((8, 128), F32)],
        in_specs=[HBM] * (n_src + n),
        out_specs=[SEM] * n_sem + [HBM] * (n_src + n) + [pl.BlockSpec(memory_space=pltpu.VMEM)],
        input_output_aliases={k: n_sem + k for k in range(n_src + n)},
        compiler_params=pltpu.CompilerParams(has_side_effects=EFFECT),
    )(*[_in_hbm(a) for a in srcs + lands])
    sems, thru, token = outs[:n_sem], list(outs[n_sem:-1]), outs[-1]
    src_thru, land_thru = thru[:n_src], thru[n_src:]
    handles, first = [], 0
    for g, size in enumerate(sizes):
        pairs = [(sems[2 * (g * n_l + li)], sems[2 * (g * n_l + li) + 1]) for li in range(n_l)]
        mine = [src_thru.pop(0) for _ in range(size)] if groups[g][1] is not None else None
        handles.append((groups[g][0], pairs, mine, land_thru[first:first + size]))
        first += size
    return handles, token


def _exchange_wait(name, names, srcs, lands, waits, after):
    n = len(names)
    n_src = n if srcs is not None else 0

    def body(*refs):
        src_refs, land_refs = refs[:n_src], refs[n_src:n_src + n]
        sems = refs[n_src + n:n_src + n + 2 * len(waits)]
        me = _my_slot()
        for wi, (_, leg) in enumerate(waits):
            for m, via in leg.routes:
                for k in range(n):
                    src = land_refs[k] if leg.src_is_land else src_refs[k]
                    cp = pltpu.make_async_remote_copy(
                        src_ref=leg.src_of(names[k], src, me, m), dst_ref=leg.dst_of(names[k], land_refs[k], me ^ via, m),
                        send_sem=sems[2 * wi].at[_sem_index(k, m)], recv_sem=sems[2 * wi + 1].at[_sem_index(k, m)],
                        device_id=_peer(via), device_id_type=MESH)
                    cp.wait_send()
                    cp.wait_recv()

    arrays = (list(srcs) if srcs is not None else []) + list(lands)
    outs = pl.pallas_call(
        body, name=name,
        out_shape=[pltpu.HBM(a.shape, a.dtype) for a in arrays],
        in_specs=[HBM] * len(arrays) + [SEM] * (2 * len(waits)) + [pl.BlockSpec(memory_space=pl.ANY)],
        out_specs=[HBM] * len(arrays),
        input_output_aliases={k: k for k in range(len(arrays))},
        compiler_params=pltpu.CompilerParams(has_side_effects=EFFECT),
    )(*arrays, *[s for pair, _ in waits for s in pair], after)
    return (outs[:n_src] if srcs is not None else None), outs[n_src:]


def _shard_slot(name, ref, slot):
    return SHARD_AT[name](ref, slot)


GATHER_ICI = Leg(((2, 2), (4, 4), (6, 6)), lambda name, ref, me, m: ref,
                 lambda name, ref, sender, m: _shard_slot(name, ref, sender), False)
GATHER_D2D = Leg(((1, 1),), GATHER_ICI.src_of, GATHER_ICI.dst_of, False)
GATHER_FORWARD = Leg(((2, 1), (4, 1), (6, 1)), lambda name, ref, me, m: _shard_slot(name, ref, me ^ m),
                     lambda name, ref, sender, m: _shard_slot(name, ref, sender ^ m), True)
SCATTER = Leg(tuple((m, m) for m in range(1, N_DEV)), lambda name, ref, me, m: _shard_slot(name, ref, me ^ m),
              lambda name, ref, sender, m: ref.at[m - 1], False)


def _vec_exchange(vectors):
    n = len(vectors)

    def body(*refs):
        vec, vec_out = refs[n + 1], refs[n]
        send_sems, recv_sems, local_sem = refs[n + 2:]
        for k in range(n):
            vec[k:k + 1, :] = refs[k][...]
        me = _my_slot()
        local = pltpu.make_async_copy(vec, vec_out.at[me], local_sem)
        local.start()
        sends = []
        for mask in range(1, N_DEV):
            cp = pltpu.make_async_remote_copy(
                src_ref=vec, dst_ref=vec_out.at[me], send_sem=send_sems.at[mask - 1],
                recv_sem=recv_sems.at[mask - 1], device_id=_peer(mask), device_id_type=MESH)
            cp.start()
            sends.append(cp)
        for mask in range(1, N_DEV):
            pltpu.make_async_remote_copy(
                src_ref=vec, dst_ref=vec_out.at[me ^ mask], send_sem=send_sems.at[mask - 1],
                recv_sem=recv_sems.at[mask - 1], device_id=_peer(mask), device_id_type=MESH).wait_recv()
        for cp in sends:
            cp.wait_send()
        local.wait()

    return pl.pallas_call(
        body, name="vec_exchange",
        out_shape=jax.ShapeDtypeStruct((N_DEV, n, D), F32),
        in_specs=[pl.BlockSpec(memory_space=pltpu.VMEM)] * n,
        out_specs=pl.BlockSpec(memory_space=pl.ANY),
        scratch_shapes=[pltpu.VMEM((n, D), F32), pltpu.SemaphoreType.DMA((N_DEV - 1,)),
                        pltpu.SemaphoreType.DMA((N_DEV - 1,)), pltpu.SemaphoreType.DMA],
    )(*vectors)


def _adamw_update(g, w_ref, m_ref, v_ref, g_out, d_out, m_out, v_out):
    m_new = ADAM_B1 * m_ref[...] + (1.0 - ADAM_B1) * g
    v_new = ADAM_B2 * v_ref[...] + (1.0 - ADAM_B2) * (g * g)
    m_hat = m_new / (1.0 - ADAM_B1 ** ADAM_STEP)
    v_hat = v_new / (1.0 - ADAM_B2 ** ADAM_STEP)
    g_out[...] = g
    d_out[...] = -ADAM_LR * (m_hat / (jnp.sqrt(v_hat) + ADAM_EPS) + ADAM_WD * w_ref[...])
    m_out[...] = m_new
    v_out[...] = v_new


ADAMW_ROWS = 256


def _adamw(name, own, parts, w, m, v, slot):
    shard = w.shape
    if name in ("w_in", "w_ff1"):
        tr = ADAMW_ROWS
        grid = (shard[0] // tr,)
        own_spec = pl.BlockSpec((tr, shard[1]), lambda i, s: (i, s[0]))
        blk = pl.BlockSpec((tr, shard[1]), lambda i, s: (i, 0))
        parts_spec = pl.BlockSpec((N_DEV - 1, tr, shard[1]), lambda i, s: (0, i, 0))
    elif name == "pool_w":
        grid = (shard[0],)
        own_spec = pl.BlockSpec((None,) + shard[1:], lambda g, s: (g, s[0], 0))
        blk = pl.BlockSpec((None,) + shard[1:], lambda g, s: (g, 0, 0))
        parts_spec = pl.BlockSpec((N_DEV - 1, None) + shard[1:], lambda g, s: (0, g, 0, 0))
    elif name == "dw_kernel":
        grid = (1,)
        own_spec = pl.BlockSpec((None,) + shard, lambda i, s: (s[0], 0, 0))
        blk = pl.BlockSpec(shard, lambda i, s: (0, 0))
        parts_spec = pl.BlockSpec((N_DEV - 1,) + shard, lambda i, s: (0, 0, 0))
    else:
        tr = min(ADAMW_ROWS, shard[0])
        grid = (shard[0] // tr,)
        own_spec = pl.BlockSpec((tr, shard[1]), lambda i, s: (s[0] * grid[0] + i, 0))
        blk = pl.BlockSpec((tr, shard[1]), lambda i, s: (i, 0))
        parts_spec = pl.BlockSpec((N_DEV - 1, tr, shard[1]), lambda i, s: (0, i, 0))

    def body(slot_ref, own_ref, p_ref, w_ref, m_ref, v_ref, g_out, d_out, m_out, v_out):
        g = own_ref[...].astype(F32)
        for k in range(N_DEV - 1):
            g = g + p_ref[k].astype(F32)
        _adamw_update(g, w_ref, m_ref, v_ref, g_out, d_out, m_out, v_out)

    return pl.pallas_call(
        body, name="adamw_" + name,
        grid_spec=pltpu.PrefetchScalarGridSpec(
            num_scalar_prefetch=1, grid=grid, in_specs=[own_spec, parts_spec, blk, blk, blk], out_specs=[blk] * 4),
        out_shape=[jax.ShapeDtypeStruct(shard, F32)] * 4,
        compiler_params=_params(("arbitrary",), 32),
    )(slot, own, parts, w, m, v)


def _adamw_vectors(parts, w, m, v):
    def body(p_ref, w_ref, m_ref, v_ref, g_out, d_out, m_out, v_out):
        g = p_ref[0]
        for s in range(1, N_DEV):
            g = g + p_ref[s]
        _adamw_update(g, w_ref, m_ref, v_ref, g_out, d_out, m_out, v_out)

    return pl.pallas_call(
        body, name="adamw_vectors",
        out_shape=[jax.ShapeDtypeStruct(w.shape, F32)] * 4,
    )(parts, w, m, v)


def _in_proj_fwd(x, g_pre, w_in, token):
    seq = x.shape[0]
    tm, tn = 1024, IN_COLS // 4

    def body(x_ref, g_ref, w_ref, _token, proj_ref, u_ref):
        @pl.when(pl.program_id(1) == 0)
        def _():
            xf = x_ref[...]
            u_ref[...] = (xf * _rms_scale(xf) * g_ref[...]).astype(BF16)
        proj_ref[...] = _dot(u_ref[...], w_ref[...])

    return pl.pallas_call(
        body, name="in_proj_fwd", grid=(seq // tm, IN_COLS // tn),
        in_specs=[pl.BlockSpec((tm, D), lambda i, j: (i, 0)), pl.BlockSpec((1, D), lambda i, j: (0, 0)),
                  pl.BlockSpec((D, tn), lambda i, j: (0, j)), ANY],
        out_specs=[pl.BlockSpec((tm, tn), lambda i, j: (i, j)), pl.BlockSpec((tm, D), lambda i, j: (i, 0))],
        out_shape=[jax.ShapeDtypeStruct((seq, IN_COLS), F32), jax.ShapeDtypeStruct((seq, D), BF16)],
        compiler_params=_params(("arbitrary", "arbitrary"), 40),
    )(x, g_pre, w_in, token)


CONV_TM = 256
CONV_RS = 64


def _shifts():
    return [(b, [(a, 8 * a + b) for a in range(4) if 8 * a + b < TAPS]) for b in range(8)]


def _taps_looking_back(buf, row0, lanes, weight):
    acc = None
    for b, group in _shifts():
        part = None
        for a, s in group:
            term = weight(TAPS - 1 - s) * buf[pl.ds(row0 - 8 - 8 * a, CONV_RS + 8), lanes]
            part = term if part is None else part + term
        if b:
            part = pltpu.roll(part, b, 0)
        acc = part[8:, :] if acc is None else acc + part[8:, :]
    return acc


def _taps_looking_ahead(buf, row0, lanes, weight):
    acc = None
    for b, group in _shifts():
        part = None
        for a, s in group:
            term = weight(TAPS - 1 - s) * buf[pl.ds(row0 + 8 * a, CONV_RS + 8), lanes]
            part = term if part is None else part + term
        if b:
            part = pltpu.roll(part, CONV_RS + 8 - b, 0)
        acc = part[:CONV_RS, :] if acc is None else acc + part[:CONV_RS, :]
    return acc


def _layer_norm_parts(cv):
    mu = jnp.mean(cv, axis=-1, keepdims=True)
    cen = cv - mu
    rstd = lax.rsqrt(jnp.mean(cen * cen, axis=-1, keepdims=True) + LN_EPS)
    return cen * rstd, rstd


def _conv_fwd(proj, dw, dw_bias, ln_g, ln_b, w_conv_out, token):
    seq = proj.shape[0]
    tm = CONV_TM

    def body(a_ref, gate_ref, dw_ref, bias_ref, lg_ref, lb_ref, w_ref, _token, cv_ref, y_ref, ug):
        i = pl.program_id(0)

        @pl.when(i == 0)
        def _():
            ug[0:HALO, :] = jnp.zeros((HALO, D), F32)

        @pl.when(i > 0)
        def _():
            ug[0:HALO, :] = ug[tm:tm + HALO, :]
        ug[HALO:HALO + tm, :] = a_ref[...] * _sigmoid(gate_ref[...])

        def channel_block(cb, carry):
            lanes = pl.ds(pl.multiple_of(cb * 128, 128), 128)
            for r0 in range(0, tm, CONV_RS):
                taps = _taps_looking_back(ug, HALO + r0, lanes, lambda k: dw_ref[cb, k:k + 1, :])
                cv_ref[pl.ds(r0, CONV_RS), lanes] = taps + bias_ref[:, lanes]
            return carry
        lax.fori_loop(0, D // 128, channel_block, 0)

        n, _ = _layer_norm_parts(cv_ref[...])
        ln = n * lg_ref[...] + lb_ref[...]
        y_ref[...] = _dot((ln * _sigmoid(ln)).astype(BF16), w_ref[...])

    vec = pl.BlockSpec((1, D), lambda i: (0, 0))
    tile = pl.BlockSpec((tm, D), lambda i: (i, 0))
    return pl.pallas_call(
        body, name="conv_fwd", grid=(seq // tm,),
        in_specs=[pl.BlockSpec((tm, D), lambda i: (i, 0)), pl.BlockSpec((tm, D), lambda i: (i, 1)),
                  pl.BlockSpec((N_DEV, TAPS, 128), lambda i: (0, 0, 0)), vec, vec, vec,
                  pl.BlockSpec((D, D), lambda i: (0, 0)), ANY],
        out_specs=[tile, tile],
        out_shape=[jax.ShapeDtypeStruct((seq, D), F32)] * 2,
        scratch_shapes=[pltpu.VMEM((HALO + tm, D), F32)],
        compiler_params=_params(("arbitrary",), 32),
    )(proj, proj, dw, dw_bias, ln_g, ln_b, w_conv_out, token)


def _pool_counts(tile_index, tm, window):
    t = tile_index * tm + lax.broadcasted_iota(jnp.int32, (tm, 1), 0)
    return 1.0 / jnp.minimum(t + 1, window).astype(F32)


def _pool_merge_fwd(proj, y_conv, x, pool_w, pool_scale, w_pool_out, w_o, g_post, token):
    seq = proj.shape[0]
    tm = CONV_TM

    def body(p_ref, gc_ref, gp_ref, yc_ref, x_ref, pw_ref, ps_ref, wpo_ref, wo_ref, g_ref, _token,
             z_ref, zl_ref, yp_ref, mg_ref, o_ref, h1_ref, pbuf):
        i = pl.program_id(0)

        @pl.when(i == 0)
        def _():
            pbuf[0:HALO, :] = jnp.zeros((HALO, D), F32)

        @pl.when(i > 0)
        def _():
            pbuf[0:HALO, :] = pbuf[tm:tm + HALO, :]
        pbuf[HALO:HALO + tm, :] = p_ref[...]

        for g, window in enumerate(POOL_WINDOWS):
            lanes = pl.ds(g * PG, PG)
            acc = pbuf[pl.ds(HALO, tm), lanes]
            for j in range(1, window):
                acc = acc + pbuf[pl.ds(HALO - j, tm), lanes]
            zg = acc * _pool_counts(i, tm, window) - pbuf[pl.ds(HALO, tm), lanes]
            z_ref[:, lanes] = zg.astype(BF16)
            zl_ref[:, lanes] = _dot(zg.astype(BF16), pw_ref[g])
        zl = zl_ref[...]
        y_pool = _dot((zl * ps_ref[...]).astype(BF16), wpo_ref[...])
        yp_ref[...] = y_pool
        merged = (_sigmoid(gc_ref[...]) * yc_ref[...] + _sigmoid(gp_ref[...]) * y_pool).astype(BF16)
        mg_ref[...] = merged
        o = _dot(merged, wo_ref[...])
        o_ref[...] = o
        h1_ref[...] = x_ref[...] + o * _rms_scale(o) * g_ref[...]

    vec = pl.BlockSpec((1, D), lambda i: (0, 0))
    tile = pl.BlockSpec((tm, D), lambda i: (i, 0))
    mat = pl.BlockSpec((D, D), lambda i: (0, 0))
    return pl.pallas_call(
        body, name="pool_merge_fwd", grid=(seq // tm,),
        in_specs=[pl.BlockSpec((tm, D), lambda i: (i, 2)), pl.BlockSpec((tm, D), lambda i: (i, 3)),
                  pl.BlockSpec((tm, D), lambda i: (i, 4)), tile, tile,
                  pl.BlockSpec((4, PG, PG), lambda i: (0, 0, 0)), vec, mat, mat, vec, ANY],
        out_specs=[tile] * 6,
        out_shape=[jax.ShapeDtypeStruct((seq, D), dt) for dt in (BF16, F32, F32, BF16, F32, F32)],
        scratch_shapes=[pltpu.VMEM((HALO + tm, D), F32)],
        compiler_params=_params(("arbitrary",), 48),
    )(proj, proj, proj, y_conv, x, pool_w, pool_scale, w_pool_out, w_o, g_post, token)


def _mlp_fwd(h1, g_pre, w_ff1, w_ff2, g_post, target):
    seq = h1.shape[0]
    tm, tf = 1024, D_FF // N_DEV
    n_f = D_FF // tf

    def body(h1_ref, gpre_ref, w1_ref, w2_ref, gpost_ref, tgt_ref, v_ref, dm_ref, dh2_ref, sse_ref, ggrad_ref, macc):
        i, j = pl.program_id(0), pl.program_id(1)

        @pl.when(j == 0)
        def _():
            h = h1_ref[...]
            v_ref[...] = (h * _rms_scale(h) * gpre_ref[...]).astype(BF16)
        f = jnp.maximum(_dot(v_ref[...], w1_ref[...]), 0.0)
        part = _dot((f * f).astype(BF16), w2_ref[...])

        @pl.when(j == 0)
        def _():
            macc[...] = part

        @pl.when(j > 0)
        def _():
            macc[...] += part

        @pl.when(j == n_f - 1)
        def _():
            mo = macc[...]
            err = h1_ref[...] + mo * _rms_scale(mo) * gpost_ref[...] - tgt_ref[...]
            dh2 = err * (1.0 / D)
            dh2_ref[...] = dh2
            dm, ggrad = _rms_bwd(dh2, mo, gpost_ref[...])
            dm_ref[...] = dm.astype(BF16)
            _acc_out(ggrad_ref, i == 0, ggrad)
            _acc_out(sse_ref, i == 0, jnp.sum(jnp.sum(err * err, axis=1, keepdims=True), axis=0, keepdims=True))

    vec = pl.BlockSpec((1, D), lambda i, j: (0, 0))
    tile = pl.BlockSpec((tm, D), lambda i, j: (i, 0))
    return pl.pallas_call(
        body, name="mlp_fwd", grid=(seq // tm, n_f),
        in_specs=[tile, vec, pl.BlockSpec((D, tf), lambda i, j: (0, j)), pl.BlockSpec((tf, D), lambda i, j: (j, 0)), vec, tile],
        out_specs=[tile, tile, tile, pl.BlockSpec((1, 1), lambda i, j: (0, 0)), vec],
        out_shape=[jax.ShapeDtypeStruct((seq, D), BF16), jax.ShapeDtypeStruct((seq, D), BF16),
                   jax.ShapeDtypeStruct((seq, D), F32), jax.ShapeDtypeStruct((1, 1), F32),
                   jax.ShapeDtypeStruct((1, D), F32)],
        scratch_shapes=[pltpu.VMEM((tm, D), F32)],
        compiler_params=_params(("arbitrary", "arbitrary"), 56),
    )(h1, g_pre, w_ff1, w_ff2, g_post, target)


def _mlp_bwd(v, dm, w_ff1, w_ff2):
    seq = v.shape[0]
    tm, tf = 1024, D_FF // N_DEV
    n_t = seq // tm

    def body(v_ref, dm_ref, w1_ref, w2_ref, dv_hbm, g1_ref, g2_ref, dv_acc, g1_acc, g2_acc, sem):
        j, i = pl.program_id(0), pl.program_id(1)
        vt, dmt = v_ref[...], dm_ref[...]
        f = jnp.maximum(_dot(vt, w1_ref[...]), 0.0)
        df = (_dot_nt(dmt, w2_ref[...]) * (2.0 * f)).astype(BF16)
        rows = pl.ds(pl.multiple_of(i * tm, tm), tm)
        dv_part = _dot_nt(df, w1_ref[...])

        @pl.when(j == 0)
        def _():
            dv_acc[rows, :] = dv_part

        @pl.when(j > 0)
        def _():
            dv_acc[rows, :] += dv_part
        g1_part = _dot_tn(vt, df)
        g2_part = _dot_tn((f * f).astype(BF16), dmt)

        @pl.when(i == 0)
        def _():
            g1_acc[...] = g1_part
            g2_acc[...] = g2_part

        @pl.when(i > 0)
        def _():
            g1_acc[...] += g1_part
            g2_acc[...] += g2_part

        @pl.when(i == n_t - 1)
        def _():
            g1_ref[...] = g1_acc[...].astype(BF16)
            g2_ref[...] = g2_acc[...].astype(BF16)

        @pl.when(jnp.logical_and(i == n_t - 1, j == D_FF // tf - 1))
        def _():
            cp = pltpu.make_async_copy(dv_acc, dv_hbm, sem)
            cp.start()
            cp.wait()

    tile = pl.BlockSpec((tm, D), lambda j, i: (i, 0))
    return pl.pallas_call(
        body, name="mlp_bwd", grid=(D_FF // tf, n_t),
        in_specs=[tile, tile, pl.BlockSpec((D, tf), lambda j, i: (0, j)), pl.BlockSpec((tf, D), lambda j, i: (j, 0))],
        out_specs=[pl.BlockSpec(memory_space=pl.ANY), pl.BlockSpec((D, tf), lambda j, i: (0, j)),
                   pl.BlockSpec((tf, D), lambda j, i: (j, 0))],
        out_shape=[jax.ShapeDtypeStruct((seq, D), F32), jax.ShapeDtypeStruct((D, D_FF), BF16),
                   jax.ShapeDtypeStruct((D_FF, D), BF16)],
        scratch_shapes=[pltpu.VMEM((seq, D), F32), pltpu.VMEM((D, tf), F32), pltpu.VMEM((tf, D), F32),
                        pltpu.SemaphoreType.DMA],
        compiler_params=_params(("arbitrary", "arbitrary"), 52),
    )(v, dm, w_ff1, w_ff2)


def _merge_bwd(dh2, dv, h1, g_mlp_pre, o, g_mix_post, w_o, merged, proj, y_conv, y_pool, token):
    seq = dh2.shape[0]
    tm = 256
    n_t = seq // tm

    def body(dh2_ref, dv_ref, h1_ref, gpre_ref, o_ref, gpost_ref, wo_ref, mg_ref, gc_ref, gp_ref, yc_ref, yp_ref, _token,
             dh1_ref, dyc_ref, dyp_ref, dg_ref, gwo_ref, ggpre_ref, ggpost_ref, gwo_acc):
        i = pl.program_id(0)
        dnorm, ggpre = _rms_bwd(dv_ref[...], h1_ref[...], gpre_ref[...])
        dh1 = dh2_ref[...] + dnorm
        dh1_ref[...] = dh1
        do, ggpost = _rms_bwd(dh1, o_ref[...], gpost_ref[...])
        do = do.astype(BF16)
        _acc_out(ggpre_ref, i == 0, ggpre)
        _acc_out(ggpost_ref, i == 0, ggpost)
        _acc_out(gwo_acc, i == 0, _dot_tn(mg_ref[...], do))
        dmerged = _dot_nt(do, wo_ref[...])
        sc, sp = _sigmoid(gc_ref[...]), _sigmoid(gp_ref[...])
        dyc_ref[...] = (dmerged * sc).astype(BF16)
        dyp_ref[...] = (dmerged * sp).astype(BF16)
        dg_ref[:, 0:D] = (dmerged * yc_ref[...] * (sc * (1.0 - sc))).astype(BF16)
        dg_ref[:, D:2 * D] = (dmerged * yp_ref[...] * (sp * (1.0 - sp))).astype(BF16)

        @pl.when(i == n_t - 1)
        def _():
            gwo_ref[...] = gwo_acc[...].astype(BF16)

    vec = pl.BlockSpec((1, D), lambda i: (0, 0))
    tile = pl.BlockSpec((tm, D), lambda i: (i, 0))
    mat = pl.BlockSpec((D, D), lambda i: (0, 0))
    return pl.pallas_call(
        body, name="merge_bwd", grid=(n_t,),
        in_specs=[tile, tile, tile, vec, tile, vec, mat, tile,
                  pl.BlockSpec((tm, D), lambda i: (i, 3)), pl.BlockSpec((tm, D), lambda i: (i, 4)), tile, tile, ANY],
        out_specs=[tile, tile, tile, pl.BlockSpec((tm, 2 * D), lambda i: (i, 0)), mat, vec, vec],
        out_shape=[jax.ShapeDtypeStruct((seq, D), F32), jax.ShapeDtypeStruct((seq, D), BF16),
                   jax.ShapeDtypeStruct((seq, D), BF16), jax.ShapeDtypeStruct((seq, 2 * D), BF16),
                   jax.ShapeDtypeStruct((D, D), BF16), jax.ShapeDtypeStruct((1, D), F32),
                   jax.ShapeDtypeStruct((1, D), F32)],
        scratch_shapes=[pltpu.VMEM((D, D), F32)],
        compiler_params=_params(("arbitrary",), 48),
    )(dh2, dv, h1, g_mlp_pre, o, g_mix_post, w_o, merged, proj, proj, y_conv, y_pool, token)


def _pool_bwd(dy_pool, zl, z, pool_w, pool_scale, w_pool_out):
    seq = dy_pool.shape[0]
    tm = CONV_TM
    n_t = seq // tm

    def body(dy_ref, zl_ref, z_ref, pw_ref, ps_ref, wpo_ref, dp_ref, gwpo_ref, gpw_ref, gps_ref, qbuf, gwpo_acc, gpw_acc):
        i = pl.program_id(0)
        tile_index = n_t - 1 - i
        first = i == 0
        dy = dy_ref[...]
        zl = zl_ref[...]
        dzs = _dot_nt(dy, wpo_ref[...])
        _acc_out(gwpo_acc, first, _dot_tn((zl * ps_ref[...]).astype(BF16), dy))
        _acc_out(gps_ref, first, jnp.sum(dzs * zl, axis=0, keepdims=True))
        dzl = (dzs * ps_ref[...]).astype(BF16)

        @pl.when(first)
        def _():
            qbuf[tm:tm + HALO, :] = jnp.zeros((HALO, D), F32)

        @pl.when(jnp.logical_not(first))
        def _():
            qbuf[tm:tm + HALO, :] = qbuf[0:HALO, :]

        dzs_list = []
        for g, window in enumerate(POOL_WINDOWS):
            lanes = pl.ds(g * PG, PG)
            dzl_g = dzl[:, g * PG:(g + 1) * PG]
            dz = _dot_nt(dzl_g, pw_ref[g])
            _acc_out(gpw_acc.at[g], first, _dot_tn(z_ref[:, lanes], dzl_g))
            qbuf[pl.ds(0, tm), lanes] = dz * _pool_counts(tile_index, tm, window)
            dzs_list.append(dz)
        for g, window in enumerate(POOL_WINDOWS):
            lanes = pl.ds(g * PG, PG)
            acc = qbuf[pl.ds(0, tm), lanes]
            for j in range(1, window):
                acc = acc + qbuf[pl.ds(j, tm), lanes]
            dp_ref[:, lanes] = (acc - dzs_list[g]).astype(BF16)

        @pl.when(i == n_t - 1)
        def _():
            gwpo_ref[...] = gwpo_acc[...].astype(BF16)
            gpw_ref[...] = gpw_acc[...].astype(BF16)

    vec = pl.BlockSpec((1, D), lambda i: (0, 0))
    tile = pl.BlockSpec((tm, D), lambda i: (n_t - 1 - i, 0))
    mat = pl.BlockSpec((D, D), lambda i: (0, 0))
    pw = pl.BlockSpec((4, PG, PG), lambda i: (0, 0, 0))
    return pl.pallas_call(
        body, name="pool_bwd", grid=(n_t,),
        in_specs=[tile, tile, tile, pw, vec, mat],
        out_specs=[tile, mat, pw, vec],
        out_shape=[jax.ShapeDtypeStruct((seq, D), BF16), jax.ShapeDtypeStruct((D, D), BF16),
                   jax.ShapeDtypeStruct((4, PG, PG), BF16), jax.ShapeDtypeStruct((1, D), F32)],
        scratch_shapes=[pltpu.VMEM((tm + HALO, D), F32), pltpu.VMEM((D, D), F32), pltpu.VMEM((4, PG, PG), F32)],
        compiler_params=_params(("arbitrary",), 40),
    )(dy_pool, zl, z, pool_w, pool_scale, w_pool_out)


def _conv_bwd(dy_conv, cv, proj, dw, ln_g, ln_b, w_conv_out):
    seq = dy_conv.shape[0]
    tm = CONV_TM
    n_t = seq // tm
    halo_blocks = tm // HALO

    def body(dy_ref, cv_ref, a_ref, gate_ref, ah_ref, gh_ref, dw_ref, lg_ref, lb_ref, w_ref,
             dglu_ref, gw_ref, gdw_ref, gbias_ref, glg_ref, glb_ref, ug, dcv, dug, gw_acc):
        i = pl.program_id(0)
        tile_index = n_t - 1 - i
        first = i == 0
        dy = dy_ref[...]
        n, rstd = _layer_norm_parts(cv_ref[...])
        ln = n * lg_ref[...] + lb_ref[...]
        sg = _sigmoid(ln)
        _acc_out(gw_acc, first, _dot_tn((ln * sg).astype(BF16), dy))
        dln = _dot_nt(dy, w_ref[...]) * (sg * (1.0 + ln * (1.0 - sg)))
        _acc_out(glg_ref, first, jnp.sum(dln * n, axis=0, keepdims=True))
        _acc_out(glb_ref, first, jnp.sum(dln, axis=0, keepdims=True))
        dn = dln * lg_ref[...]
        dcv_tile = rstd * (dn - jnp.mean(dn, axis=-1, keepdims=True) - n * jnp.mean(dn * n, axis=-1, keepdims=True))
        _acc_out(gbias_ref, first, jnp.sum(dcv_tile, axis=0, keepdims=True))

        @pl.when(first)
        def _():
            dcv[tm:tm + HALO, :] = jnp.zeros((HALO, D), F32)

        @pl.when(jnp.logical_not(first))
        def _():
            dcv[tm:tm + HALO, :] = dcv[0:HALO, :]
        dcv[0:tm, :] = dcv_tile

        a, gate = a_ref[...], gate_ref[...]
        sgate = _sigmoid(gate)
        ug[HALO:HALO + tm, :] = a * sgate
        before = jnp.where(tile_index > 0, 1.0, 0.0)
        ug[0:HALO, :] = ah_ref[...] * _sigmoid(gh_ref[...]) * before

        @pl.when(first)
        def _():
            gdw_ref[...] = jnp.zeros((N_DEV, TAPS + 1, 128), F32)

        def channel_block(cb, carry):
            lanes = pl.ds(pl.multiple_of(cb * 128, 128), 128)
            for r0 in range(0, tm, CONV_RS):
                dug[pl.ds(r0, CONV_RS), lanes] = _taps_looking_ahead(dcv, r0, lanes, lambda k: dw_ref[cb, k:k + 1, :])
            for b, group in _shifts():
                sums = [jnp.zeros((8, 128), F32) for _ in group]
                for r0 in range(0, tm, CONV_RS):
                    window = ug[pl.ds(r0, CONV_RS + HALO), lanes]
                    if b:
                        window = pltpu.roll(window, b, 0)
                    d = dcv[pl.ds(r0, CONV_RS), lanes]
                    for n_a, (a, s) in enumerate(group):
                        prod = d * window[HALO - 8 * a:HALO - 8 * a + CONV_RS, :]
                        sums[n_a] = sums[n_a] + jnp.sum(prod.reshape(CONV_RS // 8, 8, 128), axis=0)
                for n_a, (a, s) in enumerate(group):
                    k = TAPS - 1 - s
                    gdw_ref[cb, k:k + 1, :] += jnp.sum(sums[n_a], axis=0, keepdims=True)
            return carry
        lax.fori_loop(0, D // 128, channel_block, 0)

        d_ug = dug[...]
        dglu_ref[:, 0:D] = (d_ug * sgate).astype(BF16)
        dglu_ref[:, D:2 * D] = (d_ug * a * (sgate * (1.0 - sgate))).astype(BF16)

        @pl.when(i == n_t - 1)
        def _():
            gw_ref[...] = gw_acc[...].astype(BF16)

    def halo_index(col):
        return lambda i: (jnp.maximum((n_t - 1 - i) * halo_blocks - 1, 0), col)

    vec = pl.BlockSpec((1, D), lambda i: (0, 0))
    tile = pl.BlockSpec((tm, D), lambda i: (n_t - 1 - i, 0))
    mat = pl.BlockSpec((D, D), lambda i: (0, 0))
    dwspec = pl.BlockSpec((N_DEV, TAPS, 128), lambda i: (0, 0, 0))
    return pl.pallas_call(
        body, name="conv_bwd", grid=(n_t,),
        in_specs=[tile, tile, pl.BlockSpec((tm, D), lambda i: (n_t - 1 - i, 0)), pl.BlockSpec((tm, D), lambda i: (n_t - 1 - i, 1)),
                  pl.BlockSpec((HALO, D), halo_index(0)), pl.BlockSpec((HALO, D), halo_index(1)), dwspec, vec, vec, mat],
        out_specs=[pl.BlockSpec((tm, 2 * D), lambda i: (n_t - 1 - i, 0)), mat,
                   pl.BlockSpec((N_DEV, TAPS + 1, 128), lambda i: (0, 0, 0)), vec, vec, vec],
        out_shape=[jax.ShapeDtypeStruct((seq, 2 * D), BF16), jax.ShapeDtypeStruct((D, D), BF16),
                   jax.ShapeDtypeStruct((N_DEV, TAPS + 1, 128), F32), jax.ShapeDtypeStruct((1, D), F32),
                   jax.ShapeDtypeStruct((1, D), F32), jax.ShapeDtypeStruct((1, D), F32)],
        scratch_shapes=[pltpu.VMEM((HALO + tm, D), F32), pltpu.VMEM((tm + HALO, D), F32), pltpu.VMEM((tm, D), F32),
                        pltpu.VMEM((D, D), F32)],
        compiler_params=_params(("arbitrary",), 48),
    )(dy_conv, cv, proj, proj, proj, proj, dw, ln_g, ln_b, w_conv_out)


def _in_proj_bwd_x(d_glu, dp, dgates, w_in, x, g_pre, dh1, token):
    seq = x.shape[0]
    tm = 256

    def body(dglu_ref, dp_ref, dg_ref, w_ref, x_ref, g_ref, dh1_ref, _token, dx_ref, gg_ref):
        du = _dot_nt(dglu_ref[...], w_ref[:, 0:2 * D])
        du += _dot_nt(dp_ref[...], w_ref[:, 2 * D:3 * D])
        du += _dot_nt(dg_ref[...], w_ref[:, 3 * D:5 * D])
        dnorm, gg = _rms_bwd(du, x_ref[...], g_ref[...])
        dx_ref[...] = dh1_ref[...] + dnorm
        _acc_out(gg_ref, pl.program_id(0) == 0, gg)

    vec = pl.BlockSpec((1, D), lambda i: (0, 0))
    tile = pl.BlockSpec((tm, D), lambda i: (i, 0))
    wide = pl.BlockSpec((tm, 2 * D), lambda i: (i, 0))
    return pl.pallas_call(
        body, name="in_proj_bwd_x", grid=(seq // tm,),
        in_specs=[wide, tile, wide, pl.BlockSpec((D, IN_COLS), lambda i: (0, 0)), tile, vec, tile, ANY],
        out_specs=[tile, vec],
        out_shape=[jax.ShapeDtypeStruct((seq, D), F32), jax.ShapeDtypeStruct((1, D), F32)],
        compiler_params=_params(("arbitrary",), 48),
    )(d_glu, dp, dgates, w_in, x, g_pre, dh1, token)


def _in_proj_bwd_w(u, d_glu, dp, dgates, token):
    seq = u.shape[0]
    tm = 1024
    n_t = seq // tm

    def body(u_ref, dglu_ref, dp_ref, dg_ref, _token, out_ref, acc):
        b, i = pl.program_id(0), pl.program_id(1)
        ut = u_ref[...]

        def add(d_ref):
            _acc_out(acc, i == 0, _dot_tn(ut, d_ref[...]))

        pl.when(b < 2)(lambda: add(dglu_ref))
        pl.when(b == 2)(lambda: add(dp_ref))
        pl.when(b > 2)(lambda: add(dg_ref))

        @pl.when(i == n_t - 1)
        def _():
            out_ref[...] = acc[...].astype(BF16)

    return pl.pallas_call(
        body, name="in_proj_bwd_w", grid=(IN_COLS // D, n_t),
        in_specs=[pl.BlockSpec((tm, D), lambda b, i: (i, 0)),
                  pl.BlockSpec((tm, D), lambda b, i: (jnp.where(b < 2, i, 0), jnp.minimum(b, 1))),
                  pl.BlockSpec((tm, D), lambda b, i: (jnp.where(b == 2, i, 0), 0)),
                  pl.BlockSpec((tm, D), lambda b, i: (jnp.where(b > 2, i, 0), jnp.maximum(b - 3, 0))), ANY],
        out_specs=pl.BlockSpec((D, D), lambda b, i: (0, b)),
        out_shape=jax.ShapeDtypeStruct((D, IN_COLS), BF16),
        scratch_shapes=[pltpu.VMEM((D, D), F32)],
        compiler_params=_params(("arbitrary", "arbitrary"), 40),
    )(u, d_glu, dp, dgates, token)


VEC_NAMES = ("mix_pre_g", "dw_bias", "conv_ln_g", "conv_ln_b", "pool_scale", "mix_post_g", "mlp_pre_g", "mlp_post_g")
WEIGHT_ORDER = ("mix_pre_g", "w_in", "dw_kernel", "dw_bias", "conv_ln_g", "conv_ln_b", "w_conv_out", "pool_w",
                "pool_scale", "w_pool_out", "w_o", "mix_post_g", "mlp_pre_g", "w_ff1", "w_ff2", "mlp_post_g")


def _step(x, loss_target, w, m, v):
    row = lambda a: a.reshape(1, D)
    names = [s[0] for s in SHARDED]

    stage, seeded = _stage_shards([w[n] for n in names])
    stage, seeded = dict(zip(names, stage)), dict(zip(names, seeded))
    gather_groups = (("w_in",), ("w_conv_out", "dw_kernel"), ("pool_w", "w_pool_out", "w_o"), ("w_ff1", "w_ff2"))
    first_level, token = _exchange_start(
        "gather_start", [(g, [stage[n] for n in g], [seeded[n] for n in g]) for g in gather_groups],
        [GATHER_ICI, GATHER_D2D])
    forwarded, full = {}, {}

    def forward(k, after):
        group, (ici_sems, d2d_sems), staged, landed = first_level[k]
        staged, landed = _exchange_wait("gather_ici_" + group[0], group, staged, landed, [(ici_sems, GATHER_ICI)], after)
        (second,), tok = _exchange_start("gather_forward_" + group[0], [(group, None, landed)], [GATHER_FORWARD])
        forwarded[k] = (staged, second[3], [(d2d_sems, GATHER_D2D), (second[1][0], GATHER_FORWARD)])
        return tok

    def gathered(k, after):
        group = gather_groups[k]
        staged, landed, waits = forwarded[k]
        _, arrays = _exchange_wait("gather_wait_" + group[0], group, staged, landed, waits, after)
        full.update(zip(group, arrays))

    gathered(0, forward(0, token))
    tok = forward(1, full["w_in"])
    proj, u = _in_proj_fwd(x, row(w["mix_pre_g"]), full["w_in"], tok)
    gathered(1, proj)
    tok = forward(2, proj)
    cv, y_conv = _conv_fwd(proj, full["dw_kernel"], row(w["dw_bias"]), row(w["conv_ln_g"]), row(w["conv_ln_b"]),
                           full["w_conv_out"], tok)
    gathered(2, y_conv)
    tok = forward(3, y_conv)
    z, zl, y_pool, merged, o, h1 = _pool_merge_fwd(proj, y_conv, x, full["pool_w"], row(w["pool_scale"]),
                                                   full["w_pool_out"], full["w_o"], row(w["mix_post_g"]), tok)
    gathered(3, h1)
    vv, dm, dh2, sse, g_mlp_post = _mlp_fwd(h1, row(w["mlp_pre_g"]), full["w_ff1"], full["w_ff2"],
                                            row(w["mlp_post_g"]), loss_target)

    shard_of = {name: shard for name, _, shard, _ in SHARDED}

    def scatter_start(tag, group, grads):
        landings = [lax.empty((N_DEV - 1,) + shard_of[n], g.dtype) for n, g in zip(group, grads)]
        (handle,), tok = _exchange_start("scatter_start_" + tag, [(group, grads, landings)], [SCATTER])
        return handle, tok

    dv, g_ff1, g_ff2 = _mlp_bwd(vv, dm, full["w_ff1"], full["w_ff2"])
    h_ff, tok_ff = scatter_start("ff", ("w_ff1", "w_ff2"), [g_ff1, g_ff2])
    dh1, dy_conv, dy_pool, dgates, g_wo, g_mlp_pre, g_mix_post = _merge_bwd(
        dh2, dv, h1, row(w["mlp_pre_g"]), o, row(w["mix_post_g"]), full["w_o"], merged, proj, y_conv, y_pool, tok_ff)
    dp, g_wpo, g_pw, g_pool_scale = _pool_bwd(dy_pool, zl, z, full["pool_w"], row(w["pool_scale"]), full["w_pool_out"])
    d_glu, g_wco, g_dw, g_bias, g_ln_g, g_ln_b = _conv_bwd(dy_conv, cv, proj, full["dw_kernel"], row(w["conv_ln_g"]),
                                                            row(w["conv_ln_b"]), full["w_conv_out"])
    h_mix, tok_mix = scatter_start("mix", ("w_o", "w_pool_out", "pool_w", "w_conv_out", "dw_kernel"),
                                   [g_wo, g_wpo, g_pw, g_wco, g_dw[:, :TAPS, :]])
    g_win = _in_proj_bwd_w(u, d_glu, dp, dgates, tok_mix)
    h_in, tok_in = scatter_start("in", ("w_in",), [g_win])
    grad_x, g_mix_pre = _in_proj_bwd_x(d_glu, dp, dgates, full["w_in"], x, row(w["mix_pre_g"]), dh1, tok_in)

    out = {}
    slot = jnp.reshape(_my_slot(), (1,)).astype(jnp.int32)
    after = grad_x
    for tag, (group, (sems,), grads, landings) in (("ff", h_ff), ("mix", h_mix), ("in", h_in)):
        mine, landed = _exchange_wait("scatter_wait_" + tag, group, grads, landings, [(sems, SCATTER)], after)
        for name, own, parts in zip(group, mine, landed):
            out[name] = _adamw(name, own, parts, w[name], m[name], v[name], slot)
            after = out[name][0]
    vec_parts = _vec_exchange(
        [g_mix_pre, g_bias, g_ln_g, g_ln_b, g_pool_scale, g_mix_post, g_mlp_pre, g_mlp_post])
    stack = lambda d: jnp.stack([d[n] for n in VEC_NAMES], axis=0)
    res = _adamw_vectors(vec_parts, stack(w), stack(m), stack(v))
    for k, name in enumerate(VEC_NAMES):
        out[name] = [r[k] for r in res]

    loss = lax.psum(sse[0, 0] * (0.5 / D), ("x", "y", "c"))
    return loss, grad_x, out


def kernel(x, mix_pre_g, w_in, dw_kernel, dw_bias, conv_ln_g, conv_ln_b, w_conv_out, pool_w, pool_scale, w_pool_out, w_o, mix_post_g, mlp_pre_g, w_ff1, w_ff2, mlp_post_g, loss_target, m_mix_pre_g, m_w_in, m_dw_kernel, m_dw_bias, m_conv_ln_g, m_conv_ln_b, m_w_conv_out, m_pool_w, m_pool_scale, m_w_pool_out, m_w_o, m_mix_post_g, m_mlp_pre_g, m_w_ff1, m_w_ff2, m_mlp_post_g, v_mix_pre_g, v_w_in, v_dw_kernel, v_dw_bias, v_conv_ln_g, v_conv_ln_b, v_w_conv_out, v_pool_w, v_pool_scale, v_w_pool_out, v_w_o, v_mix_post_g, v_mlp_pre_g, v_w_ff1, v_w_ff2, v_mlp_post_g):
    w = dict(mix_pre_g=mix_pre_g, w_in=w_in, dw_kernel=dw_kernel, dw_bias=dw_bias, conv_ln_g=conv_ln_g, conv_ln_b=conv_ln_b,
             w_conv_out=w_conv_out, pool_w=pool_w, pool_scale=pool_scale, w_pool_out=w_pool_out, w_o=w_o,
             mix_post_g=mix_post_g, mlp_pre_g=mlp_pre_g, w_ff1=w_ff1, w_ff2=w_ff2, mlp_post_g=mlp_post_g)
    m = dict(mix_pre_g=m_mix_pre_g, w_in=m_w_in, dw_kernel=m_dw_kernel, dw_bias=m_dw_bias, conv_ln_g=m_conv_ln_g,
             conv_ln_b=m_conv_ln_b, w_conv_out=m_w_conv_out, pool_w=m_pool_w, pool_scale=m_pool_scale,
             w_pool_out=m_w_pool_out, w_o=m_w_o, mix_post_g=m_mix_post_g, mlp_pre_g=m_mlp_pre_g, w_ff1=m_w_ff1,
             w_ff2=m_w_ff2, mlp_post_g=m_mlp_post_g)
    v = dict(mix_pre_g=v_mix_pre_g, w_in=v_w_in, dw_kernel=v_dw_kernel, dw_bias=v_dw_bias, conv_ln_g=v_conv_ln_g,
             conv_ln_b=v_conv_ln_b, w_conv_out=v_w_conv_out, pool_w=v_pool_w, pool_scale=v_pool_scale,
             w_pool_out=v_w_pool_out, w_o=v_w_o, mix_post_g=v_mix_post_g, mlp_pre_g=v_mlp_pre_g, w_ff1=v_w_ff1,
             w_ff2=v_w_ff2, mlp_post_g=v_mlp_post_g)
    seq = x.shape[1]
    loss, grad_x, out = _step(x.reshape(seq, D), loss_target.reshape(seq, D), w, m, v)
    grads, deltas, new_m, new_v = ([out[n][k] for n in WEIGHT_ORDER] for k in range(4))
    return (loss, grad_x.reshape(x.shape), *grads, *deltas, *new_m, *new_v)
```

```python
import collections

import jax
import jax.numpy as jnp
from jax import lax
from jax.experimental import pallas as pl
from jax.experimental.pallas import tpu as pltpu

D = 1024
D_FF = 4 * D
IN_COLS = 5 * D
TAPS = 31
HALO = 32
POOL_WINDOWS = (2, 4, 8, 16)
PG = D // 4
N_DEV = 8
RMS_EPS = 1e-6
LN_EPS = 1e-5
ADAM_LR, ADAM_B1, ADAM_B2, ADAM_EPS, ADAM_WD, ADAM_STEP = 0.001, 0.9, 0.999, 1e-08, 0.01, 10

BF16 = jnp.bfloat16
F32 = jnp.float32
MIB = 1 << 20
MESH = pl.DeviceIdType.MESH


def _params(sem, vmem_mib):
    return pltpu.CompilerParams(dimension_semantics=sem, vmem_limit_bytes=vmem_mib * MIB)


def _dot(a, b):
    return jnp.dot(a, b, preferred_element_type=F32)


def _dot_nt(a, b):
    return lax.dot_general(a, b, (((1,), (1,)), ((), ())), preferred_element_type=F32)


def _dot_tn(a, b):
    return lax.dot_general(a, b, (((0,), (0,)), ((), ())), preferred_element_type=F32)


def _rms_scale(x):
    return lax.rsqrt(jnp.mean(x * x, axis=-1, keepdims=True) + RMS_EPS)


def _rms_bwd(dy, x, g):
    xn = x * _rms_scale(x)
    dn = dy * g
    dx = _rms_scale(x) * (dn - xn * jnp.mean(dn * xn, axis=-1, keepdims=True))
    return dx, jnp.sum(dy * xn, axis=0, keepdims=True)


def _sigmoid(x):
    return jax.nn.sigmoid(x)


def _acc_out(ref, first, value):
    @pl.when(first)
    def _():
        ref[...] = value

    @pl.when(jnp.logical_not(first))
    def _():
        ref[...] += value


def _my_slot():
    return 4 * lax.axis_index("x") + 2 * lax.axis_index("y") + lax.axis_index("c")


def _peer(mask):
    x, y, c = lax.axis_index("x"), lax.axis_index("y"), lax.axis_index("c")
    return (x ^ ((mask >> 2) & 1), y ^ ((mask >> 1) & 1), c ^ (mask & 1))


def _cols(width):
    return lambda ref, slot: ref.at[:, pl.ds(pl.multiple_of(slot * width, 128), width)]


def _rows(height):
    return lambda ref, slot: ref.at[pl.ds(pl.multiple_of(slot * height, 8), height), :]


def _lead(ref, slot):
    return ref.at[slot]


def _pool_rows(ref, slot):
    return ref.at[:, pl.ds(pl.multiple_of(slot * (PG // N_DEV), 8), PG // N_DEV), :]


SHARDED = (
    ("w_in", (D, IN_COLS), (D, IN_COLS // N_DEV), _cols(IN_COLS // N_DEV)),
    ("w_ff1", (D, D_FF), (D, D_FF // N_DEV), _cols(D_FF // N_DEV)),
    ("w_ff2", (D_FF, D), (D_FF // N_DEV, D), _rows(D_FF // N_DEV)),
    ("w_conv_out", (D, D), (D // N_DEV, D), _rows(D // N_DEV)),
    ("w_pool_out", (D, D), (D // N_DEV, D), _rows(D // N_DEV)),
    ("w_o", (D, D), (D // N_DEV, D), _rows(D // N_DEV)),
    ("pool_w", (4, PG, PG), (4, PG // N_DEV, PG), _pool_rows),
    ("dw_kernel", (N_DEV, TAPS, D // N_DEV), (TAPS, D // N_DEV), _lead),
)
N_SHARDED = len(SHARDED)


SHARD_AT = {name: at for name, _, _, at in SHARDED}
HBM = pl.BlockSpec(memory_space=pltpu.HBM)
SEM = pl.BlockSpec(memory_space=pltpu.SEMAPHORE)
ANY = pl.BlockSpec(memory_space=pl.ANY)
EFFECT = pltpu.SideEffectType.DATAFLOW_SIDE_EFFECTING


def _in_hbm(a):
    return pltpu.with_memory_space_constraint(a, pltpu.HBM)


def _pinned(*arrays):
    return [_in_hbm(a) for a in arrays]


def _stage_shards(shards):
    dtypes = [F32 if name == "dw_kernel" else BF16 for name, *_ in SHARDED]

    def body(*refs):
        ins = refs[:N_SHARDED]
        stage = refs[N_SHARDED:2 * N_SHARDED]
        fulls = refs[2 * N_SHARDED:3 * N_SHARDED]
        sems = refs[3 * N_SHARDED]
        me = _my_slot()
        copies = []
        for a, (_, _, _, at) in enumerate(SHARDED):
            stage[a][...] = ins[a][...].astype(dtypes[a])
            cp = pltpu.make_async_copy(stage[a], at(fulls[a], me), sems.at[a])
            cp.start()
            copies.append(cp)
        for cp in copies:
            cp.wait()

    outs = pl.pallas_call(
        body, name="stage_shards",
        out_shape=[jax.ShapeDtypeStruct(shard, dt) for (_, _, shard, _), dt in zip(SHARDED, dtypes)]
        + [jax.ShapeDtypeStruct(full, dt) for (_, full, _, _), dt in zip(SHARDED, dtypes)],
        in_specs=[pl.BlockSpec(memory_space=pltpu.VMEM)] * N_SHARDED,
        out_specs=[pl.BlockSpec(memory_space=pltpu.VMEM)] * N_SHARDED + [pl.BlockSpec(memory_space=pl.ANY)] * N_SHARDED,
        scratch_shapes=[pltpu.SemaphoreType.DMA((N_SHARDED,))],
        compiler_params=pltpu.CompilerParams(vmem_limit_bytes=40 * MIB),
    )(*shards)
    return outs[:N_SHARDED], outs[N_SHARDED:]


Leg = collections.namedtuple("Leg", "routes src_of dst_of src_is_land")


def _sem_index(k, m):
    return k * (N_DEV - 1) + m - 1


def _exchange_start(name, groups, legs):
    sizes = [len(g[0]) for g in groups]
    names = [nm for g in groups for nm in g[0]]
    srcs = [s for g in groups if g[1] is not None for s in g[1]]
    lands = [l for g in groups for l in g[2]]
    n_src, n, n_g, n_l = len(srcs), len(names), len(groups), len(legs)

    def body(*refs):
        src_refs, land_refs = list(refs[:n_src]), refs[n_src:n_src + n]
        sems = refs[n_src + n:n_src + n + 2 * n_g * n_l]
        token = refs[-1]
        me = _my_slot()
        first = 0
        for g, size in enumerate(sizes):
            own_src = [src_refs.pop(0) for _ in range(size)] if groups[g][1] is not None else None
            for li, leg in enumerate(legs):
                send, recv = sems[2 * (g * n_l + li)], sems[2 * (g * n_l + li) + 1]
                for m, via in leg.routes:
                    for k in range(size):
                        land = land_refs[first + k]
                        src = land if leg.src_is_land else own_src[k]
                        pltpu.make_async_remote_copy(
                            src_ref=leg.src_of(names[first + k], src, me, m), dst_ref=leg.dst_of(names[first + k], land, me, m),
                            send_sem=send.at[_sem_index(k, m)], recv_sem=recv.at[_sem_index(k, m)],
                            device_id=_peer(via), device_id_type=MESH).start()
            first += size
        token[...] = jnp.zeros_like(token)

    sem_shapes = [pltpu.SemaphoreType.DMA((size * (N_DEV - 1),)) for size in sizes for _ in range(2 * n_l)]
    n_sem = len(sem_shapes)
    outs = pl.pallas_call(
        body, name=name,
        out_shape=sem_shapes + [pltpu.HBM(a.shape, a.dtype) for a in srcs + lands] + [jax.ShapeDtypeStruct((8, 128), F32)],
        in_specs=[HBM] * (n_src + n),
        out_specs=[SEM] * n_sem + [HBM] * (n_src + n) + [pl.BlockSpec(memory_space=pltpu.VMEM)],
        input_output_aliases={k: n_sem + k for k in range(n_src + n)},
        compiler_params=pltpu.CompilerParams(has_side_effects=EFFECT),
    )(*[_in_hbm(a) for a in srcs + lands])
    sems, thru, token = outs[:n_sem], list(outs[n_sem:-1]), outs[-1]
    src_thru, land_thru = thru[:n_src], thru[n_src:]
    handles, first = [], 0
    for g, size in enumerate(sizes):
        pairs = [(sems[2 * (g * n_l + li)], sems[2 * (g * n_l + li) + 1]) for li in range(n_l)]
        mine = [src_thru.pop(0) for _ in range(size)] if groups[g][1] is not None else None
        handles.append((groups[g][0], pairs, mine, land_thru[first:first + size]))
        first += size
    return handles, token


def _exchange_wait(name, names, srcs, lands, waits, after):
    n = len(names)
    n_src = n if srcs is not None else 0

    def body(*refs):
        src_refs, land_refs = refs[:n_src], refs[n_src:n_src + n]
        sems = refs[n_src + n:n_src + n + 2 * len(waits)]
        me = _my_slot()
        for wi, (_, leg) in enumerate(waits):
            for m, via in leg.routes:
                for k in range(n):
                    src = land_refs[k] if leg.src_is_land else src_refs[k]
                    cp = pltpu.make_async_remote_copy(
                        src_ref=leg.src_of(names[k], src, me, m), dst_ref=leg.dst_of(names[k], land_refs[k], me ^ via, m),
                        send_sem=sems[2 * wi].at[_sem_index(k, m)], recv_sem=sems[2 * wi + 1].at[_sem_index(k, m)],
                        device_id=_peer(via), device_id_type=MESH)
                    cp.wait_send()
                    cp.wait_recv()

    arrays = (list(srcs) if srcs is not None else []) + list(lands)
    outs = pl.pallas_call(
        body, name=name,
        out_shape=[pltpu.HBM(a.shape, a.dtype) for a in arrays],
        in_specs=[HBM] * len(arrays) + [SEM] * (2 * len(waits)) + [pl.BlockSpec(memory_space=pl.ANY)],
        out_specs=[HBM] * len(arrays),
        input_output_aliases={k: k for k in range(len(arrays))},
        compiler_params=pltpu.CompilerParams(has_side_effects=EFFECT),
    )(*arrays, *[s for pair, _ in waits for s in pair], after)
    return (outs[:n_src] if srcs is not None else None), outs[n_src:]


def _shard_slot(name, ref, slot):
    return SHARD_AT[name](ref, slot)


GATHER_ICI = Leg(((2, 2), (4, 4), (6, 6)), lambda name, ref, me, m: ref,
                 lambda name, ref, sender, m: _shard_slot(name, ref, sender), False)
GATHER_D2D = Leg(((1, 1),), GATHER_ICI.src_of, GATHER_ICI.dst_of, False)
GATHER_FORWARD = Leg(((2, 1), (4, 1), (6, 1)), lambda name, ref, me, m: _shard_slot(name, ref, me ^ m),
                     lambda name, ref, sender, m: _shard_slot(name, ref, sender ^ m), True)
SCATTER = Leg(tuple((m, m) for m in range(1, N_DEV)), lambda name, ref, me, m: _shard_slot(name, ref, me ^ m),
              lambda name, ref, sender, m: ref.at[m - 1], False)


def _vec_exchange(vectors, after):
    n = len(vectors)

    def body(*refs):
        vec, vec_out = refs[n + 2], refs[n + 1]
        send_sems, recv_sems, local_sem = refs[n + 3:]
        for k in range(n):
            vec[k:k + 1, :] = refs[k][...]
        me = _my_slot()
        local = pltpu.make_async_copy(vec, vec_out.at[me], local_sem)
        local.start()
        sends = []
        for mask in range(1, N_DEV):
            cp = pltpu.make_async_remote_copy(
                src_ref=vec, dst_ref=vec_out.at[me], send_sem=send_sems.at[mask - 1],
                recv_sem=recv_sems.at[mask - 1], device_id=_peer(mask), device_id_type=MESH)
            cp.start()
            sends.append(cp)
        for mask in range(1, N_DEV):
            pltpu.make_async_remote_copy(
                src_ref=vec, dst_ref=vec_out.at[me ^ mask], send_sem=send_sems.at[mask - 1],
                recv_sem=recv_sems.at[mask - 1], device_id=_peer(mask), device_id_type=MESH).wait_recv()
        for cp in sends:
            cp.wait_send()
        local.wait()

    return pl.pallas_call(
        body, name="vec_exchange",
        out_shape=jax.ShapeDtypeStruct((N_DEV, n, D), F32),
        in_specs=[pl.BlockSpec(memory_space=pltpu.VMEM)] * n + [ANY],
        out_specs=pl.BlockSpec(memory_space=pl.ANY),
        scratch_shapes=[pltpu.VMEM((n, D), F32), pltpu.SemaphoreType.DMA((N_DEV - 1,)),
                        pltpu.SemaphoreType.DMA((N_DEV - 1,)), pltpu.SemaphoreType.DMA],
    )(*vectors, after)


def _adamw_update(g, w_ref, m_ref, v_ref, g_out, d_out, m_out, v_out):
    m_new = ADAM_B1 * m_ref[...] + (1.0 - ADAM_B1) * g
    v_new = ADAM_B2 * v_ref[...] + (1.0 - ADAM_B2) * (g * g)
    m_hat = m_new / (1.0 - ADAM_B1 ** ADAM_STEP)
    v_hat = v_new / (1.0 - ADAM_B2 ** ADAM_STEP)
    g_out[...] = g
    d_out[...] = -ADAM_LR * (m_hat / (jnp.sqrt(v_hat) + ADAM_EPS) + ADAM_WD * w_ref[...])
    m_out[...] = m_new
    v_out[...] = v_new


ADAMW_ROWS = 256


def _adamw(name, own, parts, w, m, v, slot):
    shard = w.shape
    if name in ("w_in", "w_ff1"):
        tr = ADAMW_ROWS
        grid = (shard[0] // tr,)
        own_spec = pl.BlockSpec((tr, shard[1]), lambda i, s: (i, s[0]))
        blk = pl.BlockSpec((tr, shard[1]), lambda i, s: (i, 0))
        parts_spec = pl.BlockSpec((N_DEV - 1, tr, shard[1]), lambda i, s: (0, i, 0))
    elif name == "pool_w":
        grid = (shard[0],)
        own_spec = pl.BlockSpec((None,) + shard[1:], lambda g, s: (g, s[0], 0))
        blk = pl.BlockSpec((None,) + shard[1:], lambda g, s: (g, 0, 0))
        parts_spec = pl.BlockSpec((N_DEV - 1, None) + shard[1:], lambda g, s: (0, g, 0, 0))
    elif name == "dw_kernel":
        grid = (1,)
        own_spec = pl.BlockSpec((None,) + shard, lambda i, s: (s[0], 0, 0))
        blk = pl.BlockSpec(shard, lambda i, s: (0, 0))
        parts_spec = pl.BlockSpec((N_DEV - 1,) + shard, lambda i, s: (0, 0, 0))
    else:
        tr = min(ADAMW_ROWS, shard[0])
        grid = (shard[0] // tr,)
        own_spec = pl.BlockSpec((tr, shard[1]), lambda i, s: (s[0] * grid[0] + i, 0))
        blk = pl.BlockSpec((tr, shard[1]), lambda i, s: (i, 0))
        parts_spec = pl.BlockSpec((N_DEV - 1, tr, shard[1]), lambda i, s: (0, i, 0))

    def body(slot_ref, own_ref, p_ref, w_ref, m_ref, v_ref, g_out, d_out, m_out, v_out):
        g = own_ref[...].astype(F32)
        for k in range(N_DEV - 1):
            g = g + p_ref[k].astype(F32)
        _adamw_update(g, w_ref, m_ref, v_ref, g_out, d_out, m_out, v_out)

    return pl.pallas_call(
        body, name="adamw_" + name,
        grid_spec=pltpu.PrefetchScalarGridSpec(
            num_scalar_prefetch=1, grid=grid, in_specs=[own_spec, parts_spec, blk, blk, blk], out_specs=[blk] * 4),
        out_shape=[jax.ShapeDtypeStruct(shard, F32)] * 4,
        compiler_params=_params(("arbitrary",), 32),
    )(slot, *_pinned(own, parts, w, m, v))


def _adamw_vectors(parts, w, m, v):
    def body(p_ref, w_ref, m_ref, v_ref, g_out, d_out, m_out, v_out):
        g = p_ref[0]
        for s in range(1, N_DEV):
            g = g + p_ref[s]
        _adamw_update(g, w_ref, m_ref, v_ref, g_out, d_out, m_out, v_out)

    return pl.pallas_call(
        body, name="adamw_vectors",
        out_shape=[jax.ShapeDtypeStruct(w.shape, F32)] * 4,
    )(parts, w, m, v)


def _in_proj_fwd(x, g_pre, w_in, token):
    seq = x.shape[0]
    tm, tn = 1024, IN_COLS // 4

    def body(x_ref, g_ref, w_ref, _token, proj_ref, u_ref):
        @pl.when(pl.program_id(1) == 0)
        def _():
            xf = x_ref[...]
            u_ref[...] = (xf * _rms_scale(xf) * g_ref[...]).astype(BF16)
        proj_ref[...] = _dot(u_ref[...], w_ref[...])

    return pl.pallas_call(
        body, name="in_proj_fwd", grid=(seq // tm, IN_COLS // tn),
        in_specs=[pl.BlockSpec((tm, D), lambda i, j: (i, 0)), pl.BlockSpec((1, D), lambda i, j: (0, 0)),
                  pl.BlockSpec((D, tn), lambda i, j: (0, j)), ANY],
        out_specs=[pl.BlockSpec((tm, tn), lambda i, j: (i, j)), pl.BlockSpec((tm, D), lambda i, j: (i, 0))],
        out_shape=[jax.ShapeDtypeStruct((seq, IN_COLS), F32), jax.ShapeDtypeStruct((seq, D), BF16)],
        compiler_params=_params(("arbitrary", "arbitrary"), 40),
    )(*_pinned(x, g_pre, w_in, token))


CONV_TM = 256
CONV_RS = 64


def _shifts():
    return [(b, [(a, 8 * a + b) for a in range(4) if 8 * a + b < TAPS]) for b in range(8)]


def _taps_looking_back(buf, row0, lanes, weight):
    acc = None
    for b, group in _shifts():
        part = None
        for a, s in group:
            term = weight(TAPS - 1 - s) * buf[pl.ds(row0 - 8 - 8 * a, CONV_RS + 8), lanes]
            part = term if part is None else part + term
        if b:
            part = pltpu.roll(part, b, 0)
        acc = part[8:, :] if acc is None else acc + part[8:, :]
    return acc


def _taps_looking_ahead(buf, row0, lanes, weight):
    acc = None
    for b, group in _shifts():
        part = None
        for a, s in group:
            term = weight(TAPS - 1 - s) * buf[pl.ds(row0 + 8 * a, CONV_RS + 8), lanes]
            part = term if part is None else part + term
        if b:
            part = pltpu.roll(part, CONV_RS + 8 - b, 0)
        acc = part[:CONV_RS, :] if acc is None else acc + part[:CONV_RS, :]
    return acc


def _layer_norm_parts(cv):
    mu = jnp.mean(cv, axis=-1, keepdims=True)
    cen = cv - mu
    rstd = lax.rsqrt(jnp.mean(cen * cen, axis=-1, keepdims=True) + LN_EPS)
    return cen * rstd, rstd


def _conv_fwd(proj, dw, dw_bias, ln_g, ln_b, w_conv_out, token):
    seq = proj.shape[0]
    tm = CONV_TM

    def body(a_ref, gate_ref, dw_ref, bias_ref, lg_ref, lb_ref, w_ref, _token, cv_ref, y_ref, ug):
        i = pl.program_id(0)

        @pl.when(i == 0)
        def _():
            ug[0:HALO, :] = jnp.zeros((HALO, D), F32)

        @pl.when(i > 0)
        def _():
            ug[0:HALO, :] = ug[tm:tm + HALO, :]
        ug[HALO:HALO + tm, :] = a_ref[...] * _sigmoid(gate_ref[...])

        def channel_block(cb, carry):
            lanes = pl.ds(pl.multiple_of(cb * 128, 128), 128)
            for r0 in range(0, tm, CONV_RS):
                taps = _taps_looking_back(ug, HALO + r0, lanes, lambda k: dw_ref[cb, k:k + 1, :])
                cv_ref[pl.ds(r0, CONV_RS), lanes] = taps + bias_ref[:, lanes]
            return carry
        lax.fori_loop(0, D // 128, channel_block, 0)

        n, _ = _layer_norm_parts(cv_ref[...])
        ln = n * lg_ref[...] + lb_ref[...]
        y_ref[...] = _dot((ln * _sigmoid(ln)).astype(BF16), w_ref[...])

    vec = pl.BlockSpec((1, D), lambda i: (0, 0))
    tile = pl.BlockSpec((tm, D), lambda i: (i, 0))
    return pl.pallas_call(
        body, name="conv_fwd", grid=(seq // tm,),
        in_specs=[pl.BlockSpec((tm, D), lambda i: (i, 0)), pl.BlockSpec((tm, D), lambda i: (i, 1)),
                  pl.BlockSpec((N_DEV, TAPS, 128), lambda i: (0, 0, 0)), vec, vec, vec,
                  pl.BlockSpec((D, D), lambda i: (0, 0)), ANY],
        out_specs=[tile, tile],
        out_shape=[jax.ShapeDtypeStruct((seq, D), F32)] * 2,
        scratch_shapes=[pltpu.VMEM((HALO + tm, D), F32)],
        compiler_params=_params(("arbitrary",), 32),
    )(*_pinned(proj, proj, dw, dw_bias, ln_g, ln_b, w_conv_out, token))


def _pool_counts(tile_index, tm, window):
    t = tile_index * tm + lax.broadcasted_iota(jnp.int32, (tm, 1), 0)
    return 1.0 / jnp.minimum(t + 1, window).astype(F32)


def _pool_merge_fwd(proj, y_conv, x, pool_w, pool_scale, w_pool_out, w_o, g_post, token):
    seq = proj.shape[0]
    tm = CONV_TM

    def body(p_ref, gc_ref, gp_ref, yc_ref, x_ref, pw_ref, ps_ref, wpo_ref, wo_ref, g_ref, _token,
             z_ref, zl_ref, yp_ref, mg_ref, o_ref, h1_ref, pbuf):
        i = pl.program_id(0)

        @pl.when(i == 0)
        def _():
            pbuf[0:HALO, :] = jnp.zeros((HALO, D), F32)

        @pl.when(i > 0)
        def _():
            pbuf[0:HALO, :] = pbuf[tm:tm + HALO, :]
        pbuf[HALO:HALO + tm, :] = p_ref[...]

        for g, window in enumerate(POOL_WINDOWS):
            lanes = pl.ds(g * PG, PG)
            acc = pbuf[pl.ds(HALO, tm), lanes]
            for j in range(1, window):
                acc = acc + pbuf[pl.ds(HALO - j, tm), lanes]
            zg = acc * _pool_counts(i, tm, window) - pbuf[pl.ds(HALO, tm), lanes]
            z_ref[:, lanes] = zg.astype(BF16)
            zl_ref[:, lanes] = _dot(zg.astype(BF16), pw_ref[g])
        zl = zl_ref[...]
        y_pool = _dot((zl * ps_ref[...]).astype(BF16), wpo_ref[...])
        yp_ref[...] = y_pool
        merged = (_sigmoid(gc_ref[...]) * yc_ref[...] + _sigmoid(gp_ref[...]) * y_pool).astype(BF16)
        mg_ref[...] = merged
        o = _dot(merged, wo_ref[...])
        o_ref[...] = o
        h1_ref[...] = x_ref[...] + o * _rms_scale(o) * g_ref[...]

    vec = pl.BlockSpec((1, D), lambda i: (0, 0))
    tile = pl.BlockSpec((tm, D), lambda i: (i, 0))
    mat = pl.BlockSpec((D, D), lambda i: (0, 0))
    return pl.pallas_call(
        body, name="pool_merge_fwd", grid=(seq // tm,),
        in_specs=[pl.BlockSpec((tm, D), lambda i: (i, 2)), pl.BlockSpec((tm, D), lambda i: (i, 3)),
                  pl.BlockSpec((tm, D), lambda i: (i, 4)), tile, tile,
                  pl.BlockSpec((4, PG, PG), lambda i: (0, 0, 0)), vec, mat, mat, vec, ANY],
        out_specs=[tile] * 6,
        out_shape=[jax.ShapeDtypeStruct((seq, D), dt) for dt in (BF16, F32, F32, BF16, F32, F32)],
        scratch_shapes=[pltpu.VMEM((HALO + tm, D), F32)],
        compiler_params=_params(("arbitrary",), 48),
    )(*_pinned(proj, proj, proj, y_conv, x, pool_w, pool_scale, w_pool_out, w_o, g_post, token))


def _mlp_fwd(h1, g_pre, w_ff1, w_ff2, g_post, target):
    seq = h1.shape[0]
    tm, tf = 1024, D_FF // N_DEV
    n_f = D_FF // tf

    def body(h1_ref, gpre_ref, w1_ref, w2_ref, gpost_ref, tgt_ref, v_ref, dm_ref, dh2_ref, sse_ref, ggrad_ref, macc):
        i, j = pl.program_id(0), pl.program_id(1)

        @pl.when(j == 0)
        def _():
            h = h1_ref[...]
            v_ref[...] = (h * _rms_scale(h) * gpre_ref[...]).astype(BF16)
        f = jnp.maximum(_dot(v_ref[...], w1_ref[...]), 0.0)
        part = _dot((f * f).astype(BF16), w2_ref[...])

        @pl.when(j == 0)
        def _():
            macc[...] = part

        @pl.when(j > 0)
        def _():
            macc[...] += part

        @pl.when(j == n_f - 1)
        def _():
            mo = macc[...]
            err = h1_ref[...] + mo * _rms_scale(mo) * gpost_ref[...] - tgt_ref[...]
            dh2 = err * (1.0 / D)
            dh2_ref[...] = dh2
            dm, ggrad = _rms_bwd(dh2, mo, gpost_ref[...])
            dm_ref[...] = dm.astype(BF16)
            _acc_out(ggrad_ref, i == 0, ggrad)
            _acc_out(sse_ref, i == 0, jnp.sum(jnp.sum(err * err, axis=1, keepdims=True), axis=0, keepdims=True))

    vec = pl.BlockSpec((1, D), lambda i, j: (0, 0))
    tile = pl.BlockSpec((tm, D), lambda i, j: (i, 0))
    return pl.pallas_call(
        body, name="mlp_fwd", grid=(seq // tm, n_f),
        in_specs=[tile, vec, pl.BlockSpec((D, tf), lambda i, j: (0, j)), pl.BlockSpec((tf, D), lambda i, j: (j, 0)), vec, tile],
        out_specs=[tile, tile, tile, pl.BlockSpec((1, 1), lambda i, j: (0, 0)), vec],
        out_shape=[jax.ShapeDtypeStruct((seq, D), BF16), jax.ShapeDtypeStruct((seq, D), BF16),
                   jax.ShapeDtypeStruct((seq, D), F32), jax.ShapeDtypeStruct((1, 1), F32),
                   jax.ShapeDtypeStruct((1, D), F32)],
        scratch_shapes=[pltpu.VMEM((tm, D), F32)],
        compiler_params=_params(("arbitrary", "arbitrary"), 56),
    )(*_pinned(h1, g_pre, w_ff1, w_ff2, g_post, target))


def _mlp_bwd(v, dm, w_ff1, w_ff2):
    seq = v.shape[0]
    tm, tf = 1024, D_FF // N_DEV
    n_t = seq // tm

    def body(v_ref, dm_ref, w1_ref, w2_ref, dv_hbm, g1_ref, g2_ref, dv_acc, g1_acc, g2_acc, sem):
        j, i = pl.program_id(0), pl.program_id(1)
        vt, dmt = v_ref[...], dm_ref[...]
        f = jnp.maximum(_dot(vt, w1_ref[...]), 0.0)
        df = (_dot_nt(dmt, w2_ref[...]) * (2.0 * f)).astype(BF16)
        rows = pl.ds(pl.multiple_of(i * tm, tm), tm)
        dv_part = _dot_nt(df, w1_ref[...])

        @pl.when(j == 0)
        def _():
            dv_acc[rows, :] = dv_part

        @pl.when(j > 0)
        def _():
            dv_acc[rows, :] += dv_part
        g1_part = _dot_tn(vt, df)
        g2_part = _dot_tn((f * f).astype(BF16), dmt)

        @pl.when(i == 0)
        def _():
            g1_acc[...] = g1_part
            g2_acc[...] = g2_part

        @pl.when(i > 0)
        def _():
            g1_acc[...] += g1_part
            g2_acc[...] += g2_part

        @pl.when(i == n_t - 1)
        def _():
            g1_ref[...] = g1_acc[...].astype(BF16)
            g2_ref[...] = g2_acc[...].astype(BF16)

        @pl.when(jnp.logical_and(i == n_t - 1, j == D_FF // tf - 1))
        def _():
            cp = pltpu.make_async_copy(dv_acc, dv_hbm, sem)
            cp.start()
            cp.wait()

    tile = pl.BlockSpec((tm, D), lambda j, i: (i, 0))
    return pl.pallas_call(
        body, name="mlp_bwd", grid=(D_FF // tf, n_t),
        in_specs=[tile, tile, pl.BlockSpec((D, tf), lambda j, i: (0, j)), pl.BlockSpec((tf, D), lambda j, i: (j, 0))],
        out_specs=[pl.BlockSpec(memory_space=pl.ANY), pl.BlockSpec((D, tf), lambda j, i: (0, j)),
                   pl.BlockSpec((tf, D), lambda j, i: (j, 0))],
        out_shape=[jax.ShapeDtypeStruct((seq, D), F32), jax.ShapeDtypeStruct((D, D_FF), BF16),
                   jax.ShapeDtypeStruct((D_FF, D), BF16)],
        scratch_shapes=[pltpu.VMEM((seq, D), F32), pltpu.VMEM((D, tf), F32), pltpu.VMEM((tf, D), F32),
                        pltpu.SemaphoreType.DMA],
        compiler_params=_params(("arbitrary", "arbitrary"), 52),
    )(*_pinned(v, dm, w_ff1, w_ff2))


def _merge_bwd(dh2, dv, h1, g_mlp_pre, o, g_mix_post, w_o, merged, proj, y_conv, y_pool, token):
    seq = dh2.shape[0]
    tm = 256
    n_t = seq // tm

    def body(dh2_ref, dv_ref, h1_ref, gpre_ref, o_ref, gpost_ref, wo_ref, mg_ref, gc_ref, gp_ref, yc_ref, yp_ref, _token,
             dh1_ref, dyc_ref, dyp_ref, dg_ref, gwo_ref, ggpre_ref, ggpost_ref, gwo_acc):
        i = pl.program_id(0)
        dnorm, ggpre = _rms_bwd(dv_ref[...], h1_ref[...], gpre_ref[...])
        dh1 = dh2_ref[...] + dnorm
        dh1_ref[...] = dh1
        do, ggpost = _rms_bwd(dh1, o_ref[...], gpost_ref[...])
        do = do.astype(BF16)
        _acc_out(ggpre_ref, i == 0, ggpre)
        _acc_out(ggpost_ref, i == 0, ggpost)
        _acc_out(gwo_acc, i == 0, _dot_tn(mg_ref[...], do))
        dmerged = _dot_nt(do, wo_ref[...])
        sc, sp = _sigmoid(gc_ref[...]), _sigmoid(gp_ref[...])
        dyc_ref[...] = (dmerged * sc).astype(BF16)
        dyp_ref[...] = (dmerged * sp).astype(BF16)
        dg_ref[:, 0:D] = (dmerged * yc_ref[...] * (sc * (1.0 - sc))).astype(BF16)
        dg_ref[:, D:2 * D] = (dmerged * yp_ref[...] * (sp * (1.0 - sp))).astype(BF16)

        @pl.when(i == n_t - 1)
        def _():
            gwo_ref[...] = gwo_acc[...].astype(BF16)

    vec = pl.BlockSpec((1, D), lambda i: (0, 0))
    tile = pl.BlockSpec((tm, D), lambda i: (i, 0))
    mat = pl.BlockSpec((D, D), lambda i: (0, 0))
    return pl.pallas_call(
        body, name="merge_bwd", grid=(n_t,),
        in_specs=[tile, tile, tile, vec, tile, vec, mat, tile,
                  pl.BlockSpec((tm, D), lambda i: (i, 3)), pl.BlockSpec((tm, D), lambda i: (i, 4)), tile, tile, ANY],
        out_specs=[tile, tile, tile, pl.BlockSpec((tm, 2 * D), lambda i: (i, 0)), mat, vec, vec],
        out_shape=[jax.ShapeDtypeStruct((seq, D), F32), jax.ShapeDtypeStruct((seq, D), BF16),
                   jax.ShapeDtypeStruct((seq, D), BF16), jax.ShapeDtypeStruct((seq, 2 * D), BF16),
                   jax.ShapeDtypeStruct((D, D), BF16), jax.ShapeDtypeStruct((1, D), F32),
                   jax.ShapeDtypeStruct((1, D), F32)],
        scratch_shapes=[pltpu.VMEM((D, D), F32)],
        compiler_params=_params(("arbitrary",), 48),
    )(*_pinned(dh2, dv, h1, g_mlp_pre, o, g_mix_post, w_o, merged, proj, proj, y_conv, y_pool, token))


def _pool_bwd(dy_pool, zl, z, pool_w, pool_scale, w_pool_out):
    seq = dy_pool.shape[0]
    tm = CONV_TM
    n_t = seq // tm

    def body(dy_ref, zl_ref, z_ref, pw_ref, ps_ref, wpo_ref, dp_ref, gwpo_ref, gpw_ref, gps_ref, qbuf, gwpo_acc, gpw_acc):
        i = pl.program_id(0)
        tile_index = n_t - 1 - i
        first = i == 0
        dy = dy_ref[...]
        zl = zl_ref[...]
        dzs = _dot_nt(dy, wpo_ref[...])
        _acc_out(gwpo_acc, first, _dot_tn((zl * ps_ref[...]).astype(BF16), dy))
        _acc_out(gps_ref, first, jnp.sum(dzs * zl, axis=0, keepdims=True))
        dzl = (dzs * ps_ref[...]).astype(BF16)

        @pl.when(first)
        def _():
            qbuf[tm:tm + HALO, :] = jnp.zeros((HALO, D), F32)

        @pl.when(jnp.logical_not(first))
        def _():
            qbuf[tm:tm + HALO, :] = qbuf[0:HALO, :]

        dzs_list = []
        for g, window in enumerate(POOL_WINDOWS):
            lanes = pl.ds(g * PG, PG)
            dzl_g = dzl[:, g * PG:(g + 1) * PG]
            dz = _dot_nt(dzl_g, pw_ref[g])
            _acc_out(gpw_acc.at[g], first, _dot_tn(z_ref[:, lanes], dzl_g))
            qbuf[pl.ds(0, tm), lanes] = dz * _pool_counts(tile_index, tm, window)
            dzs_list.append(dz)
        for g, window in enumerate(POOL_WINDOWS):
            lanes = pl.ds(g * PG, PG)
            acc = qbuf[pl.ds(0, tm), lanes]
            for j in range(1, window):
                acc = acc + qbuf[pl.ds(j, tm), lanes]
            dp_ref[:, lanes] = (acc - dzs_list[g]).astype(BF16)

        @pl.when(i == n_t - 1)
        def _():
            gwpo_ref[...] = gwpo_acc[...].astype(BF16)
            gpw_ref[...] = gpw_acc[...].astype(BF16)

    vec = pl.BlockSpec((1, D), lambda i: (0, 0))
    tile = pl.BlockSpec((tm, D), lambda i: (n_t - 1 - i, 0))
    mat = pl.BlockSpec((D, D), lambda i: (0, 0))
    pw = pl.BlockSpec((4, PG, PG), lambda i: (0, 0, 0))
    return pl.pallas_call(
        body, name="pool_bwd", grid=(n_t,),
        in_specs=[tile, tile, tile, pw, vec, mat],
        out_specs=[tile, mat, pw, vec],
        out_shape=[jax.ShapeDtypeStruct((seq, D), BF16), jax.ShapeDtypeStruct((D, D), BF16),
                   jax.ShapeDtypeStruct((4, PG, PG), BF16), jax.ShapeDtypeStruct((1, D), F32)],
        scratch_shapes=[pltpu.VMEM((tm + HALO, D), F32), pltpu.VMEM((D, D), F32), pltpu.VMEM((4, PG, PG), F32)],
        compiler_params=_params(("arbitrary",), 40),
    )(*_pinned(dy_pool, zl, z, pool_w, pool_scale, w_pool_out))


def _conv_bwd(dy_conv, cv, proj, dw, ln_g, ln_b, w_conv_out):
    seq = dy_conv.shape[0]
    tm = CONV_TM
    n_t = seq // tm
    halo_blocks = tm // HALO

    def body(dy_ref, cv_ref, a_ref, gate_ref, ah_ref, gh_ref, dw_ref, lg_ref, lb_ref, w_ref,
             dglu_ref, gw_ref, gdw_ref, gbias_ref, glg_ref, glb_ref, ug, dcv, dug, gw_acc):
        i = pl.program_id(0)
        tile_index = n_t - 1 - i
        first = i == 0
        dy = dy_ref[...]
        n, rstd = _layer_norm_parts(cv_ref[...])
        ln = n * lg_ref[...] + lb_ref[...]
        sg = _sigmoid(ln)
        _acc_out(gw_acc, first, _dot_tn((ln * sg).astype(BF16), dy))
        dln = _dot_nt(dy, w_ref[...]) * (sg * (1.0 + ln * (1.0 - sg)))
        _acc_out(glg_ref, first, jnp.sum(dln * n, axis=0, keepdims=True))
        _acc_out(glb_ref, first, jnp.sum(dln, axis=0, keepdims=True))
        dn = dln * lg_ref[...]
        dcv_tile = rstd * (dn - jnp.mean(dn, axis=-1, keepdims=True) - n * jnp.mean(dn * n, axis=-1, keepdims=True))
        _acc_out(gbias_ref, first, jnp.sum(dcv_tile, axis=0, keepdims=True))

        @pl.when(first)
        def _():
            dcv[tm:tm + HALO, :] = jnp.zeros((HALO, D), F32)

        @pl.when(jnp.logical_not(first))
        def _():
            dcv[tm:tm + HALO, :] = dcv[0:HALO, :]
        dcv[0:tm, :] = dcv_tile

        a, gate = a_ref[...], gate_ref[...]
        sgate = _sigmoid(gate)
        ug[HALO:HALO + tm, :] = a * sgate
        before = jnp.where(tile_index > 0, 1.0, 0.0)
        ug[0:HALO, :] = ah_ref[...] * _sigmoid(gh_ref[...]) * before

        @pl.when(first)
        def _():
            gdw_ref[...] = jnp.zeros((N_DEV, TAPS + 1, 128), F32)

        def channel_block(cb, carry):
            lanes = pl.ds(pl.multiple_of(cb * 128, 128), 128)
            for r0 in range(0, tm, CONV_RS):
                dug[pl.ds(r0, CONV_RS), lanes] = _taps_looking_ahead(dcv, r0, lanes, lambda k: dw_ref[cb, k:k + 1, :])
            for b, group in _shifts():
                sums = [jnp.zeros((8, 128), F32) for _ in group]
                for r0 in range(0, tm, CONV_RS):
                    window = ug[pl.ds(r0, CONV_RS + HALO), lanes]
                    if b:
                        window = pltpu.roll(window, b, 0)
                    d = dcv[pl.ds(r0, CONV_RS), lanes]
                    for n_a, (a, s) in enumerate(group):
                        prod = d * window[HALO - 8 * a:HALO - 8 * a + CONV_RS, :]
                        sums[n_a] = sums[n_a] + jnp.sum(prod.reshape(CONV_RS // 8, 8, 128), axis=0)
                for n_a, (a, s) in enumerate(group):
                    k = TAPS - 1 - s
                    gdw_ref[cb, k:k + 1, :] += jnp.sum(sums[n_a], axis=0, keepdims=True)
            return carry
        lax.fori_loop(0, D // 128, channel_block, 0)

        d_ug = dug[...]
        dglu_ref[:, 0:D] = (d_ug * sgate).astype(BF16)
        dglu_ref[:, D:2 * D] = (d_ug * a * (sgate * (1.0 - sgate))).astype(BF16)

        @pl.when(i == n_t - 1)
        def _():
            gw_ref[...] = gw_acc[...].astype(BF16)

    def halo_index(col):
        return lambda i: (jnp.maximum((n_t - 1 - i) * halo_blocks - 1, 0), col)

    vec = pl.BlockSpec((1, D), lambda i: (0, 0))
    tile = pl.BlockSpec((tm, D), lambda i: (n_t - 1 - i, 0))
    mat = pl.BlockSpec((D, D), lambda i: (0, 0))
    dwspec = pl.BlockSpec((N_DEV, TAPS, 128), lambda i: (0, 0, 0))
    return pl.pallas_call(
        body, name="conv_bwd", grid=(n_t,),
        in_specs=[tile, tile, pl.BlockSpec((tm, D), lambda i: (n_t - 1 - i, 0)), pl.BlockSpec((tm, D), lambda i: (n_t - 1 - i, 1)),
                  pl.BlockSpec((HALO, D), halo_index(0)), pl.BlockSpec((HALO, D), halo_index(1)), dwspec, vec, vec, mat],
        out_specs=[pl.BlockSpec((tm, 2 * D), lambda i: (n_t - 1 - i, 0)), mat,
                   pl.BlockSpec((N_DEV, TAPS + 1, 128), lambda i: (0, 0, 0)), vec, vec, vec],
        out_shape=[jax.ShapeDtypeStruct((seq, 2 * D), BF16), jax.ShapeDtypeStruct((D, D), BF16),
                   jax.ShapeDtypeStruct((N_DEV, TAPS + 1, 128), F32), jax.ShapeDtypeStruct((1, D), F32),
                   jax.ShapeDtypeStruct((1, D), F32), jax.ShapeDtypeStruct((1, D), F32)],
        scratch_shapes=[pltpu.VMEM((HALO + tm, D), F32), pltpu.VMEM((tm + HALO, D), F32), pltpu.VMEM((tm, D), F32),
                        pltpu.VMEM((D, D), F32)],
        compiler_params=_params(("arbitrary",), 48),
    )(*_pinned(dy_conv, cv, proj, proj, proj, proj, dw, ln_g, ln_b, w_conv_out))


def _in_proj_bwd_x(d_glu, dp, dgates, w_in, x, g_pre, dh1, token):
    seq = x.shape[0]
    tm = 256

    def body(dglu_ref, dp_ref, dg_ref, w_ref, x_ref, g_ref, dh1_ref, _token, dx_ref, gg_ref):
        du = _dot_nt(dglu_ref[...], w_ref[:, 0:2 * D])
        du += _dot_nt(dp_ref[...], w_ref[:, 2 * D:3 * D])
        du += _dot_nt(dg_ref[...], w_ref[:, 3 * D:5 * D])
        dnorm, gg = _rms_bwd(du, x_ref[...], g_ref[...])
        dx_ref[...] = dh1_ref[...] + dnorm
        _acc_out(gg_ref, pl.program_id(0) == 0, gg)

    vec = pl.BlockSpec((1, D), lambda i: (0, 0))
    tile = pl.BlockSpec((tm, D), lambda i: (i, 0))
    wide = pl.BlockSpec((tm, 2 * D), lambda i: (i, 0))
    return pl.pallas_call(
        body, name="in_proj_bwd_x", grid=(seq // tm,),
        in_specs=[wide, tile, wide, pl.BlockSpec((D, IN_COLS), lambda i: (0, 0)), tile, vec, tile, ANY],
        out_specs=[tile, vec],
        out_shape=[jax.ShapeDtypeStruct((seq, D), F32), jax.ShapeDtypeStruct((1, D), F32)],
        compiler_params=_params(("arbitrary",), 48),
    )(*_pinned(d_glu, dp, dgates, w_in, x, g_pre, dh1, token))


def _in_proj_bwd_w(u, d_glu, dp, dgates, token):
    seq = u.shape[0]
    tm = 1024
    n_t = seq // tm

    def body(u_ref, dglu_ref, dp_ref, dg_ref, _token, out_ref, acc):
        b, i = pl.program_id(0), pl.program_id(1)
        ut = u_ref[...]

        def add(d_ref):
            _acc_out(acc, i == 0, _dot_tn(ut, d_ref[...]))

        pl.when(b < 2)(lambda: add(dglu_ref))
        pl.when(b == 2)(lambda: add(dp_ref))
        pl.when(b > 2)(lambda: add(dg_ref))

        @pl.when(i == n_t - 1)
        def _():
            out_ref[...] = acc[...].astype(BF16)

    return pl.pallas_call(
        body, name="in_proj_bwd_w", grid=(IN_COLS // D, n_t),
        in_specs=[pl.BlockSpec((tm, D), lambda b, i: (i, 0)),
                  pl.BlockSpec((tm, D), lambda b, i: (jnp.where(b < 2, i, 0), jnp.minimum(b, 1))),
                  pl.BlockSpec((tm, D), lambda b, i: (jnp.where(b == 2, i, 0), 0)),
                  pl.BlockSpec((tm, D), lambda b, i: (jnp.where(b > 2, i, 0), jnp.maximum(b - 3, 0))), ANY],
        out_specs=pl.BlockSpec((D, D), lambda b, i: (0, b)),
        out_shape=jax.ShapeDtypeStruct((D, IN_COLS), BF16),
        scratch_shapes=[pltpu.VMEM((D, D), F32)],
        compiler_params=_params(("arbitrary", "arbitrary"), 40),
    )(*_pinned(u, d_glu, dp, dgates, token))


VEC_NAMES = ("mix_pre_g", "dw_bias", "conv_ln_g", "conv_ln_b", "pool_scale", "mix_post_g", "mlp_pre_g", "mlp_post_g")
WEIGHT_ORDER = ("mix_pre_g", "w_in", "dw_kernel", "dw_bias", "conv_ln_g", "conv_ln_b", "w_conv_out", "pool_w",
                "pool_scale", "w_pool_out", "w_o", "mix_post_g", "mlp_pre_g", "w_ff1", "w_ff2", "mlp_post_g")


def _step(x, loss_target, w, m, v):
    row = lambda a: a.reshape(1, D)
    names = [s[0] for s in SHARDED]

    stage, seeded = _stage_shards([w[n] for n in names])
    stage, seeded = dict(zip(names, stage)), dict(zip(names, seeded))
    gather_groups = (("w_in",), ("w_conv_out", "dw_kernel"), ("pool_w", "w_pool_out", "w_o"), ("w_ff1", "w_ff2"))
    first_level, token = _exchange_start(
        "gather_start", [(g, [stage[n] for n in g], [seeded[n] for n in g]) for g in gather_groups],
        [GATHER_ICI, GATHER_D2D])
    forwarded, full = {}, {}

    def forward(k, after):
        group, (ici_sems, d2d_sems), staged, landed = first_level[k]
        staged, landed = _exchange_wait("gather_ici_" + group[0], group, staged, landed, [(ici_sems, GATHER_ICI)], after)
        (second,), tok = _exchange_start("gather_forward_" + group[0], [(group, None, landed)], [GATHER_FORWARD])
        forwarded[k] = (staged, second[3], [(d2d_sems, GATHER_D2D), (second[1][0], GATHER_FORWARD)])
        return tok

    def gathered(k, after):
        group = gather_groups[k]
        staged, landed, waits = forwarded[k]
        _, arrays = _exchange_wait("gather_wait_" + group[0], group, staged, landed, waits, after)
        full.update(zip(group, arrays))

    gathered(0, forward(0, token))
    tok = forward(1, full["w_in"])
    proj, u = _in_proj_fwd(x, row(w["mix_pre_g"]), full["w_in"], tok)
    gathered(1, proj)
    tok = forward(2, proj)
    cv, y_conv = _conv_fwd(proj, full["dw_kernel"], row(w["dw_bias"]), row(w["conv_ln_g"]), row(w["conv_ln_b"]),
                           full["w_conv_out"], tok)
    gathered(2, y_conv)
    tok = forward(3, y_conv)
    z, zl, y_pool, merged, o, h1 = _pool_merge_fwd(proj, y_conv, x, full["pool_w"], row(w["pool_scale"]),
                                                   full["w_pool_out"], full["w_o"], row(w["mix_post_g"]), tok)
    gathered(3, h1)
    vv, dm, dh2, sse, g_mlp_post = _mlp_fwd(h1, row(w["mlp_pre_g"]), full["w_ff1"], full["w_ff2"],
                                            row(w["mlp_post_g"]), loss_target)

    shard_of = {name: shard for name, _, shard, _ in SHARDED}

    def scatter_start(tag, group, grads):
        landings = [lax.empty((N_DEV - 1,) + shard_of[n], g.dtype) for n, g in zip(group, grads)]
        (handle,), tok = _exchange_start("scatter_start_" + tag, [(group, grads, landings)], [SCATTER])
        return handle, tok

    dv, g_ff1, g_ff2 = _mlp_bwd(vv, dm, full["w_ff1"], full["w_ff2"])
    h_ff, tok_ff = scatter_start("ff", ("w_ff1", "w_ff2"), [g_ff1, g_ff2])
    dh1, dy_conv, dy_pool, dgates, g_wo, g_mlp_pre, g_mix_post = _merge_bwd(
        dh2, dv, h1, row(w["mlp_pre_g"]), o, row(w["mix_post_g"]), full["w_o"], merged, proj, y_conv, y_pool, tok_ff)
    dp, g_wpo, g_pw, g_pool_scale = _pool_bwd(dy_pool, zl, z, full["pool_w"], row(w["pool_scale"]), full["w_pool_out"])
    d_glu, g_wco, g_dw, g_bias, g_ln_g, g_ln_b = _conv_bwd(dy_conv, cv, proj, full["dw_kernel"], row(w["conv_ln_g"]),
                                                            row(w["conv_ln_b"]), full["w_conv_out"])
    h_mix, tok_mix = scatter_start("mix", ("w_o", "w_pool_out", "pool_w", "w_conv_out", "dw_kernel"),
                                   [g_wo, g_wpo, g_pw, g_wco, g_dw[:, :TAPS, :]])
    g_win = _in_proj_bwd_w(u, d_glu, dp, dgates, tok_mix)
    h_in, tok_in = scatter_start("in", ("w_in",), [g_win])
    grad_x, g_mix_pre = _in_proj_bwd_x(d_glu, dp, dgates, full["w_in"], x, row(w["mix_pre_g"]), dh1, tok_in)

    out = {}
    slot = jnp.reshape(_my_slot(), (1,)).astype(jnp.int32)
    after = grad_x
    for tag, (group, (sems,), grads, landings) in (("ff", h_ff), ("mix", h_mix), ("in", h_in)):
        if tag == "in":
            vec_parts = after = _vec_exchange(
                [g_mix_pre, g_bias, g_ln_g, g_ln_b, g_pool_scale, g_mix_post, g_mlp_pre, g_mlp_post], after)
        mine, landed = _exchange_wait("scatter_wait_" + tag, group, grads, landings, [(sems, SCATTER)], after)
        for name, own, parts in zip(group, mine, landed):
            out[name] = _adamw(name, own, parts, w[name], m[name], v[name], slot)
            after = out[name][0]
    stack = lambda d: jnp.stack([d[n] for n in VEC_NAMES], axis=0)
    res = _adamw_vectors(vec_parts, stack(w), stack(m), stack(v))
    for k, name in enumerate(VEC_NAMES):
        out[name] = [r[k] for r in res]

    loss = lax.psum(sse[0, 0] * (0.5 / D), ("x", "y", "c"))
    return loss, grad_x, out


def kernel(x, mix_pre_g, w_in, dw_kernel, dw_bias, conv_ln_g, conv_ln_b, w_conv_out, pool_w, pool_scale, w_pool_out, w_o, mix_post_g, mlp_pre_g, w_ff1, w_ff2, mlp_post_g, loss_target, m_mix_pre_g, m_w_in, m_dw_kernel, m_dw_bias, m_conv_ln_g, m_conv_ln_b, m_w_conv_out, m_pool_w, m_pool_scale, m_w_pool_out, m_w_o, m_mix_post_g, m_mlp_pre_g, m_w_ff1, m_w_ff2, m_mlp_post_g, v_mix_pre_g, v_w_in, v_dw_kernel, v_dw_bias, v_conv_ln_g, v_conv_ln_b, v_w_conv_out, v_pool_w, v_pool_scale, v_w_pool_out, v_w_o, v_mix_post_g, v_mlp_pre_g, v_w_ff1, v_w_ff2, v_mlp_post_g):
    w = dict(mix_pre_g=mix_pre_g, w_in=w_in, dw_kernel=dw_kernel, dw_bias=dw_bias, conv_ln_g=conv_ln_g, conv_ln_b=conv_ln_b,
             w_conv_out=w_conv_out, pool_w=pool_w, pool_scale=pool_scale, w_pool_out=w_pool_out, w_o=w_o,
             mix_post_g=mix_post_g, mlp_pre_g=mlp_pre_g, w_ff1=w_ff1, w_ff2=w_ff2, mlp_post_g=mlp_post_g)
    m = dict(mix_pre_g=m_mix_pre_g, w_in=m_w_in, dw_kernel=m_dw_kernel, dw_bias=m_dw_bias, conv_ln_g=m_conv_ln_g,
             conv_ln_b=m_conv_ln_b, w_conv_out=m_w_conv_out, pool_w=m_pool_w, pool_scale=m_pool_scale,
             w_pool_out=m_w_pool_out, w_o=m_w_o, mix_post_g=m_mix_post_g, mlp_pre_g=m_mlp_pre_g, w_ff1=m_w_ff1,
             w_ff2=m_w_ff2, mlp_post_g=m_mlp_post_g)
    v = dict(mix_pre_g=v_mix_pre_g, w_in=v_w_in, dw_kernel=v_dw_kernel, dw_bias=v_dw_bias, conv_ln_g=v_conv_ln_g,
             conv_ln_b=v_conv_ln_b, w_conv_out=v_w_conv_out, pool_w=v_pool_w, pool_scale=v_pool_scale,
             w_pool_out=v_w_pool_out, w_o=v_w_o, mix_post_g=v_mix_post_g, mlp_pre_g=v_mlp_pre_g, w_ff1=v_w_ff1,
             w_ff2=v_w_ff2, mlp_post_g=v_mlp_post_g)
    seq = x.shape[1]
    loss, grad_x, out = _step(x.reshape(seq, D), loss_target.reshape(seq, D), w, m, v)
    grads, deltas, new_m, new_v = ([out[n][k] for n in WEIGHT_ORDER] for k in range(4))
    return (loss, grad_x.reshape(x.shape), *grads, *deltas, *new_m, *new_v)
```

```python
import collections

import jax
import jax.numpy as jnp
from jax import lax
from jax.experimental import pallas as pl
from jax.experimental.pallas import tpu as pltpu

D = 1024
D_FF = 4 * D
IN_COLS = 5 * D
TAPS = 31
HALO = 32
POOL_WINDOWS = (2, 4, 8, 16)
PG = D // 4
N_DEV = 8
RMS_EPS = 1e-6
LN_EPS = 1e-5
ADAM_LR, ADAM_B1, ADAM_B2, ADAM_EPS, ADAM_WD, ADAM_STEP = 0.001, 0.9, 0.999, 1e-08, 0.01, 10

BF16 = jnp.bfloat16
F32 = jnp.float32
MIB = 1 << 20
MESH = pl.DeviceIdType.MESH


def _params(sem, vmem_mib):
    return pltpu.CompilerParams(dimension_semantics=sem, vmem_limit_bytes=vmem_mib * MIB)


def _dot(a, b):
    return jnp.dot(a, b, preferred_element_type=F32)


def _dot_nt(a, b):
    return lax.dot_general(a, b, (((1,), (1,)), ((), ())), preferred_element_type=F32)


def _dot_tn(a, b):
    return lax.dot_general(a, b, (((0,), (0,)), ((), ())), preferred_element_type=F32)


def _rms_scale(x):
    return lax.rsqrt(jnp.mean(x * x, axis=-1, keepdims=True) + RMS_EPS)


def _rms_bwd(dy, x, g):
    xn = x * _rms_scale(x)
    dn = dy * g
    dx = _rms_scale(x) * (dn - xn * jnp.mean(dn * xn, axis=-1, keepdims=True))
    return dx, jnp.sum(dy * xn, axis=0, keepdims=True)


def _sigmoid(x):
    return jax.nn.sigmoid(x)


def _acc_out(ref, first, value):
    @pl.when(first)
    def _():
        ref[...] = value

    @pl.when(jnp.logical_not(first))
    def _():
        ref[...] += value


def _my_slot():
    return 4 * lax.axis_index("x") + 2 * lax.axis_index("y") + lax.axis_index("c")


def _peer(mask):
    x, y, c = lax.axis_index("x"), lax.axis_index("y"), lax.axis_index("c")
    return (x ^ ((mask >> 2) & 1), y ^ ((mask >> 1) & 1), c ^ (mask & 1))


def _cols(width):
    return lambda ref, slot: ref.at[:, pl.ds(pl.multiple_of(slot * width, 128), width)]


def _rows(height):
    return lambda ref, slot: ref.at[pl.ds(pl.multiple_of(slot * height, 8), height), :]


def _lead(ref, slot):
    return ref.at[slot]


def _pool_rows(ref, slot):
    return ref.at[:, pl.ds(pl.multiple_of(slot * (PG // N_DEV), 8), PG // N_DEV), :]


SHARDED = (
    ("w_in", (D, IN_COLS), (D, IN_COLS // N_DEV), _cols(IN_COLS // N_DEV)),
    ("w_ff1", (D, D_FF), (D, D_FF // N_DEV), _cols(D_FF // N_DEV)),
    ("w_ff2", (D_FF, D), (D_FF // N_DEV, D), _rows(D_FF // N_DEV)),
    ("w_conv_out", (D, D), (D // N_DEV, D), _rows(D // N_DEV)),
    ("w_pool_out", (D, D), (D // N_DEV, D), _rows(D // N_DEV)),
    ("w_o", (D, D), (D // N_DEV, D), _rows(D // N_DEV)),
    ("pool_w", (4, PG, PG), (4, PG // N_DEV, PG), _pool_rows),
    ("dw_kernel", (N_DEV, TAPS, D // N_DEV), (TAPS, D // N_DEV), _lead),
)
N_SHARDED = len(SHARDED)


SHARD_AT = {name: at for name, _, _, at in SHARDED}
HBM = pl.BlockSpec(memory_space=pltpu.HBM)
SEM = pl.BlockSpec(memory_space=pltpu.SEMAPHORE)
ANY = pl.BlockSpec(memory_space=pl.ANY)
EFFECT = pltpu.SideEffectType.DATAFLOW_SIDE_EFFECTING


def _in_hbm(a):
    return pltpu.with_memory_space_constraint(a, pltpu.HBM)


def _pinned(*arrays):
    return [_in_hbm(a) for a in arrays]


def _stage_shards(shards):
    dtypes = [F32 if name == "dw_kernel" else BF16 for name, *_ in SHARDED]

    def body(*refs):
        ins = refs[:N_SHARDED]
        stage = refs[N_SHARDED:2 * N_SHARDED]
        fulls = refs[2 * N_SHARDED:3 * N_SHARDED]
        sems = refs[3 * N_SHARDED]
        me = _my_slot()
        copies = []
        for a, (_, _, _, at) in enumerate(SHARDED):
            stage[a][...] = ins[a][...].astype(dtypes[a])
            cp = pltpu.make_async_copy(stage[a], at(fulls[a], me), sems.at[a])
            cp.start()
            copies.append(cp)
        for cp in copies:
            cp.wait()

    outs = pl.pallas_call(
        body, name="stage_shards",
        out_shape=[jax.ShapeDtypeStruct(shard, dt) for (_, _, shard, _), dt in zip(SHARDED, dtypes)]
        + [jax.ShapeDtypeStruct(full, dt) for (_, full, _, _), dt in zip(SHARDED, dtypes)],
        in_specs=[pl.BlockSpec(memory_space=pltpu.VMEM)] * N_SHARDED,
        out_specs=[pl.BlockSpec(memory_space=pltpu.VMEM)] * N_SHARDED + [pl.BlockSpec(memory_space=pl.ANY)] * N_SHARDED,
        scratch_shapes=[pltpu.SemaphoreType.DMA((N_SHARDED,))],
        compiler_params=pltpu.CompilerParams(vmem_limit_bytes=40 * MIB),
    )(*shards)
    return outs[:N_SHARDED], outs[N_SHARDED:]


Leg = collections.namedtuple("Leg", "routes src_of dst_of src_is_land")


def _sem_index(k, m):
    return k * (N_DEV - 1) + m - 1


def _exchange_start(name, groups, legs):
    sizes = [len(g[0]) for g in groups]
    names = [nm for g in groups for nm in g[0]]
    srcs = [s for g in groups if g[1] is not None for s in g[1]]
    lands = [l for g in groups for l in g[2]]
    n_src, n, n_g, n_l = len(srcs), len(names), len(groups), len(legs)

    def body(*refs):
        src_refs, land_refs = list(refs[:n_src]), refs[n_src:n_src + n]
        sems = refs[n_src + n:n_src + n + 2 * n_g * n_l]
        token = refs[-1]
        me = _my_slot()
        first = 0
        for g, size in enumerate(sizes):
            own_src = [src_refs.pop(0) for _ in range(size)] if groups[g][1] is not None else None
            for li, leg in enumerate(legs):
                send, recv = sems[2 * (g * n_l + li)], sems[2 * (g * n_l + li) + 1]
                for m, via in leg.routes:
                    for k in range(size):
                        land = land_refs[first + k]
                        src = land if leg.src_is_land else own_src[k]
                        pltpu.make_async_remote_copy(
                            src_ref=leg.src_of(names[first + k], src, me, m), dst_ref=leg.dst_of(names[first + k], land, me, m),
                            send_sem=send.at[_sem_index(k, m)], recv_sem=recv.at[_sem_index(k, m)],
                            device_id=_peer(via), device_id_type=MESH).start()
            first += size
        token[...] = jnp.zeros_like(token)

    sem_shapes = [pltpu.SemaphoreType.DMA((size * (N_DEV - 1),)) for size in sizes for _ in range(2 * n_l)]
    n_sem = len(sem_shapes)
    outs = pl.pallas_call(
        body, name=name,
        out_shape=sem_shapes + [pltpu.HBM(a.shape, a.dtype) for a in srcs + lands] + [jax.ShapeDtypeStruct((8, 128), F32)],
        in_specs=[HBM] * (n_src + n),
        out_specs=[SEM] * n_sem + [HBM] * (n_src + n) + [pl.BlockSpec(memory_space=pltpu.VMEM)],
        input_output_aliases={k: n_sem + k for k in range(n_src + n)},
        compiler_params=pltpu.CompilerParams(has_side_effects=EFFECT),
    )(*[_in_hbm(a) for a in srcs + lands])
    sems, thru, token = outs[:n_sem], list(outs[n_sem:-1]), outs[-1]
    src_thru, land_thru = thru[:n_src], thru[n_src:]
    handles, first = [], 0
    for g, size in enumerate(sizes):
        pairs = [(sems[2 * (g * n_l + li)], sems[2 * (g * n_l + li) + 1]) for li in range(n_l)]
        mine = [src_thru.pop(0) for _ in range(size)] if groups[g][1] is not None else None
        handles.append((groups[g][0], pairs, mine, land_thru[first:first + size]))
        first += size
    return handles, token


def _exchange_wait(name, names, srcs, lands, waits, after):
    n = len(names)
    n_src = n if srcs is not None else 0

    def body(*refs):
        src_refs, land_refs = refs[:n_src], refs[n_src:n_src + n]
        sems = refs[n_src + n:n_src + n + 2 * len(waits)]
        me = _my_slot()
        for wi, (_, leg) in enumerate(waits):
            for m, via in leg.routes:
                for k in range(n):
                    src = land_refs[k] if leg.src_is_land else src_refs[k]
                    cp = pltpu.make_async_remote_copy(
                        src_ref=leg.src_of(names[k], src, me, m), dst_ref=leg.dst_of(names[k], land_refs[k], me ^ via, m),
                        send_sem=sems[2 * wi].at[_sem_index(k, m)], recv_sem=sems[2 * wi + 1].at[_sem_index(k, m)],
                        device_id=_peer(via), device_id_type=MESH)
                    cp.wait_send()
                    cp.wait_recv()

    arrays = (list(srcs) if srcs is not None else []) + list(lands)
    outs = pl.pallas_call(
        body, name=name,
        out_shape=[pltpu.HBM(a.shape, a.dtype) for a in arrays],
        in_specs=[HBM] * len(arrays) + [SEM] * (2 * len(waits)) + [pl.BlockSpec(memory_space=pl.ANY)],
        out_specs=[HBM] * len(arrays),
        input_output_aliases={k: k for k in range(len(arrays))},
        compiler_params=pltpu.CompilerParams(has_side_effects=EFFECT),
    )(*arrays, *[s for pair, _ in waits for s in pair], after)
    return (outs[:n_src] if srcs is not None else None), outs[n_src:]


def _shard_slot(name, ref, slot):
    return SHARD_AT[name](ref, slot)


GATHER_ICI = Leg(((2, 2), (4, 4), (6, 6)), lambda name, ref, me, m: ref,
                 lambda name, ref, sender, m: _shard_slot(name, ref, sender), False)
GATHER_D2D = Leg(((1, 1),), GATHER_ICI.src_of, GATHER_ICI.dst_of, False)
GATHER_FORWARD = Leg(((2, 1), (4, 1), (6, 1)), lambda name, ref, me, m: _shard_slot(name, ref, me ^ m),
                     lambda name, ref, sender, m: _shard_slot(name, ref, sender ^ m), True)
SCATTER = Leg(tuple((m, m) for m in range(1, N_DEV)), lambda name, ref, me, m: _shard_slot(name, ref, me ^ m),
              lambda name, ref, sender, m: ref.at[m - 1], False)


def _vec_exchange(vectors, after):
    n = len(vectors)

    def body(*refs):
        vec, vec_out = refs[n + 2], refs[n + 1]
        send_sems, recv_sems, local_sem = refs[n + 3:]
        for k in range(n):
            vec[k:k + 1, :] = refs[k][...]
        me = _my_slot()
        local = pltpu.make_async_copy(vec, vec_out.at[me], local_sem)
        local.start()
        sends = []
        for mask in range(1, N_DEV):
            cp = pltpu.make_async_remote_copy(
                src_ref=vec, dst_ref=vec_out.at[me], send_sem=send_sems.at[mask - 1],
                recv_sem=recv_sems.at[mask - 1], device_id=_peer(mask), device_id_type=MESH)
            cp.start()
            sends.append(cp)
        for mask in range(1, N_DEV):
            pltpu.make_async_remote_copy(
                src_ref=vec, dst_ref=vec_out.at[me ^ mask], send_sem=send_sems.at[mask - 1],
                recv_sem=recv_sems.at[mask - 1], device_id=_peer(mask), device_id_type=MESH).wait_recv()
        for cp in sends:
            cp.wait_send()
        local.wait()

    return pl.pallas_call(
        body, name="vec_exchange",
        out_shape=jax.ShapeDtypeStruct((N_DEV, n, D), F32),
        in_specs=[pl.BlockSpec(memory_space=pltpu.VMEM)] * n + [ANY],
        out_specs=pl.BlockSpec(memory_space=pl.ANY),
        scratch_shapes=[pltpu.VMEM((n, D), F32), pltpu.SemaphoreType.DMA((N_DEV - 1,)),
                        pltpu.SemaphoreType.DMA((N_DEV - 1,)), pltpu.SemaphoreType.DMA],
    )(*vectors, after)


def _adamw_update(g, w_ref, m_ref, v_ref, g_out, d_out, m_out, v_out):
    m_new = ADAM_B1 * m_ref[...] + (1.0 - ADAM_B1) * g
    v_new = ADAM_B2 * v_ref[...] + (1.0 - ADAM_B2) * (g * g)
    m_hat = m_new / (1.0 - ADAM_B1 ** ADAM_STEP)
    v_hat = v_new / (1.0 - ADAM_B2 ** ADAM_STEP)
    g_out[...] = g
    d_out[...] = -ADAM_LR * (m_hat / (jnp.sqrt(v_hat) + ADAM_EPS) + ADAM_WD * w_ref[...])
    m_out[...] = m_new
    v_out[...] = v_new


ADAMW_ROWS = 256


def _adamw(name, own, parts, w, m, v, slot):
    shard = w.shape
    if name in ("w_in", "w_ff1"):
        tr = ADAMW_ROWS
        grid = (shard[0] // tr,)
        own_spec = pl.BlockSpec((tr, shard[1]), lambda i, s: (i, s[0]))
        blk = pl.BlockSpec((tr, shard[1]), lambda i, s: (i, 0))
        parts_spec = pl.BlockSpec((N_DEV - 1, tr, shard[1]), lambda i, s: (0, i, 0))
    elif name == "pool_w":
        grid = (shard[0],)
        own_spec = pl.BlockSpec((None,) + shard[1:], lambda g, s: (g, s[0], 0))
        blk = pl.BlockSpec((None,) + shard[1:], lambda g, s: (g, 0, 0))
        parts_spec = pl.BlockSpec((N_DEV - 1, None) + shard[1:], lambda g, s: (0, g, 0, 0))
    elif name == "dw_kernel":
        grid = (1,)
        own_spec = pl.BlockSpec((None,) + shard, lambda i, s: (s[0], 0, 0))
        blk = pl.BlockSpec(shard, lambda i, s: (0, 0))
        parts_spec = pl.BlockSpec((N_DEV - 1,) + shard, lambda i, s: (0, 0, 0))
    else:
        tr = min(ADAMW_ROWS, shard[0])
        grid = (shard[0] // tr,)
        own_spec = pl.BlockSpec((tr, shard[1]), lambda i, s: (s[0] * grid[0] + i, 0))
        blk = pl.BlockSpec((tr, shard[1]), lambda i, s: (i, 0))
        parts_spec = pl.BlockSpec((N_DEV - 1, tr, shard[1]), lambda i, s: (0, i, 0))

    def body(slot_ref, own_ref, p_ref, w_ref, m_ref, v_ref, g_out, d_out, m_out, v_out):
        g = own_ref[...].astype(F32)
        for k in range(N_DEV - 1):
            g = g + p_ref[k].astype(F32)
        _adamw_update(g, w_ref, m_ref, v_ref, g_out, d_out, m_out, v_out)

    return pl.pallas_call(
        body, name="adamw_" + name,
        grid_spec=pltpu.PrefetchScalarGridSpec(
            num_scalar_prefetch=1, grid=grid, in_specs=[own_spec, parts_spec, blk, blk, blk], out_specs=[blk] * 4),
        out_shape=[jax.ShapeDtypeStruct(shard, F32)] * 4,
        compiler_params=_params(("arbitrary",), 32),
    )(slot, *_pinned(own, parts, w, m, v))


def _adamw_vectors(parts, w, m, v):
    n_vec = w.shape[0]

    def body(p_ref, w_ref, m_ref, v_ref, g_out, d_out, m_out, v_out, loss_out):
        total = p_ref[0]
        for s in range(1, N_DEV):
            total = total + p_ref[s]
        _adamw_update(total[0:n_vec], w_ref, m_ref, v_ref, g_out, d_out, m_out, v_out)
        loss_out[...] = total[n_vec:n_vec + 1] * (0.5 / D)

    return pl.pallas_call(
        body, name="adamw_vectors",
        out_shape=[jax.ShapeDtypeStruct(w.shape, F32)] * 4 + [jax.ShapeDtypeStruct((1, D), F32)],
    )(parts, w, m, v)


def _in_proj_fwd(x, g_pre, w_in, token):
    seq = x.shape[0]
    tm, tn = 1024, IN_COLS // 4

    def body(x_ref, g_ref, w_ref, _token, proj_ref, u_ref):
        @pl.when(pl.program_id(1) == 0)
        def _():
            xf = x_ref[...]
            u_ref[...] = (xf * _rms_scale(xf) * g_ref[...]).astype(BF16)
        proj_ref[...] = _dot(u_ref[...], w_ref[...]).astype(BF16)

    return pl.pallas_call(
        body, name="in_proj_fwd", grid=(seq // tm, IN_COLS // tn),
        in_specs=[pl.BlockSpec((tm, D), lambda i, j: (i, 0)), pl.BlockSpec((1, D), lambda i, j: (0, 0)),
                  pl.BlockSpec((D, tn), lambda i, j: (0, j)), ANY],
        out_specs=[pl.BlockSpec((tm, tn), lambda i, j: (i, j)), pl.BlockSpec((tm, D), lambda i, j: (i, 0))],
        out_shape=[jax.ShapeDtypeStruct((seq, IN_COLS), BF16), jax.ShapeDtypeStruct((seq, D), BF16)],
        compiler_params=_params(("arbitrary", "arbitrary"), 40),
    )(*_pinned(x, g_pre, w_in, token))


CONV_TM = 256
CONV_RS = 128


def _shifts():
    return [(b, [(a, 8 * a + b) for a in range(4) if 8 * a + b < TAPS]) for b in range(8)]


def _taps_looking_back(buf, row0, lanes, weight):
    acc = None
    for b, group in _shifts():
        part = None
        for a, s in group:
            term = weight(TAPS - 1 - s) * buf[pl.ds(row0 - 8 - 8 * a, CONV_RS + 8), lanes]
            part = term if part is None else part + term
        if b:
            part = pltpu.roll(part, b, 0)
        acc = part[8:, :] if acc is None else acc + part[8:, :]
    return acc


def _taps_looking_ahead(buf, row0, lanes, weight):
    acc = None
    for b, group in _shifts():
        part = None
        for a, s in group:
            term = weight(TAPS - 1 - s) * buf[pl.ds(row0 + 8 * a, CONV_RS + 8), lanes]
            part = term if part is None else part + term
        if b:
            part = pltpu.roll(part, CONV_RS + 8 - b, 0)
        acc = part[:CONV_RS, :] if acc is None else acc + part[:CONV_RS, :]
    return acc


def _layer_norm_parts(cv):
    mu = jnp.mean(cv, axis=-1, keepdims=True)
    cen = cv - mu
    rstd = lax.rsqrt(jnp.mean(cen * cen, axis=-1, keepdims=True) + LN_EPS)
    return cen * rstd, rstd


def _conv_fwd(proj, dw, dw_bias, ln_g, ln_b, w_conv_out, token):
    seq = proj.shape[0]
    tm = CONV_TM

    def body(a_ref, gate_ref, dw_ref, bias_ref, lg_ref, lb_ref, w_ref, _token, cv_ref, y_ref, ug):
        i = pl.program_id(0)

        @pl.when(i == 0)
        def _():
            ug[0:HALO, :] = jnp.zeros((HALO, D), F32)

        @pl.when(i > 0)
        def _():
            ug[0:HALO, :] = ug[tm:tm + HALO, :]
        ug[HALO:HALO + tm, :] = a_ref[...].astype(F32) * _sigmoid(gate_ref[...].astype(F32))

        def channel_block(cb, carry):
            lanes = pl.ds(pl.multiple_of(cb * 128, 128), 128)
            for r0 in range(0, tm, CONV_RS):
                taps = _taps_looking_back(ug, HALO + r0, lanes, lambda k: dw_ref[cb, k:k + 1, :])
                cv_ref[pl.ds(r0, CONV_RS), lanes] = taps + bias_ref[:, lanes]
            return carry
        lax.fori_loop(0, D // 128, channel_block, 0)

        n, _ = _layer_norm_parts(cv_ref[...])
        ln = n * lg_ref[...] + lb_ref[...]
        y_ref[...] = _dot((ln * _sigmoid(ln)).astype(BF16), w_ref[...]).astype(BF16)

    vec = pl.BlockSpec((1, D), lambda i: (0, 0))
    tile = pl.BlockSpec((tm, D), lambda i: (i, 0))
    return pl.pallas_call(
        body, name="conv_fwd", grid=(seq // tm,),
        in_specs=[pl.BlockSpec((tm, D), lambda i: (i, 0)), pl.BlockSpec((tm, D), lambda i: (i, 1)),
                  pl.BlockSpec((N_DEV, TAPS, 128), lambda i: (0, 0, 0)), vec, vec, vec,
                  pl.BlockSpec((D, D), lambda i: (0, 0)), ANY],
        out_specs=[tile, tile],
        out_shape=[jax.ShapeDtypeStruct((seq, D), F32), jax.ShapeDtypeStruct((seq, D), BF16)],
        scratch_shapes=[pltpu.VMEM((HALO + tm, D), F32)],
        compiler_params=_params(("arbitrary",), 32),
    )(*_pinned(proj, proj, dw, dw_bias, ln_g, ln_b, w_conv_out, token))


def _pool_counts(tile_index, tm, window):
    t = tile_index * tm + lax.broadcasted_iota(jnp.int32, (tm, 1), 0)
    return 1.0 / jnp.minimum(t + 1, window).astype(F32)


def _pool_merge_fwd(proj, y_conv, x, pool_w, pool_scale, w_pool_out, w_o, g_post, token):
    seq = proj.shape[0]
    tm = CONV_TM

    def body(p_ref, gc_ref, gp_ref, yc_ref, x_ref, pw_ref, ps_ref, wpo_ref, wo_ref, g_ref, _token,
             z_ref, zl_ref, yp_ref, mg_ref, o_ref, h1_ref, pbuf):
        i = pl.program_id(0)

        @pl.when(i == 0)
        def _():
            pbuf[0:HALO, :] = jnp.zeros((HALO, D), F32)

        @pl.when(i > 0)
        def _():
            pbuf[0:HALO, :] = pbuf[tm:tm + HALO, :]
        pbuf[HALO:HALO + tm, :] = p_ref[...].astype(F32)

        for g, window in enumerate(POOL_WINDOWS):
            lanes = pl.ds(g * PG, PG)
            acc = pbuf[pl.ds(HALO, tm), lanes]
            for j in range(1, window):
                acc = acc + pbuf[pl.ds(HALO - j, tm), lanes]
            zg = acc * _pool_counts(i, tm, window) - pbuf[pl.ds(HALO, tm), lanes]
            z_ref[:, lanes] = zg.astype(BF16)
            zl_ref[:, lanes] = _dot(zg.astype(BF16), pw_ref[g])
        zl = zl_ref[...]
        y_pool = _dot((zl * ps_ref[...]).astype(BF16), wpo_ref[...])
        yp_ref[...] = y_pool.astype(BF16)
        merged = (_sigmoid(gc_ref[...].astype(F32)) * yc_ref[...].astype(F32)
                  + _sigmoid(gp_ref[...].astype(F32)) * y_pool).astype(BF16)
        mg_ref[...] = merged
        o = _dot(merged, wo_ref[...])
        o_ref[...] = o
        h1_ref[...] = x_ref[...] + o * _rms_scale(o) * g_ref[...]

    vec = pl.BlockSpec((1, D), lambda i: (0, 0))
    tile = pl.BlockSpec((tm, D), lambda i: (i, 0))
    mat = pl.BlockSpec((D, D), lambda i: (0, 0))
    return pl.pallas_call(
        body, name="pool_merge_fwd", grid=(seq // tm,),
        in_specs=[pl.BlockSpec((tm, D), lambda i: (i, 2)), pl.BlockSpec((tm, D), lambda i: (i, 3)),
                  pl.BlockSpec((tm, D), lambda i: (i, 4)), tile, tile,
                  pl.BlockSpec((4, PG, PG), lambda i: (0, 0, 0)), vec, mat, mat, vec, ANY],
        out_specs=[tile] * 6,
        out_shape=[jax.ShapeDtypeStruct((seq, D), dt) for dt in (BF16, F32, BF16, BF16, F32, F32)],
        scratch_shapes=[pltpu.VMEM((HALO + tm, D), F32)],
        compiler_params=_params(("arbitrary",), 48),
    )(*_pinned(proj, proj, proj, y_conv, x, pool_w, pool_scale, w_pool_out, w_o, g_post, token))


def _mlp_fwd(h1, g_pre, w_ff1, w_ff2, g_post, target):
    seq = h1.shape[0]
    tm, tf = 1024, D_FF // N_DEV
    n_f = D_FF // tf

    def body(h1_ref, gpre_ref, w1_ref, w2_ref, gpost_ref, tgt_ref, v_ref, dm_ref, dh2_ref, sse_ref, ggrad_ref, macc):
        i, j = pl.program_id(0), pl.program_id(1)

        @pl.when(j == 0)
        def _():
            h = h1_ref[...]
            v_ref[...] = (h * _rms_scale(h) * gpre_ref[...]).astype(BF16)
        f = jnp.maximum(_dot(v_ref[...], w1_ref[...]), 0.0)
        part = _dot((f * f).astype(BF16), w2_ref[...])

        @pl.when(j == 0)
        def _():
            macc[...] = part

        @pl.when(j > 0)
        def _():
            macc[...] += part

        @pl.when(j == n_f - 1)
        def _():
            mo = macc[...]
            err = h1_ref[...] + mo * _rms_scale(mo) * gpost_ref[...] - tgt_ref[...]
            dh2 = err * (1.0 / D)
            dh2_ref[...] = dh2
            dm, ggrad = _rms_bwd(dh2, mo, gpost_ref[...])
            dm_ref[...] = dm.astype(BF16)
            _acc_out(ggrad_ref, i == 0, ggrad)
            _acc_out(sse_ref, i == 0, jnp.sum(jnp.sum(err * err, axis=1, keepdims=True), axis=0, keepdims=True))

    vec = pl.BlockSpec((1, D), lambda i, j: (0, 0))
    tile = pl.BlockSpec((tm, D), lambda i, j: (i, 0))
    return pl.pallas_call(
        body, name="mlp_fwd", grid=(seq // tm, n_f),
        in_specs=[tile, vec, pl.BlockSpec((D, tf), lambda i, j: (0, j)), pl.BlockSpec((tf, D), lambda i, j: (j, 0)), vec, tile],
        out_specs=[tile, tile, tile, pl.BlockSpec((1, 1), lambda i, j: (0, 0)), vec],
        out_shape=[jax.ShapeDtypeStruct((seq, D), BF16), jax.ShapeDtypeStruct((seq, D), BF16),
                   jax.ShapeDtypeStruct((seq, D), F32), jax.ShapeDtypeStruct((1, 1), F32),
                   jax.ShapeDtypeStruct((1, D), F32)],
        scratch_shapes=[pltpu.VMEM((tm, D), F32)],
        compiler_params=_params(("arbitrary", "arbitrary"), 56),
    )(*_pinned(h1, g_pre, w_ff1, w_ff2, g_post, target))


def _mlp_bwd(v, dm, w_ff1, w_ff2):
    seq = v.shape[0]
    tm, tf = 1024, D_FF // N_DEV
    n_t = seq // tm

    def body(v_ref, dm_ref, w1_ref, w2_ref, dv_hbm, g1_ref, g2_ref, dv_acc, g1_acc, g2_acc, sem):
        j, i = pl.program_id(0), pl.program_id(1)
        vt, dmt = v_ref[...], dm_ref[...]
        f = jnp.maximum(_dot(vt, w1_ref[...]), 0.0)
        df = (_dot_nt(dmt, w2_ref[...]) * (2.0 * f)).astype(BF16)
        rows = pl.ds(pl.multiple_of(i * tm, tm), tm)
        dv_part = _dot_nt(df, w1_ref[...])

        @pl.when(j == 0)
        def _():
            dv_acc[rows, :] = dv_part

        @pl.when(j > 0)
        def _():
            dv_acc[rows, :] += dv_part
        g1_part = _dot_tn(vt, df)
        g2_part = _dot_tn((f * f).astype(BF16), dmt)

        @pl.when(i == 0)
        def _():
            g1_acc[...] = g1_part
            g2_acc[...] = g2_part

        @pl.when(i > 0)
        def _():
            g1_acc[...] += g1_part
            g2_acc[...] += g2_part

        @pl.when(i == n_t - 1)
        def _():
            g1_ref[...] = g1_acc[...].astype(BF16)
            g2_ref[...] = g2_acc[...].astype(BF16)

        @pl.when(jnp.logical_and(i == n_t - 1, j == D_FF // tf - 1))
        def _():
            cp = pltpu.make_async_copy(dv_acc, dv_hbm, sem)
            cp.start()
            cp.wait()

    tile = pl.BlockSpec((tm, D), lambda j, i: (i, 0))
    return pl.pallas_call(
        body, name="mlp_bwd", grid=(D_FF // tf, n_t),
        in_specs=[tile, tile, pl.BlockSpec((D, tf), lambda j, i: (0, j)), pl.BlockSpec((tf, D), lambda j, i: (j, 0))],
        out_specs=[pl.BlockSpec(memory_space=pl.ANY), pl.BlockSpec((D, tf), lambda j, i: (0, j)),
                   pl.BlockSpec((tf, D), lambda j, i: (j, 0))],
        out_shape=[jax.ShapeDtypeStruct((seq, D), F32), jax.ShapeDtypeStruct((D, D_FF), BF16),
                   jax.ShapeDtypeStruct((D_FF, D), BF16)],
        scratch_shapes=[pltpu.VMEM((seq, D), F32), pltpu.VMEM((D, tf), F32), pltpu.VMEM((tf, D), F32),
                        pltpu.SemaphoreType.DMA],
        compiler_params=_params(("arbitrary", "arbitrary"), 52),
    )(*_pinned(v, dm, w_ff1, w_ff2))


def _merge_bwd(dh2, dv, h1, g_mlp_pre, o, g_mix_post, w_o, merged, proj, y_conv, y_pool, token):
    seq = dh2.shape[0]
    tm = 256
    n_t = seq // tm

    def body(dh2_ref, dv_ref, h1_ref, gpre_ref, o_ref, gpost_ref, wo_ref, mg_ref, gc_ref, gp_ref, yc_ref, yp_ref, _token,
             dh1_ref, dyc_ref, dyp_ref, dg_ref, gwo_ref, ggpre_ref, ggpost_ref, gwo_acc):
        i = pl.program_id(0)
        dnorm, ggpre = _rms_bwd(dv_ref[...], h1_ref[...], gpre_ref[...])
        dh1 = dh2_ref[...] + dnorm
        dh1_ref[...] = dh1
        do, ggpost = _rms_bwd(dh1, o_ref[...], gpost_ref[...])
        do = do.astype(BF16)
        _acc_out(ggpre_ref, i == 0, ggpre)
        _acc_out(ggpost_ref, i == 0, ggpost)
        _acc_out(gwo_acc, i == 0, _dot_tn(mg_ref[...], do))
        dmerged = _dot_nt(do, wo_ref[...])
        sc, sp = _sigmoid(gc_ref[...].astype(F32)), _sigmoid(gp_ref[...].astype(F32))
        dyc_ref[...] = (dmerged * sc).astype(BF16)
        dyp_ref[...] = (dmerged * sp).astype(BF16)
        dg_ref[:, 0:D] = (dmerged * yc_ref[...].astype(F32) * (sc * (1.0 - sc))).astype(BF16)
        dg_ref[:, D:2 * D] = (dmerged * yp_ref[...].astype(F32) * (sp * (1.0 - sp))).astype(BF16)

        @pl.when(i == n_t - 1)
        def _():
            gwo_ref[...] = gwo_acc[...].astype(BF16)

    vec = pl.BlockSpec((1, D), lambda i: (0, 0))
    tile = pl.BlockSpec((tm, D), lambda i: (i, 0))
    mat = pl.BlockSpec((D, D), lambda i: (0, 0))
    return pl.pallas_call(
        body, name="merge_bwd", grid=(n_t,),
        in_specs=[tile, tile, tile, vec, tile, vec, mat, tile,
                  pl.BlockSpec((tm, D), lambda i: (i, 3)), pl.BlockSpec((tm, D), lambda i: (i, 4)), tile, tile, ANY],
        out_specs=[tile, tile, tile, pl.BlockSpec((tm, 2 * D), lambda i: (i, 0)), mat, vec, vec],
        out_shape=[jax.ShapeDtypeStruct((seq, D), F32), jax.ShapeDtypeStruct((seq, D), BF16),
                   jax.ShapeDtypeStruct((seq, D), BF16), jax.ShapeDtypeStruct((seq, 2 * D), BF16),
                   jax.ShapeDtypeStruct((D, D), BF16), jax.ShapeDtypeStruct((1, D), F32),
                   jax.ShapeDtypeStruct((1, D), F32)],
        scratch_shapes=[pltpu.VMEM((D, D), F32)],
        compiler_params=_params(("arbitrary",), 48),
    )(*_pinned(dh2, dv, h1, g_mlp_pre, o, g_mix_post, w_o, merged, proj, proj, y_conv, y_pool, token))


def _pool_bwd(dy_pool, zl, z, pool_w, pool_scale, w_pool_out):
    seq = dy_pool.shape[0]
    tm = CONV_TM
    n_t = seq // tm

    def body(dy_ref, zl_ref, z_ref, pw_ref, ps_ref, wpo_ref, dp_ref, gwpo_ref, gpw_ref, gps_ref, qbuf, gwpo_acc, gpw_acc):
        i = pl.program_id(0)
        tile_index = n_t - 1 - i
        first = i == 0
        dy = dy_ref[...]
        zl = zl_ref[...]
        dzs = _dot_nt(dy, wpo_ref[...])
        _acc_out(gwpo_acc, first, _dot_tn((zl * ps_ref[...]).astype(BF16), dy))
        _acc_out(gps_ref, first, jnp.sum(dzs * zl, axis=0, keepdims=True))
        dzl = (dzs * ps_ref[...]).astype(BF16)

        @pl.when(first)
        def _():
            qbuf[tm:tm + HALO, :] = jnp.zeros((HALO, D), F32)

        @pl.when(jnp.logical_not(first))
        def _():
            qbuf[tm:tm + HALO, :] = qbuf[0:HALO, :]

        dzs_list = []
        for g, window in enumerate(POOL_WINDOWS):
            lanes = pl.ds(g * PG, PG)
            dzl_g = dzl[:, g * PG:(g + 1) * PG]
            dz = _dot_nt(dzl_g, pw_ref[g])
            _acc_out(gpw_acc.at[g], first, _dot_tn(z_ref[:, lanes], dzl_g))
            qbuf[pl.ds(0, tm), lanes] = dz * _pool_counts(tile_index, tm, window)
            dzs_list.append(dz)
        for g, window in enumerate(POOL_WINDOWS):
            lanes = pl.ds(g * PG, PG)
            acc = qbuf[pl.ds(0, tm), lanes]
            for j in range(1, window):
                acc = acc + qbuf[pl.ds(j, tm), lanes]
            dp_ref[:, lanes] = (acc - dzs_list[g]).astype(BF16)

        @pl.when(i == n_t - 1)
        def _():
            gwpo_ref[...] = gwpo_acc[...].astype(BF16)
            gpw_ref[...] = gpw_acc[...].astype(BF16)

    vec = pl.BlockSpec((1, D), lambda i: (0, 0))
    tile = pl.BlockSpec((tm, D), lambda i: (n_t - 1 - i, 0))
    mat = pl.BlockSpec((D, D), lambda i: (0, 0))
    pw = pl.BlockSpec((4, PG, PG), lambda i: (0, 0, 0))
    return pl.pallas_call(
        body, name="pool_bwd", grid=(n_t,),
        in_specs=[tile, tile, tile, pw, vec, mat],
        out_specs=[tile, mat, pw, vec],
        out_shape=[jax.ShapeDtypeStruct((seq, D), BF16), jax.ShapeDtypeStruct((D, D), BF16),
                   jax.ShapeDtypeStruct((4, PG, PG), BF16), jax.ShapeDtypeStruct((1, D), F32)],
        scratch_shapes=[pltpu.VMEM((tm + HALO, D), F32), pltpu.VMEM((D, D), F32), pltpu.VMEM((4, PG, PG), F32)],
        compiler_params=_params(("arbitrary",), 40),
    )(*_pinned(dy_pool, zl, z, pool_w, pool_scale, w_pool_out))


def _conv_bwd(dy_conv, cv, proj, dw, ln_g, ln_b, w_conv_out):
    seq = dy_conv.shape[0]
    tm = CONV_TM
    n_t = seq // tm
    halo_blocks = tm // HALO

    def body(dy_ref, cv_ref, a_ref, gate_ref, ah_ref, gh_ref, dw_ref, lg_ref, lb_ref, w_ref,
             dglu_ref, gw_ref, gdw_ref, gbias_ref, glg_ref, glb_ref, ug, dcv, dug, gw_acc):
        i = pl.program_id(0)
        tile_index = n_t - 1 - i
        first = i == 0
        dy = dy_ref[...]
        n, rstd = _layer_norm_parts(cv_ref[...])
        ln = n * lg_ref[...] + lb_ref[...]
        sg = _sigmoid(ln)
        _acc_out(gw_acc, first, _dot_tn((ln * sg).astype(BF16), dy))
        dln = _dot_nt(dy, w_ref[...]) * (sg * (1.0 + ln * (1.0 - sg)))
        _acc_out(glg_ref, first, jnp.sum(dln * n, axis=0, keepdims=True))
        _acc_out(glb_ref, first, jnp.sum(dln, axis=0, keepdims=True))
        dn = dln * lg_ref[...]
        dcv_tile = rstd * (dn - jnp.mean(dn, axis=-1, keepdims=True) - n * jnp.mean(dn * n, axis=-1, keepdims=True))
        _acc_out(gbias_ref, first, jnp.sum(dcv_tile, axis=0, keepdims=True))

        @pl.when(first)
        def _():
            dcv[tm:tm + HALO, :] = jnp.zeros((HALO, D), F32)

        @pl.when(jnp.logical_not(first))
        def _():
            dcv[tm:tm + HALO, :] = dcv[0:HALO, :]
        dcv[0:tm, :] = dcv_tile

        a, gate = a_ref[...].astype(F32), gate_ref[...].astype(F32)
        sgate = _sigmoid(gate)
        ug[HALO:HALO + tm, :] = a * sgate
        before = jnp.where(tile_index > 0, 1.0, 0.0)
        ug[0:HALO, :] = ah_ref[...].astype(F32) * _sigmoid(gh_ref[...].astype(F32)) * before

        @pl.when(first)
        def _():
            gdw_ref[...] = jnp.zeros((N_DEV, TAPS + 1, 128), F32)

        def channel_block(cb, carry):
            lanes = pl.ds(pl.multiple_of(cb * 128, 128), 128)
            for r0 in range(0, tm, CONV_RS):
                dug[pl.ds(r0, CONV_RS), lanes] = _taps_looking_ahead(dcv, r0, lanes, lambda k: dw_ref[cb, k:k + 1, :])
            for b, group in _shifts():
                sums = [jnp.zeros((8, 128), F32) for _ in group]
                for r0 in range(0, tm, CONV_RS):
                    window = ug[pl.ds(r0, CONV_RS + HALO), lanes]
                    if b:
                        window = pltpu.roll(window, b, 0)
                    d = dcv[pl.ds(r0, CONV_RS), lanes]
                    for n_a, (a, s) in enumerate(group):
                        prod = d * window[HALO - 8 * a:HALO - 8 * a + CONV_RS, :]
                        sums[n_a] = sums[n_a] + jnp.sum(prod.reshape(CONV_RS // 8, 8, 128), axis=0)
                for n_a, (a, s) in enumerate(group):
                    k = TAPS - 1 - s
                    gdw_ref[cb, k:k + 1, :] += jnp.sum(sums[n_a], axis=0, keepdims=True)
            return carry
        lax.fori_loop(0, D // 128, channel_block, 0)

        d_ug = dug[...]
        dglu_ref[:, 0:D] = (d_ug * sgate).astype(BF16)
        dglu_ref[:, D:2 * D] = (d_ug * a * (sgate * (1.0 - sgate))).astype(BF16)

        @pl.when(i == n_t - 1)
        def _():
            gw_ref[...] = gw_acc[...].astype(BF16)

    def halo_index(col):
        return lambda i: (jnp.maximum((n_t - 1 - i) * halo_blocks - 1, 0), col)

    vec = pl.BlockSpec((1, D), lambda i: (0, 0))
    tile = pl.BlockSpec((tm, D), lambda i: (n_t - 1 - i, 0))
    mat = pl.BlockSpec((D, D), lambda i: (0, 0))
    dwspec = pl.BlockSpec((N_DEV, TAPS, 128), lambda i: (0, 0, 0))
    return pl.pallas_call(
        body, name="conv_bwd", grid=(n_t,),
        in_specs=[tile, tile, pl.BlockSpec((tm, D), lambda i: (n_t - 1 - i, 0)), pl.BlockSpec((tm, D), lambda i: (n_t - 1 - i, 1)),
                  pl.BlockSpec((HALO, D), halo_index(0)), pl.BlockSpec((HALO, D), halo_index(1)), dwspec, vec, vec, mat],
        out_specs=[pl.BlockSpec((tm, 2 * D), lambda i: (n_t - 1 - i, 0)), mat,
                   pl.BlockSpec((N_DEV, TAPS + 1, 128), lambda i: (0, 0, 0)), vec, vec, vec],
        out_shape=[jax.ShapeDtypeStruct((seq, 2 * D), BF16), jax.ShapeDtypeStruct((D, D), BF16),
                   jax.ShapeDtypeStruct((N_DEV, TAPS + 1, 128), F32), jax.ShapeDtypeStruct((1, D), F32),
                   jax.ShapeDtypeStruct((1, D), F32), jax.ShapeDtypeStruct((1, D), F32)],
        scratch_shapes=[pltpu.VMEM((HALO + tm, D), F32), pltpu.VMEM((tm + HALO, D), F32), pltpu.VMEM((tm, D), F32),
                        pltpu.VMEM((D, D), F32)],
        compiler_params=_params(("arbitrary",), 48),
    )(*_pinned(dy_conv, cv, proj, proj, proj, proj, dw, ln_g, ln_b, w_conv_out))


def _in_proj_bwd_x(d_glu, dp, dgates, w_in, x, g_pre, dh1, token):
    seq = x.shape[0]
    tm = 256

    def body(dglu_ref, dp_ref, dg_ref, w_ref, x_ref, g_ref, dh1_ref, _token, dx_ref, gg_ref):
        du = _dot_nt(dglu_ref[...], w_ref[:, 0:2 * D])
        du += _dot_nt(dp_ref[...], w_ref[:, 2 * D:3 * D])
        du += _dot_nt(dg_ref[...], w_ref[:, 3 * D:5 * D])
        dnorm, gg = _rms_bwd(du, x_ref[...], g_ref[...])
        dx_ref[...] = dh1_ref[...] + dnorm
        _acc_out(gg_ref, pl.program_id(0) == 0, gg)

    vec = pl.BlockSpec((1, D), lambda i: (0, 0))
    tile = pl.BlockSpec((tm, D), lambda i: (i, 0))
    wide = pl.BlockSpec((tm, 2 * D), lambda i: (i, 0))
    return pl.pallas_call(
        body, name="in_proj_bwd_x", grid=(seq // tm,),
        in_specs=[wide, tile, wide, pl.BlockSpec((D, IN_COLS), lambda i: (0, 0)), tile, vec, tile, ANY],
        out_specs=[tile, vec],
        out_shape=[jax.ShapeDtypeStruct((seq, D), F32), jax.ShapeDtypeStruct((1, D), F32)],
        compiler_params=_params(("arbitrary",), 48),
    )(*_pinned(d_glu, dp, dgates, w_in, x, g_pre, dh1, token))


def _in_proj_bwd_w(u, d_glu, dp, dgates, token):
    seq = u.shape[0]
    tm = 1024
    n_t = seq // tm

    def body(u_ref, dglu_ref, dp_ref, dg_ref, _token, out_ref, acc):
        b, i = pl.program_id(0), pl.program_id(1)
        ut = u_ref[...]

        def add(d_ref):
            _acc_out(acc, i == 0, _dot_tn(ut, d_ref[...]))

        pl.when(b < 2)(lambda: add(dglu_ref))
        pl.when(b == 2)(lambda: add(dp_ref))
        pl.when(b > 2)(lambda: add(dg_ref))

        @pl.when(i == n_t - 1)
        def _():
            out_ref[...] = acc[...].astype(BF16)

    return pl.pallas_call(
        body, name="in_proj_bwd_w", grid=(IN_COLS // D, n_t),
        in_specs=[pl.BlockSpec((tm, D), lambda b, i: (i, 0)),
                  pl.BlockSpec((tm, D), lambda b, i: (jnp.where(b < 2, i, 0), jnp.minimum(b, 1))),
                  pl.BlockSpec((tm, D), lambda b, i: (jnp.where(b == 2, i, 0), 0)),
                  pl.BlockSpec((tm, D), lambda b, i: (jnp.where(b > 2, i, 0), jnp.maximum(b - 3, 0))), ANY],
        out_specs=pl.BlockSpec((D, D), lambda b, i: (0, b)),
        out_shape=jax.ShapeDtypeStruct((D, IN_COLS), BF16),
        scratch_shapes=[pltpu.VMEM((D, D), F32)],
        compiler_params=_params(("arbitrary", "arbitrary"), 40),
    )(*_pinned(u, d_glu, dp, dgates, token))


VEC_NAMES = ("mix_pre_g", "dw_bias", "conv_ln_g", "conv_ln_b", "pool_scale", "mix_post_g", "mlp_pre_g", "mlp_post_g")
WEIGHT_ORDER = ("mix_pre_g", "w_in", "dw_kernel", "dw_bias", "conv_ln_g", "conv_ln_b", "w_conv_out", "pool_w",
                "pool_scale", "w_pool_out", "w_o", "mix_post_g", "mlp_pre_g", "w_ff1", "w_ff2", "mlp_post_g")


def _step(x, loss_target, w, m, v):
    row = lambda a: a.reshape(1, D)
    names = [s[0] for s in SHARDED]

    stage, seeded = _stage_shards([w[n] for n in names])
    stage, seeded = dict(zip(names, stage)), dict(zip(names, seeded))
    gather_groups = (("w_in",), ("w_conv_out", "dw_kernel"), ("pool_w", "w_pool_out", "w_o"), ("w_ff1", "w_ff2"))
    first_level, token = _exchange_start(
        "gather_start", [(g, [stage[n] for n in g], [seeded[n] for n in g]) for g in gather_groups],
        [GATHER_ICI, GATHER_D2D])
    forwarded, full = {}, {}

    def forward(k, after):
        group, (ici_sems, d2d_sems), staged, landed = first_level[k]
        staged, landed = _exchange_wait("gather_ici_" + group[0], group, staged, landed, [(ici_sems, GATHER_ICI)], after)
        (second,), tok = _exchange_start("gather_forward_" + group[0], [(group, None, landed)], [GATHER_FORWARD])
        forwarded[k] = (staged, second[3], [(d2d_sems, GATHER_D2D), (second[1][0], GATHER_FORWARD)])
        return tok

    def gathered(k, after):
        group = gather_groups[k]
        staged, landed, waits = forwarded[k]
        _, arrays = _exchange_wait("gather_wait_" + group[0], group, staged, landed, waits, after)
        full.update(zip(group, arrays))

    gathered(0, forward(0, token))
    tok = forward(1, full["w_in"])
    proj, u = _in_proj_fwd(x, row(w["mix_pre_g"]), full["w_in"], tok)
    gathered(1, proj)
    tok = forward(2, proj)
    cv, y_conv = _conv_fwd(proj, full["dw_kernel"], row(w["dw_bias"]), row(w["conv_ln_g"]), row(w["conv_ln_b"]),
                           full["w_conv_out"], tok)
    gathered(2, y_conv)
    tok = forward(3, y_conv)
    z, zl, y_pool, merged, o, h1 = _pool_merge_fwd(proj, y_conv, x, full["pool_w"], row(w["pool_scale"]),
                                                   full["w_pool_out"], full["w_o"], row(w["mix_post_g"]), tok)
    gathered(3, h1)
    vv, dm, dh2, sse, g_mlp_post = _mlp_fwd(h1, row(w["mlp_pre_g"]), full["w_ff1"], full["w_ff2"],
                                            row(w["mlp_post_g"]), loss_target)

    shard_of = {name: shard for name, _, shard, _ in SHARDED}

    def scatter_start(tag, group, grads):
        landings = [lax.empty((N_DEV - 1,) + shard_of[n], g.dtype) for n, g in zip(group, grads)]
        (handle,), tok = _exchange_start("scatter_start_" + tag, [(group, grads, landings)], [SCATTER])
        return handle, tok

    dv, g_ff1, g_ff2 = _mlp_bwd(vv, dm, full["w_ff1"], full["w_ff2"])
    h_ff, tok_ff = scatter_start("ff", ("w_ff1", "w_ff2"), [g_ff1, g_ff2])
    dh1, dy_conv, dy_pool, dgates, g_wo, g_mlp_pre, g_mix_post = _merge_bwd(
        dh2, dv, h1, row(w["mlp_pre_g"]), o, row(w["mix_post_g"]), full["w_o"], merged, proj, y_conv, y_pool, tok_ff)
    dp, g_wpo, g_pw, g_pool_scale = _pool_bwd(dy_pool, zl, z, full["pool_w"], row(w["pool_scale"]), full["w_pool_out"])
    d_glu, g_wco, g_dw, g_bias, g_ln_g, g_ln_b = _conv_bwd(dy_conv, cv, proj, full["dw_kernel"], row(w["conv_ln_g"]),
                                                            row(w["conv_ln_b"]), full["w_conv_out"])
    h_mix, tok_mix = scatter_start("mix", ("w_o", "w_pool_out", "pool_w", "w_conv_out", "dw_kernel"),
                                   [g_wo, g_wpo, g_pw, g_wco, g_dw[:, :TAPS, :]])
    g_win = _in_proj_bwd_w(u, d_glu, dp, dgates, tok_mix)
    h_in, tok_in = scatter_start("in", ("w_in",), [g_win])
    grad_x, g_mix_pre = _in_proj_bwd_x(d_glu, dp, dgates, full["w_in"], x, row(w["mix_pre_g"]), dh1, tok_in)

    out = {}
    slot = jnp.reshape(_my_slot(), (1,)).astype(jnp.int32)
    after = grad_x
    for tag, (group, (sems,), grads, landings) in (("ff", h_ff), ("mix", h_mix), ("in", h_in)):
        if tag == "in":
            vec_parts = after = _vec_exchange(
                [g_mix_pre, g_bias, g_ln_g, g_ln_b, g_pool_scale, g_mix_post, g_mlp_pre, g_mlp_post,
                 jnp.broadcast_to(sse, (1, D))], after)
        mine, landed = _exchange_wait("scatter_wait_" + tag, group, grads, landings, [(sems, SCATTER)], after)
        for name, own, parts in zip(group, mine, landed):
            out[name] = _adamw(name, own, parts, w[name], m[name], v[name], slot)
            after = out[name][0]
    stack = lambda d: jnp.stack([d[n] for n in VEC_NAMES], axis=0)
    *res, loss_row = _adamw_vectors(vec_parts, stack(w), stack(m), stack(v))
    for k, name in enumerate(VEC_NAMES):
        out[name] = [r[k] for r in res]
    return loss_row[0, 0], grad_x, out


def kernel(x, mix_pre_g, w_in, dw_kernel, dw_bias, conv_ln_g, conv_ln_b, w_conv_out, pool_w, pool_scale, w_pool_out, w_o, mix_post_g, mlp_pre_g, w_ff1, w_ff2, mlp_post_g, loss_target, m_mix_pre_g, m_w_in, m_dw_kernel, m_dw_bias, m_conv_ln_g, m_conv_ln_b, m_w_conv_out, m_pool_w, m_pool_scale, m_w_pool_out, m_w_o, m_mix_post_g, m_mlp_pre_g, m_w_ff1, m_w_ff2, m_mlp_post_g, v_mix_pre_g, v_w_in, v_dw_kernel, v_dw_bias, v_conv_ln_g, v_conv_ln_b, v_w_conv_out, v_pool_w, v_pool_scale, v_w_pool_out, v_w_o, v_mix_post_g, v_mlp_pre_g, v_w_ff1, v_w_ff2, v_mlp_post_g):
    w = dict(mix_pre_g=mix_pre_g, w_in=w_in, dw_kernel=dw_kernel, dw_bias=dw_bias, conv_ln_g=conv_ln_g, conv_ln_b=conv_ln_b,
             w_conv_out=w_conv_out, pool_w=pool_w, pool_scale=pool_scale, w_pool_out=w_pool_out, w_o=w_o,
             mix_post_g=mix_post_g, mlp_pre_g=mlp_pre_g, w_ff1=w_ff1, w_ff2=w_ff2, mlp_post_g=mlp_post_g)
    m = dict(mix_pre_g=m_mix_pre_g, w_in=m_w_in, dw_kernel=m_dw_kernel, dw_bias=m_dw_bias, conv_ln_g=m_conv_ln_g,
             conv_ln_b=m_conv_ln_b, w_conv_out=m_w_conv_out, pool_w=m_pool_w, pool_scale=m_pool_scale,
             w_pool_out=m_w_pool_out, w_o=m_w_o, mix_post_g=m_mix_post_g, mlp_pre_g=m_mlp_pre_g, w_ff1=m_w_ff1,
             w_ff2=m_w_ff2, mlp_post_g=m_mlp_post_g)
    v = dict(mix_pre_g=v_mix_pre_g, w_in=v_w_in, dw_kernel=v_dw_kernel, dw_bias=v_dw_bias, conv_ln_g=v_conv_ln_g,
             conv_ln_b=v_conv_ln_b, w_conv_out=v_w_conv_out, pool_w=v_pool_w, pool_scale=v_pool_scale,
             w_pool_out=v_w_pool_out, w_o=v_w_o, mix_post_g=v_mix_post_g, mlp_pre_g=v_mlp_pre_g, w_ff1=v_w_ff1,
             w_ff2=v_w_ff2, mlp_post_g=v_mlp_post_g)
    seq = x.shape[1]
    loss, grad_x, out = _step(x.reshape(seq, D), loss_target.reshape(seq, D), w, m, v)
    grads, deltas, new_m, new_v = ([out[n][k] for n in WEIGHT_ORDER] for k in range(4))
    return (loss, grad_x.reshape(x.shape), *grads, *deltas, *new_m, *new_v)
```

```python
import collections

import jax
import jax.numpy as jnp
from jax import lax
from jax.experimental import pallas as pl
from jax.experimental.pallas import tpu as pltpu

D = 1024
D_FF = 4 * D
IN_COLS = 5 * D
TAPS = 31
HALO = 32
POOL_WINDOWS = (2, 4, 8, 16)
PG = D // 4
N_DEV = 8
RMS_EPS = 1e-6
LN_EPS = 1e-5
ADAM_LR, ADAM_B1, ADAM_B2, ADAM_EPS, ADAM_WD, ADAM_STEP = 0.001, 0.9, 0.999, 1e-08, 0.01, 10

BF16 = jnp.bfloat16
F32 = jnp.float32
MIB = 1 << 20
MESH = pl.DeviceIdType.MESH


def _params(sem, vmem_mib):
    return pltpu.CompilerParams(dimension_semantics=sem, vmem_limit_bytes=vmem_mib * MIB)


def _dot(a, b):
    return jnp.dot(a, b, preferred_element_type=F32)


def _dot_nt(a, b):
    return lax.dot_general(a, b, (((1,), (1,)), ((), ())), preferred_element_type=F32)


def _dot_tn(a, b):
    return lax.dot_general(a, b, (((0,), (0,)), ((), ())), preferred_element_type=F32)


def _rms_scale(x):
    return lax.rsqrt(jnp.mean(x * x, axis=-1, keepdims=True) + RMS_EPS)


def _rms_bwd(dy, x, g):
    xn = x * _rms_scale(x)
    dn = dy * g
    dx = _rms_scale(x) * (dn - xn * jnp.mean(dn * xn, axis=-1, keepdims=True))
    return dx, jnp.sum(dy * xn, axis=0, keepdims=True)


def _sigmoid(x):
    return jax.nn.sigmoid(x)


def _acc_out(ref, first, value):
    @pl.when(first)
    def _():
        ref[...] = value

    @pl.when(jnp.logical_not(first))
    def _():
        ref[...] += value


def _my_slot():
    return 4 * lax.axis_index("x") + 2 * lax.axis_index("y") + lax.axis_index("c")


def _peer(mask):
    x, y, c = lax.axis_index("x"), lax.axis_index("y"), lax.axis_index("c")
    return (x ^ ((mask >> 2) & 1), y ^ ((mask >> 1) & 1), c ^ (mask & 1))


def _cols(width):
    return lambda ref, slot: ref.at[:, pl.ds(pl.multiple_of(slot * width, 128), width)]


def _rows(height):
    return lambda ref, slot: ref.at[pl.ds(pl.multiple_of(slot * height, 8), height), :]


def _lead(ref, slot):
    return ref.at[slot]


def _pool_rows(ref, slot):
    return ref.at[:, pl.ds(pl.multiple_of(slot * (PG // N_DEV), 8), PG // N_DEV), :]


SHARDED = (
    ("w_in", (D, IN_COLS), (D, IN_COLS // N_DEV), _cols(IN_COLS // N_DEV)),
    ("w_ff1", (D, D_FF), (D, D_FF // N_DEV), _cols(D_FF // N_DEV)),
    ("w_ff2", (D_FF, D), (D_FF // N_DEV, D), _rows(D_FF // N_DEV)),
    ("w_conv_out", (D, D), (D // N_DEV, D), _rows(D // N_DEV)),
    ("w_pool_out", (D, D), (D // N_DEV, D), _rows(D // N_DEV)),
    ("w_o", (D, D), (D // N_DEV, D), _rows(D // N_DEV)),
    ("pool_w", (4, PG, PG), (4, PG // N_DEV, PG), _pool_rows),
    ("dw_kernel", (N_DEV, TAPS, D // N_DEV), (TAPS, D // N_DEV), _lead),
)
N_SHARDED = len(SHARDED)


SHARD_AT = {name: at for name, _, _, at in SHARDED}
HBM = pl.BlockSpec(memory_space=pltpu.HBM)
SEM = pl.BlockSpec(memory_space=pltpu.SEMAPHORE)
ANY = pl.BlockSpec(memory_space=pl.ANY)
EFFECT = pltpu.SideEffectType.DATAFLOW_SIDE_EFFECTING


def _in_hbm(a):
    return pltpu.with_memory_space_constraint(a, pltpu.HBM)


def _pinned(*arrays):
    return [_in_hbm(a) for a in arrays]


def _stage_shards(shards):
    dtypes = [F32 if name == "dw_kernel" else BF16 for name, *_ in SHARDED]

    def body(*refs):
        ins = refs[:N_SHARDED]
        fulls = refs[N_SHARDED:2 * N_SHARDED]
        raw = refs[2 * N_SHARDED:3 * N_SHARDED]
        stage = refs[3 * N_SHARDED:4 * N_SHARDED]
        in_sems, out_sems = refs[4 * N_SHARDED:]
        me = _my_slot()
        loads = [pltpu.make_async_copy(ins[a], raw[a], in_sems.at[a]) for a in range(N_SHARDED)]
        for cp in loads:
            cp.start()
        stores = []
        for a, (_, _, _, at) in enumerate(SHARDED):
            loads[a].wait()
            stage[a][...] = raw[a][...].astype(dtypes[a])
            cp = pltpu.make_async_copy(stage[a], at(fulls[a], me), out_sems.at[a])
            cp.start()
            stores.append(cp)
        for cp in stores:
            cp.wait()

    return pl.pallas_call(
        body, name="stage_shards",
        out_shape=[pltpu.HBM(full, dt) for (_, full, _, _), dt in zip(SHARDED, dtypes)],
        in_specs=[pl.BlockSpec(memory_space=pl.ANY)] * N_SHARDED,
        out_specs=[pl.BlockSpec(memory_space=pl.ANY)] * N_SHARDED,
        scratch_shapes=[pltpu.VMEM(shard, F32) for _, _, shard, _ in SHARDED]
        + [pltpu.VMEM(shard, dt) for (_, _, shard, _), dt in zip(SHARDED, dtypes)]
        + [pltpu.SemaphoreType.DMA((N_SHARDED,)), pltpu.SemaphoreType.DMA((N_SHARDED,))],
        compiler_params=pltpu.CompilerParams(vmem_limit_bytes=40 * MIB),
    )(*_pinned(*shards))


Leg = collections.namedtuple("Leg", "routes src_of dst_of src_is_land")


def _sem_index(k, m):
    return k * (N_DEV - 1) + m - 1


def _exchange_start(name, groups, legs):
    sizes = [len(g[0]) for g in groups]
    names = [nm for g in groups for nm in g[0]]
    srcs = [s for g in groups if g[1] is not None for s in g[1]]
    lands = [l for g in groups for l in g[2]]
    n_src, n, n_g, n_l = len(srcs), len(names), len(groups), len(legs)

    def body(*refs):
        src_refs, land_refs = list(refs[:n_src]), refs[n_src:n_src + n]
        sems = refs[n_src + n:n_src + n + 2 * n_g * n_l]
        token = refs[-1]
        me = _my_slot()
        first = 0
        for g, size in enumerate(sizes):
            own_src = [src_refs.pop(0) for _ in range(size)] if groups[g][1] is not None else None
            for li, leg in enumerate(legs):
                send, recv = sems[2 * (g * n_l + li)], sems[2 * (g * n_l + li) + 1]
                for m, via in leg.routes:
                    for k in range(size):
                        land = land_refs[first + k]
                        src = land if leg.src_is_land else own_src[k]
                        pltpu.make_async_remote_copy(
                            src_ref=leg.src_of(names[first + k], src, me, m), dst_ref=leg.dst_of(names[first + k], land, me, m),
                            send_sem=send.at[_sem_index(k, m)], recv_sem=recv.at[_sem_index(k, m)],
                            device_id=_peer(via), device_id_type=MESH).start()
            first += size
        token[...] = jnp.zeros_like(token)

    sem_shapes = [pltpu.SemaphoreType.DMA((size * (N_DEV - 1),)) for size in sizes for _ in range(2 * n_l)]
    n_sem = len(sem_shapes)
    outs = pl.pallas_call(
        body, name=name,
        out_shape=sem_shapes + [pltpu.HBM(a.shape, a.dtype) for a in srcs + lands] + [jax.ShapeDtypeStruct((8, 128), F32)],
        in_specs=[HBM] * (n_src + n),
        out_specs=[SEM] * n_sem + [HBM] * (n_src + n) + [pl.BlockSpec(memory_space=pltpu.VMEM)],
        input_output_aliases={k: n_sem + k for k in range(n_src + n)},
        compiler_params=pltpu.CompilerParams(has_side_effects=EFFECT),
    )(*[_in_hbm(a) for a in srcs + lands])
    sems, thru, token = outs[:n_sem], list(outs[n_sem:-1]), outs[-1]
    src_thru, land_thru = thru[:n_src], thru[n_src:]
    handles, first = [], 0
    for g, size in enumerate(sizes):
        pairs = [(sems[2 * (g * n_l + li)], sems[2 * (g * n_l + li) + 1]) for li in range(n_l)]
        mine = [src_thru.pop(0) for _ in range(size)] if groups[g][1] is not None else None
        handles.append((groups[g][0], pairs, mine, land_thru[first:first + size]))
        first += size
    return handles, token


def _exchange_wait(name, names, srcs, lands, waits, after):
    n = len(names)
    n_src = n if srcs is not None else 0

    def body(*refs):
        src_refs, land_refs = refs[:n_src], refs[n_src:n_src + n]
        sems = refs[n_src + n:n_src + n + 2 * len(waits)]
        me = _my_slot()
        for wi, (_, leg) in enumerate(waits):
            for m, via in leg.routes:
                for k in range(n):
                    src = land_refs[k] if leg.src_is_land else src_refs[k]
                    cp = pltpu.make_async_remote_copy(
                        src_ref=leg.src_of(names[k], src, me, m), dst_ref=leg.dst_of(names[k], land_refs[k], me ^ via, m),
                        send_sem=sems[2 * wi].at[_sem_index(k, m)], recv_sem=sems[2 * wi + 1].at[_sem_index(k, m)],
                        device_id=_peer(via), device_id_type=MESH)
                    cp.wait_send()
                    cp.wait_recv()

    arrays = (list(srcs) if srcs is not None else []) + list(lands)
    outs = pl.pallas_call(
        body, name=name,
        out_shape=[pltpu.HBM(a.shape, a.dtype) for a in arrays],
        in_specs=[HBM] * len(arrays) + [SEM] * (2 * len(waits)) + [pl.BlockSpec(memory_space=pl.ANY)],
        out_specs=[HBM] * len(arrays),
        input_output_aliases={k: k for k in range(len(arrays))},
        compiler_params=pltpu.CompilerParams(has_side_effects=EFFECT),
    )(*arrays, *[s for pair, _ in waits for s in pair], after)
    return (outs[:n_src] if srcs is not None else None), outs[n_src:]


def _shard_slot(name, ref, slot):
    return SHARD_AT[name](ref, slot)


GATHER_ICI = Leg(((2, 2), (4, 4), (6, 6)), lambda name, ref, me, m: _shard_slot(name, ref, me),
                 lambda name, ref, sender, m: _shard_slot(name, ref, sender), True)
GATHER_D2D = Leg(((1, 1),), GATHER_ICI.src_of, GATHER_ICI.dst_of, True)
GATHER_FORWARD = Leg(((2, 1), (4, 1), (6, 1)), lambda name, ref, me, m: _shard_slot(name, ref, me ^ m),
                     lambda name, ref, sender, m: _shard_slot(name, ref, sender ^ m), True)
SCATTER = Leg(tuple((m, m) for m in range(1, N_DEV)), lambda name, ref, me, m: _shard_slot(name, ref, me ^ m),
              lambda name, ref, sender, m: ref.at[m - 1], False)


def _vec_exchange(vectors, after):
    n = len(vectors)

    def body(*refs):
        vec, vec_out = refs[n + 2], refs[n + 1]
        send_sems, recv_sems, local_sem = refs[n + 3:]
        for k in range(n):
            vec[k:k + 1, :] = refs[k][...]
        me = _my_slot()
        local = pltpu.make_async_copy(vec, vec_out.at[me], local_sem)
        local.start()
        sends = []
        for mask in range(1, N_DEV):
            cp = pltpu.make_async_remote_copy(
                src_ref=vec, dst_ref=vec_out.at[me], send_sem=send_sems.at[mask - 1],
                recv_sem=recv_sems.at[mask - 1], device_id=_peer(mask), device_id_type=MESH)
            cp.start()
            sends.append(cp)
        for mask in range(1, N_DEV):
            pltpu.make_async_remote_copy(
                src_ref=vec, dst_ref=vec_out.at[me ^ mask], send_sem=send_sems.at[mask - 1],
                recv_sem=recv_sems.at[mask - 1], device_id=_peer(mask), device_id_type=MESH).wait_recv()
        for cp in sends:
            cp.wait_send()
        local.wait()

    return pl.pallas_call(
        body, name="vec_exchange",
        out_shape=jax.ShapeDtypeStruct((N_DEV, n, D), F32),
        in_specs=[pl.BlockSpec(memory_space=pltpu.VMEM)] * n + [ANY],
        out_specs=pl.BlockSpec(memory_space=pl.ANY),
        scratch_shapes=[pltpu.VMEM((n, D), F32), pltpu.SemaphoreType.DMA((N_DEV - 1,)),
                        pltpu.SemaphoreType.DMA((N_DEV - 1,)), pltpu.SemaphoreType.DMA],
    )(*vectors, after)


def _adamw_update(g, w_ref, m_ref, v_ref, g_out, d_out, m_out, v_out):
    m_new = ADAM_B1 * m_ref[...] + (1.0 - ADAM_B1) * g
    v_new = ADAM_B2 * v_ref[...] + (1.0 - ADAM_B2) * (g * g)
    m_hat = m_new / (1.0 - ADAM_B1 ** ADAM_STEP)
    v_hat = v_new / (1.0 - ADAM_B2 ** ADAM_STEP)
    g_out[...] = g
    d_out[...] = -ADAM_LR * (m_hat / (jnp.sqrt(v_hat) + ADAM_EPS) + ADAM_WD * w_ref[...])
    m_out[...] = m_new
    v_out[...] = v_new


ADAMW_ROWS = 256


def _adamw(name, own, parts, w, m, v, slot):
    shard = w.shape
    if name in ("w_in", "w_ff1"):
        tr = ADAMW_ROWS
        grid = (shard[0] // tr,)
        own_spec = pl.BlockSpec((tr, shard[1]), lambda i, s: (i, s[0]))
        blk = pl.BlockSpec((tr, shard[1]), lambda i, s: (i, 0))
        parts_spec = pl.BlockSpec((N_DEV - 1, tr, shard[1]), lambda i, s: (0, i, 0))
    elif name == "pool_w":
        grid = (shard[0],)
        own_spec = pl.BlockSpec((None,) + shard[1:], lambda g, s: (g, s[0], 0))
        blk = pl.BlockSpec((None,) + shard[1:], lambda g, s: (g, 0, 0))
        parts_spec = pl.BlockSpec((N_DEV - 1, None) + shard[1:], lambda g, s: (0, g, 0, 0))
    elif name == "dw_kernel":
        grid = (1,)
        own_spec = pl.BlockSpec((None,) + shard, lambda i, s: (s[0], 0, 0))
        blk = pl.BlockSpec(shard, lambda i, s: (0, 0))
        parts_spec = pl.BlockSpec((N_DEV - 1,) + shard, lambda i, s: (0, 0, 0))
    else:
        tr = min(ADAMW_ROWS, shard[0])
        grid = (shard[0] // tr,)
        own_spec = pl.BlockSpec((tr, shard[1]), lambda i, s: (s[0] * grid[0] + i, 0))
        blk = pl.BlockSpec((tr, shard[1]), lambda i, s: (i, 0))
        parts_spec = pl.BlockSpec((N_DEV - 1, tr, shard[1]), lambda i, s: (0, i, 0))

    def body(slot_ref, own_ref, p_ref, w_ref, m_ref, v_ref, g_out, d_out, m_out, v_out):
        g = own_ref[...].astype(F32)
        for k in range(N_DEV - 1):
            g = g + p_ref[k].astype(F32)
        _adamw_update(g, w_ref, m_ref, v_ref, g_out, d_out, m_out, v_out)

    return pl.pallas_call(
        body, name="adamw_" + name,
        grid_spec=pltpu.PrefetchScalarGridSpec(
            num_scalar_prefetch=1, grid=grid, in_specs=[own_spec, parts_spec, blk, blk, blk], out_specs=[blk] * 4),
        out_shape=[jax.ShapeDtypeStruct(shard, F32)] * 4,
        compiler_params=_params(("arbitrary",), 32),
    )(slot, *_pinned(own, parts, w, m, v))


def _adamw_vectors(parts, w, m, v):
    n_vec = w.shape[0]

    def body(p_ref, w_ref, m_ref, v_ref, g_out, d_out, m_out, v_out, loss_out):
        total = p_ref[0]
        for s in range(1, N_DEV):
            total = total + p_ref[s]
        _adamw_update(total[0:n_vec], w_ref, m_ref, v_ref, g_out, d_out, m_out, v_out)
        loss_out[...] = total[n_vec:n_vec + 1] * (0.5 / D)

    return pl.pallas_call(
        body, name="adamw_vectors",
        out_shape=[jax.ShapeDtypeStruct(w.shape, F32)] * 4 + [jax.ShapeDtypeStruct((1, D), F32)],
    )(parts, w, m, v)


def _in_proj_fwd(x, g_pre, w_in, token):
    seq = x.shape[0]
    tm, tn = 1024, IN_COLS // 4

    def body(x_ref, g_ref, w_ref, _token, proj_ref, u_ref):
        @pl.when(pl.program_id(1) == 0)
        def _():
            xf = x_ref[...]
            u_ref[...] = (xf * _rms_scale(xf) * g_ref[...]).astype(BF16)
        proj_ref[...] = _dot(u_ref[...], w_ref[...]).astype(BF16)

    return pl.pallas_call(
        body, name="in_proj_fwd", grid=(seq // tm, IN_COLS // tn),
        in_specs=[pl.BlockSpec((tm, D), lambda i, j: (i, 0)), pl.BlockSpec((1, D), lambda i, j: (0, 0)),
                  pl.BlockSpec((D, tn), lambda i, j: (0, j)), ANY],
        out_specs=[pl.BlockSpec((tm, tn), lambda i, j: (i, j)), pl.BlockSpec((tm, D), lambda i, j: (i, 0))],
        out_shape=[pltpu.HBM((seq, IN_COLS), BF16), pltpu.HBM((seq, D), BF16)],
        compiler_params=_params(("arbitrary", "arbitrary"), 40),
    )(*_pinned(x, g_pre, w_in, token))


CONV_TM = 256
CONV_RS = 128


def _shifts():
    return [(b, [(a, 8 * a + b) for a in range(4) if 8 * a + b < TAPS]) for b in range(8)]


def _taps_looking_back(buf, row0, lanes, weight):
    acc = None
    for b, group in _shifts():
        part = None
        for a, s in group:
            term = weight(TAPS - 1 - s) * buf[pl.ds(row0 - 8 - 8 * a, CONV_RS + 8), lanes]
            part = term if part is None else part + term
        if b:
            part = pltpu.roll(part, b, 0)
        acc = part[8:, :] if acc is None else acc + part[8:, :]
    return acc


def _taps_looking_ahead(buf, row0, lanes, weight):
    acc = None
    for b, group in _shifts():
        part = None
        for a, s in group:
            term = weight(TAPS - 1 - s) * buf[pl.ds(row0 + 8 * a, CONV_RS + 8), lanes]
            part = term if part is None else part + term
        if b:
            part = pltpu.roll(part, CONV_RS + 8 - b, 0)
        acc = part[:CONV_RS, :] if acc is None else acc + part[:CONV_RS, :]
    return acc


def _layer_norm_parts(cv):
    mu = jnp.mean(cv, axis=-1, keepdims=True)
    cen = cv - mu
    rstd = lax.rsqrt(jnp.mean(cen * cen, axis=-1, keepdims=True) + LN_EPS)
    return cen * rstd, rstd


def _conv_fwd(proj, dw, dw_bias, ln_g, ln_b, w_conv_out, token):
    seq = proj.shape[0]
    tm = CONV_TM

    def body(a_ref, gate_ref, dw_ref, bias_ref, lg_ref, lb_ref, w_ref, _token, cv_ref, y_ref, ug):
        i = pl.program_id(0)

        @pl.when(i == 0)
        def _():
            ug[0:HALO, :] = jnp.zeros((HALO, D), F32)

        @pl.when(i > 0)
        def _():
            ug[0:HALO, :] = ug[tm:tm + HALO, :]
        ug[HALO:HALO + tm, :] = a_ref[...].astype(F32) * _sigmoid(gate_ref[...].astype(F32))

        def channel_block(cb, carry):
            lanes = pl.ds(pl.multiple_of(cb * 128, 128), 128)
            for r0 in range(0, tm, CONV_RS):
                taps = _taps_looking_back(ug, HALO + r0, lanes, lambda k: dw_ref[cb, k:k + 1, :])
                cv_ref[pl.ds(r0, CONV_RS), lanes] = taps + bias_ref[:, lanes]
            return carry
        lax.fori_loop(0, D // 128, channel_block, 0)

        n, _ = _layer_norm_parts(cv_ref[...])
        ln = n * lg_ref[...] + lb_ref[...]
        y_ref[...] = _dot((ln * _sigmoid(ln)).astype(BF16), w_ref[...]).astype(BF16)

    vec = pl.BlockSpec((1, D), lambda i: (0, 0))
    tile = pl.BlockSpec((tm, D), lambda i: (i, 0))
    return pl.pallas_call(
        body, name="conv_fwd", grid=(seq // tm,),
        in_specs=[pl.BlockSpec((tm, D), lambda i: (i, 0)), pl.BlockSpec((tm, D), lambda i: (i, 1)),
                  pl.BlockSpec((N_DEV, TAPS, 128), lambda i: (0, 0, 0)), vec, vec, vec,
                  pl.BlockSpec((D, D), lambda i: (0, 0)), ANY],
        out_specs=[tile, tile],
        out_shape=[pltpu.HBM((seq, D), F32), pltpu.HBM((seq, D), BF16)],
        scratch_shapes=[pltpu.VMEM((HALO + tm, D), F32)],
        compiler_params=_params(("arbitrary",), 32),
    )(*_pinned(proj, proj, dw, dw_bias, ln_g, ln_b, w_conv_out, token))


def _pool_counts(tile_index, tm, window):
    t = tile_index * tm + lax.broadcasted_iota(jnp.int32, (tm, 1), 0)
    return 1.0 / jnp.minimum(t + 1, window).astype(F32)


def _pool_merge_fwd(proj, y_conv, x, pool_w, pool_scale, w_pool_out, w_o, g_post, token):
    seq = proj.shape[0]
    tm = CONV_TM

    def body(p_ref, gc_ref, gp_ref, yc_ref, x_ref, pw_ref, ps_ref, wpo_ref, wo_ref, g_ref, _token,
             z_ref, zl_ref, yp_ref, mg_ref, o_ref, h1_ref, pbuf):
        i = pl.program_id(0)

        @pl.when(i == 0)
        def _():
            pbuf[0:HALO, :] = jnp.zeros((HALO, D), F32)

        @pl.when(i > 0)
        def _():
            pbuf[0:HALO, :] = pbuf[tm:tm + HALO, :]
        pbuf[HALO:HALO + tm, :] = p_ref[...].astype(F32)

        for g, window in enumerate(POOL_WINDOWS):
            lanes = pl.ds(g * PG, PG)
            acc = pbuf[pl.ds(HALO, tm), lanes]
            for j in range(1, window):
                acc = acc + pbuf[pl.ds(HALO - j, tm), lanes]
            zg = acc * _pool_counts(i, tm, window) - pbuf[pl.ds(HALO, tm), lanes]
            z_ref[:, lanes] = zg.astype(BF16)
            zl_ref[:, lanes] = _dot(zg.astype(BF16), pw_ref[g])
        zl = zl_ref[...]
        y_pool = _dot((zl * ps_ref[...]).astype(BF16), wpo_ref[...])
        yp_ref[...] = y_pool.astype(BF16)
        merged = (_sigmoid(gc_ref[...].astype(F32)) * yc_ref[...].astype(F32)
                  + _sigmoid(gp_ref[...].astype(F32)) * y_pool).astype(BF16)
        mg_ref[...] = merged
        o = _dot(merged, wo_ref[...])
        o_ref[...] = o
        h1_ref[...] = x_ref[...] + o * _rms_scale(o) * g_ref[...]

    vec = pl.BlockSpec((1, D), lambda i: (0, 0))
    tile = pl.BlockSpec((tm, D), lambda i: (i, 0))
    mat = pl.BlockSpec((D, D), lambda i: (0, 0))
    return pl.pallas_call(
        body, name="pool_merge_fwd", grid=(seq // tm,),
        in_specs=[pl.BlockSpec((tm, D), lambda i: (i, 2)), pl.BlockSpec((tm, D), lambda i: (i, 3)),
                  pl.BlockSpec((tm, D), lambda i: (i, 4)), tile, tile,
                  pl.BlockSpec((4, PG, PG), lambda i: (0, 0, 0)), vec, mat, mat, vec, ANY],
        out_specs=[tile] * 6,
        out_shape=[pltpu.HBM((seq, D), dt) for dt in (BF16, F32, BF16, BF16, F32, F32)],
        scratch_shapes=[pltpu.VMEM((HALO + tm, D), F32)],
        compiler_params=_params(("arbitrary",), 48),
    )(*_pinned(proj, proj, proj, y_conv, x, pool_w, pool_scale, w_pool_out, w_o, g_post, token))


def _mlp_fwd(h1, g_pre, w_ff1, w_ff2, g_post, target):
    seq = h1.shape[0]
    tm, tf = 1024, D_FF // N_DEV
    n_f = D_FF // tf

    def body(h1_ref, gpre_ref, w1_ref, w2_ref, gpost_ref, tgt_ref, v_ref, dm_ref, dh2_ref, sse_ref, ggrad_ref, macc):
        i, j = pl.program_id(0), pl.program_id(1)

        @pl.when(j == 0)
        def _():
            h = h1_ref[...]
            v_ref[...] = (h * _rms_scale(h) * gpre_ref[...]).astype(BF16)
        f = jnp.maximum(_dot(v_ref[...], w1_ref[...]), 0.0)
        part = _dot((f * f).astype(BF16), w2_ref[...])

        @pl.when(j == 0)
        def _():
            macc[...] = part

        @pl.when(j > 0)
        def _():
            macc[...] += part

        @pl.when(j == n_f - 1)
        def _():
            mo = macc[...]
            err = h1_ref[...] + mo * _rms_scale(mo) * gpost_ref[...] - tgt_ref[...]
            dh2 = err * (1.0 / D)
            dh2_ref[...] = dh2
            dm, ggrad = _rms_bwd(dh2, mo, gpost_ref[...])
            dm_ref[...] = dm.astype(BF16)
            _acc_out(ggrad_ref, i == 0, ggrad)
            _acc_out(sse_ref, i == 0, jnp.sum(jnp.sum(err * err, axis=1, keepdims=True), axis=0, keepdims=True))

    vec = pl.BlockSpec((1, D), lambda i, j: (0, 0))
    tile = pl.BlockSpec((tm, D), lambda i, j: (i, 0))
    return pl.pallas_call(
        body, name="mlp_fwd", grid=(seq // tm, n_f),
        in_specs=[tile, vec, pl.BlockSpec((D, tf), lambda i, j: (0, j)), pl.BlockSpec((tf, D), lambda i, j: (j, 0)), vec, tile],
        out_specs=[tile, tile, tile, pl.BlockSpec((1, 1), lambda i, j: (0, 0)), vec],
        out_shape=[pltpu.HBM((seq, D), BF16), pltpu.HBM((seq, D), BF16),
                   pltpu.HBM((seq, D), F32), jax.ShapeDtypeStruct((1, 1), F32),
                   jax.ShapeDtypeStruct((1, D), F32)],
        scratch_shapes=[pltpu.VMEM((tm, D), F32)],
        compiler_params=_params(("arbitrary", "arbitrary"), 56),
    )(*_pinned(h1, g_pre, w_ff1, w_ff2, g_post, target))


def _mlp_bwd(v, dm, w_ff1, w_ff2):
    seq = v.shape[0]
    tm, tf = 1024, D_FF // N_DEV
    n_t = seq // tm

    def body(v_ref, dm_ref, w1_ref, w2_ref, dv_hbm, g1_ref, g2_ref, dv_acc, g1_acc, g2_acc, sem):
        j, i = pl.program_id(0), pl.program_id(1)
        vt, dmt = v_ref[...], dm_ref[...]
        f = jnp.maximum(_dot(vt, w1_ref[...]), 0.0)
        df = (_dot_nt(dmt, w2_ref[...]) * (2.0 * f)).astype(BF16)
        rows = pl.ds(pl.multiple_of(i * tm, tm), tm)
        dv_part = _dot_nt(df, w1_ref[...])

        @pl.when(j == 0)
        def _():
            dv_acc[rows, :] = dv_part

        @pl.when(j > 0)
        def _():
            dv_acc[rows, :] += dv_part
        g1_part = _dot_tn(vt, df)
        g2_part = _dot_tn((f * f).astype(BF16), dmt)

        @pl.when(i == 0)
        def _():
            g1_acc[...] = g1_part
            g2_acc[...] = g2_part

        @pl.when(i > 0)
        def _():
            g1_acc[...] += g1_part
            g2_acc[...] += g2_part

        @pl.when(i == n_t - 1)
        def _():
            g1_ref[...] = g1_acc[...].astype(BF16)
            g2_ref[...] = g2_acc[...].astype(BF16)

        @pl.when(jnp.logical_and(i == n_t - 1, j == D_FF // tf - 1))
        def _():
            cp = pltpu.make_async_copy(dv_acc, dv_hbm, sem)
            cp.start()
            cp.wait()

    tile = pl.BlockSpec((tm, D), lambda j, i: (i, 0))
    return pl.pallas_call(
        body, name="mlp_bwd", grid=(D_FF // tf, n_t),
        in_specs=[tile, tile, pl.BlockSpec((D, tf), lambda j, i: (0, j)), pl.BlockSpec((tf, D), lambda j, i: (j, 0))],
        out_specs=[pl.BlockSpec(memory_space=pl.ANY), pl.BlockSpec((D, tf), lambda j, i: (0, j)),
                   pl.BlockSpec((tf, D), lambda j, i: (j, 0))],
        out_shape=[pltpu.HBM((seq, D), F32), pltpu.HBM((D, D_FF), BF16),
                   pltpu.HBM((D_FF, D), BF16)],
        scratch_shapes=[pltpu.VMEM((seq, D), F32), pltpu.VMEM((D, tf), F32), pltpu.VMEM((tf, D), F32),
                        pltpu.SemaphoreType.DMA],
        compiler_params=_params(("arbitrary", "arbitrary"), 52),
    )(*_pinned(v, dm, w_ff1, w_ff2))


def _merge_bwd(dh2, dv, h1, g_mlp_pre, o, g_mix_post, w_o, merged, proj, y_conv, y_pool, token):
    seq = dh2.shape[0]
    tm = 256
    n_t = seq // tm

    def body(dh2_ref, dv_ref, h1_ref, gpre_ref, o_ref, gpost_ref, wo_ref, mg_ref, gc_ref, gp_ref, yc_ref, yp_ref, _token,
             dh1_ref, dyc_ref, dyp_ref, dg_ref, gwo_ref, ggpre_ref, ggpost_ref, gwo_acc):
        i = pl.program_id(0)
        dnorm, ggpre = _rms_bwd(dv_ref[...], h1_ref[...], gpre_ref[...])
        dh1 = dh2_ref[...] + dnorm
        dh1_ref[...] = dh1
        do, ggpost = _rms_bwd(dh1, o_ref[...], gpost_ref[...])
        do = do.astype(BF16)
        _acc_out(ggpre_ref, i == 0, ggpre)
        _acc_out(ggpost_ref, i == 0, ggpost)
        _acc_out(gwo_acc, i == 0, _dot_tn(mg_ref[...], do))
        dmerged = _dot_nt(do, wo_ref[...])
        sc, sp = _sigmoid(gc_ref[...].astype(F32)), _sigmoid(gp_ref[...].astype(F32))
        dyc_ref[...] = (dmerged * sc).astype(BF16)
        dyp_ref[...] = (dmerged * sp).astype(BF16)
        dg_ref[:, 0:D] = (dmerged * yc_ref[...].astype(F32) * (sc * (1.0 - sc))).astype(BF16)
        dg_ref[:, D:2 * D] = (dmerged * yp_ref[...].astype(F32) * (sp * (1.0 - sp))).astype(BF16)

        @pl.when(i == n_t - 1)
        def _():
            gwo_ref[...] = gwo_acc[...].astype(BF16)

    vec = pl.BlockSpec((1, D), lambda i: (0, 0))
    tile = pl.BlockSpec((tm, D), lambda i: (i, 0))
    mat = pl.BlockSpec((D, D), lambda i: (0, 0))
    return pl.pallas_call(
        body, name="merge_bwd", grid=(n_t,),
        in_specs=[tile, tile, tile, vec, tile, vec, mat, tile,
                  pl.BlockSpec((tm, D), lambda i: (i, 3)), pl.BlockSpec((tm, D), lambda i: (i, 4)), tile, tile, ANY],
        out_specs=[tile, tile, tile, pl.BlockSpec((tm, 2 * D), lambda i: (i, 0)), mat, vec, vec],
        out_shape=[pltpu.HBM((seq, D), F32), pltpu.HBM((seq, D), BF16),
                   pltpu.HBM((seq, D), BF16), pltpu.HBM((seq, 2 * D), BF16),
                   pltpu.HBM((D, D), BF16), jax.ShapeDtypeStruct((1, D), F32),
                   jax.ShapeDtypeStruct((1, D), F32)],
        scratch_shapes=[pltpu.VMEM((D, D), F32)],
        compiler_params=_params(("arbitrary",), 48),
    )(*_pinned(dh2, dv, h1, g_mlp_pre, o, g_mix_post, w_o, merged, proj, proj, y_conv, y_pool, token))


def _pool_bwd(dy_pool, zl, z, pool_w, pool_scale, w_pool_out):
    seq = dy_pool.shape[0]
    tm = CONV_TM
    n_t = seq // tm

    def body(dy_ref, zl_ref, z_ref, pw_ref, ps_ref, wpo_ref, dp_ref, gwpo_ref, gpw_ref, gps_ref, qbuf, gwpo_acc, gpw_acc):
        i = pl.program_id(0)
        tile_index = n_t - 1 - i
        first = i == 0
        dy = dy_ref[...]
        zl = zl_ref[...]
        dzs = _dot_nt(dy, wpo_ref[...])
        _acc_out(gwpo_acc, first, _dot_tn((zl * ps_ref[...]).astype(BF16), dy))
        _acc_out(gps_ref, first, jnp.sum(dzs * zl, axis=0, keepdims=True))
        dzl = (dzs * ps_ref[...]).astype(BF16)

        @pl.when(first)
        def _():
            qbuf[tm:tm + HALO, :] = jnp.zeros((HALO, D), F32)

        @pl.when(jnp.logical_not(first))
        def _():
            qbuf[tm:tm + HALO, :] = qbuf[0:HALO, :]

        dzs_list = []
        for g, window in enumerate(POOL_WINDOWS):
            lanes = pl.ds(g * PG, PG)
            dzl_g = dzl[:, g * PG:(g + 1) * PG]
            dz = _dot_nt(dzl_g, pw_ref[g])
            _acc_out(gpw_acc.at[g], first, _dot_tn(z_ref[:, lanes], dzl_g))
            qbuf[pl.ds(0, tm), lanes] = dz * _pool_counts(tile_index, tm, window)
            dzs_list.append(dz)
        for g, window in enumerate(POOL_WINDOWS):
            lanes = pl.ds(g * PG, PG)
            acc = qbuf[pl.ds(0, tm), lanes]
            for j in range(1, window):
                acc = acc + qbuf[pl.ds(j, tm), lanes]
            dp_ref[:, lanes] = (acc - dzs_list[g]).astype(BF16)

        @pl.when(i == n_t - 1)
        def _():
            gwpo_ref[...] = gwpo_acc[...].astype(BF16)
            gpw_ref[...] = gpw_acc[...].astype(BF16)

    vec = pl.BlockSpec((1, D), lambda i: (0, 0))
    tile = pl.BlockSpec((tm, D), lambda i: (n_t - 1 - i, 0))
    mat = pl.BlockSpec((D, D), lambda i: (0, 0))
    pw = pl.BlockSpec((4, PG, PG), lambda i: (0, 0, 0))
    return pl.pallas_call(
        body, name="pool_bwd", grid=(n_t,),
        in_specs=[tile, tile, tile, pw, vec, mat],
        out_specs=[tile, mat, pw, vec],
        out_shape=[pltpu.HBM((seq, D), BF16), pltpu.HBM((D, D), BF16),
                   pltpu.HBM((4, PG, PG), BF16), jax.ShapeDtypeStruct((1, D), F32)],
        scratch_shapes=[pltpu.VMEM((tm + HALO, D), F32), pltpu.VMEM((D, D), F32), pltpu.VMEM((4, PG, PG), F32)],
        compiler_params=_params(("arbitrary",), 40),
    )(*_pinned(dy_pool, zl, z, pool_w, pool_scale, w_pool_out))


def _conv_bwd(dy_conv, cv, proj, dw, ln_g, ln_b, w_conv_out):
    seq = dy_conv.shape[0]
    tm = CONV_TM
    n_t = seq // tm
    halo_blocks = tm // HALO

    def body(dy_ref, cv_ref, a_ref, gate_ref, ah_ref, gh_ref, dw_ref, lg_ref, lb_ref, w_ref,
             dglu_ref, gw_ref, gdw_ref, gbias_ref, glg_ref, glb_ref, ug, dcv, dug, gw_acc):
        i = pl.program_id(0)
        tile_index = n_t - 1 - i
        first = i == 0
        dy = dy_ref[...]
        n, rstd = _layer_norm_parts(cv_ref[...])
        ln = n * lg_ref[...] + lb_ref[...]
        sg = _sigmoid(ln)
        _acc_out(gw_acc, first, _dot_tn((ln * sg).astype(BF16), dy))
        dln = _dot_nt(dy, w_ref[...]) * (sg * (1.0 + ln * (1.0 - sg)))
        _acc_out(glg_ref, first, jnp.sum(dln * n, axis=0, keepdims=True))
        _acc_out(glb_ref, first, jnp.sum(dln, axis=0, keepdims=True))
        dn = dln * lg_ref[...]
        dcv_tile = rstd * (dn - jnp.mean(dn, axis=-1, keepdims=True) - n * jnp.mean(dn * n, axis=-1, keepdims=True))
        _acc_out(gbias_ref, first, jnp.sum(dcv_tile, axis=0, keepdims=True))

        @pl.when(first)
        def _():
            dcv[tm:tm + HALO, :] = jnp.zeros((HALO, D), F32)

        @pl.when(jnp.logical_not(first))
        def _():
            dcv[tm:tm + HALO, :] = dcv[0:HALO, :]
        dcv[0:tm, :] = dcv_tile

        a, gate = a_ref[...].astype(F32), gate_ref[...].astype(F32)
        sgate = _sigmoid(gate)
        ug[HALO:HALO + tm, :] = a * sgate
        before = jnp.where(tile_index > 0, 1.0, 0.0)
        ug[0:HALO, :] = ah_ref[...].astype(F32) * _sigmoid(gh_ref[...].astype(F32)) * before

        @pl.when(first)
        def _():
            gdw_ref[...] = jnp.zeros((N_DEV, TAPS + 1, 128), F32)

        def channel_block(cb, carry):
            lanes = pl.ds(pl.multiple_of(cb * 128, 128), 128)
            for r0 in range(0, tm, CONV_RS):
                dug[pl.ds(r0, CONV_RS), lanes] = _taps_looking_ahead(dcv, r0, lanes, lambda k: dw_ref[cb, k:k + 1, :])
            for b, group in _shifts():
                sums = [jnp.zeros((8, 128), F32) for _ in group]
                for r0 in range(0, tm, CONV_RS):
                    window = ug[pl.ds(r0, CONV_RS + HALO), lanes]
                    if b:
                        window = pltpu.roll(window, b, 0)
                    d = dcv[pl.ds(r0, CONV_RS), lanes]
                    for n_a, (a, s) in enumerate(group):
                        prod = d * window[HALO - 8 * a:HALO - 8 * a + CONV_RS, :]
                        sums[n_a] = sums[n_a] + jnp.sum(prod.reshape(CONV_RS // 8, 8, 128), axis=0)
                for n_a, (a, s) in enumerate(group):
                    k = TAPS - 1 - s
                    gdw_ref[cb, k:k + 1, :] += jnp.sum(sums[n_a], axis=0, keepdims=True)
            return carry
        lax.fori_loop(0, D // 128, channel_block, 0)

        d_ug = dug[...]
        dglu_ref[:, 0:D] = (d_ug * sgate).astype(BF16)
        dglu_ref[:, D:2 * D] = (d_ug * a * (sgate * (1.0 - sgate))).astype(BF16)

        @pl.when(i == n_t - 1)
        def _():
            gw_ref[...] = gw_acc[...].astype(BF16)

    def halo_index(col):
        return lambda i: (jnp.maximum((n_t - 1 - i) * halo_blocks - 1, 0), col)

    vec = pl.BlockSpec((1, D), lambda i: (0, 0))
    tile = pl.BlockSpec((tm, D), lambda i: (n_t - 1 - i, 0))
    mat = pl.BlockSpec((D, D), lambda i: (0, 0))
    dwspec = pl.BlockSpec((N_DEV, TAPS, 128), lambda i: (0, 0, 0))
    return pl.pallas_call(
        body, name="conv_bwd", grid=(n_t,),
        in_specs=[tile, tile, pl.BlockSpec((tm, D), lambda i: (n_t - 1 - i, 0)), pl.BlockSpec((tm, D), lambda i: (n_t - 1 - i, 1)),
                  pl.BlockSpec((HALO, D), halo_index(0)), pl.BlockSpec((HALO, D), halo_index(1)), dwspec, vec, vec, mat],
        out_specs=[pl.BlockSpec((tm, 2 * D), lambda i: (n_t - 1 - i, 0)), mat,
                   pl.BlockSpec((N_DEV, TAPS + 1, 128), lambda i: (0, 0, 0)), vec, vec, vec],
        out_shape=[pltpu.HBM((seq, 2 * D), BF16), pltpu.HBM((D, D), BF16),
                   pltpu.HBM((N_DEV, TAPS + 1, 128), F32), jax.ShapeDtypeStruct((1, D), F32),
                   jax.ShapeDtypeStruct((1, D), F32), jax.ShapeDtypeStruct((1, D), F32)],
        scratch_shapes=[pltpu.VMEM((HALO + tm, D), F32), pltpu.VMEM((tm + HALO, D), F32), pltpu.VMEM((tm, D), F32),
                        pltpu.VMEM((D, D), F32)],
        compiler_params=_params(("arbitrary",), 48),
    )(*_pinned(dy_conv, cv, proj, proj, proj, proj, dw, ln_g, ln_b, w_conv_out))


def _in_proj_bwd_x(d_glu, dp, dgates, w_in, x, g_pre, dh1, token):
    seq = x.shape[0]
    tm = 256

    def body(dglu_ref, dp_ref, dg_ref, w_ref, x_ref, g_ref, dh1_ref, _token, dx_ref, gg_ref):
        du = _dot_nt(dglu_ref[...], w_ref[:, 0:2 * D])
        du += _dot_nt(dp_ref[...], w_ref[:, 2 * D:3 * D])
        du += _dot_nt(dg_ref[...], w_ref[:, 3 * D:5 * D])
        dnorm, gg = _rms_bwd(du, x_ref[...], g_ref[...])
        dx_ref[...] = dh1_ref[...] + dnorm
        _acc_out(gg_ref, pl.program_id(0) == 0, gg)

    vec = pl.BlockSpec((1, D), lambda i: (0, 0))
    tile = pl.BlockSpec((tm, D), lambda i: (i, 0))
    wide = pl.BlockSpec((tm, 2 * D), lambda i: (i, 0))
    return pl.pallas_call(
        body, name="in_proj_bwd_x", grid=(seq // tm,),
        in_specs=[wide, tile, wide, pl.BlockSpec((D, IN_COLS), lambda i: (0, 0)), tile, vec, tile, ANY],
        out_specs=[tile, vec],
        out_shape=[pltpu.HBM((seq, D), F32), jax.ShapeDtypeStruct((1, D), F32)],
        compiler_params=_params(("arbitrary",), 48),
    )(*_pinned(d_glu, dp, dgates, w_in, x, g_pre, dh1, token))


def _in_proj_bwd_w(u, d_glu, dp, dgates, token):
    seq = u.shape[0]
    tm = 1024
    n_t = seq // tm

    def body(u_ref, dglu_ref, dp_ref, dg_ref, _token, out_ref, acc):
        b, i = pl.program_id(0), pl.program_id(1)
        ut = u_ref[...]

        def add(d_ref):
            _acc_out(acc, i == 0, _dot_tn(ut, d_ref[...]))

        pl.when(b < 2)(lambda: add(dglu_ref))
        pl.when(b == 2)(lambda: add(dp_ref))
        pl.when(b > 2)(lambda: add(dg_ref))

        @pl.when(i == n_t - 1)
        def _():
            out_ref[...] = acc[...].astype(BF16)

    return pl.pallas_call(
        body, name="in_proj_bwd_w", grid=(IN_COLS // D, n_t),
        in_specs=[pl.BlockSpec((tm, D), lambda b, i: (i, 0)),
                  pl.BlockSpec((tm, D), lambda b, i: (jnp.where(b < 2, i, 0), jnp.minimum(b, 1))),
                  pl.BlockSpec((tm, D), lambda b, i: (jnp.where(b == 2, i, 0), 0)),
                  pl.BlockSpec((tm, D), lambda b, i: (jnp.where(b > 2, i, 0), jnp.maximum(b - 3, 0))), ANY],
        out_specs=pl.BlockSpec((D, D), lambda b, i: (0, b)),
        out_shape=pltpu.HBM((D, IN_COLS), BF16),
        scratch_shapes=[pltpu.VMEM((D, D), F32)],
        compiler_params=_params(("arbitrary", "arbitrary"), 40),
    )(*_pinned(u, d_glu, dp, dgates, token))


VEC_NAMES = ("mix_pre_g", "dw_bias", "conv_ln_g", "conv_ln_b", "pool_scale", "mix_post_g", "mlp_pre_g", "mlp_post_g")
WEIGHT_ORDER = ("mix_pre_g", "w_in", "dw_kernel", "dw_bias", "conv_ln_g", "conv_ln_b", "w_conv_out", "pool_w",
                "pool_scale", "w_pool_out", "w_o", "mix_post_g", "mlp_pre_g", "w_ff1", "w_ff2", "mlp_post_g")


def _step(x, loss_target, w, m, v):
    row = lambda a: a.reshape(1, D)
    names = [s[0] for s in SHARDED]

    seeded = dict(zip(names, _stage_shards([w[n] for n in names])))
    gather_groups = (("w_in",), ("w_conv_out", "dw_kernel"), ("pool_w", "w_pool_out", "w_o"), ("w_ff1", "w_ff2"))
    first_level, token = _exchange_start(
        "gather_start", [(g, None, [seeded[n] for n in g]) for g in gather_groups], [GATHER_ICI, GATHER_D2D])
    forwarded, full = {}, {}

    def forward(k, after):
        group, (ici_sems, d2d_sems), _, landed = first_level[k]
        _, landed = _exchange_wait("gather_ici_" + group[0], group, None, landed, [(ici_sems, GATHER_ICI)], after)
        (second,), tok = _exchange_start("gather_forward_" + group[0], [(group, None, landed)], [GATHER_FORWARD])
        forwarded[k] = (second[3], [(d2d_sems, GATHER_D2D), (second[1][0], GATHER_FORWARD)])
        return tok

    def gathered(k, after):
        group = gather_groups[k]
        landed, waits = forwarded[k]
        _, arrays = _exchange_wait("gather_wait_" + group[0], group, None, landed, waits, after)
        full.update(zip(group, arrays))

    gathered(0, forward(0, token))
    tok = forward(1, full["w_in"])
    proj, u = _in_proj_fwd(x, row(w["mix_pre_g"]), full["w_in"], tok)
    gathered(1, proj)
    tok = forward(2, proj)
    cv, y_conv = _conv_fwd(proj, full["dw_kernel"], row(w["dw_bias"]), row(w["conv_ln_g"]), row(w["conv_ln_b"]),
                           full["w_conv_out"], tok)
    gathered(2, y_conv)
    tok = forward(3, y_conv)
    z, zl, y_pool, merged, o, h1 = _pool_merge_fwd(proj, y_conv, x, full["pool_w"], row(w["pool_scale"]),
                                                   full["w_pool_out"], full["w_o"], row(w["mix_post_g"]), tok)
    gathered(3, h1)
    vv, dm, dh2, sse, g_mlp_post = _mlp_fwd(h1, row(w["mlp_pre_g"]), full["w_ff1"], full["w_ff2"],
                                            row(w["mlp_post_g"]), loss_target)

    shard_of = {name: shard for name, _, shard, _ in SHARDED}

    def scatter_start(tag, group, grads):
        landings = [lax.empty((N_DEV - 1,) + shard_of[n], g.dtype) for n, g in zip(group, grads)]
        (handle,), tok = _exchange_start("scatter_start_" + tag, [(group, grads, landings)], [SCATTER])
        return handle, tok

    dv, g_ff1, g_ff2 = _mlp_bwd(vv, dm, full["w_ff1"], full["w_ff2"])
    h_ff, tok_ff = scatter_start("ff", ("w_ff1", "w_ff2"), [g_ff1, g_ff2])
    dh1, dy_conv, dy_pool, dgates, g_wo, g_mlp_pre, g_mix_post = _merge_bwd(
        dh2, dv, h1, row(w["mlp_pre_g"]), o, row(w["mix_post_g"]), full["w_o"], merged, proj, y_conv, y_pool, tok_ff)
    dp, g_wpo, g_pw, g_pool_scale = _pool_bwd(dy_pool, zl, z, full["pool_w"], row(w["pool_scale"]), full["w_pool_out"])
    d_glu, g_wco, g_dw, g_bias, g_ln_g, g_ln_b = _conv_bwd(dy_conv, cv, proj, full["dw_kernel"], row(w["conv_ln_g"]),
                                                            row(w["conv_ln_b"]), full["w_conv_out"])
    h_mix, tok_mix = scatter_start("mix", ("w_o", "w_pool_out", "pool_w", "w_conv_out", "dw_kernel"),
                                   [g_wo, g_wpo, g_pw, g_wco, g_dw[:, :TAPS, :]])
    g_win = _in_proj_bwd_w(u, d_glu, dp, dgates, tok_mix)
    h_in, tok_in = scatter_start("in", ("w_in",), [g_win])
    grad_x, g_mix_pre = _in_proj_bwd_x(d_glu, dp, dgates, full["w_in"], x, row(w["mix_pre_g"]), dh1, tok_in)

    out = {}
    slot = jnp.reshape(_my_slot(), (1,)).astype(jnp.int32)
    after = grad_x
    for tag, (group, (sems,), grads, landings) in (("ff", h_ff), ("mix", h_mix), ("in", h_in)):
        if tag == "in":
            vec_parts = after = _vec_exchange(
                [g_mix_pre, g_bias, g_ln_g, g_ln_b, g_pool_scale, g_mix_post, g_mlp_pre, g_mlp_post,
                 jnp.broadcast_to(sse, (1, D))], after)
        mine, landed = _exchange_wait("scatter_wait_" + tag, group, grads, landings, [(sems, SCATTER)], after)
        for name, own, parts in zip(group, mine, landed):
            out[name] = _adamw(name, own, parts, w[name], m[name], v[name], slot)
            after = out[name][0]
    stack = lambda d: jnp.stack([d[n] for n in VEC_NAMES], axis=0)
    *res, loss_row = _adamw_vectors(vec_parts, stack(w), stack(m), stack(v))
    for k, name in enumerate(VEC_NAMES):
        out[name] = [r[k] for r in res]
    return loss_row[0, 0], grad_x, out


def kernel(x, mix_pre_g, w_in, dw_kernel, dw_bias, conv_ln_g, conv_ln_b, w_conv_out, pool_w, pool_scale, w_pool_out, w_o, mix_post_g, mlp_pre_g, w_ff1, w_ff2, mlp_post_g, loss_target, m_mix_pre_g, m_w_in, m_dw_kernel, m_dw_bias, m_conv_ln_g, m_conv_ln_b, m_w_conv_out, m_pool_w, m_pool_scale, m_w_pool_out, m_w_o, m_mix_post_g, m_mlp_pre_g, m_w_ff1, m_w_ff2, m_mlp_post_g, v_mix_pre_g, v_w_in, v_dw_kernel, v_dw_bias, v_conv_ln_g, v_conv_ln_b, v_w_conv_out, v_pool_w, v_pool_scale, v_w_pool_out, v_w_o, v_mix_post_g, v_mlp_pre_g, v_w_ff1, v_w_ff2, v_mlp_post_g):
    w = dict(mix_pre_g=mix_pre_g, w_in=w_in, dw_kernel=dw_kernel, dw_bias=dw_bias, conv_ln_g=conv_ln_g, conv_ln_b=conv_ln_b,
             w_conv_out=w_conv_out, pool_w=pool_w, pool_scale=pool_scale, w_pool_out=w_pool_out, w_o=w_o,
             mix_post_g=mix_post_g, mlp_pre_g=mlp_pre_g, w_ff1=w_ff1, w_ff2=w_ff2, mlp_post_g=mlp_post_g)
    m = dict(mix_pre_g=m_mix_pre_g, w_in=m_w_in, dw_kernel=m_dw_kernel, dw_bias=m_dw_bias, conv_ln_g=m_conv_ln_g,
             conv_ln_b=m_conv_ln_b, w_conv_out=m_w_conv_out, pool_w=m_pool_w, pool_scale=m_pool_scale,
             w_pool_out=m_w_pool_out, w_o=m_w_o, mix_post_g=m_mix_post_g, mlp_pre_g=m_mlp_pre_g, w_ff1=m_w_ff1,
             w_ff2=m_w_ff2, mlp_post_g=m_mlp_post_g)
    v = dict(mix_pre_g=v_mix_pre_g, w_in=v_w_in, dw_kernel=v_dw_kernel, dw_bias=v_dw_bias, conv_ln_g=v_conv_ln_g,
             conv_ln_b=v_conv_ln_b, w_conv_out=v_w_conv_out, pool_w=v_pool_w, pool_scale=v_pool_scale,
             w_pool_out=v_w_pool_out, w_o=v_w_o, mix_post_g=v_mix_post_g, mlp_pre_g=v_mlp_pre_g, w_ff1=v_w_ff1,
             w_ff2=v_w_ff2, mlp_post_g=v_mlp_post_g)
    seq = x.shape[1]
    loss, grad_x, out = _step(x.reshape(seq, D), loss_target.reshape(seq, D), w, m, v)
    grads, deltas, new_m, new_v = ([out[n][k] for n in WEIGHT_ORDER] for k in range(4))
    return (loss, grad_x.reshape(x.shape), *grads, *deltas, *new_m, *new_v)
```

```python
import collections

import jax
import jax.numpy as jnp
from jax import lax
from jax.experimental import pallas as pl
from jax.experimental.pallas import tpu as pltpu

D = 1024
D_FF = 4 * D
IN_COLS = 5 * D
TAPS = 31
HALO = 32
POOL_WINDOWS = (2, 4, 8, 16)
PG = D // 4
N_DEV = 8
RMS_EPS = 1e-6
LN_EPS = 1e-5
ADAM_LR, ADAM_B1, ADAM_B2, ADAM_EPS, ADAM_WD, ADAM_STEP = 0.001, 0.9, 0.999, 1e-08, 0.01, 10

BF16 = jnp.bfloat16
F32 = jnp.float32
MIB = 1 << 20
MESH = pl.DeviceIdType.MESH


def _params(sem, vmem_mib):
    return pltpu.CompilerParams(dimension_semantics=sem, vmem_limit_bytes=vmem_mib * MIB)


def _dot(a, b):
    return jnp.dot(a, b, preferred_element_type=F32)


def _dot_nt(a, b):
    return lax.dot_general(a, b, (((1,), (1,)), ((), ())), preferred_element_type=F32)


def _dot_tn(a, b):
    return lax.dot_general(a, b, (((0,), (0,)), ((), ())), preferred_element_type=F32)


def _rms_scale(x):
    return lax.rsqrt(jnp.mean(x * x, axis=-1, keepdims=True) + RMS_EPS)


def _rms_bwd(dy, x, g):
    xn = x * _rms_scale(x)
    dn = dy * g
    dx = _rms_scale(x) * (dn - xn * jnp.mean(dn * xn, axis=-1, keepdims=True))
    return dx, jnp.sum(dy * xn, axis=0, keepdims=True)


def _sigmoid(x):
    return jax.nn.sigmoid(x)


def _acc_out(ref, first, value):
    @pl.when(first)
    def _():
        ref[...] = value

    @pl.when(jnp.logical_not(first))
    def _():
        ref[...] += value


def _my_slot():
    return 4 * lax.axis_index("x") + 2 * lax.axis_index("y") + lax.axis_index("c")


def _peer(mask):
    x, y, c = lax.axis_index("x"), lax.axis_index("y"), lax.axis_index("c")
    return (x ^ ((mask >> 2) & 1), y ^ ((mask >> 1) & 1), c ^ (mask & 1))


def _cols(width):
    return lambda ref, slot: ref.at[:, pl.ds(pl.multiple_of(slot * width, 128), width)]


def _rows(height):
    return lambda ref, slot: ref.at[pl.ds(pl.multiple_of(slot * height, 8), height), :]


def _lead(ref, slot):
    return ref.at[slot]


def _pool_rows(ref, slot):
    return ref.at[:, pl.ds(pl.multiple_of(slot * (PG // N_DEV), 8), PG // N_DEV), :]


SHARDED = (
    ("w_in", (D, IN_COLS), (D, IN_COLS // N_DEV), _cols(IN_COLS // N_DEV)),
    ("w_ff1", (D, D_FF), (D, D_FF // N_DEV), _cols(D_FF // N_DEV)),
    ("w_ff2", (D_FF, D), (D_FF // N_DEV, D), _rows(D_FF // N_DEV)),
    ("w_conv_out", (D, D), (D // N_DEV, D), _rows(D // N_DEV)),
    ("w_pool_out", (D, D), (D // N_DEV, D), _rows(D // N_DEV)),
    ("w_o", (D, D), (D // N_DEV, D), _rows(D // N_DEV)),
    ("pool_w", (4, PG, PG), (4, PG // N_DEV, PG), _pool_rows),
    ("dw_kernel", (N_DEV, TAPS, D // N_DEV), (TAPS, D // N_DEV), _lead),
)
N_SHARDED = len(SHARDED)


SHARD_AT = {name: at for name, _, _, at in SHARDED}
HBM = pl.BlockSpec(memory_space=pltpu.HBM)
SEM = pl.BlockSpec(memory_space=pltpu.SEMAPHORE)
ANY = pl.BlockSpec(memory_space=pl.ANY)
EFFECT = pltpu.SideEffectType.DATAFLOW_SIDE_EFFECTING


def _in_hbm(a):
    return pltpu.with_memory_space_constraint(a, pltpu.HBM)


def _pinned(*arrays):
    return [_in_hbm(a) for a in arrays]


def _stage_shards(shards):
    dtypes = [F32 if name == "dw_kernel" else BF16 for name, *_ in SHARDED]

    def body(*refs):
        ins = refs[:N_SHARDED]
        fulls = refs[N_SHARDED:2 * N_SHARDED]
        raw = refs[2 * N_SHARDED:3 * N_SHARDED]
        stage = refs[3 * N_SHARDED:4 * N_SHARDED]
        in_sems, out_sems = refs[4 * N_SHARDED:]
        me = _my_slot()
        loads = [pltpu.make_async_copy(ins[a], raw[a], in_sems.at[a]) for a in range(N_SHARDED)]
        for cp in loads:
            cp.start()
        stores = []
        for a, (_, _, _, at) in enumerate(SHARDED):
            loads[a].wait()
            stage[a][...] = raw[a][...].astype(dtypes[a])
            cp = pltpu.make_async_copy(stage[a], at(fulls[a], me), out_sems.at[a])
            cp.start()
            stores.append(cp)
        for cp in stores:
            cp.wait()

    return pl.pallas_call(
        body, name="stage_shards",
        out_shape=[pltpu.HBM(full, dt) for (_, full, _, _), dt in zip(SHARDED, dtypes)],
        in_specs=[pl.BlockSpec(memory_space=pl.ANY)] * N_SHARDED,
        out_specs=[pl.BlockSpec(memory_space=pl.ANY)] * N_SHARDED,
        scratch_shapes=[pltpu.VMEM(shard, F32) for _, _, shard, _ in SHARDED]
        + [pltpu.VMEM(shard, dt) for (_, _, shard, _), dt in zip(SHARDED, dtypes)]
        + [pltpu.SemaphoreType.DMA((N_SHARDED,)), pltpu.SemaphoreType.DMA((N_SHARDED,))],
        compiler_params=pltpu.CompilerParams(vmem_limit_bytes=40 * MIB),
    )(*_pinned(*shards))


Leg = collections.namedtuple("Leg", "routes src_of dst_of src_is_land")


def _sem_index(k, m):
    return k * (N_DEV - 1) + m - 1


def _exchange_start(name, groups, legs):
    sizes = [len(g[0]) for g in groups]
    names = [nm for g in groups for nm in g[0]]
    srcs = [s for g in groups if g[1] is not None for s in g[1]]
    lands = [l for g in groups for l in g[2]]
    n_src, n, n_g, n_l = len(srcs), len(names), len(groups), len(legs)

    def body(*refs):
        src_refs, land_refs = list(refs[:n_src]), refs[n_src:n_src + n]
        sems = refs[n_src + n:n_src + n + 2 * n_g * n_l]
        token = refs[-1]
        me = _my_slot()
        first = 0
        for g, size in enumerate(sizes):
            own_src = [src_refs.pop(0) for _ in range(size)] if groups[g][1] is not None else None
            for li, leg in enumerate(legs):
                send, recv = sems[2 * (g * n_l + li)], sems[2 * (g * n_l + li) + 1]
                for m, via in leg.routes:
                    for k in range(size):
                        land = land_refs[first + k]
                        src = land if leg.src_is_land else own_src[k]
                        pltpu.make_async_remote_copy(
                            src_ref=leg.src_of(names[first + k], src, me, m), dst_ref=leg.dst_of(names[first + k], land, me, m),
                            send_sem=send.at[_sem_index(k, m)], recv_sem=recv.at[_sem_index(k, m)],
                            device_id=_peer(via), device_id_type=MESH).start()
            first += size
        token[...] = jnp.zeros_like(token)

    sem_shapes = [pltpu.SemaphoreType.DMA((size * (N_DEV - 1),)) for size in sizes for _ in range(2 * n_l)]
    n_sem = len(sem_shapes)
    outs = pl.pallas_call(
        body, name=name,
        out_shape=sem_shapes + [pltpu.HBM(a.shape, a.dtype) for a in srcs + lands] + [jax.ShapeDtypeStruct((8, 128), F32)],
        in_specs=[HBM] * (n_src + n),
        out_specs=[SEM] * n_sem + [HBM] * (n_src + n) + [pl.BlockSpec(memory_space=pltpu.VMEM)],
        input_output_aliases={k: n_sem + k for k in range(n_src + n)},
        compiler_params=pltpu.CompilerParams(has_side_effects=EFFECT),
    )(*[_in_hbm(a) for a in srcs + lands])
    sems, thru, token = outs[:n_sem], list(outs[n_sem:-1]), outs[-1]
    src_thru, land_thru = thru[:n_src], thru[n_src:]
    handles, first = [], 0
    for g, size in enumerate(sizes):
        pairs = [(sems[2 * (g * n_l + li)], sems[2 * (g * n_l + li) + 1]) for li in range(n_l)]
        mine = [src_thru.pop(0) for _ in range(size)] if groups[g][1] is not None else None
        handles.append((groups[g][0], pairs, mine, land_thru[first:first + size]))
        first += size
    return handles, token


def _exchange_wait(name, names, srcs, lands, waits, after):
    n = len(names)
    n_src = n if srcs is not None else 0

    def body(*refs):
        src_refs, land_refs = refs[:n_src], refs[n_src:n_src + n]
        sems = refs[n_src + n:n_src + n + 2 * len(waits)]
        me = _my_slot()
        for wi, (_, leg) in enumerate(waits):
            for m, via in leg.routes:
                for k in range(n):
                    src = land_refs[k] if leg.src_is_land else src_refs[k]
                    cp = pltpu.make_async_remote_copy(
                        src_ref=leg.src_of(names[k], src, me, m), dst_ref=leg.dst_of(names[k], land_refs[k], me ^ via, m),
                        send_sem=sems[2 * wi].at[_sem_index(k, m)], recv_sem=sems[2 * wi + 1].at[_sem_index(k, m)],
                        device_id=_peer(via), device_id_type=MESH)
                    cp.wait_send()
                    cp.wait_recv()

    arrays = (list(srcs) if srcs is not None else []) + list(lands)
    outs = pl.pallas_call(
        body, name=name,
        out_shape=[pltpu.HBM(a.shape, a.dtype) for a in arrays],
        in_specs=[HBM] * len(arrays) + [SEM] * (2 * len(waits)) + [pl.BlockSpec(memory_space=pl.ANY)],
        out_specs=[HBM] * len(arrays),
        input_output_aliases={k: k for k in range(len(arrays))},
        compiler_params=pltpu.CompilerParams(has_side_effects=EFFECT),
    )(*arrays, *[s for pair, _ in waits for s in pair], after)
    return (outs[:n_src] if srcs is not None else None), outs[n_src:]


def _shard_slot(name, ref, slot):
    return SHARD_AT[name](ref, slot)


GATHER_ICI = Leg(((2, 2), (4, 4), (6, 6)), lambda name, ref, me, m: _shard_slot(name, ref, me),
                 lambda name, ref, sender, m: _shard_slot(name, ref, sender), True)
GATHER_D2D = Leg(((1, 1),), GATHER_ICI.src_of, GATHER_ICI.dst_of, True)
GATHER_FORWARD = Leg(((2, 1), (4, 1), (6, 1)), lambda name, ref, me, m: _shard_slot(name, ref, me ^ m),
                     lambda name, ref, sender, m: _shard_slot(name, ref, sender ^ m), True)
SCATTER = Leg(tuple((m, m) for m in range(1, N_DEV)), lambda name, ref, me, m: _shard_slot(name, ref, me ^ m),
              lambda name, ref, sender, m: ref.at[m - 1], False)


def _vec_exchange(vectors, after):
    n = len(vectors)

    def body(*refs):
        vec, vec_out = refs[n + 2], refs[n + 1]
        send_sems, recv_sems, local_sem = refs[n + 3:]
        for k in range(n):
            vec[k:k + 1, :] = refs[k][...]
        me = _my_slot()
        local = pltpu.make_async_copy(vec, vec_out.at[me], local_sem)
        local.start()
        sends = []
        for mask in range(1, N_DEV):
            cp = pltpu.make_async_remote_copy(
                src_ref=vec, dst_ref=vec_out.at[me], send_sem=send_sems.at[mask - 1],
                recv_sem=recv_sems.at[mask - 1], device_id=_peer(mask), device_id_type=MESH)
            cp.start()
            sends.append(cp)
        for mask in range(1, N_DEV):
            pltpu.make_async_remote_copy(
                src_ref=vec, dst_ref=vec_out.at[me ^ mask], send_sem=send_sems.at[mask - 1],
                recv_sem=recv_sems.at[mask - 1], device_id=_peer(mask), device_id_type=MESH).wait_recv()
        for cp in sends:
            cp.wait_send()
        local.wait()

    return pl.pallas_call(
        body, name="vec_exchange",
        out_shape=jax.ShapeDtypeStruct((N_DEV, n, D), F32),
        in_specs=[pl.BlockSpec(memory_space=pltpu.VMEM)] * n + [ANY],
        out_specs=pl.BlockSpec(memory_space=pl.ANY),
        scratch_shapes=[pltpu.VMEM((n, D), F32), pltpu.SemaphoreType.DMA((N_DEV - 1,)),
                        pltpu.SemaphoreType.DMA((N_DEV - 1,)), pltpu.SemaphoreType.DMA],
    )(*vectors, after)


def _adamw_update(g, w_ref, m_ref, v_ref, g_out, d_out, m_out, v_out):
    m_new = ADAM_B1 * m_ref[...] + (1.0 - ADAM_B1) * g
    v_new = ADAM_B2 * v_ref[...] + (1.0 - ADAM_B2) * (g * g)
    m_hat = m_new / (1.0 - ADAM_B1 ** ADAM_STEP)
    v_hat = v_new / (1.0 - ADAM_B2 ** ADAM_STEP)
    g_out[...] = g
    d_out[...] = -ADAM_LR * (m_hat / (jnp.sqrt(v_hat) + ADAM_EPS) + ADAM_WD * w_ref[...])
    m_out[...] = m_new
    v_out[...] = v_new


ADAMW_ROWS = 256


def _adamw(name, own, parts, w, m, v, slot):
    shard = w.shape
    if name in ("w_in", "w_ff1"):
        tr = ADAMW_ROWS
        grid = (shard[0] // tr,)
        own_spec = pl.BlockSpec((tr, shard[1]), lambda i, s: (i, s[0]))
        blk = pl.BlockSpec((tr, shard[1]), lambda i, s: (i, 0))
        parts_spec = pl.BlockSpec((N_DEV - 1, tr, shard[1]), lambda i, s: (0, i, 0))
    elif name == "pool_w":
        grid = (shard[0],)
        own_spec = pl.BlockSpec((None,) + shard[1:], lambda g, s: (g, s[0], 0))
        blk = pl.BlockSpec((None,) + shard[1:], lambda g, s: (g, 0, 0))
        parts_spec = pl.BlockSpec((N_DEV - 1, None) + shard[1:], lambda g, s: (0, g, 0, 0))
    elif name == "dw_kernel":
        grid = (1,)
        own_spec = pl.BlockSpec((None,) + shard, lambda i, s: (s[0], 0, 0))
        blk = pl.BlockSpec(shard, lambda i, s: (0, 0))
        parts_spec = pl.BlockSpec((N_DEV - 1,) + shard, lambda i, s: (0, 0, 0))
    else:
        tr = min(ADAMW_ROWS, shard[0])
        grid = (shard[0] // tr,)
        own_spec = pl.BlockSpec((tr, shard[1]), lambda i, s: (s[0] * grid[0] + i, 0))
        blk = pl.BlockSpec((tr, shard[1]), lambda i, s: (i, 0))
        parts_spec = pl.BlockSpec((N_DEV - 1, tr, shard[1]), lambda i, s: (0, i, 0))

    def body(slot_ref, own_ref, p_ref, w_ref, m_ref, v_ref, g_out, d_out, m_out, v_out):
        g = own_ref[...].astype(F32)
        for k in range(N_DEV - 1):
            g = g + p_ref[k].astype(F32)
        _adamw_update(g, w_ref, m_ref, v_ref, g_out, d_out, m_out, v_out)

    return pl.pallas_call(
        body, name="adamw_" + name,
        grid_spec=pltpu.PrefetchScalarGridSpec(
            num_scalar_prefetch=1, grid=grid, in_specs=[own_spec, parts_spec, blk, blk, blk], out_specs=[blk] * 4),
        out_shape=[jax.ShapeDtypeStruct(shard, F32)] * 4,
        compiler_params=_params(("arbitrary",), 32),
    )(slot, *_pinned(own, parts, w, m, v))


def _adamw_vectors(parts, w, m, v):
    n_vec = w.shape[0]

    def body(p_ref, w_ref, m_ref, v_ref, g_out, d_out, m_out, v_out, loss_out):
        total = p_ref[0]
        for s in range(1, N_DEV):
            total = total + p_ref[s]
        _adamw_update(total[0:n_vec], w_ref, m_ref, v_ref, g_out, d_out, m_out, v_out)
        loss_out[...] = total[n_vec:n_vec + 1] * (0.5 / D)

    return pl.pallas_call(
        body, name="adamw_vectors",
        out_shape=[jax.ShapeDtypeStruct(w.shape, F32)] * 4 + [jax.ShapeDtypeStruct((1, D), F32)],
    )(parts, w, m, v)


def _in_proj_fwd(x, g_pre, w_in, token):
    seq = x.shape[0]
    tm, tn = 1024, IN_COLS // 4

    def body(x_ref, g_ref, w_ref, _token, proj_ref, u_ref):
        @pl.when(pl.program_id(1) == 0)
        def _():
            xf = x_ref[...]
            u_ref[...] = (xf * _rms_scale(xf) * g_ref[...]).astype(BF16)
        proj_ref[...] = _dot(u_ref[...], w_ref[...]).astype(BF16)

    return pl.pallas_call(
        body, name="in_proj_fwd", grid=(seq // tm, IN_COLS // tn),
        in_specs=[pl.BlockSpec((tm, D), lambda i, j: (i, 0)), pl.BlockSpec((1, D), lambda i, j: (0, 0)),
                  pl.BlockSpec((D, tn), lambda i, j: (0, j)), ANY],
        out_specs=[pl.BlockSpec((tm, tn), lambda i, j: (i, j)), pl.BlockSpec((tm, D), lambda i, j: (i, 0))],
        out_shape=[pltpu.HBM((seq, IN_COLS), BF16), pltpu.HBM((seq, D), BF16)],
        compiler_params=_params(("arbitrary", "arbitrary"), 40),
    )(*_pinned(x, g_pre, w_in, token))


CONV_TM = 256
CONV_RS = 128


def _shifts():
    return [(b, [(a, 8 * a + b) for a in range(4) if 8 * a + b < TAPS]) for b in range(8)]


def _taps_looking_back(buf, row0, lanes, weight):
    acc = None
    for b, group in _shifts():
        part = None
        for a, s in group:
            term = weight(TAPS - 1 - s) * buf[pl.ds(row0 - 8 - 8 * a, CONV_RS + 8), lanes]
            part = term if part is None else part + term
        if b:
            part = pltpu.roll(part, b, 0)
        acc = part[8:, :] if acc is None else acc + part[8:, :]
    return acc


def _taps_looking_ahead(buf, row0, lanes, weight):
    acc = None
    for b, group in _shifts():
        part = None
        for a, s in group:
            term = weight(TAPS - 1 - s) * buf[pl.ds(row0 + 8 * a, CONV_RS + 8), lanes]
            part = term if part is None else part + term
        if b:
            part = pltpu.roll(part, CONV_RS + 8 - b, 0)
        acc = part[:CONV_RS, :] if acc is None else acc + part[:CONV_RS, :]
    return acc


def _layer_norm_parts(cv):
    mu = jnp.mean(cv, axis=-1, keepdims=True)
    cen = cv - mu
    rstd = lax.rsqrt(jnp.mean(cen * cen, axis=-1, keepdims=True) + LN_EPS)
    return cen * rstd, rstd


def _conv_fwd(proj, dw, dw_bias, ln_g, ln_b, w_conv_out, token):
    seq = proj.shape[0]
    tm = CONV_TM

    def body(a_ref, gate_ref, dw_ref, bias_ref, lg_ref, lb_ref, w_ref, _token, cv_ref, y_ref, ug):
        i = pl.program_id(0)

        @pl.when(i == 0)
        def _():
            ug[0:HALO, :] = jnp.zeros((HALO, D), F32)

        @pl.when(i > 0)
        def _():
            ug[0:HALO, :] = ug[tm:tm + HALO, :]
        ug[HALO:HALO + tm, :] = a_ref[...].astype(F32) * _sigmoid(gate_ref[...].astype(F32))

        def channel_block(cb, carry):
            lanes = pl.ds(pl.multiple_of(cb * 128, 128), 128)
            for r0 in range(0, tm, CONV_RS):
                taps = _taps_looking_back(ug, HALO + r0, lanes, lambda k: dw_ref[cb, k:k + 1, :])
                cv_ref[pl.ds(r0, CONV_RS), lanes] = taps + bias_ref[:, lanes]
            return carry
        lax.fori_loop(0, D // 128, channel_block, 0)

        n, _ = _layer_norm_parts(cv_ref[...])
        ln = n * lg_ref[...] + lb_ref[...]
        y_ref[...] = _dot((ln * _sigmoid(ln)).astype(BF16), w_ref[...]).astype(BF16)

    vec = pl.BlockSpec((1, D), lambda i: (0, 0))
    tile = pl.BlockSpec((tm, D), lambda i: (i, 0))
    return pl.pallas_call(
        body, name="conv_fwd", grid=(seq // tm,),
        in_specs=[pl.BlockSpec((tm, D), lambda i: (i, 0)), pl.BlockSpec((tm, D), lambda i: (i, 1)),
                  pl.BlockSpec((N_DEV, TAPS, 128), lambda i: (0, 0, 0)), vec, vec, vec,
                  pl.BlockSpec((D, D), lambda i: (0, 0)), ANY],
        out_specs=[tile, tile],
        out_shape=[pltpu.HBM((seq, D), F32), pltpu.HBM((seq, D), BF16)],
        scratch_shapes=[pltpu.VMEM((HALO + tm, D), F32)],
        compiler_params=_params(("arbitrary",), 32),
    )(*_pinned(proj, proj, dw, dw_bias, ln_g, ln_b, w_conv_out, token))


def _window_sums(rows, window, back):
    n = rows.shape[0]
    span = 1
    while span < window:
        rows = rows + pltpu.roll(rows, span if back else n - span, 0)
        span *= 2
    return rows


def _pool_counts(tile_index, tm, window):
    t = tile_index * tm + lax.broadcasted_iota(jnp.int32, (tm, 1), 0)
    return 1.0 / jnp.minimum(t + 1, window).astype(F32)


def _pool_merge_fwd(proj, y_conv, x, pool_w, pool_scale, w_pool_out, w_o, g_post, token):
    seq = proj.shape[0]
    tm = CONV_TM

    def body(p_ref, gc_ref, gp_ref, yc_ref, x_ref, pw_ref, ps_ref, wpo_ref, wo_ref, g_ref, _token,
             z_ref, zl_ref, yp_ref, mg_ref, o_ref, h1_ref, pbuf):
        i = pl.program_id(0)

        @pl.when(i == 0)
        def _():
            pbuf[0:HALO, :] = jnp.zeros((HALO, D), F32)

        @pl.when(i > 0)
        def _():
            pbuf[0:HALO, :] = pbuf[tm:tm + HALO, :]
        pbuf[HALO:HALO + tm, :] = p_ref[...].astype(F32)

        for g, window in enumerate(POOL_WINDOWS):
            lanes = pl.ds(g * PG, PG)
            acc = _window_sums(pbuf[:, lanes], window, back=True)[HALO:, :]
            zg = acc * _pool_counts(i, tm, window) - pbuf[pl.ds(HALO, tm), lanes]
            z_ref[:, lanes] = zg.astype(BF16)
            zl_ref[:, lanes] = _dot(zg.astype(BF16), pw_ref[g])
        zl = zl_ref[...]
        y_pool = _dot((zl * ps_ref[...]).astype(BF16), wpo_ref[...])
        yp_ref[...] = y_pool.astype(BF16)
        merged = (_sigmoid(gc_ref[...].astype(F32)) * yc_ref[...].astype(F32)
                  + _sigmoid(gp_ref[...].astype(F32)) * y_pool).astype(BF16)
        mg_ref[...] = merged
        o = _dot(merged, wo_ref[...])
        o_ref[...] = o
        h1_ref[...] = x_ref[...] + o * _rms_scale(o) * g_ref[...]

    vec = pl.BlockSpec((1, D), lambda i: (0, 0))
    tile = pl.BlockSpec((tm, D), lambda i: (i, 0))
    mat = pl.BlockSpec((D, D), lambda i: (0, 0))
    return pl.pallas_call(
        body, name="pool_merge_fwd", grid=(seq // tm,),
        in_specs=[pl.BlockSpec((tm, D), lambda i: (i, 2)), pl.BlockSpec((tm, D), lambda i: (i, 3)),
                  pl.BlockSpec((tm, D), lambda i: (i, 4)), tile, tile,
                  pl.BlockSpec((4, PG, PG), lambda i: (0, 0, 0)), vec, mat, mat, vec, ANY],
        out_specs=[tile] * 6,
        out_shape=[pltpu.HBM((seq, D), dt) for dt in (BF16, F32, BF16, BF16, F32, F32)],
        scratch_shapes=[pltpu.VMEM((HALO + tm, D), F32)],
        compiler_params=_params(("arbitrary",), 48),
    )(*_pinned(proj, proj, proj, y_conv, x, pool_w, pool_scale, w_pool_out, w_o, g_post, token))


def _mlp_fwd(h1, g_pre, w_ff1, w_ff2, g_post, target):
    seq = h1.shape[0]
    tm, tf = 1024, D_FF // N_DEV
    n_f = D_FF // tf

    def body(h1_ref, gpre_ref, w1_ref, w2_ref, gpost_ref, tgt_ref, v_ref, dm_ref, dh2_ref, sse_ref, ggrad_ref, macc):
        i, j = pl.program_id(0), pl.program_id(1)

        @pl.when(j == 0)
        def _():
            h = h1_ref[...]
            v_ref[...] = (h * _rms_scale(h) * gpre_ref[...]).astype(BF16)
        f = jnp.maximum(_dot(v_ref[...], w1_ref[...]), 0.0)
        part = _dot((f * f).astype(BF16), w2_ref[...])

        @pl.when(j == 0)
        def _():
            macc[...] = part

        @pl.when(j > 0)
        def _():
            macc[...] += part

        @pl.when(j == n_f - 1)
        def _():
            mo = macc[...]
            err = h1_ref[...] + mo * _rms_scale(mo) * gpost_ref[...] - tgt_ref[...]
            dh2 = err * (1.0 / D)
            dh2_ref[...] = dh2
            dm, ggrad = _rms_bwd(dh2, mo, gpost_ref[...])
            dm_ref[...] = dm.astype(BF16)
            _acc_out(ggrad_ref, i == 0, ggrad)
            _acc_out(sse_ref, i == 0, jnp.sum(jnp.sum(err * err, axis=1, keepdims=True), axis=0, keepdims=True))

    vec = pl.BlockSpec((1, D), lambda i, j: (0, 0))
    tile = pl.BlockSpec((tm, D), lambda i, j: (i, 0))
    return pl.pallas_call(
        body, name="mlp_fwd", grid=(seq // tm, n_f),
        in_specs=[tile, vec, pl.BlockSpec((D, tf), lambda i, j: (0, j)), pl.BlockSpec((tf, D), lambda i, j: (j, 0)), vec, tile],
        out_specs=[tile, tile, tile, pl.BlockSpec((1, 1), lambda i, j: (0, 0)), vec],
        out_shape=[pltpu.HBM((seq, D), BF16), pltpu.HBM((seq, D), BF16),
                   pltpu.HBM((seq, D), F32), jax.ShapeDtypeStruct((1, 1), F32),
                   jax.ShapeDtypeStruct((1, D), F32)],
        scratch_shapes=[pltpu.VMEM((tm, D), F32)],
        compiler_params=_params(("arbitrary", "arbitrary"), 56),
    )(*_pinned(h1, g_pre, w_ff1, w_ff2, g_post, target))


def _mlp_bwd(v, dm, w_ff1, w_ff2):
    seq = v.shape[0]
    tm, tf = 1024, D_FF // N_DEV
    n_t = seq // tm

    def body(v_ref, dm_ref, w1_ref, w2_ref, dv_hbm, g1_ref, g2_ref, dv_acc, g1_acc, g2_acc, sem):
        j, i = pl.program_id(0), pl.program_id(1)
        vt, dmt = v_ref[...], dm_ref[...]
        f = jnp.maximum(_dot(vt, w1_ref[...]), 0.0)
        df = (_dot_nt(dmt, w2_ref[...]) * (2.0 * f)).astype(BF16)
        rows = pl.ds(pl.multiple_of(i * tm, tm), tm)
        dv_part = _dot_nt(df, w1_ref[...])

        @pl.when(j == 0)
        def _():
            dv_acc[rows, :] = dv_part

        @pl.when(j > 0)
        def _():
            dv_acc[rows, :] += dv_part
        g1_part = _dot_tn(vt, df)
        g2_part = _dot_tn((f * f).astype(BF16), dmt)

        @pl.when(i == 0)
        def _():
            g1_acc[...] = g1_part
            g2_acc[...] = g2_part

        @pl.when(i > 0)
        def _():
            g1_acc[...] += g1_part
            g2_acc[...] += g2_part

        @pl.when(i == n_t - 1)
        def _():
            g1_ref[...] = g1_acc[...].astype(BF16)
            g2_ref[...] = g2_acc[...].astype(BF16)

        @pl.when(jnp.logical_and(i == n_t - 1, j == D_FF // tf - 1))
        def _():
            cp = pltpu.make_async_copy(dv_acc, dv_hbm, sem)
            cp.start()
            cp.wait()

    tile = pl.BlockSpec((tm, D), lambda j, i: (i, 0))
    return pl.pallas_call(
        body, name="mlp_bwd", grid=(D_FF // tf, n_t),
        in_specs=[tile, tile, pl.BlockSpec((D, tf), lambda j, i: (0, j)), pl.BlockSpec((tf, D), lambda j, i: (j, 0))],
        out_specs=[pl.BlockSpec(memory_space=pl.ANY), pl.BlockSpec((D, tf), lambda j, i: (0, j)),
                   pl.BlockSpec((tf, D), lambda j, i: (j, 0))],
        out_shape=[pltpu.HBM((seq, D), F32), pltpu.HBM((D, D_FF), BF16),
                   pltpu.HBM((D_FF, D), BF16)],
        scratch_shapes=[pltpu.VMEM((seq, D), F32), pltpu.VMEM((D, tf), F32), pltpu.VMEM((tf, D), F32),
                        pltpu.SemaphoreType.DMA],
        compiler_params=_params(("arbitrary", "arbitrary"), 52),
    )(*_pinned(v, dm, w_ff1, w_ff2))


def _merge_bwd(dh2, dv, h1, g_mlp_pre, o, g_mix_post, w_o, merged, proj, y_conv, y_pool, token):
    seq = dh2.shape[0]
    tm = 256
    n_t = seq // tm

    def body(dh2_ref, dv_ref, h1_ref, gpre_ref, o_ref, gpost_ref, wo_ref, mg_ref, gc_ref, gp_ref, yc_ref, yp_ref, _token,
             dh1_ref, dyc_ref, dyp_ref, dg_ref, gwo_ref, ggpre_ref, ggpost_ref, gwo_acc):
        i = pl.program_id(0)
        dnorm, ggpre = _rms_bwd(dv_ref[...], h1_ref[...], gpre_ref[...])
        dh1 = dh2_ref[...] + dnorm
        dh1_ref[...] = dh1
        do, ggpost = _rms_bwd(dh1, o_ref[...], gpost_ref[...])
        do = do.astype(BF16)
        _acc_out(ggpre_ref, i == 0, ggpre)
        _acc_out(ggpost_ref, i == 0, ggpost)
        _acc_out(gwo_acc, i == 0, _dot_tn(mg_ref[...], do))
        dmerged = _dot_nt(do, wo_ref[...])
        sc, sp = _sigmoid(gc_ref[...].astype(F32)), _sigmoid(gp_ref[...].astype(F32))
        dyc_ref[...] = (dmerged * sc).astype(BF16)
        dyp_ref[...] = (dmerged * sp).astype(BF16)
        dg_ref[:, 0:D] = (dmerged * yc_ref[...].astype(F32) * (sc * (1.0 - sc))).astype(BF16)
        dg_ref[:, D:2 * D] = (dmerged * yp_ref[...].astype(F32) * (sp * (1.0 - sp))).astype(BF16)

        @pl.when(i == n_t - 1)
        def _():
            gwo_ref[...] = gwo_acc[...].astype(BF16)

    vec = pl.BlockSpec((1, D), lambda i: (0, 0))
    tile = pl.BlockSpec((tm, D), lambda i: (i, 0))
    mat = pl.BlockSpec((D, D), lambda i: (0, 0))
    return pl.pallas_call(
        body, name="merge_bwd", grid=(n_t,),
        in_specs=[tile, tile, tile, vec, tile, vec, mat, tile,
                  pl.BlockSpec((tm, D), lambda i: (i, 3)), pl.BlockSpec((tm, D), lambda i: (i, 4)), tile, tile, ANY],
        out_specs=[tile, tile, tile, pl.BlockSpec((tm, 2 * D), lambda i: (i, 0)), mat, vec, vec],
        out_shape=[pltpu.HBM((seq, D), F32), pltpu.HBM((seq, D), BF16),
                   pltpu.HBM((seq, D), BF16), pltpu.HBM((seq, 2 * D), BF16),
                   pltpu.HBM((D, D), BF16), jax.ShapeDtypeStruct((1, D), F32),
                   jax.ShapeDtypeStruct((1, D), F32)],
        scratch_shapes=[pltpu.VMEM((D, D), F32)],
        compiler_params=_params(("arbitrary",), 48),
    )(*_pinned(dh2, dv, h1, g_mlp_pre, o, g_mix_post, w_o, merged, proj, proj, y_conv, y_pool, token))


def _pool_bwd(dy_pool, zl, z, pool_w, pool_scale, w_pool_out):
    seq = dy_pool.shape[0]
    tm = CONV_TM
    n_t = seq // tm

    def body(dy_ref, zl_ref, z_ref, pw_ref, ps_ref, wpo_ref, dp_ref, gwpo_ref, gpw_ref, gps_ref, qbuf, gwpo_acc, gpw_acc):
        i = pl.program_id(0)
        tile_index = n_t - 1 - i
        first = i == 0
        dy = dy_ref[...]
        zl = zl_ref[...]
        dzs = _dot_nt(dy, wpo_ref[...])
        _acc_out(gwpo_acc, first, _dot_tn((zl * ps_ref[...]).astype(BF16), dy))
        _acc_out(gps_ref, first, jnp.sum(dzs * zl, axis=0, keepdims=True))
        dzl = (dzs * ps_ref[...]).astype(BF16)

        @pl.when(first)
        def _():
            qbuf[tm:tm + HALO, :] = jnp.zeros((HALO, D), F32)

        @pl.when(jnp.logical_not(first))
        def _():
            qbuf[tm:tm + HALO, :] = qbuf[0:HALO, :]

        dzs_list = []
        for g, window in enumerate(POOL_WINDOWS):
            lanes = pl.ds(g * PG, PG)
            dzl_g = dzl[:, g * PG:(g + 1) * PG]
            dz = _dot_nt(dzl_g, pw_ref[g])
            _acc_out(gpw_acc.at[g], first, _dot_tn(z_ref[:, lanes], dzl_g))
            qbuf[pl.ds(0, tm), lanes] = dz * _pool_counts(tile_index, tm, window)
            dzs_list.append(dz)
        for g, window in enumerate(POOL_WINDOWS):
            lanes = pl.ds(g * PG, PG)
            acc = _window_sums(qbuf[:, lanes], window, back=False)[:tm, :]
            dp_ref[:, lanes] = (acc - dzs_list[g]).astype(BF16)

        @pl.when(i == n_t - 1)
        def _():
            gwpo_ref[...] = gwpo_acc[...].astype(BF16)
            gpw_ref[...] = gpw_acc[...].astype(BF16)

    vec = pl.BlockSpec((1, D), lambda i: (0, 0))
    tile = pl.BlockSpec((tm, D), lambda i: (n_t - 1 - i, 0))
    mat = pl.BlockSpec((D, D), lambda i: (0, 0))
    pw = pl.BlockSpec((4, PG, PG), lambda i: (0, 0, 0))
    return pl.pallas_call(
        body, name="pool_bwd", grid=(n_t,),
        in_specs=[tile, tile, tile, pw, vec, mat],
        out_specs=[tile, mat, pw, vec],
        out_shape=[pltpu.HBM((seq, D), BF16), pltpu.HBM((D, D), BF16),
                   pltpu.HBM((4, PG, PG), BF16), jax.ShapeDtypeStruct((1, D), F32)],
        scratch_shapes=[pltpu.VMEM((tm + HALO, D), F32), pltpu.VMEM((D, D), F32), pltpu.VMEM((4, PG, PG), F32)],
        compiler_params=_params(("arbitrary",), 40),
    )(*_pinned(dy_pool, zl, z, pool_w, pool_scale, w_pool_out))


def _conv_bwd(dy_conv, cv, proj, dw, ln_g, ln_b, w_conv_out):
    seq = dy_conv.shape[0]
    tm = CONV_TM
    n_t = seq // tm
    halo_blocks = tm // HALO

    def body(dy_ref, cv_ref, a_ref, gate_ref, ah_ref, gh_ref, dw_ref, lg_ref, lb_ref, w_ref,
             dglu_ref, gw_ref, gdw_ref, gbias_ref, glg_ref, glb_ref, ug, dcv, dug, gw_acc):
        i = pl.program_id(0)
        tile_index = n_t - 1 - i
        first = i == 0
        dy = dy_ref[...]
        n, rstd = _layer_norm_parts(cv_ref[...])
        ln = n * lg_ref[...] + lb_ref[...]
        sg = _sigmoid(ln)
        _acc_out(gw_acc, first, _dot_tn((ln * sg).astype(BF16), dy))
        dln = _dot_nt(dy, w_ref[...]) * (sg * (1.0 + ln * (1.0 - sg)))
        _acc_out(glg_ref, first, jnp.sum(dln * n, axis=0, keepdims=True))
        _acc_out(glb_ref, first, jnp.sum(dln, axis=0, keepdims=True))
        dn = dln * lg_ref[...]
        dcv_tile = rstd * (dn - jnp.mean(dn, axis=-1, keepdims=True) - n * jnp.mean(dn * n, axis=-1, keepdims=True))
        _acc_out(gbias_ref, first, jnp.sum(dcv_tile, axis=0, keepdims=True))

        @pl.when(first)
        def _():
            dcv[tm:tm + HALO, :] = jnp.zeros((HALO, D), F32)

        @pl.when(jnp.logical_not(first))
        def _():
            dcv[tm:tm + HALO, :] = dcv[0:HALO, :]
        dcv[0:tm, :] = dcv_tile

        a, gate = a_ref[...].astype(F32), gate_ref[...].astype(F32)
        sgate = _sigmoid(gate)
        ug[HALO:HALO + tm, :] = a * sgate
        before = jnp.where(tile_index > 0, 1.0, 0.0)
        ug[0:HALO, :] = ah_ref[...].astype(F32) * _sigmoid(gh_ref[...].astype(F32)) * before

        @pl.when(first)
        def _():
            gdw_ref[...] = jnp.zeros((N_DEV, TAPS + 1, 128), F32)

        def channel_block(cb, carry):
            lanes = pl.ds(pl.multiple_of(cb * 128, 128), 128)
            for r0 in range(0, tm, CONV_RS):
                dug[pl.ds(r0, CONV_RS), lanes] = _taps_looking_ahead(dcv, r0, lanes, lambda k: dw_ref[cb, k:k + 1, :])
            for b, group in _shifts():
                sums = [jnp.zeros((8, 128), F32) for _ in group]
                for r0 in range(0, tm, CONV_RS):
                    window = ug[pl.ds(r0, CONV_RS + HALO), lanes]
                    if b:
                        window = pltpu.roll(window, b, 0)
                    d = dcv[pl.ds(r0, CONV_RS), lanes]
                    for n_a, (a, s) in enumerate(group):
                        prod = d * window[HALO - 8 * a:HALO - 8 * a + CONV_RS, :]
                        sums[n_a] = sums[n_a] + jnp.sum(prod.reshape(CONV_RS // 8, 8, 128), axis=0)
                for n_a, (a, s) in enumerate(group):
                    k = TAPS - 1 - s
                    gdw_ref[cb, k:k + 1, :] += jnp.sum(sums[n_a], axis=0, keepdims=True)
            return carry
        lax.fori_loop(0, D // 128, channel_block, 0)

        d_ug = dug[...]
        dglu_ref[:, 0:D] = (d_ug * sgate).astype(BF16)
        dglu_ref[:, D:2 * D] = (d_ug * a * (sgate * (1.0 - sgate))).astype(BF16)

        @pl.when(i == n_t - 1)
        def _():
            gw_ref[...] = gw_acc[...].astype(BF16)

    def halo_index(col):
        return lambda i: (jnp.maximum((n_t - 1 - i) * halo_blocks - 1, 0), col)

    vec = pl.BlockSpec((1, D), lambda i: (0, 0))
    tile = pl.BlockSpec((tm, D), lambda i: (n_t - 1 - i, 0))
    mat = pl.BlockSpec((D, D), lambda i: (0, 0))
    dwspec = pl.BlockSpec((N_DEV, TAPS, 128), lambda i: (0, 0, 0))
    return pl.pallas_call(
        body, name="conv_bwd", grid=(n_t,),
        in_specs=[tile, tile, pl.BlockSpec((tm, D), lambda i: (n_t - 1 - i, 0)), pl.BlockSpec((tm, D), lambda i: (n_t - 1 - i, 1)),
                  pl.BlockSpec((HALO, D), halo_index(0)), pl.BlockSpec((HALO, D), halo_index(1)), dwspec, vec, vec, mat],
        out_specs=[pl.BlockSpec((tm, 2 * D), lambda i: (n_t - 1 - i, 0)), mat,
                   pl.BlockSpec((N_DEV, TAPS + 1, 128), lambda i: (0, 0, 0)), vec, vec, vec],
        out_shape=[pltpu.HBM((seq, 2 * D), BF16), pltpu.HBM((D, D), BF16),
                   pltpu.HBM((N_DEV, TAPS + 1, 128), F32), jax.ShapeDtypeStruct((1, D), F32),
                   jax.ShapeDtypeStruct((1, D), F32), jax.ShapeDtypeStruct((1, D), F32)],
        scratch_shapes=[pltpu.VMEM((HALO + tm, D), F32), pltpu.VMEM((tm + HALO, D), F32), pltpu.VMEM((tm, D), F32),
                        pltpu.VMEM((D, D), F32)],
        compiler_params=_params(("arbitrary",), 48),
    )(*_pinned(dy_conv, cv, proj, proj, proj, proj, dw, ln_g, ln_b, w_conv_out))


def _in_proj_bwd_x(d_glu, dp, dgates, w_in, x, g_pre, dh1, token):
    seq = x.shape[0]
    tm = 512

    def body(dglu_ref, dp_ref, dg_ref, w_ref, x_ref, g_ref, dh1_ref, _token, dx_ref, gg_ref):
        du = _dot_nt(dglu_ref[...], w_ref[:, 0:2 * D])
        du += _dot_nt(dp_ref[...], w_ref[:, 2 * D:3 * D])
        du += _dot_nt(dg_ref[...], w_ref[:, 3 * D:5 * D])
        dnorm, gg = _rms_bwd(du, x_ref[...], g_ref[...])
        dx_ref[...] = dh1_ref[...] + dnorm
        _acc_out(gg_ref, pl.program_id(0) == 0, gg)

    vec = pl.BlockSpec((1, D), lambda i: (0, 0))
    tile = pl.BlockSpec((tm, D), lambda i: (i, 0))
    wide = pl.BlockSpec((tm, 2 * D), lambda i: (i, 0))
    return pl.pallas_call(
        body, name="in_proj_bwd_x", grid=(seq // tm,),
        in_specs=[wide, tile, wide, pl.BlockSpec((D, IN_COLS), lambda i: (0, 0)), tile, vec, tile, ANY],
        out_specs=[tile, vec],
        out_shape=[pltpu.HBM((seq, D), F32), jax.ShapeDtypeStruct((1, D), F32)],
        compiler_params=_params(("arbitrary",), 48),
    )(*_pinned(d_glu, dp, dgates, w_in, x, g_pre, dh1, token))


def _in_proj_bwd_w(u, d_glu, dp, dgates, token):
    seq = u.shape[0]
    tm = 1024
    n_t = seq // tm

    def body(u_ref, dglu_ref, dp_ref, dg_ref, _token, out_ref, acc):
        b, i = pl.program_id(0), pl.program_id(1)
        ut = u_ref[...]

        def add(d_ref):
            _acc_out(acc, i == 0, _dot_tn(ut, d_ref[...]))

        pl.when(b < 2)(lambda: add(dglu_ref))
        pl.when(b == 2)(lambda: add(dp_ref))
        pl.when(b > 2)(lambda: add(dg_ref))

        @pl.when(i == n_t - 1)
        def _():
            out_ref[...] = acc[...].astype(BF16)

    return pl.pallas_call(
        body, name="in_proj_bwd_w", grid=(IN_COLS // D, n_t),
        in_specs=[pl.BlockSpec((tm, D), lambda b, i: (i, 0)),
                  pl.BlockSpec((tm, D), lambda b, i: (jnp.where(b < 2, i, 0), jnp.minimum(b, 1))),
                  pl.BlockSpec((tm, D), lambda b, i: (jnp.where(b == 2, i, 0), 0)),
                  pl.BlockSpec((tm, D), lambda b, i: (jnp.where(b > 2, i, 0), jnp.maximum(b - 3, 0))), ANY],
        out_specs=pl.BlockSpec((D, D), lambda b, i: (0, b)),
        out_shape=pltpu.HBM((D, IN_COLS), BF16),
        scratch_shapes=[pltpu.VMEM((D, D), F32)],
        compiler_params=_params(("arbitrary", "arbitrary"), 40),
    )(*_pinned(u, d_glu, dp, dgates, token))


VEC_NAMES = ("mix_pre_g", "dw_bias", "conv_ln_g", "conv_ln_b", "pool_scale", "mix_post_g", "mlp_pre_g", "mlp_post_g")
WEIGHT_ORDER = ("mix_pre_g", "w_in", "dw_kernel", "dw_bias", "conv_ln_g", "conv_ln_b", "w_conv_out", "pool_w",
                "pool_scale", "w_pool_out", "w_o", "mix_post_g", "mlp_pre_g", "w_ff1", "w_ff2", "mlp_post_g")


def _step(x, loss_target, w, m, v):
    row = lambda a: a.reshape(1, D)
    names = [s[0] for s in SHARDED]

    seeded = dict(zip(names, _stage_shards([w[n] for n in names])))
    gather_groups = (("w_in",), ("w_conv_out", "dw_kernel"), ("pool_w", "w_pool_out", "w_o"), ("w_ff1", "w_ff2"))
    first_level, token = _exchange_start(
        "gather_start", [(g, None, [seeded[n] for n in g]) for g in gather_groups], [GATHER_ICI, GATHER_D2D])
    forwarded, full = {}, {}

    def forward(k, after):
        group, (ici_sems, d2d_sems), _, landed = first_level[k]
        _, landed = _exchange_wait("gather_ici_" + group[0], group, None, landed, [(ici_sems, GATHER_ICI)], after)
        (second,), tok = _exchange_start("gather_forward_" + group[0], [(group, None, landed)], [GATHER_FORWARD])
        forwarded[k] = (second[3], [(d2d_sems, GATHER_D2D), (second[1][0], GATHER_FORWARD)])
        return tok

    def gathered(k, after):
        group = gather_groups[k]
        landed, waits = forwarded[k]
        _, arrays = _exchange_wait("gather_wait_" + group[0], group, None, landed, waits, after)
        full.update(zip(group, arrays))

    gathered(0, forward(0, token))
    tok = forward(1, full["w_in"])
    proj, u = _in_proj_fwd(x, row(w["mix_pre_g"]), full["w_in"], tok)
    gathered(1, proj)
    tok = forward(2, proj)
    cv, y_conv = _conv_fwd(proj, full["dw_kernel"], row(w["dw_bias"]), row(w["conv_ln_g"]), row(w["conv_ln_b"]),
                           full["w_conv_out"], tok)
    gathered(2, y_conv)
    tok = forward(3, y_conv)
    z, zl, y_pool, merged, o, h1 = _pool_merge_fwd(proj, y_conv, x, full["pool_w"], row(w["pool_scale"]),
                                                   full["w_pool_out"], full["w_o"], row(w["mix_post_g"]), tok)
    gathered(3, h1)
    vv, dm, dh2, sse, g_mlp_post = _mlp_fwd(h1, row(w["mlp_pre_g"]), full["w_ff1"], full["w_ff2"],
                                            row(w["mlp_post_g"]), loss_target)

    shard_of = {name: shard for name, _, shard, _ in SHARDED}

    def scatter_start(tag, group, grads):
        landings = [lax.empty((N_DEV - 1,) + shard_of[n], g.dtype) for n, g in zip(group, grads)]
        (handle,), tok = _exchange_start("scatter_start_" + tag, [(group, grads, landings)], [SCATTER])
        return handle, tok

    dv, g_ff1, g_ff2 = _mlp_bwd(vv, dm, full["w_ff1"], full["w_ff2"])
    h_ff, tok_ff = scatter_start("ff", ("w_ff1", "w_ff2"), [g_ff1, g_ff2])
    dh1, dy_conv, dy_pool, dgates, g_wo, g_mlp_pre, g_mix_post = _merge_bwd(
        dh2, dv, h1, row(w["mlp_pre_g"]), o, row(w["mix_post_g"]), full["w_o"], merged, proj, y_conv, y_pool, tok_ff)
    dp, g_wpo, g_pw, g_pool_scale = _pool_bwd(dy_pool, zl, z, full["pool_w"], row(w["pool_scale"]), full["w_pool_out"])
    d_glu, g_wco, g_dw, g_bias, g_ln_g, g_ln_b = _conv_bwd(dy_conv, cv, proj, full["dw_kernel"], row(w["conv_ln_g"]),
                                                            row(w["conv_ln_b"]), full["w_conv_out"])
    h_mix, tok_mix = scatter_start("mix", ("w_o", "w_pool_out", "pool_w", "w_conv_out", "dw_kernel"),
                                   [g_wo, g_wpo, g_pw, g_wco, g_dw[:, :TAPS, :]])
    g_win = _in_proj_bwd_w(u, d_glu, dp, dgates, tok_mix)
    h_in, tok_in = scatter_start("in", ("w_in",), [g_win])
    grad_x, g_mix_pre = _in_proj_bwd_x(d_glu, dp, dgates, full["w_in"], x, row(w["mix_pre_g"]), dh1, tok_in)

    out = {}
    slot = jnp.reshape(_my_slot(), (1,)).astype(jnp.int32)
    after = grad_x
    for tag, (group, (sems,), grads, landings) in (("ff", h_ff), ("mix", h_mix), ("in", h_in)):
        if tag == "in":
            vec_parts = after = _vec_exchange(
                [g_mix_pre, g_bias, g_ln_g, g_ln_b, g_pool_scale, g_mix_post, g_mlp_pre, g_mlp_post,
                 jnp.broadcast_to(sse, (1, D))], after)
        mine, landed = _exchange_wait("scatter_wait_" + tag, group, grads, landings, [(sems, SCATTER)], after)
        for name, own, parts in zip(group, mine, landed):
            out[name] = _adamw(name, own, parts, w[name], m[name], v[name], slot)
            after = out[name][0]
    stack = lambda d: jnp.stack([d[n] for n in VEC_NAMES], axis=0)
    *res, loss_row = _adamw_vectors(vec_parts, stack(w), stack(m), stack(v))
    for k, name in enumerate(VEC_NAMES):
        out[name] = [r[k] for r in res]
    return loss_row[0, 0], grad_x, out


def kernel(x, mix_pre_g, w_in, dw_kernel, dw_bias, conv_ln_g, conv_ln_b, w_conv_out, pool_w, pool_scale, w_pool_out, w_o, mix_post_g, mlp_pre_g, w_ff1, w_ff2, mlp_post_g, loss_target, m_mix_pre_g, m_w_in, m_dw_kernel, m_dw_bias, m_conv_ln_g, m_conv_ln_b, m_w_conv_out, m_pool_w, m_pool_scale, m_w_pool_out, m_w_o, m_mix_post_g, m_mlp_pre_g, m_w_ff1, m_w_ff2, m_mlp_post_g, v_mix_pre_g, v_w_in, v_dw_kernel, v_dw_bias, v_conv_ln_g, v_conv_ln_b, v_w_conv_out, v_pool_w, v_pool_scale, v_w_pool_out, v_w_o, v_mix_post_g, v_mlp_pre_g, v_w_ff1, v_w_ff2, v_mlp_post_g):
    w = dict(mix_pre_g=mix_pre_g, w_in=w_in, dw_kernel=dw_kernel, dw_bias=dw_bias, conv_ln_g=conv_ln_g, conv_ln_b=conv_ln_b,
             w_conv_out=w_conv_out, pool_w=pool_w, pool_scale=pool_scale, w_pool_out=w_pool_out, w_o=w_o,
             mix_post_g=mix_post_g, mlp_pre_g=mlp_pre_g, w_ff1=w_ff1, w_ff2=w_ff2, mlp_post_g=mlp_post_g)
    m = dict(mix_pre_g=m_mix_pre_g, w_in=m_w_in, dw_kernel=m_dw_kernel, dw_bias=m_dw_bias, conv_ln_g=m_conv_ln_g,
             conv_ln_b=m_conv_ln_b, w_conv_out=m_w_conv_out, pool_w=m_pool_w, pool_scale=m_pool_scale,
             w_pool_out=m_w_pool_out, w_o=m_w_o, mix_post_g=m_mix_post_g, mlp_pre_g=m_mlp_pre_g, w_ff1=m_w_ff1,
             w_ff2=m_w_ff2, mlp_post_g=m_mlp_post_g)
    v = dict(mix_pre_g=v_mix_pre_g, w_in=v_w_in, dw_kernel=v_dw_kernel, dw_bias=v_dw_bias, conv_ln_g=v_conv_ln_g,
             conv_ln_b=v_conv_ln_b, w_conv_out=v_w_conv_out, pool_w=v_pool_w, pool_scale=v_pool_scale,
             w_pool_out=v_w_pool_out, w_o=v_w_o, mix_post_g=v_mix_post_g, mlp_pre_g=v_mlp_pre_g, w_ff1=v_w_ff1,
             w_ff2=v_w_ff2, mlp_post_g=v_mlp_post_g)
    seq = x.shape[1]
    loss, grad_x, out = _step(x.reshape(seq, D), loss_target.reshape(seq, D), w, m, v)
    grads, deltas, new_m, new_v = ([out[n][k] for n in WEIGHT_ORDER] for k in range(4))
    return (loss, grad_x.reshape(x.shape), *grads, *deltas, *new_m, *new_v)
```

```python
import collections

import jax
import jax.numpy as jnp
from jax import lax
from jax.experimental import pallas as pl
from jax.experimental.pallas import tpu as pltpu

D = 1024
D_FF = 4 * D
IN_COLS = 5 * D
TAPS = 31
HALO = 32
POOL_WINDOWS = (2, 4, 8, 16)
PG = D // 4
N_DEV = 8
RMS_EPS = 1e-6
LN_EPS = 1e-5
ADAM_LR, ADAM_B1, ADAM_B2, ADAM_EPS, ADAM_WD, ADAM_STEP = 0.001, 0.9, 0.999, 1e-08, 0.01, 10

BF16 = jnp.bfloat16
F32 = jnp.float32
MIB = 1 << 20
MESH = pl.DeviceIdType.MESH


def _params(sem, vmem_mib):
    return pltpu.CompilerParams(dimension_semantics=sem, vmem_limit_bytes=vmem_mib * MIB)


def _dot(a, b):
    return jnp.dot(a, b, preferred_element_type=F32)


def _dot_nt(a, b):
    return lax.dot_general(a, b, (((1,), (1,)), ((), ())), preferred_element_type=F32)


def _dot_tn(a, b):
    return lax.dot_general(a, b, (((0,), (0,)), ((), ())), preferred_element_type=F32)


def _rms_scale(x):
    return lax.rsqrt(jnp.mean(x * x, axis=-1, keepdims=True) + RMS_EPS)


def _rms_bwd(dy, x, g):
    xn = x * _rms_scale(x)
    dn = dy * g
    dx = _rms_scale(x) * (dn - xn * jnp.mean(dn * xn, axis=-1, keepdims=True))
    return dx, jnp.sum(dy * xn, axis=0, keepdims=True)


def _sigmoid(x):
    return jax.nn.sigmoid(x)


def _acc_out(ref, first, value):
    @pl.when(first)
    def _():
        ref[...] = value

    @pl.when(jnp.logical_not(first))
    def _():
        ref[...] += value


def _my_slot():
    return 4 * lax.axis_index("x") + 2 * lax.axis_index("y") + lax.axis_index("c")


def _peer(mask):
    x, y, c = lax.axis_index("x"), lax.axis_index("y"), lax.axis_index("c")
    return (x ^ ((mask >> 2) & 1), y ^ ((mask >> 1) & 1), c ^ (mask & 1))


def _cols(width):
    return lambda ref, slot: ref.at[:, pl.ds(pl.multiple_of(slot * width, 128), width)]


def _rows(height):
    return lambda ref, slot: ref.at[pl.ds(pl.multiple_of(slot * height, 8), height), :]


def _lead(ref, slot):
    return ref.at[slot]


def _pool_rows(ref, slot):
    return ref.at[:, pl.ds(pl.multiple_of(slot * (PG // N_DEV), 8), PG // N_DEV), :]


SHARDED = (
    ("w_in", (D, IN_COLS), (D, IN_COLS // N_DEV), _cols(IN_COLS // N_DEV)),
    ("w_ff1", (D, D_FF), (D, D_FF // N_DEV), _cols(D_FF // N_DEV)),
    ("w_ff2", (D_FF, D), (D_FF // N_DEV, D), _rows(D_FF // N_DEV)),
    ("w_conv_out", (D, D), (D // N_DEV, D), _rows(D // N_DEV)),
    ("w_pool_out", (D, D), (D // N_DEV, D), _rows(D // N_DEV)),
    ("w_o", (D, D), (D // N_DEV, D), _rows(D // N_DEV)),
    ("pool_w", (4, PG, PG), (4, PG // N_DEV, PG), _pool_rows),
    ("dw_kernel", (N_DEV, TAPS, D // N_DEV), (TAPS, D // N_DEV), _lead),
)
N_SHARDED = len(SHARDED)


SHARD_AT = {name: at for name, _, _, at in SHARDED}
HBM = pl.BlockSpec(memory_space=pltpu.HBM)
SEM = pl.BlockSpec(memory_space=pltpu.SEMAPHORE)
ANY = pl.BlockSpec(memory_space=pl.ANY)
EFFECT = pltpu.SideEffectType.DATAFLOW_SIDE_EFFECTING


def _in_hbm(a):
    return pltpu.with_memory_space_constraint(a, pltpu.HBM)


def _pinned(*arrays):
    return [_in_hbm(a) for a in arrays]


def _stage_shards(shards):
    dtypes = [F32 if name == "dw_kernel" else BF16 for name, *_ in SHARDED]

    def body(*refs):
        ins = refs[:N_SHARDED]
        fulls = refs[N_SHARDED:2 * N_SHARDED]
        raw = refs[2 * N_SHARDED:3 * N_SHARDED]
        stage = refs[3 * N_SHARDED:4 * N_SHARDED]
        in_sems, out_sems = refs[4 * N_SHARDED:]
        me = _my_slot()
        loads = [pltpu.make_async_copy(ins[a], raw[a], in_sems.at[a]) for a in range(N_SHARDED)]
        for cp in loads:
            cp.start()
        stores = []
        for a, (_, _, _, at) in enumerate(SHARDED):
            loads[a].wait()
            stage[a][...] = raw[a][...].astype(dtypes[a])
            cp = pltpu.make_async_copy(stage[a], at(fulls[a], me), out_sems.at[a])
            cp.start()
            stores.append(cp)
        for cp in stores:
            cp.wait()

    return pl.pallas_call(
        body, name="stage_shards",
        out_shape=[pltpu.HBM(full, dt) for (_, full, _, _), dt in zip(SHARDED, dtypes)],
        in_specs=[pl.BlockSpec(memory_space=pl.ANY)] * N_SHARDED,
        out_specs=[pl.BlockSpec(memory_space=pl.ANY)] * N_SHARDED,
        scratch_shapes=[pltpu.VMEM(shard, F32) for _, _, shard, _ in SHARDED]
        + [pltpu.VMEM(shard, dt) for (_, _, shard, _), dt in zip(SHARDED, dtypes)]
        + [pltpu.SemaphoreType.DMA((N_SHARDED,)), pltpu.SemaphoreType.DMA((N_SHARDED,))],
        compiler_params=pltpu.CompilerParams(vmem_limit_bytes=40 * MIB),
    )(*_pinned(*shards))


Leg = collections.namedtuple("Leg", "routes src_of dst_of src_is_land")


def _sem_index(k, m):
    return k * (N_DEV - 1) + m - 1


def _exchange_start(name, groups, legs):
    sizes = [len(g[0]) for g in groups]
    names = [nm for g in groups for nm in g[0]]
    srcs = [s for g in groups if g[1] is not None for s in g[1]]
    lands = [l for g in groups for l in g[2]]
    n_src, n, n_g, n_l = len(srcs), len(names), len(groups), len(legs)

    def body(*refs):
        src_refs, land_refs = list(refs[:n_src]), refs[n_src:n_src + n]
        sems = refs[n_src + n:n_src + n + 2 * n_g * n_l]
        token = refs[-1]
        me = _my_slot()
        first = 0
        for g, size in enumerate(sizes):
            own_src = [src_refs.pop(0) for _ in range(size)] if groups[g][1] is not None else None
            for li, leg in enumerate(legs):
                send, recv = sems[2 * (g * n_l + li)], sems[2 * (g * n_l + li) + 1]
                for m, via in leg.routes:
                    for k in range(size):
                        land = land_refs[first + k]
                        src = land if leg.src_is_land else own_src[k]
                        pltpu.make_async_remote_copy(
                            src_ref=leg.src_of(names[first + k], src, me, m), dst_ref=leg.dst_of(names[first + k], land, me, m),
                            send_sem=send.at[_sem_index(k, m)], recv_sem=recv.at[_sem_index(k, m)],
                            device_id=_peer(via), device_id_type=MESH).start()
            first += size
        token[...] = jnp.zeros_like(token)

    sem_shapes = [pltpu.SemaphoreType.DMA((size * (N_DEV - 1),)) for size in sizes for _ in range(2 * n_l)]
    n_sem = len(sem_shapes)
    outs = pl.pallas_call(
        body, name=name,
        out_shape=sem_shapes + [pltpu.HBM(a.shape, a.dtype) for a in srcs + lands] + [jax.ShapeDtypeStruct((8, 128), F32)],
        in_specs=[HBM] * (n_src + n),
        out_specs=[SEM] * n_sem + [HBM] * (n_src + n) + [pl.BlockSpec(memory_space=pltpu.VMEM)],
        input_output_aliases={k: n_sem + k for k in range(n_src + n)},
        compiler_params=pltpu.CompilerParams(has_side_effects=EFFECT),
    )(*[_in_hbm(a) for a in srcs + lands])
    sems, thru, token = outs[:n_sem], list(outs[n_sem:-1]), outs[-1]
    src_thru, land_thru = thru[:n_src], thru[n_src:]
    handles, first = [], 0
    for g, size in enumerate(sizes):
        pairs = [(sems[2 * (g * n_l + li)], sems[2 * (g * n_l + li) + 1]) for li in range(n_l)]
        mine = [src_thru.pop(0) for _ in range(size)] if groups[g][1] is not None else None
        handles.append((groups[g][0], pairs, mine, land_thru[first:first + size]))
        first += size
    return handles, token


def _exchange_wait(name, names, srcs, lands, waits, after):
    n = len(names)
    n_src = n if srcs is not None else 0

    def body(*refs):
        src_refs, land_refs = refs[:n_src], refs[n_src:n_src + n]
        sems = refs[n_src + n:n_src + n + 2 * len(waits)]
        me = _my_slot()
        for wi, (_, leg) in enumerate(waits):
            for m, via in leg.routes:
                for k in range(n):
                    src = land_refs[k] if leg.src_is_land else src_refs[k]
                    cp = pltpu.make_async_remote_copy(
                        src_ref=leg.src_of(names[k], src, me, m), dst_ref=leg.dst_of(names[k], land_refs[k], me ^ via, m),
                        send_sem=sems[2 * wi].at[_sem_index(k, m)], recv_sem=sems[2 * wi + 1].at[_sem_index(k, m)],
                        device_id=_peer(via), device_id_type=MESH)
                    cp.wait_send()
                    cp.wait_recv()

    arrays = (list(srcs) if srcs is not None else []) + list(lands)
    outs = pl.pallas_call(
        body, name=name,
        out_shape=[pltpu.HBM(a.shape, a.dtype) for a in arrays],
        in_specs=[HBM] * len(arrays) + [SEM] * (2 * len(waits)) + [pl.BlockSpec(memory_space=pl.ANY)],
        out_specs=[HBM] * len(arrays),
        input_output_aliases={k: k for k in range(len(arrays))},
        compiler_params=pltpu.CompilerParams(has_side_effects=EFFECT),
    )(*arrays, *[s for pair, _ in waits for s in pair], after)
    return (outs[:n_src] if srcs is not None else None), outs[n_src:]


def _shard_slot(name, ref, slot):
    return SHARD_AT[name](ref, slot)


GATHER_ICI = Leg(((2, 2), (4, 4), (6, 6)), lambda name, ref, me, m: _shard_slot(name, ref, me),
                 lambda name, ref, sender, m: _shard_slot(name, ref, sender), True)
GATHER_D2D = Leg(((1, 1),), GATHER_ICI.src_of, GATHER_ICI.dst_of, True)
GATHER_FORWARD = Leg(((2, 1), (4, 1), (6, 1)), lambda name, ref, me, m: _shard_slot(name, ref, me ^ m),
                     lambda name, ref, sender, m: _shard_slot(name, ref, sender ^ m), True)
SCATTER = Leg(tuple((m, m) for m in range(1, N_DEV)), lambda name, ref, me, m: _shard_slot(name, ref, me ^ m),
              lambda name, ref, sender, m: ref.at[m - 1], False)


def _vec_exchange(vectors, after):
    n = len(vectors)

    def body(*refs):
        vec, vec_out = refs[n + 2], refs[n + 1]
        send_sems, recv_sems, local_sem = refs[n + 3:]
        for k in range(n):
            vec[k:k + 1, :] = refs[k][...]
        me = _my_slot()
        local = pltpu.make_async_copy(vec, vec_out.at[me], local_sem)
        local.start()
        sends = []
        for mask in range(1, N_DEV):
            cp = pltpu.make_async_remote_copy(
                src_ref=vec, dst_ref=vec_out.at[me], send_sem=send_sems.at[mask - 1],
                recv_sem=recv_sems.at[mask - 1], device_id=_peer(mask), device_id_type=MESH)
            cp.start()
            sends.append(cp)
        for mask in range(1, N_DEV):
            pltpu.make_async_remote_copy(
                src_ref=vec, dst_ref=vec_out.at[me ^ mask], send_sem=send_sems.at[mask - 1],
                recv_sem=recv_sems.at[mask - 1], device_id=_peer(mask), device_id_type=MESH).wait_recv()
        for cp in sends:
            cp.wait_send()
        local.wait()

    return pl.pallas_call(
        body, name="vec_exchange",
        out_shape=jax.ShapeDtypeStruct((N_DEV, n, D), F32),
        in_specs=[pl.BlockSpec(memory_space=pltpu.VMEM)] * n + [ANY],
        out_specs=pl.BlockSpec(memory_space=pl.ANY),
        scratch_shapes=[pltpu.VMEM((n, D), F32), pltpu.SemaphoreType.DMA((N_DEV - 1,)),
                        pltpu.SemaphoreType.DMA((N_DEV - 1,)), pltpu.SemaphoreType.DMA],
    )(*vectors, after)


def _adamw_update(g, w_ref, m_ref, v_ref, g_out, d_out, m_out, v_out):
    m_new = ADAM_B1 * m_ref[...] + (1.0 - ADAM_B1) * g
    v_new = ADAM_B2 * v_ref[...] + (1.0 - ADAM_B2) * (g * g)
    m_hat = m_new / (1.0 - ADAM_B1 ** ADAM_STEP)
    v_hat = v_new / (1.0 - ADAM_B2 ** ADAM_STEP)
    g_out[...] = g
    d_out[...] = -ADAM_LR * (m_hat / (jnp.sqrt(v_hat) + ADAM_EPS) + ADAM_WD * w_ref[...])
    m_out[...] = m_new
    v_out[...] = v_new


ADAMW_ROWS = 256


def _adamw(name, own, parts, w, m, v, slot, token):
    shard = w.shape
    if name in ("w_in", "w_ff1"):
        tr = ADAMW_ROWS
        grid = (shard[0] // tr,)
        own_spec = pl.BlockSpec((tr, shard[1]), lambda i, s: (i, s[0]))
        blk = pl.BlockSpec((tr, shard[1]), lambda i, s: (i, 0))
        parts_spec = pl.BlockSpec((N_DEV - 1, tr, shard[1]), lambda i, s: (0, i, 0))
    elif name == "pool_w":
        grid = (shard[0],)
        own_spec = pl.BlockSpec((None,) + shard[1:], lambda g, s: (g, s[0], 0))
        blk = pl.BlockSpec((None,) + shard[1:], lambda g, s: (g, 0, 0))
        parts_spec = pl.BlockSpec((N_DEV - 1, None) + shard[1:], lambda g, s: (0, g, 0, 0))
    elif name == "dw_kernel":
        grid = (1,)
        own_spec = pl.BlockSpec((None,) + shard, lambda i, s: (s[0], 0, 0))
        blk = pl.BlockSpec(shard, lambda i, s: (0, 0))
        parts_spec = pl.BlockSpec((N_DEV - 1,) + shard, lambda i, s: (0, 0, 0))
    else:
        tr = min(ADAMW_ROWS, shard[0])
        grid = (shard[0] // tr,)
        own_spec = pl.BlockSpec((tr, shard[1]), lambda i, s: (s[0] * grid[0] + i, 0))
        blk = pl.BlockSpec((tr, shard[1]), lambda i, s: (i, 0))
        parts_spec = pl.BlockSpec((N_DEV - 1, tr, shard[1]), lambda i, s: (0, i, 0))

    def body(slot_ref, own_ref, p_ref, w_ref, m_ref, v_ref, _token, g_out, d_out, m_out, v_out):
        g = own_ref[...].astype(F32)
        for k in range(N_DEV - 1):
            g = g + p_ref[k].astype(F32)
        _adamw_update(g, w_ref, m_ref, v_ref, g_out, d_out, m_out, v_out)

    return pl.pallas_call(
        body, name="adamw_" + name,
        grid_spec=pltpu.PrefetchScalarGridSpec(
            num_scalar_prefetch=1, grid=grid, in_specs=[own_spec, parts_spec, blk, blk, blk, ANY], out_specs=[blk] * 4),
        out_shape=[jax.ShapeDtypeStruct(shard, F32)] * 4,
        compiler_params=_params(("arbitrary",), 32),
    )(slot, *_pinned(own, parts, w, m, v, token))


def _adamw_vectors(parts, w, m, v):
    n_vec = w.shape[0]

    def body(p_ref, w_ref, m_ref, v_ref, g_out, d_out, m_out, v_out, loss_out):
        total = p_ref[0]
        for s in range(1, N_DEV):
            total = total + p_ref[s]
        _adamw_update(total[0:n_vec], w_ref, m_ref, v_ref, g_out, d_out, m_out, v_out)
        loss_out[...] = total[n_vec:n_vec + 1] * (0.5 / D)

    return pl.pallas_call(
        body, name="adamw_vectors",
        out_shape=[jax.ShapeDtypeStruct(w.shape, F32)] * 4 + [jax.ShapeDtypeStruct((1, D), F32)],
    )(parts, w, m, v)


def _in_proj_fwd(x, g_pre, w_in, token):
    seq = x.shape[0]
    tm, tn = 1024, IN_COLS // 4

    def body(x_ref, g_ref, w_ref, _token, proj_ref, u_ref):
        @pl.when(pl.program_id(1) == 0)
        def _():
            xf = x_ref[...]
            u_ref[...] = (xf * _rms_scale(xf) * g_ref[...]).astype(BF16)
        proj_ref[...] = _dot(u_ref[...], w_ref[...]).astype(BF16)

    return pl.pallas_call(
        body, name="in_proj_fwd", grid=(seq // tm, IN_COLS // tn),
        in_specs=[pl.BlockSpec((tm, D), lambda i, j: (i, 0)), pl.BlockSpec((1, D), lambda i, j: (0, 0)),
                  pl.BlockSpec((D, tn), lambda i, j: (0, j)), ANY],
        out_specs=[pl.BlockSpec((tm, tn), lambda i, j: (i, j)), pl.BlockSpec((tm, D), lambda i, j: (i, 0))],
        out_shape=[pltpu.HBM((seq, IN_COLS), BF16), pltpu.HBM((seq, D), BF16)],
        compiler_params=_params(("arbitrary", "arbitrary"), 40),
    )(*_pinned(x, g_pre, w_in, token))


CONV_TM = 512
CONV_RS = 128


def _shifts():
    return [(b, [(a, 8 * a + b) for a in range(4) if 8 * a + b < TAPS]) for b in range(8)]


def _taps_looking_back(buf, row0, lanes, weight):
    acc = None
    for b, group in _shifts():
        part = None
        for a, s in group:
            term = weight(TAPS - 1 - s) * buf[pl.ds(row0 - 8 - 8 * a, CONV_RS + 8), lanes]
            part = term if part is None else part + term
        if b:
            part = pltpu.roll(part, b, 0)
        acc = part[8:, :] if acc is None else acc + part[8:, :]
    return acc


def _taps_looking_ahead(buf, row0, lanes, weight):
    acc = None
    for b, group in _shifts():
        part = None
        for a, s in group:
            term = weight(TAPS - 1 - s) * buf[pl.ds(row0 + 8 * a, CONV_RS + 8), lanes]
            part = term if part is None else part + term
        if b:
            part = pltpu.roll(part, CONV_RS + 8 - b, 0)
        acc = part[:CONV_RS, :] if acc is None else acc + part[:CONV_RS, :]
    return acc


def _layer_norm_parts(cv):
    mu = jnp.mean(cv, axis=-1, keepdims=True)
    cen = cv - mu
    rstd = lax.rsqrt(jnp.mean(cen * cen, axis=-1, keepdims=True) + LN_EPS)
    return cen * rstd, rstd


def _conv_fwd(proj, dw, dw_bias, ln_g, ln_b, w_conv_out, token):
    seq = proj.shape[0]
    tm = CONV_TM

    def body(a_ref, gate_ref, dw_ref, bias_ref, lg_ref, lb_ref, w_ref, _token, cv_ref, y_ref, ug):
        i = pl.program_id(0)

        @pl.when(i == 0)
        def _():
            ug[0:HALO, :] = jnp.zeros((HALO, D), F32)

        @pl.when(i > 0)
        def _():
            ug[0:HALO, :] = ug[tm:tm + HALO, :]
        ug[HALO:HALO + tm, :] = a_ref[...].astype(F32) * _sigmoid(gate_ref[...].astype(F32))

        def channel_block(cb, carry):
            lanes = pl.ds(pl.multiple_of(cb * 128, 128), 128)
            for r0 in range(0, tm, CONV_RS):
                taps = _taps_looking_back(ug, HALO + r0, lanes, lambda k: dw_ref[cb, k:k + 1, :])
                cv_ref[pl.ds(r0, CONV_RS), lanes] = taps + bias_ref[:, lanes]
            return carry
        lax.fori_loop(0, D // 128, channel_block, 0)

        n, _ = _layer_norm_parts(cv_ref[...])
        ln = n * lg_ref[...] + lb_ref[...]
        y_ref[...] = _dot((ln * _sigmoid(ln)).astype(BF16), w_ref[...]).astype(BF16)

    vec = pl.BlockSpec((1, D), lambda i: (0, 0))
    tile = pl.BlockSpec((tm, D), lambda i: (i, 0))
    return pl.pallas_call(
        body, name="conv_fwd", grid=(seq // tm,),
        in_specs=[pl.BlockSpec((tm, D), lambda i: (i, 0)), pl.BlockSpec((tm, D), lambda i: (i, 1)),
                  pl.BlockSpec((N_DEV, TAPS, 128), lambda i: (0, 0, 0)), vec, vec, vec,
                  pl.BlockSpec((D, D), lambda i: (0, 0)), ANY],
        out_specs=[tile, tile],
        out_shape=[pltpu.HBM((seq, D), F32), pltpu.HBM((seq, D), BF16)],
        scratch_shapes=[pltpu.VMEM((HALO + tm, D), F32)],
        compiler_params=_params(("arbitrary",), 32),
    )(*_pinned(proj, proj, dw, dw_bias, ln_g, ln_b, w_conv_out, token))


def _window_sums(rows, window, back):
    n = rows.shape[0]
    span = 1
    while span < window:
        rows = rows + pltpu.roll(rows, span if back else n - span, 0)
        span *= 2
    return rows


def _pool_counts(tile_index, tm, window):
    t = tile_index * tm + lax.broadcasted_iota(jnp.int32, (tm, 1), 0)
    return 1.0 / jnp.minimum(t + 1, window).astype(F32)


def _pool_merge_fwd(proj, y_conv, x, pool_w, pool_scale, w_pool_out, w_o, g_post, token):
    seq = proj.shape[0]
    tm = CONV_TM

    def body(p_ref, gc_ref, gp_ref, yc_ref, x_ref, pw_ref, ps_ref, wpo_ref, wo_ref, g_ref, _token,
             z_ref, zl_ref, yp_ref, mg_ref, o_ref, h1_ref, pbuf):
        i = pl.program_id(0)

        @pl.when(i == 0)
        def _():
            pbuf[0:HALO, :] = jnp.zeros((HALO, D), F32)

        @pl.when(i > 0)
        def _():
            pbuf[0:HALO, :] = pbuf[tm:tm + HALO, :]
        pbuf[HALO:HALO + tm, :] = p_ref[...].astype(F32)

        for g, window in enumerate(POOL_WINDOWS):
            lanes = pl.ds(g * PG, PG)
            acc = _window_sums(pbuf[:, lanes], window, back=True)[HALO:, :]
            zg = acc * _pool_counts(i, tm, window) - pbuf[pl.ds(HALO, tm), lanes]
            z_ref[:, lanes] = zg.astype(BF16)
            zl_ref[:, lanes] = _dot(zg.astype(BF16), pw_ref[g])
        zl = zl_ref[...]
        y_pool = _dot((zl * ps_ref[...]).astype(BF16), wpo_ref[...])
        yp_ref[...] = y_pool.astype(BF16)
        merged = (_sigmoid(gc_ref[...].astype(F32)) * yc_ref[...].astype(F32)
                  + _sigmoid(gp_ref[...].astype(F32)) * y_pool).astype(BF16)
        mg_ref[...] = merged
        o = _dot(merged, wo_ref[...])
        o_ref[...] = o
        h1_ref[...] = x_ref[...] + o * _rms_scale(o) * g_ref[...]

    vec = pl.BlockSpec((1, D), lambda i: (0, 0))
    tile = pl.BlockSpec((tm, D), lambda i: (i, 0))
    mat = pl.BlockSpec((D, D), lambda i: (0, 0))
    return pl.pallas_call(
        body, name="pool_merge_fwd", grid=(seq // tm,),
        in_specs=[pl.BlockSpec((tm, D), lambda i: (i, 2)), pl.BlockSpec((tm, D), lambda i: (i, 3)),
                  pl.BlockSpec((tm, D), lambda i: (i, 4)), tile, tile,
                  pl.BlockSpec((4, PG, PG), lambda i: (0, 0, 0)), vec, mat, mat, vec, ANY],
        out_specs=[tile] * 6,
        out_shape=[pltpu.HBM((seq, D), dt) for dt in (BF16, F32, BF16, BF16, F32, F32)],
        scratch_shapes=[pltpu.VMEM((HALO + tm, D), F32)],
        compiler_params=_params(("arbitrary",), 48),
    )(*_pinned(proj, proj, proj, y_conv, x, pool_w, pool_scale, w_pool_out, w_o, g_post, token))


def _mlp_fwd(h1, g_pre, w_ff1, w_ff2, g_post, target):
    seq = h1.shape[0]
    tm, tf = 1024, D_FF // N_DEV
    n_f = D_FF // tf

    def body(h1_ref, gpre_ref, w1_ref, w2_ref, gpost_ref, tgt_ref, v_ref, dm_ref, dh2_ref, sse_ref, ggrad_ref, macc):
        i, j = pl.program_id(0), pl.program_id(1)

        @pl.when(j == 0)
        def _():
            h = h1_ref[...]
            v_ref[...] = (h * _rms_scale(h) * gpre_ref[...]).astype(BF16)
        f = jnp.maximum(_dot(v_ref[...], w1_ref[...]), 0.0)
        part = _dot((f * f).astype(BF16), w2_ref[...])

        @pl.when(j == 0)
        def _():
            macc[...] = part

        @pl.when(j > 0)
        def _():
            macc[...] += part

        @pl.when(j == n_f - 1)
        def _():
            mo = macc[...]
            err = h1_ref[...] + mo * _rms_scale(mo) * gpost_ref[...] - tgt_ref[...]
            dh2 = err * (1.0 / D)
            dh2_ref[...] = dh2
            dm, ggrad = _rms_bwd(dh2, mo, gpost_ref[...])
            dm_ref[...] = dm.astype(BF16)
            _acc_out(ggrad_ref, i == 0, ggrad)
            _acc_out(sse_ref, i == 0, jnp.sum(jnp.sum(err * err, axis=1, keepdims=True), axis=0, keepdims=True))

    vec = pl.BlockSpec((1, D), lambda i, j: (0, 0))
    tile = pl.BlockSpec((tm, D), lambda i, j: (i, 0))
    return pl.pallas_call(
        body, name="mlp_fwd", grid=(seq // tm, n_f),
        in_specs=[tile, vec, pl.BlockSpec((D, tf), lambda i, j: (0, j)), pl.BlockSpec((tf, D), lambda i, j: (j, 0)), vec, tile],
        out_specs=[tile, tile, tile, pl.BlockSpec((1, 1), lambda i, j: (0, 0)), vec],
        out_shape=[pltpu.HBM((seq, D), BF16), pltpu.HBM((seq, D), BF16),
                   pltpu.HBM((seq, D), F32), jax.ShapeDtypeStruct((1, 1), F32),
                   jax.ShapeDtypeStruct((1, D), F32)],
        scratch_shapes=[pltpu.VMEM((tm, D), F32)],
        compiler_params=_params(("arbitrary", "arbitrary"), 56),
    )(*_pinned(h1, g_pre, w_ff1, w_ff2, g_post, target))


def _mlp_bwd(v, dm, w_ff1, w_ff2):
    seq = v.shape[0]
    tm, tf = 1024, D_FF // N_DEV
    n_t = seq // tm

    def body(v_ref, dm_ref, w1_ref, w2_ref, dv_hbm, g1_ref, g2_ref, dv_acc, g1_acc, g2_acc, sem):
        j, i = pl.program_id(0), pl.program_id(1)
        vt, dmt = v_ref[...], dm_ref[...]
        f = jnp.maximum(_dot(vt, w1_ref[...]), 0.0)
        df = (_dot_nt(dmt, w2_ref[...]) * (2.0 * f)).astype(BF16)
        rows = pl.ds(pl.multiple_of(i * tm, tm), tm)
        dv_part = _dot_nt(df, w1_ref[...])

        @pl.when(j == 0)
        def _():
            dv_acc[rows, :] = dv_part

        @pl.when(j > 0)
        def _():
            dv_acc[rows, :] += dv_part
        g1_part = _dot_tn(vt, df)
        g2_part = _dot_tn((f * f).astype(BF16), dmt)

        @pl.when(i == 0)
        def _():
            g1_acc[...] = g1_part
            g2_acc[...] = g2_part

        @pl.when(i > 0)
        def _():
            g1_acc[...] += g1_part
            g2_acc[...] += g2_part

        @pl.when(i == n_t - 1)
        def _():
            g1_ref[...] = g1_acc[...].astype(BF16)
            g2_ref[...] = g2_acc[...].astype(BF16)

        @pl.when(jnp.logical_and(i == n_t - 1, j == D_FF // tf - 1))
        def _():
            cp = pltpu.make_async_copy(dv_acc, dv_hbm, sem)
            cp.start()
            cp.wait()

    tile = pl.BlockSpec((tm, D), lambda j, i: (i, 0))
    return pl.pallas_call(
        body, name="mlp_bwd", grid=(D_FF // tf, n_t),
        in_specs=[tile, tile, pl.BlockSpec((D, tf), lambda j, i: (0, j)), pl.BlockSpec((tf, D), lambda j, i: (j, 0))],
        out_specs=[pl.BlockSpec(memory_space=pl.ANY), pl.BlockSpec((D, tf), lambda j, i: (0, j)),
                   pl.BlockSpec((tf, D), lambda j, i: (j, 0))],
        out_shape=[pltpu.HBM((seq, D), F32), pltpu.HBM((D, D_FF), BF16),
                   pltpu.HBM((D_FF, D), BF16)],
        scratch_shapes=[pltpu.VMEM((seq, D), F32), pltpu.VMEM((D, tf), F32), pltpu.VMEM((tf, D), F32),
                        pltpu.SemaphoreType.DMA],
        compiler_params=_params(("arbitrary", "arbitrary"), 52),
    )(*_pinned(v, dm, w_ff1, w_ff2))


def _merge_bwd(dh2, dv, h1, g_mlp_pre, o, g_mix_post, w_o, merged, proj, y_conv, y_pool, token):
    seq = dh2.shape[0]
    tm = 256
    n_t = seq // tm

    def body(dh2_ref, dv_ref, h1_ref, gpre_ref, o_ref, gpost_ref, wo_ref, mg_ref, gc_ref, gp_ref, yc_ref, yp_ref, _token,
             dh1_ref, dyc_ref, dyp_ref, dg_ref, gwo_ref, ggpre_ref, ggpost_ref, gwo_acc):
        i = pl.program_id(0)
        dnorm, ggpre = _rms_bwd(dv_ref[...], h1_ref[...], gpre_ref[...])
        dh1 = dh2_ref[...] + dnorm
        dh1_ref[...] = dh1
        do, ggpost = _rms_bwd(dh1, o_ref[...], gpost_ref[...])
        do = do.astype(BF16)
        _acc_out(ggpre_ref, i == 0, ggpre)
        _acc_out(ggpost_ref, i == 0, ggpost)
        _acc_out(gwo_acc, i == 0, _dot_tn(mg_ref[...], do))
        dmerged = _dot_nt(do, wo_ref[...])
        sc, sp = _sigmoid(gc_ref[...].astype(F32)), _sigmoid(gp_ref[...].astype(F32))
        dyc_ref[...] = (dmerged * sc).astype(BF16)
        dyp_ref[...] = (dmerged * sp).astype(BF16)
        dg_ref[:, 0:D] = (dmerged * yc_ref[...].astype(F32) * (sc * (1.0 - sc))).astype(BF16)
        dg_ref[:, D:2 * D] = (dmerged * yp_ref[...].astype(F32) * (sp * (1.0 - sp))).astype(BF16)

        @pl.when(i == n_t - 1)
        def _():
            gwo_ref[...] = gwo_acc[...].astype(BF16)

    vec = pl.BlockSpec((1, D), lambda i: (0, 0))
    tile = pl.BlockSpec((tm, D), lambda i: (i, 0))
    mat = pl.BlockSpec((D, D), lambda i: (0, 0))
    return pl.pallas_call(
        body, name="merge_bwd", grid=(n_t,),
        in_specs=[tile, tile, tile, vec, tile, vec, mat, tile,
                  pl.BlockSpec((tm, D), lambda i: (i, 3)), pl.BlockSpec((tm, D), lambda i: (i, 4)), tile, tile, ANY],
        out_specs=[tile, tile, tile, pl.BlockSpec((tm, 2 * D), lambda i: (i, 0)), mat, vec, vec],
        out_shape=[pltpu.HBM((seq, D), F32), pltpu.HBM((seq, D), BF16),
                   pltpu.HBM((seq, D), BF16), pltpu.HBM((seq, 2 * D), BF16),
                   pltpu.HBM((D, D), BF16), jax.ShapeDtypeStruct((1, D), F32),
                   jax.ShapeDtypeStruct((1, D), F32)],
        scratch_shapes=[pltpu.VMEM((D, D), F32)],
        compiler_params=_params(("arbitrary",), 48),
    )(*_pinned(dh2, dv, h1, g_mlp_pre, o, g_mix_post, w_o, merged, proj, proj, y_conv, y_pool, token))


def _pool_bwd(dy_pool, zl, z, pool_w, pool_scale, w_pool_out):
    seq = dy_pool.shape[0]
    tm = CONV_TM
    n_t = seq // tm

    def body(dy_ref, zl_ref, z_ref, pw_ref, ps_ref, wpo_ref, dp_ref, gwpo_ref, gpw_ref, gps_ref, qbuf, gwpo_acc, gpw_acc):
        i = pl.program_id(0)
        tile_index = n_t - 1 - i
        first = i == 0
        dy = dy_ref[...]
        zl = zl_ref[...]
        dzs = _dot_nt(dy, wpo_ref[...])
        _acc_out(gwpo_acc, first, _dot_tn((zl * ps_ref[...]).astype(BF16), dy))
        _acc_out(gps_ref, first, jnp.sum(dzs * zl, axis=0, keepdims=True))
        dzl = (dzs * ps_ref[...]).astype(BF16)

        @pl.when(first)
        def _():
            qbuf[tm:tm + HALO, :] = jnp.zeros((HALO, D), F32)

        @pl.when(jnp.logical_not(first))
        def _():
            qbuf[tm:tm + HALO, :] = qbuf[0:HALO, :]

        dzs_list = []
        for g, window in enumerate(POOL_WINDOWS):
            lanes = pl.ds(g * PG, PG)
            dzl_g = dzl[:, g * PG:(g + 1) * PG]
            dz = _dot_nt(dzl_g, pw_ref[g])
            _acc_out(gpw_acc.at[g], first, _dot_tn(z_ref[:, lanes], dzl_g))
            qbuf[pl.ds(0, tm), lanes] = dz * _pool_counts(tile_index, tm, window)
            dzs_list.append(dz)
        for g, window in enumerate(POOL_WINDOWS):
            lanes = pl.ds(g * PG, PG)
            acc = _window_sums(qbuf[:, lanes], window, back=False)[:tm, :]
            dp_ref[:, lanes] = (acc - dzs_list[g]).astype(BF16)

        @pl.when(i == n_t - 1)
        def _():
            gwpo_ref[...] = gwpo_acc[...].astype(BF16)
            gpw_ref[...] = gpw_acc[...].astype(BF16)

    vec = pl.BlockSpec((1, D), lambda i: (0, 0))
    tile = pl.BlockSpec((tm, D), lambda i: (n_t - 1 - i, 0))
    mat = pl.BlockSpec((D, D), lambda i: (0, 0))
    pw = pl.BlockSpec((4, PG, PG), lambda i: (0, 0, 0))
    return pl.pallas_call(
        body, name="pool_bwd", grid=(n_t,),
        in_specs=[tile, tile, tile, pw, vec, mat],
        out_specs=[tile, mat, pw, vec],
        out_shape=[pltpu.HBM((seq, D), BF16), pltpu.HBM((D, D), BF16),
                   pltpu.HBM((4, PG, PG), BF16), jax.ShapeDtypeStruct((1, D), F32)],
        scratch_shapes=[pltpu.VMEM((tm + HALO, D), F32), pltpu.VMEM((D, D), F32), pltpu.VMEM((4, PG, PG), F32)],
        compiler_params=_params(("arbitrary",), 40),
    )(*_pinned(dy_pool, zl, z, pool_w, pool_scale, w_pool_out))


def _conv_bwd(dy_conv, cv, proj, dw, ln_g, ln_b, w_conv_out):
    seq = dy_conv.shape[0]
    tm = CONV_TM
    n_t = seq // tm
    halo_blocks = tm // HALO

    def body(dy_ref, cv_ref, a_ref, gate_ref, ah_ref, gh_ref, dw_ref, lg_ref, lb_ref, w_ref,
             dglu_ref, gw_ref, gdw_ref, gbias_ref, glg_ref, glb_ref, ug, dcv, dug, gw_acc):
        i = pl.program_id(0)
        tile_index = n_t - 1 - i
        first = i == 0
        dy = dy_ref[...]
        n, rstd = _layer_norm_parts(cv_ref[...])
        ln = n * lg_ref[...] + lb_ref[...]
        sg = _sigmoid(ln)
        _acc_out(gw_acc, first, _dot_tn((ln * sg).astype(BF16), dy))
        dln = _dot_nt(dy, w_ref[...]) * (sg * (1.0 + ln * (1.0 - sg)))
        _acc_out(glg_ref, first, jnp.sum(dln * n, axis=0, keepdims=True))
        _acc_out(glb_ref, first, jnp.sum(dln, axis=0, keepdims=True))
        dn = dln * lg_ref[...]
        dcv_tile = rstd * (dn - jnp.mean(dn, axis=-1, keepdims=True) - n * jnp.mean(dn * n, axis=-1, keepdims=True))
        _acc_out(gbias_ref, first, jnp.sum(dcv_tile, axis=0, keepdims=True))

        @pl.when(first)
        def _():
            dcv[tm:tm + HALO, :] = jnp.zeros((HALO, D), F32)

        @pl.when(jnp.logical_not(first))
        def _():
            dcv[tm:tm + HALO, :] = dcv[0:HALO, :]
        dcv[0:tm, :] = dcv_tile

        a, gate = a_ref[...].astype(F32), gate_ref[...].astype(F32)
        sgate = _sigmoid(gate)
        ug[HALO:HALO + tm, :] = a * sgate
        before = jnp.where(tile_index > 0, 1.0, 0.0)
        ug[0:HALO, :] = ah_ref[...].astype(F32) * _sigmoid(gh_ref[...].astype(F32)) * before

        @pl.when(first)
        def _():
            gdw_ref[...] = jnp.zeros((N_DEV, TAPS + 1, 128), F32)

        def channel_block(cb, carry):
            lanes = pl.ds(pl.multiple_of(cb * 128, 128), 128)
            for r0 in range(0, tm, CONV_RS):
                dug[pl.ds(r0, CONV_RS), lanes] = _taps_looking_ahead(dcv, r0, lanes, lambda k: dw_ref[cb, k:k + 1, :])
            for b, group in _shifts():
                sums = [jnp.zeros((8, 128), F32) for _ in group]
                for r0 in range(0, tm, CONV_RS):
                    window = ug[pl.ds(r0, CONV_RS + HALO), lanes]
                    if b:
                        window = pltpu.roll(window, b, 0)
                    d = dcv[pl.ds(r0, CONV_RS), lanes]
                    for n_a, (a, s) in enumerate(group):
                        prod = d * window[HALO - 8 * a:HALO - 8 * a + CONV_RS, :]
                        sums[n_a] = sums[n_a] + jnp.sum(prod.reshape(CONV_RS // 8, 8, 128), axis=0)
                for n_a, (a, s) in enumerate(group):
                    k = TAPS - 1 - s
                    gdw_ref[cb, k:k + 1, :] += jnp.sum(sums[n_a], axis=0, keepdims=True)
            return carry
        lax.fori_loop(0, D // 128, channel_block, 0)

        d_ug = dug[...]
        dglu_ref[:, 0:D] = (d_ug * sgate).astype(BF16)
        dglu_ref[:, D:2 * D] = (d_ug * a * (sgate * (1.0 - sgate))).astype(BF16)

        @pl.when(i == n_t - 1)
        def _():
            gw_ref[...] = gw_acc[...].astype(BF16)

    def halo_index(col):
        return lambda i: (jnp.maximum((n_t - 1 - i) * halo_blocks - 1, 0), col)

    vec = pl.BlockSpec((1, D), lambda i: (0, 0))
    tile = pl.BlockSpec((tm, D), lambda i: (n_t - 1 - i, 0))
    mat = pl.BlockSpec((D, D), lambda i: (0, 0))
    dwspec = pl.BlockSpec((N_DEV, TAPS, 128), lambda i: (0, 0, 0))
    return pl.pallas_call(
        body, name="conv_bwd", grid=(n_t,),
        in_specs=[tile, tile, pl.BlockSpec((tm, D), lambda i: (n_t - 1 - i, 0)), pl.BlockSpec((tm, D), lambda i: (n_t - 1 - i, 1)),
                  pl.BlockSpec((HALO, D), halo_index(0)), pl.BlockSpec((HALO, D), halo_index(1)), dwspec, vec, vec, mat],
        out_specs=[pl.BlockSpec((tm, 2 * D), lambda i: (n_t - 1 - i, 0)), mat,
                   pl.BlockSpec((N_DEV, TAPS + 1, 128), lambda i: (0, 0, 0)), vec, vec, vec],
        out_shape=[pltpu.HBM((seq, 2 * D), BF16), pltpu.HBM((D, D), BF16),
                   pltpu.HBM((N_DEV, TAPS + 1, 128), F32), jax.ShapeDtypeStruct((1, D), F32),
                   jax.ShapeDtypeStruct((1, D), F32), jax.ShapeDtypeStruct((1, D), F32)],
        scratch_shapes=[pltpu.VMEM((HALO + tm, D), F32), pltpu.VMEM((tm + HALO, D), F32), pltpu.VMEM((tm, D), F32),
                        pltpu.VMEM((D, D), F32)],
        compiler_params=_params(("arbitrary",), 48),
    )(*_pinned(dy_conv, cv, proj, proj, proj, proj, dw, ln_g, ln_b, w_conv_out))


def _in_proj_bwd_x(d_glu, dp, dgates, w_in, x, g_pre, dh1, token):
    seq = x.shape[0]
    tm = 512

    def body(dglu_ref, dp_ref, dg_ref, w_ref, x_ref, g_ref, dh1_ref, _token, dx_ref, gg_ref):
        du = _dot_nt(dglu_ref[...], w_ref[:, 0:2 * D])
        du += _dot_nt(dp_ref[...], w_ref[:, 2 * D:3 * D])
        du += _dot_nt(dg_ref[...], w_ref[:, 3 * D:5 * D])
        dnorm, gg = _rms_bwd(du, x_ref[...], g_ref[...])
        dx_ref[...] = dh1_ref[...] + dnorm
        _acc_out(gg_ref, pl.program_id(0) == 0, gg)

    vec = pl.BlockSpec((1, D), lambda i: (0, 0))
    tile = pl.BlockSpec((tm, D), lambda i: (i, 0))
    wide = pl.BlockSpec((tm, 2 * D), lambda i: (i, 0))
    return pl.pallas_call(
        body, name="in_proj_bwd_x", grid=(seq // tm,),
        in_specs=[wide, tile, wide, pl.BlockSpec((D, IN_COLS), lambda i: (0, 0)), tile, vec, tile, ANY],
        out_specs=[tile, vec],
        out_shape=[pltpu.HBM((seq, D), F32), jax.ShapeDtypeStruct((1, D), F32)],
        compiler_params=_params(("arbitrary",), 48),
    )(*_pinned(d_glu, dp, dgates, w_in, x, g_pre, dh1, token))


def _in_proj_bwd_w(u, d_glu, dp, dgates, token):
    seq = u.shape[0]
    tm = 1024
    n_t = seq // tm

    def body(u_ref, dglu_ref, dp_ref, dg_ref, _token, out_ref, acc):
        b, i = pl.program_id(0), pl.program_id(1)
        ut = u_ref[...]

        def add(d_ref):
            _acc_out(acc, i == 0, _dot_tn(ut, d_ref[...]))

        pl.when(b < 2)(lambda: add(dglu_ref))
        pl.when(b == 2)(lambda: add(dp_ref))
        pl.when(b > 2)(lambda: add(dg_ref))

        @pl.when(i == n_t - 1)
        def _():
            out_ref[...] = acc[...].astype(BF16)

    return pl.pallas_call(
        body, name="in_proj_bwd_w", grid=(IN_COLS // D, n_t),
        in_specs=[pl.BlockSpec((tm, D), lambda b, i: (i, 0)),
                  pl.BlockSpec((tm, D), lambda b, i: (jnp.where(b < 2, i, 0), jnp.minimum(b, 1))),
                  pl.BlockSpec((tm, D), lambda b, i: (jnp.where(b == 2, i, 0), 0)),
                  pl.BlockSpec((tm, D), lambda b, i: (jnp.where(b > 2, i, 0), jnp.maximum(b - 3, 0))), ANY],
        out_specs=pl.BlockSpec((D, D), lambda b, i: (0, b)),
        out_shape=pltpu.HBM((D, IN_COLS), BF16),
        scratch_shapes=[pltpu.VMEM((D, D), F32)],
        compiler_params=_params(("arbitrary", "arbitrary"), 40),
    )(*_pinned(u, d_glu, dp, dgates, token))


VEC_NAMES = ("mix_pre_g", "dw_bias", "conv_ln_g", "conv_ln_b", "pool_scale", "mix_post_g", "mlp_pre_g", "mlp_post_g")
WEIGHT_ORDER = ("mix_pre_g", "w_in", "dw_kernel", "dw_bias", "conv_ln_g", "conv_ln_b", "w_conv_out", "pool_w",
                "pool_scale", "w_pool_out", "w_o", "mix_post_g", "mlp_pre_g", "w_ff1", "w_ff2", "mlp_post_g")


def _step(x, loss_target, w, m, v):
    row = lambda a: a.reshape(1, D)
    names = [s[0] for s in SHARDED]

    seeded = dict(zip(names, _stage_shards([w[n] for n in names])))
    gather_groups = (("w_in",), ("w_conv_out", "dw_kernel"), ("pool_w", "w_pool_out", "w_o"), ("w_ff1", "w_ff2"))
    first_level, token = _exchange_start(
        "gather_start", [(g, None, [seeded[n] for n in g]) for g in gather_groups], [GATHER_ICI, GATHER_D2D])
    forwarded, full = {}, {}

    def forward(k, after):
        group, (ici_sems, d2d_sems), _, landed = first_level[k]
        _, landed = _exchange_wait("gather_ici_" + group[0], group, None, landed, [(ici_sems, GATHER_ICI)], after)
        (second,), tok = _exchange_start("gather_forward_" + group[0], [(group, None, landed)], [GATHER_FORWARD])
        forwarded[k] = (second[3], [(d2d_sems, GATHER_D2D), (second[1][0], GATHER_FORWARD)])
        return tok

    def gathered(k, after):
        group = gather_groups[k]
        landed, waits = forwarded[k]
        _, arrays = _exchange_wait("gather_wait_" + group[0], group, None, landed, waits, after)
        full.update(zip(group, arrays))

    gathered(0, forward(0, token))
    tok = forward(1, full["w_in"])
    proj, u = _in_proj_fwd(x, row(w["mix_pre_g"]), full["w_in"], tok)
    gathered(1, proj)
    tok = forward(2, proj)
    cv, y_conv = _conv_fwd(proj, full["dw_kernel"], row(w["dw_bias"]), row(w["conv_ln_g"]), row(w["conv_ln_b"]),
                           full["w_conv_out"], tok)
    gathered(2, y_conv)
    tok = forward(3, y_conv)
    z, zl, y_pool, merged, o, h1 = _pool_merge_fwd(proj, y_conv, x, full["pool_w"], row(w["pool_scale"]),
                                                   full["w_pool_out"], full["w_o"], row(w["mix_post_g"]), tok)
    gathered(3, h1)
    vv, dm, dh2, sse, g_mlp_post = _mlp_fwd(h1, row(w["mlp_pre_g"]), full["w_ff1"], full["w_ff2"],
                                            row(w["mlp_post_g"]), loss_target)

    shard_of = {name: shard for name, _, shard, _ in SHARDED}

    def scatter_start(tag, group, grads):
        landings = [lax.empty((N_DEV - 1,) + shard_of[n], g.dtype) for n, g in zip(group, grads)]
        (handle,), tok = _exchange_start("scatter_start_" + tag, [(group, grads, landings)], [SCATTER])
        return handle, tok

    dv, g_ff1, g_ff2 = _mlp_bwd(vv, dm, full["w_ff1"], full["w_ff2"])
    h_ff, tok_ff = scatter_start("ff", ("w_ff1", "w_ff2"), [g_ff1, g_ff2])
    dh1, dy_conv, dy_pool, dgates, g_wo, g_mlp_pre, g_mix_post = _merge_bwd(
        dh2, dv, h1, row(w["mlp_pre_g"]), o, row(w["mix_post_g"]), full["w_o"], merged, proj, y_conv, y_pool, tok_ff)
    dp, g_wpo, g_pw, g_pool_scale = _pool_bwd(dy_pool, zl, z, full["pool_w"], row(w["pool_scale"]), full["w_pool_out"])
    d_glu, g_wco, g_dw, g_bias, g_ln_g, g_ln_b = _conv_bwd(dy_conv, cv, proj, full["dw_kernel"], row(w["conv_ln_g"]),
                                                            row(w["conv_ln_b"]), full["w_conv_out"])
    h_mix, tok_mix = scatter_start("mix", ("w_o", "w_pool_out", "pool_w", "w_conv_out", "dw_kernel"),
                                   [g_wo, g_wpo, g_pw, g_wco, g_dw[:, :TAPS, :]])
    g_win = _in_proj_bwd_w(u, d_glu, dp, dgates, tok_mix)
    h_in, tok_in = scatter_start("in", ("w_in",), [g_win])
    grad_x, g_mix_pre = _in_proj_bwd_x(d_glu, dp, dgates, full["w_in"], x, row(w["mix_pre_g"]), dh1, tok_in)

    out = {}
    slot = jnp.reshape(_my_slot(), (1,)).astype(jnp.int32)
    after = grad_x
    for tag, (group, (sems,), grads, landings) in (("ff", h_ff), ("mix", h_mix), ("in", h_in)):
        if tag == "in":
            vec_parts = after = _vec_exchange(
                [g_mix_pre, g_bias, g_ln_g, g_ln_b, g_pool_scale, g_mix_post, g_mlp_pre, g_mlp_post,
                 jnp.broadcast_to(sse, (1, D))], after)
        mine, landed = _exchange_wait("scatter_wait_" + tag, group, grads, landings, [(sems, SCATTER)], after)
        for name, own, parts in zip(group, mine, landed):
            out[name] = _adamw(name, own, parts, w[name], m[name], v[name], slot, after)
            after = out[name][0]
    stack = lambda d: jnp.stack([d[n] for n in VEC_NAMES], axis=0)
    *res, loss_row = _adamw_vectors(vec_parts, stack(w), stack(m), stack(v))
    for k, name in enumerate(VEC_NAMES):
        out[name] = [r[k] for r in res]
    return loss_row[0, 0], grad_x, out


def kernel(x, mix_pre_g, w_in, dw_kernel, dw_bias, conv_ln_g, conv_ln_b, w_conv_out, pool_w, pool_scale, w_pool_out, w_o, mix_post_g, mlp_pre_g, w_ff1, w_ff2, mlp_post_g, loss_target, m_mix_pre_g, m_w_in, m_dw_kernel, m_dw_bias, m_conv_ln_g, m_conv_ln_b, m_w_conv_out, m_pool_w, m_pool_scale, m_w_pool_out, m_w_o, m_mix_post_g, m_mlp_pre_g, m_w_ff1, m_w_ff2, m_mlp_post_g, v_mix_pre_g, v_w_in, v_dw_kernel, v_dw_bias, v_conv_ln_g, v_conv_ln_b, v_w_conv_out, v_pool_w, v_pool_scale, v_w_pool_out, v_w_o, v_mix_post_g, v_mlp_pre_g, v_w_ff1, v_w_ff2, v_mlp_post_g):
    w = dict(mix_pre_g=mix_pre_g, w_in=w_in, dw_kernel=dw_kernel, dw_bias=dw_bias, conv_ln_g=conv_ln_g, conv_ln_b=conv_ln_b,
             w_conv_out=w_conv_out, pool_w=pool_w, pool_scale=pool_scale, w_pool_out=w_pool_out, w_o=w_o,
             mix_post_g=mix_post_g, mlp_pre_g=mlp_pre_g, w_ff1=w_ff1, w_ff2=w_ff2, mlp_post_g=mlp_post_g)
    m = dict(mix_pre_g=m_mix_pre_g, w_in=m_w_in, dw_kernel=m_dw_kernel, dw_bias=m_dw_bias, conv_ln_g=m_conv_ln_g,
             conv_ln_b=m_conv_ln_b, w_conv_out=m_w_conv_out, pool_w=m_pool_w, pool_scale=m_pool_scale,
             w_pool_out=m_w_pool_out, w_o=m_w_o, mix_post_g=m_mix_post_g, mlp_pre_g=m_mlp_pre_g, w_ff1=m_w_ff1,
             w_ff2=m_w_ff2, mlp_post_g=m_mlp_post_g)
    v = dict(mix_pre_g=v_mix_pre_g, w_in=v_w_in, dw_kernel=v_dw_kernel, dw_bias=v_dw_bias, conv_ln_g=v_conv_ln_g,
             conv_ln_b=v_conv_ln_b, w_conv_out=v_w_conv_out, pool_w=v_pool_w, pool_scale=v_pool_scale,
             w_pool_out=v_w_pool_out, w_o=v_w_o, mix_post_g=v_mix_post_g, mlp_pre_g=v_mlp_pre_g, w_ff1=v_w_ff1,
             w_ff2=v_w_ff2, mlp_post_g=v_mlp_post_g)
    seq = x.shape[1]
    loss, grad_x, out = _step(x.reshape(seq, D), loss_target.reshape(seq, D), w, m, v)
    grads, deltas, new_m, new_v = ([out[n][k] for n in WEIGHT_ORDER] for k in range(4))
    return (loss, grad_x.reshape(x.shape), *grads, *deltas, *new_m, *new_v)
```

```python
import collections

import jax
import jax.numpy as jnp
from jax import lax
from jax.experimental import pallas as pl
from jax.experimental.pallas import tpu as pltpu

D = 1024
D_FF = 4 * D
IN_COLS = 5 * D
TAPS = 31
HALO = 32
POOL_WINDOWS = (2, 4, 8, 16)
PG = D // 4
N_DEV = 8
RMS_EPS = 1e-6
LN_EPS = 1e-5
ADAM_LR, ADAM_B1, ADAM_B2, ADAM_EPS, ADAM_WD, ADAM_STEP = 0.001, 0.9, 0.999, 1e-08, 0.01, 10

BF16 = jnp.bfloat16
F32 = jnp.float32
MIB = 1 << 20
MESH = pl.DeviceIdType.MESH


def _params(sem, vmem_mib):
    return pltpu.CompilerParams(dimension_semantics=sem, vmem_limit_bytes=vmem_mib * MIB)


def _dot(a, b):
    return jnp.dot(a, b, preferred_element_type=F32)


def _dot_nt(a, b):
    return lax.dot_general(a, b, (((1,), (1,)), ((), ())), preferred_element_type=F32)


def _dot_tn(a, b):
    return lax.dot_general(a, b, (((0,), (0,)), ((), ())), preferred_element_type=F32)


def _rms_scale(x):
    return lax.rsqrt(jnp.mean(x * x, axis=-1, keepdims=True) + RMS_EPS)


def _rms_bwd(dy, x, g):
    xn = x * _rms_scale(x)
    dn = dy * g
    dx = _rms_scale(x) * (dn - xn * jnp.mean(dn * xn, axis=-1, keepdims=True))
    return dx, jnp.sum(dy * xn, axis=0, keepdims=True)


def _sigmoid(x):
    return jax.nn.sigmoid(x)


def _acc_out(ref, first, value):
    @pl.when(first)
    def _():
        ref[...] = value

    @pl.when(jnp.logical_not(first))
    def _():
        ref[...] += value


def _my_slot():
    return 4 * lax.axis_index("x") + 2 * lax.axis_index("y") + lax.axis_index("c")


def _peer(mask):
    x, y, c = lax.axis_index("x"), lax.axis_index("y"), lax.axis_index("c")
    return (x ^ ((mask >> 2) & 1), y ^ ((mask >> 1) & 1), c ^ (mask & 1))


def _cols(width):
    return lambda ref, slot: ref.at[:, pl.ds(pl.multiple_of(slot * width, 128), width)]


def _rows(height):
    return lambda ref, slot: ref.at[pl.ds(pl.multiple_of(slot * height, 8), height), :]


def _lead(ref, slot):
    return ref.at[slot]


def _pool_rows(ref, slot):
    return ref.at[:, pl.ds(pl.multiple_of(slot * (PG // N_DEV), 8), PG // N_DEV), :]


SHARDED = (
    ("w_in", (D, IN_COLS), (D, IN_COLS // N_DEV), _cols(IN_COLS // N_DEV)),
    ("w_ff1", (D, D_FF), (D, D_FF // N_DEV), _cols(D_FF // N_DEV)),
    ("w_ff2", (D_FF, D), (D_FF // N_DEV, D), _rows(D_FF // N_DEV)),
    ("w_conv_out", (D, D), (D // N_DEV, D), _rows(D // N_DEV)),
    ("w_pool_out", (D, D), (D // N_DEV, D), _rows(D // N_DEV)),
    ("w_o", (D, D), (D // N_DEV, D), _rows(D // N_DEV)),
    ("pool_w", (4, PG, PG), (4, PG // N_DEV, PG), _pool_rows),
    ("dw_kernel", (N_DEV, TAPS, D // N_DEV), (TAPS, D // N_DEV), _lead),
)
N_SHARDED = len(SHARDED)


SHARD_AT = {name: at for name, _, _, at in SHARDED}
HBM = pl.BlockSpec(memory_space=pltpu.HBM)
SEM = pl.BlockSpec(memory_space=pltpu.SEMAPHORE)
ANY = pl.BlockSpec(memory_space=pl.ANY)
EFFECT = pltpu.SideEffectType.DATAFLOW_SIDE_EFFECTING


def _in_hbm(a):
    return pltpu.with_memory_space_constraint(a, pltpu.HBM)


def _pinned(*arrays):
    return [_in_hbm(a) for a in arrays]


def _stage_shards(shards):
    dtypes = [F32 if name == "dw_kernel" else BF16 for name, *_ in SHARDED]

    def body(*refs):
        ins = refs[:N_SHARDED]
        fulls = refs[N_SHARDED:2 * N_SHARDED]
        raw = refs[2 * N_SHARDED:3 * N_SHARDED]
        stage = refs[3 * N_SHARDED:4 * N_SHARDED]
        in_sems, out_sems = refs[4 * N_SHARDED:]
        me = _my_slot()
        loads = [pltpu.make_async_copy(ins[a], raw[a], in_sems.at[a]) for a in range(N_SHARDED)]
        for cp in loads:
            cp.start()
        stores = []
        for a, (_, _, _, at) in enumerate(SHARDED):
            loads[a].wait()
            stage[a][...] = raw[a][...].astype(dtypes[a])
            cp = pltpu.make_async_copy(stage[a], at(fulls[a], me), out_sems.at[a])
            cp.start()
            stores.append(cp)
        for cp in stores:
            cp.wait()

    return pl.pallas_call(
        body, name="stage_shards",
        out_shape=[pltpu.HBM(full, dt) for (_, full, _, _), dt in zip(SHARDED, dtypes)],
        in_specs=[pl.BlockSpec(memory_space=pl.ANY)] * N_SHARDED,
        out_specs=[pl.BlockSpec(memory_space=pl.ANY)] * N_SHARDED,
        scratch_shapes=[pltpu.VMEM(shard, F32) for _, _, shard, _ in SHARDED]
        + [pltpu.VMEM(shard, dt) for (_, _, shard, _), dt in zip(SHARDED, dtypes)]
        + [pltpu.SemaphoreType.DMA((N_SHARDED,)), pltpu.SemaphoreType.DMA((N_SHARDED,))],
        compiler_params=pltpu.CompilerParams(vmem_limit_bytes=40 * MIB),
    )(*_pinned(*shards))


Leg = collections.namedtuple("Leg", "routes src_of dst_of src_is_land")


def _sem_index(k, m):
    return k * (N_DEV - 1) + m - 1


def _exchange_start(name, groups, legs):
    sizes = [len(g[0]) for g in groups]
    names = [nm for g in groups for nm in g[0]]
    srcs = [s for g in groups if g[1] is not None for s in g[1]]
    lands = [l for g in groups for l in g[2]]
    n_src, n, n_g, n_l = len(srcs), len(names), len(groups), len(legs)

    def body(*refs):
        src_refs, land_refs = list(refs[:n_src]), refs[n_src:n_src + n]
        sems = refs[n_src + n:n_src + n + 2 * n_g * n_l]
        token = refs[-1]
        me = _my_slot()
        first = 0
        for g, size in enumerate(sizes):
            own_src = [src_refs.pop(0) for _ in range(size)] if groups[g][1] is not None else None
            for li, leg in enumerate(legs):
                send, recv = sems[2 * (g * n_l + li)], sems[2 * (g * n_l + li) + 1]
                for m, via in leg.routes:
                    for k in range(size):
                        land = land_refs[first + k]
                        src = land if leg.src_is_land else own_src[k]
                        pltpu.make_async_remote_copy(
                            src_ref=leg.src_of(names[first + k], src, me, m), dst_ref=leg.dst_of(names[first + k], land, me, m),
                            send_sem=send.at[_sem_index(k, m)], recv_sem=recv.at[_sem_index(k, m)],
                            device_id=_peer(via), device_id_type=MESH).start()
            first += size
        token[...] = jnp.zeros_like(token)

    sem_shapes = [pltpu.SemaphoreType.DMA((size * (N_DEV - 1),)) for size in sizes for _ in range(2 * n_l)]
    n_sem = len(sem_shapes)
    outs = pl.pallas_call(
        body, name=name,
        out_shape=sem_shapes + [pltpu.HBM(a.shape, a.dtype) for a in srcs + lands] + [jax.ShapeDtypeStruct((8, 128), F32)],
        in_specs=[HBM] * (n_src + n),
        out_specs=[SEM] * n_sem + [HBM] * (n_src + n) + [pl.BlockSpec(memory_space=pltpu.VMEM)],
        input_output_aliases={k: n_sem + k for k in range(n_src + n)},
        compiler_params=pltpu.CompilerParams(has_side_effects=EFFECT),
    )(*[_in_hbm(a) for a in srcs + lands])
    sems, thru, token = outs[:n_sem], list(outs[n_sem:-1]), outs[-1]
    src_thru, land_thru = thru[:n_src], thru[n_src:]
    handles, first = [], 0
    for g, size in enumerate(sizes):
        pairs = [(sems[2 * (g * n_l + li)], sems[2 * (g * n_l + li) + 1]) for li in range(n_l)]
        mine = [src_thru.pop(0) for _ in range(size)] if groups[g][1] is not None else None
        handles.append((groups[g][0], pairs, mine, land_thru[first:first + size]))
        first += size
    return handles, token


def _exchange_wait(name, names, srcs, lands, waits, after):
    n = len(names)
    n_src = n if srcs is not None else 0

    def body(*refs):
        src_refs, land_refs = refs[:n_src], refs[n_src:n_src + n]
        sems = refs[n_src + n:n_src + n + 2 * len(waits)]
        me = _my_slot()
        for wi, (_, leg) in enumerate(waits):
            for m, via in leg.routes:
                for k in range(n):
                    src = land_refs[k] if leg.src_is_land else src_refs[k]
                    cp = pltpu.make_async_remote_copy(
                        src_ref=leg.src_of(names[k], src, me, m), dst_ref=leg.dst_of(names[k], land_refs[k], me ^ via, m),
                        send_sem=sems[2 * wi].at[_sem_index(k, m)], recv_sem=sems[2 * wi + 1].at[_sem_index(k, m)],
                        device_id=_peer(via), device_id_type=MESH)
                    cp.wait_send()
                    cp.wait_recv()

    arrays = (list(srcs) if srcs is not None else []) + list(lands)
    outs = pl.pallas_call(
        body, name=name,
        out_shape=[pltpu.HBM(a.shape, a.dtype) for a in arrays],
        in_specs=[HBM] * len(arrays) + [SEM] * (2 * len(waits)) + [pl.BlockSpec(memory_space=pl.ANY)],
        out_specs=[HBM] * len(arrays),
        input_output_aliases={k: k for k in range(len(arrays))},
        compiler_params=pltpu.CompilerParams(has_side_effects=EFFECT),
    )(*arrays, *[s for pair, _ in waits for s in pair], after)
    return (outs[:n_src] if srcs is not None else None), outs[n_src:]


def _shard_slot(name, ref, slot):
    return SHARD_AT[name](ref, slot)


GATHER_ICI = Leg(((2, 2), (4, 4), (6, 6)), lambda name, ref, me, m: _shard_slot(name, ref, me),
                 lambda name, ref, sender, m: _shard_slot(name, ref, sender), True)
GATHER_D2D = Leg(((1, 1),), GATHER_ICI.src_of, GATHER_ICI.dst_of, True)
GATHER_FORWARD = Leg(((2, 1), (4, 1), (6, 1)), lambda name, ref, me, m: _shard_slot(name, ref, me ^ m),
                     lambda name, ref, sender, m: _shard_slot(name, ref, sender ^ m), True)
SCATTER = Leg(tuple((m, m) for m in range(1, N_DEV)), lambda name, ref, me, m: _shard_slot(name, ref, me ^ m),
              lambda name, ref, sender, m: ref.at[m - 1], False)


def _vec_exchange(vectors, after):
    n = len(vectors)

    def body(*refs):
        vec, vec_out = refs[n + 2], refs[n + 1]
        send_sems, recv_sems, local_sem = refs[n + 3:]
        for k in range(n):
            vec[k:k + 1, :] = refs[k][...]
        me = _my_slot()
        local = pltpu.make_async_copy(vec, vec_out.at[me], local_sem)
        local.start()
        sends = []
        for mask in range(1, N_DEV):
            cp = pltpu.make_async_remote_copy(
                src_ref=vec, dst_ref=vec_out.at[me], send_sem=send_sems.at[mask - 1],
                recv_sem=recv_sems.at[mask - 1], device_id=_peer(mask), device_id_type=MESH)
            cp.start()
            sends.append(cp)
        for mask in range(1, N_DEV):
            pltpu.make_async_remote_copy(
                src_ref=vec, dst_ref=vec_out.at[me ^ mask], send_sem=send_sems.at[mask - 1],
                recv_sem=recv_sems.at[mask - 1], device_id=_peer(mask), device_id_type=MESH).wait_recv()
        for cp in sends:
            cp.wait_send()
        local.wait()

    return pl.pallas_call(
        body, name="vec_exchange",
        out_shape=jax.ShapeDtypeStruct((N_DEV, n, D), F32),
        in_specs=[pl.BlockSpec(memory_space=pltpu.VMEM)] * n + [ANY],
        out_specs=pl.BlockSpec(memory_space=pl.ANY),
        scratch_shapes=[pltpu.VMEM((n, D), F32), pltpu.SemaphoreType.DMA((N_DEV - 1,)),
                        pltpu.SemaphoreType.DMA((N_DEV - 1,)), pltpu.SemaphoreType.DMA],
    )(*vectors, after)


def _adamw_update(g, w_ref, m_ref, v_ref, g_out, d_out, m_out, v_out):
    m_new = ADAM_B1 * m_ref[...] + (1.0 - ADAM_B1) * g
    v_new = ADAM_B2 * v_ref[...] + (1.0 - ADAM_B2) * (g * g)
    m_hat = m_new / (1.0 - ADAM_B1 ** ADAM_STEP)
    v_hat = v_new / (1.0 - ADAM_B2 ** ADAM_STEP)
    g_out[...] = g
    d_out[...] = -ADAM_LR * (m_hat / (jnp.sqrt(v_hat) + ADAM_EPS) + ADAM_WD * w_ref[...])
    m_out[...] = m_new
    v_out[...] = v_new


ADAMW_ROWS = 256


def _adamw(name, own, parts, w, m, v, slot, token):
    shard = w.shape
    if name in ("w_in", "w_ff1"):
        tr = ADAMW_ROWS
        grid = (shard[0] // tr,)
        own_spec = pl.BlockSpec((tr, shard[1]), lambda i, s: (i, s[0]))
        blk = pl.BlockSpec((tr, shard[1]), lambda i, s: (i, 0))
        parts_spec = pl.BlockSpec((N_DEV - 1, tr, shard[1]), lambda i, s: (0, i, 0))
    elif name == "pool_w":
        grid = (shard[0],)
        own_spec = pl.BlockSpec((None,) + shard[1:], lambda g, s: (g, s[0], 0))
        blk = pl.BlockSpec((None,) + shard[1:], lambda g, s: (g, 0, 0))
        parts_spec = pl.BlockSpec((N_DEV - 1, None) + shard[1:], lambda g, s: (0, g, 0, 0))
    elif name == "dw_kernel":
        grid = (1,)
        own_spec = pl.BlockSpec((None,) + shard, lambda i, s: (s[0], 0, 0))
        blk = pl.BlockSpec(shard, lambda i, s: (0, 0))
        parts_spec = pl.BlockSpec((N_DEV - 1,) + shard, lambda i, s: (0, 0, 0))
    else:
        tr = min(ADAMW_ROWS, shard[0])
        grid = (shard[0] // tr,)
        own_spec = pl.BlockSpec((tr, shard[1]), lambda i, s: (s[0] * grid[0] + i, 0))
        blk = pl.BlockSpec((tr, shard[1]), lambda i, s: (i, 0))
        parts_spec = pl.BlockSpec((N_DEV - 1, tr, shard[1]), lambda i, s: (0, i, 0))

    def body(slot_ref, own_ref, p_ref, w_ref, m_ref, v_ref, _token, g_out, d_out, m_out, v_out):
        g = own_ref[...].astype(F32)
        for k in range(N_DEV - 1):
            g = g + p_ref[k].astype(F32)
        _adamw_update(g, w_ref, m_ref, v_ref, g_out, d_out, m_out, v_out)

    return pl.pallas_call(
        body, name="adamw_" + name,
        grid_spec=pltpu.PrefetchScalarGridSpec(
            num_scalar_prefetch=1, grid=grid, in_specs=[own_spec, parts_spec, blk, blk, blk, ANY], out_specs=[blk] * 4),
        out_shape=[jax.ShapeDtypeStruct(shard, F32)] * 4,
        compiler_params=_params(("arbitrary",), 32),
    )(slot, *_pinned(own, parts, w, m, v, token))


def _adamw_vectors(parts, w, m, v):
    n_vec = w.shape[0]

    def body(p_ref, w_ref, m_ref, v_ref, g_out, d_out, m_out, v_out, loss_out):
        total = p_ref[0]
        for s in range(1, N_DEV):
            total = total + p_ref[s]
        _adamw_update(total[0:n_vec], w_ref, m_ref, v_ref, g_out, d_out, m_out, v_out)
        loss_out[...] = total[n_vec:n_vec + 1] * (0.5 / D)

    return pl.pallas_call(
        body, name="adamw_vectors",
        out_shape=[jax.ShapeDtypeStruct(w.shape, F32)] * 4 + [jax.ShapeDtypeStruct((1, D), F32)],
    )(parts, w, m, v)


def _input_norm(x, g_pre, token):
    seq = x.shape[0]
    tm = 1024

    def body(x_ref, g_ref, _token, u_ref):
        xf = x_ref[...]
        u_ref[...] = (xf * _rms_scale(xf) * g_ref[...]).astype(BF16)

    return pl.pallas_call(
        body, name="input_norm", grid=(seq // tm,),
        in_specs=[pl.BlockSpec((tm, D), lambda i: (i, 0)), pl.BlockSpec((1, D), lambda i: (0, 0)), ANY],
        out_specs=pl.BlockSpec((tm, D), lambda i: (i, 0)),
        out_shape=pltpu.HBM((seq, D), BF16),
        compiler_params=_params(("arbitrary",), 32),
    )(*_pinned(x, g_pre, token))


def _in_proj_fwd(u, w_in, token):
    seq = u.shape[0]
    tm, tn = 1024, IN_COLS // 4

    def body(u_ref, w_ref, _token, proj_ref):
        proj_ref[...] = _dot(u_ref[...], w_ref[...]).astype(BF16)

    return pl.pallas_call(
        body, name="in_proj_fwd", grid=(seq // tm, IN_COLS // tn),
        in_specs=[pl.BlockSpec((tm, D), lambda i, j: (i, 0)), pl.BlockSpec((D, tn), lambda i, j: (0, j)), ANY],
        out_specs=pl.BlockSpec((tm, tn), lambda i, j: (i, j)),
        out_shape=pltpu.HBM((seq, IN_COLS), BF16),
        compiler_params=_params(("arbitrary", "arbitrary"), 40),
    )(*_pinned(u, w_in, token))


CONV_TM = 512
CONV_RS = 128


def _shifts():
    return [(b, [(a, 8 * a + b) for a in range(4) if 8 * a + b < TAPS]) for b in range(8)]


def _taps_looking_back(buf, row0, lanes, weight):
    acc = None
    for b, group in _shifts():
        part = None
        for a, s in group:
            term = weight(TAPS - 1 - s) * buf[pl.ds(row0 - 8 - 8 * a, CONV_RS + 8), lanes]
            part = term if part is None else part + term
        if b:
            part = pltpu.roll(part, b, 0)
        acc = part[8:, :] if acc is None else acc + part[8:, :]
    return acc


def _taps_looking_ahead(buf, row0, lanes, weight):
    acc = None
    for b, group in _shifts():
        part = None
        for a, s in group:
            term = weight(TAPS - 1 - s) * buf[pl.ds(row0 + 8 * a, CONV_RS + 8), lanes]
            part = term if part is None else part + term
        if b:
            part = pltpu.roll(part, CONV_RS + 8 - b, 0)
        acc = part[:CONV_RS, :] if acc is None else acc + part[:CONV_RS, :]
    return acc


def _layer_norm_parts(cv):
    mu = jnp.mean(cv, axis=-1, keepdims=True)
    cen = cv - mu
    rstd = lax.rsqrt(jnp.mean(cen * cen, axis=-1, keepdims=True) + LN_EPS)
    return cen * rstd, rstd


def _conv_fwd(proj, dw, dw_bias, ln_g, ln_b, w_conv_out, token):
    seq = proj.shape[0]
    tm = CONV_TM

    def body(a_ref, gate_ref, dw_ref, bias_ref, lg_ref, lb_ref, w_ref, _token, cv_ref, y_ref, ug):
        i = pl.program_id(0)

        @pl.when(i == 0)
        def _():
            ug[0:HALO, :] = jnp.zeros((HALO, D), F32)

        @pl.when(i > 0)
        def _():
            ug[0:HALO, :] = ug[tm:tm + HALO, :]
        ug[HALO:HALO + tm, :] = a_ref[...].astype(F32) * _sigmoid(gate_ref[...].astype(F32))

        def channel_block(cb, carry):
            lanes = pl.ds(pl.multiple_of(cb * 128, 128), 128)
            for r0 in range(0, tm, CONV_RS):
                taps = _taps_looking_back(ug, HALO + r0, lanes, lambda k: dw_ref[cb, k:k + 1, :])
                cv_ref[pl.ds(r0, CONV_RS), lanes] = taps + bias_ref[:, lanes]
            return carry
        lax.fori_loop(0, D // 128, channel_block, 0)

        n, _ = _layer_norm_parts(cv_ref[...])
        ln = n * lg_ref[...] + lb_ref[...]
        y_ref[...] = _dot((ln * _sigmoid(ln)).astype(BF16), w_ref[...]).astype(BF16)

    vec = pl.BlockSpec((1, D), lambda i: (0, 0))
    tile = pl.BlockSpec((tm, D), lambda i: (i, 0))
    return pl.pallas_call(
        body, name="conv_fwd", grid=(seq // tm,),
        in_specs=[pl.BlockSpec((tm, D), lambda i: (i, 0)), pl.BlockSpec((tm, D), lambda i: (i, 1)),
                  pl.BlockSpec((N_DEV, TAPS, 128), lambda i: (0, 0, 0)), vec, vec, vec,
                  pl.BlockSpec((D, D), lambda i: (0, 0)), ANY],
        out_specs=[tile, tile],
        out_shape=[pltpu.HBM((seq, D), F32), pltpu.HBM((seq, D), BF16)],
        scratch_shapes=[pltpu.VMEM((HALO + tm, D), F32)],
        compiler_params=_params(("arbitrary",), 32),
    )(*_pinned(proj, proj, dw, dw_bias, ln_g, ln_b, w_conv_out, token))


def _window_sums(rows, window, back):
    n = rows.shape[0]
    span = 1
    while span < window:
        rows = rows + pltpu.roll(rows, span if back else n - span, 0)
        span *= 2
    return rows


def _pool_counts(tile_index, tm, window):
    t = tile_index * tm + lax.broadcasted_iota(jnp.int32, (tm, 1), 0)
    return 1.0 / jnp.minimum(t + 1, window).astype(F32)


def _pool_merge_fwd(proj, y_conv, x, pool_w, pool_scale, w_pool_out, w_o, g_post, token):
    seq = proj.shape[0]
    tm = CONV_TM

    def body(p_ref, gc_ref, gp_ref, yc_ref, x_ref, pw_ref, ps_ref, wpo_ref, wo_ref, g_ref, _token,
             z_ref, zl_ref, yp_ref, mg_ref, o_ref, h1_ref, pbuf):
        i = pl.program_id(0)

        @pl.when(i == 0)
        def _():
            pbuf[0:HALO, :] = jnp.zeros((HALO, D), F32)

        @pl.when(i > 0)
        def _():
            pbuf[0:HALO, :] = pbuf[tm:tm + HALO, :]
        pbuf[HALO:HALO + tm, :] = p_ref[...].astype(F32)

        for g, window in enumerate(POOL_WINDOWS):
            lanes = pl.ds(g * PG, PG)
            acc = _window_sums(pbuf[:, lanes], window, back=True)[HALO:, :]
            zg = acc * _pool_counts(i, tm, window) - pbuf[pl.ds(HALO, tm), lanes]
            z_ref[:, lanes] = zg.astype(BF16)
            zl_ref[:, lanes] = _dot(zg.astype(BF16), pw_ref[g])
        zl = zl_ref[...]
        y_pool = _dot((zl * ps_ref[...]).astype(BF16), wpo_ref[...])
        yp_ref[...] = y_pool.astype(BF16)
        merged = (_sigmoid(gc_ref[...].astype(F32)) * yc_ref[...].astype(F32)
                  + _sigmoid(gp_ref[...].astype(F32)) * y_pool).astype(BF16)
        mg_ref[...] = merged
        o = _dot(merged, wo_ref[...])
        o_ref[...] = o
        h1_ref[...] = x_ref[...] + o * _rms_scale(o) * g_ref[...]

    vec = pl.BlockSpec((1, D), lambda i: (0, 0))
    tile = pl.BlockSpec((tm, D), lambda i: (i, 0))
    mat = pl.BlockSpec((D, D), lambda i: (0, 0))
    return pl.pallas_call(
        body, name="pool_merge_fwd", grid=(seq // tm,),
        in_specs=[pl.BlockSpec((tm, D), lambda i: (i, 2)), pl.BlockSpec((tm, D), lambda i: (i, 3)),
                  pl.BlockSpec((tm, D), lambda i: (i, 4)), tile, tile,
                  pl.BlockSpec((4, PG, PG), lambda i: (0, 0, 0)), vec, mat, mat, vec, ANY],
        out_specs=[tile] * 6,
        out_shape=[pltpu.HBM((seq, D), dt) for dt in (BF16, F32, BF16, BF16, F32, F32)],
        scratch_shapes=[pltpu.VMEM((HALO + tm, D), F32)],
        compiler_params=_params(("arbitrary",), 48),
    )(*_pinned(proj, proj, proj, y_conv, x, pool_w, pool_scale, w_pool_out, w_o, g_post, token))


def _mlp_fwd(h1, g_pre, w_ff1, w_ff2, g_post, target):
    seq = h1.shape[0]
    tm, tf = 1024, D_FF // N_DEV
    n_f = D_FF // tf

    def body(h1_ref, gpre_ref, w1_ref, w2_ref, gpost_ref, tgt_ref, v_ref, dm_ref, dh2_ref, sse_ref, ggrad_ref, macc):
        i, j = pl.program_id(0), pl.program_id(1)

        @pl.when(j == 0)
        def _():
            h = h1_ref[...]
            v_ref[...] = (h * _rms_scale(h) * gpre_ref[...]).astype(BF16)
        f = jnp.maximum(_dot(v_ref[...], w1_ref[...]), 0.0)
        part = _dot((f * f).astype(BF16), w2_ref[...])

        @pl.when(j == 0)
        def _():
            macc[...] = part

        @pl.when(j > 0)
        def _():
            macc[...] += part

        @pl.when(j == n_f - 1)
        def _():
            mo = macc[...]
            err = h1_ref[...] + mo * _rms_scale(mo) * gpost_ref[...] - tgt_ref[...]
            dh2 = err * (1.0 / D)
            dh2_ref[...] = dh2
            dm, ggrad = _rms_bwd(dh2, mo, gpost_ref[...])
            dm_ref[...] = dm.astype(BF16)
            _acc_out(ggrad_ref, i == 0, ggrad)
            _acc_out(sse_ref, i == 0, jnp.sum(jnp.sum(err * err, axis=1, keepdims=True), axis=0, keepdims=True))

    vec = pl.BlockSpec((1, D), lambda i, j: (0, 0))
    tile = pl.BlockSpec((tm, D), lambda i, j: (i, 0))
    return pl.pallas_call(
        body, name="mlp_fwd", grid=(seq // tm, n_f),
        in_specs=[tile, vec, pl.BlockSpec((D, tf), lambda i, j: (0, j)), pl.BlockSpec((tf, D), lambda i, j: (j, 0)), vec, tile],
        out_specs=[tile, tile, tile, pl.BlockSpec((1, 1), lambda i, j: (0, 0)), vec],
        out_shape=[pltpu.HBM((seq, D), BF16), pltpu.HBM((seq, D), BF16),
                   pltpu.HBM((seq, D), F32), jax.ShapeDtypeStruct((1, 1), F32),
                   jax.ShapeDtypeStruct((1, D), F32)],
        scratch_shapes=[pltpu.VMEM((tm, D), F32)],
        compiler_params=_params(("arbitrary", "arbitrary"), 56),
    )(*_pinned(h1, g_pre, w_ff1, w_ff2, g_post, target))


def _mlp_bwd(v, dm, w_ff1, w_ff2):
    seq = v.shape[0]
    tm, tf = 1024, D_FF // N_DEV
    n_t = seq // tm

    def body(v_ref, dm_ref, w1_ref, w2_ref, dv_hbm, g1_ref, g2_ref, dv_acc, g1_acc, g2_acc, sem):
        j, i = pl.program_id(0), pl.program_id(1)
        vt, dmt = v_ref[...], dm_ref[...]
        f = jnp.maximum(_dot(vt, w1_ref[...]), 0.0)
        df = (_dot_nt(dmt, w2_ref[...]) * (2.0 * f)).astype(BF16)
        rows = pl.ds(pl.multiple_of(i * tm, tm), tm)
        dv_part = _dot_nt(df, w1_ref[...])

        @pl.when(j == 0)
        def _():
            dv_acc[rows, :] = dv_part

        @pl.when(j > 0)
        def _():
            dv_acc[rows, :] += dv_part
        g1_part = _dot_tn(vt, df)
        g2_part = _dot_tn((f * f).astype(BF16), dmt)

        @pl.when(i == 0)
        def _():
            g1_acc[...] = g1_part
            g2_acc[...] = g2_part

        @pl.when(i > 0)
        def _():
            g1_acc[...] += g1_part
            g2_acc[...] += g2_part

        @pl.when(i == n_t - 1)
        def _():
            g1_ref[...] = g1_acc[...].astype(BF16)
            g2_ref[...] = g2_acc[...].astype(BF16)

        @pl.when(jnp.logical_and(i == n_t - 1, j == D_FF // tf - 1))
        def _():
            cp = pltpu.make_async_copy(dv_acc, dv_hbm, sem)
            cp.start()
            cp.wait()

    tile = pl.BlockSpec((tm, D), lambda j, i: (i, 0))
    return pl.pallas_call(
        body, name="mlp_bwd", grid=(D_FF // tf, n_t),
        in_specs=[tile, tile, pl.BlockSpec((D, tf), lambda j, i: (0, j)), pl.BlockSpec((tf, D), lambda j, i: (j, 0))],
        out_specs=[pl.BlockSpec(memory_space=pl.ANY), pl.BlockSpec((D, tf), lambda j, i: (0, j)),
                   pl.BlockSpec((tf, D), lambda j, i: (j, 0))],
        out_shape=[pltpu.HBM((seq, D), F32), pltpu.HBM((D, D_FF), BF16),
                   pltpu.HBM((D_FF, D), BF16)],
        scratch_shapes=[pltpu.VMEM((seq, D), F32), pltpu.VMEM((D, tf), F32), pltpu.VMEM((tf, D), F32),
                        pltpu.SemaphoreType.DMA],
        compiler_params=_params(("arbitrary", "arbitrary"), 52),
    )(*_pinned(v, dm, w_ff1, w_ff2))


def _merge_bwd(dh2, dv, h1, g_mlp_pre, o, g_mix_post, w_o, merged, proj, y_conv, y_pool, token):
    seq = dh2.shape[0]
    tm = 256
    n_t = seq // tm

    def body(dh2_ref, dv_ref, h1_ref, gpre_ref, o_ref, gpost_ref, wo_ref, mg_ref, gc_ref, gp_ref, yc_ref, yp_ref, _token,
             dh1_ref, dyc_ref, dyp_ref, dg_ref, gwo_ref, ggpre_ref, ggpost_ref, gwo_acc):
        i = pl.program_id(0)
        dnorm, ggpre = _rms_bwd(dv_ref[...], h1_ref[...], gpre_ref[...])
        dh1 = dh2_ref[...] + dnorm
        dh1_ref[...] = dh1
        do, ggpost = _rms_bwd(dh1, o_ref[...], gpost_ref[...])
        do = do.astype(BF16)
        _acc_out(ggpre_ref, i == 0, ggpre)
        _acc_out(ggpost_ref, i == 0, ggpost)
        _acc_out(gwo_acc, i == 0, _dot_tn(mg_ref[...], do))
        dmerged = _dot_nt(do, wo_ref[...])
        sc, sp = _sigmoid(gc_ref[...].astype(F32)), _sigmoid(gp_ref[...].astype(F32))
        dyc_ref[...] = (dmerged * sc).astype(BF16)
        dyp_ref[...] = (dmerged * sp).astype(BF16)
        dg_ref[:, 0:D] = (dmerged * yc_ref[...].astype(F32) * (sc * (1.0 - sc))).astype(BF16)
        dg_ref[:, D:2 * D] = (dmerged * yp_ref[...].astype(F32) * (sp * (1.0 - sp))).astype(BF16)

        @pl.when(i == n_t - 1)
        def _():
            gwo_ref[...] = gwo_acc[...].astype(BF16)

    vec = pl.BlockSpec((1, D), lambda i: (0, 0))
    tile = pl.BlockSpec((tm, D), lambda i: (i, 0))
    mat = pl.BlockSpec((D, D), lambda i: (0, 0))
    return pl.pallas_call(
        body, name="merge_bwd", grid=(n_t,),
        in_specs=[tile, tile, tile, vec, tile, vec, mat, tile,
                  pl.BlockSpec((tm, D), lambda i: (i, 3)), pl.BlockSpec((tm, D), lambda i: (i, 4)), tile, tile, ANY],
        out_specs=[tile, tile, tile, pl.BlockSpec((tm, 2 * D), lambda i: (i, 0)), mat, vec, vec],
        out_shape=[pltpu.HBM((seq, D), F32), pltpu.HBM((seq, D), BF16),
                   pltpu.HBM((seq, D), BF16), pltpu.HBM((seq, 2 * D), BF16),
                   pltpu.HBM((D, D), BF16), jax.ShapeDtypeStruct((1, D), F32),
                   jax.ShapeDtypeStruct((1, D), F32)],
        scratch_shapes=[pltpu.VMEM((D, D), F32)],
        compiler_params=_params(("arbitrary",), 48),
    )(*_pinned(dh2, dv, h1, g_mlp_pre, o, g_mix_post, w_o, merged, proj, proj, y_conv, y_pool, token))


def _pool_bwd(dy_pool, zl, z, pool_w, pool_scale, w_pool_out):
    seq = dy_pool.shape[0]
    tm = CONV_TM
    n_t = seq // tm

    def body(dy_ref, zl_ref, z_ref, pw_ref, ps_ref, wpo_ref, dp_ref, gwpo_ref, gpw_ref, gps_ref, qbuf, gwpo_acc, gpw_acc):
        i = pl.program_id(0)
        tile_index = n_t - 1 - i
        first = i == 0
        dy = dy_ref[...]
        zl = zl_ref[...]
        dzs = _dot_nt(dy, wpo_ref[...])
        _acc_out(gwpo_acc, first, _dot_tn((zl * ps_ref[...]).astype(BF16), dy))
        _acc_out(gps_ref, first, jnp.sum(dzs * zl, axis=0, keepdims=True))
        dzl = (dzs * ps_ref[...]).astype(BF16)

        @pl.when(first)
        def _():
            qbuf[tm:tm + HALO, :] = jnp.zeros((HALO, D), F32)

        @pl.when(jnp.logical_not(first))
        def _():
            qbuf[tm:tm + HALO, :] = qbuf[0:HALO, :]

        dzs_list = []
        for g, window in enumerate(POOL_WINDOWS):
            lanes = pl.ds(g * PG, PG)
            dzl_g = dzl[:, g * PG:(g + 1) * PG]
            dz = _dot_nt(dzl_g, pw_ref[g])
            _acc_out(gpw_acc.at[g], first, _dot_tn(z_ref[:, lanes], dzl_g))
            qbuf[pl.ds(0, tm), lanes] = dz * _pool_counts(tile_index, tm, window)
            dzs_list.append(dz)
        for g, window in enumerate(POOL_WINDOWS):
            lanes = pl.ds(g * PG, PG)
            acc = _window_sums(qbuf[:, lanes], window, back=False)[:tm, :]
            dp_ref[:, lanes] = (acc - dzs_list[g]).astype(BF16)

        @pl.when(i == n_t - 1)
        def _():
            gwpo_ref[...] = gwpo_acc[...].astype(BF16)
            gpw_ref[...] = gpw_acc[...].astype(BF16)

    vec = pl.BlockSpec((1, D), lambda i: (0, 0))
    tile = pl.BlockSpec((tm, D), lambda i: (n_t - 1 - i, 0))
    mat = pl.BlockSpec((D, D), lambda i: (0, 0))
    pw = pl.BlockSpec((4, PG, PG), lambda i: (0, 0, 0))
    return pl.pallas_call(
        body, name="pool_bwd", grid=(n_t,),
        in_specs=[tile, tile, tile, pw, vec, mat],
        out_specs=[tile, mat, pw, vec],
        out_shape=[pltpu.HBM((seq, D), BF16), pltpu.HBM((D, D), BF16),
                   pltpu.HBM((4, PG, PG), BF16), jax.ShapeDtypeStruct((1, D), F32)],
        scratch_shapes=[pltpu.VMEM((tm + HALO, D), F32), pltpu.VMEM((D, D), F32), pltpu.VMEM((4, PG, PG), F32)],
        compiler_params=_params(("arbitrary",), 40),
    )(*_pinned(dy_pool, zl, z, pool_w, pool_scale, w_pool_out))


def _conv_bwd(dy_conv, cv, proj, dw, ln_g, ln_b, w_conv_out):
    seq = dy_conv.shape[0]
    tm = CONV_TM // 2
    n_t = seq // tm
    halo_blocks = tm // HALO

    def body(dy_ref, cv_ref, a_ref, gate_ref, ah_ref, gh_ref, dw_ref, lg_ref, lb_ref, w_ref,
             dglu_ref, gw_ref, gdw_ref, gbias_ref, glg_ref, glb_ref, ug, dcv, dug, gw_acc):
        i = pl.program_id(0)
        tile_index = n_t - 1 - i
        first = i == 0
        dy = dy_ref[...]
        n, rstd = _layer_norm_parts(cv_ref[...])
        ln = n * lg_ref[...] + lb_ref[...]
        sg = _sigmoid(ln)
        _acc_out(gw_acc, first, _dot_tn((ln * sg).astype(BF16), dy))
        dln = _dot_nt(dy, w_ref[...]) * (sg * (1.0 + ln * (1.0 - sg)))
        _acc_out(glg_ref, first, jnp.sum(dln * n, axis=0, keepdims=True))
        _acc_out(glb_ref, first, jnp.sum(dln, axis=0, keepdims=True))
        dn = dln * lg_ref[...]
        dcv_tile = rstd * (dn - jnp.mean(dn, axis=-1, keepdims=True) - n * jnp.mean(dn * n, axis=-1, keepdims=True))
        _acc_out(gbias_ref, first, jnp.sum(dcv_tile, axis=0, keepdims=True))

        @pl.when(first)
        def _():
            dcv[tm:tm + HALO, :] = jnp.zeros((HALO, D), F32)

        @pl.when(jnp.logical_not(first))
        def _():
            dcv[tm:tm + HALO, :] = dcv[0:HALO, :]
        dcv[0:tm, :] = dcv_tile

        a, gate = a_ref[...].astype(F32), gate_ref[...].astype(F32)
        sgate = _sigmoid(gate)
        ug[HALO:HALO + tm, :] = a * sgate
        before = jnp.where(tile_index > 0, 1.0, 0.0)
        ug[0:HALO, :] = ah_ref[...].astype(F32) * _sigmoid(gh_ref[...].astype(F32)) * before

        @pl.when(first)
        def _():
            gdw_ref[...] = jnp.zeros((N_DEV, TAPS + 1, 128), F32)

        def channel_block(cb, carry):
            lanes = pl.ds(pl.multiple_of(cb * 128, 128), 128)
            for r0 in range(0, tm, CONV_RS):
                dug[pl.ds(r0, CONV_RS), lanes] = _taps_looking_ahead(dcv, r0, lanes, lambda k: dw_ref[cb, k:k + 1, :])
            for b, group in _shifts():
                sums = [jnp.zeros((8, 128), F32) for _ in group]
                for r0 in range(0, tm, CONV_RS):
                    window = ug[pl.ds(r0, CONV_RS + HALO), lanes]
                    if b:
                        window = pltpu.roll(window, b, 0)
                    d = dcv[pl.ds(r0, CONV_RS), lanes]
                    for n_a, (a, s) in enumerate(group):
                        prod = d * window[HALO - 8 * a:HALO - 8 * a + CONV_RS, :]
                        sums[n_a] = sums[n_a] + jnp.sum(prod.reshape(CONV_RS // 8, 8, 128), axis=0)
                for n_a, (a, s) in enumerate(group):
                    k = TAPS - 1 - s
                    gdw_ref[cb, k:k + 1, :] += jnp.sum(sums[n_a], axis=0, keepdims=True)
            return carry
        lax.fori_loop(0, D // 128, channel_block, 0)

        d_ug = dug[...]
        dglu_ref[:, 0:D] = (d_ug * sgate).astype(BF16)
        dglu_ref[:, D:2 * D] = (d_ug * a * (sgate * (1.0 - sgate))).astype(BF16)

        @pl.when(i == n_t - 1)
        def _():
            gw_ref[...] = gw_acc[...].astype(BF16)

    def halo_index(col):
        return lambda i: (jnp.maximum((n_t - 1 - i) * halo_blocks - 1, 0), col)

    vec = pl.BlockSpec((1, D), lambda i: (0, 0))
    tile = pl.BlockSpec((tm, D), lambda i: (n_t - 1 - i, 0))
    mat = pl.BlockSpec((D, D), lambda i: (0, 0))
    dwspec = pl.BlockSpec((N_DEV, TAPS, 128), lambda i: (0, 0, 0))
    return pl.pallas_call(
        body, name="conv_bwd", grid=(n_t,),
        in_specs=[tile, tile, pl.BlockSpec((tm, D), lambda i: (n_t - 1 - i, 0)), pl.BlockSpec((tm, D), lambda i: (n_t - 1 - i, 1)),
                  pl.BlockSpec((HALO, D), halo_index(0)), pl.BlockSpec((HALO, D), halo_index(1)), dwspec, vec, vec, mat],
        out_specs=[pl.BlockSpec((tm, 2 * D), lambda i: (n_t - 1 - i, 0)), mat,
                   pl.BlockSpec((N_DEV, TAPS + 1, 128), lambda i: (0, 0, 0)), vec, vec, vec],
        out_shape=[pltpu.HBM((seq, 2 * D), BF16), pltpu.HBM((D, D), BF16),
                   pltpu.HBM((N_DEV, TAPS + 1, 128), F32), jax.ShapeDtypeStruct((1, D), F32),
                   jax.ShapeDtypeStruct((1, D), F32), jax.ShapeDtypeStruct((1, D), F32)],
        scratch_shapes=[pltpu.VMEM((HALO + tm, D), F32), pltpu.VMEM((tm + HALO, D), F32), pltpu.VMEM((tm, D), F32),
                        pltpu.VMEM((D, D), F32)],
        compiler_params=_params(("arbitrary",), 48),
    )(*_pinned(dy_conv, cv, proj, proj, proj, proj, dw, ln_g, ln_b, w_conv_out))


def _in_proj_bwd_x(d_glu, dp, dgates, w_in, x, g_pre, dh1, token):
    seq = x.shape[0]
    tm = 512

    def body(dglu_ref, dp_ref, dg_ref, w_ref, x_ref, g_ref, dh1_ref, _token, dx_ref, gg_ref):
        du = _dot_nt(dglu_ref[...], w_ref[:, 0:2 * D])
        du += _dot_nt(dp_ref[...], w_ref[:, 2 * D:3 * D])
        du += _dot_nt(dg_ref[...], w_ref[:, 3 * D:5 * D])
        dnorm, gg = _rms_bwd(du, x_ref[...], g_ref[...])
        dx_ref[...] = dh1_ref[...] + dnorm
        _acc_out(gg_ref, pl.program_id(0) == 0, gg)

    vec = pl.BlockSpec((1, D), lambda i: (0, 0))
    tile = pl.BlockSpec((tm, D), lambda i: (i, 0))
    wide = pl.BlockSpec((tm, 2 * D), lambda i: (i, 0))
    return pl.pallas_call(
        body, name="in_proj_bwd_x", grid=(seq // tm,),
        in_specs=[wide, tile, wide, pl.BlockSpec((D, IN_COLS), lambda i: (0, 0)), tile, vec, tile, ANY],
        out_specs=[tile, vec],
        out_shape=[pltpu.HBM((seq, D), F32), jax.ShapeDtypeStruct((1, D), F32)],
        compiler_params=_params(("arbitrary",), 48),
    )(*_pinned(d_glu, dp, dgates, w_in, x, g_pre, dh1, token))


def _in_proj_bwd_w(u, d_glu, dp, dgates, token):
    seq = u.shape[0]
    tm = 1024
    n_t = seq // tm

    def body(u_ref, dglu_ref, dp_ref, dg_ref, _token, out_ref, acc):
        b, i = pl.program_id(0), pl.program_id(1)
        ut = u_ref[...]

        def add(d_ref):
            _acc_out(acc, i == 0, _dot_tn(ut, d_ref[...]))

        pl.when(b < 2)(lambda: add(dglu_ref))
        pl.when(b == 2)(lambda: add(dp_ref))
        pl.when(b > 2)(lambda: add(dg_ref))

        @pl.when(i == n_t - 1)
        def _():
            out_ref[...] = acc[...].astype(BF16)

    return pl.pallas_call(
        body, name="in_proj_bwd_w", grid=(IN_COLS // D, n_t),
        in_specs=[pl.BlockSpec((tm, D), lambda b, i: (i, 0)),
                  pl.BlockSpec((tm, D), lambda b, i: (jnp.where(b < 2, i, 0), jnp.minimum(b, 1))),
                  pl.BlockSpec((tm, D), lambda b, i: (jnp.where(b == 2, i, 0), 0)),
                  pl.BlockSpec((tm, D), lambda b, i: (jnp.where(b > 2, i, 0), jnp.maximum(b - 3, 0))), ANY],
        out_specs=pl.BlockSpec((D, D), lambda b, i: (0, b)),
        out_shape=pltpu.HBM((D, IN_COLS), BF16),
        scratch_shapes=[pltpu.VMEM((D, D), F32)],
        compiler_params=_params(("arbitrary", "arbitrary"), 40),
    )(*_pinned(u, d_glu, dp, dgates, token))


VEC_NAMES = ("mix_pre_g", "dw_bias", "conv_ln_g", "conv_ln_b", "pool_scale", "mix_post_g", "mlp_pre_g", "mlp_post_g")
WEIGHT_ORDER = ("mix_pre_g", "w_in", "dw_kernel", "dw_bias", "conv_ln_g", "conv_ln_b", "w_conv_out", "pool_w",
                "pool_scale", "w_pool_out", "w_o", "mix_post_g", "mlp_pre_g", "w_ff1", "w_ff2", "mlp_post_g")


def _step(x, loss_target, w, m, v):
    row = lambda a: a.reshape(1, D)
    names = [s[0] for s in SHARDED]

    seeded = dict(zip(names, _stage_shards([w[n] for n in names])))
    gather_groups = (("w_in",), ("w_conv_out", "dw_kernel"), ("pool_w", "w_pool_out", "w_o"), ("w_ff1", "w_ff2"))
    first_level, token = _exchange_start(
        "gather_start", [(g, None, [seeded[n] for n in g]) for g in gather_groups], [GATHER_ICI, GATHER_D2D])
    forwarded, full = {}, {}

    def forward(k, after):
        group, (ici_sems, d2d_sems), _, landed = first_level[k]
        _, landed = _exchange_wait("gather_ici_" + group[0], group, None, landed, [(ici_sems, GATHER_ICI)], after)
        (second,), tok = _exchange_start("gather_forward_" + group[0], [(group, None, landed)], [GATHER_FORWARD])
        forwarded[k] = (second[3], [(d2d_sems, GATHER_D2D), (second[1][0], GATHER_FORWARD)])
        return tok

    def gathered(k, after):
        group = gather_groups[k]
        landed, waits = forwarded[k]
        _, arrays = _exchange_wait("gather_wait_" + group[0], group, None, landed, waits, after)
        full.update(zip(group, arrays))

    u = _input_norm(x, row(w["mix_pre_g"]), token)
    gathered(0, forward(0, u))
    tok = forward(1, full["w_in"])
    proj = _in_proj_fwd(u, full["w_in"], tok)
    gathered(1, proj)
    tok = forward(2, proj)
    cv, y_conv = _conv_fwd(proj, full["dw_kernel"], row(w["dw_bias"]), row(w["conv_ln_g"]), row(w["conv_ln_b"]),
                           full["w_conv_out"], tok)
    gathered(2, y_conv)
    tok = forward(3, y_conv)
    z, zl, y_pool, merged, o, h1 = _pool_merge_fwd(proj, y_conv, x, full["pool_w"], row(w["pool_scale"]),
                                                   full["w_pool_out"], full["w_o"], row(w["mix_post_g"]), tok)
    gathered(3, h1)
    vv, dm, dh2, sse, g_mlp_post = _mlp_fwd(h1, row(w["mlp_pre_g"]), full["w_ff1"], full["w_ff2"],
                                            row(w["mlp_post_g"]), loss_target)

    shard_of = {name: shard for name, _, shard, _ in SHARDED}

    def scatter_start(tag, group, grads):
        landings = [lax.empty((N_DEV - 1,) + shard_of[n], g.dtype) for n, g in zip(group, grads)]
        (handle,), tok = _exchange_start("scatter_start_" + tag, [(group, grads, landings)], [SCATTER])
        return handle, tok

    dv, g_ff1, g_ff2 = _mlp_bwd(vv, dm, full["w_ff1"], full["w_ff2"])
    h_ff, tok_ff = scatter_start("ff", ("w_ff1", "w_ff2"), [g_ff1, g_ff2])
    dh1, dy_conv, dy_pool, dgates, g_wo, g_mlp_pre, g_mix_post = _merge_bwd(
        dh2, dv, h1, row(w["mlp_pre_g"]), o, row(w["mix_post_g"]), full["w_o"], merged, proj, y_conv, y_pool, tok_ff)
    dp, g_wpo, g_pw, g_pool_scale = _pool_bwd(dy_pool, zl, z, full["pool_w"], row(w["pool_scale"]), full["w_pool_out"])
    d_glu, g_wco, g_dw, g_bias, g_ln_g, g_ln_b = _conv_bwd(dy_conv, cv, proj, full["dw_kernel"], row(w["conv_ln_g"]),
                                                            row(w["conv_ln_b"]), full["w_conv_out"])
    h_mix, tok_mix = scatter_start("mix", ("w_o", "w_pool_out", "pool_w", "w_conv_out", "dw_kernel"),
                                   [g_wo, g_wpo, g_pw, g_wco, g_dw[:, :TAPS, :]])
    g_win = _in_proj_bwd_w(u, d_glu, dp, dgates, tok_mix)
    h_in, tok_in = scatter_start("in", ("w_in",), [g_win])
    grad_x, g_mix_pre = _in_proj_bwd_x(d_glu, dp, dgates, full["w_in"], x, row(w["mix_pre_g"]), dh1, tok_in)

    out = {}
    slot = jnp.reshape(_my_slot(), (1,)).astype(jnp.int32)
    after = grad_x
    for tag, (group, (sems,), grads, landings) in (("ff", h_ff), ("mix", h_mix), ("in", h_in)):
        if tag == "in":
            vec_parts = after = _vec_exchange(
                [g_mix_pre, g_bias, g_ln_g, g_ln_b, g_pool_scale, g_mix_post, g_mlp_pre, g_mlp_post,
                 jnp.broadcast_to(sse, (1, D))], after)
        mine, landed = _exchange_wait("scatter_wait_" + tag, group, grads, landings, [(sems, SCATTER)], after)
        for name, own, parts in zip(group, mine, landed):
            out[name] = _adamw(name, own, parts, w[name], m[name], v[name], slot, after)
            after = out[name][0]
    stack = lambda d: jnp.stack([d[n] for n in VEC_NAMES], axis=0)
    *res, loss_row = _adamw_vectors(vec_parts, stack(w), stack(m), stack(v))
    for k, name in enumerate(VEC_NAMES):
        out[name] = [r[k] for r in res]
    return loss_row[0, 0], grad_x, out


def kernel(x, mix_pre_g, w_in, dw_kernel, dw_bias, conv_ln_g, conv_ln_b, w_conv_out, pool_w, pool_scale, w_pool_out, w_o, mix_post_g, mlp_pre_g, w_ff1, w_ff2, mlp_post_g, loss_target, m_mix_pre_g, m_w_in, m_dw_kernel, m_dw_bias, m_conv_ln_g, m_conv_ln_b, m_w_conv_out, m_pool_w, m_pool_scale, m_w_pool_out, m_w_o, m_mix_post_g, m_mlp_pre_g, m_w_ff1, m_w_ff2, m_mlp_post_g, v_mix_pre_g, v_w_in, v_dw_kernel, v_dw_bias, v_conv_ln_g, v_conv_ln_b, v_w_conv_out, v_pool_w, v_pool_scale, v_w_pool_out, v_w_o, v_mix_post_g, v_mlp_pre_g, v_w_ff1, v_w_ff2, v_mlp_post_g):
    w = dict(mix_pre_g=mix_pre_g, w_in=w_in, dw_kernel=dw_kernel, dw_bias=dw_bias, conv_ln_g=conv_ln_g, conv_ln_b=conv_ln_b,
             w_conv_out=w_conv_out, pool_w=pool_w, pool_scale=pool_scale, w_pool_out=w_pool_out, w_o=w_o,
             mix_post_g=mix_post_g, mlp_pre_g=mlp_pre_g, w_ff1=w_ff1, w_ff2=w_ff2, mlp_post_g=mlp_post_g)
    m = dict(mix_pre_g=m_mix_pre_g, w_in=m_w_in, dw_kernel=m_dw_kernel, dw_bias=m_dw_bias, conv_ln_g=m_conv_ln_g,
             conv_ln_b=m_conv_ln_b, w_conv_out=m_w_conv_out, pool_w=m_pool_w, pool_scale=m_pool_scale,
             w_pool_out=m_w_pool_out, w_o=m_w_o, mix_post_g=m_mix_post_g, mlp_pre_g=m_mlp_pre_g, w_ff1=m_w_ff1,
             w_ff2=m_w_ff2, mlp_post_g=m_mlp_post_g)
    v = dict(mix_pre_g=v_mix_pre_g, w_in=v_w_in, dw_kernel=v_dw_kernel, dw_bias=v_dw_bias, conv_ln_g=v_conv_ln_g,
             conv_ln_b=v_conv_ln_b, w_conv_out=v_w_conv_out, pool_w=v_pool_w, pool_scale=v_pool_scale,
             w_pool_out=v_w_pool_out, w_o=v_w_o, mix_post_g=v_mix_post_g, mlp_pre_g=v_mlp_pre_g, w_ff1=v_w_ff1,
             w_ff2=v_w_ff2, mlp_post_g=v_mlp_post_g)
    seq = x.shape[1]
    loss, grad_x, out = _step(x.reshape(seq, D), loss_target.reshape(seq, D), w, m, v)
    grads, deltas, new_m, new_v = ([out[n][k] for n in WEIGHT_ORDER] for k in range(4))
    return (loss, grad_x.reshape(x.shape), *grads, *deltas, *new_m, *new_v)
```

```python
import collections

import jax
import jax.numpy as jnp
from jax import lax
from jax.experimental import pallas as pl
from jax.experimental.pallas import tpu as pltpu

D = 1024
D_FF = 4 * D
IN_COLS = 5 * D
TAPS = 31
HALO = 32
POOL_WINDOWS = (2, 4, 8, 16)
PG = D // 4
N_DEV = 8
RMS_EPS = 1e-6
LN_EPS = 1e-5
ADAM_LR, ADAM_B1, ADAM_B2, ADAM_EPS, ADAM_WD, ADAM_STEP = 0.001, 0.9, 0.999, 1e-08, 0.01, 10

BF16 = jnp.bfloat16
F32 = jnp.float32
MIB = 1 << 20
MESH = pl.DeviceIdType.MESH


def _params(sem, vmem_mib):
    return pltpu.CompilerParams(dimension_semantics=sem, vmem_limit_bytes=vmem_mib * MIB)


def _dot(a, b):
    return jnp.dot(a, b, preferred_element_type=F32)


def _dot_nt(a, b):
    return lax.dot_general(a, b, (((1,), (1,)), ((), ())), preferred_element_type=F32)


def _dot_tn(a, b):
    return lax.dot_general(a, b, (((0,), (0,)), ((), ())), preferred_element_type=F32)


def _rms_scale(x):
    return lax.rsqrt(jnp.mean(x * x, axis=-1, keepdims=True) + RMS_EPS)


def _rms_bwd(dy, x, g):
    xn = x * _rms_scale(x)
    dn = dy * g
    dx = _rms_scale(x) * (dn - xn * jnp.mean(dn * xn, axis=-1, keepdims=True))
    return dx, jnp.sum(dy * xn, axis=0, keepdims=True)


def _sigmoid(x):
    return jax.nn.sigmoid(x)


def _acc_out(ref, first, value):
    @pl.when(first)
    def _():
        ref[...] = value

    @pl.when(jnp.logical_not(first))
    def _():
        ref[...] += value


def _my_slot():
    return 4 * lax.axis_index("x") + 2 * lax.axis_index("y") + lax.axis_index("c")


def _peer(mask):
    x, y, c = lax.axis_index("x"), lax.axis_index("y"), lax.axis_index("c")
    return (x ^ ((mask >> 2) & 1), y ^ ((mask >> 1) & 1), c ^ (mask & 1))


def _cols(width):
    return lambda ref, slot: ref.at[:, pl.ds(pl.multiple_of(slot * width, 128), width)]


def _rows(height):
    return lambda ref, slot: ref.at[pl.ds(pl.multiple_of(slot * height, 8), height), :]


def _lead(ref, slot):
    return ref.at[slot]


def _pool_rows(ref, slot):
    return ref.at[:, pl.ds(pl.multiple_of(slot * (PG // N_DEV), 8), PG // N_DEV), :]


SHARDED = (
    ("w_in", (D, IN_COLS), (D, IN_COLS // N_DEV), _cols(IN_COLS // N_DEV)),
    ("w_ff1", (D, D_FF), (D, D_FF // N_DEV), _cols(D_FF // N_DEV)),
    ("w_ff2", (D_FF, D), (D_FF // N_DEV, D), _rows(D_FF // N_DEV)),
    ("w_conv_out", (D, D), (D // N_DEV, D), _rows(D // N_DEV)),
    ("w_pool_out", (D, D), (D // N_DEV, D), _rows(D // N_DEV)),
    ("w_o", (D, D), (D // N_DEV, D), _rows(D // N_DEV)),
    ("pool_w", (4, PG, PG), (4, PG // N_DEV, PG), _pool_rows),
    ("dw_kernel", (N_DEV, TAPS, D // N_DEV), (TAPS, D // N_DEV), _lead),
)
N_SHARDED = len(SHARDED)


SHARD_AT = {name: at for name, _, _, at in SHARDED}
HBM = pl.BlockSpec(memory_space=pltpu.HBM)
SEM = pl.BlockSpec(memory_space=pltpu.SEMAPHORE)
ANY = pl.BlockSpec(memory_space=pl.ANY)
EFFECT = pltpu.SideEffectType.DATAFLOW_SIDE_EFFECTING


def _in_hbm(a):
    return pltpu.with_memory_space_constraint(a, pltpu.HBM)


def _pinned(*arrays):
    return [_in_hbm(a) for a in arrays]


def _stage_shards(shards):
    dtypes = [F32 if name == "dw_kernel" else BF16 for name, *_ in SHARDED]

    def body(*refs):
        ins = refs[:N_SHARDED]
        fulls = refs[N_SHARDED:2 * N_SHARDED]
        raw = refs[2 * N_SHARDED:3 * N_SHARDED]
        stage = refs[3 * N_SHARDED:4 * N_SHARDED]
        in_sems, out_sems = refs[4 * N_SHARDED:]
        me = _my_slot()
        loads = [pltpu.make_async_copy(ins[a], raw[a], in_sems.at[a]) for a in range(N_SHARDED)]
        for cp in loads:
            cp.start()
        stores = []
        for a, (_, _, _, at) in enumerate(SHARDED):
            loads[a].wait()
            stage[a][...] = raw[a][...].astype(dtypes[a])
            cp = pltpu.make_async_copy(stage[a], at(fulls[a], me), out_sems.at[a])
            cp.start()
            stores.append(cp)
        for cp in stores:
            cp.wait()

    return pl.pallas_call(
        body, name="stage_shards",
        out_shape=[pltpu.HBM(full, dt) for (_, full, _, _), dt in zip(SHARDED, dtypes)],
        in_specs=[pl.BlockSpec(memory_space=pl.ANY)] * N_SHARDED,
        out_specs=[pl.BlockSpec(memory_space=pl.ANY)] * N_SHARDED,
        scratch_shapes=[pltpu.VMEM(shard, F32) for _, _, shard, _ in SHARDED]
        + [pltpu.VMEM(shard, dt) for (_, _, shard, _), dt in zip(SHARDED, dtypes)]
        + [pltpu.SemaphoreType.DMA((N_SHARDED,)), pltpu.SemaphoreType.DMA((N_SHARDED,))],
        compiler_params=pltpu.CompilerParams(vmem_limit_bytes=40 * MIB),
    )(*_pinned(*shards))


Leg = collections.namedtuple("Leg", "routes src_of dst_of src_is_land")


def _sem_index(k, m):
    return k * (N_DEV - 1) + m - 1


def _exchange_start(name, groups, legs):
    sizes = [len(g[0]) for g in groups]
    names = [nm for g in groups for nm in g[0]]
    srcs = [s for g in groups if g[1] is not None for s in g[1]]
    lands = [l for g in groups for l in g[2]]
    n_src, n, n_g, n_l = len(srcs), len(names), len(groups), len(legs)

    def body(*refs):
        src_refs, land_refs = list(refs[:n_src]), refs[n_src:n_src + n]
        sems = refs[n_src + n:n_src + n + 2 * n_g * n_l]
        token = refs[-1]
        me = _my_slot()
        first = 0
        for g, size in enumerate(sizes):
            own_src = [src_refs.pop(0) for _ in range(size)] if groups[g][1] is not None else None
            for li, leg in enumerate(legs):
                send, recv = sems[2 * (g * n_l + li)], sems[2 * (g * n_l + li) + 1]
                for m, via in leg.routes:
                    for k in range(size):
                        land = land_refs[first + k]
                        src = land if leg.src_is_land else own_src[k]
                        pltpu.make_async_remote_copy(
                            src_ref=leg.src_of(names[first + k], src, me, m), dst_ref=leg.dst_of(names[first + k], land, me, m),
                            send_sem=send.at[_sem_index(k, m)], recv_sem=recv.at[_sem_index(k, m)],
                            device_id=_peer(via), device_id_type=MESH).start()
            first += size
        token[...] = jnp.zeros_like(token)

    sem_shapes = [pltpu.SemaphoreType.DMA((size * (N_DEV - 1),)) for size in sizes for _ in range(2 * n_l)]
    n_sem = len(sem_shapes)
    outs = pl.pallas_call(
        body, name=name,
        out_shape=sem_shapes + [pltpu.HBM(a.shape, a.dtype) for a in srcs + lands] + [jax.ShapeDtypeStruct((8, 128), F32)],
        in_specs=[HBM] * (n_src + n),
        out_specs=[SEM] * n_sem + [HBM] * (n_src + n) + [pl.BlockSpec(memory_space=pltpu.VMEM)],
        input_output_aliases={k: n_sem + k for k in range(n_src + n)},
        compiler_params=pltpu.CompilerParams(has_side_effects=EFFECT),
    )(*[_in_hbm(a) for a in srcs + lands])
    sems, thru, token = outs[:n_sem], list(outs[n_sem:-1]), outs[-1]
    src_thru, land_thru = thru[:n_src], thru[n_src:]
    handles, first = [], 0
    for g, size in enumerate(sizes):
        pairs = [(sems[2 * (g * n_l + li)], sems[2 * (g * n_l + li) + 1]) for li in range(n_l)]
        mine = [src_thru.pop(0) for _ in range(size)] if groups[g][1] is not None else None
        handles.append((groups[g][0], pairs, mine, land_thru[first:first + size]))
        first += size
    return handles, token


def _exchange_wait(name, names, srcs, lands, waits, after):
    n = len(names)
    n_src = n if srcs is not None else 0

    def body(*refs):
        src_refs, land_refs = refs[:n_src], refs[n_src:n_src + n]
        sems = refs[n_src + n:n_src + n + 2 * len(waits)]
        me = _my_slot()
        for wi, (_, leg) in enumerate(waits):
            for m, via in leg.routes:
                for k in range(n):
                    src = land_refs[k] if leg.src_is_land else src_refs[k]
                    cp = pltpu.make_async_remote_copy(
                        src_ref=leg.src_of(names[k], src, me, m), dst_ref=leg.dst_of(names[k], land_refs[k], me ^ via, m),
                        send_sem=sems[2 * wi].at[_sem_index(k, m)], recv_sem=sems[2 * wi + 1].at[_sem_index(k, m)],
                        device_id=_peer(via), device_id_type=MESH)
                    cp.wait_send()
                    cp.wait_recv()

    arrays = (list(srcs) if srcs is not None else []) + list(lands)
    outs = pl.pallas_call(
        body, name=name,
        out_shape=[pltpu.HBM(a.shape, a.dtype) for a in arrays],
        in_specs=[HBM] * len(arrays) + [SEM] * (2 * len(waits)) + [pl.BlockSpec(memory_space=pl.ANY)],
        out_specs=[HBM] * len(arrays),
        input_output_aliases={k: k for k in range(len(arrays))},
        compiler_params=pltpu.CompilerParams(has_side_effects=EFFECT),
    )(*arrays, *[s for pair, _ in waits for s in pair], after)
    return (outs[:n_src] if srcs is not None else None), outs[n_src:]


def _shard_slot(name, ref, slot):
    return SHARD_AT[name](ref, slot)


GATHER_ICI = Leg(((2, 2), (4, 4), (6, 6)), lambda name, ref, me, m: _shard_slot(name, ref, me),
                 lambda name, ref, sender, m: _shard_slot(name, ref, sender), True)
GATHER_D2D = Leg(((1, 1),), GATHER_ICI.src_of, GATHER_ICI.dst_of, True)
GATHER_FORWARD = Leg(((2, 1), (4, 1), (6, 1)), lambda name, ref, me, m: _shard_slot(name, ref, me ^ m),
                     lambda name, ref, sender, m: _shard_slot(name, ref, sender ^ m), True)
SCATTER = Leg(tuple((m, m) for m in range(1, N_DEV)), lambda name, ref, me, m: _shard_slot(name, ref, me ^ m),
              lambda name, ref, sender, m: ref.at[m - 1], False)


def _vec_exchange(vectors, after):
    n = len(vectors)

    def body(*refs):
        vec, vec_out = refs[n + 2], refs[n + 1]
        send_sems, recv_sems, local_sem = refs[n + 3:]
        for k in range(n):
            vec[k:k + 1, :] = refs[k][...]
        me = _my_slot()
        local = pltpu.make_async_copy(vec, vec_out.at[me], local_sem)
        local.start()
        sends = []
        for mask in range(1, N_DEV):
            cp = pltpu.make_async_remote_copy(
                src_ref=vec, dst_ref=vec_out.at[me], send_sem=send_sems.at[mask - 1],
                recv_sem=recv_sems.at[mask - 1], device_id=_peer(mask), device_id_type=MESH)
            cp.start()
            sends.append(cp)
        for mask in range(1, N_DEV):
            pltpu.make_async_remote_copy(
                src_ref=vec, dst_ref=vec_out.at[me ^ mask], send_sem=send_sems.at[mask - 1],
                recv_sem=recv_sems.at[mask - 1], device_id=_peer(mask), device_id_type=MESH).wait_recv()
        for cp in sends:
            cp.wait_send()
        local.wait()

    return pl.pallas_call(
        body, name="vec_exchange",
        out_shape=jax.ShapeDtypeStruct((N_DEV, n, D), F32),
        in_specs=[pl.BlockSpec(memory_space=pltpu.VMEM)] * n + [ANY],
        out_specs=pl.BlockSpec(memory_space=pl.ANY),
        scratch_shapes=[pltpu.VMEM((n, D), F32), pltpu.SemaphoreType.DMA((N_DEV - 1,)),
                        pltpu.SemaphoreType.DMA((N_DEV - 1,)), pltpu.SemaphoreType.DMA],
    )(*vectors, after)


def _adamw_update(g, w_ref, m_ref, v_ref, g_out, d_out, m_out, v_out):
    m_new = ADAM_B1 * m_ref[...] + (1.0 - ADAM_B1) * g
    v_new = ADAM_B2 * v_ref[...] + (1.0 - ADAM_B2) * (g * g)
    m_hat = m_new / (1.0 - ADAM_B1 ** ADAM_STEP)
    v_hat = v_new / (1.0 - ADAM_B2 ** ADAM_STEP)
    g_out[...] = g
    d_out[...] = -ADAM_LR * (m_hat / (jnp.sqrt(v_hat) + ADAM_EPS) + ADAM_WD * w_ref[...])
    m_out[...] = m_new
    v_out[...] = v_new


ADAMW_ROWS = 256


def _adamw(name, own, parts, w, m, v, slot, token):
    shard = w.shape
    if name in ("w_in", "w_ff1"):
        tr = ADAMW_ROWS
        grid = (shard[0] // tr,)
        own_spec = pl.BlockSpec((tr, shard[1]), lambda i, s: (i, s[0]))
        blk = pl.BlockSpec((tr, shard[1]), lambda i, s: (i, 0))
        parts_spec = pl.BlockSpec((N_DEV - 1, tr, shard[1]), lambda i, s: (0, i, 0))
    elif name == "pool_w":
        grid = (shard[0],)
        own_spec = pl.BlockSpec((None,) + shard[1:], lambda g, s: (g, s[0], 0))
        blk = pl.BlockSpec((None,) + shard[1:], lambda g, s: (g, 0, 0))
        parts_spec = pl.BlockSpec((N_DEV - 1, None) + shard[1:], lambda g, s: (0, g, 0, 0))
    elif name == "dw_kernel":
        grid = (1,)
        own_spec = pl.BlockSpec((None,) + shard, lambda i, s: (s[0], 0, 0))
        blk = pl.BlockSpec(shard, lambda i, s: (0, 0))
        parts_spec = pl.BlockSpec((N_DEV - 1,) + shard, lambda i, s: (0, 0, 0))
    else:
        tr = min(ADAMW_ROWS, shard[0])
        grid = (shard[0] // tr,)
        own_spec = pl.BlockSpec((tr, shard[1]), lambda i, s: (s[0] * grid[0] + i, 0))
        blk = pl.BlockSpec((tr, shard[1]), lambda i, s: (i, 0))
        parts_spec = pl.BlockSpec((N_DEV - 1, tr, shard[1]), lambda i, s: (0, i, 0))

    def body(slot_ref, own_ref, p_ref, w_ref, m_ref, v_ref, _token, g_out, d_out, m_out, v_out):
        g = own_ref[...].astype(F32)
        for k in range(N_DEV - 1):
            g = g + p_ref[k].astype(F32)
        _adamw_update(g, w_ref, m_ref, v_ref, g_out, d_out, m_out, v_out)

    return pl.pallas_call(
        body, name="adamw_" + name,
        grid_spec=pltpu.PrefetchScalarGridSpec(
            num_scalar_prefetch=1, grid=grid, in_specs=[own_spec, parts_spec, blk, blk, blk, ANY], out_specs=[blk] * 4),
        out_shape=[jax.ShapeDtypeStruct(shard, F32)] * 4,
        compiler_params=_params(("arbitrary",), 32),
    )(slot, *_pinned(own, parts, w, m, v, token))


def _adamw_vectors(parts, w, m, v):
    n_vec = w.shape[0]

    def body(p_ref, w_ref, m_ref, v_ref, g_out, d_out, m_out, v_out, loss_out):
        total = p_ref[0]
        for s in range(1, N_DEV):
            total = total + p_ref[s]
        _adamw_update(total[0:n_vec], w_ref, m_ref, v_ref, g_out, d_out, m_out, v_out)
        loss_out[...] = total[n_vec:n_vec + 1] * (0.5 / D)

    return pl.pallas_call(
        body, name="adamw_vectors",
        out_shape=[jax.ShapeDtypeStruct(w.shape, F32)] * 4 + [jax.ShapeDtypeStruct((1, D), F32)],
    )(parts, w, m, v)


def _input_norm(x, g_pre, token):
    seq = x.shape[0]
    tm = 1024

    def body(x_ref, g_ref, _token, u_ref, ut_ref):
        xf = x_ref[...]
        u = (xf * _rms_scale(xf) * g_ref[...]).astype(BF16)
        u_ref[...] = u
        ut_ref[...] = u.T

    return pl.pallas_call(
        body, name="input_norm", grid=(seq // tm,),
        in_specs=[pl.BlockSpec((tm, D), lambda i: (i, 0)), pl.BlockSpec((1, D), lambda i: (0, 0)), ANY],
        out_specs=[pl.BlockSpec((tm, D), lambda i: (i, 0)), pl.BlockSpec((D, tm), lambda i: (0, i))],
        out_shape=[pltpu.HBM((seq, D), BF16), pltpu.HBM((D, seq), BF16)],
        compiler_params=_params(("arbitrary",), 40),
    )(*_pinned(x, g_pre, token))


def _in_proj_fwd(u, w_in, token):
    seq = u.shape[0]
    tm, tn = 1024, IN_COLS // 4

    def body(u_ref, w_ref, _token, proj_ref):
        proj_ref[...] = _dot(u_ref[...], w_ref[...]).astype(BF16)

    return pl.pallas_call(
        body, name="in_proj_fwd", grid=(seq // tm, IN_COLS // tn),
        in_specs=[pl.BlockSpec((tm, D), lambda i, j: (i, 0)), pl.BlockSpec((D, tn), lambda i, j: (0, j)), ANY],
        out_specs=pl.BlockSpec((tm, tn), lambda i, j: (i, j)),
        out_shape=pltpu.HBM((seq, IN_COLS), BF16),
        compiler_params=_params(("arbitrary", "arbitrary"), 40),
    )(*_pinned(u, w_in, token))


CONV_TM = 512
CONV_RS = 128


def _shifts():
    return [(b, [(a, 8 * a + b) for a in range(4) if 8 * a + b < TAPS]) for b in range(8)]


def _taps_looking_back(buf, row0, lanes, weight):
    acc = None
    for b, group in _shifts():
        part = None
        for a, s in group:
            term = weight(TAPS - 1 - s) * buf[pl.ds(row0 - 8 - 8 * a, CONV_RS + 8), lanes]
            part = term if part is None else part + term
        if b:
            part = pltpu.roll(part, b, 0)
        acc = part[8:, :] if acc is None else acc + part[8:, :]
    return acc


def _taps_looking_ahead(buf, row0, lanes, weight):
    acc = None
    for b, group in _shifts():
        part = None
        for a, s in group:
            term = weight(TAPS - 1 - s) * buf[pl.ds(row0 + 8 * a, CONV_RS + 8), lanes]
            part = term if part is None else part + term
        if b:
            part = pltpu.roll(part, CONV_RS + 8 - b, 0)
        acc = part[:CONV_RS, :] if acc is None else acc + part[:CONV_RS, :]
    return acc


def _layer_norm_parts(cv):
    mu = jnp.mean(cv, axis=-1, keepdims=True)
    cen = cv - mu
    rstd = lax.rsqrt(jnp.mean(cen * cen, axis=-1, keepdims=True) + LN_EPS)
    return cen * rstd, rstd


def _conv_fwd(proj, dw, dw_bias, ln_g, ln_b, w_conv_out, token):
    seq = proj.shape[0]
    tm = CONV_TM

    def body(a_ref, gate_ref, dw_ref, bias_ref, lg_ref, lb_ref, w_ref, _token, cv_ref, y_ref, ug):
        i = pl.program_id(0)

        @pl.when(i == 0)
        def _():
            ug[0:HALO, :] = jnp.zeros((HALO, D), F32)

        @pl.when(i > 0)
        def _():
            ug[0:HALO, :] = ug[tm:tm + HALO, :]
        ug[HALO:HALO + tm, :] = a_ref[...].astype(F32) * _sigmoid(gate_ref[...].astype(F32))

        def channel_block(cb, carry):
            lanes = pl.ds(pl.multiple_of(cb * 128, 128), 128)
            for r0 in range(0, tm, CONV_RS):
                taps = _taps_looking_back(ug, HALO + r0, lanes, lambda k: dw_ref[cb, k:k + 1, :])
                cv_ref[pl.ds(r0, CONV_RS), lanes] = taps + bias_ref[:, lanes]
            return carry
        lax.fori_loop(0, D // 128, channel_block, 0)

        n, _ = _layer_norm_parts(cv_ref[...])
        ln = n * lg_ref[...] + lb_ref[...]
        y_ref[...] = _dot((ln * _sigmoid(ln)).astype(BF16), w_ref[...]).astype(BF16)

    vec = pl.BlockSpec((1, D), lambda i: (0, 0))
    tile = pl.BlockSpec((tm, D), lambda i: (i, 0))
    return pl.pallas_call(
        body, name="conv_fwd", grid=(seq // tm,),
        in_specs=[pl.BlockSpec((tm, D), lambda i: (i, 0)), pl.BlockSpec((tm, D), lambda i: (i, 1)),
                  pl.BlockSpec((N_DEV, TAPS, 128), lambda i: (0, 0, 0)), vec, vec, vec,
                  pl.BlockSpec((D, D), lambda i: (0, 0)), ANY],
        out_specs=[tile, tile],
        out_shape=[pltpu.HBM((seq, D), F32), pltpu.HBM((seq, D), BF16)],
        scratch_shapes=[pltpu.VMEM((HALO + tm, D), F32)],
        compiler_params=_params(("arbitrary",), 32),
    )(*_pinned(proj, proj, dw, dw_bias, ln_g, ln_b, w_conv_out, token))


def _window_sums(rows, window, back):
    n = rows.shape[0]
    span = 1
    while span < window:
        rows = rows + pltpu.roll(rows, span if back else n - span, 0)
        span *= 2
    return rows


def _pool_counts(tile_index, tm, window):
    t = tile_index * tm + lax.broadcasted_iota(jnp.int32, (tm, 1), 0)
    return 1.0 / jnp.minimum(t + 1, window).astype(F32)


def _pool_merge_fwd(proj, y_conv, x, pool_w, pool_scale, w_pool_out, w_o, g_post, token):
    seq = proj.shape[0]
    tm = CONV_TM

    def body(p_ref, gc_ref, gp_ref, yc_ref, x_ref, pw_ref, ps_ref, wpo_ref, wo_ref, g_ref, _token,
             z_ref, zl_ref, yp_ref, mg_ref, o_ref, h1_ref, pbuf):
        i = pl.program_id(0)

        @pl.when(i == 0)
        def _():
            pbuf[0:HALO, :] = jnp.zeros((HALO, D), F32)

        @pl.when(i > 0)
        def _():
            pbuf[0:HALO, :] = pbuf[tm:tm + HALO, :]
        pbuf[HALO:HALO + tm, :] = p_ref[...].astype(F32)

        for g, window in enumerate(POOL_WINDOWS):
            lanes = pl.ds(g * PG, PG)
            acc = _window_sums(pbuf[:, lanes], window, back=True)[HALO:, :]
            zg = acc * _pool_counts(i, tm, window) - pbuf[pl.ds(HALO, tm), lanes]
            z_ref[:, lanes] = zg.astype(BF16)
            zl_ref[:, lanes] = _dot(zg.astype(BF16), pw_ref[g])
        zl = zl_ref[...]
        y_pool = _dot((zl * ps_ref[...]).astype(BF16), wpo_ref[...])
        yp_ref[...] = y_pool.astype(BF16)
        merged = (_sigmoid(gc_ref[...].astype(F32)) * yc_ref[...].astype(F32)
                  + _sigmoid(gp_ref[...].astype(F32)) * y_pool).astype(BF16)
        mg_ref[...] = merged
        o = _dot(merged, wo_ref[...])
        o_ref[...] = o
        h1_ref[...] = x_ref[...] + o * _rms_scale(o) * g_ref[...]

    vec = pl.BlockSpec((1, D), lambda i: (0, 0))
    tile = pl.BlockSpec((tm, D), lambda i: (i, 0))
    mat = pl.BlockSpec((D, D), lambda i: (0, 0))
    return pl.pallas_call(
        body, name="pool_merge_fwd", grid=(seq // tm,),
        in_specs=[pl.BlockSpec((tm, D), lambda i: (i, 2)), pl.BlockSpec((tm, D), lambda i: (i, 3)),
                  pl.BlockSpec((tm, D), lambda i: (i, 4)), tile, tile,
                  pl.BlockSpec((4, PG, PG), lambda i: (0, 0, 0)), vec, mat, mat, vec, ANY],
        out_specs=[tile] * 6,
        out_shape=[pltpu.HBM((seq, D), dt) for dt in (BF16, F32, BF16, BF16, F32, F32)],
        scratch_shapes=[pltpu.VMEM((HALO + tm, D), F32)],
        compiler_params=_params(("arbitrary",), 48),
    )(*_pinned(proj, proj, proj, y_conv, x, pool_w, pool_scale, w_pool_out, w_o, g_post, token))


def _mlp_fwd(h1, g_pre, w_ff1, w_ff2, g_post, target):
    seq = h1.shape[0]
    tm, tf = 1024, D_FF // N_DEV
    n_f = D_FF // tf

    def body(h1_ref, gpre_ref, w1_ref, w2_ref, gpost_ref, tgt_ref, v_ref, dm_ref, dh2_ref, sse_ref, ggrad_ref, macc):
        i, j = pl.program_id(0), pl.program_id(1)

        @pl.when(j == 0)
        def _():
            h = h1_ref[...]
            v_ref[...] = (h * _rms_scale(h) * gpre_ref[...]).astype(BF16)
        f = jnp.maximum(_dot(v_ref[...], w1_ref[...]), 0.0)
        part = _dot((f * f).astype(BF16), w2_ref[...])

        @pl.when(j == 0)
        def _():
            macc[...] = part

        @pl.when(j > 0)
        def _():
            macc[...] += part

        @pl.when(j == n_f - 1)
        def _():
            mo = macc[...]
            err = h1_ref[...] + mo * _rms_scale(mo) * gpost_ref[...] - tgt_ref[...]
            dh2 = err * (1.0 / D)
            dh2_ref[...] = dh2
            dm, ggrad = _rms_bwd(dh2, mo, gpost_ref[...])
            dm_ref[...] = dm.astype(BF16)
            _acc_out(ggrad_ref, i == 0, ggrad)
            _acc_out(sse_ref, i == 0, jnp.sum(jnp.sum(err * err, axis=1, keepdims=True), axis=0, keepdims=True))

    vec = pl.BlockSpec((1, D), lambda i, j: (0, 0))
    tile = pl.BlockSpec((tm, D), lambda i, j: (i, 0))
    return pl.pallas_call(
        body, name="mlp_fwd", grid=(seq // tm, n_f),
        in_specs=[tile, vec, pl.BlockSpec((D, tf), lambda i, j: (0, j)), pl.BlockSpec((tf, D), lambda i, j: (j, 0)), vec, tile],
        out_specs=[tile, tile, tile, pl.BlockSpec((1, 1), lambda i, j: (0, 0)), vec],
        out_shape=[pltpu.HBM((seq, D), BF16), pltpu.HBM((seq, D), BF16),
                   pltpu.HBM((seq, D), F32), jax.ShapeDtypeStruct((1, 1), F32),
                   jax.ShapeDtypeStruct((1, D), F32)],
        scratch_shapes=[pltpu.VMEM((tm, D), F32)],
        compiler_params=_params(("arbitrary", "arbitrary"), 56),
    )(*_pinned(h1, g_pre, w_ff1, w_ff2, g_post, target))


def _mlp_bwd(v, dm, w_ff1, w_ff2):
    seq = v.shape[0]
    tm, tf = 1024, D_FF // N_DEV
    n_t = seq // tm

    def body(v_ref, dm_ref, w1_ref, w2_ref, dv_hbm, g1_ref, g2_ref, dv_acc, g1_acc, g2_acc, sem):
        j, i = pl.program_id(0), pl.program_id(1)
        vt, dmt = v_ref[...], dm_ref[...]
        f = jnp.maximum(_dot(vt, w1_ref[...]), 0.0)
        df = (_dot_nt(dmt, w2_ref[...]) * (2.0 * f)).astype(BF16)
        rows = pl.ds(pl.multiple_of(i * tm, tm), tm)
        dv_part = _dot_nt(df, w1_ref[...])

        @pl.when(j == 0)
        def _():
            dv_acc[rows, :] = dv_part

        @pl.when(j > 0)
        def _():
            dv_acc[rows, :] += dv_part
        g1_part = _dot_tn(vt, df)
        g2_part = _dot_tn((f * f).astype(BF16), dmt)

        @pl.when(i == 0)
        def _():
            g1_acc[...] = g1_part
            g2_acc[...] = g2_part

        @pl.when(i > 0)
        def _():
            g1_acc[...] += g1_part
            g2_acc[...] += g2_part

        @pl.when(i == n_t - 1)
        def _():
            g1_ref[...] = g1_acc[...].astype(BF16)
            g2_ref[...] = g2_acc[...].astype(BF16)

        @pl.when(jnp.logical_and(i == n_t - 1, j == D_FF // tf - 1))
        def _():
            cp = pltpu.make_async_copy(dv_acc, dv_hbm, sem)
            cp.start()
            cp.wait()

    tile = pl.BlockSpec((tm, D), lambda j, i: (i, 0))
    return pl.pallas_call(
        body, name="mlp_bwd", grid=(D_FF // tf, n_t),
        in_specs=[tile, tile, pl.BlockSpec((D, tf), lambda j, i: (0, j)), pl.BlockSpec((tf, D), lambda j, i: (j, 0))],
        out_specs=[pl.BlockSpec(memory_space=pl.ANY), pl.BlockSpec((D, tf), lambda j, i: (0, j)),
                   pl.BlockSpec((tf, D), lambda j, i: (j, 0))],
        out_shape=[pltpu.HBM((seq, D), F32), pltpu.HBM((D, D_FF), BF16),
                   pltpu.HBM((D_FF, D), BF16)],
        scratch_shapes=[pltpu.VMEM((seq, D), F32), pltpu.VMEM((D, tf), F32), pltpu.VMEM((tf, D), F32),
                        pltpu.SemaphoreType.DMA],
        compiler_params=_params(("arbitrary", "arbitrary"), 52),
    )(*_pinned(v, dm, w_ff1, w_ff2))


def _merge_bwd(dh2, dv, h1, g_mlp_pre, o, g_mix_post, w_o, merged, proj, y_conv, y_pool, token):
    seq = dh2.shape[0]
    tm = 256
    n_t = seq // tm

    def body(dh2_ref, dv_ref, h1_ref, gpre_ref, o_ref, gpost_ref, wo_ref, mg_ref, gc_ref, gp_ref, yc_ref, yp_ref, _token,
             dh1_ref, dyc_ref, dyp_ref, dg_ref, gwo_ref, ggpre_ref, ggpost_ref, gwo_acc):
        i = pl.program_id(0)
        dnorm, ggpre = _rms_bwd(dv_ref[...], h1_ref[...], gpre_ref[...])
        dh1 = dh2_ref[...] + dnorm
        dh1_ref[...] = dh1
        do, ggpost = _rms_bwd(dh1, o_ref[...], gpost_ref[...])
        do = do.astype(BF16)
        _acc_out(ggpre_ref, i == 0, ggpre)
        _acc_out(ggpost_ref, i == 0, ggpost)
        _acc_out(gwo_acc, i == 0, _dot_tn(mg_ref[...], do))
        dmerged = _dot_nt(do, wo_ref[...])
        sc, sp = _sigmoid(gc_ref[...].astype(F32)), _sigmoid(gp_ref[...].astype(F32))
        dyc_ref[...] = (dmerged * sc).astype(BF16)
        dyp_ref[...] = (dmerged * sp).astype(BF16)
        dg_ref[:, 0:D] = (dmerged * yc_ref[...].astype(F32) * (sc * (1.0 - sc))).astype(BF16)
        dg_ref[:, D:2 * D] = (dmerged * yp_ref[...].astype(F32) * (sp * (1.0 - sp))).astype(BF16)

        @pl.when(i == n_t - 1)
        def _():
            gwo_ref[...] = gwo_acc[...].astype(BF16)

    vec = pl.BlockSpec((1, D), lambda i: (0, 0))
    tile = pl.BlockSpec((tm, D), lambda i: (i, 0))
    mat = pl.BlockSpec((D, D), lambda i: (0, 0))
    return pl.pallas_call(
        body, name="merge_bwd", grid=(n_t,),
        in_specs=[tile, tile, tile, vec, tile, vec, mat, tile,
                  pl.BlockSpec((tm, D), lambda i: (i, 3)), pl.BlockSpec((tm, D), lambda i: (i, 4)), tile, tile, ANY],
        out_specs=[tile, tile, tile, pl.BlockSpec((tm, 2 * D), lambda i: (i, 0)), mat, vec, vec],
        out_shape=[pltpu.HBM((seq, D), F32), pltpu.HBM((seq, D), BF16),
                   pltpu.HBM((seq, D), BF16), pltpu.HBM((seq, 2 * D), BF16),
                   pltpu.HBM((D, D), BF16), jax.ShapeDtypeStruct((1, D), F32),
                   jax.ShapeDtypeStruct((1, D), F32)],
        scratch_shapes=[pltpu.VMEM((D, D), F32)],
        compiler_params=_params(("arbitrary",), 48),
    )(*_pinned(dh2, dv, h1, g_mlp_pre, o, g_mix_post, w_o, merged, proj, proj, y_conv, y_pool, token))


def _pool_bwd(dy_pool, zl, z, pool_w, pool_scale, w_pool_out):
    seq = dy_pool.shape[0]
    tm = CONV_TM
    n_t = seq // tm

    def body(dy_ref, zl_ref, z_ref, pw_ref, ps_ref, wpo_ref, dp_ref, gwpo_ref, gpw_ref, gps_ref, qbuf, gwpo_acc, gpw_acc):
        i = pl.program_id(0)
        tile_index = n_t - 1 - i
        first = i == 0
        dy = dy_ref[...]
        zl = zl_ref[...]
        dzs = _dot_nt(dy, wpo_ref[...])
        _acc_out(gwpo_acc, first, _dot_tn((zl * ps_ref[...]).astype(BF16), dy))
        _acc_out(gps_ref, first, jnp.sum(dzs * zl, axis=0, keepdims=True))
        dzl = (dzs * ps_ref[...]).astype(BF16)

        @pl.when(first)
        def _():
            qbuf[tm:tm + HALO, :] = jnp.zeros((HALO, D), F32)

        @pl.when(jnp.logical_not(first))
        def _():
            qbuf[tm:tm + HALO, :] = qbuf[0:HALO, :]

        dzs_list = []
        for g, window in enumerate(POOL_WINDOWS):
            lanes = pl.ds(g * PG, PG)
            dzl_g = dzl[:, g * PG:(g + 1) * PG]
            dz = _dot_nt(dzl_g, pw_ref[g])
            _acc_out(gpw_acc.at[g], first, _dot_tn(z_ref[:, lanes], dzl_g))
            qbuf[pl.ds(0, tm), lanes] = dz * _pool_counts(tile_index, tm, window)
            dzs_list.append(dz)
        for g, window in enumerate(POOL_WINDOWS):
            lanes = pl.ds(g * PG, PG)
            acc = _window_sums(qbuf[:, lanes], window, back=False)[:tm, :]
            dp_ref[:, lanes] = (acc - dzs_list[g]).astype(BF16)

        @pl.when(i == n_t - 1)
        def _():
            gwpo_ref[...] = gwpo_acc[...].astype(BF16)
            gpw_ref[...] = gpw_acc[...].astype(BF16)

    vec = pl.BlockSpec((1, D), lambda i: (0, 0))
    tile = pl.BlockSpec((tm, D), lambda i: (n_t - 1 - i, 0))
    mat = pl.BlockSpec((D, D), lambda i: (0, 0))
    pw = pl.BlockSpec((4, PG, PG), lambda i: (0, 0, 0))
    return pl.pallas_call(
        body, name="pool_bwd", grid=(n_t,),
        in_specs=[tile, tile, tile, pw, vec, mat],
        out_specs=[tile, mat, pw, vec],
        out_shape=[pltpu.HBM((seq, D), BF16), pltpu.HBM((D, D), BF16),
                   pltpu.HBM((4, PG, PG), BF16), jax.ShapeDtypeStruct((1, D), F32)],
        scratch_shapes=[pltpu.VMEM((tm + HALO, D), F32), pltpu.VMEM((D, D), F32), pltpu.VMEM((4, PG, PG), F32)],
        compiler_params=_params(("arbitrary",), 40),
    )(*_pinned(dy_pool, zl, z, pool_w, pool_scale, w_pool_out))


def _conv_bwd(dy_conv, cv, proj, dw, ln_g, ln_b, w_conv_out):
    seq = dy_conv.shape[0]
    tm = CONV_TM // 2
    n_t = seq // tm
    halo_blocks = tm // HALO

    def body(dy_ref, cv_ref, a_ref, gate_ref, ah_ref, gh_ref, dw_ref, lg_ref, lb_ref, w_ref,
             dglu_ref, gw_ref, gdw_ref, gbias_ref, glg_ref, glb_ref, ug, dcv, dug, gw_acc):
        i = pl.program_id(0)
        tile_index = n_t - 1 - i
        first = i == 0
        dy = dy_ref[...]
        n, rstd = _layer_norm_parts(cv_ref[...])
        ln = n * lg_ref[...] + lb_ref[...]
        sg = _sigmoid(ln)
        _acc_out(gw_acc, first, _dot_tn((ln * sg).astype(BF16), dy))
        dln = _dot_nt(dy, w_ref[...]) * (sg * (1.0 + ln * (1.0 - sg)))
        _acc_out(glg_ref, first, jnp.sum(dln * n, axis=0, keepdims=True))
        _acc_out(glb_ref, first, jnp.sum(dln, axis=0, keepdims=True))
        dn = dln * lg_ref[...]
        dcv_tile = rstd * (dn - jnp.mean(dn, axis=-1, keepdims=True) - n * jnp.mean(dn * n, axis=-1, keepdims=True))
        _acc_out(gbias_ref, first, jnp.sum(dcv_tile, axis=0, keepdims=True))

        @pl.when(first)
        def _():
            dcv[tm:tm + HALO, :] = jnp.zeros((HALO, D), F32)

        @pl.when(jnp.logical_not(first))
        def _():
            dcv[tm:tm + HALO, :] = dcv[0:HALO, :]
        dcv[0:tm, :] = dcv_tile

        a, gate = a_ref[...].astype(F32), gate_ref[...].astype(F32)
        sgate = _sigmoid(gate)
        ug[HALO:HALO + tm, :] = a * sgate
        before = jnp.where(tile_index > 0, 1.0, 0.0)
        ug[0:HALO, :] = ah_ref[...].astype(F32) * _sigmoid(gh_ref[...].astype(F32)) * before

        @pl.when(first)
        def _():
            gdw_ref[...] = jnp.zeros((N_DEV, TAPS + 1, 128), F32)

        def channel_block(cb, carry):
            lanes = pl.ds(pl.multiple_of(cb * 128, 128), 128)
            for r0 in range(0, tm, CONV_RS):
                dug[pl.ds(r0, CONV_RS), lanes] = _taps_looking_ahead(dcv, r0, lanes, lambda k: dw_ref[cb, k:k + 1, :])
            for b, group in _shifts():
                sums = [jnp.zeros((8, 128), F32) for _ in group]
                for r0 in range(0, tm, CONV_RS):
                    window = ug[pl.ds(r0, CONV_RS + HALO), lanes]
                    if b:
                        window = pltpu.roll(window, b, 0)
                    d = dcv[pl.ds(r0, CONV_RS), lanes]
                    for n_a, (a, s) in enumerate(group):
                        prod = d * window[HALO - 8 * a:HALO - 8 * a + CONV_RS, :]
                        sums[n_a] = sums[n_a] + jnp.sum(prod.reshape(CONV_RS // 8, 8, 128), axis=0)
                for n_a, (a, s) in enumerate(group):
                    k = TAPS - 1 - s
                    gdw_ref[cb, k:k + 1, :] += jnp.sum(sums[n_a], axis=0, keepdims=True)
            return carry
        lax.fori_loop(0, D // 128, channel_block, 0)

        d_ug = dug[...]
        dglu_ref[:, 0:D] = (d_ug * sgate).astype(BF16)
        dglu_ref[:, D:2 * D] = (d_ug * a * (sgate * (1.0 - sgate))).astype(BF16)

        @pl.when(i == n_t - 1)
        def _():
            gw_ref[...] = gw_acc[...].astype(BF16)

    def halo_index(col):
        return lambda i: (jnp.maximum((n_t - 1 - i) * halo_blocks - 1, 0), col)

    vec = pl.BlockSpec((1, D), lambda i: (0, 0))
    tile = pl.BlockSpec((tm, D), lambda i: (n_t - 1 - i, 0))
    mat = pl.BlockSpec((D, D), lambda i: (0, 0))
    dwspec = pl.BlockSpec((N_DEV, TAPS, 128), lambda i: (0, 0, 0))
    return pl.pallas_call(
        body, name="conv_bwd", grid=(n_t,),
        in_specs=[tile, tile, pl.BlockSpec((tm, D), lambda i: (n_t - 1 - i, 0)), pl.BlockSpec((tm, D), lambda i: (n_t - 1 - i, 1)),
                  pl.BlockSpec((HALO, D), halo_index(0)), pl.BlockSpec((HALO, D), halo_index(1)), dwspec, vec, vec, mat],
        out_specs=[pl.BlockSpec((tm, 2 * D), lambda i: (n_t - 1 - i, 0)), mat,
                   pl.BlockSpec((N_DEV, TAPS + 1, 128), lambda i: (0, 0, 0)), vec, vec, vec],
        out_shape=[pltpu.HBM((seq, 2 * D), BF16), pltpu.HBM((D, D), BF16),
                   pltpu.HBM((N_DEV, TAPS + 1, 128), F32), jax.ShapeDtypeStruct((1, D), F32),
                   jax.ShapeDtypeStruct((1, D), F32), jax.ShapeDtypeStruct((1, D), F32)],
        scratch_shapes=[pltpu.VMEM((HALO + tm, D), F32), pltpu.VMEM((tm + HALO, D), F32), pltpu.VMEM((tm, D), F32),
                        pltpu.VMEM((D, D), F32)],
        compiler_params=_params(("arbitrary",), 48),
    )(*_pinned(dy_conv, cv, proj, proj, proj, proj, dw, ln_g, ln_b, w_conv_out))


def _in_proj_bwd_x(d_glu, dp, dgates, w_in, x, g_pre, dh1, token):
    seq = x.shape[0]
    tm = 512

    def body(dglu_ref, dp_ref, dg_ref, w_ref, x_ref, g_ref, dh1_ref, _token, dx_ref, gg_ref):
        du = _dot_nt(dglu_ref[...], w_ref[:, 0:2 * D])
        du += _dot_nt(dp_ref[...], w_ref[:, 2 * D:3 * D])
        du += _dot_nt(dg_ref[...], w_ref[:, 3 * D:5 * D])
        dnorm, gg = _rms_bwd(du, x_ref[...], g_ref[...])
        dx_ref[...] = dh1_ref[...] + dnorm
        _acc_out(gg_ref, pl.program_id(0) == 0, gg)

    vec = pl.BlockSpec((1, D), lambda i: (0, 0))
    tile = pl.BlockSpec((tm, D), lambda i: (i, 0))
    wide = pl.BlockSpec((tm, 2 * D), lambda i: (i, 0))
    return pl.pallas_call(
        body, name="in_proj_bwd_x", grid=(seq // tm,),
        in_specs=[wide, tile, wide, pl.BlockSpec((D, IN_COLS), lambda i: (0, 0)), tile, vec, tile, ANY],
        out_specs=[tile, vec],
        out_shape=[pltpu.HBM((seq, D), F32), jax.ShapeDtypeStruct((1, D), F32)],
        compiler_params=_params(("arbitrary",), 48),
    )(*_pinned(d_glu, dp, dgates, w_in, x, g_pre, dh1, token))


def _in_proj_bwd_w(u_t, d_glu, dp, dgates, token):
    seq = u_t.shape[1]
    tm = 2048
    n_t = seq // tm

    def body(u_ref, dglu_ref, dp_ref, dg_ref, _token, out_ref, acc):
        b, i = pl.program_id(0), pl.program_id(1)
        ut = u_ref[...]

        def add(d_ref):
            _acc_out(acc, i == 0, _dot(ut, d_ref[...]))

        pl.when(b < 2)(lambda: add(dglu_ref))
        pl.when(b == 2)(lambda: add(dp_ref))
        pl.when(b > 2)(lambda: add(dg_ref))

        @pl.when(i == n_t - 1)
        def _():
            out_ref[...] = acc[...].astype(BF16)

    return pl.pallas_call(
        body, name="in_proj_bwd_w", grid=(IN_COLS // D, n_t),
        in_specs=[pl.BlockSpec((D, tm), lambda b, i: (0, i)),
                  pl.BlockSpec((tm, D), lambda b, i: (jnp.where(b < 2, i, 0), jnp.minimum(b, 1))),
                  pl.BlockSpec((tm, D), lambda b, i: (jnp.where(b == 2, i, 0), 0)),
                  pl.BlockSpec((tm, D), lambda b, i: (jnp.where(b > 2, i, 0), jnp.maximum(b - 3, 0))), ANY],
        out_specs=pl.BlockSpec((D, D), lambda b, i: (0, b)),
        out_shape=pltpu.HBM((D, IN_COLS), BF16),
        scratch_shapes=[pltpu.VMEM((D, D), F32)],
        compiler_params=_params(("arbitrary", "arbitrary"), 52),
    )(*_pinned(u_t, d_glu, dp, dgates, token))


VEC_NAMES = ("mix_pre_g", "dw_bias", "conv_ln_g", "conv_ln_b", "pool_scale", "mix_post_g", "mlp_pre_g", "mlp_post_g")
WEIGHT_ORDER = ("mix_pre_g", "w_in", "dw_kernel", "dw_bias", "conv_ln_g", "conv_ln_b", "w_conv_out", "pool_w",
                "pool_scale", "w_pool_out", "w_o", "mix_post_g", "mlp_pre_g", "w_ff1", "w_ff2", "mlp_post_g")


def _step(x, loss_target, w, m, v):
    row = lambda a: a.reshape(1, D)
    names = [s[0] for s in SHARDED]

    seeded = dict(zip(names, _stage_shards([w[n] for n in names])))
    gather_groups = (("w_in",), ("w_conv_out", "dw_kernel"), ("pool_w", "w_pool_out", "w_o"), ("w_ff1", "w_ff2"))
    first_level, token = _exchange_start(
        "gather_start", [(g, None, [seeded[n] for n in g]) for g in gather_groups], [GATHER_ICI, GATHER_D2D])
    forwarded, full = {}, {}

    def forward(k, after):
        group, (ici_sems, d2d_sems), _, landed = first_level[k]
        _, landed = _exchange_wait("gather_ici_" + group[0], group, None, landed, [(ici_sems, GATHER_ICI)], after)
        (second,), tok = _exchange_start("gather_forward_" + group[0], [(group, None, landed)], [GATHER_FORWARD])
        forwarded[k] = (second[3], [(d2d_sems, GATHER_D2D), (second[1][0], GATHER_FORWARD)])
        return tok

    def gathered(k, after):
        group = gather_groups[k]
        landed, waits = forwarded[k]
        _, arrays = _exchange_wait("gather_wait_" + group[0], group, None, landed, waits, after)
        full.update(zip(group, arrays))

    u, u_t = _input_norm(x, row(w["mix_pre_g"]), token)
    gathered(0, forward(0, u))
    tok = forward(1, full["w_in"])
    proj = _in_proj_fwd(u, full["w_in"], tok)
    gathered(1, proj)
    tok = forward(2, proj)
    cv, y_conv = _conv_fwd(proj, full["dw_kernel"], row(w["dw_bias"]), row(w["conv_ln_g"]), row(w["conv_ln_b"]),
                           full["w_conv_out"], tok)
    gathered(2, y_conv)
    tok = forward(3, y_conv)
    z, zl, y_pool, merged, o, h1 = _pool_merge_fwd(proj, y_conv, x, full["pool_w"], row(w["pool_scale"]),
                                                   full["w_pool_out"], full["w_o"], row(w["mix_post_g"]), tok)
    gathered(3, h1)
    vv, dm, dh2, sse, g_mlp_post = _mlp_fwd(h1, row(w["mlp_pre_g"]), full["w_ff1"], full["w_ff2"],
                                            row(w["mlp_post_g"]), loss_target)

    shard_of = {name: shard for name, _, shard, _ in SHARDED}

    def scatter_start(tag, group, grads):
        landings = [lax.empty((N_DEV - 1,) + shard_of[n], g.dtype) for n, g in zip(group, grads)]
        (handle,), tok = _exchange_start("scatter_start_" + tag, [(group, grads, landings)], [SCATTER])
        return handle, tok

    dv, g_ff1, g_ff2 = _mlp_bwd(vv, dm, full["w_ff1"], full["w_ff2"])
    h_ff, tok_ff = scatter_start("ff", ("w_ff1", "w_ff2"), [g_ff1, g_ff2])
    dh1, dy_conv, dy_pool, dgates, g_wo, g_mlp_pre, g_mix_post = _merge_bwd(
        dh2, dv, h1, row(w["mlp_pre_g"]), o, row(w["mix_post_g"]), full["w_o"], merged, proj, y_conv, y_pool, tok_ff)
    dp, g_wpo, g_pw, g_pool_scale = _pool_bwd(dy_pool, zl, z, full["pool_w"], row(w["pool_scale"]), full["w_pool_out"])
    d_glu, g_wco, g_dw, g_bias, g_ln_g, g_ln_b = _conv_bwd(dy_conv, cv, proj, full["dw_kernel"], row(w["conv_ln_g"]),
                                                            row(w["conv_ln_b"]), full["w_conv_out"])
    h_mix, tok_mix = scatter_start("mix", ("w_o", "w_pool_out", "pool_w", "w_conv_out", "dw_kernel"),
                                   [g_wo, g_wpo, g_pw, g_wco, g_dw[:, :TAPS, :]])
    g_win = _in_proj_bwd_w(u_t, d_glu, dp, dgates, tok_mix)
    h_in, tok_in = scatter_start("in", ("w_in",), [g_win])
    grad_x, g_mix_pre = _in_proj_bwd_x(d_glu, dp, dgates, full["w_in"], x, row(w["mix_pre_g"]), dh1, tok_in)

    out = {}
    slot = jnp.reshape(_my_slot(), (1,)).astype(jnp.int32)
    after = grad_x
    for tag, (group, (sems,), grads, landings) in (("ff", h_ff), ("mix", h_mix), ("in", h_in)):
        if tag == "in":
            vec_parts = after = _vec_exchange(
                [g_mix_pre, g_bias, g_ln_g, g_ln_b, g_pool_scale, g_mix_post, g_mlp_pre, g_mlp_post,
                 jnp.broadcast_to(sse, (1, D))], after)
        mine, landed = _exchange_wait("scatter_wait_" + tag, group, grads, landings, [(sems, SCATTER)], after)
        for name, own, parts in zip(group, mine, landed):
            out[name] = _adamw(name, own, parts, w[name], m[name], v[name], slot, after)
            after = out[name][0]
    stack = lambda d: jnp.stack([d[n] for n in VEC_NAMES], axis=0)
    *res, loss_row = _adamw_vectors(vec_parts, stack(w), stack(m), stack(v))
    for k, name in enumerate(VEC_NAMES):
        out[name] = [r[k] for r in res]
    return loss_row[0, 0], grad_x, out


def kernel(x, mix_pre_g, w_in, dw_kernel, dw_bias, conv_ln_g, conv_ln_b, w_conv_out, pool_w, pool_scale, w_pool_out, w_o, mix_post_g, mlp_pre_g, w_ff1, w_ff2, mlp_post_g, loss_target, m_mix_pre_g, m_w_in, m_dw_kernel, m_dw_bias, m_conv_ln_g, m_conv_ln_b, m_w_conv_out, m_pool_w, m_pool_scale, m_w_pool_out, m_w_o, m_mix_post_g, m_mlp_pre_g, m_w_ff1, m_w_ff2, m_mlp_post_g, v_mix_pre_g, v_w_in, v_dw_kernel, v_dw_bias, v_conv_ln_g, v_conv_ln_b, v_w_conv_out, v_pool_w, v_pool_scale, v_w_pool_out, v_w_o, v_mix_post_g, v_mlp_pre_g, v_w_ff1, v_w_ff2, v_mlp_post_g):
    w = dict(mix_pre_g=mix_pre_g, w_in=w_in, dw_kernel=dw_kernel, dw_bias=dw_bias, conv_ln_g=conv_ln_g, conv_ln_b=conv_ln_b,
             w_conv_out=w_conv_out, pool_w=pool_w, pool_scale=pool_scale, w_pool_out=w_pool_out, w_o=w_o,
             mix_post_g=mix_post_g, mlp_pre_g=mlp_pre_g, w_ff1=w_ff1, w_ff2=w_ff2, mlp_post_g=mlp_post_g)
    m = dict(mix_pre_g=m_mix_pre_g, w_in=m_w_in, dw_kernel=m_dw_kernel, dw_bias=m_dw_bias, conv_ln_g=m_conv_ln_g,
             conv_ln_b=m_conv_ln_b, w_conv_out=m_w_conv_out, pool_w=m_pool_w, pool_scale=m_pool_scale,
             w_pool_out=m_w_pool_out, w_o=m_w_o, mix_post_g=m_mix_post_g, mlp_pre_g=m_mlp_pre_g, w_ff1=m_w_ff1,
             w_ff2=m_w_ff2, mlp_post_g=m_mlp_post_g)
    v = dict(mix_pre_g=v_mix_pre_g, w_in=v_w_in, dw_kernel=v_dw_kernel, dw_bias=v_dw_bias, conv_ln_g=v_conv_ln_g,
             conv_ln_b=v_conv_ln_b, w_conv_out=v_w_conv_out, pool_w=v_pool_w, pool_scale=v_pool_scale,
             w_pool_out=v_w_pool_out, w_o=v_w_o, mix_post_g=v_mix_post_g, mlp_pre_g=v_mlp_pre_g, w_ff1=v_w_ff1,
             w_ff2=v_w_ff2, mlp_post_g=v_mlp_post_g)
    seq = x.shape[1]
    loss, grad_x, out = _step(x.reshape(seq, D), loss_target.reshape(seq, D), w, m, v)
    grads, deltas, new_m, new_v = ([out[n][k] for n in WEIGHT_ORDER] for k in range(4))
    return (loss, grad_x.reshape(x.shape), *grads, *deltas, *new_m, *new_v)
```

```python
import collections

import jax
import jax.numpy as jnp
from jax import lax
from jax.experimental import pallas as pl
from jax.experimental.pallas import tpu as pltpu

D = 1024
D_FF = 4 * D
IN_COLS = 5 * D
TAPS = 31
HALO = 32
POOL_WINDOWS = (2, 4, 8, 16)
PG = D // 4
N_DEV = 8
RMS_EPS = 1e-6
LN_EPS = 1e-5
ADAM_LR, ADAM_B1, ADAM_B2, ADAM_EPS, ADAM_WD, ADAM_STEP = 0.001, 0.9, 0.999, 1e-08, 0.01, 10

BF16 = jnp.bfloat16
F32 = jnp.float32
MIB = 1 << 20
MESH = pl.DeviceIdType.MESH


def _params(sem, vmem_mib):
    return pltpu.CompilerParams(dimension_semantics=sem, vmem_limit_bytes=vmem_mib * MIB)


def _dot(a, b):
    return jnp.dot(a, b, preferred_element_type=F32)


def _dot_nt(a, b):
    return lax.dot_general(a, b, (((1,), (1,)), ((), ())), preferred_element_type=F32)


def _dot_tn(a, b):
    return lax.dot_general(a, b, (((0,), (0,)), ((), ())), preferred_element_type=F32)


def _rms_scale(x):
    return lax.rsqrt(jnp.mean(x * x, axis=-1, keepdims=True) + RMS_EPS)


def _rms_bwd(dy, x, g):
    xn = x * _rms_scale(x)
    dn = dy * g
    dx = _rms_scale(x) * (dn - xn * jnp.mean(dn * xn, axis=-1, keepdims=True))
    return dx, jnp.sum(dy * xn, axis=0, keepdims=True)


def _sigmoid(x):
    return jax.nn.sigmoid(x)


def _acc_out(ref, first, value):
    @pl.when(first)
    def _():
        ref[...] = value

    @pl.when(jnp.logical_not(first))
    def _():
        ref[...] += value


def _my_slot():
    return 4 * lax.axis_index("x") + 2 * lax.axis_index("y") + lax.axis_index("c")


def _peer(mask):
    x, y, c = lax.axis_index("x"), lax.axis_index("y"), lax.axis_index("c")
    return (x ^ ((mask >> 2) & 1), y ^ ((mask >> 1) & 1), c ^ (mask & 1))


def _cols(width):
    return lambda ref, slot: ref.at[:, pl.ds(pl.multiple_of(slot * width, 128), width)]


def _rows(height):
    return lambda ref, slot: ref.at[pl.ds(pl.multiple_of(slot * height, 8), height), :]


def _lead(ref, slot):
    return ref.at[slot]


def _pool_rows(ref, slot):
    return ref.at[:, pl.ds(pl.multiple_of(slot * (PG // N_DEV), 8), PG // N_DEV), :]


SHARDED = (
    ("w_in", (D, IN_COLS), (D, IN_COLS // N_DEV), _cols(IN_COLS // N_DEV)),
    ("w_ff1", (D, D_FF), (D, D_FF // N_DEV), _cols(D_FF // N_DEV)),
    ("w_ff2", (D_FF, D), (D_FF // N_DEV, D), _rows(D_FF // N_DEV)),
    ("w_conv_out", (D, D), (D // N_DEV, D), _rows(D // N_DEV)),
    ("w_pool_out", (D, D), (D // N_DEV, D), _rows(D // N_DEV)),
    ("w_o", (D, D), (D // N_DEV, D), _rows(D // N_DEV)),
    ("pool_w", (4, PG, PG), (4, PG // N_DEV, PG), _pool_rows),
    ("dw_kernel", (N_DEV, TAPS, D // N_DEV), (TAPS, D // N_DEV), _lead),
)
N_SHARDED = len(SHARDED)


SHARD_AT = {name: at for name, _, _, at in SHARDED}
HBM = pl.BlockSpec(memory_space=pltpu.HBM)
SEM = pl.BlockSpec(memory_space=pltpu.SEMAPHORE)
ANY = pl.BlockSpec(memory_space=pl.ANY)
EFFECT = pltpu.SideEffectType.DATAFLOW_SIDE_EFFECTING


def _in_hbm(a):
    return pltpu.with_memory_space_constraint(a, pltpu.HBM)


def _pinned(*arrays):
    return [_in_hbm(a) for a in arrays]


def _stage_shards(shards):
    dtypes = [F32 if name == "dw_kernel" else BF16 for name, *_ in SHARDED]

    def body(*refs):
        ins = refs[:N_SHARDED]
        fulls = refs[N_SHARDED:2 * N_SHARDED]
        raw = refs[2 * N_SHARDED:3 * N_SHARDED]
        stage = refs[3 * N_SHARDED:4 * N_SHARDED]
        in_sems, out_sems = refs[4 * N_SHARDED:]
        me = _my_slot()
        loads = [pltpu.make_async_copy(ins[a], raw[a], in_sems.at[a]) for a in range(N_SHARDED)]
        for cp in loads:
            cp.start()
        stores = []
        for a, (_, _, _, at) in enumerate(SHARDED):
            loads[a].wait()
            stage[a][...] = raw[a][...].astype(dtypes[a])
            cp = pltpu.make_async_copy(stage[a], at(fulls[a], me), out_sems.at[a])
            cp.start()
            stores.append(cp)
        for cp in stores:
            cp.wait()

    return pl.pallas_call(
        body, name="stage_shards",
        out_shape=[pltpu.HBM(full, dt) for (_, full, _, _), dt in zip(SHARDED, dtypes)],
        in_specs=[pl.BlockSpec(memory_space=pl.ANY)] * N_SHARDED,
        out_specs=[pl.BlockSpec(memory_space=pl.ANY)] * N_SHARDED,
        scratch_shapes=[pltpu.VMEM(shard, F32) for _, _, shard, _ in SHARDED]
        + [pltpu.VMEM(shard, dt) for (_, _, shard, _), dt in zip(SHARDED, dtypes)]
        + [pltpu.SemaphoreType.DMA((N_SHARDED,)), pltpu.SemaphoreType.DMA((N_SHARDED,))],
        compiler_params=pltpu.CompilerParams(vmem_limit_bytes=40 * MIB),
    )(*_pinned(*shards))


Leg = collections.namedtuple("Leg", "routes src_of dst_of src_is_land")


def _sem_index(k, m):
    return k * (N_DEV - 1) + m - 1


def _exchange_start(name, groups, legs):
    sizes = [len(g[0]) for g in groups]
    names = [nm for g in groups for nm in g[0]]
    srcs = [s for g in groups if g[1] is not None for s in g[1]]
    lands = [l for g in groups for l in g[2]]
    n_src, n, n_g, n_l = len(srcs), len(names), len(groups), len(legs)

    def body(*refs):
        src_refs, land_refs = list(refs[:n_src]), refs[n_src:n_src + n]
        sems = refs[n_src + n:n_src + n + 2 * n_g * n_l]
        token = refs[-1]
        me = _my_slot()
        first = 0
        for g, size in enumerate(sizes):
            own_src = [src_refs.pop(0) for _ in range(size)] if groups[g][1] is not None else None
            for li, leg in enumerate(legs):
                send, recv = sems[2 * (g * n_l + li)], sems[2 * (g * n_l + li) + 1]
                for m, via in leg.routes:
                    for k in range(size):
                        land = land_refs[first + k]
                        src = land if leg.src_is_land else own_src[k]
                        pltpu.make_async_remote_copy(
                            src_ref=leg.src_of(names[first + k], src, me, m), dst_ref=leg.dst_of(names[first + k], land, me, m),
                            send_sem=send.at[_sem_index(k, m)], recv_sem=recv.at[_sem_index(k, m)],
                            device_id=_peer(via), device_id_type=MESH).start()
            first += size
        token[...] = jnp.zeros_like(token)

    sem_shapes = [pltpu.SemaphoreType.DMA((size * (N_DEV - 1),)) for size in sizes for _ in range(2 * n_l)]
    n_sem = len(sem_shapes)
    outs = pl.pallas_call(
        body, name=name,
        out_shape=sem_shapes + [pltpu.HBM(a.shape, a.dtype) for a in srcs + lands] + [jax.ShapeDtypeStruct((8, 128), F32)],
        in_specs=[HBM] * (n_src + n),
        out_specs=[SEM] * n_sem + [HBM] * (n_src + n) + [pl.BlockSpec(memory_space=pltpu.VMEM)],
        input_output_aliases={k: n_sem + k for k in range(n_src + n)},
        compiler_params=pltpu.CompilerParams(has_side_effects=EFFECT),
    )(*[_in_hbm(a) for a in srcs + lands])
    sems, thru, token = outs[:n_sem], list(outs[n_sem:-1]), outs[-1]
    src_thru, land_thru = thru[:n_src], thru[n_src:]
    handles, first = [], 0
    for g, size in enumerate(sizes):
        pairs = [(sems[2 * (g * n_l + li)], sems[2 * (g * n_l + li) + 1]) for li in range(n_l)]
        mine = [src_thru.pop(0) for _ in range(size)] if groups[g][1] is not None else None
        handles.append((groups[g][0], pairs, mine, land_thru[first:first + size]))
        first += size
    return handles, token


def _exchange_wait(name, names, srcs, lands, waits, after):
    n = len(names)
    n_src = n if srcs is not None else 0

    def body(*refs):
        src_refs, land_refs = refs[:n_src], refs[n_src:n_src + n]
        sems = refs[n_src + n:n_src + n + 2 * len(waits)]
        me = _my_slot()
        for wi, (_, leg) in enumerate(waits):
            for m, via in leg.routes:
                for k in range(n):
                    src = land_refs[k] if leg.src_is_land else src_refs[k]
                    cp = pltpu.make_async_remote_copy(
                        src_ref=leg.src_of(names[k], src, me, m), dst_ref=leg.dst_of(names[k], land_refs[k], me ^ via, m),
                        send_sem=sems[2 * wi].at[_sem_index(k, m)], recv_sem=sems[2 * wi + 1].at[_sem_index(k, m)],
                        device_id=_peer(via), device_id_type=MESH)
                    cp.wait_send()
                    cp.wait_recv()

    arrays = (list(srcs) if srcs is not None else []) + list(lands)
    outs = pl.pallas_call(
        body, name=name,
        out_shape=[pltpu.HBM(a.shape, a.dtype) for a in arrays],
        in_specs=[HBM] * len(arrays) + [SEM] * (2 * len(waits)) + [pl.BlockSpec(memory_space=pl.ANY)],
        out_specs=[HBM] * len(arrays),
        input_output_aliases={k: k for k in range(len(arrays))},
        compiler_params=pltpu.CompilerParams(has_side_effects=EFFECT),
    )(*arrays, *[s for pair, _ in waits for s in pair], after)
    return (outs[:n_src] if srcs is not None else None), outs[n_src:]


def _shard_slot(name, ref, slot):
    return SHARD_AT[name](ref, slot)


GATHER_ICI = Leg(((2, 2), (4, 4), (6, 6)), lambda name, ref, me, m: _shard_slot(name, ref, me),
                 lambda name, ref, sender, m: _shard_slot(name, ref, sender), True)
GATHER_D2D = Leg(((1, 1),), GATHER_ICI.src_of, GATHER_ICI.dst_of, True)
GATHER_FORWARD = Leg(((2, 1), (4, 1), (6, 1)), lambda name, ref, me, m: _shard_slot(name, ref, me ^ m),
                     lambda name, ref, sender, m: _shard_slot(name, ref, sender ^ m), True)
SCATTER = Leg(tuple((m, m) for m in range(1, N_DEV)), lambda name, ref, me, m: _shard_slot(name, ref, me ^ m),
              lambda name, ref, sender, m: ref.at[m - 1], False)


def _vec_exchange(vectors, after):
    n = len(vectors)

    def body(*refs):
        vec, vec_out = refs[n + 2], refs[n + 1]
        send_sems, recv_sems, local_sem = refs[n + 3:]
        for k in range(n):
            vec[k:k + 1, :] = refs[k][...]
        me = _my_slot()
        local = pltpu.make_async_copy(vec, vec_out.at[me], local_sem)
        local.start()
        sends = []
        for mask in range(1, N_DEV):
            cp = pltpu.make_async_remote_copy(
                src_ref=vec, dst_ref=vec_out.at[me], send_sem=send_sems.at[mask - 1],
                recv_sem=recv_sems.at[mask - 1], device_id=_peer(mask), device_id_type=MESH)
            cp.start()
            sends.append(cp)
        for mask in range(1, N_DEV):
            pltpu.make_async_remote_copy(
                src_ref=vec, dst_ref=vec_out.at[me ^ mask], send_sem=send_sems.at[mask - 1],
                recv_sem=recv_sems.at[mask - 1], device_id=_peer(mask), device_id_type=MESH).wait_recv()
        for cp in sends:
            cp.wait_send()
        local.wait()

    return pl.pallas_call(
        body, name="vec_exchange",
        out_shape=jax.ShapeDtypeStruct((N_DEV, n, D), F32),
        in_specs=[pl.BlockSpec(memory_space=pltpu.VMEM)] * n + [ANY],
        out_specs=pl.BlockSpec(memory_space=pl.ANY),
        scratch_shapes=[pltpu.VMEM((n, D), F32), pltpu.SemaphoreType.DMA((N_DEV - 1,)),
                        pltpu.SemaphoreType.DMA((N_DEV - 1,)), pltpu.SemaphoreType.DMA],
    )(*vectors, after)


def _adamw_update(g, w_ref, m_ref, v_ref, g_out, d_out, m_out, v_out):
    m_new = ADAM_B1 * m_ref[...] + (1.0 - ADAM_B1) * g
    v_new = ADAM_B2 * v_ref[...] + (1.0 - ADAM_B2) * (g * g)
    m_hat = m_new / (1.0 - ADAM_B1 ** ADAM_STEP)
    v_hat = v_new / (1.0 - ADAM_B2 ** ADAM_STEP)
    g_out[...] = g
    d_out[...] = -ADAM_LR * (m_hat / (jnp.sqrt(v_hat) + ADAM_EPS) + ADAM_WD * w_ref[...])
    m_out[...] = m_new
    v_out[...] = v_new


ADAMW_ROWS = 256


def _adamw(name, own, parts, w, m, v, slot, token):
    shard = w.shape
    if name in ("w_in", "w_ff1"):
        tr = ADAMW_ROWS
        grid = (shard[0] // tr,)
        own_spec = pl.BlockSpec((tr, shard[1]), lambda i, s: (i, s[0]))
        blk = pl.BlockSpec((tr, shard[1]), lambda i, s: (i, 0))
        parts_spec = pl.BlockSpec((N_DEV - 1, tr, shard[1]), lambda i, s: (0, i, 0))
    elif name == "pool_w":
        grid = (shard[0],)
        own_spec = pl.BlockSpec((None,) + shard[1:], lambda g, s: (g, s[0], 0))
        blk = pl.BlockSpec((None,) + shard[1:], lambda g, s: (g, 0, 0))
        parts_spec = pl.BlockSpec((N_DEV - 1, None) + shard[1:], lambda g, s: (0, g, 0, 0))
    elif name == "dw_kernel":
        grid = (1,)
        own_spec = pl.BlockSpec((None,) + shard, lambda i, s: (s[0], 0, 0))
        blk = pl.BlockSpec(shard, lambda i, s: (0, 0))
        parts_spec = pl.BlockSpec((N_DEV - 1,) + shard, lambda i, s: (0, 0, 0))
    else:
        tr = min(ADAMW_ROWS, shard[0])
        grid = (shard[0] // tr,)
        own_spec = pl.BlockSpec((tr, shard[1]), lambda i, s: (s[0] * grid[0] + i, 0))
        blk = pl.BlockSpec((tr, shard[1]), lambda i, s: (i, 0))
        parts_spec = pl.BlockSpec((N_DEV - 1, tr, shard[1]), lambda i, s: (0, i, 0))

    def body(slot_ref, own_ref, p_ref, w_ref, m_ref, v_ref, _token, g_out, d_out, m_out, v_out):
        g = own_ref[...].astype(F32)
        for k in range(N_DEV - 1):
            g = g + p_ref[k].astype(F32)
        _adamw_update(g, w_ref, m_ref, v_ref, g_out, d_out, m_out, v_out)

    return pl.pallas_call(
        body, name="adamw_" + name,
        grid_spec=pltpu.PrefetchScalarGridSpec(
            num_scalar_prefetch=1, grid=grid, in_specs=[own_spec, parts_spec, blk, blk, blk, ANY], out_specs=[blk] * 4),
        out_shape=[jax.ShapeDtypeStruct(shard, F32)] * 4,
        compiler_params=_params(("arbitrary",), 32),
    )(slot, *_pinned(own, parts, w, m, v, token))


def _adamw_vectors(parts, w, m, v):
    n_vec = w.shape[0]

    def body(p_ref, w_ref, m_ref, v_ref, g_out, d_out, m_out, v_out, loss_out):
        total = p_ref[0]
        for s in range(1, N_DEV):
            total = total + p_ref[s]
        _adamw_update(total[0:n_vec], w_ref, m_ref, v_ref, g_out, d_out, m_out, v_out)
        loss_out[...] = total[n_vec:n_vec + 1] * (0.5 / D)

    return pl.pallas_call(
        body, name="adamw_vectors",
        out_shape=[jax.ShapeDtypeStruct(w.shape, F32)] * 4 + [jax.ShapeDtypeStruct((1, D), F32)],
    )(parts, w, m, v)


def _input_norm(x, g_pre, token):
    seq = x.shape[0]
    tm = 1024

    def body(x_ref, g_ref, _token, u_ref, ut_ref):
        xf = x_ref[...]
        u = (xf * _rms_scale(xf) * g_ref[...]).astype(BF16)
        u_ref[...] = u
        ut_ref[...] = u.T

    return pl.pallas_call(
        body, name="input_norm", grid=(seq // tm,),
        in_specs=[pl.BlockSpec((tm, D), lambda i: (i, 0)), pl.BlockSpec((1, D), lambda i: (0, 0)), ANY],
        out_specs=[pl.BlockSpec((tm, D), lambda i: (i, 0)), pl.BlockSpec((D, tm), lambda i: (0, i))],
        out_shape=[pltpu.HBM((seq, D), BF16), pltpu.HBM((D, seq), BF16)],
        compiler_params=_params(("arbitrary",), 40),
    )(*_pinned(x, g_pre, token))


def _in_proj_fwd(u, w_in, token):
    seq = u.shape[0]
    tm, tn = 1024, IN_COLS // 4

    def body(u_ref, w_ref, _token, proj_ref):
        proj_ref[...] = _dot(u_ref[...], w_ref[...]).astype(BF16)

    return pl.pallas_call(
        body, name="in_proj_fwd", grid=(seq // tm, IN_COLS // tn),
        in_specs=[pl.BlockSpec((tm, D), lambda i, j: (i, 0)), pl.BlockSpec((D, tn), lambda i, j: (0, j)), ANY],
        out_specs=pl.BlockSpec((tm, tn), lambda i, j: (i, j)),
        out_shape=pltpu.HBM((seq, IN_COLS), BF16),
        compiler_params=_params(("arbitrary", "arbitrary"), 40),
    )(*_pinned(u, w_in, token))


CONV_TM = 512
CONV_RS = 128


def _shifts():
    return [(b, [(a, 8 * a + b) for a in range(4) if 8 * a + b < TAPS]) for b in range(8)]


def _taps_looking_back(buf, row0, lanes, weight):
    acc = None
    for b, group in _shifts():
        part = None
        for a, s in group:
            term = weight(TAPS - 1 - s) * buf[pl.ds(row0 - 8 - 8 * a, CONV_RS + 8), lanes]
            part = term if part is None else part + term
        if b:
            part = pltpu.roll(part, b, 0)
        acc = part[8:, :] if acc is None else acc + part[8:, :]
    return acc


def _taps_looking_ahead(buf, row0, lanes, weight):
    acc = None
    for b, group in _shifts():
        part = None
        for a, s in group:
            term = weight(TAPS - 1 - s) * buf[pl.ds(row0 + 8 * a, CONV_RS + 8), lanes]
            part = term if part is None else part + term
        if b:
            part = pltpu.roll(part, CONV_RS + 8 - b, 0)
        acc = part[:CONV_RS, :] if acc is None else acc + part[:CONV_RS, :]
    return acc


def _layer_norm_parts(cv):
    mu = jnp.mean(cv, axis=-1, keepdims=True)
    cen = cv - mu
    rstd = lax.rsqrt(jnp.mean(cen * cen, axis=-1, keepdims=True) + LN_EPS)
    return cen * rstd, rstd


def _conv_fwd(proj, dw, dw_bias, ln_g, ln_b, w_conv_out, token):
    seq = proj.shape[0]
    tm = CONV_TM

    def body(a_ref, gate_ref, dw_ref, bias_ref, lg_ref, lb_ref, w_ref, _token, cv_ref, y_ref, ug):
        i = pl.program_id(0)

        @pl.when(i == 0)
        def _():
            ug[0:HALO, :] = jnp.zeros((HALO, D), F32)

        @pl.when(i > 0)
        def _():
            ug[0:HALO, :] = ug[tm:tm + HALO, :]
        ug[HALO:HALO + tm, :] = a_ref[...].astype(F32) * _sigmoid(gate_ref[...].astype(F32))

        def channel_block(cb, carry):
            lanes = pl.ds(pl.multiple_of(cb * 128, 128), 128)
            for r0 in range(0, tm, CONV_RS):
                taps = _taps_looking_back(ug, HALO + r0, lanes, lambda k: dw_ref[cb, k:k + 1, :])
                cv_ref[pl.ds(r0, CONV_RS), lanes] = taps + bias_ref[:, lanes]
            return carry
        lax.fori_loop(0, D // 128, channel_block, 0)

        n, _ = _layer_norm_parts(cv_ref[...])
        ln = n * lg_ref[...] + lb_ref[...]
        y_ref[...] = _dot((ln * _sigmoid(ln)).astype(BF16), w_ref[...]).astype(BF16)

    vec = pl.BlockSpec((1, D), lambda i: (0, 0))
    tile = pl.BlockSpec((tm, D), lambda i: (i, 0))
    return pl.pallas_call(
        body, name="conv_fwd", grid=(seq // tm,),
        in_specs=[pl.BlockSpec((tm, D), lambda i: (i, 0)), pl.BlockSpec((tm, D), lambda i: (i, 1)),
                  pl.BlockSpec((N_DEV, TAPS, 128), lambda i: (0, 0, 0)), vec, vec, vec,
                  pl.BlockSpec((D, D), lambda i: (0, 0)), ANY],
        out_specs=[tile, tile],
        out_shape=[pltpu.HBM((seq, D), F32), pltpu.HBM((seq, D), BF16)],
        scratch_shapes=[pltpu.VMEM((HALO + tm, D), F32)],
        compiler_params=_params(("arbitrary",), 32),
    )(*_pinned(proj, proj, dw, dw_bias, ln_g, ln_b, w_conv_out, token))


def _window_sums(rows, window, back):
    n = rows.shape[0]
    span = 1
    while span < window:
        rows = rows + pltpu.roll(rows, span if back else n - span, 0)
        span *= 2
    return rows


def _pool_counts(tile_index, tm, window):
    t = tile_index * tm + lax.broadcasted_iota(jnp.int32, (tm, 1), 0)
    return 1.0 / jnp.minimum(t + 1, window).astype(F32)


def _pool_merge_fwd(proj, y_conv, x, pool_w, pool_scale, w_pool_out, w_o, g_post, token):
    seq = proj.shape[0]
    tm = CONV_TM

    def body(p_ref, gc_ref, gp_ref, yc_ref, x_ref, pw_ref, ps_ref, wpo_ref, wo_ref, g_ref, _token,
             z_ref, zl_ref, yp_ref, mg_ref, o_ref, h1_ref, pbuf):
        i = pl.program_id(0)

        @pl.when(i == 0)
        def _():
            pbuf[0:HALO, :] = jnp.zeros((HALO, D), F32)

        @pl.when(i > 0)
        def _():
            pbuf[0:HALO, :] = pbuf[tm:tm + HALO, :]
        pbuf[HALO:HALO + tm, :] = p_ref[...].astype(F32)

        for g, window in enumerate(POOL_WINDOWS):
            lanes = pl.ds(g * PG, PG)
            acc = _window_sums(pbuf[:, lanes], window, back=True)[HALO:, :]
            zg = acc * _pool_counts(i, tm, window) - pbuf[pl.ds(HALO, tm), lanes]
            z_ref[:, lanes] = zg.astype(BF16)
            zl_ref[:, lanes] = _dot(zg.astype(BF16), pw_ref[g])
        zl = zl_ref[...]
        y_pool = _dot((zl * ps_ref[...]).astype(BF16), wpo_ref[...])
        yp_ref[...] = y_pool.astype(BF16)
        merged = (_sigmoid(gc_ref[...].astype(F32)) * yc_ref[...].astype(F32)
                  + _sigmoid(gp_ref[...].astype(F32)) * y_pool).astype(BF16)
        mg_ref[...] = merged
        o = _dot(merged, wo_ref[...])
        o_ref[...] = o
        h1_ref[...] = x_ref[...] + o * _rms_scale(o) * g_ref[...]

    vec = pl.BlockSpec((1, D), lambda i: (0, 0))
    tile = pl.BlockSpec((tm, D), lambda i: (i, 0))
    mat = pl.BlockSpec((D, D), lambda i: (0, 0))
    return pl.pallas_call(
        body, name="pool_merge_fwd", grid=(seq // tm,),
        in_specs=[pl.BlockSpec((tm, D), lambda i: (i, 2)), pl.BlockSpec((tm, D), lambda i: (i, 3)),
                  pl.BlockSpec((tm, D), lambda i: (i, 4)), tile, tile,
                  pl.BlockSpec((4, PG, PG), lambda i: (0, 0, 0)), vec, mat, mat, vec, ANY],
        out_specs=[tile] * 6,
        out_shape=[pltpu.HBM((seq, D), dt) for dt in (BF16, F32, BF16, BF16, F32, F32)],
        scratch_shapes=[pltpu.VMEM((HALO + tm, D), F32)],
        compiler_params=_params(("arbitrary",), 48),
    )(*_pinned(proj, proj, proj, y_conv, x, pool_w, pool_scale, w_pool_out, w_o, g_post, token))


def _mlp_fwd(h1, g_pre, w_ff1, w_ff2, g_post, target):
    seq = h1.shape[0]
    tm, tf = 1024, D_FF // N_DEV
    n_f = D_FF // tf

    def body(h1_ref, gpre_ref, w1_ref, w2_ref, gpost_ref, tgt_ref, v_ref, dm_ref, dh2_ref, sse_ref, ggrad_ref, macc):
        i, j = pl.program_id(0), pl.program_id(1)

        @pl.when(j == 0)
        def _():
            h = h1_ref[...]
            v_ref[...] = (h * _rms_scale(h) * gpre_ref[...]).astype(BF16)
        f = jnp.maximum(_dot(v_ref[...], w1_ref[...]), 0.0)
        part = _dot((f * f).astype(BF16), w2_ref[...])

        @pl.when(j == 0)
        def _():
            macc[...] = part

        @pl.when(j > 0)
        def _():
            macc[...] += part

        @pl.when(j == n_f - 1)
        def _():
            mo = macc[...]
            err = h1_ref[...] + mo * _rms_scale(mo) * gpost_ref[...] - tgt_ref[...]
            dh2 = err * (1.0 / D)
            dh2_ref[...] = dh2
            dm, ggrad = _rms_bwd(dh2, mo, gpost_ref[...])
            dm_ref[...] = dm.astype(BF16)
            _acc_out(ggrad_ref, i == 0, ggrad)
            _acc_out(sse_ref, i == 0, jnp.sum(jnp.sum(err * err, axis=1, keepdims=True), axis=0, keepdims=True))

    vec = pl.BlockSpec((1, D), lambda i, j: (0, 0))
    tile = pl.BlockSpec((tm, D), lambda i, j: (i, 0))
    return pl.pallas_call(
        body, name="mlp_fwd", grid=(seq // tm, n_f),
        in_specs=[tile, vec, pl.BlockSpec((D, tf), lambda i, j: (0, j)), pl.BlockSpec((tf, D), lambda i, j: (j, 0)), vec, tile],
        out_specs=[tile, tile, tile, pl.BlockSpec((1, 1), lambda i, j: (0, 0)), vec],
        out_shape=[pltpu.HBM((seq, D), BF16), pltpu.HBM((seq, D), BF16),
                   pltpu.HBM((seq, D), F32), jax.ShapeDtypeStruct((1, 1), F32),
                   jax.ShapeDtypeStruct((1, D), F32)],
        scratch_shapes=[pltpu.VMEM((tm, D), F32)],
        compiler_params=_params(("arbitrary", "arbitrary"), 56),
    )(*_pinned(h1, g_pre, w_ff1, w_ff2, g_post, target))


def _mlp_bwd(v, dm, w_ff1, w_ff2):
    seq = v.shape[0]
    tm, tf = 1024, D_FF // N_DEV
    n_t = seq // tm

    def body(v_ref, dm_ref, w1_ref, w2_ref, dv_hbm, g1_ref, g2_ref, dv_acc, g1_acc, g2_acc, sem):
        j, i = pl.program_id(0), pl.program_id(1)
        vt, dmt = v_ref[...], dm_ref[...]
        f = jnp.maximum(_dot(vt, w1_ref[...]), 0.0)
        df = (_dot_nt(dmt, w2_ref[...]) * (2.0 * f)).astype(BF16)
        rows = pl.ds(pl.multiple_of(i * tm, tm), tm)
        dv_part = _dot_nt(df, w1_ref[...])

        @pl.when(j == 0)
        def _():
            dv_acc[rows, :] = dv_part

        @pl.when(j > 0)
        def _():
            dv_acc[rows, :] += dv_part
        g1_part = _dot_tn(vt, df)
        g2_part = _dot_tn((f * f).astype(BF16), dmt)

        @pl.when(i == 0)
        def _():
            g1_acc[...] = g1_part
            g2_acc[...] = g2_part

        @pl.when(i > 0)
        def _():
            g1_acc[...] += g1_part
            g2_acc[...] += g2_part

        @pl.when(i == n_t - 1)
        def _():
            g1_ref[...] = g1_acc[...].astype(BF16)
            g2_ref[...] = g2_acc[...].astype(BF16)

        @pl.when(jnp.logical_and(i == n_t - 1, j == D_FF // tf - 1))
        def _():
            cp = pltpu.make_async_copy(dv_acc, dv_hbm, sem)
            cp.start()
            cp.wait()

    tile = pl.BlockSpec((tm, D), lambda j, i: (i, 0))
    return pl.pallas_call(
        body, name="mlp_bwd", grid=(D_FF // tf, n_t),
        in_specs=[tile, tile, pl.BlockSpec((D, tf), lambda j, i: (0, j)), pl.BlockSpec((tf, D), lambda j, i: (j, 0))],
        out_specs=[pl.BlockSpec(memory_space=pl.ANY), pl.BlockSpec((D, tf), lambda j, i: (0, j)),
                   pl.BlockSpec((tf, D), lambda j, i: (j, 0))],
        out_shape=[pltpu.HBM((seq, D), F32), pltpu.HBM((D, D_FF), BF16),
                   pltpu.HBM((D_FF, D), BF16)],
        scratch_shapes=[pltpu.VMEM((seq, D), F32), pltpu.VMEM((D, tf), F32), pltpu.VMEM((tf, D), F32),
                        pltpu.SemaphoreType.DMA],
        compiler_params=_params(("arbitrary", "arbitrary"), 52),
    )(*_pinned(v, dm, w_ff1, w_ff2))


def _merge_bwd(dh2, dv, h1, g_mlp_pre, o, g_mix_post, w_o, merged, proj, y_conv, y_pool, token):
    seq = dh2.shape[0]
    tm = 256
    n_t = seq // tm

    def body(dh2_ref, dv_ref, h1_ref, gpre_ref, o_ref, gpost_ref, wo_ref, mg_ref, gc_ref, gp_ref, yc_ref, yp_ref, _token,
             dh1_ref, dyc_ref, dyp_ref, dg_ref, gwo_ref, ggpre_ref, ggpost_ref, gwo_acc):
        i = pl.program_id(0)
        dnorm, ggpre = _rms_bwd(dv_ref[...], h1_ref[...], gpre_ref[...])
        dh1 = dh2_ref[...] + dnorm
        dh1_ref[...] = dh1
        do, ggpost = _rms_bwd(dh1, o_ref[...], gpost_ref[...])
        do = do.astype(BF16)
        _acc_out(ggpre_ref, i == 0, ggpre)
        _acc_out(ggpost_ref, i == 0, ggpost)
        _acc_out(gwo_acc, i == 0, _dot_tn(mg_ref[...], do))
        dmerged = _dot_nt(do, wo_ref[...])
        sc, sp = _sigmoid(gc_ref[...].astype(F32)), _sigmoid(gp_ref[...].astype(F32))
        dyc_ref[...] = (dmerged * sc).astype(BF16)
        dyp_ref[...] = (dmerged * sp).astype(BF16)
        dg_ref[:, 0:D] = (dmerged * yc_ref[...].astype(F32) * (sc * (1.0 - sc))).astype(BF16)
        dg_ref[:, D:2 * D] = (dmerged * yp_ref[...].astype(F32) * (sp * (1.0 - sp))).astype(BF16)

        @pl.when(i == n_t - 1)
        def _():
            gwo_ref[...] = gwo_acc[...].astype(BF16)

    vec = pl.BlockSpec((1, D), lambda i: (0, 0))
    tile = pl.BlockSpec((tm, D), lambda i: (i, 0))
    mat = pl.BlockSpec((D, D), lambda i: (0, 0))
    return pl.pallas_call(
        body, name="merge_bwd", grid=(n_t,),
        in_specs=[tile, tile, tile, vec, tile, vec, mat, tile,
                  pl.BlockSpec((tm, D), lambda i: (i, 3)), pl.BlockSpec((tm, D), lambda i: (i, 4)), tile, tile, ANY],
        out_specs=[tile, tile, tile, pl.BlockSpec((tm, 2 * D), lambda i: (i, 0)), mat, vec, vec],
        out_shape=[pltpu.HBM((seq, D), F32), pltpu.HBM((seq, D), BF16),
                   pltpu.HBM((seq, D), BF16), pltpu.HBM((seq, 2 * D), BF16),
                   pltpu.HBM((D, D), BF16), jax.ShapeDtypeStruct((1, D), F32),
                   jax.ShapeDtypeStruct((1, D), F32)],
        scratch_shapes=[pltpu.VMEM((D, D), F32)],
        compiler_params=_params(("arbitrary",), 48),
    )(*_pinned(dh2, dv, h1, g_mlp_pre, o, g_mix_post, w_o, merged, proj, proj, y_conv, y_pool, token))


def _pool_bwd(dy_pool, zl, z, pool_w, pool_scale, w_pool_out, token):
    seq = dy_pool.shape[0]
    tm = CONV_TM
    n_t = seq // tm

    def body(dy_ref, zl_ref, z_ref, pw_ref, ps_ref, wpo_ref, _token,
             dp_ref, gwpo_ref, gpw_ref, gps_ref, qbuf, gwpo_acc, gpw_acc):
        i = pl.program_id(0)
        tile_index = n_t - 1 - i
        first = i == 0
        dy = dy_ref[...]
        zl = zl_ref[...]
        dzs = _dot_nt(dy, wpo_ref[...])
        _acc_out(gwpo_acc, first, _dot_tn((zl * ps_ref[...]).astype(BF16), dy))
        _acc_out(gps_ref, first, jnp.sum(dzs * zl, axis=0, keepdims=True))
        dzl = (dzs * ps_ref[...]).astype(BF16)

        @pl.when(first)
        def _():
            qbuf[tm:tm + HALO, :] = jnp.zeros((HALO, D), F32)

        @pl.when(jnp.logical_not(first))
        def _():
            qbuf[tm:tm + HALO, :] = qbuf[0:HALO, :]

        dzs_list = []
        for g, window in enumerate(POOL_WINDOWS):
            lanes = pl.ds(g * PG, PG)
            dzl_g = dzl[:, g * PG:(g + 1) * PG]
            dz = _dot_nt(dzl_g, pw_ref[g])
            _acc_out(gpw_acc.at[g], first, _dot_tn(z_ref[:, lanes], dzl_g))
            qbuf[pl.ds(0, tm), lanes] = dz * _pool_counts(tile_index, tm, window)
            dzs_list.append(dz)
        for g, window in enumerate(POOL_WINDOWS):
            lanes = pl.ds(g * PG, PG)
            acc = _window_sums(qbuf[:, lanes], window, back=False)[:tm, :]
            dp_ref[:, lanes] = (acc - dzs_list[g]).astype(BF16)

        @pl.when(i == n_t - 1)
        def _():
            gwpo_ref[...] = gwpo_acc[...].astype(BF16)
            gpw_ref[...] = gpw_acc[...].astype(BF16)

    vec = pl.BlockSpec((1, D), lambda i: (0, 0))
    tile = pl.BlockSpec((tm, D), lambda i: (n_t - 1 - i, 0))
    mat = pl.BlockSpec((D, D), lambda i: (0, 0))
    pw = pl.BlockSpec((4, PG, PG), lambda i: (0, 0, 0))
    return pl.pallas_call(
        body, name="pool_bwd", grid=(n_t,),
        in_specs=[tile, tile, tile, pw, vec, mat, ANY],
        out_specs=[tile, mat, pw, vec],
        out_shape=[pltpu.HBM((seq, D), BF16), pltpu.HBM((D, D), BF16),
                   pltpu.HBM((4, PG, PG), BF16), jax.ShapeDtypeStruct((1, D), F32)],
        scratch_shapes=[pltpu.VMEM((tm + HALO, D), F32), pltpu.VMEM((D, D), F32), pltpu.VMEM((4, PG, PG), F32)],
        compiler_params=_params(("arbitrary",), 40),
    )(*_pinned(dy_pool, zl, z, pool_w, pool_scale, w_pool_out, token))


def _conv_bwd(dy_conv, cv, proj, dw, ln_g, ln_b, w_conv_out, token):
    seq = dy_conv.shape[0]
    tm = CONV_TM // 2
    n_t = seq // tm
    halo_blocks = tm // HALO

    def body(dy_ref, cv_ref, a_ref, gate_ref, ah_ref, gh_ref, dw_ref, lg_ref, lb_ref, w_ref, _token,
             dglu_ref, gw_ref, gdw_ref, gbias_ref, glg_ref, glb_ref, ug, dcv, dug, gw_acc):
        i = pl.program_id(0)
        tile_index = n_t - 1 - i
        first = i == 0
        dy = dy_ref[...]
        n, rstd = _layer_norm_parts(cv_ref[...])
        ln = n * lg_ref[...] + lb_ref[...]
        sg = _sigmoid(ln)
        _acc_out(gw_acc, first, _dot_tn((ln * sg).astype(BF16), dy))
        dln = _dot_nt(dy, w_ref[...]) * (sg * (1.0 + ln * (1.0 - sg)))
        _acc_out(glg_ref, first, jnp.sum(dln * n, axis=0, keepdims=True))
        _acc_out(glb_ref, first, jnp.sum(dln, axis=0, keepdims=True))
        dn = dln * lg_ref[...]
        dcv_tile = rstd * (dn - jnp.mean(dn, axis=-1, keepdims=True) - n * jnp.mean(dn * n, axis=-1, keepdims=True))
        _acc_out(gbias_ref, first, jnp.sum(dcv_tile, axis=0, keepdims=True))

        @pl.when(first)
        def _():
            dcv[tm:tm + HALO, :] = jnp.zeros((HALO, D), F32)

        @pl.when(jnp.logical_not(first))
        def _():
            dcv[tm:tm + HALO, :] = dcv[0:HALO, :]
        dcv[0:tm, :] = dcv_tile

        a, gate = a_ref[...].astype(F32), gate_ref[...].astype(F32)
        sgate = _sigmoid(gate)
        ug[HALO:HALO + tm, :] = a * sgate
        before = jnp.where(tile_index > 0, 1.0, 0.0)
        ug[0:HALO, :] = ah_ref[...].astype(F32) * _sigmoid(gh_ref[...].astype(F32)) * before

        @pl.when(first)
        def _():
            gdw_ref[...] = jnp.zeros((N_DEV, TAPS + 1, 128), F32)

        def channel_block(cb, carry):
            lanes = pl.ds(pl.multiple_of(cb * 128, 128), 128)
            for r0 in range(0, tm, CONV_RS):
                dug[pl.ds(r0, CONV_RS), lanes] = _taps_looking_ahead(dcv, r0, lanes, lambda k: dw_ref[cb, k:k + 1, :])
            for b, group in _shifts():
                sums = [jnp.zeros((8, 128), F32) for _ in group]
                for r0 in range(0, tm, CONV_RS):
                    window = ug[pl.ds(r0, CONV_RS + HALO), lanes]
                    if b:
                        window = pltpu.roll(window, b, 0)
                    d = dcv[pl.ds(r0, CONV_RS), lanes]
                    for n_a, (a, s) in enumerate(group):
                        prod = d * window[HALO - 8 * a:HALO - 8 * a + CONV_RS, :]
                        sums[n_a] = sums[n_a] + jnp.sum(prod.reshape(CONV_RS // 8, 8, 128), axis=0)
                for n_a, (a, s) in enumerate(group):
                    k = TAPS - 1 - s
                    gdw_ref[cb, k:k + 1, :] += jnp.sum(sums[n_a], axis=0, keepdims=True)
            return carry
        lax.fori_loop(0, D // 128, channel_block, 0)

        d_ug = dug[...]
        dglu_ref[:, 0:D] = (d_ug * sgate).astype(BF16)
        dglu_ref[:, D:2 * D] = (d_ug * a * (sgate * (1.0 - sgate))).astype(BF16)

        @pl.when(i == n_t - 1)
        def _():
            gw_ref[...] = gw_acc[...].astype(BF16)

    def halo_index(col):
        return lambda i: (jnp.maximum((n_t - 1 - i) * halo_blocks - 1, 0), col)

    vec = pl.BlockSpec((1, D), lambda i: (0, 0))
    tile = pl.BlockSpec((tm, D), lambda i: (n_t - 1 - i, 0))
    mat = pl.BlockSpec((D, D), lambda i: (0, 0))
    dwspec = pl.BlockSpec((N_DEV, TAPS, 128), lambda i: (0, 0, 0))
    return pl.pallas_call(
        body, name="conv_bwd", grid=(n_t,),
        in_specs=[tile, tile, pl.BlockSpec((tm, D), lambda i: (n_t - 1 - i, 0)), pl.BlockSpec((tm, D), lambda i: (n_t - 1 - i, 1)),
                  pl.BlockSpec((HALO, D), halo_index(0)), pl.BlockSpec((HALO, D), halo_index(1)), dwspec, vec, vec, mat,
                  ANY],
        out_specs=[pl.BlockSpec((tm, 2 * D), lambda i: (n_t - 1 - i, 0)), mat,
                   pl.BlockSpec((N_DEV, TAPS + 1, 128), lambda i: (0, 0, 0)), vec, vec, vec],
        out_shape=[pltpu.HBM((seq, 2 * D), BF16), pltpu.HBM((D, D), BF16),
                   pltpu.HBM((N_DEV, TAPS + 1, 128), F32), jax.ShapeDtypeStruct((1, D), F32),
                   jax.ShapeDtypeStruct((1, D), F32), jax.ShapeDtypeStruct((1, D), F32)],
        scratch_shapes=[pltpu.VMEM((HALO + tm, D), F32), pltpu.VMEM((tm + HALO, D), F32), pltpu.VMEM((tm, D), F32),
                        pltpu.VMEM((D, D), F32)],
        compiler_params=_params(("arbitrary",), 48),
    )(*_pinned(dy_conv, cv, proj, proj, proj, proj, dw, ln_g, ln_b, w_conv_out, token))


def _in_proj_bwd_x(d_glu, dp, dgates, w_in, x, g_pre, dh1, token):
    seq = x.shape[0]
    tm = 512

    def body(dglu_ref, dp_ref, dg_ref, w_ref, x_ref, g_ref, dh1_ref, _token, dx_ref, gg_ref):
        du = _dot_nt(dglu_ref[...], w_ref[:, 0:2 * D])
        du += _dot_nt(dp_ref[...], w_ref[:, 2 * D:3 * D])
        du += _dot_nt(dg_ref[...], w_ref[:, 3 * D:5 * D])
        dnorm, gg = _rms_bwd(du, x_ref[...], g_ref[...])
        dx_ref[...] = dh1_ref[...] + dnorm
        _acc_out(gg_ref, pl.program_id(0) == 0, gg)

    vec = pl.BlockSpec((1, D), lambda i: (0, 0))
    tile = pl.BlockSpec((tm, D), lambda i: (i, 0))
    wide = pl.BlockSpec((tm, 2 * D), lambda i: (i, 0))
    return pl.pallas_call(
        body, name="in_proj_bwd_x", grid=(seq // tm,),
        in_specs=[wide, tile, wide, pl.BlockSpec((D, IN_COLS), lambda i: (0, 0)), tile, vec, tile, ANY],
        out_specs=[tile, vec],
        out_shape=[pltpu.HBM((seq, D), F32), jax.ShapeDtypeStruct((1, D), F32)],
        compiler_params=_params(("arbitrary",), 48),
    )(*_pinned(d_glu, dp, dgates, w_in, x, g_pre, dh1, token))


def _in_proj_bwd_w(u_t, d_glu, dp, dgates, token):
    seq = u_t.shape[1]
    tm = 2048
    n_t = seq // tm

    def body(u_ref, dglu_ref, dp_ref, dg_ref, _token, out_ref, acc):
        b, i = pl.program_id(0), pl.program_id(1)
        ut = u_ref[...]

        def add(d_ref):
            _acc_out(acc, i == 0, _dot(ut, d_ref[...]))

        pl.when(b < 2)(lambda: add(dglu_ref))
        pl.when(b == 2)(lambda: add(dp_ref))
        pl.when(b > 2)(lambda: add(dg_ref))

        @pl.when(i == n_t - 1)
        def _():
            out_ref[...] = acc[...].astype(BF16)

    return pl.pallas_call(
        body, name="in_proj_bwd_w", grid=(IN_COLS // D, n_t),
        in_specs=[pl.BlockSpec((D, tm), lambda b, i: (0, i)),
                  pl.BlockSpec((tm, D), lambda b, i: (jnp.where(b < 2, i, 0), jnp.minimum(b, 1))),
                  pl.BlockSpec((tm, D), lambda b, i: (jnp.where(b == 2, i, 0), 0)),
                  pl.BlockSpec((tm, D), lambda b, i: (jnp.where(b > 2, i, 0), jnp.maximum(b - 3, 0))), ANY],
        out_specs=pl.BlockSpec((D, D), lambda b, i: (0, b)),
        out_shape=pltpu.HBM((D, IN_COLS), BF16),
        scratch_shapes=[pltpu.VMEM((D, D), F32)],
        compiler_params=_params(("arbitrary", "arbitrary"), 52),
    )(*_pinned(u_t, d_glu, dp, dgates, token))


VEC_NAMES = ("mix_pre_g", "dw_bias", "conv_ln_g", "conv_ln_b", "pool_scale", "mix_post_g", "mlp_pre_g", "mlp_post_g")
WEIGHT_ORDER = ("mix_pre_g", "w_in", "dw_kernel", "dw_bias", "conv_ln_g", "conv_ln_b", "w_conv_out", "pool_w",
                "pool_scale", "w_pool_out", "w_o", "mix_post_g", "mlp_pre_g", "w_ff1", "w_ff2", "mlp_post_g")


def _step(x, loss_target, w, m, v):
    row = lambda a: a.reshape(1, D)
    names = [s[0] for s in SHARDED]

    seeded = dict(zip(names, _stage_shards([w[n] for n in names])))
    gather_groups = (("w_in",), ("w_conv_out", "dw_kernel"), ("pool_w", "w_pool_out", "w_o"), ("w_ff1", "w_ff2"))
    first_level, token = _exchange_start(
        "gather_start", [(g, None, [seeded[n] for n in g]) for g in gather_groups], [GATHER_ICI, GATHER_D2D])
    forwarded, full = {}, {}

    def forward(k, after):
        group, (ici_sems, d2d_sems), _, landed = first_level[k]
        _, landed = _exchange_wait("gather_ici_" + group[0], group, None, landed, [(ici_sems, GATHER_ICI)], after)
        (second,), tok = _exchange_start("gather_forward_" + group[0], [(group, None, landed)], [GATHER_FORWARD])
        forwarded[k] = (second[3], [(d2d_sems, GATHER_D2D), (second[1][0], GATHER_FORWARD)])
        return tok

    def gathered(k, after):
        group = gather_groups[k]
        landed, waits = forwarded[k]
        _, arrays = _exchange_wait("gather_wait_" + group[0], group, None, landed, waits, after)
        full.update(zip(group, arrays))

    u, u_t = _input_norm(x, row(w["mix_pre_g"]), token)
    gathered(0, forward(0, u))
    tok = forward(1, full["w_in"])
    proj = _in_proj_fwd(u, full["w_in"], tok)
    gathered(1, proj)
    tok = forward(2, proj)
    cv, y_conv = _conv_fwd(proj, full["dw_kernel"], row(w["dw_bias"]), row(w["conv_ln_g"]), row(w["conv_ln_b"]),
                           full["w_conv_out"], tok)
    gathered(2, y_conv)
    tok = forward(3, y_conv)
    z, zl, y_pool, merged, o, h1 = _pool_merge_fwd(proj, y_conv, x, full["pool_w"], row(w["pool_scale"]),
                                                   full["w_pool_out"], full["w_o"], row(w["mix_post_g"]), tok)
    gathered(3, h1)
    vv, dm, dh2, sse, g_mlp_post = _mlp_fwd(h1, row(w["mlp_pre_g"]), full["w_ff1"], full["w_ff2"],
                                            row(w["mlp_post_g"]), loss_target)

    shard_of = {name: shard for name, _, shard, _ in SHARDED}

    def scatter_start(tag, group, grads):
        landings = [lax.empty((N_DEV - 1,) + shard_of[n], g.dtype) for n, g in zip(group, grads)]
        (handle,), tok = _exchange_start("scatter_start_" + tag, [(group, grads, landings)], [SCATTER])
        return handle, tok

    dv, g_ff1, g_ff2 = _mlp_bwd(vv, dm, full["w_ff1"], full["w_ff2"])
    h_ff, tok_ff = scatter_start("ff", ("w_ff1", "w_ff2"), [g_ff1, g_ff2])
    dh1, dy_conv, dy_pool, dgates, g_wo, g_mlp_pre, g_mix_post = _merge_bwd(
        dh2, dv, h1, row(w["mlp_pre_g"]), o, row(w["mix_post_g"]), full["w_o"], merged, proj, y_conv, y_pool, tok_ff)
    dp, g_wpo, g_pw, g_pool_scale = _pool_bwd(dy_pool, zl, z, full["pool_w"], row(w["pool_scale"]), full["w_pool_out"],
                                              dh1)
    h_pool, tok_pool = scatter_start("pool", ("w_o", "w_pool_out", "pool_w"), [g_wo, g_wpo, g_pw])
    d_glu, g_wco, g_dw, g_bias, g_ln_g, g_ln_b = _conv_bwd(dy_conv, cv, proj, full["dw_kernel"], row(w["conv_ln_g"]),
                                                            row(w["conv_ln_b"]), full["w_conv_out"], tok_pool)
    h_conv, tok_conv = scatter_start("conv", ("w_conv_out", "dw_kernel"), [g_wco, g_dw[:, :TAPS, :]])
    g_win = _in_proj_bwd_w(u_t, d_glu, dp, dgates, tok_conv)
    h_in, tok_in = scatter_start("in", ("w_in",), [g_win])
    grad_x, g_mix_pre = _in_proj_bwd_x(d_glu, dp, dgates, full["w_in"], x, row(w["mix_pre_g"]), dh1, tok_in)

    out = {}
    slot = jnp.reshape(_my_slot(), (1,)).astype(jnp.int32)
    after = grad_x
    for tag, (group, (sems,), grads, landings) in (("ff", h_ff), ("pool", h_pool), ("conv", h_conv), ("in", h_in)):
        if tag == "in":
            vec_parts = after = _vec_exchange(
                [g_mix_pre, g_bias, g_ln_g, g_ln_b, g_pool_scale, g_mix_post, g_mlp_pre, g_mlp_post,
                 jnp.broadcast_to(sse, (1, D))], after)
        mine, landed = _exchange_wait("scatter_wait_" + tag, group, grads, landings, [(sems, SCATTER)], after)
        for name, own, parts in zip(group, mine, landed):
            out[name] = _adamw(name, own, parts, w[name], m[name], v[name], slot, after)
            after = out[name][0]
    stack = lambda d: jnp.stack([d[n] for n in VEC_NAMES], axis=0)
    *res, loss_row = _adamw_vectors(vec_parts, stack(w), stack(m), stack(v))
    for k, name in enumerate(VEC_NAMES):
        out[name] = [r[k] for r in res]
    return loss_row[0, 0], grad_x, out


def kernel(x, mix_pre_g, w_in, dw_kernel, dw_bias, conv_ln_g, conv_ln_b, w_conv_out, pool_w, pool_scale, w_pool_out, w_o, mix_post_g, mlp_pre_g, w_ff1, w_ff2, mlp_post_g, loss_target, m_mix_pre_g, m_w_in, m_dw_kernel, m_dw_bias, m_conv_ln_g, m_conv_ln_b, m_w_conv_out, m_pool_w, m_pool_scale, m_w_pool_out, m_w_o, m_mix_post_g, m_mlp_pre_g, m_w_ff1, m_w_ff2, m_mlp_post_g, v_mix_pre_g, v_w_in, v_dw_kernel, v_dw_bias, v_conv_ln_g, v_conv_ln_b, v_w_conv_out, v_pool_w, v_pool_scale, v_w_pool_out, v_w_o, v_mix_post_g, v_mlp_pre_g, v_w_ff1, v_w_ff2, v_mlp_post_g):
    w = dict(mix_pre_g=mix_pre_g, w_in=w_in, dw_kernel=dw_kernel, dw_bias=dw_bias, conv_ln_g=conv_ln_g, conv_ln_b=conv_ln_b,
             w_conv_out=w_conv_out, pool_w=pool_w, pool_scale=pool_scale, w_pool_out=w_pool_out, w_o=w_o,
             mix_post_g=mix_post_g, mlp_pre_g=mlp_pre_g, w_ff1=w_ff1, w_ff2=w_ff2, mlp_post_g=mlp_post_g)
    m = dict(mix_pre_g=m_mix_pre_g, w_in=m_w_in, dw_kernel=m_dw_kernel, dw_bias=m_dw_bias, conv_ln_g=m_conv_ln_g,
             conv_ln_b=m_conv_ln_b, w_conv_out=m_w_conv_out, pool_w=m_pool_w, pool_scale=m_pool_scale,
             w_pool_out=m_w_pool_out, w_o=m_w_o, mix_post_g=m_mix_post_g, mlp_pre_g=m_mlp_pre_g, w_ff1=m_w_ff1,
             w_ff2=m_w_ff2, mlp_post_g=m_mlp_post_g)
    v = dict(mix_pre_g=v_mix_pre_g, w_in=v_w_in, dw_kernel=v_dw_kernel, dw_bias=v_dw_bias, conv_ln_g=v_conv_ln_g,
             conv_ln_b=v_conv_ln_b, w_conv_out=v_w_conv_out, pool_w=v_pool_w, pool_scale=v_pool_scale,
             w_pool_out=v_w_pool_out, w_o=v_w_o, mix_post_g=v_mix_post_g, mlp_pre_g=v_mlp_pre_g, w_ff1=v_w_ff1,
             w_ff2=v_w_ff2, mlp_post_g=v_mlp_post_g)
    seq = x.shape[1]
    loss, grad_x, out = _step(x.reshape(seq, D), loss_target.reshape(seq, D), w, m, v)
    grads, deltas, new_m, new_v = ([out[n][k] for n in WEIGHT_ORDER] for k in range(4))
    return (loss, grad_x.reshape(x.shape), *grads, *deltas, *new_m, *new_v)
```

```python
import collections

import jax
import jax.numpy as jnp
from jax import lax
from jax.experimental import pallas as pl
from jax.experimental.pallas import tpu as pltpu

D = 1024
D_FF = 4 * D
IN_COLS = 5 * D
TAPS = 31
HALO = 32
POOL_WINDOWS = (2, 4, 8, 16)
PG = D // 4
N_DEV = 8
RMS_EPS = 1e-6
LN_EPS = 1e-5
ADAM_LR, ADAM_B1, ADAM_B2, ADAM_EPS, ADAM_WD, ADAM_STEP = 0.001, 0.9, 0.999, 1e-08, 0.01, 10

BF16 = jnp.bfloat16
F32 = jnp.float32
MIB = 1 << 20
MESH = pl.DeviceIdType.MESH


def _params(sem, vmem_mib):
    return pltpu.CompilerParams(dimension_semantics=sem, vmem_limit_bytes=vmem_mib * MIB)


def _dot(a, b):
    return jnp.dot(a, b, preferred_element_type=F32)


def _dot_nt(a, b):
    return lax.dot_general(a, b, (((1,), (1,)), ((), ())), preferred_element_type=F32)


def _dot_tn(a, b):
    return lax.dot_general(a, b, (((0,), (0,)), ((), ())), preferred_element_type=F32)


def _rms_scale(x):
    return lax.rsqrt(jnp.mean(x * x, axis=-1, keepdims=True) + RMS_EPS)


def _rms_bwd(dy, x, g):
    xn = x * _rms_scale(x)
    dn = dy * g
    dx = _rms_scale(x) * (dn - xn * jnp.mean(dn * xn, axis=-1, keepdims=True))
    return dx, jnp.sum(dy * xn, axis=0, keepdims=True)


def _sigmoid(x):
    return jax.nn.sigmoid(x)


def _acc_out(ref, first, value):
    @pl.when(first)
    def _():
        ref[...] = value

    @pl.when(jnp.logical_not(first))
    def _():
        ref[...] += value


def _my_slot():
    return 4 * lax.axis_index("x") + 2 * lax.axis_index("y") + lax.axis_index("c")


def _peer(mask):
    x, y, c = lax.axis_index("x"), lax.axis_index("y"), lax.axis_index("c")
    return (x ^ ((mask >> 2) & 1), y ^ ((mask >> 1) & 1), c ^ (mask & 1))


def _cols(width):
    return lambda ref, slot: ref.at[:, pl.ds(pl.multiple_of(slot * width, 128), width)]


def _rows(height):
    return lambda ref, slot: ref.at[pl.ds(pl.multiple_of(slot * height, 8), height), :]


def _lead(ref, slot):
    return ref.at[slot]


def _pool_rows(ref, slot):
    return ref.at[:, pl.ds(pl.multiple_of(slot * (PG // N_DEV), 8), PG // N_DEV), :]


SHARDED = (
    ("w_in", (D, IN_COLS), (D, IN_COLS // N_DEV), _cols(IN_COLS // N_DEV)),
    ("w_ff1", (D, D_FF), (D, D_FF // N_DEV), _cols(D_FF // N_DEV)),
    ("w_ff2", (D_FF, D), (D_FF // N_DEV, D), _rows(D_FF // N_DEV)),
    ("w_conv_out", (D, D), (D // N_DEV, D), _rows(D // N_DEV)),
    ("w_pool_out", (D, D), (D // N_DEV, D), _rows(D // N_DEV)),
    ("w_o", (D, D), (D // N_DEV, D), _rows(D // N_DEV)),
    ("pool_w", (4, PG, PG), (4, PG // N_DEV, PG), _pool_rows),
    ("dw_kernel", (N_DEV, TAPS, D // N_DEV), (TAPS, D // N_DEV), _lead),
)
N_SHARDED = len(SHARDED)


SHARD_AT = {name: at for name, _, _, at in SHARDED}
HBM = pl.BlockSpec(memory_space=pltpu.HBM)
SEM = pl.BlockSpec(memory_space=pltpu.SEMAPHORE)
ANY = pl.BlockSpec(memory_space=pl.ANY)
EFFECT = pltpu.SideEffectType.DATAFLOW_SIDE_EFFECTING


def _in_hbm(a):
    return pltpu.with_memory_space_constraint(a, pltpu.HBM)


def _pinned(*arrays):
    return [_in_hbm(a) for a in arrays]


def _stage_shards(names, shards, token):
    n = len(names)
    specs = [s for name in names for s in SHARDED if s[0] == name]
    dtypes = [F32 if name == "dw_kernel" else BF16 for name in names]

    def body(*refs):
        ins = refs[:n]
        fulls = refs[n + 1:2 * n + 1]
        raw = refs[2 * n + 1:3 * n + 1]
        stage = refs[3 * n + 1:4 * n + 1]
        in_sems, out_sems = refs[4 * n + 1:]
        me = _my_slot()
        loads = [pltpu.make_async_copy(ins[a], raw[a], in_sems.at[a]) for a in range(n)]
        for cp in loads:
            cp.start()
        stores = []
        for a, (_, _, _, at) in enumerate(specs):
            loads[a].wait()
            stage[a][...] = raw[a][...].astype(dtypes[a])
            cp = pltpu.make_async_copy(stage[a], at(fulls[a], me), out_sems.at[a])
            cp.start()
            stores.append(cp)
        for cp in stores:
            cp.wait()

    return pl.pallas_call(
        body, name="stage_" + names[0],
        out_shape=[pltpu.HBM(full, dt) for (_, full, _, _), dt in zip(specs, dtypes)],
        in_specs=[ANY] * (n + 1),
        out_specs=[ANY] * n,
        scratch_shapes=[pltpu.VMEM(shard, F32) for _, _, shard, _ in specs]
        + [pltpu.VMEM(shard, dt) for (_, _, shard, _), dt in zip(specs, dtypes)]
        + [pltpu.SemaphoreType.DMA((n,)), pltpu.SemaphoreType.DMA((n,))],
        compiler_params=pltpu.CompilerParams(vmem_limit_bytes=40 * MIB),
    )(*_pinned(*shards, token))


Leg = collections.namedtuple("Leg", "routes src_of dst_of src_is_land")


def _sem_index(k, m):
    return k * (N_DEV - 1) + m - 1


def _exchange_start(name, groups, legs):
    sizes = [len(g[0]) for g in groups]
    names = [nm for g in groups for nm in g[0]]
    srcs = [s for g in groups if g[1] is not None for s in g[1]]
    lands = [l for g in groups for l in g[2]]
    n_src, n, n_g, n_l = len(srcs), len(names), len(groups), len(legs)

    def body(*refs):
        src_refs, land_refs = list(refs[:n_src]), refs[n_src:n_src + n]
        sems = refs[n_src + n:n_src + n + 2 * n_g * n_l]
        token = refs[-1]
        me = _my_slot()
        first = 0
        for g, size in enumerate(sizes):
            own_src = [src_refs.pop(0) for _ in range(size)] if groups[g][1] is not None else None
            for li, leg in enumerate(legs):
                send, recv = sems[2 * (g * n_l + li)], sems[2 * (g * n_l + li) + 1]
                for m, via in leg.routes:
                    for k in range(size):
                        land = land_refs[first + k]
                        src = land if leg.src_is_land else own_src[k]
                        pltpu.make_async_remote_copy(
                            src_ref=leg.src_of(names[first + k], src, me, m), dst_ref=leg.dst_of(names[first + k], land, me, m),
                            send_sem=send.at[_sem_index(k, m)], recv_sem=recv.at[_sem_index(k, m)],
                            device_id=_peer(via), device_id_type=MESH).start()
            first += size
        token[...] = jnp.zeros_like(token)

    sem_shapes = [pltpu.SemaphoreType.DMA((size * (N_DEV - 1),)) for size in sizes for _ in range(2 * n_l)]
    n_sem = len(sem_shapes)
    outs = pl.pallas_call(
        body, name=name,
        out_shape=sem_shapes + [pltpu.HBM(a.shape, a.dtype) for a in srcs + lands] + [jax.ShapeDtypeStruct((8, 128), F32)],
        in_specs=[HBM] * (n_src + n),
        out_specs=[SEM] * n_sem + [HBM] * (n_src + n) + [pl.BlockSpec(memory_space=pltpu.VMEM)],
        input_output_aliases={k: n_sem + k for k in range(n_src + n)},
        compiler_params=pltpu.CompilerParams(has_side_effects=EFFECT),
    )(*[_in_hbm(a) for a in srcs + lands])
    sems, thru, token = outs[:n_sem], list(outs[n_sem:-1]), outs[-1]
    src_thru, land_thru = thru[:n_src], thru[n_src:]
    handles, first = [], 0
    for g, size in enumerate(sizes):
        pairs = [(sems[2 * (g * n_l + li)], sems[2 * (g * n_l + li) + 1]) for li in range(n_l)]
        mine = [src_thru.pop(0) for _ in range(size)] if groups[g][1] is not None else None
        handles.append((groups[g][0], pairs, mine, land_thru[first:first + size]))
        first += size
    return handles, token


def _exchange_wait(name, names, srcs, lands, waits, after):
    n = len(names)
    n_src = n if srcs is not None else 0

    def body(*refs):
        src_refs, land_refs = refs[:n_src], refs[n_src:n_src + n]
        sems = refs[n_src + n:n_src + n + 2 * len(waits)]
        me = _my_slot()
        for wi, (_, leg) in enumerate(waits):
            for m, via in leg.routes:
                for k in range(n):
                    src = land_refs[k] if leg.src_is_land else src_refs[k]
                    cp = pltpu.make_async_remote_copy(
                        src_ref=leg.src_of(names[k], src, me, m), dst_ref=leg.dst_of(names[k], land_refs[k], me ^ via, m),
                        send_sem=sems[2 * wi].at[_sem_index(k, m)], recv_sem=sems[2 * wi + 1].at[_sem_index(k, m)],
                        device_id=_peer(via), device_id_type=MESH)
                    cp.wait_send()
                    cp.wait_recv()

    arrays = (list(srcs) if srcs is not None else []) + list(lands)
    outs = pl.pallas_call(
        body, name=name,
        out_shape=[pltpu.HBM(a.shape, a.dtype) for a in arrays],
        in_specs=[HBM] * len(arrays) + [SEM] * (2 * len(waits)) + [pl.BlockSpec(memory_space=pl.ANY)],
        out_specs=[HBM] * len(arrays),
        input_output_aliases={k: k for k in range(len(arrays))},
        compiler_params=pltpu.CompilerParams(has_side_effects=EFFECT),
    )(*arrays, *[s for pair, _ in waits for s in pair], after)
    return (outs[:n_src] if srcs is not None else None), outs[n_src:]


def _shard_slot(name, ref, slot):
    return SHARD_AT[name](ref, slot)


GATHER_ICI = Leg(((2, 2), (4, 4), (6, 6)), lambda name, ref, me, m: _shard_slot(name, ref, me),
                 lambda name, ref, sender, m: _shard_slot(name, ref, sender), True)
GATHER_D2D = Leg(((1, 1),), GATHER_ICI.src_of, GATHER_ICI.dst_of, True)
GATHER_FORWARD = Leg(((2, 1), (4, 1), (6, 1)), lambda name, ref, me, m: _shard_slot(name, ref, me ^ m),
                     lambda name, ref, sender, m: _shard_slot(name, ref, sender ^ m), True)
SCATTER = Leg(tuple((m, m) for m in range(1, N_DEV)), lambda name, ref, me, m: _shard_slot(name, ref, me ^ m),
              lambda name, ref, sender, m: ref.at[m - 1], False)


def _vec_exchange(vectors, after):
    n = len(vectors)

    def body(*refs):
        vec, vec_out = refs[n + 2], refs[n + 1]
        send_sems, recv_sems, local_sem = refs[n + 3:]
        for k in range(n):
            vec[k:k + 1, :] = refs[k][...]
        me = _my_slot()
        local = pltpu.make_async_copy(vec, vec_out.at[me], local_sem)
        local.start()
        sends = []
        for mask in range(1, N_DEV):
            cp = pltpu.make_async_remote_copy(
                src_ref=vec, dst_ref=vec_out.at[me], send_sem=send_sems.at[mask - 1],
                recv_sem=recv_sems.at[mask - 1], device_id=_peer(mask), device_id_type=MESH)
            cp.start()
            sends.append(cp)
        for mask in range(1, N_DEV):
            pltpu.make_async_remote_copy(
                src_ref=vec, dst_ref=vec_out.at[me ^ mask], send_sem=send_sems.at[mask - 1],
                recv_sem=recv_sems.at[mask - 1], device_id=_peer(mask), device_id_type=MESH).wait_recv()
        for cp in sends:
            cp.wait_send()
        local.wait()

    return pl.pallas_call(
        body, name="vec_exchange",
        out_shape=jax.ShapeDtypeStruct((N_DEV, n, D), F32),
        in_specs=[pl.BlockSpec(memory_space=pltpu.VMEM)] * n + [ANY],
        out_specs=pl.BlockSpec(memory_space=pl.ANY),
        scratch_shapes=[pltpu.VMEM((n, D), F32), pltpu.SemaphoreType.DMA((N_DEV - 1,)),
                        pltpu.SemaphoreType.DMA((N_DEV - 1,)), pltpu.SemaphoreType.DMA],
    )(*vectors, after)


def _adamw_update(g, w_ref, m_ref, v_ref, g_out, d_out, m_out, v_out):
    m_new = ADAM_B1 * m_ref[...] + (1.0 - ADAM_B1) * g
    v_new = ADAM_B2 * v_ref[...] + (1.0 - ADAM_B2) * (g * g)
    m_hat = m_new / (1.0 - ADAM_B1 ** ADAM_STEP)
    v_hat = v_new / (1.0 - ADAM_B2 ** ADAM_STEP)
    g_out[...] = g
    d_out[...] = -ADAM_LR * (m_hat / (jnp.sqrt(v_hat) + ADAM_EPS) + ADAM_WD * w_ref[...])
    m_out[...] = m_new
    v_out[...] = v_new


ADAMW_ROWS = 256


def _adamw(name, own, parts, w, m, v, slot, token):
    shard = w.shape
    if name in ("w_in", "w_ff1"):
        tr = ADAMW_ROWS
        grid = (shard[0] // tr,)
        own_spec = pl.BlockSpec((tr, shard[1]), lambda i, s: (i, s[0]))
        blk = pl.BlockSpec((tr, shard[1]), lambda i, s: (i, 0))
        parts_spec = pl.BlockSpec((N_DEV - 1, tr, shard[1]), lambda i, s: (0, i, 0))
    elif name == "pool_w":
        grid = (shard[0],)
        own_spec = pl.BlockSpec((None,) + shard[1:], lambda g, s: (g, s[0], 0))
        blk = pl.BlockSpec((None,) + shard[1:], lambda g, s: (g, 0, 0))
        parts_spec = pl.BlockSpec((N_DEV - 1, None) + shard[1:], lambda g, s: (0, g, 0, 0))
    elif name == "dw_kernel":
        grid = (1,)
        own_spec = pl.BlockSpec((None,) + shard, lambda i, s: (s[0], 0, 0))
        blk = pl.BlockSpec(shard, lambda i, s: (0, 0))
        parts_spec = pl.BlockSpec((N_DEV - 1,) + shard, lambda i, s: (0, 0, 0))
    else:
        tr = min(ADAMW_ROWS, shard[0])
        grid = (shard[0] // tr,)
        own_spec = pl.BlockSpec((tr, shard[1]), lambda i, s: (s[0] * grid[0] + i, 0))
        blk = pl.BlockSpec((tr, shard[1]), lambda i, s: (i, 0))
        parts_spec = pl.BlockSpec((N_DEV - 1, tr, shard[1]), lambda i, s: (0, i, 0))

    def body(slot_ref, own_ref, p_ref, w_ref, m_ref, v_ref, _token, g_out, d_out, m_out, v_out):
        g = own_ref[...].astype(F32)
        for k in range(N_DEV - 1):
            g = g + p_ref[k].astype(F32)
        _adamw_update(g, w_ref, m_ref, v_ref, g_out, d_out, m_out, v_out)

    return pl.pallas_call(
        body, name="adamw_" + name,
        grid_spec=pltpu.PrefetchScalarGridSpec(
            num_scalar_prefetch=1, grid=grid, in_specs=[own_spec, parts_spec, blk, blk, blk, ANY], out_specs=[blk] * 4),
        out_shape=[jax.ShapeDtypeStruct(shard, F32)] * 4,
        compiler_params=_params(("arbitrary",), 32),
    )(slot, *_pinned(own, parts, w, m, v, token))


def _adamw_vectors(parts, w, m, v):
    n_vec = w.shape[0]

    def body(p_ref, w_ref, m_ref, v_ref, g_out, d_out, m_out, v_out, loss_out):
        total = p_ref[0]
        for s in range(1, N_DEV):
            total = total + p_ref[s]
        _adamw_update(total[0:n_vec], w_ref, m_ref, v_ref, g_out, d_out, m_out, v_out)
        loss_out[...] = total[n_vec:n_vec + 1] * (0.5 / D)

    return pl.pallas_call(
        body, name="adamw_vectors",
        out_shape=[jax.ShapeDtypeStruct(w.shape, F32)] * 4 + [jax.ShapeDtypeStruct((1, D), F32)],
    )(parts, w, m, v)


def _input_norm(x, g_pre, token):
    seq = x.shape[0]
    tm = 1024

    def body(x_ref, g_ref, _token, u_ref, ut_ref):
        xf = x_ref[...]
        u = (xf * _rms_scale(xf) * g_ref[...]).astype(BF16)
        u_ref[...] = u
        ut_ref[...] = u.T

    return pl.pallas_call(
        body, name="input_norm", grid=(seq // tm,),
        in_specs=[pl.BlockSpec((tm, D), lambda i: (i, 0)), pl.BlockSpec((1, D), lambda i: (0, 0)), ANY],
        out_specs=[pl.BlockSpec((tm, D), lambda i: (i, 0)), pl.BlockSpec((D, tm), lambda i: (0, i))],
        out_shape=[pltpu.HBM((seq, D), BF16), pltpu.HBM((D, seq), BF16)],
        compiler_params=_params(("arbitrary",), 40),
    )(*_pinned(x, g_pre, token))


def _in_proj_fwd(u, w_in, token):
    seq = u.shape[0]
    tm, tn = 1024, IN_COLS // 4

    def body(u_ref, w_ref, _token, proj_ref):
        proj_ref[...] = _dot(u_ref[...], w_ref[...]).astype(BF16)

    return pl.pallas_call(
        body, name="in_proj_fwd", grid=(seq // tm, IN_COLS // tn),
        in_specs=[pl.BlockSpec((tm, D), lambda i, j: (i, 0)), pl.BlockSpec((D, tn), lambda i, j: (0, j)), ANY],
        out_specs=pl.BlockSpec((tm, tn), lambda i, j: (i, j)),
        out_shape=pltpu.HBM((seq, IN_COLS), BF16),
        compiler_params=_params(("arbitrary", "arbitrary"), 40),
    )(*_pinned(u, w_in, token))


CONV_TM = 512
CONV_RS = 128


def _shifts():
    return [(b, [(a, 8 * a + b) for a in range(4) if 8 * a + b < TAPS]) for b in range(8)]


def _taps_looking_back(buf, row0, lanes, weight):
    acc = None
    for b, group in _shifts():
        part = None
        for a, s in group:
            term = weight(TAPS - 1 - s) * buf[pl.ds(row0 - 8 - 8 * a, CONV_RS + 8), lanes]
            part = term if part is None else part + term
        if b:
            part = pltpu.roll(part, b, 0)
        acc = part[8:, :] if acc is None else acc + part[8:, :]
    return acc


def _taps_looking_ahead(buf, row0, lanes, weight):
    acc = None
    for b, group in _shifts():
        part = None
        for a, s in group:
            term = weight(TAPS - 1 - s) * buf[pl.ds(row0 + 8 * a, CONV_RS + 8), lanes]
            part = term if part is None else part + term
        if b:
            part = pltpu.roll(part, CONV_RS + 8 - b, 0)
        acc = part[:CONV_RS, :] if acc is None else acc + part[:CONV_RS, :]
    return acc


def _layer_norm_parts(cv):
    mu = jnp.mean(cv, axis=-1, keepdims=True)
    cen = cv - mu
    rstd = lax.rsqrt(jnp.mean(cen * cen, axis=-1, keepdims=True) + LN_EPS)
    return cen * rstd, rstd


def _conv_fwd(proj, dw, dw_bias, ln_g, ln_b, w_conv_out, token):
    seq = proj.shape[0]
    tm = CONV_TM

    def body(a_ref, gate_ref, dw_ref, bias_ref, lg_ref, lb_ref, w_ref, _token, cv_ref, y_ref, ug):
        i = pl.program_id(0)

        @pl.when(i == 0)
        def _():
            ug[0:HALO, :] = jnp.zeros((HALO, D), F32)

        @pl.when(i > 0)
        def _():
            ug[0:HALO, :] = ug[tm:tm + HALO, :]
        ug[HALO:HALO + tm, :] = a_ref[...].astype(F32) * _sigmoid(gate_ref[...].astype(F32))

        def channel_block(cb, carry):
            lanes = pl.ds(pl.multiple_of(cb * 128, 128), 128)
            for r0 in range(0, tm, CONV_RS):
                taps = _taps_looking_back(ug, HALO + r0, lanes, lambda k: dw_ref[cb, k:k + 1, :])
                cv_ref[pl.ds(r0, CONV_RS), lanes] = taps + bias_ref[:, lanes]
            return carry
        lax.fori_loop(0, D // 128, channel_block, 0)

        n, _ = _layer_norm_parts(cv_ref[...])
        ln = n * lg_ref[...] + lb_ref[...]
        y_ref[...] = _dot((ln * _sigmoid(ln)).astype(BF16), w_ref[...]).astype(BF16)

    vec = pl.BlockSpec((1, D), lambda i: (0, 0))
    tile = pl.BlockSpec((tm, D), lambda i: (i, 0))
    return pl.pallas_call(
        body, name="conv_fwd", grid=(seq // tm,),
        in_specs=[pl.BlockSpec((tm, D), lambda i: (i, 0)), pl.BlockSpec((tm, D), lambda i: (i, 1)),
                  pl.BlockSpec((N_DEV, TAPS, 128), lambda i: (0, 0, 0)), vec, vec, vec,
                  pl.BlockSpec((D, D), lambda i: (0, 0)), ANY],
        out_specs=[tile, tile],
        out_shape=[pltpu.HBM((seq, D), F32), pltpu.HBM((seq, D), BF16)],
        scratch_shapes=[pltpu.VMEM((HALO + tm, D), F32)],
        compiler_params=_params(("arbitrary",), 32),
    )(*_pinned(proj, proj, dw, dw_bias, ln_g, ln_b, w_conv_out, token))


def _window_sums(rows, window, back):
    n = rows.shape[0]
    span = 1
    while span < window:
        rows = rows + pltpu.roll(rows, span if back else n - span, 0)
        span *= 2
    return rows


def _pool_counts(tile_index, tm, window):
    t = tile_index * tm + lax.broadcasted_iota(jnp.int32, (tm, 1), 0)
    return 1.0 / jnp.minimum(t + 1, window).astype(F32)


def _pool_merge_fwd(proj, y_conv, x, pool_w, pool_scale, w_pool_out, w_o, g_post, token):
    seq = proj.shape[0]
    tm = CONV_TM

    def body(p_ref, gc_ref, gp_ref, yc_ref, x_ref, pw_ref, ps_ref, wpo_ref, wo_ref, g_ref, _token,
             z_ref, zl_ref, yp_ref, mg_ref, o_ref, h1_ref, pbuf):
        i = pl.program_id(0)

        @pl.when(i == 0)
        def _():
            pbuf[0:HALO, :] = jnp.zeros((HALO, D), F32)

        @pl.when(i > 0)
        def _():
            pbuf[0:HALO, :] = pbuf[tm:tm + HALO, :]
        pbuf[HALO:HALO + tm, :] = p_ref[...].astype(F32)

        for g, window in enumerate(POOL_WINDOWS):
            lanes = pl.ds(g * PG, PG)
            acc = _window_sums(pbuf[:, lanes], window, back=True)[HALO:, :]
            zg = acc * _pool_counts(i, tm, window) - pbuf[pl.ds(HALO, tm), lanes]
            z_ref[:, lanes] = zg.astype(BF16)
            zl_ref[:, lanes] = _dot(zg.astype(BF16), pw_ref[g])
        zl = zl_ref[...]
        y_pool = _dot((zl * ps_ref[...]).astype(BF16), wpo_ref[...])
        yp_ref[...] = y_pool.astype(BF16)
        merged = (_sigmoid(gc_ref[...].astype(F32)) * yc_ref[...].astype(F32)
                  + _sigmoid(gp_ref[...].astype(F32)) * y_pool).astype(BF16)
        mg_ref[...] = merged
        o = _dot(merged, wo_ref[...])
        o_ref[...] = o
        h1_ref[...] = x_ref[...] + o * _rms_scale(o) * g_ref[...]

    vec = pl.BlockSpec((1, D), lambda i: (0, 0))
    tile = pl.BlockSpec((tm, D), lambda i: (i, 0))
    mat = pl.BlockSpec((D, D), lambda i: (0, 0))
    return pl.pallas_call(
        body, name="pool_merge_fwd", grid=(seq // tm,),
        in_specs=[pl.BlockSpec((tm, D), lambda i: (i, 2)), pl.BlockSpec((tm, D), lambda i: (i, 3)),
                  pl.BlockSpec((tm, D), lambda i: (i, 4)), tile, tile,
                  pl.BlockSpec((4, PG, PG), lambda i: (0, 0, 0)), vec, mat, mat, vec, ANY],
        out_specs=[tile] * 6,
        out_shape=[pltpu.HBM((seq, D), dt) for dt in (BF16, F32, BF16, BF16, F32, F32)],
        scratch_shapes=[pltpu.VMEM((HALO + tm, D), F32)],
        compiler_params=_params(("arbitrary",), 48),
    )(*_pinned(proj, proj, proj, y_conv, x, pool_w, pool_scale, w_pool_out, w_o, g_post, token))


def _mlp_fwd(h1, g_pre, w_ff1, w_ff2, g_post, target):
    seq = h1.shape[0]
    tm, tf = 1024, D_FF // N_DEV
    n_f = D_FF // tf

    def body(h1_ref, gpre_ref, w1_ref, w2_ref, gpost_ref, tgt_ref, v_ref, dm_ref, dh2_ref, sse_ref, ggrad_ref, macc):
        i, j = pl.program_id(0), pl.program_id(1)

        @pl.when(j == 0)
        def _():
            h = h1_ref[...]
            v_ref[...] = (h * _rms_scale(h) * gpre_ref[...]).astype(BF16)
        f = jnp.maximum(_dot(v_ref[...], w1_ref[...]), 0.0)
        part = _dot((f * f).astype(BF16), w2_ref[...])

        @pl.when(j == 0)
        def _():
            macc[...] = part

        @pl.when(j > 0)
        def _():
            macc[...] += part

        @pl.when(j == n_f - 1)
        def _():
            mo = macc[...]
            err = h1_ref[...] + mo * _rms_scale(mo) * gpost_ref[...] - tgt_ref[...]
            dh2 = err * (1.0 / D)
            dh2_ref[...] = dh2
            dm, ggrad = _rms_bwd(dh2, mo, gpost_ref[...])
            dm_ref[...] = dm.astype(BF16)
            _acc_out(ggrad_ref, i == 0, ggrad)
            _acc_out(sse_ref, i == 0, jnp.sum(jnp.sum(err * err, axis=1, keepdims=True), axis=0, keepdims=True))

    vec = pl.BlockSpec((1, D), lambda i, j: (0, 0))
    tile = pl.BlockSpec((tm, D), lambda i, j: (i, 0))
    return pl.pallas_call(
        body, name="mlp_fwd", grid=(seq // tm, n_f),
        in_specs=[tile, vec, pl.BlockSpec((D, tf), lambda i, j: (0, j)), pl.BlockSpec((tf, D), lambda i, j: (j, 0)), vec, tile],
        out_specs=[tile, tile, tile, pl.BlockSpec((1, 1), lambda i, j: (0, 0)), vec],
        out_shape=[pltpu.HBM((seq, D), BF16), pltpu.HBM((seq, D), BF16),
                   pltpu.HBM((seq, D), F32), jax.ShapeDtypeStruct((1, 1), F32),
                   jax.ShapeDtypeStruct((1, D), F32)],
        scratch_shapes=[pltpu.VMEM((tm, D), F32)],
        compiler_params=_params(("arbitrary", "arbitrary"), 56),
    )(*_pinned(h1, g_pre, w_ff1, w_ff2, g_post, target))


def _mlp_bwd(v, dm, w_ff1, w_ff2):
    seq = v.shape[0]
    tm, tf = 1024, D_FF // N_DEV
    n_t = seq // tm

    def body(v_ref, dm_ref, w1_ref, w2_ref, dv_hbm, g1_ref, g2_ref, dv_acc, g1_acc, g2_acc, sem):
        j, i = pl.program_id(0), pl.program_id(1)
        vt, dmt = v_ref[...], dm_ref[...]
        f = jnp.maximum(_dot(vt, w1_ref[...]), 0.0)
        df = (_dot_nt(dmt, w2_ref[...]) * (2.0 * f)).astype(BF16)
        rows = pl.ds(pl.multiple_of(i * tm, tm), tm)
        dv_part = _dot_nt(df, w1_ref[...])

        @pl.when(j == 0)
        def _():
            dv_acc[rows, :] = dv_part

        @pl.when(j > 0)
        def _():
            dv_acc[rows, :] += dv_part
        g1_part = _dot_tn(vt, df)
        g2_part = _dot_tn((f * f).astype(BF16), dmt)

        @pl.when(i == 0)
        def _():
            g1_acc[...] = g1_part
            g2_acc[...] = g2_part

        @pl.when(i > 0)
        def _():
            g1_acc[...] += g1_part
            g2_acc[...] += g2_part

        @pl.when(i == n_t - 1)
        def _():
            g1_ref[...] = g1_acc[...].astype(BF16)
            g2_ref[...] = g2_acc[...].astype(BF16)

        @pl.when(jnp.logical_and(i == n_t - 1, j == D_FF // tf - 1))
        def _():
            cp = pltpu.make_async_copy(dv_acc, dv_hbm, sem)
            cp.start()
            cp.wait()

    tile = pl.BlockSpec((tm, D), lambda j, i: (i, 0))
    return pl.pallas_call(
        body, name="mlp_bwd", grid=(D_FF // tf, n_t),
        in_specs=[tile, tile, pl.BlockSpec((D, tf), lambda j, i: (0, j)), pl.BlockSpec((tf, D), lambda j, i: (j, 0))],
        out_specs=[pl.BlockSpec(memory_space=pl.ANY), pl.BlockSpec((D, tf), lambda j, i: (0, j)),
                   pl.BlockSpec((tf, D), lambda j, i: (j, 0))],
        out_shape=[pltpu.HBM((seq, D), F32), pltpu.HBM((D, D_FF), BF16),
                   pltpu.HBM((D_FF, D), BF16)],
        scratch_shapes=[pltpu.VMEM((seq, D), F32), pltpu.VMEM((D, tf), F32), pltpu.VMEM((tf, D), F32),
                        pltpu.SemaphoreType.DMA],
        compiler_params=_params(("arbitrary", "arbitrary"), 52),
    )(*_pinned(v, dm, w_ff1, w_ff2))


def _merge_bwd(dh2, dv, h1, g_mlp_pre, o, g_mix_post, w_o, merged, proj, y_conv, y_pool, token):
    seq = dh2.shape[0]
    tm = 256
    n_t = seq // tm

    def body(dh2_ref, dv_ref, h1_ref, gpre_ref, o_ref, gpost_ref, wo_ref, mg_ref, gc_ref, gp_ref, yc_ref, yp_ref, _token,
             dh1_ref, dyc_ref, dyp_ref, dg_ref, gwo_ref, ggpre_ref, ggpost_ref, gwo_acc):
        i = pl.program_id(0)
        dnorm, ggpre = _rms_bwd(dv_ref[...], h1_ref[...], gpre_ref[...])
        dh1 = dh2_ref[...] + dnorm
        dh1_ref[...] = dh1
        do, ggpost = _rms_bwd(dh1, o_ref[...], gpost_ref[...])
        do = do.astype(BF16)
        _acc_out(ggpre_ref, i == 0, ggpre)
        _acc_out(ggpost_ref, i == 0, ggpost)
        _acc_out(gwo_acc, i == 0, _dot_tn(mg_ref[...], do))
        dmerged = _dot_nt(do, wo_ref[...])
        sc, sp = _sigmoid(gc_ref[...].astype(F32)), _sigmoid(gp_ref[...].astype(F32))
        dyc_ref[...] = (dmerged * sc).astype(BF16)
        dyp_ref[...] = (dmerged * sp).astype(BF16)
        dg_ref[:, 0:D] = (dmerged * yc_ref[...].astype(F32) * (sc * (1.0 - sc))).astype(BF16)
        dg_ref[:, D:2 * D] = (dmerged * yp_ref[...].astype(F32) * (sp * (1.0 - sp))).astype(BF16)

        @pl.when(i == n_t - 1)
        def _():
            gwo_ref[...] = gwo_acc[...].astype(BF16)

    vec = pl.BlockSpec((1, D), lambda i: (0, 0))
    tile = pl.BlockSpec((tm, D), lambda i: (i, 0))
    mat = pl.BlockSpec((D, D), lambda i: (0, 0))
    return pl.pallas_call(
        body, name="merge_bwd", grid=(n_t,),
        in_specs=[tile, tile, tile, vec, tile, vec, mat, tile,
                  pl.BlockSpec((tm, D), lambda i: (i, 3)), pl.BlockSpec((tm, D), lambda i: (i, 4)), tile, tile, ANY],
        out_specs=[tile, tile, tile, pl.BlockSpec((tm, 2 * D), lambda i: (i, 0)), mat, vec, vec],
        out_shape=[pltpu.HBM((seq, D), F32), pltpu.HBM((seq, D), BF16),
                   pltpu.HBM((seq, D), BF16), pltpu.HBM((seq, 2 * D), BF16),
                   pltpu.HBM((D, D), BF16), jax.ShapeDtypeStruct((1, D), F32),
                   jax.ShapeDtypeStruct((1, D), F32)],
        scratch_shapes=[pltpu.VMEM((D, D), F32)],
        compiler_params=_params(("arbitrary",), 48),
    )(*_pinned(dh2, dv, h1, g_mlp_pre, o, g_mix_post, w_o, merged, proj, proj, y_conv, y_pool, token))


def _pool_bwd(dy_pool, zl, z, pool_w, pool_scale, w_pool_out, token):
    seq = dy_pool.shape[0]
    tm = CONV_TM
    n_t = seq // tm

    def body(dy_ref, zl_ref, z_ref, pw_ref, ps_ref, wpo_ref, _token,
             dp_ref, gwpo_ref, gpw_ref, gps_ref, qbuf, gwpo_acc, gpw_acc):
        i = pl.program_id(0)
        tile_index = n_t - 1 - i
        first = i == 0
        dy = dy_ref[...]
        zl = zl_ref[...]
        dzs = _dot_nt(dy, wpo_ref[...])
        _acc_out(gwpo_acc, first, _dot_tn((zl * ps_ref[...]).astype(BF16), dy))
        _acc_out(gps_ref, first, jnp.sum(dzs * zl, axis=0, keepdims=True))
        dzl = (dzs * ps_ref[...]).astype(BF16)

        @pl.when(first)
        def _():
            qbuf[tm:tm + HALO, :] = jnp.zeros((HALO, D), F32)

        @pl.when(jnp.logical_not(first))
        def _():
            qbuf[tm:tm + HALO, :] = qbuf[0:HALO, :]

        dzs_list = []
        for g, window in enumerate(POOL_WINDOWS):
            lanes = pl.ds(g * PG, PG)
            dzl_g = dzl[:, g * PG:(g + 1) * PG]
            dz = _dot_nt(dzl_g, pw_ref[g])
            _acc_out(gpw_acc.at[g], first, _dot_tn(z_ref[:, lanes], dzl_g))
            qbuf[pl.ds(0, tm), lanes] = dz * _pool_counts(tile_index, tm, window)
            dzs_list.append(dz)
        for g, window in enumerate(POOL_WINDOWS):
            lanes = pl.ds(g * PG, PG)
            acc = _window_sums(qbuf[:, lanes], window, back=False)[:tm, :]
            dp_ref[:, lanes] = (acc - dzs_list[g]).astype(BF16)

        @pl.when(i == n_t - 1)
        def _():
            gwpo_ref[...] = gwpo_acc[...].astype(BF16)
            gpw_ref[...] = gpw_acc[...].astype(BF16)

    vec = pl.BlockSpec((1, D), lambda i: (0, 0))
    tile = pl.BlockSpec((tm, D), lambda i: (n_t - 1 - i, 0))
    mat = pl.BlockSpec((D, D), lambda i: (0, 0))
    pw = pl.BlockSpec((4, PG, PG), lambda i: (0, 0, 0))
    return pl.pallas_call(
        body, name="pool_bwd", grid=(n_t,),
        in_specs=[tile, tile, tile, pw, vec, mat, ANY],
        out_specs=[tile, mat, pw, vec],
        out_shape=[pltpu.HBM((seq, D), BF16), pltpu.HBM((D, D), BF16),
                   pltpu.HBM((4, PG, PG), BF16), jax.ShapeDtypeStruct((1, D), F32)],
        scratch_shapes=[pltpu.VMEM((tm + HALO, D), F32), pltpu.VMEM((D, D), F32), pltpu.VMEM((4, PG, PG), F32)],
        compiler_params=_params(("arbitrary",), 40),
    )(*_pinned(dy_pool, zl, z, pool_w, pool_scale, w_pool_out, token))


def _conv_bwd(dy_conv, cv, proj, dw, ln_g, ln_b, w_conv_out, token):
    seq = dy_conv.shape[0]
    tm = CONV_TM // 2
    n_t = seq // tm
    halo_blocks = tm // HALO

    def body(dy_ref, cv_ref, a_ref, gate_ref, ah_ref, gh_ref, dw_ref, lg_ref, lb_ref, w_ref, _token,
             dglu_ref, gw_ref, gdw_ref, gbias_ref, glg_ref, glb_ref, ug, dcv, dug, gw_acc):
        i = pl.program_id(0)
        tile_index = n_t - 1 - i
        first = i == 0
        dy = dy_ref[...]
        n, rstd = _layer_norm_parts(cv_ref[...])
        ln = n * lg_ref[...] + lb_ref[...]
        sg = _sigmoid(ln)
        _acc_out(gw_acc, first, _dot_tn((ln * sg).astype(BF16), dy))
        dln = _dot_nt(dy, w_ref[...]) * (sg * (1.0 + ln * (1.0 - sg)))
        _acc_out(glg_ref, first, jnp.sum(dln * n, axis=0, keepdims=True))
        _acc_out(glb_ref, first, jnp.sum(dln, axis=0, keepdims=True))
        dn = dln * lg_ref[...]
        dcv_tile = rstd * (dn - jnp.mean(dn, axis=-1, keepdims=True) - n * jnp.mean(dn * n, axis=-1, keepdims=True))
        _acc_out(gbias_ref, first, jnp.sum(dcv_tile, axis=0, keepdims=True))

        @pl.when(first)
        def _():
            dcv[tm:tm + HALO, :] = jnp.zeros((HALO, D), F32)

        @pl.when(jnp.logical_not(first))
        def _():
            dcv[tm:tm + HALO, :] = dcv[0:HALO, :]
        dcv[0:tm, :] = dcv_tile

        a, gate = a_ref[...].astype(F32), gate_ref[...].astype(F32)
        sgate = _sigmoid(gate)
        ug[HALO:HALO + tm, :] = a * sgate
        before = jnp.where(tile_index > 0, 1.0, 0.0)
        ug[0:HALO, :] = ah_ref[...].astype(F32) * _sigmoid(gh_ref[...].astype(F32)) * before

        @pl.when(first)
        def _():
            gdw_ref[...] = jnp.zeros((N_DEV, TAPS + 1, 128), F32)

        def channel_block(cb, carry):
            lanes = pl.ds(pl.multiple_of(cb * 128, 128), 128)
            for r0 in range(0, tm, CONV_RS):
                dug[pl.ds(r0, CONV_RS), lanes] = _taps_looking_ahead(dcv, r0, lanes, lambda k: dw_ref[cb, k:k + 1, :])
            for b, group in _shifts():
                sums = [jnp.zeros((8, 128), F32) for _ in group]
                for r0 in range(0, tm, CONV_RS):
                    window = ug[pl.ds(r0, CONV_RS + HALO), lanes]
                    if b:
                        window = pltpu.roll(window, b, 0)
                    d = dcv[pl.ds(r0, CONV_RS), lanes]
                    for n_a, (a, s) in enumerate(group):
                        prod = d * window[HALO - 8 * a:HALO - 8 * a + CONV_RS, :]
                        sums[n_a] = sums[n_a] + jnp.sum(prod.reshape(CONV_RS // 8, 8, 128), axis=0)
                for n_a, (a, s) in enumerate(group):
                    k = TAPS - 1 - s
                    gdw_ref[cb, k:k + 1, :] += jnp.sum(sums[n_a], axis=0, keepdims=True)
            return carry
        lax.fori_loop(0, D // 128, channel_block, 0)

        d_ug = dug[...]
        dglu_ref[:, 0:D] = (d_ug * sgate).astype(BF16)
        dglu_ref[:, D:2 * D] = (d_ug * a * (sgate * (1.0 - sgate))).astype(BF16)

        @pl.when(i == n_t - 1)
        def _():
            gw_ref[...] = gw_acc[...].astype(BF16)

    def halo_index(col):
        return lambda i: (jnp.maximum((n_t - 1 - i) * halo_blocks - 1, 0), col)

    vec = pl.BlockSpec((1, D), lambda i: (0, 0))
    tile = pl.BlockSpec((tm, D), lambda i: (n_t - 1 - i, 0))
    mat = pl.BlockSpec((D, D), lambda i: (0, 0))
    dwspec = pl.BlockSpec((N_DEV, TAPS, 128), lambda i: (0, 0, 0))
    return pl.pallas_call(
        body, name="conv_bwd", grid=(n_t,),
        in_specs=[tile, tile, pl.BlockSpec((tm, D), lambda i: (n_t - 1 - i, 0)), pl.BlockSpec((tm, D), lambda i: (n_t - 1 - i, 1)),
                  pl.BlockSpec((HALO, D), halo_index(0)), pl.BlockSpec((HALO, D), halo_index(1)), dwspec, vec, vec, mat,
                  ANY],
        out_specs=[pl.BlockSpec((tm, 2 * D), lambda i: (n_t - 1 - i, 0)), mat,
                   pl.BlockSpec((N_DEV, TAPS + 1, 128), lambda i: (0, 0, 0)), vec, vec, vec],
        out_shape=[pltpu.HBM((seq, 2 * D), BF16), pltpu.HBM((D, D), BF16),
                   pltpu.HBM((N_DEV, TAPS + 1, 128), F32), jax.ShapeDtypeStruct((1, D), F32),
                   jax.ShapeDtypeStruct((1, D), F32), jax.ShapeDtypeStruct((1, D), F32)],
        scratch_shapes=[pltpu.VMEM((HALO + tm, D), F32), pltpu.VMEM((tm + HALO, D), F32), pltpu.VMEM((tm, D), F32),
                        pltpu.VMEM((D, D), F32)],
        compiler_params=_params(("arbitrary",), 48),
    )(*_pinned(dy_conv, cv, proj, proj, proj, proj, dw, ln_g, ln_b, w_conv_out, token))


def _in_proj_bwd_x(d_glu, dp, dgates, w_in, x, g_pre, dh1, token):
    seq = x.shape[0]
    tm = 512

    def body(dglu_ref, dp_ref, dg_ref, w_ref, x_ref, g_ref, dh1_ref, _token, dx_ref, gg_ref):
        du = _dot_nt(dglu_ref[...], w_ref[:, 0:2 * D])
        du += _dot_nt(dp_ref[...], w_ref[:, 2 * D:3 * D])
        du += _dot_nt(dg_ref[...], w_ref[:, 3 * D:5 * D])
        dnorm, gg = _rms_bwd(du, x_ref[...], g_ref[...])
        dx_ref[...] = dh1_ref[...] + dnorm
        _acc_out(gg_ref, pl.program_id(0) == 0, gg)

    vec = pl.BlockSpec((1, D), lambda i: (0, 0))
    tile = pl.BlockSpec((tm, D), lambda i: (i, 0))
    wide = pl.BlockSpec((tm, 2 * D), lambda i: (i, 0))
    return pl.pallas_call(
        body, name="in_proj_bwd_x", grid=(seq // tm,),
        in_specs=[wide, tile, wide, pl.BlockSpec((D, IN_COLS), lambda i: (0, 0)), tile, vec, tile, ANY],
        out_specs=[tile, vec],
        out_shape=[pltpu.HBM((seq, D), F32), jax.ShapeDtypeStruct((1, D), F32)],
        compiler_params=_params(("arbitrary",), 48),
    )(*_pinned(d_glu, dp, dgates, w_in, x, g_pre, dh1, token))


def _in_proj_bwd_w(u_t, d_glu, dp, dgates, token):
    seq = u_t.shape[1]
    tm = 2048
    n_t = seq // tm

    def body(u_ref, dglu_ref, dp_ref, dg_ref, _token, out_ref, acc):
        b, i = pl.program_id(0), pl.program_id(1)
        ut = u_ref[...]

        def add(d_ref):
            _acc_out(acc, i == 0, _dot(ut, d_ref[...]))

        pl.when(b < 2)(lambda: add(dglu_ref))
        pl.when(b == 2)(lambda: add(dp_ref))
        pl.when(b > 2)(lambda: add(dg_ref))

        @pl.when(i == n_t - 1)
        def _():
            out_ref[...] = acc[...].astype(BF16)

    return pl.pallas_call(
        body, name="in_proj_bwd_w", grid=(IN_COLS // D, n_t),
        in_specs=[pl.BlockSpec((D, tm), lambda b, i: (0, i)),
                  pl.BlockSpec((tm, D), lambda b, i: (jnp.where(b < 2, i, 0), jnp.minimum(b, 1))),
                  pl.BlockSpec((tm, D), lambda b, i: (jnp.where(b == 2, i, 0), 0)),
                  pl.BlockSpec((tm, D), lambda b, i: (jnp.where(b > 2, i, 0), jnp.maximum(b - 3, 0))), ANY],
        out_specs=pl.BlockSpec((D, D), lambda b, i: (0, b)),
        out_shape=pltpu.HBM((D, IN_COLS), BF16),
        scratch_shapes=[pltpu.VMEM((D, D), F32)],
        compiler_params=_params(("arbitrary", "arbitrary"), 52),
    )(*_pinned(u_t, d_glu, dp, dgates, token))


VEC_NAMES = ("mix_pre_g", "dw_bias", "conv_ln_g", "conv_ln_b", "pool_scale", "mix_post_g", "mlp_pre_g", "mlp_post_g")
WEIGHT_ORDER = ("mix_pre_g", "w_in", "dw_kernel", "dw_bias", "conv_ln_g", "conv_ln_b", "w_conv_out", "pool_w",
                "pool_scale", "w_pool_out", "w_o", "mix_post_g", "mlp_pre_g", "w_ff1", "w_ff2", "mlp_post_g")


def _step(x, loss_target, w, m, v):
    row = lambda a: a.reshape(1, D)
    names = [s[0] for s in SHARDED]

    gather_groups = (("w_in",), ("w_conv_out", "dw_kernel"), ("pool_w", "w_pool_out", "w_o"), ("w_ff1", "w_ff2"))
    legs = [GATHER_ICI, GATHER_D2D]
    first_level, token = _exchange_start(
        "gather_start_w_in", [(gather_groups[0], None, _stage_shards(gather_groups[0], [w["w_in"]], x))], legs)
    later = [n for g in gather_groups[1:] for n in g]
    seeded = dict(zip(later, _stage_shards(later, [w[n] for n in later], token)))
    rest, token = _exchange_start("gather_start", [(g, None, [seeded[n] for n in g]) for g in gather_groups[1:]], legs)
    first_level = first_level + rest
    forwarded, full = {}, {}

    def forward(k, after):
        group, (ici_sems, d2d_sems), _, landed = first_level[k]
        _, landed = _exchange_wait("gather_ici_" + group[0], group, None, landed, [(ici_sems, GATHER_ICI)], after)
        (second,), tok = _exchange_start("gather_forward_" + group[0], [(group, None, landed)], [GATHER_FORWARD])
        forwarded[k] = (second[3], [(d2d_sems, GATHER_D2D), (second[1][0], GATHER_FORWARD)])
        return tok

    def gathered(k, after):
        group = gather_groups[k]
        landed, waits = forwarded[k]
        _, arrays = _exchange_wait("gather_wait_" + group[0], group, None, landed, waits, after)
        full.update(zip(group, arrays))

    u, u_t = _input_norm(x, row(w["mix_pre_g"]), token)
    gathered(0, forward(0, u))
    tok = forward(1, full["w_in"])
    proj = _in_proj_fwd(u, full["w_in"], tok)
    gathered(1, proj)
    tok = forward(2, proj)
    cv, y_conv = _conv_fwd(proj, full["dw_kernel"], row(w["dw_bias"]), row(w["conv_ln_g"]), row(w["conv_ln_b"]),
                           full["w_conv_out"], tok)
    gathered(2, y_conv)
    tok = forward(3, y_conv)
    z, zl, y_pool, merged, o, h1 = _pool_merge_fwd(proj, y_conv, x, full["pool_w"], row(w["pool_scale"]),
                                                   full["w_pool_out"], full["w_o"], row(w["mix_post_g"]), tok)
    gathered(3, h1)
    vv, dm, dh2, sse, g_mlp_post = _mlp_fwd(h1, row(w["mlp_pre_g"]), full["w_ff1"], full["w_ff2"],
                                            row(w["mlp_post_g"]), loss_target)

    shard_of = {name: shard for name, _, shard, _ in SHARDED}

    def scatter_start(tag, group, grads):
        landings = [lax.empty((N_DEV - 1,) + shard_of[n], g.dtype) for n, g in zip(group, grads)]
        (handle,), tok = _exchange_start("scatter_start_" + tag, [(group, grads, landings)], [SCATTER])
        return handle, tok

    dv, g_ff1, g_ff2 = _mlp_bwd(vv, dm, full["w_ff1"], full["w_ff2"])
    h_ff, tok_ff = scatter_start("ff", ("w_ff1", "w_ff2"), [g_ff1, g_ff2])
    dh1, dy_conv, dy_pool, dgates, g_wo, g_mlp_pre, g_mix_post = _merge_bwd(
        dh2, dv, h1, row(w["mlp_pre_g"]), o, row(w["mix_post_g"]), full["w_o"], merged, proj, y_conv, y_pool, tok_ff)
    dp, g_wpo, g_pw, g_pool_scale = _pool_bwd(dy_pool, zl, z, full["pool_w"], row(w["pool_scale"]), full["w_pool_out"],
                                              dh1)
    h_pool, tok_pool = scatter_start("pool", ("w_o", "w_pool_out", "pool_w"), [g_wo, g_wpo, g_pw])
    d_glu, g_wco, g_dw, g_bias, g_ln_g, g_ln_b = _conv_bwd(dy_conv, cv, proj, full["dw_kernel"], row(w["conv_ln_g"]),
                                                            row(w["conv_ln_b"]), full["w_conv_out"], tok_pool)
    h_conv, tok_conv = scatter_start("conv", ("w_conv_out", "dw_kernel"), [g_wco, g_dw[:, :TAPS, :]])
    g_win = _in_proj_bwd_w(u_t, d_glu, dp, dgates, tok_conv)
    h_in, tok_in = scatter_start("in", ("w_in",), [g_win])
    grad_x, g_mix_pre = _in_proj_bwd_x(d_glu, dp, dgates, full["w_in"], x, row(w["mix_pre_g"]), dh1, tok_in)

    out = {}
    slot = jnp.reshape(_my_slot(), (1,)).astype(jnp.int32)
    after = grad_x
    for tag, (group, (sems,), grads, landings) in (("ff", h_ff), ("pool", h_pool), ("conv", h_conv), ("in", h_in)):
        if tag == "in":
            vec_parts = after = _vec_exchange(
                [g_mix_pre, g_bias, g_ln_g, g_ln_b, g_pool_scale, g_mix_post, g_mlp_pre, g_mlp_post,
                 jnp.broadcast_to(sse, (1, D))], after)
        mine, landed = _exchange_wait("scatter_wait_" + tag, group, grads, landings, [(sems, SCATTER)], after)
        for name, own, parts in zip(group, mine, landed):
            out[name] = _adamw(name, own, parts, w[name], m[name], v[name], slot, after)
            after = out[name][0]
    stack = lambda d: jnp.stack([d[n] for n in VEC_NAMES], axis=0)
    *res, loss_row = _adamw_vectors(vec_parts, stack(w), stack(m), stack(v))
    for k, name in enumerate(VEC_NAMES):
        out[name] = [r[k] for r in res]
    return loss_row[0, 0], grad_x, out


def kernel(x, mix_pre_g, w_in, dw_kernel, dw_bias, conv_ln_g, conv_ln_b, w_conv_out, pool_w, pool_scale, w_pool_out, w_o, mix_post_g, mlp_pre_g, w_ff1, w_ff2, mlp_post_g, loss_target, m_mix_pre_g, m_w_in, m_dw_kernel, m_dw_bias, m_conv_ln_g, m_conv_ln_b, m_w_conv_out, m_pool_w, m_pool_scale, m_w_pool_out, m_w_o, m_mix_post_g, m_mlp_pre_g, m_w_ff1, m_w_ff2, m_mlp_post_g, v_mix_pre_g, v_w_in, v_dw_kernel, v_dw_bias, v_conv_ln_g, v_conv_ln_b, v_w_conv_out, v_pool_w, v_pool_scale, v_w_pool_out, v_w_o, v_mix_post_g, v_mlp_pre_g, v_w_ff1, v_w_ff2, v_mlp_post_g):
    w = dict(mix_pre_g=mix_pre_g, w_in=w_in, dw_kernel=dw_kernel, dw_bias=dw_bias, conv_ln_g=conv_ln_g, conv_ln_b=conv_ln_b,
             w_conv_out=w_conv_out, pool_w=pool_w, pool_scale=pool_scale, w_pool_out=w_pool_out, w_o=w_o,
             mix_post_g=mix_post_g, mlp_pre_g=mlp_pre_g, w_ff1=w_ff1, w_ff2=w_ff2, mlp_post_g=mlp_post_g)
    m = dict(mix_pre_g=m_mix_pre_g, w_in=m_w_in, dw_kernel=m_dw_kernel, dw_bias=m_dw_bias, conv_ln_g=m_conv_ln_g,
             conv_ln_b=m_conv_ln_b, w_conv_out=m_w_conv_out, pool_w=m_pool_w, pool_scale=m_pool_scale,
             w_pool_out=m_w_pool_out, w_o=m_w_o, mix_post_g=m_mix_post_g, mlp_pre_g=m_mlp_pre_g, w_ff1=m_w_ff1,
             w_ff2=m_w_ff2, mlp_post_g=m_mlp_post_g)
    v = dict(mix_pre_g=v_mix_pre_g, w_in=v_w_in, dw_kernel=v_dw_kernel, dw_bias=v_dw_bias, conv_ln_g=v_conv_ln_g,
             conv_ln_b=v_conv_ln_b, w_conv_out=v_w_conv_out, pool_w=v_pool_w, pool_scale=v_pool_scale,
             w_pool_out=v_w_pool_out, w_o=v_w_o, mix_post_g=v_mix_post_g, mlp_pre_g=v_mlp_pre_g, w_ff1=v_w_ff1,
             w_ff2=v_w_ff2, mlp_post_g=v_mlp_post_g)
    seq = x.shape[1]
    loss, grad_x, out = _step(x.reshape(seq, D), loss_target.reshape(seq, D), w, m, v)
    grads, deltas, new_m, new_v = ([out[n][k] for n in WEIGHT_ORDER] for k in range(4))
    return (loss, grad_x.reshape(x.shape), *grads, *deltas, *new_m, *new_v)
```

```python
import collections

import jax
import jax.numpy as jnp
from jax import lax
from jax.experimental import pallas as pl
from jax.experimental.pallas import tpu as pltpu

D = 1024
D_FF = 4 * D
IN_COLS = 5 * D
TAPS = 31
HALO = 32
POOL_WINDOWS = (2, 4, 8, 16)
PG = D // 4
N_DEV = 8
RMS_EPS = 1e-6
LN_EPS = 1e-5
ADAM_LR, ADAM_B1, ADAM_B2, ADAM_EPS, ADAM_WD, ADAM_STEP = 0.001, 0.9, 0.999, 1e-08, 0.01, 10

BF16 = jnp.bfloat16
F32 = jnp.float32
MIB = 1 << 20
MESH = pl.DeviceIdType.MESH


def _params(sem, vmem_mib):
    return pltpu.CompilerParams(dimension_semantics=sem, vmem_limit_bytes=vmem_mib * MIB)


def _dot(a, b):
    return jnp.dot(a, b, preferred_element_type=F32)


def _dot_nt(a, b):
    return lax.dot_general(a, b, (((1,), (1,)), ((), ())), preferred_element_type=F32)


def _dot_tn(a, b):
    return lax.dot_general(a, b, (((0,), (0,)), ((), ())), preferred_element_type=F32)


def _rms_scale(x):
    return lax.rsqrt(jnp.mean(x * x, axis=-1, keepdims=True) + RMS_EPS)


def _rms_bwd(dy, x, g):
    xn = x * _rms_scale(x)
    dn = dy * g
    dx = _rms_scale(x) * (dn - xn * jnp.mean(dn * xn, axis=-1, keepdims=True))
    return dx, jnp.sum(dy * xn, axis=0, keepdims=True)


def _sigmoid(x):
    return jax.nn.sigmoid(x)


def _acc_out(ref, first, value):
    @pl.when(first)
    def _():
        ref[...] = value

    @pl.when(jnp.logical_not(first))
    def _():
        ref[...] += value


def _my_slot():
    return 4 * lax.axis_index("x") + 2 * lax.axis_index("y") + lax.axis_index("c")


def _peer(mask):
    x, y, c = lax.axis_index("x"), lax.axis_index("y"), lax.axis_index("c")
    return (x ^ ((mask >> 2) & 1), y ^ ((mask >> 1) & 1), c ^ (mask & 1))


def _cols(width):
    return lambda ref, slot: ref.at[:, pl.ds(pl.multiple_of(slot * width, 128), width)]


def _rows(height):
    return lambda ref, slot: ref.at[pl.ds(pl.multiple_of(slot * height, 8), height), :]


def _lead(ref, slot):
    return ref.at[slot]


def _pool_rows(ref, slot):
    return ref.at[:, pl.ds(pl.multiple_of(slot * (PG // N_DEV), 8), PG // N_DEV), :]


SHARDED = (
    ("w_in", (D, IN_COLS), (D, IN_COLS // N_DEV), _cols(IN_COLS // N_DEV)),
    ("w_ff1", (D, D_FF), (D, D_FF // N_DEV), _cols(D_FF // N_DEV)),
    ("w_ff2", (D_FF, D), (D_FF // N_DEV, D), _rows(D_FF // N_DEV)),
    ("w_conv_out", (D, D), (D // N_DEV, D), _rows(D // N_DEV)),
    ("w_pool_out", (D, D), (D // N_DEV, D), _rows(D // N_DEV)),
    ("w_o", (D, D), (D // N_DEV, D), _rows(D // N_DEV)),
    ("pool_w", (4, PG, PG), (4, PG // N_DEV, PG), _pool_rows),
    ("dw_kernel", (N_DEV, TAPS, D // N_DEV), (TAPS, D // N_DEV), _lead),
)
N_SHARDED = len(SHARDED)


SHARD_AT = {name: at for name, _, _, at in SHARDED}
HBM = pl.BlockSpec(memory_space=pltpu.HBM)
SEM = pl.BlockSpec(memory_space=pltpu.SEMAPHORE)
ANY = pl.BlockSpec(memory_space=pl.ANY)
EFFECT = pltpu.SideEffectType.DATAFLOW_SIDE_EFFECTING


def _in_hbm(a):
    return pltpu.with_memory_space_constraint(a, pltpu.HBM)


def _pinned(*arrays):
    return [_in_hbm(a) for a in arrays]


def _stage_shards(names, shards, token):
    n = len(names)
    specs = [s for name in names for s in SHARDED if s[0] == name]
    dtypes = [F32 if name == "dw_kernel" else BF16 for name in names]

    def body(*refs):
        ins = refs[:n]
        fulls = refs[n + 1:2 * n + 1]
        raw = refs[2 * n + 1:3 * n + 1]
        stage = refs[3 * n + 1:4 * n + 1]
        in_sems, out_sems = refs[4 * n + 1:]
        me = _my_slot()
        loads = [pltpu.make_async_copy(ins[a], raw[a], in_sems.at[a]) for a in range(n)]
        for cp in loads:
            cp.start()
        stores = []
        for a, (_, _, _, at) in enumerate(specs):
            loads[a].wait()
            stage[a][...] = raw[a][...].astype(dtypes[a])
            cp = pltpu.make_async_copy(stage[a], at(fulls[a], me), out_sems.at[a])
            cp.start()
            stores.append(cp)
        for cp in stores:
            cp.wait()

    return pl.pallas_call(
        body, name="stage_" + names[0],
        out_shape=[pltpu.HBM(full, dt) for (_, full, _, _), dt in zip(specs, dtypes)],
        in_specs=[ANY] * (n + 1),
        out_specs=[ANY] * n,
        scratch_shapes=[pltpu.VMEM(shard, F32) for _, _, shard, _ in specs]
        + [pltpu.VMEM(shard, dt) for (_, _, shard, _), dt in zip(specs, dtypes)]
        + [pltpu.SemaphoreType.DMA((n,)), pltpu.SemaphoreType.DMA((n,))],
        compiler_params=pltpu.CompilerParams(vmem_limit_bytes=40 * MIB),
    )(*_pinned(*shards, token))


Leg = collections.namedtuple("Leg", "routes src_of dst_of src_is_land")


def _sem_index(k, m):
    return k * (N_DEV - 1) + m - 1


def _exchange_start(name, groups, legs):
    sizes = [len(g[0]) for g in groups]
    names = [nm for g in groups for nm in g[0]]
    srcs = [s for g in groups if g[1] is not None for s in g[1]]
    lands = [l for g in groups for l in g[2]]
    n_src, n, n_g, n_l = len(srcs), len(names), len(groups), len(legs)

    def body(*refs):
        src_refs, land_refs = list(refs[:n_src]), refs[n_src:n_src + n]
        sems = refs[n_src + n:n_src + n + 2 * n_g * n_l]
        token = refs[-1]
        me = _my_slot()
        first = 0
        for g, size in enumerate(sizes):
            own_src = [src_refs.pop(0) for _ in range(size)] if groups[g][1] is not None else None
            for li, leg in enumerate(legs):
                send, recv = sems[2 * (g * n_l + li)], sems[2 * (g * n_l + li) + 1]
                for m, via in leg.routes:
                    for k in range(size):
                        land = land_refs[first + k]
                        src = land if leg.src_is_land else own_src[k]
                        pltpu.make_async_remote_copy(
                            src_ref=leg.src_of(names[first + k], src, me, m), dst_ref=leg.dst_of(names[first + k], land, me, m),
                            send_sem=send.at[_sem_index(k, m)], recv_sem=recv.at[_sem_index(k, m)],
                            device_id=_peer(via), device_id_type=MESH).start()
            first += size
        token[...] = jnp.zeros_like(token)

    sem_shapes = [pltpu.SemaphoreType.DMA((size * (N_DEV - 1),)) for size in sizes for _ in range(2 * n_l)]
    n_sem = len(sem_shapes)
    outs = pl.pallas_call(
        body, name=name,
        out_shape=sem_shapes + [pltpu.HBM(a.shape, a.dtype) for a in srcs + lands] + [jax.ShapeDtypeStruct((8, 128), F32)],
        in_specs=[HBM] * (n_src + n),
        out_specs=[SEM] * n_sem + [HBM] * (n_src + n) + [pl.BlockSpec(memory_space=pltpu.VMEM)],
        input_output_aliases={k: n_sem + k for k in range(n_src + n)},
        compiler_params=pltpu.CompilerParams(has_side_effects=EFFECT),
    )(*[_in_hbm(a) for a in srcs + lands])
    sems, thru, token = outs[:n_sem], list(outs[n_sem:-1]), outs[-1]
    src_thru, land_thru = thru[:n_src], thru[n_src:]
    handles, first = [], 0
    for g, size in enumerate(sizes):
        pairs = [(sems[2 * (g * n_l + li)], sems[2 * (g * n_l + li) + 1]) for li in range(n_l)]
        mine = [src_thru.pop(0) for _ in range(size)] if groups[g][1] is not None else None
        handles.append((groups[g][0], pairs, mine, land_thru[first:first + size]))
        first += size
    return handles, token


def _exchange_wait(name, names, srcs, lands, waits, after):
    n = len(names)
    n_src = n if srcs is not None else 0

    def body(*refs):
        src_refs, land_refs = refs[:n_src], refs[n_src:n_src + n]
        sems = refs[n_src + n:n_src + n + 2 * len(waits)]
        me = _my_slot()
        for wi, (_, leg) in enumerate(waits):
            for m, via in leg.routes:
                for k in range(n):
                    src = land_refs[k] if leg.src_is_land else src_refs[k]
                    cp = pltpu.make_async_remote_copy(
                        src_ref=leg.src_of(names[k], src, me, m), dst_ref=leg.dst_of(names[k], land_refs[k], me ^ via, m),
                        send_sem=sems[2 * wi].at[_sem_index(k, m)], recv_sem=sems[2 * wi + 1].at[_sem_index(k, m)],
                        device_id=_peer(via), device_id_type=MESH)
                    cp.wait_send()
                    cp.wait_recv()

    arrays = (list(srcs) if srcs is not None else []) + list(lands)
    outs = pl.pallas_call(
        body, name=name,
        out_shape=[pltpu.HBM(a.shape, a.dtype) for a in arrays],
        in_specs=[HBM] * len(arrays) + [SEM] * (2 * len(waits)) + [pl.BlockSpec(memory_space=pl.ANY)],
        out_specs=[HBM] * len(arrays),
        input_output_aliases={k: k for k in range(len(arrays))},
        compiler_params=pltpu.CompilerParams(has_side_effects=EFFECT),
    )(*arrays, *[s for pair, _ in waits for s in pair], after)
    return (outs[:n_src] if srcs is not None else None), outs[n_src:]


def _shard_slot(name, ref, slot):
    return SHARD_AT[name](ref, slot)


GATHER_ICI = Leg(((2, 2), (4, 4), (6, 6)), lambda name, ref, me, m: _shard_slot(name, ref, me),
                 lambda name, ref, sender, m: _shard_slot(name, ref, sender), True)
GATHER_D2D = Leg(((1, 1),), GATHER_ICI.src_of, GATHER_ICI.dst_of, True)
GATHER_FORWARD = Leg(((2, 1), (4, 1), (6, 1)), lambda name, ref, me, m: _shard_slot(name, ref, me ^ m),
                     lambda name, ref, sender, m: _shard_slot(name, ref, sender ^ m), True)
SCATTER = Leg(tuple((m, m) for m in range(1, N_DEV)), lambda name, ref, me, m: _shard_slot(name, ref, me ^ m),
              lambda name, ref, sender, m: ref.at[m - 1], False)


def _vec_exchange(vectors, after):
    n = len(vectors)

    def body(*refs):
        vec, vec_out = refs[n + 2], refs[n + 1]
        send_sems, recv_sems, local_sem = refs[n + 3:]
        for k in range(n):
            vec[k:k + 1, :] = refs[k][...]
        me = _my_slot()
        local = pltpu.make_async_copy(vec, vec_out.at[me], local_sem)
        local.start()
        sends = []
        for mask in range(1, N_DEV):
            cp = pltpu.make_async_remote_copy(
                src_ref=vec, dst_ref=vec_out.at[me], send_sem=send_sems.at[mask - 1],
                recv_sem=recv_sems.at[mask - 1], device_id=_peer(mask), device_id_type=MESH)
            cp.start()
            sends.append(cp)
        for mask in range(1, N_DEV):
            pltpu.make_async_remote_copy(
                src_ref=vec, dst_ref=vec_out.at[me ^ mask], send_sem=send_sems.at[mask - 1],
                recv_sem=recv_sems.at[mask - 1], device_id=_peer(mask), device_id_type=MESH).wait_recv()
        for cp in sends:
            cp.wait_send()
        local.wait()

    return pl.pallas_call(
        body, name="vec_exchange",
        out_shape=jax.ShapeDtypeStruct((N_DEV, n, D), F32),
        in_specs=[pl.BlockSpec(memory_space=pltpu.VMEM)] * n + [ANY],
        out_specs=pl.BlockSpec(memory_space=pl.ANY),
        scratch_shapes=[pltpu.VMEM((n, D), F32), pltpu.SemaphoreType.DMA((N_DEV - 1,)),
                        pltpu.SemaphoreType.DMA((N_DEV - 1,)), pltpu.SemaphoreType.DMA],
    )(*vectors, after)


def _adamw_update(g, w_ref, m_ref, v_ref, g_out, d_out, m_out, v_out):
    m_new = ADAM_B1 * m_ref[...] + (1.0 - ADAM_B1) * g
    v_new = ADAM_B2 * v_ref[...] + (1.0 - ADAM_B2) * (g * g)
    m_hat = m_new / (1.0 - ADAM_B1 ** ADAM_STEP)
    v_hat = v_new / (1.0 - ADAM_B2 ** ADAM_STEP)
    g_out[...] = g
    d_out[...] = -ADAM_LR * (m_hat / (jnp.sqrt(v_hat) + ADAM_EPS) + ADAM_WD * w_ref[...])
    m_out[...] = m_new
    v_out[...] = v_new


ADAMW_ROWS = 256


def _adamw(name, own, parts, w, m, v, slot, token):
    shard = w.shape
    if name in ("w_in", "w_ff1"):
        tr = ADAMW_ROWS
        grid = (shard[0] // tr,)
        own_spec = pl.BlockSpec((tr, shard[1]), lambda i, s: (i, s[0]))
        blk = pl.BlockSpec((tr, shard[1]), lambda i, s: (i, 0))
        parts_spec = pl.BlockSpec((N_DEV - 1, tr, shard[1]), lambda i, s: (0, i, 0))
    elif name == "pool_w":
        grid = (shard[0],)
        own_spec = pl.BlockSpec((None,) + shard[1:], lambda g, s: (g, s[0], 0))
        blk = pl.BlockSpec((None,) + shard[1:], lambda g, s: (g, 0, 0))
        parts_spec = pl.BlockSpec((N_DEV - 1, None) + shard[1:], lambda g, s: (0, g, 0, 0))
    elif name == "dw_kernel":
        grid = (1,)
        own_spec = pl.BlockSpec((None,) + shard, lambda i, s: (s[0], 0, 0))
        blk = pl.BlockSpec(shard, lambda i, s: (0, 0))
        parts_spec = pl.BlockSpec((N_DEV - 1,) + shard, lambda i, s: (0, 0, 0))
    else:
        tr = min(ADAMW_ROWS, shard[0])
        grid = (shard[0] // tr,)
        own_spec = pl.BlockSpec((tr, shard[1]), lambda i, s: (s[0] * grid[0] + i, 0))
        blk = pl.BlockSpec((tr, shard[1]), lambda i, s: (i, 0))
        parts_spec = pl.BlockSpec((N_DEV - 1, tr, shard[1]), lambda i, s: (0, i, 0))

    def body(slot_ref, own_ref, p_ref, w_ref, m_ref, v_ref, _token, g_out, d_out, m_out, v_out):
        g = own_ref[...].astype(F32)
        for k in range(N_DEV - 1):
            g = g + p_ref[k].astype(F32)
        _adamw_update(g, w_ref, m_ref, v_ref, g_out, d_out, m_out, v_out)

    return pl.pallas_call(
        body, name="adamw_" + name,
        grid_spec=pltpu.PrefetchScalarGridSpec(
            num_scalar_prefetch=1, grid=grid, in_specs=[own_spec, parts_spec, blk, blk, blk, ANY], out_specs=[blk] * 4),
        out_shape=[jax.ShapeDtypeStruct(shard, F32)] * 4,
        compiler_params=_params(("arbitrary",), 32),
    )(slot, *_pinned(own, parts, w, m, v, token))


def _adamw_vectors(parts, w, m, v):
    n_vec = w.shape[0]

    def body(p_ref, w_ref, m_ref, v_ref, g_out, d_out, m_out, v_out, loss_out):
        total = p_ref[0]
        for s in range(1, N_DEV):
            total = total + p_ref[s]
        _adamw_update(total[0:n_vec], w_ref, m_ref, v_ref, g_out, d_out, m_out, v_out)
        loss_out[...] = total[n_vec:n_vec + 1] * (0.5 / D)

    return pl.pallas_call(
        body, name="adamw_vectors",
        out_shape=[jax.ShapeDtypeStruct(w.shape, F32)] * 4 + [jax.ShapeDtypeStruct((1, D), F32)],
    )(parts, w, m, v)


def _input_norm(x, g_pre, token):
    seq = x.shape[0]
    tm = 1024

    def body(x_ref, g_ref, _token, u_ref, ut_ref):
        xf = x_ref[...]
        u = (xf * _rms_scale(xf) * g_ref[...]).astype(BF16)
        u_ref[...] = u
        ut_ref[...] = u.T

    return pl.pallas_call(
        body, name="input_norm", grid=(seq // tm,),
        in_specs=[pl.BlockSpec((tm, D), lambda i: (i, 0)), pl.BlockSpec((1, D), lambda i: (0, 0)), ANY],
        out_specs=[pl.BlockSpec((tm, D), lambda i: (i, 0)), pl.BlockSpec((D, tm), lambda i: (0, i))],
        out_shape=[pltpu.HBM((seq, D), BF16), pltpu.HBM((D, seq), BF16)],
        compiler_params=_params(("arbitrary",), 40),
    )(*_pinned(x, g_pre, token))


def _in_proj_fwd(u, w_in, token):
    seq = u.shape[0]
    tm, tn = 1024, IN_COLS // 4

    def body(u_ref, w_ref, _token, proj_ref):
        proj_ref[...] = _dot(u_ref[...], w_ref[...]).astype(BF16)

    return pl.pallas_call(
        body, name="in_proj_fwd", grid=(seq // tm, IN_COLS // tn),
        in_specs=[pl.BlockSpec((tm, D), lambda i, j: (i, 0)), pl.BlockSpec((D, tn), lambda i, j: (0, j)), ANY],
        out_specs=pl.BlockSpec((tm, tn), lambda i, j: (i, j)),
        out_shape=pltpu.HBM((seq, IN_COLS), BF16),
        compiler_params=_params(("arbitrary", "arbitrary"), 40),
    )(*_pinned(u, w_in, token))


CONV_TM = 512
CONV_RS = 128


def _shifts():
    return [(b, [(a, 8 * a + b) for a in range(4) if 8 * a + b < TAPS]) for b in range(8)]


def _taps_looking_back(buf, row0, lanes, weight):
    acc = None
    for b, group in _shifts():
        part = None
        for a, s in group:
            term = weight(TAPS - 1 - s) * buf[pl.ds(row0 - 8 - 8 * a, CONV_RS + 8), lanes]
            part = term if part is None else part + term
        if b:
            part = pltpu.roll(part, b, 0)
        acc = part[8:, :] if acc is None else acc + part[8:, :]
    return acc


def _taps_looking_ahead(buf, row0, lanes, weight):
    acc = None
    for b, group in _shifts():
        part = None
        for a, s in group:
            term = weight(TAPS - 1 - s) * buf[pl.ds(row0 + 8 * a, CONV_RS + 8), lanes]
            part = term if part is None else part + term
        if b:
            part = pltpu.roll(part, CONV_RS + 8 - b, 0)
        acc = part[:CONV_RS, :] if acc is None else acc + part[:CONV_RS, :]
    return acc


def _layer_norm_parts(cv):
    mu = jnp.mean(cv, axis=-1, keepdims=True)
    cen = cv - mu
    rstd = lax.rsqrt(jnp.mean(cen * cen, axis=-1, keepdims=True) + LN_EPS)
    return cen * rstd, rstd


def _conv_fwd(proj, dw, dw_bias, ln_g, ln_b, w_conv_out, token):
    seq = proj.shape[0]
    tm = CONV_TM

    def body(a_ref, gate_ref, dw_ref, bias_ref, lg_ref, lb_ref, w_ref, _token, cv_ref, y_ref, ug):
        i = pl.program_id(0)

        @pl.when(i == 0)
        def _():
            ug[0:HALO, :] = jnp.zeros((HALO, D), F32)

        @pl.when(i > 0)
        def _():
            ug[0:HALO, :] = ug[tm:tm + HALO, :]
        ug[HALO:HALO + tm, :] = a_ref[...].astype(F32) * _sigmoid(gate_ref[...].astype(F32))

        def channel_block(cb, carry):
            lanes = pl.ds(pl.multiple_of(cb * 128, 128), 128)
            for r0 in range(0, tm, CONV_RS):
                taps = _taps_looking_back(ug, HALO + r0, lanes, lambda k: dw_ref[cb, k:k + 1, :])
                cv_ref[pl.ds(r0, CONV_RS), lanes] = taps + bias_ref[:, lanes]
            return carry
        lax.fori_loop(0, D // 128, channel_block, 0)

        n, _ = _layer_norm_parts(cv_ref[...])
        ln = n * lg_ref[...] + lb_ref[...]
        y_ref[...] = _dot((ln * _sigmoid(ln)).astype(BF16), w_ref[...]).astype(BF16)

    vec = pl.BlockSpec((1, D), lambda i: (0, 0))
    tile = pl.BlockSpec((tm, D), lambda i: (i, 0))
    return pl.pallas_call(
        body, name="conv_fwd", grid=(seq // tm,),
        in_specs=[pl.BlockSpec((tm, D), lambda i: (i, 0)), pl.BlockSpec((tm, D), lambda i: (i, 1)),
                  pl.BlockSpec((N_DEV, TAPS, 128), lambda i: (0, 0, 0)), vec, vec, vec,
                  pl.BlockSpec((D, D), lambda i: (0, 0)), ANY],
        out_specs=[tile, tile],
        out_shape=[pltpu.HBM((seq, D), F32), pltpu.HBM((seq, D), BF16)],
        scratch_shapes=[pltpu.VMEM((HALO + tm, D), F32)],
        compiler_params=_params(("arbitrary",), 32),
    )(*_pinned(proj, proj, dw, dw_bias, ln_g, ln_b, w_conv_out, token))


def _window_sums(rows, window, back):
    n = rows.shape[0]
    span = 1
    while span < window:
        rows = rows + pltpu.roll(rows, span if back else n - span, 0)
        span *= 2
    return rows


def _pool_counts(tile_index, tm, window):
    t = tile_index * tm + lax.broadcasted_iota(jnp.int32, (tm, 1), 0)
    return 1.0 / jnp.minimum(t + 1, window).astype(F32)


def _pool_merge_fwd(proj, y_conv, x, pool_w, pool_scale, w_pool_out, w_o, g_post, token):
    seq = proj.shape[0]
    tm = CONV_TM

    def body(p_ref, gc_ref, gp_ref, yc_ref, x_ref, pw_ref, ps_ref, wpo_ref, wo_ref, g_ref, _token,
             z_ref, zl_ref, yp_ref, mg_ref, o_ref, h1_ref, pbuf):
        i = pl.program_id(0)

        @pl.when(i == 0)
        def _():
            pbuf[0:HALO, :] = jnp.zeros((HALO, D), F32)

        @pl.when(i > 0)
        def _():
            pbuf[0:HALO, :] = pbuf[tm:tm + HALO, :]
        pbuf[HALO:HALO + tm, :] = p_ref[...].astype(F32)

        for g, window in enumerate(POOL_WINDOWS):
            lanes = pl.ds(g * PG, PG)
            acc = _window_sums(pbuf[:, lanes], window, back=True)[HALO:, :]
            zg = acc * _pool_counts(i, tm, window) - pbuf[pl.ds(HALO, tm), lanes]
            z_ref[:, lanes] = zg.astype(BF16)
            zl_ref[:, lanes] = _dot(zg.astype(BF16), pw_ref[g])
        zl = zl_ref[...]
        y_pool = _dot((zl * ps_ref[...]).astype(BF16), wpo_ref[...])
        yp_ref[...] = y_pool.astype(BF16)
        merged = (_sigmoid(gc_ref[...].astype(F32)) * yc_ref[...].astype(F32)
                  + _sigmoid(gp_ref[...].astype(F32)) * y_pool).astype(BF16)
        mg_ref[...] = merged
        o = _dot(merged, wo_ref[...])
        o_ref[...] = o
        h1_ref[...] = x_ref[...] + o * _rms_scale(o) * g_ref[...]

    vec = pl.BlockSpec((1, D), lambda i: (0, 0))
    tile = pl.BlockSpec((tm, D), lambda i: (i, 0))
    mat = pl.BlockSpec((D, D), lambda i: (0, 0))
    return pl.pallas_call(
        body, name="pool_merge_fwd", grid=(seq // tm,),
        in_specs=[pl.BlockSpec((tm, D), lambda i: (i, 2)), pl.BlockSpec((tm, D), lambda i: (i, 3)),
                  pl.BlockSpec((tm, D), lambda i: (i, 4)), tile, tile,
                  pl.BlockSpec((4, PG, PG), lambda i: (0, 0, 0)), vec, mat, mat, vec, ANY],
        out_specs=[tile] * 6,
        out_shape=[pltpu.HBM((seq, D), dt) for dt in (BF16, F32, BF16, BF16, F32, F32)],
        scratch_shapes=[pltpu.VMEM((HALO + tm, D), F32)],
        compiler_params=_params(("arbitrary",), 48),
    )(*_pinned(proj, proj, proj, y_conv, x, pool_w, pool_scale, w_pool_out, w_o, g_post, token))


def _mlp_fwd(h1, g_pre, w_ff1, w_ff2, g_post, target):
    seq = h1.shape[0]
    tm, tf = 1024, D_FF // N_DEV
    n_f = D_FF // tf

    def body(h1_ref, gpre_ref, w1_ref, w2_ref, gpost_ref, tgt_ref, v_ref, dm_ref, dh2_ref, sse_ref, ggrad_ref, macc):
        i, j = pl.program_id(0), pl.program_id(1)

        @pl.when(j == 0)
        def _():
            h = h1_ref[...]
            v_ref[...] = (h * _rms_scale(h) * gpre_ref[...]).astype(BF16)
        f = jnp.maximum(_dot(v_ref[...], w1_ref[...]), 0.0)
        part = _dot((f * f).astype(BF16), w2_ref[...])

        @pl.when(j == 0)
        def _():
            macc[...] = part

        @pl.when(j > 0)
        def _():
            macc[...] += part

        @pl.when(j == n_f - 1)
        def _():
            mo = macc[...]
            err = h1_ref[...] + mo * _rms_scale(mo) * gpost_ref[...] - tgt_ref[...]
            dh2 = err * (1.0 / D)
            dh2_ref[...] = dh2
            dm, ggrad = _rms_bwd(dh2, mo, gpost_ref[...])
            dm_ref[...] = dm.astype(BF16)
            _acc_out(ggrad_ref, i == 0, ggrad)
            _acc_out(sse_ref, i == 0, jnp.sum(jnp.sum(err * err, axis=1, keepdims=True), axis=0, keepdims=True))

    vec = pl.BlockSpec((1, D), lambda i, j: (0, 0))
    tile = pl.BlockSpec((tm, D), lambda i, j: (i, 0))
    return pl.pallas_call(
        body, name="mlp_fwd", grid=(seq // tm, n_f),
        in_specs=[tile, vec, pl.BlockSpec((D, tf), lambda i, j: (0, j)), pl.BlockSpec((tf, D), lambda i, j: (j, 0)), vec, tile],
        out_specs=[tile, tile, tile, pl.BlockSpec((1, 1), lambda i, j: (0, 0)), vec],
        out_shape=[pltpu.HBM((seq, D), BF16), pltpu.HBM((seq, D), BF16),
                   pltpu.HBM((seq, D), F32), jax.ShapeDtypeStruct((1, 1), F32),
                   jax.ShapeDtypeStruct((1, D), F32)],
        scratch_shapes=[pltpu.VMEM((tm, D), F32)],
        compiler_params=_params(("arbitrary", "arbitrary"), 56),
    )(*_pinned(h1, g_pre, w_ff1, w_ff2, g_post, target))


def _mlp_bwd(v, dm, w_ff1, w_ff2):
    seq = v.shape[0]
    tm, tf = 1024, D_FF // N_DEV
    n_t = seq // tm

    def body(v_ref, dm_ref, w1_ref, w2_ref, dv_hbm, g1_ref, g2_ref, dv_acc, g1_acc, g2_acc, sems):
        j, i = pl.program_id(0), pl.program_id(1)
        vt, dmt = v_ref[...], dm_ref[...]
        f = jnp.maximum(_dot(vt, w1_ref[...]), 0.0)
        df = (_dot_nt(dmt, w2_ref[...]) * (2.0 * f)).astype(BF16)
        rows = pl.ds(pl.multiple_of(i * tm, tm), tm)
        dv_part = _dot_nt(df, w1_ref[...])

        @pl.when(j == 0)
        def _():
            dv_acc[rows, :] = dv_part

        @pl.when(j > 0)
        def _():
            dv_acc[rows, :] += dv_part
        g1_part = _dot_tn(vt, df)
        g2_part = _dot_tn((f * f).astype(BF16), dmt)

        @pl.when(i == 0)
        def _():
            g1_acc[...] = g1_part
            g2_acc[...] = g2_part

        @pl.when(i > 0)
        def _():
            g1_acc[...] += g1_part
            g2_acc[...] += g2_part

        @pl.when(i == n_t - 1)
        def _():
            g1_ref[...] = g1_acc[...].astype(BF16)
            g2_ref[...] = g2_acc[...].astype(BF16)

        last_pass = j == D_FF // tf - 1

        @pl.when(last_pass)
        def _():
            pltpu.make_async_copy(dv_acc.at[rows, :], dv_hbm.at[rows, :], sems.at[i]).start()

        @pl.when(jnp.logical_and(last_pass, i == n_t - 1))
        def _():
            for t in range(n_t):
                done = pl.ds(t * tm, tm)
                pltpu.make_async_copy(dv_acc.at[done, :], dv_hbm.at[done, :], sems.at[t]).wait()

    tile = pl.BlockSpec((tm, D), lambda j, i: (i, 0))
    return pl.pallas_call(
        body, name="mlp_bwd", grid=(D_FF // tf, n_t),
        in_specs=[tile, tile, pl.BlockSpec((D, tf), lambda j, i: (0, j)), pl.BlockSpec((tf, D), lambda j, i: (j, 0))],
        out_specs=[pl.BlockSpec(memory_space=pl.ANY), pl.BlockSpec((D, tf), lambda j, i: (0, j)),
                   pl.BlockSpec((tf, D), lambda j, i: (j, 0))],
        out_shape=[pltpu.HBM((seq, D), F32), pltpu.HBM((D, D_FF), BF16),
                   pltpu.HBM((D_FF, D), BF16)],
        scratch_shapes=[pltpu.VMEM((seq, D), F32), pltpu.VMEM((D, tf), F32), pltpu.VMEM((tf, D), F32),
                        pltpu.SemaphoreType.DMA((n_t,))],
        compiler_params=_params(("arbitrary", "arbitrary"), 52),
    )(*_pinned(v, dm, w_ff1, w_ff2))


def _merge_bwd(dh2, dv, h1, g_mlp_pre, o, g_mix_post, w_o, merged, proj, y_conv, y_pool, token):
    seq = dh2.shape[0]
    tm = 256
    n_t = seq // tm

    def body(dh2_ref, dv_ref, h1_ref, gpre_ref, o_ref, gpost_ref, wo_ref, mg_ref, gc_ref, gp_ref, yc_ref, yp_ref, _token,
             dh1_ref, dyc_ref, dyp_ref, dg_ref, gwo_ref, ggpre_ref, ggpost_ref, gwo_acc):
        i = pl.program_id(0)
        dnorm, ggpre = _rms_bwd(dv_ref[...], h1_ref[...], gpre_ref[...])
        dh1 = dh2_ref[...] + dnorm
        dh1_ref[...] = dh1
        do, ggpost = _rms_bwd(dh1, o_ref[...], gpost_ref[...])
        do = do.astype(BF16)
        _acc_out(ggpre_ref, i == 0, ggpre)
        _acc_out(ggpost_ref, i == 0, ggpost)
        _acc_out(gwo_acc, i == 0, _dot_tn(mg_ref[...], do))
        dmerged = _dot_nt(do, wo_ref[...])
        sc, sp = _sigmoid(gc_ref[...].astype(F32)), _sigmoid(gp_ref[...].astype(F32))
        dyc_ref[...] = (dmerged * sc).astype(BF16)
        dyp_ref[...] = (dmerged * sp).astype(BF16)
        dg_ref[:, 0:D] = (dmerged * yc_ref[...].astype(F32) * (sc * (1.0 - sc))).astype(BF16)
        dg_ref[:, D:2 * D] = (dmerged * yp_ref[...].astype(F32) * (sp * (1.0 - sp))).astype(BF16)

        @pl.when(i == n_t - 1)
        def _():
            gwo_ref[...] = gwo_acc[...].astype(BF16)

    vec = pl.BlockSpec((1, D), lambda i: (0, 0))
    tile = pl.BlockSpec((tm, D), lambda i: (i, 0))
    mat = pl.BlockSpec((D, D), lambda i: (0, 0))
    return pl.pallas_call(
        body, name="merge_bwd", grid=(n_t,),
        in_specs=[tile, tile, tile, vec, tile, vec, mat, tile,
                  pl.BlockSpec((tm, D), lambda i: (i, 3)), pl.BlockSpec((tm, D), lambda i: (i, 4)), tile, tile, ANY],
        out_specs=[tile, tile, tile, pl.BlockSpec((tm, 2 * D), lambda i: (i, 0)), mat, vec, vec],
        out_shape=[pltpu.HBM((seq, D), F32), pltpu.HBM((seq, D), BF16),
                   pltpu.HBM((seq, D), BF16), pltpu.HBM((seq, 2 * D), BF16),
                   pltpu.HBM((D, D), BF16), jax.ShapeDtypeStruct((1, D), F32),
                   jax.ShapeDtypeStruct((1, D), F32)],
        scratch_shapes=[pltpu.VMEM((D, D), F32)],
        compiler_params=_params(("arbitrary",), 48),
    )(*_pinned(dh2, dv, h1, g_mlp_pre, o, g_mix_post, w_o, merged, proj, proj, y_conv, y_pool, token))


def _pool_bwd(dy_pool, zl, z, pool_w, pool_scale, w_pool_out, token):
    seq = dy_pool.shape[0]
    tm = CONV_TM
    n_t = seq // tm

    def body(dy_ref, zl_ref, z_ref, pw_ref, ps_ref, wpo_ref, _token,
             dp_ref, gwpo_ref, gpw_ref, gps_ref, qbuf, gwpo_acc, gpw_acc):
        i = pl.program_id(0)
        tile_index = n_t - 1 - i
        first = i == 0
        dy = dy_ref[...]
        zl = zl_ref[...]
        dzs = _dot_nt(dy, wpo_ref[...])
        _acc_out(gwpo_acc, first, _dot_tn((zl * ps_ref[...]).astype(BF16), dy))
        _acc_out(gps_ref, first, jnp.sum(dzs * zl, axis=0, keepdims=True))
        dzl = (dzs * ps_ref[...]).astype(BF16)

        @pl.when(first)
        def _():
            qbuf[tm:tm + HALO, :] = jnp.zeros((HALO, D), F32)

        @pl.when(jnp.logical_not(first))
        def _():
            qbuf[tm:tm + HALO, :] = qbuf[0:HALO, :]

        dzs_list = []
        for g, window in enumerate(POOL_WINDOWS):
            lanes = pl.ds(g * PG, PG)
            dzl_g = dzl[:, g * PG:(g + 1) * PG]
            dz = _dot_nt(dzl_g, pw_ref[g])
            _acc_out(gpw_acc.at[g], first, _dot_tn(z_ref[:, lanes], dzl_g))
            qbuf[pl.ds(0, tm), lanes] = dz * _pool_counts(tile_index, tm, window)
            dzs_list.append(dz)
        for g, window in enumerate(POOL_WINDOWS):
            lanes = pl.ds(g * PG, PG)
            acc = _window_sums(qbuf[:, lanes], window, back=False)[:tm, :]
            dp_ref[:, lanes] = (acc - dzs_list[g]).astype(BF16)

        @pl.when(i == n_t - 1)
        def _():
            gwpo_ref[...] = gwpo_acc[...].astype(BF16)
            gpw_ref[...] = gpw_acc[...].astype(BF16)

    vec = pl.BlockSpec((1, D), lambda i: (0, 0))
    tile = pl.BlockSpec((tm, D), lambda i: (n_t - 1 - i, 0))
    mat = pl.BlockSpec((D, D), lambda i: (0, 0))
    pw = pl.BlockSpec((4, PG, PG), lambda i: (0, 0, 0))
    return pl.pallas_call(
        body, name="pool_bwd", grid=(n_t,),
        in_specs=[tile, tile, tile, pw, vec, mat, ANY],
        out_specs=[tile, mat, pw, vec],
        out_shape=[pltpu.HBM((seq, D), BF16), pltpu.HBM((D, D), BF16),
                   pltpu.HBM((4, PG, PG), BF16), jax.ShapeDtypeStruct((1, D), F32)],
        scratch_shapes=[pltpu.VMEM((tm + HALO, D), F32), pltpu.VMEM((D, D), F32), pltpu.VMEM((4, PG, PG), F32)],
        compiler_params=_params(("arbitrary",), 40),
    )(*_pinned(dy_pool, zl, z, pool_w, pool_scale, w_pool_out, token))


def _conv_bwd(dy_conv, cv, proj, dw, ln_g, ln_b, w_conv_out, token):
    seq = dy_conv.shape[0]
    tm = CONV_TM // 2
    n_t = seq // tm
    halo_blocks = tm // HALO

    def body(dy_ref, cv_ref, a_ref, gate_ref, ah_ref, gh_ref, dw_ref, lg_ref, lb_ref, w_ref, _token,
             dglu_ref, gw_ref, gdw_ref, gbias_ref, glg_ref, glb_ref, ug, dcv, dug, gw_acc):
        i = pl.program_id(0)
        tile_index = n_t - 1 - i
        first = i == 0
        dy = dy_ref[...]
        n, rstd = _layer_norm_parts(cv_ref[...])
        ln = n * lg_ref[...] + lb_ref[...]
        sg = _sigmoid(ln)
        _acc_out(gw_acc, first, _dot_tn((ln * sg).astype(BF16), dy))
        dln = _dot_nt(dy, w_ref[...]) * (sg * (1.0 + ln * (1.0 - sg)))
        _acc_out(glg_ref, first, jnp.sum(dln * n, axis=0, keepdims=True))
        _acc_out(glb_ref, first, jnp.sum(dln, axis=0, keepdims=True))
        dn = dln * lg_ref[...]
        dcv_tile = rstd * (dn - jnp.mean(dn, axis=-1, keepdims=True) - n * jnp.mean(dn * n, axis=-1, keepdims=True))
        _acc_out(gbias_ref, first, jnp.sum(dcv_tile, axis=0, keepdims=True))

        @pl.when(first)
        def _():
            dcv[tm:tm + HALO, :] = jnp.zeros((HALO, D), F32)

        @pl.when(jnp.logical_not(first))
        def _():
            dcv[tm:tm + HALO, :] = dcv[0:HALO, :]
        dcv[0:tm, :] = dcv_tile

        a, gate = a_ref[...].astype(F32), gate_ref[...].astype(F32)
        sgate = _sigmoid(gate)
        ug[HALO:HALO + tm, :] = a * sgate
        before = jnp.where(tile_index > 0, 1.0, 0.0)
        ug[0:HALO, :] = ah_ref[...].astype(F32) * _sigmoid(gh_ref[...].astype(F32)) * before

        @pl.when(first)
        def _():
            gdw_ref[...] = jnp.zeros((N_DEV, TAPS + 1, 128), F32)

        def channel_block(cb, carry):
            lanes = pl.ds(pl.multiple_of(cb * 128, 128), 128)
            for r0 in range(0, tm, CONV_RS):
                dug[pl.ds(r0, CONV_RS), lanes] = _taps_looking_ahead(dcv, r0, lanes, lambda k: dw_ref[cb, k:k + 1, :])
            for b, group in _shifts():
                sums = [jnp.zeros((8, 128), F32) for _ in group]
                for r0 in range(0, tm, CONV_RS):
                    window = ug[pl.ds(r0, CONV_RS + HALO), lanes]
                    if b:
                        window = pltpu.roll(window, b, 0)
                    d = dcv[pl.ds(r0, CONV_RS), lanes]
                    for n_a, (a, s) in enumerate(group):
                        prod = d * window[HALO - 8 * a:HALO - 8 * a + CONV_RS, :]
                        sums[n_a] = sums[n_a] + jnp.sum(prod.reshape(CONV_RS // 8, 8, 128), axis=0)
                for n_a, (a, s) in enumerate(group):
                    k = TAPS - 1 - s
                    gdw_ref[cb, k:k + 1, :] += jnp.sum(sums[n_a], axis=0, keepdims=True)
            return carry
        lax.fori_loop(0, D // 128, channel_block, 0)

        d_ug = dug[...]
        dglu_ref[:, 0:D] = (d_ug * sgate).astype(BF16)
        dglu_ref[:, D:2 * D] = (d_ug * a * (sgate * (1.0 - sgate))).astype(BF16)

        @pl.when(i == n_t - 1)
        def _():
            gw_ref[...] = gw_acc[...].astype(BF16)

    def halo_index(col):
        return lambda i: (jnp.maximum((n_t - 1 - i) * halo_blocks - 1, 0), col)

    vec = pl.BlockSpec((1, D), lambda i: (0, 0))
    tile = pl.BlockSpec((tm, D), lambda i: (n_t - 1 - i, 0))
    mat = pl.BlockSpec((D, D), lambda i: (0, 0))
    dwspec = pl.BlockSpec((N_DEV, TAPS, 128), lambda i: (0, 0, 0))
    return pl.pallas_call(
        body, name="conv_bwd", grid=(n_t,),
        in_specs=[tile, tile, pl.BlockSpec((tm, D), lambda i: (n_t - 1 - i, 0)), pl.BlockSpec((tm, D), lambda i: (n_t - 1 - i, 1)),
                  pl.BlockSpec((HALO, D), halo_index(0)), pl.BlockSpec((HALO, D), halo_index(1)), dwspec, vec, vec, mat,
                  ANY],
        out_specs=[pl.BlockSpec((tm, 2 * D), lambda i: (n_t - 1 - i, 0)), mat,
                   pl.BlockSpec((N_DEV, TAPS + 1, 128), lambda i: (0, 0, 0)), vec, vec, vec],
        out_shape=[pltpu.HBM((seq, 2 * D), BF16), pltpu.HBM((D, D), BF16),
                   pltpu.HBM((N_DEV, TAPS + 1, 128), F32), jax.ShapeDtypeStruct((1, D), F32),
                   jax.ShapeDtypeStruct((1, D), F32), jax.ShapeDtypeStruct((1, D), F32)],
        scratch_shapes=[pltpu.VMEM((HALO + tm, D), F32), pltpu.VMEM((tm + HALO, D), F32), pltpu.VMEM((tm, D), F32),
                        pltpu.VMEM((D, D), F32)],
        compiler_params=_params(("arbitrary",), 48),
    )(*_pinned(dy_conv, cv, proj, proj, proj, proj, dw, ln_g, ln_b, w_conv_out, token))


def _in_proj_bwd_x(d_glu, dp, dgates, w_in, x, g_pre, dh1, token):
    seq = x.shape[0]
    tm = 512

    def body(dglu_ref, dp_ref, dg_ref, w_ref, x_ref, g_ref, dh1_ref, _token, dx_ref, gg_ref):
        du = _dot_nt(dglu_ref[...], w_ref[:, 0:2 * D])
        du += _dot_nt(dp_ref[...], w_ref[:, 2 * D:3 * D])
        du += _dot_nt(dg_ref[...], w_ref[:, 3 * D:5 * D])
        dnorm, gg = _rms_bwd(du, x_ref[...], g_ref[...])
        dx_ref[...] = dh1_ref[...] + dnorm
        _acc_out(gg_ref, pl.program_id(0) == 0, gg)

    vec = pl.BlockSpec((1, D), lambda i: (0, 0))
    tile = pl.BlockSpec((tm, D), lambda i: (i, 0))
    wide = pl.BlockSpec((tm, 2 * D), lambda i: (i, 0))
    return pl.pallas_call(
        body, name="in_proj_bwd_x", grid=(seq // tm,),
        in_specs=[wide, tile, wide, pl.BlockSpec((D, IN_COLS), lambda i: (0, 0)), tile, vec, tile, ANY],
        out_specs=[tile, vec],
        out_shape=[pltpu.HBM((seq, D), F32), jax.ShapeDtypeStruct((1, D), F32)],
        compiler_params=_params(("arbitrary",), 48),
    )(*_pinned(d_glu, dp, dgates, w_in, x, g_pre, dh1, token))


def _in_proj_bwd_w(u_t, d_glu, dp, dgates, token):
    seq = u_t.shape[1]
    tm = 2048
    n_t = seq // tm

    def body(u_ref, dglu_ref, dp_ref, dg_ref, _token, out_ref, acc):
        b, i = pl.program_id(0), pl.program_id(1)
        ut = u_ref[...]

        def add(d_ref):
            _acc_out(acc, i == 0, _dot(ut, d_ref[...]))

        pl.when(b < 2)(lambda: add(dglu_ref))
        pl.when(b == 2)(lambda: add(dp_ref))
        pl.when(b > 2)(lambda: add(dg_ref))

        @pl.when(i == n_t - 1)
        def _():
            out_ref[...] = acc[...].astype(BF16)

    return pl.pallas_call(
        body, name="in_proj_bwd_w", grid=(IN_COLS // D, n_t),
        in_specs=[pl.BlockSpec((D, tm), lambda b, i: (0, i)),
                  pl.BlockSpec((tm, D), lambda b, i: (jnp.where(b < 2, i, 0), jnp.minimum(b, 1))),
                  pl.BlockSpec((tm, D), lambda b, i: (jnp.where(b == 2, i, 0), 0)),
                  pl.BlockSpec((tm, D), lambda b, i: (jnp.where(b > 2, i, 0), jnp.maximum(b - 3, 0))), ANY],
        out_specs=pl.BlockSpec((D, D), lambda b, i: (0, b)),
        out_shape=pltpu.HBM((D, IN_COLS), BF16),
        scratch_shapes=[pltpu.VMEM((D, D), F32)],
        compiler_params=_params(("arbitrary", "arbitrary"), 52),
    )(*_pinned(u_t, d_glu, dp, dgates, token))


VEC_NAMES = ("mix_pre_g", "dw_bias", "conv_ln_g", "conv_ln_b", "pool_scale", "mix_post_g", "mlp_pre_g", "mlp_post_g")
WEIGHT_ORDER = ("mix_pre_g", "w_in", "dw_kernel", "dw_bias", "conv_ln_g", "conv_ln_b", "w_conv_out", "pool_w",
                "pool_scale", "w_pool_out", "w_o", "mix_post_g", "mlp_pre_g", "w_ff1", "w_ff2", "mlp_post_g")


def _step(x, loss_target, w, m, v):
    row = lambda a: a.reshape(1, D)
    names = [s[0] for s in SHARDED]

    gather_groups = (("w_in",), ("w_conv_out", "dw_kernel"), ("pool_w", "w_pool_out", "w_o"), ("w_ff1", "w_ff2"))
    legs = [GATHER_ICI, GATHER_D2D]
    first_level, token = _exchange_start(
        "gather_start_w_in", [(gather_groups[0], None, _stage_shards(gather_groups[0], [w["w_in"]], x))], legs)
    later = [n for g in gather_groups[1:] for n in g]
    seeded = dict(zip(later, _stage_shards(later, [w[n] for n in later], token)))
    rest, token = _exchange_start("gather_start", [(g, None, [seeded[n] for n in g]) for g in gather_groups[1:]], legs)
    first_level = first_level + rest
    forwarded, full = {}, {}

    def forward(k, after):
        group, (ici_sems, d2d_sems), _, landed = first_level[k]
        _, landed = _exchange_wait("gather_ici_" + group[0], group, None, landed, [(ici_sems, GATHER_ICI)], after)
        (second,), tok = _exchange_start("gather_forward_" + group[0], [(group, None, landed)], [GATHER_FORWARD])
        forwarded[k] = (second[3], [(d2d_sems, GATHER_D2D), (second[1][0], GATHER_FORWARD)])
        return tok

    def gathered(k, after):
        group = gather_groups[k]
        landed, waits = forwarded[k]
        _, arrays = _exchange_wait("gather_wait_" + group[0], group, None, landed, waits, after)
        full.update(zip(group, arrays))

    u, u_t = _input_norm(x, row(w["mix_pre_g"]), token)
    gathered(0, forward(0, u))
    tok = forward(1, full["w_in"])
    proj = _in_proj_fwd(u, full["w_in"], tok)
    gathered(1, proj)
    tok = forward(2, proj)
    cv, y_conv = _conv_fwd(proj, full["dw_kernel"], row(w["dw_bias"]), row(w["conv_ln_g"]), row(w["conv_ln_b"]),
                           full["w_conv_out"], tok)
    gathered(2, y_conv)
    tok = forward(3, y_conv)
    z, zl, y_pool, merged, o, h1 = _pool_merge_fwd(proj, y_conv, x, full["pool_w"], row(w["pool_scale"]),
                                                   full["w_pool_out"], full["w_o"], row(w["mix_post_g"]), tok)
    gathered(3, h1)
    vv, dm, dh2, sse, g_mlp_post = _mlp_fwd(h1, row(w["mlp_pre_g"]), full["w_ff1"], full["w_ff2"],
                                            row(w["mlp_post_g"]), loss_target)

    shard_of = {name: shard for name, _, shard, _ in SHARDED}

    def scatter_start(tag, group, grads):
        landings = [lax.empty((N_DEV - 1,) + shard_of[n], g.dtype) for n, g in zip(group, grads)]
        (handle,), tok = _exchange_start("scatter_start_" + tag, [(group, grads, landings)], [SCATTER])
        return handle, tok

    dv, g_ff1, g_ff2 = _mlp_bwd(vv, dm, full["w_ff1"], full["w_ff2"])
    h_ff, tok_ff = scatter_start("ff", ("w_ff1", "w_ff2"), [g_ff1, g_ff2])
    dh1, dy_conv, dy_pool, dgates, g_wo, g_mlp_pre, g_mix_post = _merge_bwd(
        dh2, dv, h1, row(w["mlp_pre_g"]), o, row(w["mix_post_g"]), full["w_o"], merged, proj, y_conv, y_pool, tok_ff)
    dp, g_wpo, g_pw, g_pool_scale = _pool_bwd(dy_pool, zl, z, full["pool_w"], row(w["pool_scale"]), full["w_pool_out"],
                                              dh1)
    h_pool, tok_pool = scatter_start("pool", ("w_o", "w_pool_out", "pool_w"), [g_wo, g_wpo, g_pw])
    d_glu, g_wco, g_dw, g_bias, g_ln_g, g_ln_b = _conv_bwd(dy_conv, cv, proj, full["dw_kernel"], row(w["conv_ln_g"]),
                                                            row(w["conv_ln_b"]), full["w_conv_out"], tok_pool)
    h_conv, tok_conv = scatter_start("conv", ("w_conv_out", "dw_kernel"), [g_wco, g_dw[:, :TAPS, :]])
    g_win = _in_proj_bwd_w(u_t, d_glu, dp, dgates, tok_conv)
    h_in, tok_in = scatter_start("in", ("w_in",), [g_win])
    grad_x, g_mix_pre = _in_proj_bwd_x(d_glu, dp, dgates, full["w_in"], x, row(w["mix_pre_g"]), dh1, tok_in)

    out = {}
    slot = jnp.reshape(_my_slot(), (1,)).astype(jnp.int32)
    after = grad_x
    for tag, (group, (sems,), grads, landings) in (("ff", h_ff), ("pool", h_pool), ("conv", h_conv), ("in", h_in)):
        if tag == "in":
            vec_parts = after = _vec_exchange(
                [g_mix_pre, g_bias, g_ln_g, g_ln_b, g_pool_scale, g_mix_post, g_mlp_pre, g_mlp_post,
                 jnp.broadcast_to(sse, (1, D))], after)
        mine, landed = _exchange_wait("scatter_wait_" + tag, group, grads, landings, [(sems, SCATTER)], after)
        for name, own, parts in zip(group, mine, landed):
            out[name] = _adamw(name, own, parts, w[name], m[name], v[name], slot, after)
            after = out[name][0]
    stack = lambda d: jnp.stack([d[n] for n in VEC_NAMES], axis=0)
    *res, loss_row = _adamw_vectors(vec_parts, stack(w), stack(m), stack(v))
    for k, name in enumerate(VEC_NAMES):
        out[name] = [r[k] for r in res]
    return loss_row[0, 0], grad_x, out


def kernel(x, mix_pre_g, w_in, dw_kernel, dw_bias, conv_ln_g, conv_ln_b, w_conv_out, pool_w, pool_scale, w_pool_out, w_o, mix_post_g, mlp_pre_g, w_ff1, w_ff2, mlp_post_g, loss_target, m_mix_pre_g, m_w_in, m_dw_kernel, m_dw_bias, m_conv_ln_g, m_conv_ln_b, m_w_conv_out, m_pool_w, m_pool_scale, m_w_pool_out, m_w_o, m_mix_post_g, m_mlp_pre_g, m_w_ff1, m_w_ff2, m_mlp_post_g, v_mix_pre_g, v_w_in, v_dw_kernel, v_dw_bias, v_conv_ln_g, v_conv_ln_b, v_w_conv_out, v_pool_w, v_pool_scale, v_w_pool_out, v_w_o, v_mix_post_g, v_mlp_pre_g, v_w_ff1, v_w_ff2, v_mlp_post_g):
    w = dict(mix_pre_g=mix_pre_g, w_in=w_in, dw_kernel=dw_kernel, dw_bias=dw_bias, conv_ln_g=conv_ln_g, conv_ln_b=conv_ln_b,
             w_conv_out=w_conv_out, pool_w=pool_w, pool_scale=pool_scale, w_pool_out=w_pool_out, w_o=w_o,
             mix_post_g=mix_post_g, mlp_pre_g=mlp_pre_g, w_ff1=w_ff1, w_ff2=w_ff2, mlp_post_g=mlp_post_g)
    m = dict(mix_pre_g=m_mix_pre_g, w_in=m_w_in, dw_kernel=m_dw_kernel, dw_bias=m_dw_bias, conv_ln_g=m_conv_ln_g,
             conv_ln_b=m_conv_ln_b, w_conv_out=m_w_conv_out, pool_w=m_pool_w, pool_scale=m_pool_scale,
             w_pool_out=m_w_pool_out, w_o=m_w_o, mix_post_g=m_mix_post_g, mlp_pre_g=m_mlp_pre_g, w_ff1=m_w_ff1,
             w_ff2=m_w_ff2, mlp_post_g=m_mlp_post_g)
    v = dict(mix_pre_g=v_mix_pre_g, w_in=v_w_in, dw_kernel=v_dw_kernel, dw_bias=v_dw_bias, conv_ln_g=v_conv_ln_g,
             conv_ln_b=v_conv_ln_b, w_conv_out=v_w_conv_out, pool_w=v_pool_w, pool_scale=v_pool_scale,
             w_pool_out=v_w_pool_out, w_o=v_w_o, mix_post_g=v_mix_post_g, mlp_pre_g=v_mlp_pre_g, w_ff1=v_w_ff1,
             w_ff2=v_w_ff2, mlp_post_g=v_mlp_post_g)
    seq = x.shape[1]
    loss, grad_x, out = _step(x.reshape(seq, D), loss_target.reshape(seq, D), w, m, v)
    grads, deltas, new_m, new_v = ([out[n][k] for n in WEIGHT_ORDER] for k in range(4))
    return (loss, grad_x.reshape(x.shape), *grads, *deltas, *new_m, *new_v)
```

```python
import collections

import jax
import jax.numpy as jnp
from jax import lax
from jax.experimental import pallas as pl
from jax.experimental.pallas import tpu as pltpu

D = 1024
D_FF = 4 * D
IN_COLS = 5 * D
TAPS = 31
HALO = 32
POOL_WINDOWS = (2, 4, 8, 16)
PG = D // 4
N_DEV = 8
RMS_EPS = 1e-6
LN_EPS = 1e-5
ADAM_LR, ADAM_B1, ADAM_B2, ADAM_EPS, ADAM_WD, ADAM_STEP = 0.001, 0.9, 0.999, 1e-08, 0.01, 10

BF16 = jnp.bfloat16
F32 = jnp.float32
MIB = 1 << 20
MESH = pl.DeviceIdType.MESH


def _params(sem, vmem_mib):
    return pltpu.CompilerParams(dimension_semantics=sem, vmem_limit_bytes=vmem_mib * MIB)


def _dot(a, b):
    return jnp.dot(a, b, preferred_element_type=F32)


def _dot_nt(a, b):
    return lax.dot_general(a, b, (((1,), (1,)), ((), ())), preferred_element_type=F32)


def _dot_tn(a, b):
    return lax.dot_general(a, b, (((0,), (0,)), ((), ())), preferred_element_type=F32)


def _rms_scale(x):
    return lax.rsqrt(jnp.mean(x * x, axis=-1, keepdims=True) + RMS_EPS)


def _rms_bwd(dy, x, g):
    xn = x * _rms_scale(x)
    dn = dy * g
    dx = _rms_scale(x) * (dn - xn * jnp.mean(dn * xn, axis=-1, keepdims=True))
    return dx, jnp.sum(dy * xn, axis=0, keepdims=True)


def _sigmoid(x):
    return jax.nn.sigmoid(x)


def _acc_out(ref, first, value):
    @pl.when(first)
    def _():
        ref[...] = value

    @pl.when(jnp.logical_not(first))
    def _():
        ref[...] += value


def _my_slot():
    return 4 * lax.axis_index("x") + 2 * lax.axis_index("y") + lax.axis_index("c")


def _peer(mask):
    x, y, c = lax.axis_index("x"), lax.axis_index("y"), lax.axis_index("c")
    return (x ^ ((mask >> 2) & 1), y ^ ((mask >> 1) & 1), c ^ (mask & 1))


def _cols(width):
    return lambda ref, slot: ref.at[:, pl.ds(pl.multiple_of(slot * width, 128), width)]


def _rows(height):
    return lambda ref, slot: ref.at[pl.ds(pl.multiple_of(slot * height, 8), height), :]


def _lead(ref, slot):
    return ref.at[slot]


def _pool_rows(ref, slot):
    return ref.at[:, pl.ds(pl.multiple_of(slot * (PG // N_DEV), 8), PG // N_DEV), :]


SHARDED = (
    ("w_in", (D, IN_COLS), (D, IN_COLS // N_DEV), _cols(IN_COLS // N_DEV)),
    ("w_ff1", (D, D_FF), (D, D_FF // N_DEV), _cols(D_FF // N_DEV)),
    ("w_ff2", (D_FF, D), (D_FF // N_DEV, D), _rows(D_FF // N_DEV)),
    ("w_conv_out", (D, D), (D // N_DEV, D), _rows(D // N_DEV)),
    ("w_pool_out", (D, D), (D // N_DEV, D), _rows(D // N_DEV)),
    ("w_o", (D, D), (D // N_DEV, D), _rows(D // N_DEV)),
    ("pool_w", (4, PG, PG), (4, PG // N_DEV, PG), _pool_rows),
    ("dw_kernel", (N_DEV, TAPS, D // N_DEV), (TAPS, D // N_DEV), _lead),
)
N_SHARDED = len(SHARDED)


SHARD_AT = {name: at for name, _, _, at in SHARDED}
HBM = pl.BlockSpec(memory_space=pltpu.HBM)
SEM = pl.BlockSpec(memory_space=pltpu.SEMAPHORE)
ANY = pl.BlockSpec(memory_space=pl.ANY)
EFFECT = pltpu.SideEffectType.DATAFLOW_SIDE_EFFECTING


def _in_hbm(a):
    return pltpu.with_memory_space_constraint(a, pltpu.HBM)


def _pinned(*arrays):
    return [_in_hbm(a) for a in arrays]


def _stage_shards(names, shards, token):
    n = len(names)
    specs = [s for name in names for s in SHARDED if s[0] == name]
    dtypes = [F32 if name == "dw_kernel" else BF16 for name in names]

    def body(*refs):
        ins = refs[:n]
        fulls = refs[n + 1:2 * n + 1]
        raw = refs[2 * n + 1:3 * n + 1]
        stage = refs[3 * n + 1:4 * n + 1]
        in_sems, out_sems = refs[4 * n + 1:]
        me = _my_slot()
        loads = [pltpu.make_async_copy(ins[a], raw[a], in_sems.at[a]) for a in range(n)]
        for cp in loads:
            cp.start()
        stores = []
        for a, (_, _, _, at) in enumerate(specs):
            loads[a].wait()
            stage[a][...] = raw[a][...].astype(dtypes[a])
            cp = pltpu.make_async_copy(stage[a], at(fulls[a], me), out_sems.at[a])
            cp.start()
            stores.append(cp)
        for cp in stores:
            cp.wait()

    return pl.pallas_call(
        body, name="stage_" + names[0],
        out_shape=[pltpu.HBM(full, dt) for (_, full, _, _), dt in zip(specs, dtypes)],
        in_specs=[ANY] * (n + 1),
        out_specs=[ANY] * n,
        scratch_shapes=[pltpu.VMEM(shard, F32) for _, _, shard, _ in specs]
        + [pltpu.VMEM(shard, dt) for (_, _, shard, _), dt in zip(specs, dtypes)]
        + [pltpu.SemaphoreType.DMA((n,)), pltpu.SemaphoreType.DMA((n,))],
        compiler_params=pltpu.CompilerParams(vmem_limit_bytes=40 * MIB),
    )(*_pinned(*shards, token))


Leg = collections.namedtuple("Leg", "routes src_of dst_of src_is_land")


def _sem_index(k, m):
    return k * (N_DEV - 1) + m - 1


def _exchange_start(name, groups, legs):
    sizes = [len(g[0]) for g in groups]
    names = [nm for g in groups for nm in g[0]]
    srcs = [s for g in groups if g[1] is not None for s in g[1]]
    lands = [l for g in groups for l in g[2]]
    n_src, n, n_g, n_l = len(srcs), len(names), len(groups), len(legs)

    def body(*refs):
        src_refs, land_refs = list(refs[:n_src]), refs[n_src:n_src + n]
        sems = refs[n_src + n:n_src + n + 2 * n_g * n_l]
        token = refs[-1]
        me = _my_slot()
        first = 0
        for g, size in enumerate(sizes):
            own_src = [src_refs.pop(0) for _ in range(size)] if groups[g][1] is not None else None
            for li, leg in enumerate(legs):
                send, recv = sems[2 * (g * n_l + li)], sems[2 * (g * n_l + li) + 1]
                for m, via in leg.routes:
                    for k in range(size):
                        land = land_refs[first + k]
                        src = land if leg.src_is_land else own_src[k]
                        pltpu.make_async_remote_copy(
                            src_ref=leg.src_of(names[first + k], src, me, m), dst_ref=leg.dst_of(names[first + k], land, me, m),
                            send_sem=send.at[_sem_index(k, m)], recv_sem=recv.at[_sem_index(k, m)],
                            device_id=_peer(via), device_id_type=MESH).start()
            first += size
        token[...] = jnp.zeros_like(token)

    sem_shapes = [pltpu.SemaphoreType.DMA((size * (N_DEV - 1),)) for size in sizes for _ in range(2 * n_l)]
    n_sem = len(sem_shapes)
    outs = pl.pallas_call(
        body, name=name,
        out_shape=sem_shapes + [pltpu.HBM(a.shape, a.dtype) for a in srcs + lands] + [jax.ShapeDtypeStruct((8, 128), F32)],
        in_specs=[HBM] * (n_src + n),
        out_specs=[SEM] * n_sem + [HBM] * (n_src + n) + [pl.BlockSpec(memory_space=pltpu.VMEM)],
        input_output_aliases={k: n_sem + k for k in range(n_src + n)},
        compiler_params=pltpu.CompilerParams(has_side_effects=EFFECT),
    )(*[_in_hbm(a) for a in srcs + lands])
    sems, thru, token = outs[:n_sem], list(outs[n_sem:-1]), outs[-1]
    src_thru, land_thru = thru[:n_src], thru[n_src:]
    handles, first = [], 0
    for g, size in enumerate(sizes):
        pairs = [(sems[2 * (g * n_l + li)], sems[2 * (g * n_l + li) + 1]) for li in range(n_l)]
        mine = [src_thru.pop(0) for _ in range(size)] if groups[g][1] is not None else None
        handles.append((groups[g][0], pairs, mine, land_thru[first:first + size]))
        first += size
    return handles, token


def _exchange_wait(name, names, srcs, lands, waits, after):
    n = len(names)
    n_src = n if srcs is not None else 0

    def body(*refs):
        src_refs, land_refs = refs[:n_src], refs[n_src:n_src + n]
        sems = refs[n_src + n:n_src + n + 2 * len(waits)]
        me = _my_slot()
        for wi, (_, leg) in enumerate(waits):
            for m, via in leg.routes:
                for k in range(n):
                    src = land_refs[k] if leg.src_is_land else src_refs[k]
                    cp = pltpu.make_async_remote_copy(
                        src_ref=leg.src_of(names[k], src, me, m), dst_ref=leg.dst_of(names[k], land_refs[k], me ^ via, m),
                        send_sem=sems[2 * wi].at[_sem_index(k, m)], recv_sem=sems[2 * wi + 1].at[_sem_index(k, m)],
                        device_id=_peer(via), device_id_type=MESH)
                    cp.wait_send()
                    cp.wait_recv()

    arrays = (list(srcs) if srcs is not None else []) + list(lands)
    outs = pl.pallas_call(
        body, name=name,
        out_shape=[pltpu.HBM(a.shape, a.dtype) for a in arrays],
        in_specs=[HBM] * len(arrays) + [SEM] * (2 * len(waits)) + [pl.BlockSpec(memory_space=pl.ANY)],
        out_specs=[HBM] * len(arrays),
        input_output_aliases={k: k for k in range(len(arrays))},
        compiler_params=pltpu.CompilerParams(has_side_effects=EFFECT),
    )(*arrays, *[s for pair, _ in waits for s in pair], after)
    return (outs[:n_src] if srcs is not None else None), outs[n_src:]


def _shard_slot(name, ref, slot):
    return SHARD_AT[name](ref, slot)


GATHER_ICI = Leg(((2, 2), (4, 4), (6, 6)), lambda name, ref, me, m: _shard_slot(name, ref, me),
                 lambda name, ref, sender, m: _shard_slot(name, ref, sender), True)
GATHER_D2D = Leg(((1, 1),), GATHER_ICI.src_of, GATHER_ICI.dst_of, True)
GATHER_FORWARD = Leg(((2, 1), (4, 1), (6, 1)), lambda name, ref, me, m: _shard_slot(name, ref, me ^ m),
                     lambda name, ref, sender, m: _shard_slot(name, ref, sender ^ m), True)
SCATTER = Leg(tuple((m, m) for m in range(1, N_DEV)), lambda name, ref, me, m: _shard_slot(name, ref, me ^ m),
              lambda name, ref, sender, m: ref.at[m - 1], False)


def _vec_exchange(vectors, after):
    n = len(vectors)

    def body(*refs):
        vec, vec_out = refs[n + 2], refs[n + 1]
        send_sems, recv_sems, local_sem = refs[n + 3:]
        for k in range(n):
            vec[k:k + 1, :] = refs[k][...]
        me = _my_slot()
        local = pltpu.make_async_copy(vec, vec_out.at[me], local_sem)
        local.start()
        sends = []
        for mask in range(1, N_DEV):
            cp = pltpu.make_async_remote_copy(
                src_ref=vec, dst_ref=vec_out.at[me], send_sem=send_sems.at[mask - 1],
                recv_sem=recv_sems.at[mask - 1], device_id=_peer(mask), device_id_type=MESH)
            cp.start()
            sends.append(cp)
        for mask in range(1, N_DEV):
            pltpu.make_async_remote_copy(
                src_ref=vec, dst_ref=vec_out.at[me ^ mask], send_sem=send_sems.at[mask - 1],
                recv_sem=recv_sems.at[mask - 1], device_id=_peer(mask), device_id_type=MESH).wait_recv()
        for cp in sends:
            cp.wait_send()
        local.wait()

    return pl.pallas_call(
        body, name="vec_exchange",
        out_shape=jax.ShapeDtypeStruct((N_DEV, n, D), F32),
        in_specs=[pl.BlockSpec(memory_space=pltpu.VMEM)] * n + [ANY],
        out_specs=pl.BlockSpec(memory_space=pl.ANY),
        scratch_shapes=[pltpu.VMEM((n, D), F32), pltpu.SemaphoreType.DMA((N_DEV - 1,)),
                        pltpu.SemaphoreType.DMA((N_DEV - 1,)), pltpu.SemaphoreType.DMA],
    )(*vectors, after)


def _adamw_update(g, w_ref, m_ref, v_ref, g_out, d_out, m_out, v_out):
    m_new = ADAM_B1 * m_ref[...] + (1.0 - ADAM_B1) * g
    v_new = ADAM_B2 * v_ref[...] + (1.0 - ADAM_B2) * (g * g)
    m_hat = m_new / (1.0 - ADAM_B1 ** ADAM_STEP)
    v_hat = v_new / (1.0 - ADAM_B2 ** ADAM_STEP)
    g_out[...] = g
    d_out[...] = -ADAM_LR * (m_hat / (jnp.sqrt(v_hat) + ADAM_EPS) + ADAM_WD * w_ref[...])
    m_out[...] = m_new
    v_out[...] = v_new


ADAMW_ROWS = 256


def _adamw(name, own, parts, w, m, v, slot, token):
    shard = w.shape
    if name in ("w_in", "w_ff1"):
        tr = ADAMW_ROWS
        grid = (shard[0] // tr,)
        own_spec = pl.BlockSpec((tr, shard[1]), lambda i, s: (i, s[0]))
        blk = pl.BlockSpec((tr, shard[1]), lambda i, s: (i, 0))
        parts_spec = pl.BlockSpec((N_DEV - 1, tr, shard[1]), lambda i, s: (0, i, 0))
    elif name == "pool_w":
        grid = (shard[0],)
        own_spec = pl.BlockSpec((None,) + shard[1:], lambda g, s: (g, s[0], 0))
        blk = pl.BlockSpec((None,) + shard[1:], lambda g, s: (g, 0, 0))
        parts_spec = pl.BlockSpec((N_DEV - 1, None) + shard[1:], lambda g, s: (0, g, 0, 0))
    elif name == "dw_kernel":
        grid = (1,)
        own_spec = pl.BlockSpec((None,) + shard, lambda i, s: (s[0], 0, 0))
        blk = pl.BlockSpec(shard, lambda i, s: (0, 0))
        parts_spec = pl.BlockSpec((N_DEV - 1,) + shard, lambda i, s: (0, 0, 0))
    else:
        tr = min(ADAMW_ROWS, shard[0])
        grid = (shard[0] // tr,)
        own_spec = pl.BlockSpec((tr, shard[1]), lambda i, s: (s[0] * grid[0] + i, 0))
        blk = pl.BlockSpec((tr, shard[1]), lambda i, s: (i, 0))
        parts_spec = pl.BlockSpec((N_DEV - 1, tr, shard[1]), lambda i, s: (0, i, 0))

    def body(slot_ref, own_ref, p_ref, w_ref, m_ref, v_ref, _token, g_out, d_out, m_out, v_out):
        g = own_ref[...].astype(F32)
        for k in range(N_DEV - 1):
            g = g + p_ref[k].astype(F32)
        _adamw_update(g, w_ref, m_ref, v_ref, g_out, d_out, m_out, v_out)

    return pl.pallas_call(
        body, name="adamw_" + name,
        grid_spec=pltpu.PrefetchScalarGridSpec(
            num_scalar_prefetch=1, grid=grid, in_specs=[own_spec, parts_spec, blk, blk, blk, ANY], out_specs=[blk] * 4),
        out_shape=[jax.ShapeDtypeStruct(shard, F32)] * 4,
        compiler_params=_params(("arbitrary",), 32),
    )(slot, *_pinned(own, parts, w, m, v, token))


def _adamw_vectors(parts, w, m, v):
    n_vec = w.shape[0]

    def body(p_ref, w_ref, m_ref, v_ref, g_out, d_out, m_out, v_out, loss_out):
        total = p_ref[0]
        for s in range(1, N_DEV):
            total = total + p_ref[s]
        _adamw_update(total[0:n_vec], w_ref, m_ref, v_ref, g_out, d_out, m_out, v_out)
        loss_out[...] = total[n_vec:n_vec + 1] * (0.5 / D)

    return pl.pallas_call(
        body, name="adamw_vectors",
        out_shape=[jax.ShapeDtypeStruct(w.shape, F32)] * 4 + [jax.ShapeDtypeStruct((1, D), F32)],
    )(parts, w, m, v)


def _input_norm(x, g_pre, token):
    seq = x.shape[0]
    tm = 1024

    def body(x_ref, g_ref, _token, u_ref, ut_ref):
        xf = x_ref[...]
        u = (xf * _rms_scale(xf) * g_ref[...]).astype(BF16)
        u_ref[...] = u
        ut_ref[...] = u.T

    return pl.pallas_call(
        body, name="input_norm", grid=(seq // tm,),
        in_specs=[pl.BlockSpec((tm, D), lambda i: (i, 0)), pl.BlockSpec((1, D), lambda i: (0, 0)), ANY],
        out_specs=[pl.BlockSpec((tm, D), lambda i: (i, 0)), pl.BlockSpec((D, tm), lambda i: (0, i))],
        out_shape=[pltpu.HBM((seq, D), BF16), pltpu.HBM((D, seq), BF16)],
        compiler_params=_params(("arbitrary",), 40),
    )(*_pinned(x, g_pre, token))


IN_PROJ_TN = IN_COLS // 4


def _in_proj_fwd(name, u, w_in, blocks, proj, token):
    seq = u.shape[0]
    tm, tn = 1024, IN_PROJ_TN

    def body(_blocks, u_ref, w_ref, _token, _proj_in, proj_ref):
        proj_ref[...] = _dot(u_ref[...], w_ref[...]).astype(BF16)

    return pl.pallas_call(
        body, name=name,
        grid_spec=pltpu.PrefetchScalarGridSpec(
            num_scalar_prefetch=1, grid=(seq // tm, blocks.shape[0]),
            in_specs=[pl.BlockSpec((tm, D), lambda i, j, b: (i, 0)), pl.BlockSpec((D, tn), lambda i, j, b: (0, b[j])),
                      ANY, ANY],
            out_specs=pl.BlockSpec((tm, tn), lambda i, j, b: (i, b[j]))),
        out_shape=pltpu.HBM((seq, IN_COLS), BF16),
        input_output_aliases={4: 0},
        compiler_params=_params(("arbitrary", "arbitrary"), 40),
    )(blocks, *_pinned(u, w_in, token, proj))


CONV_TM = 512
CONV_RS = 128


def _shifts():
    return [(b, [(a, 8 * a + b) for a in range(4) if 8 * a + b < TAPS]) for b in range(8)]


def _taps_looking_back(buf, row0, lanes, weight):
    acc = None
    for b, group in _shifts():
        part = None
        for a, s in group:
            term = weight(TAPS - 1 - s) * buf[pl.ds(row0 - 8 - 8 * a, CONV_RS + 8), lanes]
            part = term if part is None else part + term
        if b:
            part = pltpu.roll(part, b, 0)
        acc = part[8:, :] if acc is None else acc + part[8:, :]
    return acc


def _taps_looking_ahead(buf, row0, lanes, weight):
    acc = None
    for b, group in _shifts():
        part = None
        for a, s in group:
            term = weight(TAPS - 1 - s) * buf[pl.ds(row0 + 8 * a, CONV_RS + 8), lanes]
            part = term if part is None else part + term
        if b:
            part = pltpu.roll(part, CONV_RS + 8 - b, 0)
        acc = part[:CONV_RS, :] if acc is None else acc + part[:CONV_RS, :]
    return acc


def _layer_norm_parts(cv):
    mu = jnp.mean(cv, axis=-1, keepdims=True)
    cen = cv - mu
    rstd = lax.rsqrt(jnp.mean(cen * cen, axis=-1, keepdims=True) + LN_EPS)
    return cen * rstd, rstd


def _conv_fwd(proj, dw, dw_bias, ln_g, ln_b, w_conv_out, token):
    seq = proj.shape[0]
    tm = CONV_TM

    def body(a_ref, gate_ref, dw_ref, bias_ref, lg_ref, lb_ref, w_ref, _token, cv_ref, y_ref, ug):
        i = pl.program_id(0)

        @pl.when(i == 0)
        def _():
            ug[0:HALO, :] = jnp.zeros((HALO, D), F32)

        @pl.when(i > 0)
        def _():
            ug[0:HALO, :] = ug[tm:tm + HALO, :]
        ug[HALO:HALO + tm, :] = a_ref[...].astype(F32) * _sigmoid(gate_ref[...].astype(F32))

        def channel_block(cb, carry):
            lanes = pl.ds(pl.multiple_of(cb * 128, 128), 128)
            for r0 in range(0, tm, CONV_RS):
                taps = _taps_looking_back(ug, HALO + r0, lanes, lambda k: dw_ref[cb, k:k + 1, :])
                cv_ref[pl.ds(r0, CONV_RS), lanes] = taps + bias_ref[:, lanes]
            return carry
        lax.fori_loop(0, D // 128, channel_block, 0)

        n, _ = _layer_norm_parts(cv_ref[...])
        ln = n * lg_ref[...] + lb_ref[...]
        y_ref[...] = _dot((ln * _sigmoid(ln)).astype(BF16), w_ref[...]).astype(BF16)

    vec = pl.BlockSpec((1, D), lambda i: (0, 0))
    tile = pl.BlockSpec((tm, D), lambda i: (i, 0))
    return pl.pallas_call(
        body, name="conv_fwd", grid=(seq // tm,),
        in_specs=[pl.BlockSpec((tm, D), lambda i: (i, 0)), pl.BlockSpec((tm, D), lambda i: (i, 1)),
                  pl.BlockSpec((N_DEV, TAPS, 128), lambda i: (0, 0, 0)), vec, vec, vec,
                  pl.BlockSpec((D, D), lambda i: (0, 0)), ANY],
        out_specs=[tile, tile],
        out_shape=[pltpu.HBM((seq, D), F32), pltpu.HBM((seq, D), BF16)],
        scratch_shapes=[pltpu.VMEM((HALO + tm, D), F32)],
        compiler_params=_params(("arbitrary",), 32),
    )(*_pinned(proj, proj, dw, dw_bias, ln_g, ln_b, w_conv_out, token))


def _window_sums(rows, window, back):
    n = rows.shape[0]
    span = 1
    while span < window:
        rows = rows + pltpu.roll(rows, span if back else n - span, 0)
        span *= 2
    return rows


def _pool_counts(tile_index, tm, window):
    t = tile_index * tm + lax.broadcasted_iota(jnp.int32, (tm, 1), 0)
    return 1.0 / jnp.minimum(t + 1, window).astype(F32)


def _pool_merge_fwd(proj, y_conv, x, pool_w, pool_scale, w_pool_out, w_o, g_post, token):
    seq = proj.shape[0]
    tm = CONV_TM

    def body(p_ref, gc_ref, gp_ref, yc_ref, x_ref, pw_ref, ps_ref, wpo_ref, wo_ref, g_ref, _token,
             z_ref, zl_ref, yp_ref, mg_ref, o_ref, h1_ref, pbuf):
        i = pl.program_id(0)

        @pl.when(i == 0)
        def _():
            pbuf[0:HALO, :] = jnp.zeros((HALO, D), F32)

        @pl.when(i > 0)
        def _():
            pbuf[0:HALO, :] = pbuf[tm:tm + HALO, :]
        pbuf[HALO:HALO + tm, :] = p_ref[...].astype(F32)

        for g, window in enumerate(POOL_WINDOWS):
            lanes = pl.ds(g * PG, PG)
            acc = _window_sums(pbuf[:, lanes], window, back=True)[HALO:, :]
            zg = acc * _pool_counts(i, tm, window) - pbuf[pl.ds(HALO, tm), lanes]
            z_ref[:, lanes] = zg.astype(BF16)
            zl_ref[:, lanes] = _dot(zg.astype(BF16), pw_ref[g])
        zl = zl_ref[...]
        y_pool = _dot((zl * ps_ref[...]).astype(BF16), wpo_ref[...])
        yp_ref[...] = y_pool.astype(BF16)
        merged = (_sigmoid(gc_ref[...].astype(F32)) * yc_ref[...].astype(F32)
                  + _sigmoid(gp_ref[...].astype(F32)) * y_pool).astype(BF16)
        mg_ref[...] = merged
        o = _dot(merged, wo_ref[...])
        o_ref[...] = o
        h1_ref[...] = x_ref[...] + o * _rms_scale(o) * g_ref[...]

    vec = pl.BlockSpec((1, D), lambda i: (0, 0))
    tile = pl.BlockSpec((tm, D), lambda i: (i, 0))
    mat = pl.BlockSpec((D, D), lambda i: (0, 0))
    return pl.pallas_call(
        body, name="pool_merge_fwd", grid=(seq // tm,),
        in_specs=[pl.BlockSpec((tm, D), lambda i: (i, 2)), pl.BlockSpec((tm, D), lambda i: (i, 3)),
                  pl.BlockSpec((tm, D), lambda i: (i, 4)), tile, tile,
                  pl.BlockSpec((4, PG, PG), lambda i: (0, 0, 0)), vec, mat, mat, vec, ANY],
        out_specs=[tile] * 6,
        out_shape=[pltpu.HBM((seq, D), dt) for dt in (BF16, F32, BF16, BF16, F32, F32)],
        scratch_shapes=[pltpu.VMEM((HALO + tm, D), F32)],
        compiler_params=_params(("arbitrary",), 48),
    )(*_pinned(proj, proj, proj, y_conv, x, pool_w, pool_scale, w_pool_out, w_o, g_post, token))


def _mlp_fwd(h1, g_pre, w_ff1, w_ff2, g_post, target):
    seq = h1.shape[0]
    tm, tf = 1024, D_FF // N_DEV
    n_f = D_FF // tf

    def body(h1_ref, gpre_ref, w1_ref, w2_ref, gpost_ref, tgt_ref, v_ref, dm_ref, dh2_ref, sse_ref, ggrad_ref, macc):
        i, j = pl.program_id(0), pl.program_id(1)

        @pl.when(j == 0)
        def _():
            h = h1_ref[...]
            v_ref[...] = (h * _rms_scale(h) * gpre_ref[...]).astype(BF16)
        f = jnp.maximum(_dot(v_ref[...], w1_ref[...]), 0.0)
        part = _dot((f * f).astype(BF16), w2_ref[...])

        @pl.when(j == 0)
        def _():
            macc[...] = part

        @pl.when(j > 0)
        def _():
            macc[...] += part

        @pl.when(j == n_f - 1)
        def _():
            mo = macc[...]
            err = h1_ref[...] + mo * _rms_scale(mo) * gpost_ref[...] - tgt_ref[...]
            dh2 = err * (1.0 / D)
            dh2_ref[...] = dh2
            dm, ggrad = _rms_bwd(dh2, mo, gpost_ref[...])
            dm_ref[...] = dm.astype(BF16)
            _acc_out(ggrad_ref, i == 0, ggrad)
            _acc_out(sse_ref, i == 0, jnp.sum(jnp.sum(err * err, axis=1, keepdims=True), axis=0, keepdims=True))

    vec = pl.BlockSpec((1, D), lambda i, j: (0, 0))
    tile = pl.BlockSpec((tm, D), lambda i, j: (i, 0))
    return pl.pallas_call(
        body, name="mlp_fwd", grid=(seq // tm, n_f),
        in_specs=[tile, vec, pl.BlockSpec((D, tf), lambda i, j: (0, j)), pl.BlockSpec((tf, D), lambda i, j: (j, 0)), vec, tile],
        out_specs=[tile, tile, tile, pl.BlockSpec((1, 1), lambda i, j: (0, 0)), vec],
        out_shape=[pltpu.HBM((seq, D), BF16), pltpu.HBM((seq, D), BF16),
                   pltpu.HBM((seq, D), F32), jax.ShapeDtypeStruct((1, 1), F32),
                   jax.ShapeDtypeStruct((1, D), F32)],
        scratch_shapes=[pltpu.VMEM((tm, D), F32)],
        compiler_params=_params(("arbitrary", "arbitrary"), 56),
    )(*_pinned(h1, g_pre, w_ff1, w_ff2, g_post, target))


def _mlp_bwd(v, dm, w_ff1, w_ff2):
    seq = v.shape[0]
    tm, tf = 1024, D_FF // N_DEV
    n_t = seq // tm

    def body(v_ref, dm_ref, w1_ref, w2_ref, dv_hbm, g1_ref, g2_ref, dv_acc, g1_acc, g2_acc, sems):
        j, i = pl.program_id(0), pl.program_id(1)
        vt, dmt = v_ref[...], dm_ref[...]
        f = jnp.maximum(_dot(vt, w1_ref[...]), 0.0)
        df = (_dot_nt(dmt, w2_ref[...]) * (2.0 * f)).astype(BF16)
        rows = pl.ds(pl.multiple_of(i * tm, tm), tm)
        dv_part = _dot_nt(df, w1_ref[...])

        @pl.when(j == 0)
        def _():
            dv_acc[rows, :] = dv_part

        @pl.when(j > 0)
        def _():
            dv_acc[rows, :] += dv_part
        g1_part = _dot_tn(vt, df)
        g2_part = _dot_tn((f * f).astype(BF16), dmt)

        @pl.when(i == 0)
        def _():
            g1_acc[...] = g1_part
            g2_acc[...] = g2_part

        @pl.when(i > 0)
        def _():
            g1_acc[...] += g1_part
            g2_acc[...] += g2_part

        @pl.when(i == n_t - 1)
        def _():
            g1_ref[...] = g1_acc[...].astype(BF16)
            g2_ref[...] = g2_acc[...].astype(BF16)

        last_pass = j == D_FF // tf - 1

        @pl.when(last_pass)
        def _():
            pltpu.make_async_copy(dv_acc.at[rows, :], dv_hbm.at[rows, :], sems.at[i]).start()

        @pl.when(jnp.logical_and(last_pass, i == n_t - 1))
        def _():
            for t in range(n_t):
                done = pl.ds(t * tm, tm)
                pltpu.make_async_copy(dv_acc.at[done, :], dv_hbm.at[done, :], sems.at[t]).wait()

    tile = pl.BlockSpec((tm, D), lambda j, i: (i, 0))
    return pl.pallas_call(
        body, name="mlp_bwd", grid=(D_FF // tf, n_t),
        in_specs=[tile, tile, pl.BlockSpec((D, tf), lambda j, i: (0, j)), pl.BlockSpec((tf, D), lambda j, i: (j, 0))],
        out_specs=[pl.BlockSpec(memory_space=pl.ANY), pl.BlockSpec((D, tf), lambda j, i: (0, j)),
                   pl.BlockSpec((tf, D), lambda j, i: (j, 0))],
        out_shape=[pltpu.HBM((seq, D), F32), pltpu.HBM((D, D_FF), BF16),
                   pltpu.HBM((D_FF, D), BF16)],
        scratch_shapes=[pltpu.VMEM((seq, D), F32), pltpu.VMEM((D, tf), F32), pltpu.VMEM((tf, D), F32),
                        pltpu.SemaphoreType.DMA((n_t,))],
        compiler_params=_params(("arbitrary", "arbitrary"), 52),
    )(*_pinned(v, dm, w_ff1, w_ff2))


def _merge_bwd(dh2, dv, h1, g_mlp_pre, o, g_mix_post, w_o, merged, proj, y_conv, y_pool, token):
    seq = dh2.shape[0]
    tm = 256
    n_t = seq // tm

    def body(dh2_ref, dv_ref, h1_ref, gpre_ref, o_ref, gpost_ref, wo_ref, mg_ref, gc_ref, gp_ref, yc_ref, yp_ref, _token,
             dh1_ref, dyc_ref, dyp_ref, dg_ref, gwo_ref, ggpre_ref, ggpost_ref, gwo_acc):
        i = pl.program_id(0)
        dnorm, ggpre = _rms_bwd(dv_ref[...], h1_ref[...], gpre_ref[...])
        dh1 = dh2_ref[...] + dnorm
        dh1_ref[...] = dh1
        do, ggpost = _rms_bwd(dh1, o_ref[...], gpost_ref[...])
        do = do.astype(BF16)
        _acc_out(ggpre_ref, i == 0, ggpre)
        _acc_out(ggpost_ref, i == 0, ggpost)
        _acc_out(gwo_acc, i == 0, _dot_tn(mg_ref[...], do))
        dmerged = _dot_nt(do, wo_ref[...])
        sc, sp = _sigmoid(gc_ref[...].astype(F32)), _sigmoid(gp_ref[...].astype(F32))
        dyc_ref[...] = (dmerged * sc).astype(BF16)
        dyp_ref[...] = (dmerged * sp).astype(BF16)
        dg_ref[:, 0:D] = (dmerged * yc_ref[...].astype(F32) * (sc * (1.0 - sc))).astype(BF16)
        dg_ref[:, D:2 * D] = (dmerged * yp_ref[...].astype(F32) * (sp * (1.0 - sp))).astype(BF16)

        @pl.when(i == n_t - 1)
        def _():
            gwo_ref[...] = gwo_acc[...].astype(BF16)

    vec = pl.BlockSpec((1, D), lambda i: (0, 0))
    tile = pl.BlockSpec((tm, D), lambda i: (i, 0))
    mat = pl.BlockSpec((D, D), lambda i: (0, 0))
    return pl.pallas_call(
        body, name="merge_bwd", grid=(n_t,),
        in_specs=[tile, tile, tile, vec, tile, vec, mat, tile,
                  pl.BlockSpec((tm, D), lambda i: (i, 3)), pl.BlockSpec((tm, D), lambda i: (i, 4)), tile, tile, ANY],
        out_specs=[tile, tile, tile, pl.BlockSpec((tm, 2 * D), lambda i: (i, 0)), mat, vec, vec],
        out_shape=[pltpu.HBM((seq, D), F32), pltpu.HBM((seq, D), BF16),
                   pltpu.HBM((seq, D), BF16), pltpu.HBM((seq, 2 * D), BF16),
                   pltpu.HBM((D, D), BF16), jax.ShapeDtypeStruct((1, D), F32),
                   jax.ShapeDtypeStruct((1, D), F32)],
        scratch_shapes=[pltpu.VMEM((D, D), F32)],
        compiler_params=_params(("arbitrary",), 48),
    )(*_pinned(dh2, dv, h1, g_mlp_pre, o, g_mix_post, w_o, merged, proj, proj, y_conv, y_pool, token))


def _pool_bwd(dy_pool, zl, z, pool_w, pool_scale, w_pool_out, token):
    seq = dy_pool.shape[0]
    tm = CONV_TM
    n_t = seq // tm

    def body(dy_ref, zl_ref, z_ref, pw_ref, ps_ref, wpo_ref, _token,
             dp_ref, gwpo_ref, gpw_ref, gps_ref, qbuf, gwpo_acc, gpw_acc):
        i = pl.program_id(0)
        tile_index = n_t - 1 - i
        first = i == 0
        dy = dy_ref[...]
        zl = zl_ref[...]
        dzs = _dot_nt(dy, wpo_ref[...])
        _acc_out(gwpo_acc, first, _dot_tn((zl * ps_ref[...]).astype(BF16), dy))
        _acc_out(gps_ref, first, jnp.sum(dzs * zl, axis=0, keepdims=True))
        dzl = (dzs * ps_ref[...]).astype(BF16)

        @pl.when(first)
        def _():
            qbuf[tm:tm + HALO, :] = jnp.zeros((HALO, D), F32)

        @pl.when(jnp.logical_not(first))
        def _():
            qbuf[tm:tm + HALO, :] = qbuf[0:HALO, :]

        dzs_list = []
        for g, window in enumerate(POOL_WINDOWS):
            lanes = pl.ds(g * PG, PG)
            dzl_g = dzl[:, g * PG:(g + 1) * PG]
            dz = _dot_nt(dzl_g, pw_ref[g])
            _acc_out(gpw_acc.at[g], first, _dot_tn(z_ref[:, lanes], dzl_g))
            qbuf[pl.ds(0, tm), lanes] = dz * _pool_counts(tile_index, tm, window)
            dzs_list.append(dz)
        for g, window in enumerate(POOL_WINDOWS):
            lanes = pl.ds(g * PG, PG)
            acc = _window_sums(qbuf[:, lanes], window, back=False)[:tm, :]
            dp_ref[:, lanes] = (acc - dzs_list[g]).astype(BF16)

        @pl.when(i == n_t - 1)
        def _():
            gwpo_ref[...] = gwpo_acc[...].astype(BF16)
            gpw_ref[...] = gpw_acc[...].astype(BF16)

    vec = pl.BlockSpec((1, D), lambda i: (0, 0))
    tile = pl.BlockSpec((tm, D), lambda i: (n_t - 1 - i, 0))
    mat = pl.BlockSpec((D, D), lambda i: (0, 0))
    pw = pl.BlockSpec((4, PG, PG), lambda i: (0, 0, 0))
    return pl.pallas_call(
        body, name="pool_bwd", grid=(n_t,),
        in_specs=[tile, tile, tile, pw, vec, mat, ANY],
        out_specs=[tile, mat, pw, vec],
        out_shape=[pltpu.HBM((seq, D), BF16), pltpu.HBM((D, D), BF16),
                   pltpu.HBM((4, PG, PG), BF16), jax.ShapeDtypeStruct((1, D), F32)],
        scratch_shapes=[pltpu.VMEM((tm + HALO, D), F32), pltpu.VMEM((D, D), F32), pltpu.VMEM((4, PG, PG), F32)],
        compiler_params=_params(("arbitrary",), 40),
    )(*_pinned(dy_pool, zl, z, pool_w, pool_scale, w_pool_out, token))


def _conv_bwd(dy_conv, cv, proj, dw, ln_g, ln_b, w_conv_out, token):
    seq = dy_conv.shape[0]
    tm = CONV_TM // 2
    n_t = seq // tm
    halo_blocks = tm // HALO

    def body(dy_ref, cv_ref, a_ref, gate_ref, ah_ref, gh_ref, dw_ref, lg_ref, lb_ref, w_ref, _token,
             dglu_ref, gw_ref, gdw_ref, gbias_ref, glg_ref, glb_ref, ug, dcv, dug, gw_acc):
        i = pl.program_id(0)
        tile_index = n_t - 1 - i
        first = i == 0
        dy = dy_ref[...]
        n, rstd = _layer_norm_parts(cv_ref[...])
        ln = n * lg_ref[...] + lb_ref[...]
        sg = _sigmoid(ln)
        _acc_out(gw_acc, first, _dot_tn((ln * sg).astype(BF16), dy))
        dln = _dot_nt(dy, w_ref[...]) * (sg * (1.0 + ln * (1.0 - sg)))
        _acc_out(glg_ref, first, jnp.sum(dln * n, axis=0, keepdims=True))
        _acc_out(glb_ref, first, jnp.sum(dln, axis=0, keepdims=True))
        dn = dln * lg_ref[...]
        dcv_tile = rstd * (dn - jnp.mean(dn, axis=-1, keepdims=True) - n * jnp.mean(dn * n, axis=-1, keepdims=True))
        _acc_out(gbias_ref, first, jnp.sum(dcv_tile, axis=0, keepdims=True))

        @pl.when(first)
        def _():
            dcv[tm:tm + HALO, :] = jnp.zeros((HALO, D), F32)

        @pl.when(jnp.logical_not(first))
        def _():
            dcv[tm:tm + HALO, :] = dcv[0:HALO, :]
        dcv[0:tm, :] = dcv_tile

        a, gate = a_ref[...].astype(F32), gate_ref[...].astype(F32)
        sgate = _sigmoid(gate)
        ug[HALO:HALO + tm, :] = a * sgate
        before = jnp.where(tile_index > 0, 1.0, 0.0)
        ug[0:HALO, :] = ah_ref[...].astype(F32) * _sigmoid(gh_ref[...].astype(F32)) * before

        @pl.when(first)
        def _():
            gdw_ref[...] = jnp.zeros((N_DEV, TAPS + 1, 128), F32)

        def channel_block(cb, carry):
            lanes = pl.ds(pl.multiple_of(cb * 128, 128), 128)
            for r0 in range(0, tm, CONV_RS):
                dug[pl.ds(r0, CONV_RS), lanes] = _taps_looking_ahead(dcv, r0, lanes, lambda k: dw_ref[cb, k:k + 1, :])
            for b, group in _shifts():
                sums = [jnp.zeros((8, 128), F32) for _ in group]
                for r0 in range(0, tm, CONV_RS):
                    window = ug[pl.ds(r0, CONV_RS + HALO), lanes]
                    if b:
                        window = pltpu.roll(window, b, 0)
                    d = dcv[pl.ds(r0, CONV_RS), lanes]
                    for n_a, (a, s) in enumerate(group):
                        prod = d * window[HALO - 8 * a:HALO - 8 * a + CONV_RS, :]
                        sums[n_a] = sums[n_a] + jnp.sum(prod.reshape(CONV_RS // 8, 8, 128), axis=0)
                for n_a, (a, s) in enumerate(group):
                    k = TAPS - 1 - s
                    gdw_ref[cb, k:k + 1, :] += jnp.sum(sums[n_a], axis=0, keepdims=True)
            return carry
        lax.fori_loop(0, D // 128, channel_block, 0)

        d_ug = dug[...]
        dglu_ref[:, 0:D] = (d_ug * sgate).astype(BF16)
        dglu_ref[:, D:2 * D] = (d_ug * a * (sgate * (1.0 - sgate))).astype(BF16)

        @pl.when(i == n_t - 1)
        def _():
            gw_ref[...] = gw_acc[...].astype(BF16)

    def halo_index(col):
        return lambda i: (jnp.maximum((n_t - 1 - i) * halo_blocks - 1, 0), col)

    vec = pl.BlockSpec((1, D), lambda i: (0, 0))
    tile = pl.BlockSpec((tm, D), lambda i: (n_t - 1 - i, 0))
    mat = pl.BlockSpec((D, D), lambda i: (0, 0))
    dwspec = pl.BlockSpec((N_DEV, TAPS, 128), lambda i: (0, 0, 0))
    return pl.pallas_call(
        body, name="conv_bwd", grid=(n_t,),
        in_specs=[tile, tile, pl.BlockSpec((tm, D), lambda i: (n_t - 1 - i, 0)), pl.BlockSpec((tm, D), lambda i: (n_t - 1 - i, 1)),
                  pl.BlockSpec((HALO, D), halo_index(0)), pl.BlockSpec((HALO, D), halo_index(1)), dwspec, vec, vec, mat,
                  ANY],
        out_specs=[pl.BlockSpec((tm, 2 * D), lambda i: (n_t - 1 - i, 0)), mat,
                   pl.BlockSpec((N_DEV, TAPS + 1, 128), lambda i: (0, 0, 0)), vec, vec, vec],
        out_shape=[pltpu.HBM((seq, 2 * D), BF16), pltpu.HBM((D, D), BF16),
                   pltpu.HBM((N_DEV, TAPS + 1, 128), F32), jax.ShapeDtypeStruct((1, D), F32),
                   jax.ShapeDtypeStruct((1, D), F32), jax.ShapeDtypeStruct((1, D), F32)],
        scratch_shapes=[pltpu.VMEM((HALO + tm, D), F32), pltpu.VMEM((tm + HALO, D), F32), pltpu.VMEM((tm, D), F32),
                        pltpu.VMEM((D, D), F32)],
        compiler_params=_params(("arbitrary",), 48),
    )(*_pinned(dy_conv, cv, proj, proj, proj, proj, dw, ln_g, ln_b, w_conv_out, token))


def _in_proj_bwd_x(d_glu, dp, dgates, w_in, x, g_pre, dh1, token):
    seq = x.shape[0]
    tm = 512

    def body(dglu_ref, dp_ref, dg_ref, w_ref, x_ref, g_ref, dh1_ref, _token, dx_ref, gg_ref):
        du = _dot_nt(dglu_ref[...], w_ref[:, 0:2 * D])
        du += _dot_nt(dp_ref[...], w_ref[:, 2 * D:3 * D])
        du += _dot_nt(dg_ref[...], w_ref[:, 3 * D:5 * D])
        dnorm, gg = _rms_bwd(du, x_ref[...], g_ref[...])
        dx_ref[...] = dh1_ref[...] + dnorm
        _acc_out(gg_ref, pl.program_id(0) == 0, gg)

    vec = pl.BlockSpec((1, D), lambda i: (0, 0))
    tile = pl.BlockSpec((tm, D), lambda i: (i, 0))
    wide = pl.BlockSpec((tm, 2 * D), lambda i: (i, 0))
    return pl.pallas_call(
        body, name="in_proj_bwd_x", grid=(seq // tm,),
        in_specs=[wide, tile, wide, pl.BlockSpec((D, IN_COLS), lambda i: (0, 0)), tile, vec, tile, ANY],
        out_specs=[tile, vec],
        out_shape=[pltpu.HBM((seq, D), F32), jax.ShapeDtypeStruct((1, D), F32)],
        compiler_params=_params(("arbitrary",), 48),
    )(*_pinned(d_glu, dp, dgates, w_in, x, g_pre, dh1, token))


def _in_proj_bwd_w(u_t, d_glu, dp, dgates, token):
    seq = u_t.shape[1]
    tm = 2048
    n_t = seq // tm

    def body(u_ref, dglu_ref, dp_ref, dg_ref, _token, out_ref, acc):
        b, i = pl.program_id(0), pl.program_id(1)
        ut = u_ref[...]

        def add(d_ref):
            _acc_out(acc, i == 0, _dot(ut, d_ref[...]))

        pl.when(b < 2)(lambda: add(dglu_ref))
        pl.when(b == 2)(lambda: add(dp_ref))
        pl.when(b > 2)(lambda: add(dg_ref))

        @pl.when(i == n_t - 1)
        def _():
            out_ref[...] = acc[...].astype(BF16)

    return pl.pallas_call(
        body, name="in_proj_bwd_w", grid=(IN_COLS // D, n_t),
        in_specs=[pl.BlockSpec((D, tm), lambda b, i: (0, i)),
                  pl.BlockSpec((tm, D), lambda b, i: (jnp.where(b < 2, i, 0), jnp.minimum(b, 1))),
                  pl.BlockSpec((tm, D), lambda b, i: (jnp.where(b == 2, i, 0), 0)),
                  pl.BlockSpec((tm, D), lambda b, i: (jnp.where(b > 2, i, 0), jnp.maximum(b - 3, 0))), ANY],
        out_specs=pl.BlockSpec((D, D), lambda b, i: (0, b)),
        out_shape=pltpu.HBM((D, IN_COLS), BF16),
        scratch_shapes=[pltpu.VMEM((D, D), F32)],
        compiler_params=_params(("arbitrary", "arbitrary"), 52),
    )(*_pinned(u_t, d_glu, dp, dgates, token))


VEC_NAMES = ("mix_pre_g", "dw_bias", "conv_ln_g", "conv_ln_b", "pool_scale", "mix_post_g", "mlp_pre_g", "mlp_post_g")
WEIGHT_ORDER = ("mix_pre_g", "w_in", "dw_kernel", "dw_bias", "conv_ln_g", "conv_ln_b", "w_conv_out", "pool_w",
                "pool_scale", "w_pool_out", "w_o", "mix_post_g", "mlp_pre_g", "w_ff1", "w_ff2", "mlp_post_g")


def _step(x, loss_target, w, m, v):
    row = lambda a: a.reshape(1, D)
    names = [s[0] for s in SHARDED]

    gather_groups = (("w_in",), ("w_conv_out", "dw_kernel"), ("pool_w", "w_pool_out", "w_o"), ("w_ff1", "w_ff2"))
    legs = [GATHER_ICI, GATHER_D2D]
    first_level, token = _exchange_start(
        "gather_start_w_in", [(gather_groups[0], None, _stage_shards(gather_groups[0], [w["w_in"]], x))], legs)
    later = [n for g in gather_groups[1:] for n in g]
    seeded = dict(zip(later, _stage_shards(later, [w[n] for n in later], token)))
    rest, token = _exchange_start("gather_start", [(g, None, [seeded[n] for n in g]) for g in gather_groups[1:]], legs)
    first_level = first_level + rest
    forwarded, full = {}, {}

    def forward(k, after):
        group, (ici_sems, d2d_sems), _, landed = first_level[k]
        _, landed = _exchange_wait("gather_ici_" + group[0], group, None, landed, [(ici_sems, GATHER_ICI)], after)
        (second,), tok = _exchange_start("gather_forward_" + group[0], [(group, None, landed)], [GATHER_FORWARD])
        forwarded[k] = (second[3], [(d2d_sems, GATHER_D2D), (second[1][0], GATHER_FORWARD)])
        return tok

    def gathered(k, after):
        group = gather_groups[k]
        landed, waits = forwarded[k]
        _, arrays = _exchange_wait("gather_wait_" + group[0], group, None, landed, waits, after)
        full.update(zip(group, arrays))

    u, u_t = _input_norm(x, row(w["mix_pre_g"]), token)
    group, (ici_sems, d2d_sems), _, landed = first_level[0]
    _, landed = _exchange_wait("gather_d2d_w_in", group, None, landed, [(d2d_sems, GATHER_D2D)], u)
    home = jnp.reshape(_my_slot() // 2, (1,)).astype(jnp.int32)
    proj = _in_proj_fwd("in_proj_fwd_home", u, landed[0], home, lax.empty((x.shape[0], IN_COLS), BF16), u)
    _, landed = _exchange_wait("gather_ici_w_in", group, None, landed, [(ici_sems, GATHER_ICI)], proj)
    (second,), tok = _exchange_start("gather_forward_w_in", [(group, None, landed)], [GATHER_FORWARD])
    (full["w_in"],) = _exchange_wait("gather_wait_w_in", group, None, second[3], [(second[1][0], GATHER_FORWARD)], tok)[1]
    tok = forward(1, full["w_in"])
    away = jnp.stack([home[0] ^ 1, home[0] ^ 2, home[0] ^ 3]).astype(jnp.int32)
    proj = _in_proj_fwd("in_proj_fwd", u, full["w_in"], away, proj, tok)
    gathered(1, proj)
    tok = forward(2, proj)
    cv, y_conv = _conv_fwd(proj, full["dw_kernel"], row(w["dw_bias"]), row(w["conv_ln_g"]), row(w["conv_ln_b"]),
                           full["w_conv_out"], tok)
    gathered(2, y_conv)
    tok = forward(3, y_conv)
    z, zl, y_pool, merged, o, h1 = _pool_merge_fwd(proj, y_conv, x, full["pool_w"], row(w["pool_scale"]),
                                                   full["w_pool_out"], full["w_o"], row(w["mix_post_g"]), tok)
    gathered(3, h1)
    vv, dm, dh2, sse, g_mlp_post = _mlp_fwd(h1, row(w["mlp_pre_g"]), full["w_ff1"], full["w_ff2"],
                                            row(w["mlp_post_g"]), loss_target)

    shard_of = {name: shard for name, _, shard, _ in SHARDED}

    def scatter_start(tag, group, grads):
        landings = [lax.empty((N_DEV - 1,) + shard_of[n], g.dtype) for n, g in zip(group, grads)]
        (handle,), tok = _exchange_start("scatter_start_" + tag, [(group, grads, landings)], [SCATTER])
        return handle, tok

    dv, g_ff1, g_ff2 = _mlp_bwd(vv, dm, full["w_ff1"], full["w_ff2"])
    h_ff, tok_ff = scatter_start("ff", ("w_ff1", "w_ff2"), [g_ff1, g_ff2])
    dh1, dy_conv, dy_pool, dgates, g_wo, g_mlp_pre, g_mix_post = _merge_bwd(
        dh2, dv, h1, row(w["mlp_pre_g"]), o, row(w["mix_post_g"]), full["w_o"], merged, proj, y_conv, y_pool, tok_ff)
    dp, g_wpo, g_pw, g_pool_scale = _pool_bwd(dy_pool, zl, z, full["pool_w"], row(w["pool_scale"]), full["w_pool_out"],
                                              dh1)
    h_pool, tok_pool = scatter_start("pool", ("w_o", "w_pool_out", "pool_w"), [g_wo, g_wpo, g_pw])
    d_glu, g_wco, g_dw, g_bias, g_ln_g, g_ln_b = _conv_bwd(dy_conv, cv, proj, full["dw_kernel"], row(w["conv_ln_g"]),
                                                            row(w["conv_ln_b"]), full["w_conv_out"], tok_pool)
    h_conv, tok_conv = scatter_start("conv", ("w_conv_out", "dw_kernel"), [g_wco, g_dw[:, :TAPS, :]])
    g_win = _in_proj_bwd_w(u_t, d_glu, dp, dgates, tok_conv)
    h_in, tok_in = scatter_start("in", ("w_in",), [g_win])
    grad_x, g_mix_pre = _in_proj_bwd_x(d_glu, dp, dgates, full["w_in"], x, row(w["mix_pre_g"]), dh1, tok_in)

    out = {}
    slot = jnp.reshape(_my_slot(), (1,)).astype(jnp.int32)
    after = grad_x
    for tag, (group, (sems,), grads, landings) in (("ff", h_ff), ("pool", h_pool), ("conv", h_conv), ("in", h_in)):
        if tag == "in":
            vec_parts = after = _vec_exchange(
                [g_mix_pre, g_bias, g_ln_g, g_ln_b, g_pool_scale, g_mix_post, g_mlp_pre, g_mlp_post,
                 jnp.broadcast_to(sse, (1, D))], after)
        mine, landed = _exchange_wait("scatter_wait_" + tag, group, grads, landings, [(sems, SCATTER)], after)
        for name, own, parts in zip(group, mine, landed):
            out[name] = _adamw(name, own, parts, w[name], m[name], v[name], slot, after)
            after = out[name][0]
    stack = lambda d: jnp.stack([d[n] for n in VEC_NAMES], axis=0)
    *res, loss_row = _adamw_vectors(vec_parts, stack(w), stack(m), stack(v))
    for k, name in enumerate(VEC_NAMES):
        out[name] = [r[k] for r in res]
    return loss_row[0, 0], grad_x, out


def kernel(x, mix_pre_g, w_in, dw_kernel, dw_bias, conv_ln_g, conv_ln_b, w_conv_out, pool_w, pool_scale, w_pool_out, w_o, mix_post_g, mlp_pre_g, w_ff1, w_ff2, mlp_post_g, loss_target, m_mix_pre_g, m_w_in, m_dw_kernel, m_dw_bias, m_conv_ln_g, m_conv_ln_b, m_w_conv_out, m_pool_w, m_pool_scale, m_w_pool_out, m_w_o, m_mix_post_g, m_mlp_pre_g, m_w_ff1, m_w_ff2, m_mlp_post_g, v_mix_pre_g, v_w_in, v_dw_kernel, v_dw_bias, v_conv_ln_g, v_conv_ln_b, v_w_conv_out, v_pool_w, v_pool_scale, v_w_pool_out, v_w_o, v_mix_post_g, v_mlp_pre_g, v_w_ff1, v_w_ff2, v_mlp_post_g):
    w = dict(mix_pre_g=mix_pre_g, w_in=w_in, dw_kernel=dw_kernel, dw_bias=dw_bias, conv_ln_g=conv_ln_g, conv_ln_b=conv_ln_b,
             w_conv_out=w_conv_out, pool_w=pool_w, pool_scale=pool_scale, w_pool_out=w_pool_out, w_o=w_o,
             mix_post_g=mix_post_g, mlp_pre_g=mlp_pre_g, w_ff1=w_ff1, w_ff2=w_ff2, mlp_post_g=mlp_post_g)
    m = dict(mix_pre_g=m_mix_pre_g, w_in=m_w_in, dw_kernel=m_dw_kernel, dw_bias=m_dw_bias, conv_ln_g=m_conv_ln_g,
             conv_ln_b=m_conv_ln_b, w_conv_out=m_w_conv_out, pool_w=m_pool_w, pool_scale=m_pool_scale,
             w_pool_out=m_w_pool_out, w_o=m_w_o, mix_post_g=m_mix_post_g, mlp_pre_g=m_mlp_pre_g, w_ff1=m_w_ff1,
             w_ff2=m_w_ff2, mlp_post_g=m_mlp_post_g)
    v = dict(mix_pre_g=v_mix_pre_g, w_in=v_w_in, dw_kernel=v_dw_kernel, dw_bias=v_dw_bias, conv_ln_g=v_conv_ln_g,
             conv_ln_b=v_conv_ln_b, w_conv_out=v_w_conv_out, pool_w=v_pool_w, pool_scale=v_pool_scale,
             w_pool_out=v_w_pool_out, w_o=v_w_o, mix_post_g=v_mix_post_g, mlp_pre_g=v_mlp_pre_g, w_ff1=v_w_ff1,
             w_ff2=v_w_ff2, mlp_post_g=v_mlp_post_g)
    seq = x.shape[1]
    loss, grad_x, out = _step(x.reshape(seq, D), loss_target.reshape(seq, D), w, m, v)
    grads, deltas, new_m, new_v = ([out[n][k] for n in WEIGHT_ORDER] for k in range(4))
    return (loss, grad_x.reshape(x.shape), *grads, *deltas, *new_m, *new_v)
```

```python
import collections

import jax
import jax.numpy as jnp
from jax import lax
from jax.experimental import pallas as pl
from jax.experimental.pallas import tpu as pltpu

D = 1024
D_FF = 4 * D
IN_COLS = 5 * D
TAPS = 31
HALO = 32
POOL_WINDOWS = (2, 4, 8, 16)
PG = D // 4
N_DEV = 8
RMS_EPS = 1e-6
LN_EPS = 1e-5
ADAM_LR, ADAM_B1, ADAM_B2, ADAM_EPS, ADAM_WD, ADAM_STEP = 0.001, 0.9, 0.999, 1e-08, 0.01, 10

BF16 = jnp.bfloat16
F32 = jnp.float32
MIB = 1 << 20
MESH = pl.DeviceIdType.MESH


def _params(sem, vmem_mib):
    return pltpu.CompilerParams(dimension_semantics=sem, vmem_limit_bytes=vmem_mib * MIB)


def _dot(a, b):
    return jnp.dot(a, b, preferred_element_type=F32)


def _dot_nt(a, b):
    return lax.dot_general(a, b, (((1,), (1,)), ((), ())), preferred_element_type=F32)


def _dot_tn(a, b):
    return lax.dot_general(a, b, (((0,), (0,)), ((), ())), preferred_element_type=F32)


def _rms_scale(x):
    return lax.rsqrt(jnp.mean(x * x, axis=-1, keepdims=True) + RMS_EPS)


def _rms_bwd(dy, x, g):
    xn = x * _rms_scale(x)
    dn = dy * g
    dx = _rms_scale(x) * (dn - xn * jnp.mean(dn * xn, axis=-1, keepdims=True))
    return dx, jnp.sum(dy * xn, axis=0, keepdims=True)


def _sigmoid(x):
    return jax.nn.sigmoid(x)


def _acc_out(ref, first, value):
    @pl.when(first)
    def _():
        ref[...] = value

    @pl.when(jnp.logical_not(first))
    def _():
        ref[...] += value


def _my_slot():
    return 4 * lax.axis_index("x") + 2 * lax.axis_index("y") + lax.axis_index("c")


def _peer(mask):
    x, y, c = lax.axis_index("x"), lax.axis_index("y"), lax.axis_index("c")
    return (x ^ ((mask >> 2) & 1), y ^ ((mask >> 1) & 1), c ^ (mask & 1))


def _cols(width):
    return lambda ref, slot: ref.at[:, pl.ds(pl.multiple_of(slot * width, 128), width)]


def _rows(height):
    return lambda ref, slot: ref.at[pl.ds(pl.multiple_of(slot * height, 8), height), :]


def _lead(ref, slot):
    return ref.at[slot]


def _pool_rows(ref, slot):
    return ref.at[:, pl.ds(pl.multiple_of(slot * (PG // N_DEV), 8), PG // N_DEV), :]


SHARDED = (
    ("w_in", (D, IN_COLS), (D, IN_COLS // N_DEV), _cols(IN_COLS // N_DEV)),
    ("w_ff1", (D, D_FF), (D, D_FF // N_DEV), _cols(D_FF // N_DEV)),
    ("w_ff2", (D_FF, D), (D_FF // N_DEV, D), _rows(D_FF // N_DEV)),
    ("w_conv_out", (D, D), (D // N_DEV, D), _rows(D // N_DEV)),
    ("w_pool_out", (D, D), (D // N_DEV, D), _rows(D // N_DEV)),
    ("w_o", (D, D), (D // N_DEV, D), _rows(D // N_DEV)),
    ("pool_w", (4, PG, PG), (4, PG // N_DEV, PG), _pool_rows),
    ("dw_kernel", (N_DEV, TAPS, D // N_DEV), (TAPS, D // N_DEV), _lead),
)
N_SHARDED = len(SHARDED)


SHARD_AT = {name: at for name, _, _, at in SHARDED}
HBM = pl.BlockSpec(memory_space=pltpu.HBM)
SEM = pl.BlockSpec(memory_space=pltpu.SEMAPHORE)
ANY = pl.BlockSpec(memory_space=pl.ANY)
EFFECT = pltpu.SideEffectType.DATAFLOW_SIDE_EFFECTING


def _in_hbm(a):
    return pltpu.with_memory_space_constraint(a, pltpu.HBM)


def _pinned(*arrays):
    return [_in_hbm(a) for a in arrays]


def _stage_shards(names, shards, token):
    n = len(names)
    specs = [s for name in names for s in SHARDED if s[0] == name]
    dtypes = [F32 if name == "dw_kernel" else BF16 for name in names]

    def body(*refs):
        ins = refs[:n]
        fulls = refs[n + 1:2 * n + 1]
        raw = refs[2 * n + 1:3 * n + 1]
        stage = refs[3 * n + 1:4 * n + 1]
        in_sems, out_sems = refs[4 * n + 1:]
        me = _my_slot()
        loads = [pltpu.make_async_copy(ins[a], raw[a], in_sems.at[a]) for a in range(n)]
        for cp in loads:
            cp.start()
        stores = []
        for a, (_, _, _, at) in enumerate(specs):
            loads[a].wait()
            stage[a][...] = raw[a][...].astype(dtypes[a])
            cp = pltpu.make_async_copy(stage[a], at(fulls[a], me), out_sems.at[a])
            cp.start()
            stores.append(cp)
        for cp in stores:
            cp.wait()

    return pl.pallas_call(
        body, name="stage_" + names[0],
        out_shape=[pltpu.HBM(full, dt) for (_, full, _, _), dt in zip(specs, dtypes)],
        in_specs=[ANY] * (n + 1),
        out_specs=[ANY] * n,
        scratch_shapes=[pltpu.VMEM(shard, F32) for _, _, shard, _ in specs]
        + [pltpu.VMEM(shard, dt) for (_, _, shard, _), dt in zip(specs, dtypes)]
        + [pltpu.SemaphoreType.DMA((n,)), pltpu.SemaphoreType.DMA((n,))],
        compiler_params=pltpu.CompilerParams(vmem_limit_bytes=40 * MIB),
    )(*_pinned(*shards, token))


Leg = collections.namedtuple("Leg", "routes src_of dst_of src_is_land")


def _sem_index(k, m):
    return k * (N_DEV - 1) + m - 1


def _exchange_start(name, groups, legs):
    sizes = [len(g[0]) for g in groups]
    names = [nm for g in groups for nm in g[0]]
    srcs = [s for g in groups if g[1] is not None for s in g[1]]
    lands = [l for g in groups for l in g[2]]
    n_src, n, n_g, n_l = len(srcs), len(names), len(groups), len(legs)

    def body(*refs):
        src_refs, land_refs = list(refs[:n_src]), refs[n_src:n_src + n]
        sems = refs[n_src + n:n_src + n + 2 * n_g * n_l]
        token = refs[-1]
        me = _my_slot()
        first = 0
        for g, size in enumerate(sizes):
            own_src = [src_refs.pop(0) for _ in range(size)] if groups[g][1] is not None else None
            for li, leg in enumerate(legs):
                send, recv = sems[2 * (g * n_l + li)], sems[2 * (g * n_l + li) + 1]
                for m, via in leg.routes:
                    for k in range(size):
                        land = land_refs[first + k]
                        src = land if leg.src_is_land else own_src[k]
                        pltpu.make_async_remote_copy(
                            src_ref=leg.src_of(names[first + k], src, me, m), dst_ref=leg.dst_of(names[first + k], land, me, m),
                            send_sem=send.at[_sem_index(k, m)], recv_sem=recv.at[_sem_index(k, m)],
                            device_id=_peer(via), device_id_type=MESH).start()
            first += size
        token[...] = jnp.zeros_like(token)

    sem_shapes = [pltpu.SemaphoreType.DMA((size * (N_DEV - 1),)) for size in sizes for _ in range(2 * n_l)]
    n_sem = len(sem_shapes)
    outs = pl.pallas_call(
        body, name=name,
        out_shape=sem_shapes + [pltpu.HBM(a.shape, a.dtype) for a in srcs + lands] + [jax.ShapeDtypeStruct((8, 128), F32)],
        in_specs=[HBM] * (n_src + n),
        out_specs=[SEM] * n_sem + [HBM] * (n_src + n) + [pl.BlockSpec(memory_space=pltpu.VMEM)],
        input_output_aliases={k: n_sem + k for k in range(n_src + n)},
        compiler_params=pltpu.CompilerParams(has_side_effects=EFFECT),
    )(*[_in_hbm(a) for a in srcs + lands])
    sems, thru, token = outs[:n_sem], list(outs[n_sem:-1]), outs[-1]
    src_thru, land_thru = thru[:n_src], thru[n_src:]
    handles, first = [], 0
    for g, size in enumerate(sizes):
        pairs = [(sems[2 * (g * n_l + li)], sems[2 * (g * n_l + li) + 1]) for li in range(n_l)]
        mine = [src_thru.pop(0) for _ in range(size)] if groups[g][1] is not None else None
        handles.append((groups[g][0], pairs, mine, land_thru[first:first + size]))
        first += size
    return handles, token


def _exchange_wait(name, names, srcs, lands, waits, after):
    n = len(names)
    n_src = n if srcs is not None else 0

    def body(*refs):
        src_refs, land_refs = refs[:n_src], refs[n_src:n_src + n]
        sems = refs[n_src + n:n_src + n + 2 * len(waits)]
        me = _my_slot()
        for wi, (_, leg) in enumerate(waits):
            for m, via in leg.routes:
                for k in range(n):
                    src = land_refs[k] if leg.src_is_land else src_refs[k]
                    cp = pltpu.make_async_remote_copy(
                        src_ref=leg.src_of(names[k], src, me, m), dst_ref=leg.dst_of(names[k], land_refs[k], me ^ via, m),
                        send_sem=sems[2 * wi].at[_sem_index(k, m)], recv_sem=sems[2 * wi + 1].at[_sem_index(k, m)],
                        device_id=_peer(via), device_id_type=MESH)
                    cp.wait_send()
                    cp.wait_recv()

    arrays = (list(srcs) if srcs is not None else []) + list(lands)
    outs = pl.pallas_call(
        body, name=name,
        out_shape=[pltpu.HBM(a.shape, a.dtype) for a in arrays],
        in_specs=[HBM] * len(arrays) + [SEM] * (2 * len(waits)) + [pl.BlockSpec(memory_space=pl.ANY)],
        out_specs=[HBM] * len(arrays),
        input_output_aliases={k: k for k in range(len(arrays))},
        compiler_params=pltpu.CompilerParams(has_side_effects=EFFECT),
    )(*arrays, *[s for pair, _ in waits for s in pair], after)
    return (outs[:n_src] if srcs is not None else None), outs[n_src:]


def _shard_slot(name, ref, slot):
    return SHARD_AT[name](ref, slot)


GATHER_ICI = Leg(((2, 2), (4, 4), (6, 6)), lambda name, ref, me, m: _shard_slot(name, ref, me),
                 lambda name, ref, sender, m: _shard_slot(name, ref, sender), True)
GATHER_D2D = Leg(((1, 1),), GATHER_ICI.src_of, GATHER_ICI.dst_of, True)
GATHER_FORWARD = Leg(((2, 1), (4, 1), (6, 1)), lambda name, ref, me, m: _shard_slot(name, ref, me ^ m),
                     lambda name, ref, sender, m: _shard_slot(name, ref, sender ^ m), True)
SCATTER = Leg(tuple((m, m) for m in range(1, N_DEV)), lambda name, ref, me, m: _shard_slot(name, ref, me ^ m),
              lambda name, ref, sender, m: ref.at[m - 1], False)


def _vec_exchange(vectors, after):
    n = len(vectors)

    def body(*refs):
        vec, vec_out = refs[n + 2], refs[n + 1]
        send_sems, recv_sems, local_sem = refs[n + 3:]
        for k in range(n):
            vec[k:k + 1, :] = refs[k][...]
        me = _my_slot()
        local = pltpu.make_async_copy(vec, vec_out.at[me], local_sem)
        local.start()
        sends = []
        for mask in range(1, N_DEV):
            cp = pltpu.make_async_remote_copy(
                src_ref=vec, dst_ref=vec_out.at[me], send_sem=send_sems.at[mask - 1],
                recv_sem=recv_sems.at[mask - 1], device_id=_peer(mask), device_id_type=MESH)
            cp.start()
            sends.append(cp)
        for mask in range(1, N_DEV):
            pltpu.make_async_remote_copy(
                src_ref=vec, dst_ref=vec_out.at[me ^ mask], send_sem=send_sems.at[mask - 1],
                recv_sem=recv_sems.at[mask - 1], device_id=_peer(mask), device_id_type=MESH).wait_recv()
        for cp in sends:
            cp.wait_send()
        local.wait()

    return pl.pallas_call(
        body, name="vec_exchange",
        out_shape=jax.ShapeDtypeStruct((N_DEV, n, D), F32),
        in_specs=[pl.BlockSpec(memory_space=pltpu.VMEM)] * n + [ANY],
        out_specs=pl.BlockSpec(memory_space=pl.ANY),
        scratch_shapes=[pltpu.VMEM((n, D), F32), pltpu.SemaphoreType.DMA((N_DEV - 1,)),
                        pltpu.SemaphoreType.DMA((N_DEV - 1,)), pltpu.SemaphoreType.DMA],
    )(*vectors, after)


def _adamw_update(g, w_ref, m_ref, v_ref, g_out, d_out, m_out, v_out):
    m_new = ADAM_B1 * m_ref[...] + (1.0 - ADAM_B1) * g
    v_new = ADAM_B2 * v_ref[...] + (1.0 - ADAM_B2) * (g * g)
    m_hat = m_new / (1.0 - ADAM_B1 ** ADAM_STEP)
    v_hat = v_new / (1.0 - ADAM_B2 ** ADAM_STEP)
    g_out[...] = g
    d_out[...] = -ADAM_LR * (m_hat / (jnp.sqrt(v_hat) + ADAM_EPS) + ADAM_WD * w_ref[...])
    m_out[...] = m_new
    v_out[...] = v_new


ADAMW_ROWS = 256


def _adamw(name, own, parts, w, m, v, slot, token):
    shard = w.shape
    if name in ("w_in", "w_ff1"):
        tr = ADAMW_ROWS
        grid = (shard[0] // tr,)
        own_spec = pl.BlockSpec((tr, shard[1]), lambda i, s: (i, s[0]))
        blk = pl.BlockSpec((tr, shard[1]), lambda i, s: (i, 0))
        parts_spec = pl.BlockSpec((N_DEV - 1, tr, shard[1]), lambda i, s: (0, i, 0))
    elif name == "pool_w":
        grid = (shard[0],)
        own_spec = pl.BlockSpec((None,) + shard[1:], lambda g, s: (g, s[0], 0))
        blk = pl.BlockSpec((None,) + shard[1:], lambda g, s: (g, 0, 0))
        parts_spec = pl.BlockSpec((N_DEV - 1, None) + shard[1:], lambda g, s: (0, g, 0, 0))
    elif name == "dw_kernel":
        grid = (1,)
        own_spec = pl.BlockSpec((None,) + shard, lambda i, s: (s[0], 0, 0))
        blk = pl.BlockSpec(shard, lambda i, s: (0, 0))
        parts_spec = pl.BlockSpec((N_DEV - 1,) + shard, lambda i, s: (0, 0, 0))
    else:
        tr = min(ADAMW_ROWS, shard[0])
        grid = (shard[0] // tr,)
        own_spec = pl.BlockSpec((tr, shard[1]), lambda i, s: (s[0] * grid[0] + i, 0))
        blk = pl.BlockSpec((tr, shard[1]), lambda i, s: (i, 0))
        parts_spec = pl.BlockSpec((N_DEV - 1, tr, shard[1]), lambda i, s: (0, i, 0))

    def body(slot_ref, own_ref, p_ref, w_ref, m_ref, v_ref, _token, g_out, d_out, m_out, v_out):
        g = own_ref[...].astype(F32)
        for k in range(N_DEV - 1):
            g = g + p_ref[k].astype(F32)
        _adamw_update(g, w_ref, m_ref, v_ref, g_out, d_out, m_out, v_out)

    return pl.pallas_call(
        body, name="adamw_" + name,
        grid_spec=pltpu.PrefetchScalarGridSpec(
            num_scalar_prefetch=1, grid=grid, in_specs=[own_spec, parts_spec, blk, blk, blk, ANY], out_specs=[blk] * 4),
        out_shape=[jax.ShapeDtypeStruct(shard, F32)] * 4,
        compiler_params=_params(("arbitrary",), 32),
    )(slot, *_pinned(own, parts, w, m, v, token))


def _adamw_w_in(owns, parts, w, m, v, slot, token):
    n_parts = len(owns)
    shard = w.shape
    tr = ADAMW_ROWS
    per = shard[0] // n_parts // tr

    def local(i, p):
        return jnp.clip(i - p * per, 0, per - 1)

    def body(slot_ref, *refs):
        own_refs, part_refs = refs[:n_parts], refs[n_parts:2 * n_parts]
        w_ref, m_ref, v_ref, _token, g_out, d_out, m_out, v_out = refs[2 * n_parts:]
        i = pl.program_id(0)
        for p in range(n_parts):
            @pl.when(i // per == p)
            def _(p=p):
                g = own_refs[p][...].astype(F32)
                for k in range(N_DEV - 1):
                    g = g + part_refs[p][k].astype(F32)
                _adamw_update(g, w_ref, m_ref, v_ref, g_out, d_out, m_out, v_out)

    blk = pl.BlockSpec((tr, shard[1]), lambda i, s: (i, 0))
    own_specs = [pl.BlockSpec((tr, shard[1]), lambda i, s, p=p: (local(i, p), s[0])) for p in range(n_parts)]
    part_specs = [pl.BlockSpec((N_DEV - 1, tr, shard[1]), lambda i, s, p=p: (0, local(i, p), 0)) for p in range(n_parts)]
    return pl.pallas_call(
        body, name="adamw_w_in",
        grid_spec=pltpu.PrefetchScalarGridSpec(
            num_scalar_prefetch=1, grid=(shard[0] // tr,), in_specs=own_specs + part_specs + [blk, blk, blk, ANY],
            out_specs=[blk] * 4),
        out_shape=[jax.ShapeDtypeStruct(shard, F32)] * 4,
        compiler_params=_params(("arbitrary",), 32),
    )(slot, *_pinned(*owns, *parts, w, m, v, token))


def _adamw_vectors(parts, w, m, v):
    n_vec = w.shape[0]

    def body(p_ref, w_ref, m_ref, v_ref, g_out, d_out, m_out, v_out, loss_out):
        total = p_ref[0]
        for s in range(1, N_DEV):
            total = total + p_ref[s]
        _adamw_update(total[0:n_vec], w_ref, m_ref, v_ref, g_out, d_out, m_out, v_out)
        loss_out[...] = total[n_vec:n_vec + 1] * (0.5 / D)

    return pl.pallas_call(
        body, name="adamw_vectors",
        out_shape=[jax.ShapeDtypeStruct(w.shape, F32)] * 4 + [jax.ShapeDtypeStruct((1, D), F32)],
    )(parts, w, m, v)


def _input_norm(x, g_pre, token):
    seq = x.shape[0]
    tm = 1024

    def body(x_ref, g_ref, _token, u_ref, ut_ref):
        xf = x_ref[...]
        u = (xf * _rms_scale(xf) * g_ref[...]).astype(BF16)
        u_ref[...] = u
        ut_ref[...] = u.T

    return pl.pallas_call(
        body, name="input_norm", grid=(seq // tm,),
        in_specs=[pl.BlockSpec((tm, D), lambda i: (i, 0)), pl.BlockSpec((1, D), lambda i: (0, 0)), ANY],
        out_specs=[pl.BlockSpec((tm, D), lambda i: (i, 0)), pl.BlockSpec((D, tm), lambda i: (0, i))],
        out_shape=[pltpu.HBM((seq, D), BF16), pltpu.HBM((D, seq), BF16)],
        compiler_params=_params(("arbitrary",), 40),
    )(*_pinned(x, g_pre, token))


IN_PROJ_TN = IN_COLS // 4


def _in_proj_fwd(name, u, w_in, blocks, proj, token):
    seq = u.shape[0]
    tm, tn = 1024, IN_PROJ_TN

    def body(_blocks, u_ref, w_ref, _token, _proj_in, proj_ref):
        proj_ref[...] = _dot(u_ref[...], w_ref[...]).astype(BF16)

    return pl.pallas_call(
        body, name=name,
        grid_spec=pltpu.PrefetchScalarGridSpec(
            num_scalar_prefetch=1, grid=(seq // tm, blocks.shape[0]),
            in_specs=[pl.BlockSpec((tm, D), lambda i, j, b: (i, 0)), pl.BlockSpec((D, tn), lambda i, j, b: (0, b[j])),
                      ANY, ANY],
            out_specs=pl.BlockSpec((tm, tn), lambda i, j, b: (i, b[j]))),
        out_shape=pltpu.HBM((seq, IN_COLS), BF16),
        input_output_aliases={4: 0},
        compiler_params=_params(("arbitrary", "arbitrary"), 40),
    )(blocks, *_pinned(u, w_in, token, proj))


CONV_TM = 512
CONV_RS = 128


def _shifts():
    return [(b, [(a, 8 * a + b) for a in range(4) if 8 * a + b < TAPS]) for b in range(8)]


def _taps_looking_back(buf, row0, lanes, weight):
    acc = None
    for b, group in _shifts():
        part = None
        for a, s in group:
            term = weight(TAPS - 1 - s) * buf[pl.ds(row0 - 8 - 8 * a, CONV_RS + 8), lanes]
            part = term if part is None else part + term
        if b:
            part = pltpu.roll(part, b, 0)
        acc = part[8:, :] if acc is None else acc + part[8:, :]
    return acc


def _taps_looking_ahead(buf, row0, lanes, weight):
    acc = None
    for b, group in _shifts():
        part = None
        for a, s in group:
            term = weight(TAPS - 1 - s) * buf[pl.ds(row0 + 8 * a, CONV_RS + 8), lanes]
            part = term if part is None else part + term
        if b:
            part = pltpu.roll(part, CONV_RS + 8 - b, 0)
        acc = part[:CONV_RS, :] if acc is None else acc + part[:CONV_RS, :]
    return acc


def _layer_norm_parts(cv):
    mu = jnp.mean(cv, axis=-1, keepdims=True)
    cen = cv - mu
    rstd = lax.rsqrt(jnp.mean(cen * cen, axis=-1, keepdims=True) + LN_EPS)
    return cen * rstd, rstd


def _conv_fwd(proj, dw, dw_bias, ln_g, ln_b, w_conv_out, token):
    seq = proj.shape[0]
    tm = CONV_TM

    def body(a_ref, gate_ref, dw_ref, bias_ref, lg_ref, lb_ref, w_ref, _token, cv_ref, y_ref, ug):
        i = pl.program_id(0)

        @pl.when(i == 0)
        def _():
            ug[0:HALO, :] = jnp.zeros((HALO, D), F32)

        @pl.when(i > 0)
        def _():
            ug[0:HALO, :] = ug[tm:tm + HALO, :]
        ug[HALO:HALO + tm, :] = a_ref[...].astype(F32) * _sigmoid(gate_ref[...].astype(F32))

        def channel_block(cb, carry):
            lanes = pl.ds(pl.multiple_of(cb * 128, 128), 128)
            for r0 in range(0, tm, CONV_RS):
                taps = _taps_looking_back(ug, HALO + r0, lanes, lambda k: dw_ref[cb, k:k + 1, :])
                cv_ref[pl.ds(r0, CONV_RS), lanes] = taps + bias_ref[:, lanes]
            return carry
        lax.fori_loop(0, D // 128, channel_block, 0)

        n, _ = _layer_norm_parts(cv_ref[...])
        ln = n * lg_ref[...] + lb_ref[...]
        y_ref[...] = _dot((ln * _sigmoid(ln)).astype(BF16), w_ref[...]).astype(BF16)

    vec = pl.BlockSpec((1, D), lambda i: (0, 0))
    tile = pl.BlockSpec((tm, D), lambda i: (i, 0))
    return pl.pallas_call(
        body, name="conv_fwd", grid=(seq // tm,),
        in_specs=[pl.BlockSpec((tm, D), lambda i: (i, 0)), pl.BlockSpec((tm, D), lambda i: (i, 1)),
                  pl.BlockSpec((N_DEV, TAPS, 128), lambda i: (0, 0, 0)), vec, vec, vec,
                  pl.BlockSpec((D, D), lambda i: (0, 0)), ANY],
        out_specs=[tile, tile],
        out_shape=[pltpu.HBM((seq, D), F32), pltpu.HBM((seq, D), BF16)],
        scratch_shapes=[pltpu.VMEM((HALO + tm, D), F32)],
        compiler_params=_params(("arbitrary",), 32),
    )(*_pinned(proj, proj, dw, dw_bias, ln_g, ln_b, w_conv_out, token))


def _window_sums(rows, window, back):
    n = rows.shape[0]
    span = 1
    while span < window:
        rows = rows + pltpu.roll(rows, span if back else n - span, 0)
        span *= 2
    return rows


def _pool_counts(tile_index, tm, window):
    t = tile_index * tm + lax.broadcasted_iota(jnp.int32, (tm, 1), 0)
    return 1.0 / jnp.minimum(t + 1, window).astype(F32)


def _pool_merge_fwd(proj, y_conv, x, pool_w, pool_scale, w_pool_out, w_o, g_post, token):
    seq = proj.shape[0]
    tm = CONV_TM

    def body(p_ref, gc_ref, gp_ref, yc_ref, x_ref, pw_ref, ps_ref, wpo_ref, wo_ref, g_ref, _token,
             z_ref, zl_ref, yp_ref, mg_ref, o_ref, h1_ref, pbuf):
        i = pl.program_id(0)

        @pl.when(i == 0)
        def _():
            pbuf[0:HALO, :] = jnp.zeros((HALO, D), F32)

        @pl.when(i > 0)
        def _():
            pbuf[0:HALO, :] = pbuf[tm:tm + HALO, :]
        pbuf[HALO:HALO + tm, :] = p_ref[...].astype(F32)

        for g, window in enumerate(POOL_WINDOWS):
            lanes = pl.ds(g * PG, PG)
            acc = _window_sums(pbuf[:, lanes], window, back=True)[HALO:, :]
            zg = acc * _pool_counts(i, tm, window) - pbuf[pl.ds(HALO, tm), lanes]
            z_ref[:, lanes] = zg.astype(BF16)
            zl_ref[:, lanes] = _dot(zg.astype(BF16), pw_ref[g])
        zl = zl_ref[...]
        y_pool = _dot((zl * ps_ref[...]).astype(BF16), wpo_ref[...])
        yp_ref[...] = y_pool.astype(BF16)
        merged = (_sigmoid(gc_ref[...].astype(F32)) * yc_ref[...].astype(F32)
                  + _sigmoid(gp_ref[...].astype(F32)) * y_pool).astype(BF16)
        mg_ref[...] = merged
        o = _dot(merged, wo_ref[...])
        o_ref[...] = o
        h1_ref[...] = x_ref[...] + o * _rms_scale(o) * g_ref[...]

    vec = pl.BlockSpec((1, D), lambda i: (0, 0))
    tile = pl.BlockSpec((tm, D), lambda i: (i, 0))
    mat = pl.BlockSpec((D, D), lambda i: (0, 0))
    return pl.pallas_call(
        body, name="pool_merge_fwd", grid=(seq // tm,),
        in_specs=[pl.BlockSpec((tm, D), lambda i: (i, 2)), pl.BlockSpec((tm, D), lambda i: (i, 3)),
                  pl.BlockSpec((tm, D), lambda i: (i, 4)), tile, tile,
                  pl.BlockSpec((4, PG, PG), lambda i: (0, 0, 0)), vec, mat, mat, vec, ANY],
        out_specs=[tile] * 6,
        out_shape=[pltpu.HBM((seq, D), dt) for dt in (BF16, F32, BF16, BF16, F32, F32)],
        scratch_shapes=[pltpu.VMEM((HALO + tm, D), F32)],
        compiler_params=_params(("arbitrary",), 48),
    )(*_pinned(proj, proj, proj, y_conv, x, pool_w, pool_scale, w_pool_out, w_o, g_post, token))


def _mlp_fwd(h1, g_pre, w_ff1, w_ff2, g_post, target):
    seq = h1.shape[0]
    tm, tf = 1024, D_FF // N_DEV
    n_f = D_FF // tf

    def body(h1_ref, gpre_ref, w1_ref, w2_ref, gpost_ref, tgt_ref, v_ref, dm_ref, dh2_ref, sse_ref, ggrad_ref, macc):
        i, j = pl.program_id(0), pl.program_id(1)

        @pl.when(j == 0)
        def _():
            h = h1_ref[...]
            v_ref[...] = (h * _rms_scale(h) * gpre_ref[...]).astype(BF16)
        f = jnp.maximum(_dot(v_ref[...], w1_ref[...]), 0.0)
        part = _dot((f * f).astype(BF16), w2_ref[...])

        @pl.when(j == 0)
        def _():
            macc[...] = part

        @pl.when(j > 0)
        def _():
            macc[...] += part

        @pl.when(j == n_f - 1)
        def _():
            mo = macc[...]
            err = h1_ref[...] + mo * _rms_scale(mo) * gpost_ref[...] - tgt_ref[...]
            dh2 = err * (1.0 / D)
            dh2_ref[...] = dh2
            dm, ggrad = _rms_bwd(dh2, mo, gpost_ref[...])
            dm_ref[...] = dm.astype(BF16)
            _acc_out(ggrad_ref, i == 0, ggrad)
            _acc_out(sse_ref, i == 0, jnp.sum(jnp.sum(err * err, axis=1, keepdims=True), axis=0, keepdims=True))

    vec = pl.BlockSpec((1, D), lambda i, j: (0, 0))
    tile = pl.BlockSpec((tm, D), lambda i, j: (i, 0))
    return pl.pallas_call(
        body, name="mlp_fwd", grid=(seq // tm, n_f),
        in_specs=[tile, vec, pl.BlockSpec((D, tf), lambda i, j: (0, j)), pl.BlockSpec((tf, D), lambda i, j: (j, 0)), vec, tile],
        out_specs=[tile, tile, tile, pl.BlockSpec((1, 1), lambda i, j: (0, 0)), vec],
        out_shape=[pltpu.HBM((seq, D), BF16), pltpu.HBM((seq, D), BF16),
                   pltpu.HBM((seq, D), F32), jax.ShapeDtypeStruct((1, 1), F32),
                   jax.ShapeDtypeStruct((1, D), F32)],
        scratch_shapes=[pltpu.VMEM((tm, D), F32)],
        compiler_params=_params(("arbitrary", "arbitrary"), 56),
    )(*_pinned(h1, g_pre, w_ff1, w_ff2, g_post, target))


def _mlp_bwd(v, dm, w_ff1, w_ff2):
    seq = v.shape[0]
    tm, tf = 1024, D_FF // N_DEV
    n_t = seq // tm

    def body(v_ref, dm_ref, w1_ref, w2_ref, dv_hbm, g1_ref, g2_ref, dv_acc, g1_acc, g2_acc, sems):
        j, i = pl.program_id(0), pl.program_id(1)
        vt, dmt = v_ref[...], dm_ref[...]
        f = jnp.maximum(_dot(vt, w1_ref[...]), 0.0)
        df = (_dot_nt(dmt, w2_ref[...]) * (2.0 * f)).astype(BF16)
        rows = pl.ds(pl.multiple_of(i * tm, tm), tm)
        dv_part = _dot_nt(df, w1_ref[...])

        @pl.when(j == 0)
        def _():
            dv_acc[rows, :] = dv_part

        @pl.when(j > 0)
        def _():
            dv_acc[rows, :] += dv_part
        g1_part = _dot_tn(vt, df)
        g2_part = _dot_tn((f * f).astype(BF16), dmt)

        @pl.when(i == 0)
        def _():
            g1_acc[...] = g1_part
            g2_acc[...] = g2_part

        @pl.when(i > 0)
        def _():
            g1_acc[...] += g1_part
            g2_acc[...] += g2_part

        @pl.when(i == n_t - 1)
        def _():
            g1_ref[...] = g1_acc[...].astype(BF16)
            g2_ref[...] = g2_acc[...].astype(BF16)

        last_pass = j == D_FF // tf - 1

        @pl.when(last_pass)
        def _():
            pltpu.make_async_copy(dv_acc.at[rows, :], dv_hbm.at[rows, :], sems.at[i]).start()

        @pl.when(jnp.logical_and(last_pass, i == n_t - 1))
        def _():
            for t in range(n_t):
                done = pl.ds(t * tm, tm)
                pltpu.make_async_copy(dv_acc.at[done, :], dv_hbm.at[done, :], sems.at[t]).wait()

    tile = pl.BlockSpec((tm, D), lambda j, i: (i, 0))
    return pl.pallas_call(
        body, name="mlp_bwd", grid=(D_FF // tf, n_t),
        in_specs=[tile, tile, pl.BlockSpec((D, tf), lambda j, i: (0, j)), pl.BlockSpec((tf, D), lambda j, i: (j, 0))],
        out_specs=[pl.BlockSpec(memory_space=pl.ANY), pl.BlockSpec((D, tf), lambda j, i: (0, j)),
                   pl.BlockSpec((tf, D), lambda j, i: (j, 0))],
        out_shape=[pltpu.HBM((seq, D), F32), pltpu.HBM((D, D_FF), BF16),
                   pltpu.HBM((D_FF, D), BF16)],
        scratch_shapes=[pltpu.VMEM((seq, D), F32), pltpu.VMEM((D, tf), F32), pltpu.VMEM((tf, D), F32),
                        pltpu.SemaphoreType.DMA((n_t,))],
        compiler_params=_params(("arbitrary", "arbitrary"), 52),
    )(*_pinned(v, dm, w_ff1, w_ff2))


def _merge_bwd(dh2, dv, h1, g_mlp_pre, o, g_mix_post, w_o, merged, proj, y_conv, y_pool, token):
    seq = dh2.shape[0]
    tm = 256
    n_t = seq // tm

    def body(dh2_ref, dv_ref, h1_ref, gpre_ref, o_ref, gpost_ref, wo_ref, mg_ref, gc_ref, gp_ref, yc_ref, yp_ref, _token,
             dh1_ref, dyc_ref, dyp_ref, dg_ref, gwo_ref, ggpre_ref, ggpost_ref, gwo_acc):
        i = pl.program_id(0)
        dnorm, ggpre = _rms_bwd(dv_ref[...], h1_ref[...], gpre_ref[...])
        dh1 = dh2_ref[...] + dnorm
        dh1_ref[...] = dh1
        do, ggpost = _rms_bwd(dh1, o_ref[...], gpost_ref[...])
        do = do.astype(BF16)
        _acc_out(ggpre_ref, i == 0, ggpre)
        _acc_out(ggpost_ref, i == 0, ggpost)
        _acc_out(gwo_acc, i == 0, _dot_tn(mg_ref[...], do))
        dmerged = _dot_nt(do, wo_ref[...])
        sc, sp = _sigmoid(gc_ref[...].astype(F32)), _sigmoid(gp_ref[...].astype(F32))
        dyc_ref[...] = (dmerged * sc).astype(BF16)
        dyp_ref[...] = (dmerged * sp).astype(BF16)
        dg_ref[:, 0:D] = (dmerged * yc_ref[...].astype(F32) * (sc * (1.0 - sc))).astype(BF16)
        dg_ref[:, D:2 * D] = (dmerged * yp_ref[...].astype(F32) * (sp * (1.0 - sp))).astype(BF16)

        @pl.when(i == n_t - 1)
        def _():
            gwo_ref[...] = gwo_acc[...].astype(BF16)

    vec = pl.BlockSpec((1, D), lambda i: (0, 0))
    tile = pl.BlockSpec((tm, D), lambda i: (i, 0))
    mat = pl.BlockSpec((D, D), lambda i: (0, 0))
    return pl.pallas_call(
        body, name="merge_bwd", grid=(n_t,),
        in_specs=[tile, tile, tile, vec, tile, vec, mat, tile,
                  pl.BlockSpec((tm, D), lambda i: (i, 3)), pl.BlockSpec((tm, D), lambda i: (i, 4)), tile, tile, ANY],
        out_specs=[tile, tile, tile, pl.BlockSpec((tm, 2 * D), lambda i: (i, 0)), mat, vec, vec],
        out_shape=[pltpu.HBM((seq, D), F32), pltpu.HBM((seq, D), BF16),
                   pltpu.HBM((seq, D), BF16), pltpu.HBM((seq, 2 * D), BF16),
                   pltpu.HBM((D, D), BF16), jax.ShapeDtypeStruct((1, D), F32),
                   jax.ShapeDtypeStruct((1, D), F32)],
        scratch_shapes=[pltpu.VMEM((D, D), F32)],
        compiler_params=_params(("arbitrary",), 48),
    )(*_pinned(dh2, dv, h1, g_mlp_pre, o, g_mix_post, w_o, merged, proj, proj, y_conv, y_pool, token))


def _pool_bwd(dy_pool, zl, z, pool_w, pool_scale, w_pool_out, token):
    seq = dy_pool.shape[0]
    tm = CONV_TM
    n_t = seq // tm

    def body(dy_ref, zl_ref, z_ref, pw_ref, ps_ref, wpo_ref, _token,
             dp_ref, gwpo_ref, gpw_ref, gps_ref, qbuf, gwpo_acc, gpw_acc):
        i = pl.program_id(0)
        tile_index = n_t - 1 - i
        first = i == 0
        dy = dy_ref[...]
        zl = zl_ref[...]
        dzs = _dot_nt(dy, wpo_ref[...])
        _acc_out(gwpo_acc, first, _dot_tn((zl * ps_ref[...]).astype(BF16), dy))
        _acc_out(gps_ref, first, jnp.sum(dzs * zl, axis=0, keepdims=True))
        dzl = (dzs * ps_ref[...]).astype(BF16)

        @pl.when(first)
        def _():
            qbuf[tm:tm + HALO, :] = jnp.zeros((HALO, D), F32)

        @pl.when(jnp.logical_not(first))
        def _():
            qbuf[tm:tm + HALO, :] = qbuf[0:HALO, :]

        dzs_list = []
        for g, window in enumerate(POOL_WINDOWS):
            lanes = pl.ds(g * PG, PG)
            dzl_g = dzl[:, g * PG:(g + 1) * PG]
            dz = _dot_nt(dzl_g, pw_ref[g])
            _acc_out(gpw_acc.at[g], first, _dot_tn(z_ref[:, lanes], dzl_g))
            qbuf[pl.ds(0, tm), lanes] = dz * _pool_counts(tile_index, tm, window)
            dzs_list.append(dz)
        for g, window in enumerate(POOL_WINDOWS):
            lanes = pl.ds(g * PG, PG)
            acc = _window_sums(qbuf[:, lanes], window, back=False)[:tm, :]
            dp_ref[:, lanes] = (acc - dzs_list[g]).astype(BF16)

        @pl.when(i == n_t - 1)
        def _():
            gwpo_ref[...] = gwpo_acc[...].astype(BF16)
            gpw_ref[...] = gpw_acc[...].astype(BF16)

    vec = pl.BlockSpec((1, D), lambda i: (0, 0))
    tile = pl.BlockSpec((tm, D), lambda i: (n_t - 1 - i, 0))
    mat = pl.BlockSpec((D, D), lambda i: (0, 0))
    pw = pl.BlockSpec((4, PG, PG), lambda i: (0, 0, 0))
    return pl.pallas_call(
        body, name="pool_bwd", grid=(n_t,),
        in_specs=[tile, tile, tile, pw, vec, mat, ANY],
        out_specs=[tile, mat, pw, vec],
        out_shape=[pltpu.HBM((seq, D), BF16), pltpu.HBM((D, D), BF16),
                   pltpu.HBM((4, PG, PG), BF16), jax.ShapeDtypeStruct((1, D), F32)],
        scratch_shapes=[pltpu.VMEM((tm + HALO, D), F32), pltpu.VMEM((D, D), F32), pltpu.VMEM((4, PG, PG), F32)],
        compiler_params=_params(("arbitrary",), 40),
    )(*_pinned(dy_pool, zl, z, pool_w, pool_scale, w_pool_out, token))


def _conv_bwd(dy_conv, cv, proj, dw, ln_g, ln_b, w_conv_out, token):
    seq = dy_conv.shape[0]
    tm = CONV_TM // 2
    n_t = seq // tm
    halo_blocks = tm // HALO

    def body(dy_ref, cv_ref, a_ref, gate_ref, ah_ref, gh_ref, dw_ref, lg_ref, lb_ref, w_ref, _token,
             dglu_ref, gw_ref, gdw_ref, gbias_ref, glg_ref, glb_ref, ug, dcv, dug, gw_acc):
        i = pl.program_id(0)
        tile_index = n_t - 1 - i
        first = i == 0
        dy = dy_ref[...]
        n, rstd = _layer_norm_parts(cv_ref[...])
        ln = n * lg_ref[...] + lb_ref[...]
        sg = _sigmoid(ln)
        _acc_out(gw_acc, first, _dot_tn((ln * sg).astype(BF16), dy))
        dln = _dot_nt(dy, w_ref[...]) * (sg * (1.0 + ln * (1.0 - sg)))
        _acc_out(glg_ref, first, jnp.sum(dln * n, axis=0, keepdims=True))
        _acc_out(glb_ref, first, jnp.sum(dln, axis=0, keepdims=True))
        dn = dln * lg_ref[...]
        dcv_tile = rstd * (dn - jnp.mean(dn, axis=-1, keepdims=True) - n * jnp.mean(dn * n, axis=-1, keepdims=True))
        _acc_out(gbias_ref, first, jnp.sum(dcv_tile, axis=0, keepdims=True))

        @pl.when(first)
        def _():
            dcv[tm:tm + HALO, :] = jnp.zeros((HALO, D), F32)

        @pl.when(jnp.logical_not(first))
        def _():
            dcv[tm:tm + HALO, :] = dcv[0:HALO, :]
        dcv[0:tm, :] = dcv_tile

        a, gate = a_ref[...].astype(F32), gate_ref[...].astype(F32)
        sgate = _sigmoid(gate)
        ug[HALO:HALO + tm, :] = a * sgate
        before = jnp.where(tile_index > 0, 1.0, 0.0)
        ug[0:HALO, :] = ah_ref[...].astype(F32) * _sigmoid(gh_ref[...].astype(F32)) * before

        @pl.when(first)
        def _():
            gdw_ref[...] = jnp.zeros((N_DEV, TAPS + 1, 128), F32)

        def channel_block(cb, carry):
            lanes = pl.ds(pl.multiple_of(cb * 128, 128), 128)
            for r0 in range(0, tm, CONV_RS):
                dug[pl.ds(r0, CONV_RS), lanes] = _taps_looking_ahead(dcv, r0, lanes, lambda k: dw_ref[cb, k:k + 1, :])
            for b, group in _shifts():
                sums = [jnp.zeros((8, 128), F32) for _ in group]
                for r0 in range(0, tm, CONV_RS):
                    window = ug[pl.ds(r0, CONV_RS + HALO), lanes]
                    if b:
                        window = pltpu.roll(window, b, 0)
                    d = dcv[pl.ds(r0, CONV_RS), lanes]
                    for n_a, (a, s) in enumerate(group):
                        prod = d * window[HALO - 8 * a:HALO - 8 * a + CONV_RS, :]
                        sums[n_a] = sums[n_a] + jnp.sum(prod.reshape(CONV_RS // 8, 8, 128), axis=0)
                for n_a, (a, s) in enumerate(group):
                    k = TAPS - 1 - s
                    gdw_ref[cb, k:k + 1, :] += jnp.sum(sums[n_a], axis=0, keepdims=True)
            return carry
        lax.fori_loop(0, D // 128, channel_block, 0)

        d_ug = dug[...]
        dglu_ref[:, 0:D] = (d_ug * sgate).astype(BF16)
        dglu_ref[:, D:2 * D] = (d_ug * a * (sgate * (1.0 - sgate))).astype(BF16)

        @pl.when(i == n_t - 1)
        def _():
            gw_ref[...] = gw_acc[...].astype(BF16)

    def halo_index(col):
        return lambda i: (jnp.maximum((n_t - 1 - i) * halo_blocks - 1, 0), col)

    vec = pl.BlockSpec((1, D), lambda i: (0, 0))
    tile = pl.BlockSpec((tm, D), lambda i: (n_t - 1 - i, 0))
    mat = pl.BlockSpec((D, D), lambda i: (0, 0))
    dwspec = pl.BlockSpec((N_DEV, TAPS, 128), lambda i: (0, 0, 0))
    return pl.pallas_call(
        body, name="conv_bwd", grid=(n_t,),
        in_specs=[tile, tile, pl.BlockSpec((tm, D), lambda i: (n_t - 1 - i, 0)), pl.BlockSpec((tm, D), lambda i: (n_t - 1 - i, 1)),
                  pl.BlockSpec((HALO, D), halo_index(0)), pl.BlockSpec((HALO, D), halo_index(1)), dwspec, vec, vec, mat,
                  ANY],
        out_specs=[pl.BlockSpec((tm, 2 * D), lambda i: (n_t - 1 - i, 0)), mat,
                   pl.BlockSpec((N_DEV, TAPS + 1, 128), lambda i: (0, 0, 0)), vec, vec, vec],
        out_shape=[pltpu.HBM((seq, 2 * D), BF16), pltpu.HBM((D, D), BF16),
                   pltpu.HBM((N_DEV, TAPS + 1, 128), F32), jax.ShapeDtypeStruct((1, D), F32),
                   jax.ShapeDtypeStruct((1, D), F32), jax.ShapeDtypeStruct((1, D), F32)],
        scratch_shapes=[pltpu.VMEM((HALO + tm, D), F32), pltpu.VMEM((tm + HALO, D), F32), pltpu.VMEM((tm, D), F32),
                        pltpu.VMEM((D, D), F32)],
        compiler_params=_params(("arbitrary",), 48),
    )(*_pinned(dy_conv, cv, proj, proj, proj, proj, dw, ln_g, ln_b, w_conv_out, token))


def _in_proj_bwd_x(d_glu, dp, dgates, w_in, x, g_pre, dh1, token):
    seq = x.shape[0]
    tm = 512

    def body(dglu_ref, dp_ref, dg_ref, w_ref, x_ref, g_ref, dh1_ref, _token, dx_ref, gg_ref):
        du = _dot_nt(dglu_ref[...], w_ref[:, 0:2 * D])
        du += _dot_nt(dp_ref[...], w_ref[:, 2 * D:3 * D])
        du += _dot_nt(dg_ref[...], w_ref[:, 3 * D:5 * D])
        dnorm, gg = _rms_bwd(du, x_ref[...], g_ref[...])
        dx_ref[...] = dh1_ref[...] + dnorm
        _acc_out(gg_ref, pl.program_id(0) == 0, gg)

    vec = pl.BlockSpec((1, D), lambda i: (0, 0))
    tile = pl.BlockSpec((tm, D), lambda i: (i, 0))
    wide = pl.BlockSpec((tm, 2 * D), lambda i: (i, 0))
    return pl.pallas_call(
        body, name="in_proj_bwd_x", grid=(seq // tm,),
        in_specs=[wide, tile, wide, pl.BlockSpec((D, IN_COLS), lambda i: (0, 0)), tile, vec, tile, ANY],
        out_specs=[tile, vec],
        out_shape=[pltpu.HBM((seq, D), F32), jax.ShapeDtypeStruct((1, D), F32)],
        compiler_params=_params(("arbitrary",), 48),
    )(*_pinned(d_glu, dp, dgates, w_in, x, g_pre, dh1, token))


W_IN_PARTS = 2


def _in_proj_bwd_w(u_t, d_glu, dp, dgates, part, token):
    seq = u_t.shape[1]
    rows = D // W_IN_PARTS
    tm = 2048
    n_t = seq // tm

    def body(u_ref, dglu_ref, dp_ref, dg_ref, _token, out_ref, acc):
        b, i = pl.program_id(0), pl.program_id(1)
        ut = u_ref[...]

        def add(d_ref):
            _acc_out(acc, i == 0, _dot(ut, d_ref[...]))

        pl.when(b < 2)(lambda: add(dglu_ref))
        pl.when(b == 2)(lambda: add(dp_ref))
        pl.when(b > 2)(lambda: add(dg_ref))

        @pl.when(i == n_t - 1)
        def _():
            out_ref[...] = acc[...].astype(BF16)

    return pl.pallas_call(
        body, name="in_proj_bwd_w_%d" % part, grid=(IN_COLS // D, n_t),
        in_specs=[pl.BlockSpec((rows, tm), lambda b, i: (part, i)),
                  pl.BlockSpec((tm, D), lambda b, i: (jnp.where(b < 2, i, 0), jnp.minimum(b, 1))),
                  pl.BlockSpec((tm, D), lambda b, i: (jnp.where(b == 2, i, 0), 0)),
                  pl.BlockSpec((tm, D), lambda b, i: (jnp.where(b > 2, i, 0), jnp.maximum(b - 3, 0))), ANY],
        out_specs=pl.BlockSpec((rows, D), lambda b, i: (0, b)),
        out_shape=pltpu.HBM((rows, IN_COLS), BF16),
        scratch_shapes=[pltpu.VMEM((rows, D), F32)],
        compiler_params=_params(("arbitrary", "arbitrary"), 52),
    )(*_pinned(u_t, d_glu, dp, dgates, token))


VEC_NAMES = ("mix_pre_g", "dw_bias", "conv_ln_g", "conv_ln_b", "pool_scale", "mix_post_g", "mlp_pre_g", "mlp_post_g")
WEIGHT_ORDER = ("mix_pre_g", "w_in", "dw_kernel", "dw_bias", "conv_ln_g", "conv_ln_b", "w_conv_out", "pool_w",
                "pool_scale", "w_pool_out", "w_o", "mix_post_g", "mlp_pre_g", "w_ff1", "w_ff2", "mlp_post_g")


def _step(x, loss_target, w, m, v):
    row = lambda a: a.reshape(1, D)
    names = [s[0] for s in SHARDED]

    gather_groups = (("w_in",), ("w_conv_out", "dw_kernel"), ("pool_w", "w_pool_out", "w_o"), ("w_ff1", "w_ff2"))
    legs = [GATHER_ICI, GATHER_D2D]
    first_level, token = _exchange_start(
        "gather_start_w_in", [(gather_groups[0], None, _stage_shards(gather_groups[0], [w["w_in"]], x))], legs)
    later = [n for g in gather_groups[1:] for n in g]
    seeded = dict(zip(later, _stage_shards(later, [w[n] for n in later], token)))
    rest, token = _exchange_start("gather_start", [(g, None, [seeded[n] for n in g]) for g in gather_groups[1:]], legs)
    first_level = first_level + rest
    forwarded, full = {}, {}

    def forward(k, after):
        group, (ici_sems, d2d_sems), _, landed = first_level[k]
        _, landed = _exchange_wait("gather_ici_" + group[0], group, None, landed, [(ici_sems, GATHER_ICI)], after)
        (second,), tok = _exchange_start("gather_forward_" + group[0], [(group, None, landed)], [GATHER_FORWARD])
        forwarded[k] = (second[3], [(d2d_sems, GATHER_D2D), (second[1][0], GATHER_FORWARD)])
        return tok

    def gathered(k, after):
        group = gather_groups[k]
        landed, waits = forwarded[k]
        _, arrays = _exchange_wait("gather_wait_" + group[0], group, None, landed, waits, after)
        full.update(zip(group, arrays))

    u, u_t = _input_norm(x, row(w["mix_pre_g"]), token)
    group, (ici_sems, d2d_sems), _, landed = first_level[0]
    _, landed = _exchange_wait("gather_d2d_w_in", group, None, landed, [(d2d_sems, GATHER_D2D)], u)
    home = jnp.reshape(_my_slot() // 2, (1,)).astype(jnp.int32)
    proj = _in_proj_fwd("in_proj_fwd_home", u, landed[0], home, lax.empty((x.shape[0], IN_COLS), BF16), u)
    _, landed = _exchange_wait("gather_ici_w_in", group, None, landed, [(ici_sems, GATHER_ICI)], proj)
    (second,), tok = _exchange_start("gather_forward_w_in", [(group, None, landed)], [GATHER_FORWARD])
    (full["w_in"],) = _exchange_wait("gather_wait_w_in", group, None, second[3], [(second[1][0], GATHER_FORWARD)], tok)[1]
    tok = forward(1, full["w_in"])
    away = jnp.stack([home[0] ^ 1, home[0] ^ 2, home[0] ^ 3]).astype(jnp.int32)
    proj = _in_proj_fwd("in_proj_fwd", u, full["w_in"], away, proj, tok)
    gathered(1, proj)
    tok = forward(2, proj)
    cv, y_conv = _conv_fwd(proj, full["dw_kernel"], row(w["dw_bias"]), row(w["conv_ln_g"]), row(w["conv_ln_b"]),
                           full["w_conv_out"], tok)
    gathered(2, y_conv)
    tok = forward(3, y_conv)
    z, zl, y_pool, merged, o, h1 = _pool_merge_fwd(proj, y_conv, x, full["pool_w"], row(w["pool_scale"]),
                                                   full["w_pool_out"], full["w_o"], row(w["mix_post_g"]), tok)
    gathered(3, h1)
    vv, dm, dh2, sse, g_mlp_post = _mlp_fwd(h1, row(w["mlp_pre_g"]), full["w_ff1"], full["w_ff2"],
                                            row(w["mlp_post_g"]), loss_target)

    shard_of = {name: shard for name, _, shard, _ in SHARDED}

    def scatter_start(tag, group, grads, piece=None):
        landings = [lax.empty((N_DEV - 1,) + (piece or shard_of[n]), g.dtype) for n, g in zip(group, grads)]
        (handle,), tok = _exchange_start("scatter_start_" + tag, [(group, grads, landings)], [SCATTER])
        return handle, tok

    dv, g_ff1, g_ff2 = _mlp_bwd(vv, dm, full["w_ff1"], full["w_ff2"])
    h_ff, tok_ff = scatter_start("ff", ("w_ff1", "w_ff2"), [g_ff1, g_ff2])
    dh1, dy_conv, dy_pool, dgates, g_wo, g_mlp_pre, g_mix_post = _merge_bwd(
        dh2, dv, h1, row(w["mlp_pre_g"]), o, row(w["mix_post_g"]), full["w_o"], merged, proj, y_conv, y_pool, tok_ff)
    dp, g_wpo, g_pw, g_pool_scale = _pool_bwd(dy_pool, zl, z, full["pool_w"], row(w["pool_scale"]), full["w_pool_out"],
                                              dh1)
    h_pool, tok_pool = scatter_start("pool", ("w_o", "w_pool_out", "pool_w"), [g_wo, g_wpo, g_pw])
    d_glu, g_wco, g_dw, g_bias, g_ln_g, g_ln_b = _conv_bwd(dy_conv, cv, proj, full["dw_kernel"], row(w["conv_ln_g"]),
                                                            row(w["conv_ln_b"]), full["w_conv_out"], tok_pool)
    h_conv, tok_conv = scatter_start("conv", ("w_conv_out", "dw_kernel"), [g_wco, g_dw[:, :TAPS, :]])
    tok, h_in = tok_conv, []
    for part in range(W_IN_PARTS):
        g_part = _in_proj_bwd_w(u_t, d_glu, dp, dgates, part, tok)
        handle, tok = scatter_start("in_%d" % part, ("w_in",), [g_part], (D // W_IN_PARTS, IN_COLS // N_DEV))
        h_in.append(handle)
    grad_x, g_mix_pre = _in_proj_bwd_x(d_glu, dp, dgates, full["w_in"], x, row(w["mix_pre_g"]), dh1, tok)

    out = {}
    slot = jnp.reshape(_my_slot(), (1,)).astype(jnp.int32)
    after = grad_x
    for tag, (group, (sems,), grads, landings) in (("ff", h_ff), ("pool", h_pool), ("conv", h_conv)):
        mine, landed = _exchange_wait("scatter_wait_" + tag, group, grads, landings, [(sems, SCATTER)], after)
        for name, own, parts in zip(group, mine, landed):
            out[name] = _adamw(name, own, parts, w[name], m[name], v[name], slot, after)
            after = out[name][0]
    vec_parts = after = _vec_exchange(
        [g_mix_pre, g_bias, g_ln_g, g_ln_b, g_pool_scale, g_mix_post, g_mlp_pre, g_mlp_post,
         jnp.broadcast_to(sse, (1, D))], after)
    owns, partials = [], []
    for part, (group, (sems,), grads, landings) in enumerate(h_in):
        mine, landed = _exchange_wait("scatter_wait_in_%d" % part, group, grads, landings, [(sems, SCATTER)], after)
        owns.append(mine[0])
        partials.append(landed[0])
    out["w_in"] = _adamw_w_in(owns, partials, w["w_in"], m["w_in"], v["w_in"], slot, after)
    stack = lambda d: jnp.stack([d[n] for n in VEC_NAMES], axis=0)
    *res, loss_row = _adamw_vectors(vec_parts, stack(w), stack(m), stack(v))
    for k, name in enumerate(VEC_NAMES):
        out[name] = [r[k] for r in res]
    return loss_row[0, 0], grad_x, out


def kernel(x, mix_pre_g, w_in, dw_kernel, dw_bias, conv_ln_g, conv_ln_b, w_conv_out, pool_w, pool_scale, w_pool_out, w_o, mix_post_g, mlp_pre_g, w_ff1, w_ff2, mlp_post_g, loss_target, m_mix_pre_g, m_w_in, m_dw_kernel, m_dw_bias, m_conv_ln_g, m_conv_ln_b, m_w_conv_out, m_pool_w, m_pool_scale, m_w_pool_out, m_w_o, m_mix_post_g, m_mlp_pre_g, m_w_ff1, m_w_ff2, m_mlp_post_g, v_mix_pre_g, v_w_in, v_dw_kernel, v_dw_bias, v_conv_ln_g, v_conv_ln_b, v_w_conv_out, v_pool_w, v_pool_scale, v_w_pool_out, v_w_o, v_mix_post_g, v_mlp_pre_g, v_w_ff1, v_w_ff2, v_mlp_post_g):
    w = dict(mix_pre_g=mix_pre_g, w_in=w_in, dw_kernel=dw_kernel, dw_bias=dw_bias, conv_ln_g=conv_ln_g, conv_ln_b=conv_ln_b,
             w_conv_out=w_conv_out, pool_w=pool_w, pool_scale=pool_scale, w_pool_out=w_pool_out, w_o=w_o,
             mix_post_g=mix_post_g, mlp_pre_g=mlp_pre_g, w_ff1=w_ff1, w_ff2=w_ff2, mlp_post_g=mlp_post_g)
    m = dict(mix_pre_g=m_mix_pre_g, w_in=m_w_in, dw_kernel=m_dw_kernel, dw_bias=m_dw_bias, conv_ln_g=m_conv_ln_g,
             conv_ln_b=m_conv_ln_b, w_conv_out=m_w_conv_out, pool_w=m_pool_w, pool_scale=m_pool_scale,
             w_pool_out=m_w_pool_out, w_o=m_w_o, mix_post_g=m_mix_post_g, mlp_pre_g=m_mlp_pre_g, w_ff1=m_w_ff1,
             w_ff2=m_w_ff2, mlp_post_g=m_mlp_post_g)
    v = dict(mix_pre_g=v_mix_pre_g, w_in=v_w_in, dw_kernel=v_dw_kernel, dw_bias=v_dw_bias, conv_ln_g=v_conv_ln_g,
             conv_ln_b=v_conv_ln_b, w_conv_out=v_w_conv_out, pool_w=v_pool_w, pool_scale=v_pool_scale,
             w_pool_out=v_w_pool_out, w_o=v_w_o, mix_post_g=v_mix_post_g, mlp_pre_g=v_mlp_pre_g, w_ff1=v_w_ff1,
             w_ff2=v_w_ff2, mlp_post_g=v_mlp_post_g)
    seq = x.shape[1]
    loss, grad_x, out = _step(x.reshape(seq, D), loss_target.reshape(seq, D), w, m, v)
    grads, deltas, new_m, new_v = ([out[n][k] for n in WEIGHT_ORDER] for k in range(4))
    return (loss, grad_x.reshape(x.shape), *grads, *deltas, *new_m, *new_v)
```

```python
import collections

import jax
import jax.numpy as jnp
from jax import lax
from jax.experimental import pallas as pl
from jax.experimental.pallas import tpu as pltpu

D = 1024
D_FF = 4 * D
IN_COLS = 5 * D
TAPS = 31
HALO = 32
POOL_WINDOWS = (2, 4, 8, 16)
PG = D // 4
N_DEV = 8
RMS_EPS = 1e-6
LN_EPS = 1e-5
ADAM_LR, ADAM_B1, ADAM_B2, ADAM_EPS, ADAM_WD, ADAM_STEP = 0.001, 0.9, 0.999, 1e-08, 0.01, 10

BF16 = jnp.bfloat16
F32 = jnp.float32
MIB = 1 << 20
MESH = pl.DeviceIdType.MESH


def _params(sem, vmem_mib):
    return pltpu.CompilerParams(dimension_semantics=sem, vmem_limit_bytes=vmem_mib * MIB)


def _dot(a, b):
    return jnp.dot(a, b, preferred_element_type=F32)


def _dot_nt(a, b):
    return lax.dot_general(a, b, (((1,), (1,)), ((), ())), preferred_element_type=F32)


def _dot_tn(a, b):
    return lax.dot_general(a, b, (((0,), (0,)), ((), ())), preferred_element_type=F32)


def _rms_scale(x):
    return lax.rsqrt(jnp.mean(x * x, axis=-1, keepdims=True) + RMS_EPS)


def _rms_bwd(dy, x, g):
    xn = x * _rms_scale(x)
    dn = dy * g
    dx = _rms_scale(x) * (dn - xn * jnp.mean(dn * xn, axis=-1, keepdims=True))
    return dx, jnp.sum(dy * xn, axis=0, keepdims=True)


def _sigmoid(x):
    return jax.nn.sigmoid(x)


def _acc_out(ref, first, value):
    @pl.when(first)
    def _():
        ref[...] = value

    @pl.when(jnp.logical_not(first))
    def _():
        ref[...] += value


def _my_slot():
    return 4 * lax.axis_index("x") + 2 * lax.axis_index("y") + lax.axis_index("c")


def _peer(mask):
    x, y, c = lax.axis_index("x"), lax.axis_index("y"), lax.axis_index("c")
    return (x ^ ((mask >> 2) & 1), y ^ ((mask >> 1) & 1), c ^ (mask & 1))


def _cols(width):
    return lambda ref, slot: ref.at[:, pl.ds(pl.multiple_of(slot * width, 128), width)]


def _rows(height):
    return lambda ref, slot: ref.at[pl.ds(pl.multiple_of(slot * height, 8), height), :]


def _lead(ref, slot):
    return ref.at[slot]


def _pool_rows(ref, slot):
    return ref.at[:, pl.ds(pl.multiple_of(slot * (PG // N_DEV), 8), PG // N_DEV), :]


SHARDED = (
    ("w_in", (D, IN_COLS), (D, IN_COLS // N_DEV), _cols(IN_COLS // N_DEV)),
    ("w_ff1", (D, D_FF), (D, D_FF // N_DEV), _cols(D_FF // N_DEV)),
    ("w_ff2", (D_FF, D), (D_FF // N_DEV, D), _rows(D_FF // N_DEV)),
    ("w_conv_out", (D, D), (D // N_DEV, D), _rows(D // N_DEV)),
    ("w_pool_out", (D, D), (D // N_DEV, D), _rows(D // N_DEV)),
    ("w_o", (D, D), (D // N_DEV, D), _rows(D // N_DEV)),
    ("pool_w", (4, PG, PG), (4, PG // N_DEV, PG), _pool_rows),
    ("dw_kernel", (N_DEV, TAPS, D // N_DEV), (TAPS, D // N_DEV), _lead),
)
N_SHARDED = len(SHARDED)


SHARD_AT = {name: at for name, _, _, at in SHARDED}
HBM = pl.BlockSpec(memory_space=pltpu.HBM)
SEM = pl.BlockSpec(memory_space=pltpu.SEMAPHORE)
ANY = pl.BlockSpec(memory_space=pl.ANY)
EFFECT = pltpu.SideEffectType.DATAFLOW_SIDE_EFFECTING


def _in_hbm(a):
    return pltpu.with_memory_space_constraint(a, pltpu.HBM)


def _pinned(*arrays):
    return [_in_hbm(a) for a in arrays]


def _stage_shards(names, shards, token):
    n = len(names)
    specs = [s for name in names for s in SHARDED if s[0] == name]
    dtypes = [F32 if name == "dw_kernel" else BF16 for name in names]

    def body(*refs):
        ins = refs[:n]
        fulls = refs[n + 1:2 * n + 1]
        raw = refs[2 * n + 1:3 * n + 1]
        stage = refs[3 * n + 1:4 * n + 1]
        in_sems, out_sems = refs[4 * n + 1:]
        me = _my_slot()
        loads = [pltpu.make_async_copy(ins[a], raw[a], in_sems.at[a]) for a in range(n)]
        for cp in loads:
            cp.start()
        stores = []
        for a, (_, _, _, at) in enumerate(specs):
            loads[a].wait()
            stage[a][...] = raw[a][...].astype(dtypes[a])
            cp = pltpu.make_async_copy(stage[a], at(fulls[a], me), out_sems.at[a])
            cp.start()
            stores.append(cp)
        for cp in stores:
            cp.wait()

    return pl.pallas_call(
        body, name="stage_" + names[0],
        out_shape=[pltpu.HBM(full, dt) for (_, full, _, _), dt in zip(specs, dtypes)],
        in_specs=[ANY] * (n + 1),
        out_specs=[ANY] * n,
        scratch_shapes=[pltpu.VMEM(shard, F32) for _, _, shard, _ in specs]
        + [pltpu.VMEM(shard, dt) for (_, _, shard, _), dt in zip(specs, dtypes)]
        + [pltpu.SemaphoreType.DMA((n,)), pltpu.SemaphoreType.DMA((n,))],
        compiler_params=pltpu.CompilerParams(vmem_limit_bytes=40 * MIB),
    )(*_pinned(*shards, token))


Leg = collections.namedtuple("Leg", "routes src_of dst_of src_is_land")


def _sem_index(k, m):
    return k * (N_DEV - 1) + m - 1


def _exchange_start(name, groups, legs):
    sizes = [len(g[0]) for g in groups]
    names = [nm for g in groups for nm in g[0]]
    srcs = [s for g in groups if g[1] is not None for s in g[1]]
    lands = [l for g in groups for l in g[2]]
    n_src, n, n_g, n_l = len(srcs), len(names), len(groups), len(legs)

    def body(*refs):
        src_refs, land_refs = list(refs[:n_src]), refs[n_src:n_src + n]
        sems = refs[n_src + n:n_src + n + 2 * n_g * n_l]
        token = refs[-1]
        me = _my_slot()
        first = 0
        for g, size in enumerate(sizes):
            own_src = [src_refs.pop(0) for _ in range(size)] if groups[g][1] is not None else None
            for li, leg in enumerate(legs):
                send, recv = sems[2 * (g * n_l + li)], sems[2 * (g * n_l + li) + 1]
                for m, via in leg.routes:
                    for k in range(size):
                        land = land_refs[first + k]
                        src = land if leg.src_is_land else own_src[k]
                        pltpu.make_async_remote_copy(
                            src_ref=leg.src_of(names[first + k], src, me, m), dst_ref=leg.dst_of(names[first + k], land, me, m),
                            send_sem=send.at[_sem_index(k, m)], recv_sem=recv.at[_sem_index(k, m)],
                            device_id=_peer(via), device_id_type=MESH).start()
            first += size
        token[...] = jnp.zeros_like(token)

    sem_shapes = [pltpu.SemaphoreType.DMA((size * (N_DEV - 1),)) for size in sizes for _ in range(2 * n_l)]
    n_sem = len(sem_shapes)
    outs = pl.pallas_call(
        body, name=name,
        out_shape=sem_shapes + [pltpu.HBM(a.shape, a.dtype) for a in srcs + lands] + [jax.ShapeDtypeStruct((8, 128), F32)],
        in_specs=[HBM] * (n_src + n),
        out_specs=[SEM] * n_sem + [HBM] * (n_src + n) + [pl.BlockSpec(memory_space=pltpu.VMEM)],
        input_output_aliases={k: n_sem + k for k in range(n_src + n)},
        compiler_params=pltpu.CompilerParams(has_side_effects=EFFECT),
    )(*[_in_hbm(a) for a in srcs + lands])
    sems, thru, token = outs[:n_sem], list(outs[n_sem:-1]), outs[-1]
    src_thru, land_thru = thru[:n_src], thru[n_src:]
    handles, first = [], 0
    for g, size in enumerate(sizes):
        pairs = [(sems[2 * (g * n_l + li)], sems[2 * (g * n_l + li) + 1]) for li in range(n_l)]
        mine = [src_thru.pop(0) for _ in range(size)] if groups[g][1] is not None else None
        handles.append((groups[g][0], pairs, mine, land_thru[first:first + size]))
        first += size
    return handles, token


def _exchange_wait(name, names, srcs, lands, waits, after):
    n = len(names)
    n_src = n if srcs is not None else 0

    def body(*refs):
        src_refs, land_refs = refs[:n_src], refs[n_src:n_src + n]
        sems = refs[n_src + n:n_src + n + 2 * len(waits)]
        me = _my_slot()
        for wi, (_, leg) in enumerate(waits):
            for m, via in leg.routes:
                for k in range(n):
                    src = land_refs[k] if leg.src_is_land else src_refs[k]
                    cp = pltpu.make_async_remote_copy(
                        src_ref=leg.src_of(names[k], src, me, m), dst_ref=leg.dst_of(names[k], land_refs[k], me ^ via, m),
                        send_sem=sems[2 * wi].at[_sem_index(k, m)], recv_sem=sems[2 * wi + 1].at[_sem_index(k, m)],
                        device_id=_peer(via), device_id_type=MESH)
                    cp.wait_send()
                    cp.wait_recv()

    arrays = (list(srcs) if srcs is not None else []) + list(lands)
    outs = pl.pallas_call(
        body, name=name,
        out_shape=[pltpu.HBM(a.shape, a.dtype) for a in arrays],
        in_specs=[HBM] * len(arrays) + [SEM] * (2 * len(waits)) + [pl.BlockSpec(memory_space=pl.ANY)],
        out_specs=[HBM] * len(arrays),
        input_output_aliases={k: k for k in range(len(arrays))},
        compiler_params=pltpu.CompilerParams(has_side_effects=EFFECT),
    )(*arrays, *[s for pair, _ in waits for s in pair], after)
    return (outs[:n_src] if srcs is not None else None), outs[n_src:]


def _shard_slot(name, ref, slot):
    return SHARD_AT[name](ref, slot)


GATHER_ICI = Leg(((2, 2), (4, 4), (6, 6)), lambda name, ref, me, m: _shard_slot(name, ref, me),
                 lambda name, ref, sender, m: _shard_slot(name, ref, sender), True)
GATHER_D2D = Leg(((1, 1),), GATHER_ICI.src_of, GATHER_ICI.dst_of, True)
GATHER_FORWARD = Leg(((2, 1), (4, 1), (6, 1)), lambda name, ref, me, m: _shard_slot(name, ref, me ^ m),
                     lambda name, ref, sender, m: _shard_slot(name, ref, sender ^ m), True)
SCATTER = Leg(tuple((m, m) for m in range(1, N_DEV)), lambda name, ref, me, m: _shard_slot(name, ref, me ^ m),
              lambda name, ref, sender, m: ref.at[m - 1], False)


def _vec_exchange(vectors, after):
    n = len(vectors)

    def body(*refs):
        vec, vec_out = refs[n + 2], refs[n + 1]
        send_sems, recv_sems, local_sem = refs[n + 3:]
        for k in range(n):
            vec[k:k + 1, :] = refs[k][...]
        me = _my_slot()
        local = pltpu.make_async_copy(vec, vec_out.at[me], local_sem)
        local.start()
        sends = []
        for mask in range(1, N_DEV):
            cp = pltpu.make_async_remote_copy(
                src_ref=vec, dst_ref=vec_out.at[me], send_sem=send_sems.at[mask - 1],
                recv_sem=recv_sems.at[mask - 1], device_id=_peer(mask), device_id_type=MESH)
            cp.start()
            sends.append(cp)
        for mask in range(1, N_DEV):
            pltpu.make_async_remote_copy(
                src_ref=vec, dst_ref=vec_out.at[me ^ mask], send_sem=send_sems.at[mask - 1],
                recv_sem=recv_sems.at[mask - 1], device_id=_peer(mask), device_id_type=MESH).wait_recv()
        for cp in sends:
            cp.wait_send()
        local.wait()

    return pl.pallas_call(
        body, name="vec_exchange",
        out_shape=jax.ShapeDtypeStruct((N_DEV, n, D), F32),
        in_specs=[pl.BlockSpec(memory_space=pltpu.VMEM)] * n + [ANY],
        out_specs=pl.BlockSpec(memory_space=pl.ANY),
        scratch_shapes=[pltpu.VMEM((n, D), F32), pltpu.SemaphoreType.DMA((N_DEV - 1,)),
                        pltpu.SemaphoreType.DMA((N_DEV - 1,)), pltpu.SemaphoreType.DMA],
    )(*vectors, after)


def _adamw_update(g, w_ref, m_ref, v_ref, g_out, d_out, m_out, v_out):
    m_new = ADAM_B1 * m_ref[...] + (1.0 - ADAM_B1) * g
    v_new = ADAM_B2 * v_ref[...] + (1.0 - ADAM_B2) * (g * g)
    m_hat = m_new / (1.0 - ADAM_B1 ** ADAM_STEP)
    v_hat = v_new / (1.0 - ADAM_B2 ** ADAM_STEP)
    g_out[...] = g
    d_out[...] = -ADAM_LR * (m_hat / (jnp.sqrt(v_hat) + ADAM_EPS) + ADAM_WD * w_ref[...])
    m_out[...] = m_new
    v_out[...] = v_new


ADAMW_ROWS = 256


def _adamw(name, own, parts, w, m, v, slot, token):
    shard = w.shape
    if name in ("w_in", "w_ff1"):
        tr = ADAMW_ROWS
        grid = (shard[0] // tr,)
        own_spec = pl.BlockSpec((tr, shard[1]), lambda i, s: (i, s[0]))
        blk = pl.BlockSpec((tr, shard[1]), lambda i, s: (i, 0))
        parts_spec = pl.BlockSpec((N_DEV - 1, tr, shard[1]), lambda i, s: (0, i, 0))
    elif name == "pool_w":
        grid = (shard[0],)
        own_spec = pl.BlockSpec((None,) + shard[1:], lambda g, s: (g, s[0], 0))
        blk = pl.BlockSpec((None,) + shard[1:], lambda g, s: (g, 0, 0))
        parts_spec = pl.BlockSpec((N_DEV - 1, None) + shard[1:], lambda g, s: (0, g, 0, 0))
    elif name == "dw_kernel":
        grid = (1,)
        own_spec = pl.BlockSpec((None,) + shard, lambda i, s: (s[0], 0, 0))
        blk = pl.BlockSpec(shard, lambda i, s: (0, 0))
        parts_spec = pl.BlockSpec((N_DEV - 1,) + shard, lambda i, s: (0, 0, 0))
    else:
        tr = min(ADAMW_ROWS, shard[0])
        grid = (shard[0] // tr,)
        own_spec = pl.BlockSpec((tr, shard[1]), lambda i, s: (s[0] * grid[0] + i, 0))
        blk = pl.BlockSpec((tr, shard[1]), lambda i, s: (i, 0))
        parts_spec = pl.BlockSpec((N_DEV - 1, tr, shard[1]), lambda i, s: (0, i, 0))

    def body(slot_ref, own_ref, p_ref, w_ref, m_ref, v_ref, _token, g_out, d_out, m_out, v_out):
        g = own_ref[...].astype(F32)
        for k in range(N_DEV - 1):
            g = g + p_ref[k].astype(F32)
        _adamw_update(g, w_ref, m_ref, v_ref, g_out, d_out, m_out, v_out)

    return pl.pallas_call(
        body, name="adamw_" + name,
        grid_spec=pltpu.PrefetchScalarGridSpec(
            num_scalar_prefetch=1, grid=grid, in_specs=[own_spec, parts_spec, blk, blk, blk, ANY], out_specs=[blk] * 4),
        out_shape=[jax.ShapeDtypeStruct(shard, F32)] * 4,
        compiler_params=_params(("arbitrary",), 32),
    )(slot, *_pinned(own, parts, w, m, v, token))


def _adamw_vectors(parts, w, m, v):
    n_vec = w.shape[0]

    def body(p_ref, w_ref, m_ref, v_ref, g_out, d_out, m_out, v_out, loss_out):
        total = p_ref[0]
        for s in range(1, N_DEV):
            total = total + p_ref[s]
        _adamw_update(total[0:n_vec], w_ref, m_ref, v_ref, g_out, d_out, m_out, v_out)
        loss_out[...] = total[n_vec:n_vec + 1] * (0.5 / D)

    return pl.pallas_call(
        body, name="adamw_vectors",
        out_shape=[jax.ShapeDtypeStruct(w.shape, F32)] * 4 + [jax.ShapeDtypeStruct((1, D), F32)],
    )(parts, w, m, v)


def _input_norm(x, g_pre, token):
    seq = x.shape[0]
    tm = 1024

    def body(x_ref, g_ref, _token, u_ref, ut_ref):
        xf = x_ref[...]
        u = (xf * _rms_scale(xf) * g_ref[...]).astype(BF16)
        u_ref[...] = u
        ut_ref[...] = u.T

    return pl.pallas_call(
        body, name="input_norm", grid=(seq // tm,),
        in_specs=[pl.BlockSpec((tm, D), lambda i: (i, 0)), pl.BlockSpec((1, D), lambda i: (0, 0)), ANY],
        out_specs=[pl.BlockSpec((tm, D), lambda i: (i, 0)), pl.BlockSpec((D, tm), lambda i: (0, i))],
        out_shape=[pltpu.HBM((seq, D), BF16), pltpu.HBM((D, seq), BF16)],
        compiler_params=_params(("arbitrary",), 40),
    )(*_pinned(x, g_pre, token))


IN_PROJ_TN = IN_COLS // 4


def _in_proj_fwd(name, u, w_in, blocks, proj, token):
    seq = u.shape[0]
    tm, tn = 1024, IN_PROJ_TN

    def body(_blocks, u_ref, w_ref, _token, _proj_in, proj_ref):
        proj_ref[...] = _dot(u_ref[...], w_ref[...]).astype(BF16)

    return pl.pallas_call(
        body, name=name,
        grid_spec=pltpu.PrefetchScalarGridSpec(
            num_scalar_prefetch=1, grid=(seq // tm, blocks.shape[0]),
            in_specs=[pl.BlockSpec((tm, D), lambda i, j, b: (i, 0)), pl.BlockSpec((D, tn), lambda i, j, b: (0, b[j])),
                      ANY, ANY],
            out_specs=pl.BlockSpec((tm, tn), lambda i, j, b: (i, b[j]))),
        out_shape=pltpu.HBM((seq, IN_COLS), BF16),
        input_output_aliases={4: 0},
        compiler_params=_params(("arbitrary", "arbitrary"), 40),
    )(blocks, *_pinned(u, w_in, token, proj))


CONV_TM = 512
CONV_RS = 128


def _shifts():
    return [(b, [(a, 8 * a + b) for a in range(4) if 8 * a + b < TAPS]) for b in range(8)]


def _taps_looking_back(buf, row0, lanes, weight):
    acc = None
    for b, group in _shifts():
        part = None
        for a, s in group:
            term = weight(TAPS - 1 - s) * buf[pl.ds(row0 - 8 - 8 * a, CONV_RS + 8), lanes]
            part = term if part is None else part + term
        if b:
            part = pltpu.roll(part, b, 0)
        acc = part[8:, :] if acc is None else acc + part[8:, :]
    return acc


def _taps_looking_ahead(buf, row0, lanes, weight):
    acc = None
    for b, group in _shifts():
        part = None
        for a, s in group:
            term = weight(TAPS - 1 - s) * buf[pl.ds(row0 + 8 * a, CONV_RS + 8), lanes]
            part = term if part is None else part + term
        if b:
            part = pltpu.roll(part, CONV_RS + 8 - b, 0)
        acc = part[:CONV_RS, :] if acc is None else acc + part[:CONV_RS, :]
    return acc


def _layer_norm_parts(cv):
    mu = jnp.mean(cv, axis=-1, keepdims=True)
    cen = cv - mu
    rstd = lax.rsqrt(jnp.mean(cen * cen, axis=-1, keepdims=True) + LN_EPS)
    return cen * rstd, rstd


def _conv_fwd(proj, dw, dw_bias, ln_g, ln_b, w_conv_out, token):
    seq = proj.shape[0]
    tm = CONV_TM

    def body(a_ref, gate_ref, dw_ref, bias_ref, lg_ref, lb_ref, w_ref, _token, cv_ref, y_ref, ug):
        i = pl.program_id(0)

        @pl.when(i == 0)
        def _():
            ug[0:HALO, :] = jnp.zeros((HALO, D), F32)

        @pl.when(i > 0)
        def _():
            ug[0:HALO, :] = ug[tm:tm + HALO, :]
        ug[HALO:HALO + tm, :] = a_ref[...].astype(F32) * _sigmoid(gate_ref[...].astype(F32))

        def channel_block(cb, carry):
            lanes = pl.ds(pl.multiple_of(cb * 128, 128), 128)
            for r0 in range(0, tm, CONV_RS):
                taps = _taps_looking_back(ug, HALO + r0, lanes, lambda k: dw_ref[cb, k:k + 1, :])
                cv_ref[pl.ds(r0, CONV_RS), lanes] = taps + bias_ref[:, lanes]
            return carry
        lax.fori_loop(0, D // 128, channel_block, 0)

        n, _ = _layer_norm_parts(cv_ref[...])
        ln = n * lg_ref[...] + lb_ref[...]
        y_ref[...] = _dot((ln * _sigmoid(ln)).astype(BF16), w_ref[...]).astype(BF16)

    vec = pl.BlockSpec((1, D), lambda i: (0, 0))
    tile = pl.BlockSpec((tm, D), lambda i: (i, 0))
    return pl.pallas_call(
        body, name="conv_fwd", grid=(seq // tm,),
        in_specs=[pl.BlockSpec((tm, D), lambda i: (i, 0)), pl.BlockSpec((tm, D), lambda i: (i, 1)),
                  pl.BlockSpec((N_DEV, TAPS, 128), lambda i: (0, 0, 0)), vec, vec, vec,
                  pl.BlockSpec((D, D), lambda i: (0, 0)), ANY],
        out_specs=[tile, tile],
        out_shape=[pltpu.HBM((seq, D), F32), pltpu.HBM((seq, D), BF16)],
        scratch_shapes=[pltpu.VMEM((HALO + tm, D), F32)],
        compiler_params=_params(("arbitrary",), 32),
    )(*_pinned(proj, proj, dw, dw_bias, ln_g, ln_b, w_conv_out, token))


def _window_sums(rows, window, back):
    n = rows.shape[0]
    span = 1
    while span < window:
        rows = rows + pltpu.roll(rows, span if back else n - span, 0)
        span *= 2
    return rows


def _pool_counts(tile_index, tm, window):
    t = tile_index * tm + lax.broadcasted_iota(jnp.int32, (tm, 1), 0)
    return 1.0 / jnp.minimum(t + 1, window).astype(F32)


def _pool_merge_fwd(proj, y_conv, x, pool_w, pool_scale, w_pool_out, w_o, g_post, token):
    seq = proj.shape[0]
    tm = CONV_TM

    def body(p_ref, gc_ref, gp_ref, yc_ref, x_ref, pw_ref, ps_ref, wpo_ref, wo_ref, g_ref, _token,
             z_ref, zl_ref, yp_ref, mg_ref, o_ref, h1_ref, pbuf, zl_buf):
        i = pl.program_id(0)

        @pl.when(i == 0)
        def _():
            pbuf[0:HALO, :] = jnp.zeros((HALO, D), F32)

        @pl.when(i > 0)
        def _():
            pbuf[0:HALO, :] = pbuf[tm:tm + HALO, :]
        pbuf[HALO:HALO + tm, :] = p_ref[...].astype(F32)

        for g, window in enumerate(POOL_WINDOWS):
            lanes = pl.ds(g * PG, PG)
            acc = _window_sums(pbuf[:, lanes], window, back=True)[HALO:, :]
            zg = acc * _pool_counts(i, tm, window) - pbuf[pl.ds(HALO, tm), lanes]
            z_ref[:, lanes] = zg.astype(BF16)
            zl_buf[:, lanes] = _dot(zg.astype(BF16), pw_ref[g])
        zl = zl_buf[...]
        zl_ref[...] = zl.astype(BF16)
        y_pool = _dot((zl * ps_ref[...]).astype(BF16), wpo_ref[...])
        yp_ref[...] = y_pool.astype(BF16)
        merged = (_sigmoid(gc_ref[...].astype(F32)) * yc_ref[...].astype(F32)
                  + _sigmoid(gp_ref[...].astype(F32)) * y_pool).astype(BF16)
        mg_ref[...] = merged
        o = _dot(merged, wo_ref[...])
        o_ref[...] = o
        h1_ref[...] = x_ref[...] + o * _rms_scale(o) * g_ref[...]

    vec = pl.BlockSpec((1, D), lambda i: (0, 0))
    tile = pl.BlockSpec((tm, D), lambda i: (i, 0))
    mat = pl.BlockSpec((D, D), lambda i: (0, 0))
    return pl.pallas_call(
        body, name="pool_merge_fwd", grid=(seq // tm,),
        in_specs=[pl.BlockSpec((tm, D), lambda i: (i, 2)), pl.BlockSpec((tm, D), lambda i: (i, 3)),
                  pl.BlockSpec((tm, D), lambda i: (i, 4)), tile, tile,
                  pl.BlockSpec((4, PG, PG), lambda i: (0, 0, 0)), vec, mat, mat, vec, ANY],
        out_specs=[tile] * 6,
        out_shape=[pltpu.HBM((seq, D), dt) for dt in (BF16, BF16, BF16, BF16, F32, F32)],
        scratch_shapes=[pltpu.VMEM((HALO + tm, D), F32), pltpu.VMEM((tm, D), F32)],
        compiler_params=_params(("arbitrary",), 48),
    )(*_pinned(proj, proj, proj, y_conv, x, pool_w, pool_scale, w_pool_out, w_o, g_post, token))


def _mlp_fwd(h1, g_pre, w_ff1, w_ff2, g_post, target):
    seq = h1.shape[0]
    tm, tf = 1024, D_FF // N_DEV
    n_f = D_FF // tf

    def body(h1_ref, gpre_ref, w1_ref, w2_ref, gpost_ref, tgt_ref, v_ref, dm_ref, dh2_ref, sse_ref, ggrad_ref, macc):
        i, j = pl.program_id(0), pl.program_id(1)

        @pl.when(j == 0)
        def _():
            h = h1_ref[...]
            v_ref[...] = (h * _rms_scale(h) * gpre_ref[...]).astype(BF16)
        f = jnp.maximum(_dot(v_ref[...], w1_ref[...]), 0.0)
        part = _dot((f * f).astype(BF16), w2_ref[...])

        @pl.when(j == 0)
        def _():
            macc[...] = part

        @pl.when(j > 0)
        def _():
            macc[...] += part

        @pl.when(j == n_f - 1)
        def _():
            mo = macc[...]
            err = h1_ref[...] + mo * _rms_scale(mo) * gpost_ref[...] - tgt_ref[...]
            dh2 = err * (1.0 / D)
            dh2_ref[...] = dh2.astype(BF16)
            dm, ggrad = _rms_bwd(dh2, mo, gpost_ref[...])
            dm_ref[...] = dm.astype(BF16)
            _acc_out(ggrad_ref, i == 0, ggrad)
            _acc_out(sse_ref, i == 0, jnp.sum(jnp.sum(err * err, axis=1, keepdims=True), axis=0, keepdims=True))

    vec = pl.BlockSpec((1, D), lambda i, j: (0, 0))
    tile = pl.BlockSpec((tm, D), lambda i, j: (i, 0))
    return pl.pallas_call(
        body, name="mlp_fwd", grid=(seq // tm, n_f),
        in_specs=[tile, vec, pl.BlockSpec((D, tf), lambda i, j: (0, j)), pl.BlockSpec((tf, D), lambda i, j: (j, 0)), vec, tile],
        out_specs=[tile, tile, tile, pl.BlockSpec((1, 1), lambda i, j: (0, 0)), vec],
        out_shape=[pltpu.HBM((seq, D), BF16), pltpu.HBM((seq, D), BF16),
                   pltpu.HBM((seq, D), BF16), jax.ShapeDtypeStruct((1, 1), F32),
                   jax.ShapeDtypeStruct((1, D), F32)],
        scratch_shapes=[pltpu.VMEM((tm, D), F32)],
        compiler_params=_params(("arbitrary", "arbitrary"), 56),
    )(*_pinned(h1, g_pre, w_ff1, w_ff2, g_post, target))


def _mlp_bwd(v, dm, w_ff1, w_ff2):
    seq = v.shape[0]
    tm, tf = 1024, D_FF // N_DEV
    n_t = seq // tm

    def body(v_ref, dm_ref, w1_ref, w2_ref, dv_hbm, g1_ref, g2_ref, dv_acc, g1_acc, g2_acc, sems):
        j, i = pl.program_id(0), pl.program_id(1)
        vt, dmt = v_ref[...], dm_ref[...]
        f = jnp.maximum(_dot(vt, w1_ref[...]), 0.0)
        df = (_dot_nt(dmt, w2_ref[...]) * (2.0 * f)).astype(BF16)
        rows = pl.ds(pl.multiple_of(i * tm, tm), tm)
        dv_part = _dot_nt(df, w1_ref[...])

        @pl.when(j == 0)
        def _():
            dv_acc[rows, :] = dv_part

        @pl.when(j > 0)
        def _():
            dv_acc[rows, :] += dv_part
        g1_part = _dot_tn(vt, df)
        g2_part = _dot_tn((f * f).astype(BF16), dmt)

        @pl.when(i == 0)
        def _():
            g1_acc[...] = g1_part
            g2_acc[...] = g2_part

        @pl.when(i > 0)
        def _():
            g1_acc[...] += g1_part
            g2_acc[...] += g2_part

        @pl.when(i == n_t - 1)
        def _():
            g1_ref[...] = g1_acc[...].astype(BF16)
            g2_ref[...] = g2_acc[...].astype(BF16)

        last_pass = j == D_FF // tf - 1

        @pl.when(last_pass)
        def _():
            pltpu.make_async_copy(dv_acc.at[rows, :], dv_hbm.at[rows, :], sems.at[i]).start()

        @pl.when(jnp.logical_and(last_pass, i == n_t - 1))
        def _():
            for t in range(n_t):
                done = pl.ds(t * tm, tm)
                pltpu.make_async_copy(dv_acc.at[done, :], dv_hbm.at[done, :], sems.at[t]).wait()

    tile = pl.BlockSpec((tm, D), lambda j, i: (i, 0))
    return pl.pallas_call(
        body, name="mlp_bwd", grid=(D_FF // tf, n_t),
        in_specs=[tile, tile, pl.BlockSpec((D, tf), lambda j, i: (0, j)), pl.BlockSpec((tf, D), lambda j, i: (j, 0))],
        out_specs=[pl.BlockSpec(memory_space=pl.ANY), pl.BlockSpec((D, tf), lambda j, i: (0, j)),
                   pl.BlockSpec((tf, D), lambda j, i: (j, 0))],
        out_shape=[pltpu.HBM((seq, D), F32), pltpu.HBM((D, D_FF), BF16),
                   pltpu.HBM((D_FF, D), BF16)],
        scratch_shapes=[pltpu.VMEM((seq, D), F32), pltpu.VMEM((D, tf), F32), pltpu.VMEM((tf, D), F32),
                        pltpu.SemaphoreType.DMA((n_t,))],
        compiler_params=_params(("arbitrary", "arbitrary"), 52),
    )(*_pinned(v, dm, w_ff1, w_ff2))


def _merge_bwd(dh2, dv, h1, g_mlp_pre, o, g_mix_post, w_o, merged, proj, y_conv, y_pool, token):
    seq = dh2.shape[0]
    tm = 256
    n_t = seq // tm

    def body(dh2_ref, dv_ref, h1_ref, gpre_ref, o_ref, gpost_ref, wo_ref, mg_ref, gc_ref, gp_ref, yc_ref, yp_ref, _token,
             dh1_ref, dyc_ref, dyp_ref, dg_ref, gwo_ref, ggpre_ref, ggpost_ref, gwo_acc):
        i = pl.program_id(0)
        dnorm, ggpre = _rms_bwd(dv_ref[...], h1_ref[...], gpre_ref[...])
        dh1 = dh2_ref[...].astype(F32) + dnorm
        dh1_ref[...] = dh1
        do, ggpost = _rms_bwd(dh1, o_ref[...], gpost_ref[...])
        do = do.astype(BF16)
        _acc_out(ggpre_ref, i == 0, ggpre)
        _acc_out(ggpost_ref, i == 0, ggpost)
        _acc_out(gwo_acc, i == 0, _dot_tn(mg_ref[...], do))
        dmerged = _dot_nt(do, wo_ref[...])
        sc, sp = _sigmoid(gc_ref[...].astype(F32)), _sigmoid(gp_ref[...].astype(F32))
        dyc_ref[...] = (dmerged * sc).astype(BF16)
        dyp_ref[...] = (dmerged * sp).astype(BF16)
        dg_ref[:, 0:D] = (dmerged * yc_ref[...].astype(F32) * (sc * (1.0 - sc))).astype(BF16)
        dg_ref[:, D:2 * D] = (dmerged * yp_ref[...].astype(F32) * (sp * (1.0 - sp))).astype(BF16)

        @pl.when(i == n_t - 1)
        def _():
            gwo_ref[...] = gwo_acc[...].astype(BF16)

    vec = pl.BlockSpec((1, D), lambda i: (0, 0))
    tile = pl.BlockSpec((tm, D), lambda i: (i, 0))
    mat = pl.BlockSpec((D, D), lambda i: (0, 0))
    return pl.pallas_call(
        body, name="merge_bwd", grid=(n_t,),
        in_specs=[tile, tile, tile, vec, tile, vec, mat, tile,
                  pl.BlockSpec((tm, D), lambda i: (i, 3)), pl.BlockSpec((tm, D), lambda i: (i, 4)), tile, tile, ANY],
        out_specs=[tile, tile, tile, pl.BlockSpec((tm, 2 * D), lambda i: (i, 0)), mat, vec, vec],
        out_shape=[pltpu.HBM((seq, D), F32), pltpu.HBM((seq, D), BF16),
                   pltpu.HBM((seq, D), BF16), pltpu.HBM((seq, 2 * D), BF16),
                   pltpu.HBM((D, D), BF16), jax.ShapeDtypeStruct((1, D), F32),
                   jax.ShapeDtypeStruct((1, D), F32)],
        scratch_shapes=[pltpu.VMEM((D, D), F32)],
        compiler_params=_params(("arbitrary",), 48),
    )(*_pinned(dh2, dv, h1, g_mlp_pre, o, g_mix_post, w_o, merged, proj, proj, y_conv, y_pool, token))


def _pool_bwd(dy_pool, zl, z, pool_w, pool_scale, w_pool_out, token):
    seq = dy_pool.shape[0]
    tm = CONV_TM
    n_t = seq // tm

    def body(dy_ref, zl_ref, z_ref, pw_ref, ps_ref, wpo_ref, _token,
             dp_ref, gwpo_ref, gpw_ref, gps_ref, qbuf, gwpo_acc, gpw_acc):
        i = pl.program_id(0)
        tile_index = n_t - 1 - i
        first = i == 0
        dy = dy_ref[...]
        zl = zl_ref[...].astype(F32)
        dzs = _dot_nt(dy, wpo_ref[...])
        _acc_out(gwpo_acc, first, _dot_tn((zl * ps_ref[...]).astype(BF16), dy))
        _acc_out(gps_ref, first, jnp.sum(dzs * zl, axis=0, keepdims=True))
        dzl = (dzs * ps_ref[...]).astype(BF16)

        @pl.when(first)
        def _():
            qbuf[tm:tm + HALO, :] = jnp.zeros((HALO, D), F32)

        @pl.when(jnp.logical_not(first))
        def _():
            qbuf[tm:tm + HALO, :] = qbuf[0:HALO, :]

        dzs_list = []
        for g, window in enumerate(POOL_WINDOWS):
            lanes = pl.ds(g * PG, PG)
            dzl_g = dzl[:, g * PG:(g + 1) * PG]
            dz = _dot_nt(dzl_g, pw_ref[g])
            _acc_out(gpw_acc.at[g], first, _dot_tn(z_ref[:, lanes], dzl_g))
            qbuf[pl.ds(0, tm), lanes] = dz * _pool_counts(tile_index, tm, window)
            dzs_list.append(dz)
        for g, window in enumerate(POOL_WINDOWS):
            lanes = pl.ds(g * PG, PG)
            acc = _window_sums(qbuf[:, lanes], window, back=False)[:tm, :]
            dp_ref[:, lanes] = (acc - dzs_list[g]).astype(BF16)

        @pl.when(i == n_t - 1)
        def _():
            gwpo_ref[...] = gwpo_acc[...].astype(BF16)
            gpw_ref[...] = gpw_acc[...].astype(BF16)

    vec = pl.BlockSpec((1, D), lambda i: (0, 0))
    tile = pl.BlockSpec((tm, D), lambda i: (n_t - 1 - i, 0))
    mat = pl.BlockSpec((D, D), lambda i: (0, 0))
    pw = pl.BlockSpec((4, PG, PG), lambda i: (0, 0, 0))
    return pl.pallas_call(
        body, name="pool_bwd", grid=(n_t,),
        in_specs=[tile, tile, tile, pw, vec, mat, ANY],
        out_specs=[tile, mat, pw, vec],
        out_shape=[pltpu.HBM((seq, D), BF16), pltpu.HBM((D, D), BF16),
                   pltpu.HBM((4, PG, PG), BF16), jax.ShapeDtypeStruct((1, D), F32)],
        scratch_shapes=[pltpu.VMEM((tm + HALO, D), F32), pltpu.VMEM((D, D), F32), pltpu.VMEM((4, PG, PG), F32)],
        compiler_params=_params(("arbitrary",), 40),
    )(*_pinned(dy_pool, zl, z, pool_w, pool_scale, w_pool_out, token))


def _conv_bwd(dy_conv, cv, proj, dw, ln_g, ln_b, w_conv_out, token):
    seq = dy_conv.shape[0]
    tm = CONV_TM // 2
    n_t = seq // tm
    halo_blocks = tm // HALO

    def body(dy_ref, cv_ref, a_ref, gate_ref, ah_ref, gh_ref, dw_ref, lg_ref, lb_ref, w_ref, _token,
             dglu_ref, gw_ref, gdw_ref, gbias_ref, glg_ref, glb_ref, ug, dcv, dug, gw_acc):
        i = pl.program_id(0)
        tile_index = n_t - 1 - i
        first = i == 0
        dy = dy_ref[...]
        n, rstd = _layer_norm_parts(cv_ref[...])
        ln = n * lg_ref[...] + lb_ref[...]
        sg = _sigmoid(ln)
        _acc_out(gw_acc, first, _dot_tn((ln * sg).astype(BF16), dy))
        dln = _dot_nt(dy, w_ref[...]) * (sg * (1.0 + ln * (1.0 - sg)))
        _acc_out(glg_ref, first, jnp.sum(dln * n, axis=0, keepdims=True))
        _acc_out(glb_ref, first, jnp.sum(dln, axis=0, keepdims=True))
        dn = dln * lg_ref[...]
        dcv_tile = rstd * (dn - jnp.mean(dn, axis=-1, keepdims=True) - n * jnp.mean(dn * n, axis=-1, keepdims=True))
        _acc_out(gbias_ref, first, jnp.sum(dcv_tile, axis=0, keepdims=True))

        @pl.when(first)
        def _():
            dcv[tm:tm + HALO, :] = jnp.zeros((HALO, D), F32)

        @pl.when(jnp.logical_not(first))
        def _():
            dcv[tm:tm + HALO, :] = dcv[0:HALO, :]
        dcv[0:tm, :] = dcv_tile

        a, gate = a_ref[...].astype(F32), gate_ref[...].astype(F32)
        sgate = _sigmoid(gate)
        ug[HALO:HALO + tm, :] = a * sgate
        before = jnp.where(tile_index > 0, 1.0, 0.0)
        ug[0:HALO, :] = ah_ref[...].astype(F32) * _sigmoid(gh_ref[...].astype(F32)) * before

        @pl.when(first)
        def _():
            gdw_ref[...] = jnp.zeros((N_DEV, TAPS + 1, 128), F32)

        def channel_block(cb, carry):
            lanes = pl.ds(pl.multiple_of(cb * 128, 128), 128)
            for r0 in range(0, tm, CONV_RS):
                dug[pl.ds(r0, CONV_RS), lanes] = _taps_looking_ahead(dcv, r0, lanes, lambda k: dw_ref[cb, k:k + 1, :])
            for b, group in _shifts():
                sums = [jnp.zeros((8, 128), F32) for _ in group]
                for r0 in range(0, tm, CONV_RS):
                    window = ug[pl.ds(r0, CONV_RS + HALO), lanes]
                    if b:
                        window = pltpu.roll(window, b, 0)
                    d = dcv[pl.ds(r0, CONV_RS), lanes]
                    for n_a, (a, s) in enumerate(group):
                        prod = d * window[HALO - 8 * a:HALO - 8 * a + CONV_RS, :]
                        sums[n_a] = sums[n_a] + jnp.sum(prod.reshape(CONV_RS // 8, 8, 128), axis=0)
                for n_a, (a, s) in enumerate(group):
                    k = TAPS - 1 - s
                    gdw_ref[cb, k:k + 1, :] += jnp.sum(sums[n_a], axis=0, keepdims=True)
            return carry
        lax.fori_loop(0, D // 128, channel_block, 0)

        d_ug = dug[...]
        dglu_ref[:, 0:D] = (d_ug * sgate).astype(BF16)
        dglu_ref[:, D:2 * D] = (d_ug * a * (sgate * (1.0 - sgate))).astype(BF16)

        @pl.when(i == n_t - 1)
        def _():
            gw_ref[...] = gw_acc[...].astype(BF16)

    def halo_index(col):
        return lambda i: (jnp.maximum((n_t - 1 - i) * halo_blocks - 1, 0), col)

    vec = pl.BlockSpec((1, D), lambda i: (0, 0))
    tile = pl.BlockSpec((tm, D), lambda i: (n_t - 1 - i, 0))
    mat = pl.BlockSpec((D, D), lambda i: (0, 0))
    dwspec = pl.BlockSpec((N_DEV, TAPS, 128), lambda i: (0, 0, 0))
    return pl.pallas_call(
        body, name="conv_bwd", grid=(n_t,),
        in_specs=[tile, tile, pl.BlockSpec((tm, D), lambda i: (n_t - 1 - i, 0)), pl.BlockSpec((tm, D), lambda i: (n_t - 1 - i, 1)),
                  pl.BlockSpec((HALO, D), halo_index(0)), pl.BlockSpec((HALO, D), halo_index(1)), dwspec, vec, vec, mat,
                  ANY],
        out_specs=[pl.BlockSpec((tm, 2 * D), lambda i: (n_t - 1 - i, 0)), mat,
                   pl.BlockSpec((N_DEV, TAPS + 1, 128), lambda i: (0, 0, 0)), vec, vec, vec],
        out_shape=[pltpu.HBM((seq, 2 * D), BF16), pltpu.HBM((D, D), BF16),
                   pltpu.HBM((N_DEV, TAPS + 1, 128), F32), jax.ShapeDtypeStruct((1, D), F32),
                   jax.ShapeDtypeStruct((1, D), F32), jax.ShapeDtypeStruct((1, D), F32)],
        scratch_shapes=[pltpu.VMEM((HALO + tm, D), F32), pltpu.VMEM((tm + HALO, D), F32), pltpu.VMEM((tm, D), F32),
                        pltpu.VMEM((D, D), F32)],
        compiler_params=_params(("arbitrary",), 48),
    )(*_pinned(dy_conv, cv, proj, proj, proj, proj, dw, ln_g, ln_b, w_conv_out, token))


def _in_proj_bwd_x(d_glu, dp, dgates, w_in, x, g_pre, dh1, token):
    seq = x.shape[0]
    tm = 512

    def body(dglu_ref, dp_ref, dg_ref, w_ref, x_ref, g_ref, dh1_ref, _token, dx_ref, gg_ref):
        du = _dot_nt(dglu_ref[...], w_ref[:, 0:2 * D])
        du += _dot_nt(dp_ref[...], w_ref[:, 2 * D:3 * D])
        du += _dot_nt(dg_ref[...], w_ref[:, 3 * D:5 * D])
        dnorm, gg = _rms_bwd(du, x_ref[...], g_ref[...])
        dx_ref[...] = dh1_ref[...] + dnorm
        _acc_out(gg_ref, pl.program_id(0) == 0, gg)

    vec = pl.BlockSpec((1, D), lambda i: (0, 0))
    tile = pl.BlockSpec((tm, D), lambda i: (i, 0))
    wide = pl.BlockSpec((tm, 2 * D), lambda i: (i, 0))
    return pl.pallas_call(
        body, name="in_proj_bwd_x", grid=(seq // tm,),
        in_specs=[wide, tile, wide, pl.BlockSpec((D, IN_COLS), lambda i: (0, 0)), tile, vec, tile, ANY],
        out_specs=[tile, vec],
        out_shape=[pltpu.HBM((seq, D), F32), jax.ShapeDtypeStruct((1, D), F32)],
        compiler_params=_params(("arbitrary",), 48),
    )(*_pinned(d_glu, dp, dgates, w_in, x, g_pre, dh1, token))


def _in_proj_bwd_w(u_t, d_glu, dp, dgates, token):
    seq = u_t.shape[1]
    tm = 2048
    n_t = seq // tm

    def body(u_ref, dglu_ref, dp_ref, dg_ref, _token, out_ref, acc):
        b, i = pl.program_id(0), pl.program_id(1)
        ut = u_ref[...]

        def add(d_ref):
            _acc_out(acc, i == 0, _dot(ut, d_ref[...]))

        pl.when(b < 2)(lambda: add(dglu_ref))
        pl.when(b == 2)(lambda: add(dp_ref))
        pl.when(b > 2)(lambda: add(dg_ref))

        @pl.when(i == n_t - 1)
        def _():
            out_ref[...] = acc[...].astype(BF16)

    return pl.pallas_call(
        body, name="in_proj_bwd_w", grid=(IN_COLS // D, n_t),
        in_specs=[pl.BlockSpec((D, tm), lambda b, i: (0, i)),
                  pl.BlockSpec((tm, D), lambda b, i: (jnp.where(b < 2, i, 0), jnp.minimum(b, 1))),
                  pl.BlockSpec((tm, D), lambda b, i: (jnp.where(b == 2, i, 0), 0)),
                  pl.BlockSpec((tm, D), lambda b, i: (jnp.where(b > 2, i, 0), jnp.maximum(b - 3, 0))), ANY],
        out_specs=pl.BlockSpec((D, D), lambda b, i: (0, b)),
        out_shape=pltpu.HBM((D, IN_COLS), BF16),
        scratch_shapes=[pltpu.VMEM((D, D), F32)],
        compiler_params=_params(("arbitrary", "arbitrary"), 52),
    )(*_pinned(u_t, d_glu, dp, dgates, token))


VEC_NAMES = ("mix_pre_g", "dw_bias", "conv_ln_g", "conv_ln_b", "pool_scale", "mix_post_g", "mlp_pre_g", "mlp_post_g")
WEIGHT_ORDER = ("mix_pre_g", "w_in", "dw_kernel", "dw_bias", "conv_ln_g", "conv_ln_b", "w_conv_out", "pool_w",
                "pool_scale", "w_pool_out", "w_o", "mix_post_g", "mlp_pre_g", "w_ff1", "w_ff2", "mlp_post_g")


def _step(x, loss_target, w, m, v):
    row = lambda a: a.reshape(1, D)
    names = [s[0] for s in SHARDED]

    gather_groups = (("w_in",), ("w_conv_out", "dw_kernel"), ("pool_w", "w_pool_out", "w_o"), ("w_ff1", "w_ff2"))
    legs = [GATHER_ICI, GATHER_D2D]
    first_level, token = _exchange_start(
        "gather_start_w_in", [(gather_groups[0], None, _stage_shards(gather_groups[0], [w["w_in"]], x))], legs)
    later = [n for g in gather_groups[1:] for n in g]
    seeded = dict(zip(later, _stage_shards(later, [w[n] for n in later], token)))
    rest, token = _exchange_start("gather_start", [(g, None, [seeded[n] for n in g]) for g in gather_groups[1:]], legs)
    first_level = first_level + rest
    forwarded, full = {}, {}

    def forward(k, after):
        group, (ici_sems, d2d_sems), _, landed = first_level[k]
        _, landed = _exchange_wait("gather_ici_" + group[0], group, None, landed, [(ici_sems, GATHER_ICI)], after)
        (second,), tok = _exchange_start("gather_forward_" + group[0], [(group, None, landed)], [GATHER_FORWARD])
        forwarded[k] = (second[3], [(d2d_sems, GATHER_D2D), (second[1][0], GATHER_FORWARD)])
        return tok

    def gathered(k, after):
        group = gather_groups[k]
        landed, waits = forwarded[k]
        _, arrays = _exchange_wait("gather_wait_" + group[0], group, None, landed, waits, after)
        full.update(zip(group, arrays))

    u, u_t = _input_norm(x, row(w["mix_pre_g"]), token)
    group, (ici_sems, d2d_sems), _, landed = first_level[0]
    _, landed = _exchange_wait("gather_d2d_w_in", group, None, landed, [(d2d_sems, GATHER_D2D)], u)
    home = jnp.reshape(_my_slot() // 2, (1,)).astype(jnp.int32)
    proj = _in_proj_fwd("in_proj_fwd_home", u, landed[0], home, lax.empty((x.shape[0], IN_COLS), BF16), u)
    _, landed = _exchange_wait("gather_ici_w_in", group, None, landed, [(ici_sems, GATHER_ICI)], proj)
    (second,), tok = _exchange_start("gather_forward_w_in", [(group, None, landed)], [GATHER_FORWARD])
    (full["w_in"],) = _exchange_wait("gather_wait_w_in", group, None, second[3], [(second[1][0], GATHER_FORWARD)], tok)[1]
    tok = forward(1, full["w_in"])
    away = jnp.stack([home[0] ^ 1, home[0] ^ 2, home[0] ^ 3]).astype(jnp.int32)
    proj = _in_proj_fwd("in_proj_fwd", u, full["w_in"], away, proj, tok)
    gathered(1, proj)
    tok = forward(2, proj)
    cv, y_conv = _conv_fwd(proj, full["dw_kernel"], row(w["dw_bias"]), row(w["conv_ln_g"]), row(w["conv_ln_b"]),
                           full["w_conv_out"], tok)
    gathered(2, y_conv)
    tok = forward(3, y_conv)
    z, zl, y_pool, merged, o, h1 = _pool_merge_fwd(proj, y_conv, x, full["pool_w"], row(w["pool_scale"]),
                                                   full["w_pool_out"], full["w_o"], row(w["mix_post_g"]), tok)
    gathered(3, h1)
    vv, dm, dh2, sse, g_mlp_post = _mlp_fwd(h1, row(w["mlp_pre_g"]), full["w_ff1"], full["w_ff2"],
                                            row(w["mlp_post_g"]), loss_target)

    shard_of = {name: shard for name, _, shard, _ in SHARDED}

    def scatter_start(tag, group, grads):
        landings = [lax.empty((N_DEV - 1,) + shard_of[n], g.dtype) for n, g in zip(group, grads)]
        (handle,), tok = _exchange_start("scatter_start_" + tag, [(group, grads, landings)], [SCATTER])
        return handle, tok

    dv, g_ff1, g_ff2 = _mlp_bwd(vv, dm, full["w_ff1"], full["w_ff2"])
    h_ff, tok_ff = scatter_start("ff", ("w_ff1", "w_ff2"), [g_ff1, g_ff2])
    dh1, dy_conv, dy_pool, dgates, g_wo, g_mlp_pre, g_mix_post = _merge_bwd(
        dh2, dv, h1, row(w["mlp_pre_g"]), o, row(w["mix_post_g"]), full["w_o"], merged, proj, y_conv, y_pool, tok_ff)
    dp, g_wpo, g_pw, g_pool_scale = _pool_bwd(dy_pool, zl, z, full["pool_w"], row(w["pool_scale"]), full["w_pool_out"],
                                              dh1)
    h_pool, tok_pool = scatter_start("pool", ("w_o", "w_pool_out", "pool_w"), [g_wo, g_wpo, g_pw])
    d_glu, g_wco, g_dw, g_bias, g_ln_g, g_ln_b = _conv_bwd(dy_conv, cv, proj, full["dw_kernel"], row(w["conv_ln_g"]),
                                                            row(w["conv_ln_b"]), full["w_conv_out"], tok_pool)
    h_conv, tok_conv = scatter_start("conv", ("w_conv_out", "dw_kernel"), [g_wco, g_dw[:, :TAPS, :]])
    g_win = _in_proj_bwd_w(u_t, d_glu, dp, dgates, tok_conv)
    h_in, tok_in = scatter_start("in", ("w_in",), [g_win])
    grad_x, g_mix_pre = _in_proj_bwd_x(d_glu, dp, dgates, full["w_in"], x, row(w["mix_pre_g"]), dh1, tok_in)

    out = {}
    slot = jnp.reshape(_my_slot(), (1,)).astype(jnp.int32)
    after = grad_x
    for tag, (group, (sems,), grads, landings) in (("ff", h_ff), ("pool", h_pool), ("conv", h_conv), ("in", h_in)):
        if tag == "in":
            vec_parts = after = _vec_exchange(
                [g_mix_pre, g_bias, g_ln_g, g_ln_b, g_pool_scale, g_mix_post, g_mlp_pre, g_mlp_post,
                 jnp.broadcast_to(sse, (1, D))], after)
        mine, landed = _exchange_wait("scatter_wait_" + tag, group, grads, landings, [(sems, SCATTER)], after)
        for name, own, parts in zip(group, mine, landed):
            out[name] = _adamw(name, own, parts, w[name], m[name], v[name], slot, after)
            after = out[name][0]
    stack = lambda d: jnp.stack([d[n] for n in VEC_NAMES], axis=0)
    *res, loss_row = _adamw_vectors(vec_parts, stack(w), stack(m), stack(v))
    for k, name in enumerate(VEC_NAMES):
        out[name] = [r[k] for r in res]
    return loss_row[0, 0], grad_x, out


def kernel(x, mix_pre_g, w_in, dw_kernel, dw_bias, conv_ln_g, conv_ln_b, w_conv_out, pool_w, pool_scale, w_pool_out, w_o, mix_post_g, mlp_pre_g, w_ff1, w_ff2, mlp_post_g, loss_target, m_mix_pre_g, m_w_in, m_dw_kernel, m_dw_bias, m_conv_ln_g, m_conv_ln_b, m_w_conv_out, m_pool_w, m_pool_scale, m_w_pool_out, m_w_o, m_mix_post_g, m_mlp_pre_g, m_w_ff1, m_w_ff2, m_mlp_post_g, v_mix_pre_g, v_w_in, v_dw_kernel, v_dw_bias, v_conv_ln_g, v_conv_ln_b, v_w_conv_out, v_pool_w, v_pool_scale, v_w_pool_out, v_w_o, v_mix_post_g, v_mlp_pre_g, v_w_ff1, v_w_ff2, v_mlp_post_g):
    w = dict(mix_pre_g=mix_pre_g, w_in=w_in, dw_kernel=dw_kernel, dw_bias=dw_bias, conv_ln_g=conv_ln_g, conv_ln_b=conv_ln_b,
             w_conv_out=w_conv_out, pool_w=pool_w, pool_scale=pool_scale, w_pool_out=w_pool_out, w_o=w_o,
             mix_post_g=mix_post_g, mlp_pre_g=mlp_pre_g, w_ff1=w_ff1, w_ff2=w_ff2, mlp_post_g=mlp_post_g)
    m = dict(mix_pre_g=m_mix_pre_g, w_in=m_w_in, dw_kernel=m_dw_kernel, dw_bias=m_dw_bias, conv_ln_g=m_conv_ln_g,
             conv_ln_b=m_conv_ln_b, w_conv_out=m_w_conv_out, pool_w=m_pool_w, pool_scale=m_pool_scale,
             w_pool_out=m_w_pool_out, w_o=m_w_o, mix_post_g=m_mix_post_g, mlp_pre_g=m_mlp_pre_g, w_ff1=m_w_ff1,
             w_ff2=m_w_ff2, mlp_post_g=m_mlp_post_g)
    v = dict(mix_pre_g=v_mix_pre_g, w_in=v_w_in, dw_kernel=v_dw_kernel, dw_bias=v_dw_bias, conv_ln_g=v_conv_ln_g,
             conv_ln_b=v_conv_ln_b, w_conv_out=v_w_conv_out, pool_w=v_pool_w, pool_scale=v_pool_scale,
             w_pool_out=v_w_pool_out, w_o=v_w_o, mix_post_g=v_mix_post_g, mlp_pre_g=v_mlp_pre_g, w_ff1=v_w_ff1,
             w_ff2=v_w_ff2, mlp_post_g=v_mlp_post_g)
    seq = x.shape[1]
    loss, grad_x, out = _step(x.reshape(seq, D), loss_target.reshape(seq, D), w, m, v)
    grads, deltas, new_m, new_v = ([out[n][k] for n in WEIGHT_ORDER] for k in range(4))
    return (loss, grad_x.reshape(x.shape), *grads, *deltas, *new_m, *new_v)
```

```python
import collections

import jax
import jax.numpy as jnp
from jax import lax
from jax.experimental import pallas as pl
from jax.experimental.pallas import tpu as pltpu

D = 1024
D_FF = 4 * D
IN_COLS = 5 * D
TAPS = 31
HALO = 32
POOL_WINDOWS = (2, 4, 8, 16)
PG = D // 4
N_DEV = 8
RMS_EPS = 1e-6
LN_EPS = 1e-5
ADAM_LR, ADAM_B1, ADAM_B2, ADAM_EPS, ADAM_WD, ADAM_STEP = 0.001, 0.9, 0.999, 1e-08, 0.01, 10

BF16 = jnp.bfloat16
F32 = jnp.float32
MIB = 1 << 20
MESH = pl.DeviceIdType.MESH


def _params(sem, vmem_mib):
    return pltpu.CompilerParams(dimension_semantics=sem, vmem_limit_bytes=vmem_mib * MIB)


def _dot(a, b):
    return jnp.dot(a, b, preferred_element_type=F32)


def _dot_nt(a, b):
    return lax.dot_general(a, b, (((1,), (1,)), ((), ())), preferred_element_type=F32)


def _dot_tn(a, b):
    return lax.dot_general(a, b, (((0,), (0,)), ((), ())), preferred_element_type=F32)


def _rms_scale(x):
    return lax.rsqrt(jnp.mean(x * x, axis=-1, keepdims=True) + RMS_EPS)


def _rms_bwd(dy, x, g):
    xn = x * _rms_scale(x)
    dn = dy * g
    dx = _rms_scale(x) * (dn - xn * jnp.mean(dn * xn, axis=-1, keepdims=True))
    return dx, jnp.sum(dy * xn, axis=0, keepdims=True)


def _sigmoid(x):
    return jax.nn.sigmoid(x)


def _acc_out(ref, first, value):
    @pl.when(first)
    def _():
        ref[...] = value

    @pl.when(jnp.logical_not(first))
    def _():
        ref[...] += value


def _my_slot():
    return 4 * lax.axis_index("x") + 2 * lax.axis_index("y") + lax.axis_index("c")


def _peer(mask):
    x, y, c = lax.axis_index("x"), lax.axis_index("y"), lax.axis_index("c")
    return (x ^ ((mask >> 2) & 1), y ^ ((mask >> 1) & 1), c ^ (mask & 1))


def _cols(width):
    return lambda ref, slot: ref.at[:, pl.ds(pl.multiple_of(slot * width, 128), width)]


def _rows(height):
    return lambda ref, slot: ref.at[pl.ds(pl.multiple_of(slot * height, 8), height), :]


def _lead(ref, slot):
    return ref.at[slot]


def _pool_rows(ref, slot):
    return ref.at[:, pl.ds(pl.multiple_of(slot * (PG // N_DEV), 8), PG // N_DEV), :]


SHARDED = (
    ("w_in", (D, IN_COLS), (D, IN_COLS // N_DEV), _cols(IN_COLS // N_DEV)),
    ("w_ff1", (D, D_FF), (D, D_FF // N_DEV), _cols(D_FF // N_DEV)),
    ("w_ff2", (D_FF, D), (D_FF // N_DEV, D), _rows(D_FF // N_DEV)),
    ("w_conv_out", (D, D), (D // N_DEV, D), _rows(D // N_DEV)),
    ("w_pool_out", (D, D), (D // N_DEV, D), _rows(D // N_DEV)),
    ("w_o", (D, D), (D // N_DEV, D), _rows(D // N_DEV)),
    ("pool_w", (4, PG, PG), (4, PG // N_DEV, PG), _pool_rows),
    ("dw_kernel", (N_DEV, TAPS, D // N_DEV), (TAPS, D // N_DEV), _lead),
)
N_SHARDED = len(SHARDED)


SHARD_AT = {name: at for name, _, _, at in SHARDED}
HBM = pl.BlockSpec(memory_space=pltpu.HBM)
SEM = pl.BlockSpec(memory_space=pltpu.SEMAPHORE)
ANY = pl.BlockSpec(memory_space=pl.ANY)
EFFECT = pltpu.SideEffectType.DATAFLOW_SIDE_EFFECTING


def _in_hbm(a):
    return pltpu.with_memory_space_constraint(a, pltpu.HBM)


def _pinned(*arrays):
    return [_in_hbm(a) for a in arrays]


def _stage_shards(names, shards, token):
    n = len(names)
    specs = [s for name in names for s in SHARDED if s[0] == name]
    dtypes = [F32 if name == "dw_kernel" else BF16 for name in names]

    def body(*refs):
        ins = refs[:n]
        fulls = refs[n + 1:2 * n + 1]
        raw = refs[2 * n + 1:3 * n + 1]
        stage = refs[3 * n + 1:4 * n + 1]
        in_sems, out_sems = refs[4 * n + 1:]
        me = _my_slot()
        loads = [pltpu.make_async_copy(ins[a], raw[a], in_sems.at[a]) for a in range(n)]
        for cp in loads:
            cp.start()
        stores = []
        for a, (_, _, _, at) in enumerate(specs):
            loads[a].wait()
            stage[a][...] = raw[a][...].astype(dtypes[a])
            cp = pltpu.make_async_copy(stage[a], at(fulls[a], me), out_sems.at[a])
            cp.start()
            stores.append(cp)
        for cp in stores:
            cp.wait()

    return pl.pallas_call(
        body, name="stage_" + names[0],
        out_shape=[pltpu.HBM(full, dt) for (_, full, _, _), dt in zip(specs, dtypes)],
        in_specs=[ANY] * (n + 1),
        out_specs=[ANY] * n,
        scratch_shapes=[pltpu.VMEM(shard, F32) for _, _, shard, _ in specs]
        + [pltpu.VMEM(shard, dt) for (_, _, shard, _), dt in zip(specs, dtypes)]
        + [pltpu.SemaphoreType.DMA((n,)), pltpu.SemaphoreType.DMA((n,))],
        compiler_params=pltpu.CompilerParams(vmem_limit_bytes=40 * MIB),
    )(*_pinned(*shards, token))


Leg = collections.namedtuple("Leg", "routes src_of dst_of src_is_land")


def _sem_index(k, m):
    return k * (N_DEV - 1) + m - 1


def _exchange_start(name, groups, legs):
    sizes = [len(g[0]) for g in groups]
    names = [nm for g in groups for nm in g[0]]
    srcs = [s for g in groups if g[1] is not None for s in g[1]]
    lands = [l for g in groups for l in g[2]]
    n_src, n, n_g, n_l = len(srcs), len(names), len(groups), len(legs)

    def body(*refs):
        src_refs, land_refs = list(refs[:n_src]), refs[n_src:n_src + n]
        sems = refs[n_src + n:n_src + n + 2 * n_g * n_l]
        token = refs[-1]
        me = _my_slot()
        first = 0
        for g, size in enumerate(sizes):
            own_src = [src_refs.pop(0) for _ in range(size)] if groups[g][1] is not None else None
            for li, leg in enumerate(legs):
                send, recv = sems[2 * (g * n_l + li)], sems[2 * (g * n_l + li) + 1]
                for m, via in leg.routes:
                    for k in range(size):
                        land = land_refs[first + k]
                        src = land if leg.src_is_land else own_src[k]
                        pltpu.make_async_remote_copy(
                            src_ref=leg.src_of(names[first + k], src, me, m), dst_ref=leg.dst_of(names[first + k], land, me, m),
                            send_sem=send.at[_sem_index(k, m)], recv_sem=recv.at[_sem_index(k, m)],
                            device_id=_peer(via), device_id_type=MESH).start()
            first += size
        token[...] = jnp.zeros_like(token)

    sem_shapes = [pltpu.SemaphoreType.DMA((size * (N_DEV - 1),)) for size in sizes for _ in range(2 * n_l)]
    n_sem = len(sem_shapes)
    outs = pl.pallas_call(
        body, name=name,
        out_shape=sem_shapes + [pltpu.HBM(a.shape, a.dtype) for a in srcs + lands] + [jax.ShapeDtypeStruct((8, 128), F32)],
        in_specs=[HBM] * (n_src + n),
        out_specs=[SEM] * n_sem + [HBM] * (n_src + n) + [pl.BlockSpec(memory_space=pltpu.VMEM)],
        input_output_aliases={k: n_sem + k for k in range(n_src + n)},
        compiler_params=pltpu.CompilerParams(has_side_effects=EFFECT),
    )(*[_in_hbm(a) for a in srcs + lands])
    sems, thru, token = outs[:n_sem], list(outs[n_sem:-1]), outs[-1]
    src_thru, land_thru = thru[:n_src], thru[n_src:]
    handles, first = [], 0
    for g, size in enumerate(sizes):
        pairs = [(sems[2 * (g * n_l + li)], sems[2 * (g * n_l + li) + 1]) for li in range(n_l)]
        mine = [src_thru.pop(0) for _ in range(size)] if groups[g][1] is not None else None
        handles.append((groups[g][0], pairs, mine, land_thru[first:first + size]))
        first += size
    return handles, token


def _exchange_wait(name, names, srcs, lands, waits, after):
    n = len(names)
    n_src = n if srcs is not None else 0

    def body(*refs):
        src_refs, land_refs = refs[:n_src], refs[n_src:n_src + n]
        sems = refs[n_src + n:n_src + n + 2 * len(waits)]
        me = _my_slot()
        for wi, (_, leg) in enumerate(waits):
            for m, via in leg.routes:
                for k in range(n):
                    src = land_refs[k] if leg.src_is_land else src_refs[k]
                    cp = pltpu.make_async_remote_copy(
                        src_ref=leg.src_of(names[k], src, me, m), dst_ref=leg.dst_of(names[k], land_refs[k], me ^ via, m),
                        send_sem=sems[2 * wi].at[_sem_index(k, m)], recv_sem=sems[2 * wi + 1].at[_sem_index(k, m)],
                        device_id=_peer(via), device_id_type=MESH)
                    cp.wait_send()
                    cp.wait_recv()

    arrays = (list(srcs) if srcs is not None else []) + list(lands)
    outs = pl.pallas_call(
        body, name=name,
        out_shape=[pltpu.HBM(a.shape, a.dtype) for a in arrays],
        in_specs=[HBM] * len(arrays) + [SEM] * (2 * len(waits)) + [pl.BlockSpec(memory_space=pl.ANY)],
        out_specs=[HBM] * len(arrays),
        input_output_aliases={k: k for k in range(len(arrays))},
        compiler_params=pltpu.CompilerParams(has_side_effects=EFFECT),
    )(*arrays, *[s for pair, _ in waits for s in pair], after)
    return (outs[:n_src] if srcs is not None else None), outs[n_src:]


def _shard_slot(name, ref, slot):
    return SHARD_AT[name](ref, slot)


GATHER_ICI = Leg(((2, 2), (4, 4), (6, 6)), lambda name, ref, me, m: _shard_slot(name, ref, me),
                 lambda name, ref, sender, m: _shard_slot(name, ref, sender), True)
GATHER_D2D = Leg(((1, 1),), GATHER_ICI.src_of, GATHER_ICI.dst_of, True)
GATHER_FORWARD = Leg(((2, 1), (4, 1), (6, 1)), lambda name, ref, me, m: _shard_slot(name, ref, me ^ m),
                     lambda name, ref, sender, m: _shard_slot(name, ref, sender ^ m), True)
SCATTER = Leg(tuple((m, m) for m in range(1, N_DEV)), lambda name, ref, me, m: _shard_slot(name, ref, me ^ m),
              lambda name, ref, sender, m: ref.at[m - 1], False)


def _vec_exchange(vectors, after):
    n = len(vectors)

    def body(*refs):
        vec, vec_out = refs[n + 2], refs[n + 1]
        send_sems, recv_sems, local_sem = refs[n + 3:]
        for k in range(n):
            vec[k:k + 1, :] = refs[k][...]
        me = _my_slot()
        local = pltpu.make_async_copy(vec, vec_out.at[me], local_sem)
        local.start()
        sends = []
        for mask in range(1, N_DEV):
            cp = pltpu.make_async_remote_copy(
                src_ref=vec, dst_ref=vec_out.at[me], send_sem=send_sems.at[mask - 1],
                recv_sem=recv_sems.at[mask - 1], device_id=_peer(mask), device_id_type=MESH)
            cp.start()
            sends.append(cp)
        for mask in range(1, N_DEV):
            pltpu.make_async_remote_copy(
                src_ref=vec, dst_ref=vec_out.at[me ^ mask], send_sem=send_sems.at[mask - 1],
                recv_sem=recv_sems.at[mask - 1], device_id=_peer(mask), device_id_type=MESH).wait_recv()
        for cp in sends:
            cp.wait_send()
        local.wait()

    return pl.pallas_call(
        body, name="vec_exchange",
        out_shape=jax.ShapeDtypeStruct((N_DEV, n, D), F32),
        in_specs=[pl.BlockSpec(memory_space=pltpu.VMEM)] * n + [ANY],
        out_specs=pl.BlockSpec(memory_space=pl.ANY),
        scratch_shapes=[pltpu.VMEM((n, D), F32), pltpu.SemaphoreType.DMA((N_DEV - 1,)),
                        pltpu.SemaphoreType.DMA((N_DEV - 1,)), pltpu.SemaphoreType.DMA],
    )(*vectors, after)


def _adamw_update(g, w_ref, m_ref, v_ref, g_out, d_out, m_out, v_out):
    m_new = ADAM_B1 * m_ref[...] + (1.0 - ADAM_B1) * g
    v_new = ADAM_B2 * v_ref[...] + (1.0 - ADAM_B2) * (g * g)
    m_hat = m_new / (1.0 - ADAM_B1 ** ADAM_STEP)
    v_hat = v_new / (1.0 - ADAM_B2 ** ADAM_STEP)
    g_out[...] = g
    d_out[...] = -ADAM_LR * (m_hat / (jnp.sqrt(v_hat) + ADAM_EPS) + ADAM_WD * w_ref[...])
    m_out[...] = m_new
    v_out[...] = v_new


ADAMW_ROWS = 256


def _adamw(name, own, parts, w, m, v, slot, token):
    shard = w.shape
    if name in ("w_in", "w_ff1"):
        tr = ADAMW_ROWS
        grid = (shard[0] // tr,)
        own_spec = pl.BlockSpec((tr, shard[1]), lambda i, s: (i, s[0]))
        blk = pl.BlockSpec((tr, shard[1]), lambda i, s: (i, 0))
        parts_spec = pl.BlockSpec((N_DEV - 1, tr, shard[1]), lambda i, s: (0, i, 0))
    elif name == "pool_w":
        grid = (shard[0],)
        own_spec = pl.BlockSpec((None,) + shard[1:], lambda g, s: (g, s[0], 0))
        blk = pl.BlockSpec((None,) + shard[1:], lambda g, s: (g, 0, 0))
        parts_spec = pl.BlockSpec((N_DEV - 1, None) + shard[1:], lambda g, s: (0, g, 0, 0))
    elif name == "dw_kernel":
        grid = (1,)
        own_spec = pl.BlockSpec((None,) + shard, lambda i, s: (s[0], 0, 0))
        blk = pl.BlockSpec(shard, lambda i, s: (0, 0))
        parts_spec = pl.BlockSpec((N_DEV - 1,) + shard, lambda i, s: (0, 0, 0))
    else:
        tr = min(ADAMW_ROWS, shard[0])
        grid = (shard[0] // tr,)
        own_spec = pl.BlockSpec((tr, shard[1]), lambda i, s: (s[0] * grid[0] + i, 0))
        blk = pl.BlockSpec((tr, shard[1]), lambda i, s: (i, 0))
        parts_spec = pl.BlockSpec((N_DEV - 1, tr, shard[1]), lambda i, s: (0, i, 0))

    def body(slot_ref, own_ref, p_ref, w_ref, m_ref, v_ref, _token, g_out, d_out, m_out, v_out):
        g = own_ref[...].astype(F32)
        for k in range(N_DEV - 1):
            g = g + p_ref[k].astype(F32)
        _adamw_update(g, w_ref, m_ref, v_ref, g_out, d_out, m_out, v_out)

    return pl.pallas_call(
        body, name="adamw_" + name,
        grid_spec=pltpu.PrefetchScalarGridSpec(
            num_scalar_prefetch=1, grid=grid, in_specs=[own_spec, parts_spec, blk, blk, blk, ANY], out_specs=[blk] * 4),
        out_shape=[jax.ShapeDtypeStruct(shard, F32)] * 4,
        compiler_params=_params(("arbitrary",), 32),
    )(slot, *_pinned(own, parts, w, m, v, token))


def _adamw_vectors(parts, w, m, v):
    n_vec = w.shape[0]

    def body(p_ref, w_ref, m_ref, v_ref, g_out, d_out, m_out, v_out, loss_out):
        total = p_ref[0]
        for s in range(1, N_DEV):
            total = total + p_ref[s]
        _adamw_update(total[0:n_vec], w_ref, m_ref, v_ref, g_out, d_out, m_out, v_out)
        loss_out[...] = total[n_vec:n_vec + 1] * (0.5 / D)

    return pl.pallas_call(
        body, name="adamw_vectors",
        out_shape=[jax.ShapeDtypeStruct(w.shape, F32)] * 4 + [jax.ShapeDtypeStruct((1, D), F32)],
    )(parts, w, m, v)


def _input_norm(x, g_pre, token):
    seq = x.shape[0]
    tm = 1024

    def body(x_ref, g_ref, _token, u_ref, ut_ref):
        xf = x_ref[...]
        u = (xf * _rms_scale(xf) * g_ref[...]).astype(BF16)
        u_ref[...] = u
        ut_ref[...] = u.T

    return pl.pallas_call(
        body, name="input_norm", grid=(seq // tm,),
        in_specs=[pl.BlockSpec((tm, D), lambda i: (i, 0)), pl.BlockSpec((1, D), lambda i: (0, 0)), ANY],
        out_specs=[pl.BlockSpec((tm, D), lambda i: (i, 0)), pl.BlockSpec((D, tm), lambda i: (0, i))],
        out_shape=[pltpu.HBM((seq, D), BF16), pltpu.HBM((D, seq), BF16)],
        compiler_params=_params(("arbitrary",), 40),
    )(*_pinned(x, g_pre, token))


IN_PROJ_TN = IN_COLS // 4


def _in_proj_fwd(name, u, w_in, blocks, proj, token):
    seq = u.shape[0]
    tm, tn = 1024, IN_PROJ_TN

    def body(_blocks, u_ref, w_ref, _token, _proj_in, proj_ref):
        proj_ref[...] = _dot(u_ref[...], w_ref[...]).astype(BF16)

    return pl.pallas_call(
        body, name=name,
        grid_spec=pltpu.PrefetchScalarGridSpec(
            num_scalar_prefetch=1, grid=(seq // tm, blocks.shape[0]),
            in_specs=[pl.BlockSpec((tm, D), lambda i, j, b: (i, 0)), pl.BlockSpec((D, tn), lambda i, j, b: (0, b[j])),
                      ANY, ANY],
            out_specs=pl.BlockSpec((tm, tn), lambda i, j, b: (i, b[j]))),
        out_shape=pltpu.HBM((seq, IN_COLS), BF16),
        input_output_aliases={4: 0},
        compiler_params=_params(("arbitrary", "arbitrary"), 40),
    )(blocks, *_pinned(u, w_in, token, proj))


CONV_TM = 512
CONV_RS = 128


def _shifts():
    return [(b, [(a, 8 * a + b) for a in range(4) if 8 * a + b < TAPS]) for b in range(8)]


def _taps_looking_back(buf, row0, lanes, weight):
    acc = None
    for b, group in _shifts():
        part = None
        for a, s in group:
            term = weight(TAPS - 1 - s) * buf[pl.ds(row0 - 8 - 8 * a, CONV_RS + 8), lanes]
            part = term if part is None else part + term
        if b:
            part = pltpu.roll(part, b, 0)
        acc = part[8:, :] if acc is None else acc + part[8:, :]
    return acc


def _taps_looking_ahead(buf, row0, lanes, weight):
    acc = None
    for b, group in _shifts():
        part = None
        for a, s in group:
            term = weight(TAPS - 1 - s) * buf[pl.ds(row0 + 8 * a, CONV_RS + 8), lanes]
            part = term if part is None else part + term
        if b:
            part = pltpu.roll(part, CONV_RS + 8 - b, 0)
        acc = part[:CONV_RS, :] if acc is None else acc + part[:CONV_RS, :]
    return acc


def _layer_norm_parts(cv):
    mu = jnp.mean(cv, axis=-1, keepdims=True)
    cen = cv - mu
    rstd = lax.rsqrt(jnp.mean(cen * cen, axis=-1, keepdims=True) + LN_EPS)
    return cen * rstd, rstd


def _conv_fwd(proj, dw, dw_bias, ln_g, ln_b, w_conv_out, token):
    seq = proj.shape[0]
    tm = CONV_TM

    def body(a_ref, gate_ref, dw_ref, bias_ref, lg_ref, lb_ref, w_ref, _token, cv_ref, y_ref, ug):
        i = pl.program_id(0)

        @pl.when(i == 0)
        def _():
            ug[0:HALO, :] = jnp.zeros((HALO, D), F32)

        @pl.when(i > 0)
        def _():
            ug[0:HALO, :] = ug[tm:tm + HALO, :]
        ug[HALO:HALO + tm, :] = a_ref[...].astype(F32) * _sigmoid(gate_ref[...].astype(F32))

        def channel_block(cb, carry):
            lanes = pl.ds(pl.multiple_of(cb * 128, 128), 128)
            for r0 in range(0, tm, CONV_RS):
                taps = _taps_looking_back(ug, HALO + r0, lanes, lambda k: dw_ref[cb, k:k + 1, :])
                cv_ref[pl.ds(r0, CONV_RS), lanes] = taps + bias_ref[:, lanes]
            return carry
        lax.fori_loop(0, D // 128, channel_block, 0)

        n, _ = _layer_norm_parts(cv_ref[...])
        ln = n * lg_ref[...] + lb_ref[...]
        y_ref[...] = _dot((ln * _sigmoid(ln)).astype(BF16), w_ref[...]).astype(BF16)

    vec = pl.BlockSpec((1, D), lambda i: (0, 0))
    tile = pl.BlockSpec((tm, D), lambda i: (i, 0))
    return pl.pallas_call(
        body, name="conv_fwd", grid=(seq // tm,),
        in_specs=[pl.BlockSpec((tm, D), lambda i: (i, 0)), pl.BlockSpec((tm, D), lambda i: (i, 1)),
                  pl.BlockSpec((N_DEV, TAPS, 128), lambda i: (0, 0, 0)), vec, vec, vec,
                  pl.BlockSpec((D, D), lambda i: (0, 0)), ANY],
        out_specs=[tile, tile],
        out_shape=[pltpu.HBM((seq, D), F32), pltpu.HBM((seq, D), BF16)],
        scratch_shapes=[pltpu.VMEM((HALO + tm, D), F32)],
        compiler_params=_params(("arbitrary",), 32),
    )(*_pinned(proj, proj, dw, dw_bias, ln_g, ln_b, w_conv_out, token))


def _window_sums(rows, window, back):
    n = rows.shape[0]
    span = 1
    while span < window:
        rows = rows + pltpu.roll(rows, span if back else n - span, 0)
        span *= 2
    return rows


def _pool_counts(tile_index, tm, window):
    t = tile_index * tm + lax.broadcasted_iota(jnp.int32, (tm, 1), 0)
    return 1.0 / jnp.minimum(t + 1, window).astype(F32)


def _pool_merge_fwd(proj, y_conv, x, pool_w, pool_scale, w_pool_out, w_o, g_post, token):
    seq = proj.shape[0]
    tm = CONV_TM

    def body(p_ref, gc_ref, gp_ref, yc_ref, x_ref, pw_ref, ps_ref, wpo_ref, wo_ref, g_ref, _token,
             z_ref, zl_ref, yp_ref, mg_ref, o_ref, h1_ref, pbuf, zl_buf):
        i = pl.program_id(0)

        @pl.when(i == 0)
        def _():
            pbuf[0:HALO, :] = jnp.zeros((HALO, D), F32)

        @pl.when(i > 0)
        def _():
            pbuf[0:HALO, :] = pbuf[tm:tm + HALO, :]
        pbuf[HALO:HALO + tm, :] = p_ref[...].astype(F32)

        for g, window in enumerate(POOL_WINDOWS):
            lanes = pl.ds(g * PG, PG)
            acc = _window_sums(pbuf[:, lanes], window, back=True)[HALO:, :]
            zg = acc * _pool_counts(i, tm, window) - pbuf[pl.ds(HALO, tm), lanes]
            z_ref[:, lanes] = zg.astype(BF16)
            zl_buf[:, lanes] = _dot(zg.astype(BF16), pw_ref[g])
        zl = zl_buf[...]
        zl_ref[...] = zl.astype(BF16)
        y_pool = _dot((zl * ps_ref[...]).astype(BF16), wpo_ref[...])
        yp_ref[...] = y_pool.astype(BF16)
        merged = (_sigmoid(gc_ref[...].astype(F32)) * yc_ref[...].astype(F32)
                  + _sigmoid(gp_ref[...].astype(F32)) * y_pool).astype(BF16)
        mg_ref[...] = merged
        o = _dot(merged, wo_ref[...])
        o_ref[...] = o
        h1_ref[...] = x_ref[...] + o * _rms_scale(o) * g_ref[...]

    vec = pl.BlockSpec((1, D), lambda i: (0, 0))
    tile = pl.BlockSpec((tm, D), lambda i: (i, 0))
    mat = pl.BlockSpec((D, D), lambda i: (0, 0))
    return pl.pallas_call(
        body, name="pool_merge_fwd", grid=(seq // tm,),
        in_specs=[pl.BlockSpec((tm, D), lambda i: (i, 2)), pl.BlockSpec((tm, D), lambda i: (i, 3)),
                  pl.BlockSpec((tm, D), lambda i: (i, 4)), tile, tile,
                  pl.BlockSpec((4, PG, PG), lambda i: (0, 0, 0)), vec, mat, mat, vec, ANY],
        out_specs=[tile] * 6,
        out_shape=[pltpu.HBM((seq, D), dt) for dt in (BF16, BF16, BF16, BF16, F32, F32)],
        scratch_shapes=[pltpu.VMEM((HALO + tm, D), F32), pltpu.VMEM((tm, D), F32)],
        compiler_params=_params(("arbitrary",), 48),
    )(*_pinned(proj, proj, proj, y_conv, x, pool_w, pool_scale, w_pool_out, w_o, g_post, token))


def _mlp_fwd(h1, g_pre, w_ff1, w_ff2, g_post, target):
    seq = h1.shape[0]
    tm, tf = 1024, D_FF // N_DEV
    n_f = D_FF // tf

    def body(h1_ref, gpre_ref, w1_ref, w2_ref, gpost_ref, tgt_ref, v_ref, dm_ref, dh2_ref, sse_ref, ggrad_ref, macc):
        i, j = pl.program_id(0), pl.program_id(1)

        @pl.when(j == 0)
        def _():
            h = h1_ref[...]
            v_ref[...] = (h * _rms_scale(h) * gpre_ref[...]).astype(BF16)
        f = jnp.maximum(_dot(v_ref[...], w1_ref[...]), 0.0)
        part = _dot((f * f).astype(BF16), w2_ref[...])

        @pl.when(j == 0)
        def _():
            macc[...] = part

        @pl.when(j > 0)
        def _():
            macc[...] += part

        @pl.when(j == n_f - 1)
        def _():
            mo = macc[...]
            err = h1_ref[...] + mo * _rms_scale(mo) * gpost_ref[...] - tgt_ref[...]
            dh2 = err * (1.0 / D)
            dh2_ref[...] = dh2.astype(BF16)
            dm, ggrad = _rms_bwd(dh2, mo, gpost_ref[...])
            dm_ref[...] = dm.astype(BF16)
            _acc_out(ggrad_ref, i == 0, ggrad)
            _acc_out(sse_ref, i == 0, jnp.sum(jnp.sum(err * err, axis=1, keepdims=True), axis=0, keepdims=True))

    vec = pl.BlockSpec((1, D), lambda i, j: (0, 0))
    tile = pl.BlockSpec((tm, D), lambda i, j: (i, 0))
    return pl.pallas_call(
        body, name="mlp_fwd", grid=(seq // tm, n_f),
        in_specs=[tile, vec, pl.BlockSpec((D, tf), lambda i, j: (0, j)), pl.BlockSpec((tf, D), lambda i, j: (j, 0)), vec, tile],
        out_specs=[tile, tile, tile, pl.BlockSpec((1, 1), lambda i, j: (0, 0)), vec],
        out_shape=[pltpu.HBM((seq, D), BF16), pltpu.HBM((seq, D), BF16),
                   pltpu.HBM((seq, D), BF16), jax.ShapeDtypeStruct((1, 1), F32),
                   jax.ShapeDtypeStruct((1, D), F32)],
        scratch_shapes=[pltpu.VMEM((tm, D), F32)],
        compiler_params=_params(("arbitrary", "arbitrary"), 56),
    )(*_pinned(h1, g_pre, w_ff1, w_ff2, g_post, target))


def _mlp_bwd(v, dm, w_ff1, w_ff2):
    seq = v.shape[0]
    tm, tf = 1024, D_FF // N_DEV
    n_t = seq // tm

    def body(v_ref, dm_ref, w1_ref, w2_ref, dv_hbm, g1_ref, g2_ref, dv_acc, g1_acc, g2_acc, stage, sems):
        j, i = pl.program_id(0), pl.program_id(1)
        vt, dmt = v_ref[...], dm_ref[...]
        f = jnp.maximum(_dot(vt, w1_ref[...]), 0.0)
        df = (_dot_nt(dmt, w2_ref[...]) * (2.0 * f)).astype(BF16)
        rows = pl.ds(pl.multiple_of(i * tm, tm), tm)
        dv_part = _dot_nt(df, w1_ref[...])

        @pl.when(j == 0)
        def _():
            dv_acc[rows, :] = dv_part

        @pl.when(j > 0)
        def _():
            dv_acc[rows, :] += dv_part
        g1_part = _dot_tn(vt, df)
        g2_part = _dot_tn((f * f).astype(BF16), dmt)

        @pl.when(i == 0)
        def _():
            g1_acc[...] = g1_part
            g2_acc[...] = g2_part

        @pl.when(i > 0)
        def _():
            g1_acc[...] += g1_part
            g2_acc[...] += g2_part

        @pl.when(i == n_t - 1)
        def _():
            g1_ref[...] = g1_acc[...].astype(BF16)
            g2_ref[...] = g2_acc[...].astype(BF16)

        last_pass = j == D_FF // tf - 1

        def leaving(t):
            return pltpu.make_async_copy(stage, dv_hbm.at[pl.ds(pl.multiple_of(t * tm, tm), tm), :], sems.at[t])

        @pl.when(last_pass)
        def _():
            @pl.when(i > 0)
            def _():
                leaving(i - 1).wait()
            stage[...] = dv_acc[rows, :].astype(BF16)
            leaving(i).start()

        @pl.when(jnp.logical_and(last_pass, i == n_t - 1))
        def _():
            leaving(n_t - 1).wait()

    tile = pl.BlockSpec((tm, D), lambda j, i: (i, 0))
    return pl.pallas_call(
        body, name="mlp_bwd", grid=(D_FF // tf, n_t),
        in_specs=[tile, tile, pl.BlockSpec((D, tf), lambda j, i: (0, j)), pl.BlockSpec((tf, D), lambda j, i: (j, 0))],
        out_specs=[pl.BlockSpec(memory_space=pl.ANY), pl.BlockSpec((D, tf), lambda j, i: (0, j)),
                   pl.BlockSpec((tf, D), lambda j, i: (j, 0))],
        out_shape=[pltpu.HBM((seq, D), BF16), pltpu.HBM((D, D_FF), BF16),
                   pltpu.HBM((D_FF, D), BF16)],
        scratch_shapes=[pltpu.VMEM((seq, D), F32), pltpu.VMEM((D, tf), F32), pltpu.VMEM((tf, D), F32),
                        pltpu.VMEM((tm, D), BF16), pltpu.SemaphoreType.DMA((n_t,))],
        compiler_params=_params(("arbitrary", "arbitrary"), 52),
    )(*_pinned(v, dm, w_ff1, w_ff2))


def _merge_bwd(dh2, dv, h1, g_mlp_pre, o, g_mix_post, w_o, merged, proj, y_conv, y_pool, token):
    seq = dh2.shape[0]
    tm = 256
    n_t = seq // tm

    def body(dh2_ref, dv_ref, h1_ref, gpre_ref, o_ref, gpost_ref, wo_ref, mg_ref, gc_ref, gp_ref, yc_ref, yp_ref, _token,
             dh1_ref, dyc_ref, dyp_ref, dg_ref, gwo_ref, ggpre_ref, ggpost_ref, gwo_acc):
        i = pl.program_id(0)
        dnorm, ggpre = _rms_bwd(dv_ref[...].astype(F32), h1_ref[...], gpre_ref[...])
        dh1 = dh2_ref[...].astype(F32) + dnorm
        dh1_ref[...] = dh1
        do, ggpost = _rms_bwd(dh1, o_ref[...], gpost_ref[...])
        do = do.astype(BF16)
        _acc_out(ggpre_ref, i == 0, ggpre)
        _acc_out(ggpost_ref, i == 0, ggpost)
        _acc_out(gwo_acc, i == 0, _dot_tn(mg_ref[...], do))
        dmerged = _dot_nt(do, wo_ref[...])
        sc, sp = _sigmoid(gc_ref[...].astype(F32)), _sigmoid(gp_ref[...].astype(F32))
        dyc_ref[...] = (dmerged * sc).astype(BF16)
        dyp_ref[...] = (dmerged * sp).astype(BF16)
        dg_ref[:, 0:D] = (dmerged * yc_ref[...].astype(F32) * (sc * (1.0 - sc))).astype(BF16)
        dg_ref[:, D:2 * D] = (dmerged * yp_ref[...].astype(F32) * (sp * (1.0 - sp))).astype(BF16)

        @pl.when(i == n_t - 1)
        def _():
            gwo_ref[...] = gwo_acc[...].astype(BF16)

    vec = pl.BlockSpec((1, D), lambda i: (0, 0))
    tile = pl.BlockSpec((tm, D), lambda i: (i, 0))
    mat = pl.BlockSpec((D, D), lambda i: (0, 0))
    return pl.pallas_call(
        body, name="merge_bwd", grid=(n_t,),
        in_specs=[tile, tile, tile, vec, tile, vec, mat, tile,
                  pl.BlockSpec((tm, D), lambda i: (i, 3)), pl.BlockSpec((tm, D), lambda i: (i, 4)), tile, tile, ANY],
        out_specs=[tile, tile, tile, pl.BlockSpec((tm, 2 * D), lambda i: (i, 0)), mat, vec, vec],
        out_shape=[pltpu.HBM((seq, D), F32), pltpu.HBM((seq, D), BF16),
                   pltpu.HBM((seq, D), BF16), pltpu.HBM((seq, 2 * D), BF16),
                   pltpu.HBM((D, D), BF16), jax.ShapeDtypeStruct((1, D), F32),
                   jax.ShapeDtypeStruct((1, D), F32)],
        scratch_shapes=[pltpu.VMEM((D, D), F32)],
        compiler_params=_params(("arbitrary",), 48),
    )(*_pinned(dh2, dv, h1, g_mlp_pre, o, g_mix_post, w_o, merged, proj, proj, y_conv, y_pool, token))


def _pool_bwd(dy_pool, zl, z, pool_w, pool_scale, w_pool_out, token):
    seq = dy_pool.shape[0]
    tm = CONV_TM
    n_t = seq // tm

    def body(dy_ref, zl_ref, z_ref, pw_ref, ps_ref, wpo_ref, _token,
             dp_ref, gwpo_ref, gpw_ref, gps_ref, qbuf, gwpo_acc, gpw_acc):
        i = pl.program_id(0)
        tile_index = n_t - 1 - i
        first = i == 0
        dy = dy_ref[...]
        zl = zl_ref[...].astype(F32)
        dzs = _dot_nt(dy, wpo_ref[...])
        _acc_out(gwpo_acc, first, _dot_tn((zl * ps_ref[...]).astype(BF16), dy))
        _acc_out(gps_ref, first, jnp.sum(dzs * zl, axis=0, keepdims=True))
        dzl = (dzs * ps_ref[...]).astype(BF16)

        @pl.when(first)
        def _():
            qbuf[tm:tm + HALO, :] = jnp.zeros((HALO, D), F32)

        @pl.when(jnp.logical_not(first))
        def _():
            qbuf[tm:tm + HALO, :] = qbuf[0:HALO, :]

        dzs_list = []
        for g, window in enumerate(POOL_WINDOWS):
            lanes = pl.ds(g * PG, PG)
            dzl_g = dzl[:, g * PG:(g + 1) * PG]
            dz = _dot_nt(dzl_g, pw_ref[g])
            _acc_out(gpw_acc.at[g], first, _dot_tn(z_ref[:, lanes], dzl_g))
            qbuf[pl.ds(0, tm), lanes] = dz * _pool_counts(tile_index, tm, window)
            dzs_list.append(dz)
        for g, window in enumerate(POOL_WINDOWS):
            lanes = pl.ds(g * PG, PG)
            acc = _window_sums(qbuf[:, lanes], window, back=False)[:tm, :]
            dp_ref[:, lanes] = (acc - dzs_list[g]).astype(BF16)

        @pl.when(i == n_t - 1)
        def _():
            gwpo_ref[...] = gwpo_acc[...].astype(BF16)
            gpw_ref[...] = gpw_acc[...].astype(BF16)

    vec = pl.BlockSpec((1, D), lambda i: (0, 0))
    tile = pl.BlockSpec((tm, D), lambda i: (n_t - 1 - i, 0))
    mat = pl.BlockSpec((D, D), lambda i: (0, 0))
    pw = pl.BlockSpec((4, PG, PG), lambda i: (0, 0, 0))
    return pl.pallas_call(
        body, name="pool_bwd", grid=(n_t,),
        in_specs=[tile, tile, tile, pw, vec, mat, ANY],
        out_specs=[tile, mat, pw, vec],
        out_shape=[pltpu.HBM((seq, D), BF16), pltpu.HBM((D, D), BF16),
                   pltpu.HBM((4, PG, PG), BF16), jax.ShapeDtypeStruct((1, D), F32)],
        scratch_shapes=[pltpu.VMEM((tm + HALO, D), F32), pltpu.VMEM((D, D), F32), pltpu.VMEM((4, PG, PG), F32)],
        compiler_params=_params(("arbitrary",), 40),
    )(*_pinned(dy_pool, zl, z, pool_w, pool_scale, w_pool_out, token))


def _conv_bwd(dy_conv, cv, proj, dw, ln_g, ln_b, w_conv_out, token):
    seq = dy_conv.shape[0]
    tm = CONV_TM // 2
    n_t = seq // tm
    halo_blocks = tm // HALO

    def body(dy_ref, cv_ref, a_ref, gate_ref, ah_ref, gh_ref, dw_ref, lg_ref, lb_ref, w_ref, _token,
             dglu_ref, gw_ref, gdw_ref, gbias_ref, glg_ref, glb_ref, ug, dcv, dug, gw_acc):
        i = pl.program_id(0)
        tile_index = n_t - 1 - i
        first = i == 0
        dy = dy_ref[...]
        n, rstd = _layer_norm_parts(cv_ref[...])
        ln = n * lg_ref[...] + lb_ref[...]
        sg = _sigmoid(ln)
        _acc_out(gw_acc, first, _dot_tn((ln * sg).astype(BF16), dy))
        dln = _dot_nt(dy, w_ref[...]) * (sg * (1.0 + ln * (1.0 - sg)))
        _acc_out(glg_ref, first, jnp.sum(dln * n, axis=0, keepdims=True))
        _acc_out(glb_ref, first, jnp.sum(dln, axis=0, keepdims=True))
        dn = dln * lg_ref[...]
        dcv_tile = rstd * (dn - jnp.mean(dn, axis=-1, keepdims=True) - n * jnp.mean(dn * n, axis=-1, keepdims=True))
        _acc_out(gbias_ref, first, jnp.sum(dcv_tile, axis=0, keepdims=True))

        @pl.when(first)
        def _():
            dcv[tm:tm + HALO, :] = jnp.zeros((HALO, D), F32)

        @pl.when(jnp.logical_not(first))
        def _():
            dcv[tm:tm + HALO, :] = dcv[0:HALO, :]
        dcv[0:tm, :] = dcv_tile

        a, gate = a_ref[...].astype(F32), gate_ref[...].astype(F32)
        sgate = _sigmoid(gate)
        ug[HALO:HALO + tm, :] = a * sgate
        before = jnp.where(tile_index > 0, 1.0, 0.0)
        ug[0:HALO, :] = ah_ref[...].astype(F32) * _sigmoid(gh_ref[...].astype(F32)) * before

        @pl.when(first)
        def _():
            gdw_ref[...] = jnp.zeros((N_DEV, TAPS + 1, 128), F32)

        def channel_block(cb, carry):
            lanes = pl.ds(pl.multiple_of(cb * 128, 128), 128)
            for r0 in range(0, tm, CONV_RS):
                dug[pl.ds(r0, CONV_RS), lanes] = _taps_looking_ahead(dcv, r0, lanes, lambda k: dw_ref[cb, k:k + 1, :])
            for b, group in _shifts():
                sums = [jnp.zeros((8, 128), F32) for _ in group]
                for r0 in range(0, tm, CONV_RS):
                    window = ug[pl.ds(r0, CONV_RS + HALO), lanes]
                    if b:
                        window = pltpu.roll(window, b, 0)
                    d = dcv[pl.ds(r0, CONV_RS), lanes]
                    for n_a, (a, s) in enumerate(group):
                        prod = d * window[HALO - 8 * a:HALO - 8 * a + CONV_RS, :]
                        sums[n_a] = sums[n_a] + jnp.sum(prod.reshape(CONV_RS // 8, 8, 128), axis=0)
                for n_a, (a, s) in enumerate(group):
                    k = TAPS - 1 - s
                    gdw_ref[cb, k:k + 1, :] += jnp.sum(sums[n_a], axis=0, keepdims=True)
            return carry
        lax.fori_loop(0, D // 128, channel_block, 0)

        d_ug = dug[...]
        dglu_ref[:, 0:D] = (d_ug * sgate).astype(BF16)
        dglu_ref[:, D:2 * D] = (d_ug * a * (sgate * (1.0 - sgate))).astype(BF16)

        @pl.when(i == n_t - 1)
        def _():
            gw_ref[...] = gw_acc[...].astype(BF16)

    def halo_index(col):
        return lambda i: (jnp.maximum((n_t - 1 - i) * halo_blocks - 1, 0), col)

    vec = pl.BlockSpec((1, D), lambda i: (0, 0))
    tile = pl.BlockSpec((tm, D), lambda i: (n_t - 1 - i, 0))
    mat = pl.BlockSpec((D, D), lambda i: (0, 0))
    dwspec = pl.BlockSpec((N_DEV, TAPS, 128), lambda i: (0, 0, 0))
    return pl.pallas_call(
        body, name="conv_bwd", grid=(n_t,),
        in_specs=[tile, tile, pl.BlockSpec((tm, D), lambda i: (n_t - 1 - i, 0)), pl.BlockSpec((tm, D), lambda i: (n_t - 1 - i, 1)),
                  pl.BlockSpec((HALO, D), halo_index(0)), pl.BlockSpec((HALO, D), halo_index(1)), dwspec, vec, vec, mat,
                  ANY],
        out_specs=[pl.BlockSpec((tm, 2 * D), lambda i: (n_t - 1 - i, 0)), mat,
                   pl.BlockSpec((N_DEV, TAPS + 1, 128), lambda i: (0, 0, 0)), vec, vec, vec],
        out_shape=[pltpu.HBM((seq, 2 * D), BF16), pltpu.HBM((D, D), BF16),
                   pltpu.HBM((N_DEV, TAPS + 1, 128), F32), jax.ShapeDtypeStruct((1, D), F32),
                   jax.ShapeDtypeStruct((1, D), F32), jax.ShapeDtypeStruct((1, D), F32)],
        scratch_shapes=[pltpu.VMEM((HALO + tm, D), F32), pltpu.VMEM((tm + HALO, D), F32), pltpu.VMEM((tm, D), F32),
                        pltpu.VMEM((D, D), F32)],
        compiler_params=_params(("arbitrary",), 48),
    )(*_pinned(dy_conv, cv, proj, proj, proj, proj, dw, ln_g, ln_b, w_conv_out, token))


def _in_proj_bwd_x(d_glu, dp, dgates, w_in, x, g_pre, dh1, token):
    seq = x.shape[0]
    tm = 512

    def body(dglu_ref, dp_ref, dg_ref, w_ref, x_ref, g_ref, dh1_ref, _token, dx_ref, gg_ref):
        du = _dot_nt(dglu_ref[...], w_ref[:, 0:2 * D])
        du += _dot_nt(dp_ref[...], w_ref[:, 2 * D:3 * D])
        du += _dot_nt(dg_ref[...], w_ref[:, 3 * D:5 * D])
        dnorm, gg = _rms_bwd(du, x_ref[...], g_ref[...])
        dx_ref[...] = dh1_ref[...] + dnorm
        _acc_out(gg_ref, pl.program_id(0) == 0, gg)

    vec = pl.BlockSpec((1, D), lambda i: (0, 0))
    tile = pl.BlockSpec((tm, D), lambda i: (i, 0))
    wide = pl.BlockSpec((tm, 2 * D), lambda i: (i, 0))
    return pl.pallas_call(
        body, name="in_proj_bwd_x", grid=(seq // tm,),
        in_specs=[wide, tile, wide, pl.BlockSpec((D, IN_COLS), lambda i: (0, 0)), tile, vec, tile, ANY],
        out_specs=[tile, vec],
        out_shape=[pltpu.HBM((seq, D), F32), jax.ShapeDtypeStruct((1, D), F32)],
        compiler_params=_params(("arbitrary",), 48),
    )(*_pinned(d_glu, dp, dgates, w_in, x, g_pre, dh1, token))


def _in_proj_bwd_w(u_t, d_glu, dp, dgates, token):
    seq = u_t.shape[1]
    tm = 2048
    n_t = seq // tm

    def body(u_ref, dglu_ref, dp_ref, dg_ref, _token, out_ref, acc):
        b, i = pl.program_id(0), pl.program_id(1)
        ut = u_ref[...]

        def add(d_ref):
            _acc_out(acc, i == 0, _dot(ut, d_ref[...]))

        pl.when(b < 2)(lambda: add(dglu_ref))
        pl.when(b == 2)(lambda: add(dp_ref))
        pl.when(b > 2)(lambda: add(dg_ref))

        @pl.when(i == n_t - 1)
        def _():
            out_ref[...] = acc[...].astype(BF16)

    return pl.pallas_call(
        body, name="in_proj_bwd_w", grid=(IN_COLS // D, n_t),
        in_specs=[pl.BlockSpec((D, tm), lambda b, i: (0, i)),
                  pl.BlockSpec((tm, D), lambda b, i: (jnp.where(b < 2, i, 0), jnp.minimum(b, 1))),
                  pl.BlockSpec((tm, D), lambda b, i: (jnp.where(b == 2, i, 0), 0)),
                  pl.BlockSpec((tm, D), lambda b, i: (jnp.where(b > 2, i, 0), jnp.maximum(b - 3, 0))), ANY],
        out_specs=pl.BlockSpec((D, D), lambda b, i: (0, b)),
        out_shape=pltpu.HBM((D, IN_COLS), BF16),
        scratch_shapes=[pltpu.VMEM((D, D), F32)],
        compiler_params=_params(("arbitrary", "arbitrary"), 52),
    )(*_pinned(u_t, d_glu, dp, dgates, token))


VEC_NAMES = ("mix_pre_g", "dw_bias", "conv_ln_g", "conv_ln_b", "pool_scale", "mix_post_g", "mlp_pre_g", "mlp_post_g")
WEIGHT_ORDER = ("mix_pre_g", "w_in", "dw_kernel", "dw_bias", "conv_ln_g", "conv_ln_b", "w_conv_out", "pool_w",
                "pool_scale", "w_pool_out", "w_o", "mix_post_g", "mlp_pre_g", "w_ff1", "w_ff2", "mlp_post_g")


def _step(x, loss_target, w, m, v):
    row = lambda a: a.reshape(1, D)
    names = [s[0] for s in SHARDED]

    gather_groups = (("w_in",), ("w_conv_out", "dw_kernel"), ("pool_w", "w_pool_out", "w_o"), ("w_ff1", "w_ff2"))
    legs = [GATHER_ICI, GATHER_D2D]
    first_level, token = _exchange_start(
        "gather_start_w_in", [(gather_groups[0], None, _stage_shards(gather_groups[0], [w["w_in"]], x))], legs)
    later = [n for g in gather_groups[1:] for n in g]
    seeded = dict(zip(later, _stage_shards(later, [w[n] for n in later], token)))
    rest, token = _exchange_start("gather_start", [(g, None, [seeded[n] for n in g]) for g in gather_groups[1:]], legs)
    first_level = first_level + rest
    forwarded, full = {}, {}

    def forward(k, after):
        group, (ici_sems, d2d_sems), _, landed = first_level[k]
        _, landed = _exchange_wait("gather_ici_" + group[0], group, None, landed, [(ici_sems, GATHER_ICI)], after)
        (second,), tok = _exchange_start("gather_forward_" + group[0], [(group, None, landed)], [GATHER_FORWARD])
        forwarded[k] = (second[3], [(d2d_sems, GATHER_D2D), (second[1][0], GATHER_FORWARD)])
        return tok

    def gathered(k, after):
        group = gather_groups[k]
        landed, waits = forwarded[k]
        _, arrays = _exchange_wait("gather_wait_" + group[0], group, None, landed, waits, after)
        full.update(zip(group, arrays))

    u, u_t = _input_norm(x, row(w["mix_pre_g"]), token)
    group, (ici_sems, d2d_sems), _, landed = first_level[0]
    _, landed = _exchange_wait("gather_d2d_w_in", group, None, landed, [(d2d_sems, GATHER_D2D)], u)
    home = jnp.reshape(_my_slot() // 2, (1,)).astype(jnp.int32)
    proj = _in_proj_fwd("in_proj_fwd_home", u, landed[0], home, lax.empty((x.shape[0], IN_COLS), BF16), u)
    _, landed = _exchange_wait("gather_ici_w_in", group, None, landed, [(ici_sems, GATHER_ICI)], proj)
    (second,), tok = _exchange_start("gather_forward_w_in", [(group, None, landed)], [GATHER_FORWARD])
    (full["w_in"],) = _exchange_wait("gather_wait_w_in", group, None, second[3], [(second[1][0], GATHER_FORWARD)], tok)[1]
    tok = forward(1, full["w_in"])
    away = jnp.stack([home[0] ^ 1, home[0] ^ 2, home[0] ^ 3]).astype(jnp.int32)
    proj = _in_proj_fwd("in_proj_fwd", u, full["w_in"], away, proj, tok)
    gathered(1, proj)
    tok = forward(2, proj)
    cv, y_conv = _conv_fwd(proj, full["dw_kernel"], row(w["dw_bias"]), row(w["conv_ln_g"]), row(w["conv_ln_b"]),
                           full["w_conv_out"], tok)
    gathered(2, y_conv)
    tok = forward(3, y_conv)
    z, zl, y_pool, merged, o, h1 = _pool_merge_fwd(proj, y_conv, x, full["pool_w"], row(w["pool_scale"]),
                                                   full["w_pool_out"], full["w_o"], row(w["mix_post_g"]), tok)
    gathered(3, h1)
    vv, dm, dh2, sse, g_mlp_post = _mlp_fwd(h1, row(w["mlp_pre_g"]), full["w_ff1"], full["w_ff2"],
                                            row(w["mlp_post_g"]), loss_target)

    shard_of = {name: shard for name, _, shard, _ in SHARDED}

    def scatter_start(tag, group, grads):
        landings = [lax.empty((N_DEV - 1,) + shard_of[n], g.dtype) for n, g in zip(group, grads)]
        (handle,), tok = _exchange_start("scatter_start_" + tag, [(group, grads, landings)], [SCATTER])
        return handle, tok

    dv, g_ff1, g_ff2 = _mlp_bwd(vv, dm, full["w_ff1"], full["w_ff2"])
    h_ff, tok_ff = scatter_start("ff", ("w_ff1", "w_ff2"), [g_ff1, g_ff2])
    dh1, dy_conv, dy_pool, dgates, g_wo, g_mlp_pre, g_mix_post = _merge_bwd(
        dh2, dv, h1, row(w["mlp_pre_g"]), o, row(w["mix_post_g"]), full["w_o"], merged, proj, y_conv, y_pool, tok_ff)
    dp, g_wpo, g_pw, g_pool_scale = _pool_bwd(dy_pool, zl, z, full["pool_w"], row(w["pool_scale"]), full["w_pool_out"],
                                              dh1)
    h_pool, tok_pool = scatter_start("pool", ("w_o", "w_pool_out", "pool_w"), [g_wo, g_wpo, g_pw])
    d_glu, g_wco, g_dw, g_bias, g_ln_g, g_ln_b = _conv_bwd(dy_conv, cv, proj, full["dw_kernel"], row(w["conv_ln_g"]),
                                                            row(w["conv_ln_b"]), full["w_conv_out"], tok_pool)
    h_conv, tok_conv = scatter_start("conv", ("w_conv_out", "dw_kernel"), [g_wco, g_dw[:, :TAPS, :]])
    g_win = _in_proj_bwd_w(u_t, d_glu, dp, dgates, tok_conv)
    h_in, tok_in = scatter_start("in", ("w_in",), [g_win])
    grad_x, g_mix_pre = _in_proj_bwd_x(d_glu, dp, dgates, full["w_in"], x, row(w["mix_pre_g"]), dh1, tok_in)

    out = {}
    slot = jnp.reshape(_my_slot(), (1,)).astype(jnp.int32)
    after = grad_x
    for tag, (group, (sems,), grads, landings) in (("ff", h_ff), ("pool", h_pool), ("conv", h_conv), ("in", h_in)):
        if tag == "in":
            vec_parts = after = _vec_exchange(
                [g_mix_pre, g_bias, g_ln_g, g_ln_b, g_pool_scale, g_mix_post, g_mlp_pre, g_mlp_post,
                 jnp.broadcast_to(sse, (1, D))], after)
        mine, landed = _exchange_wait("scatter_wait_" + tag, group, grads, landings, [(sems, SCATTER)], after)
        for name, own, parts in zip(group, mine, landed):
            out[name] = _adamw(name, own, parts, w[name], m[name], v[name], slot, after)
            after = out[name][0]
    stack = lambda d: jnp.stack([d[n] for n in VEC_NAMES], axis=0)
    *res, loss_row = _adamw_vectors(vec_parts, stack(w), stack(m), stack(v))
    for k, name in enumerate(VEC_NAMES):
        out[name] = [r[k] for r in res]
    return loss_row[0, 0], grad_x, out


def kernel(x, mix_pre_g, w_in, dw_kernel, dw_bias, conv_ln_g, conv_ln_b, w_conv_out, pool_w, pool_scale, w_pool_out, w_o, mix_post_g, mlp_pre_g, w_ff1, w_ff2, mlp_post_g, loss_target, m_mix_pre_g, m_w_in, m_dw_kernel, m_dw_bias, m_conv_ln_g, m_conv_ln_b, m_w_conv_out, m_pool_w, m_pool_scale, m_w_pool_out, m_w_o, m_mix_post_g, m_mlp_pre_g, m_w_ff1, m_w_ff2, m_mlp_post_g, v_mix_pre_g, v_w_in, v_dw_kernel, v_dw_bias, v_conv_ln_g, v_conv_ln_b, v_w_conv_out, v_pool_w, v_pool_scale, v_w_pool_out, v_w_o, v_mix_post_g, v_mlp_pre_g, v_w_ff1, v_w_ff2, v_mlp_post_g):
    w = dict(mix_pre_g=mix_pre_g, w_in=w_in, dw_kernel=dw_kernel, dw_bias=dw_bias, conv_ln_g=conv_ln_g, conv_ln_b=conv_ln_b,
             w_conv_out=w_conv_out, pool_w=pool_w, pool_scale=pool_scale, w_pool_out=w_pool_out, w_o=w_o,
             mix_post_g=mix_post_g, mlp_pre_g=mlp_pre_g, w_ff1=w_ff1, w_ff2=w_ff2, mlp_post_g=mlp_post_g)
    m = dict(mix_pre_g=m_mix_pre_g, w_in=m_w_in, dw_kernel=m_dw_kernel, dw_bias=m_dw_bias, conv_ln_g=m_conv_ln_g,
             conv_ln_b=m_conv_ln_b, w_conv_out=m_w_conv_out, pool_w=m_pool_w, pool_scale=m_pool_scale,
             w_pool_out=m_w_pool_out, w_o=m_w_o, mix_post_g=m_mix_post_g, mlp_pre_g=m_mlp_pre_g, w_ff1=m_w_ff1,
             w_ff2=m_w_ff2, mlp_post_g=m_mlp_post_g)
    v = dict(mix_pre_g=v_mix_pre_g, w_in=v_w_in, dw_kernel=v_dw_kernel, dw_bias=v_dw_bias, conv_ln_g=v_conv_ln_g,
             conv_ln_b=v_conv_ln_b, w_conv_out=v_w_conv_out, pool_w=v_pool_w, pool_scale=v_pool_scale,
             w_pool_out=v_w_pool_out, w_o=v_w_o, mix_post_g=v_mix_post_g, mlp_pre_g=v_mlp_pre_g, w_ff1=v_w_ff1,
             w_ff2=v_w_ff2, mlp_post_g=v_mlp_post_g)
    seq = x.shape[1]
    loss, grad_x, out = _step(x.reshape(seq, D), loss_target.reshape(seq, D), w, m, v)
    grads, deltas, new_m, new_v = ([out[n][k] for n in WEIGHT_ORDER] for k in range(4))
    return (loss, grad_x.reshape(x.shape), *grads, *deltas, *new_m, *new_v)
```

```python
import collections

import jax
import jax.numpy as jnp
from jax import lax
from jax.experimental import pallas as pl
from jax.experimental.pallas import tpu as pltpu

D = 1024
D_FF = 4 * D
IN_COLS = 5 * D
TAPS = 31
HALO = 32
POOL_WINDOWS = (2, 4, 8, 16)
PG = D // 4
N_DEV = 8
RMS_EPS = 1e-6
LN_EPS = 1e-5
ADAM_LR, ADAM_B1, ADAM_B2, ADAM_EPS, ADAM_WD, ADAM_STEP = 0.001, 0.9, 0.999, 1e-08, 0.01, 10

BF16 = jnp.bfloat16
F32 = jnp.float32
MIB = 1 << 20
MESH = pl.DeviceIdType.MESH


def _params(sem, vmem_mib):
    return pltpu.CompilerParams(dimension_semantics=sem, vmem_limit_bytes=vmem_mib * MIB)


def _dot(a, b):
    return jnp.dot(a, b, preferred_element_type=F32)


def _dot_nt(a, b):
    return lax.dot_general(a, b, (((1,), (1,)), ((), ())), preferred_element_type=F32)


def _dot_tn(a, b):
    return lax.dot_general(a, b, (((0,), (0,)), ((), ())), preferred_element_type=F32)


def _rms_scale(x):
    return lax.rsqrt(jnp.mean(x * x, axis=-1, keepdims=True) + RMS_EPS)


def _rms_bwd(dy, x, g):
    xn = x * _rms_scale(x)
    dn = dy * g
    dx = _rms_scale(x) * (dn - xn * jnp.mean(dn * xn, axis=-1, keepdims=True))
    return dx, jnp.sum(dy * xn, axis=0, keepdims=True)


def _sigmoid(x):
    return jax.nn.sigmoid(x)


def _acc_out(ref, first, value):
    @pl.when(first)
    def _():
        ref[...] = value

    @pl.when(jnp.logical_not(first))
    def _():
        ref[...] += value


def _my_slot():
    return 4 * lax.axis_index("x") + 2 * lax.axis_index("y") + lax.axis_index("c")


def _peer(mask):
    x, y, c = lax.axis_index("x"), lax.axis_index("y"), lax.axis_index("c")
    return (x ^ ((mask >> 2) & 1), y ^ ((mask >> 1) & 1), c ^ (mask & 1))


def _cols(width):
    return lambda ref, slot: ref.at[:, pl.ds(pl.multiple_of(slot * width, 128), width)]


def _rows(height):
    return lambda ref, slot: ref.at[pl.ds(pl.multiple_of(slot * height, 8), height), :]


def _lead(ref, slot):
    return ref.at[slot]


def _pool_rows(ref, slot):
    return ref.at[:, pl.ds(pl.multiple_of(slot * (PG // N_DEV), 8), PG // N_DEV), :]


SHARDED = (
    ("w_in", (D, IN_COLS), (D, IN_COLS // N_DEV), _cols(IN_COLS // N_DEV)),
    ("w_ff1", (D, D_FF), (D, D_FF // N_DEV), _cols(D_FF // N_DEV)),
    ("w_ff2", (D_FF, D), (D_FF // N_DEV, D), _rows(D_FF // N_DEV)),
    ("w_conv_out", (D, D), (D // N_DEV, D), _rows(D // N_DEV)),
    ("w_pool_out", (D, D), (D // N_DEV, D), _rows(D // N_DEV)),
    ("w_o", (D, D), (D // N_DEV, D), _rows(D // N_DEV)),
    ("pool_w", (4, PG, PG), (4, PG // N_DEV, PG), _pool_rows),
    ("dw_kernel", (N_DEV, TAPS, D // N_DEV), (TAPS, D // N_DEV), _lead),
)
N_SHARDED = len(SHARDED)


SHARD_AT = {name: at for name, _, _, at in SHARDED}
HBM = pl.BlockSpec(memory_space=pltpu.HBM)
SEM = pl.BlockSpec(memory_space=pltpu.SEMAPHORE)
ANY = pl.BlockSpec(memory_space=pl.ANY)
EFFECT = pltpu.SideEffectType.DATAFLOW_SIDE_EFFECTING


def _in_hbm(a):
    return pltpu.with_memory_space_constraint(a, pltpu.HBM)


def _pinned(*arrays):
    return [_in_hbm(a) for a in arrays]


def _stage_shards(names, shards, token):
    n = len(names)
    specs = [s for name in names for s in SHARDED if s[0] == name]
    dtypes = [F32 if name == "dw_kernel" else BF16 for name in names]

    def body(*refs):
        ins = refs[:n]
        fulls = refs[n + 1:2 * n + 1]
        raw = refs[2 * n + 1:3 * n + 1]
        stage = refs[3 * n + 1:4 * n + 1]
        in_sems, out_sems = refs[4 * n + 1:]
        me = _my_slot()
        loads = [pltpu.make_async_copy(ins[a], raw[a], in_sems.at[a]) for a in range(n)]
        for cp in loads:
            cp.start()
        stores = []
        for a, (_, _, _, at) in enumerate(specs):
            loads[a].wait()
            stage[a][...] = raw[a][...].astype(dtypes[a])
            cp = pltpu.make_async_copy(stage[a], at(fulls[a], me), out_sems.at[a])
            cp.start()
            stores.append(cp)
        for cp in stores:
            cp.wait()

    return pl.pallas_call(
        body, name="stage_" + names[0],
        out_shape=[pltpu.HBM(full, dt) for (_, full, _, _), dt in zip(specs, dtypes)],
        in_specs=[ANY] * (n + 1),
        out_specs=[ANY] * n,
        scratch_shapes=[pltpu.VMEM(shard, F32) for _, _, shard, _ in specs]
        + [pltpu.VMEM(shard, dt) for (_, _, shard, _), dt in zip(specs, dtypes)]
        + [pltpu.SemaphoreType.DMA((n,)), pltpu.SemaphoreType.DMA((n,))],
        compiler_params=pltpu.CompilerParams(vmem_limit_bytes=40 * MIB),
    )(*_pinned(*shards, token))


Leg = collections.namedtuple("Leg", "routes src_of dst_of src_is_land")


def _sem_index(k, m):
    return k * (N_DEV - 1) + m - 1


def _exchange_start(name, groups, legs):
    sizes = [len(g[0]) for g in groups]
    names = [nm for g in groups for nm in g[0]]
    srcs = [s for g in groups if g[1] is not None for s in g[1]]
    lands = [l for g in groups for l in g[2]]
    n_src, n, n_g, n_l = len(srcs), len(names), len(groups), len(legs)

    def body(*refs):
        src_refs, land_refs = list(refs[:n_src]), refs[n_src:n_src + n]
        sems = refs[n_src + n:n_src + n + 2 * n_g * n_l]
        token = refs[-1]
        me = _my_slot()
        first = 0
        for g, size in enumerate(sizes):
            own_src = [src_refs.pop(0) for _ in range(size)] if groups[g][1] is not None else None
            for li, leg in enumerate(legs):
                send, recv = sems[2 * (g * n_l + li)], sems[2 * (g * n_l + li) + 1]
                for m, via in leg.routes:
                    for k in range(size):
                        land = land_refs[first + k]
                        src = land if leg.src_is_land else own_src[k]
                        pltpu.make_async_remote_copy(
                            src_ref=leg.src_of(names[first + k], src, me, m), dst_ref=leg.dst_of(names[first + k], land, me, m),
                            send_sem=send.at[_sem_index(k, m)], recv_sem=recv.at[_sem_index(k, m)],
                            device_id=_peer(via), device_id_type=MESH).start()
            first += size
        token[...] = jnp.zeros_like(token)

    sem_shapes = [pltpu.SemaphoreType.DMA((size * (N_DEV - 1),)) for size in sizes for _ in range(2 * n_l)]
    n_sem = len(sem_shapes)
    outs = pl.pallas_call(
        body, name=name,
        out_shape=sem_shapes + [pltpu.HBM(a.shape, a.dtype) for a in srcs + lands] + [jax.ShapeDtypeStruct((8, 128), F32)],
        in_specs=[HBM] * (n_src + n),
        out_specs=[SEM] * n_sem + [HBM] * (n_src + n) + [pl.BlockSpec(memory_space=pltpu.VMEM)],
        input_output_aliases={k: n_sem + k for k in range(n_src + n)},
        compiler_params=pltpu.CompilerParams(has_side_effects=EFFECT),
    )(*[_in_hbm(a) for a in srcs + lands])
    sems, thru, token = outs[:n_sem], list(outs[n_sem:-1]), outs[-1]
    src_thru, land_thru = thru[:n_src], thru[n_src:]
    handles, first = [], 0
    for g, size in enumerate(sizes):
        pairs = [(sems[2 * (g * n_l + li)], sems[2 * (g * n_l + li) + 1]) for li in range(n_l)]
        mine = [src_thru.pop(0) for _ in range(size)] if groups[g][1] is not None else None
        handles.append((groups[g][0], pairs, mine, land_thru[first:first + size]))
        first += size
    return handles, token


def _exchange_wait(name, names, srcs, lands, waits, after):
    n = len(names)
    n_src = n if srcs is not None else 0

    def body(*refs):
        src_refs, land_refs = refs[:n_src], refs[n_src:n_src + n]
        sems = refs[n_src + n:n_src + n + 2 * len(waits)]
        me = _my_slot()
        for wi, (_, leg) in enumerate(waits):
            for m, via in leg.routes:
                for k in range(n):
                    src = land_refs[k] if leg.src_is_land else src_refs[k]
                    cp = pltpu.make_async_remote_copy(
                        src_ref=leg.src_of(names[k], src, me, m), dst_ref=leg.dst_of(names[k], land_refs[k], me ^ via, m),
                        send_sem=sems[2 * wi].at[_sem_index(k, m)], recv_sem=sems[2 * wi + 1].at[_sem_index(k, m)],
                        device_id=_peer(via), device_id_type=MESH)
                    cp.wait_send()
                    cp.wait_recv()

    arrays = (list(srcs) if srcs is not None else []) + list(lands)
    outs = pl.pallas_call(
        body, name=name,
        out_shape=[pltpu.HBM(a.shape, a.dtype) for a in arrays],
        in_specs=[HBM] * len(arrays) + [SEM] * (2 * len(waits)) + [pl.BlockSpec(memory_space=pl.ANY)],
        out_specs=[HBM] * len(arrays),
        input_output_aliases={k: k for k in range(len(arrays))},
        compiler_params=pltpu.CompilerParams(has_side_effects=EFFECT),
    )(*arrays, *[s for pair, _ in waits for s in pair], after)
    return (outs[:n_src] if srcs is not None else None), outs[n_src:]


def _shard_slot(name, ref, slot):
    return SHARD_AT[name](ref, slot)


GATHER_ICI = Leg(((2, 2), (4, 4), (6, 6)), lambda name, ref, me, m: _shard_slot(name, ref, me),
                 lambda name, ref, sender, m: _shard_slot(name, ref, sender), True)
GATHER_D2D = Leg(((1, 1),), GATHER_ICI.src_of, GATHER_ICI.dst_of, True)
GATHER_FORWARD = Leg(((2, 1), (4, 1), (6, 1)), lambda name, ref, me, m: _shard_slot(name, ref, me ^ m),
                     lambda name, ref, sender, m: _shard_slot(name, ref, sender ^ m), True)
SCATTER = Leg(tuple((m, m) for m in range(1, N_DEV)), lambda name, ref, me, m: _shard_slot(name, ref, me ^ m),
              lambda name, ref, sender, m: ref.at[m - 1], False)


def _vec_exchange(vectors, after):
    n = len(vectors)

    def body(*refs):
        vec, vec_out = refs[n + 2], refs[n + 1]
        send_sems, recv_sems, local_sem = refs[n + 3:]
        for k in range(n):
            vec[k:k + 1, :] = refs[k][...]
        me = _my_slot()
        local = pltpu.make_async_copy(vec, vec_out.at[me], local_sem)
        local.start()
        sends = []
        for mask in range(1, N_DEV):
            cp = pltpu.make_async_remote_copy(
                src_ref=vec, dst_ref=vec_out.at[me], send_sem=send_sems.at[mask - 1],
                recv_sem=recv_sems.at[mask - 1], device_id=_peer(mask), device_id_type=MESH)
            cp.start()
            sends.append(cp)
        for mask in range(1, N_DEV):
            pltpu.make_async_remote_copy(
                src_ref=vec, dst_ref=vec_out.at[me ^ mask], send_sem=send_sems.at[mask - 1],
                recv_sem=recv_sems.at[mask - 1], device_id=_peer(mask), device_id_type=MESH).wait_recv()
        for cp in sends:
            cp.wait_send()
        local.wait()

    return pl.pallas_call(
        body, name="vec_exchange",
        out_shape=jax.ShapeDtypeStruct((N_DEV, n, D), F32),
        in_specs=[pl.BlockSpec(memory_space=pltpu.VMEM)] * n + [ANY],
        out_specs=pl.BlockSpec(memory_space=pl.ANY),
        scratch_shapes=[pltpu.VMEM((n, D), F32), pltpu.SemaphoreType.DMA((N_DEV - 1,)),
                        pltpu.SemaphoreType.DMA((N_DEV - 1,)), pltpu.SemaphoreType.DMA],
    )(*vectors, after)


def _adamw_update(g, w_ref, m_ref, v_ref, g_out, d_out, m_out, v_out):
    m_new = ADAM_B1 * m_ref[...] + (1.0 - ADAM_B1) * g
    v_new = ADAM_B2 * v_ref[...] + (1.0 - ADAM_B2) * (g * g)
    m_hat = m_new / (1.0 - ADAM_B1 ** ADAM_STEP)
    v_hat = v_new / (1.0 - ADAM_B2 ** ADAM_STEP)
    g_out[...] = g
    d_out[...] = -ADAM_LR * (m_hat / (jnp.sqrt(v_hat) + ADAM_EPS) + ADAM_WD * w_ref[...])
    m_out[...] = m_new
    v_out[...] = v_new


ADAMW_ROWS = 256


def _adamw(name, own, parts, w, m, v, slot, token):
    shard = w.shape
    if name in ("w_in", "w_ff1"):
        tr = ADAMW_ROWS
        grid = (shard[0] // tr,)
        own_spec = pl.BlockSpec((tr, shard[1]), lambda i, s: (i, s[0]))
        blk = pl.BlockSpec((tr, shard[1]), lambda i, s: (i, 0))
        parts_spec = pl.BlockSpec((N_DEV - 1, tr, shard[1]), lambda i, s: (0, i, 0))
    elif name == "pool_w":
        grid = (shard[0],)
        own_spec = pl.BlockSpec((None,) + shard[1:], lambda g, s: (g, s[0], 0))
        blk = pl.BlockSpec((None,) + shard[1:], lambda g, s: (g, 0, 0))
        parts_spec = pl.BlockSpec((N_DEV - 1, None) + shard[1:], lambda g, s: (0, g, 0, 0))
    elif name == "dw_kernel":
        grid = (1,)
        own_spec = pl.BlockSpec((None,) + shard, lambda i, s: (s[0], 0, 0))
        blk = pl.BlockSpec(shard, lambda i, s: (0, 0))
        parts_spec = pl.BlockSpec((N_DEV - 1,) + shard, lambda i, s: (0, 0, 0))
    else:
        tr = min(ADAMW_ROWS, shard[0])
        grid = (shard[0] // tr,)
        own_spec = pl.BlockSpec((tr, shard[1]), lambda i, s: (s[0] * grid[0] + i, 0))
        blk = pl.BlockSpec((tr, shard[1]), lambda i, s: (i, 0))
        parts_spec = pl.BlockSpec((N_DEV - 1, tr, shard[1]), lambda i, s: (0, i, 0))

    def body(slot_ref, own_ref, p_ref, w_ref, m_ref, v_ref, _token, g_out, d_out, m_out, v_out):
        g = own_ref[...].astype(F32)
        for k in range(N_DEV - 1):
            g = g + p_ref[k].astype(F32)
        _adamw_update(g, w_ref, m_ref, v_ref, g_out, d_out, m_out, v_out)

    return pl.pallas_call(
        body, name="adamw_" + name,
        grid_spec=pltpu.PrefetchScalarGridSpec(
            num_scalar_prefetch=1, grid=grid, in_specs=[own_spec, parts_spec, blk, blk, blk, ANY], out_specs=[blk] * 4),
        out_shape=[jax.ShapeDtypeStruct(shard, F32)] * 4,
        compiler_params=_params(("arbitrary",), 32),
    )(slot, *_pinned(own, parts, w, m, v, token))


def _adamw_vectors(parts, w, m, v):
    n_vec = w.shape[0]

    def body(p_ref, w_ref, m_ref, v_ref, g_out, d_out, m_out, v_out, loss_out):
        total = p_ref[0]
        for s in range(1, N_DEV):
            total = total + p_ref[s]
        _adamw_update(total[0:n_vec], w_ref, m_ref, v_ref, g_out, d_out, m_out, v_out)
        loss_out[...] = total[n_vec:n_vec + 1] * (0.5 / D)

    return pl.pallas_call(
        body, name="adamw_vectors",
        out_shape=[jax.ShapeDtypeStruct(w.shape, F32)] * 4 + [jax.ShapeDtypeStruct((1, D), F32)],
    )(parts, w, m, v)


def _input_norm(x, g_pre, token):
    seq = x.shape[0]
    tm = 1024

    def body(x_ref, g_ref, _token, u_ref, ut_ref):
        xf = x_ref[...]
        u = (xf * _rms_scale(xf) * g_ref[...]).astype(BF16)
        u_ref[...] = u
        ut_ref[...] = u.T

    return pl.pallas_call(
        body, name="input_norm", grid=(seq // tm,),
        in_specs=[pl.BlockSpec((tm, D), lambda i: (i, 0)), pl.BlockSpec((1, D), lambda i: (0, 0)), ANY],
        out_specs=[pl.BlockSpec((tm, D), lambda i: (i, 0)), pl.BlockSpec((D, tm), lambda i: (0, i))],
        out_shape=[pltpu.HBM((seq, D), BF16), pltpu.HBM((D, seq), BF16)],
        compiler_params=_params(("arbitrary",), 40),
    )(*_pinned(x, g_pre, token))


IN_PROJ_TN = IN_COLS // 4


def _in_proj_fwd(name, u, w_in, blocks, proj, token):
    seq = u.shape[0]
    tm, tn = 2048, IN_PROJ_TN

    def body(_blocks, u_ref, w_ref, _token, _proj_in, proj_ref):
        proj_ref[...] = _dot(u_ref[...], w_ref[...]).astype(BF16)

    return pl.pallas_call(
        body, name=name,
        grid_spec=pltpu.PrefetchScalarGridSpec(
            num_scalar_prefetch=1, grid=(seq // tm, blocks.shape[0]),
            in_specs=[pl.BlockSpec((tm, D), lambda i, j, b: (i, 0)), pl.BlockSpec((D, tn), lambda i, j, b: (0, b[j])),
                      ANY, ANY],
            out_specs=pl.BlockSpec((tm, tn), lambda i, j, b: (i, b[j]))),
        out_shape=pltpu.HBM((seq, IN_COLS), BF16),
        input_output_aliases={4: 0},
        compiler_params=_params(("arbitrary", "arbitrary"), 40),
    )(blocks, *_pinned(u, w_in, token, proj))


CONV_TM = 512
CONV_RS = 128


def _shifts():
    return [(b, [(a, 8 * a + b) for a in range(4) if 8 * a + b < TAPS]) for b in range(8)]


def _taps_looking_back(buf, row0, lanes, weight):
    acc = None
    for b, group in _shifts():
        part = None
        for a, s in group:
            term = weight(TAPS - 1 - s) * buf[pl.ds(row0 - 8 - 8 * a, CONV_RS + 8), lanes]
            part = term if part is None else part + term
        if b:
            part = pltpu.roll(part, b, 0)
        acc = part[8:, :] if acc is None else acc + part[8:, :]
    return acc


def _taps_looking_ahead(buf, row0, lanes, weight):
    acc = None
    for b, group in _shifts():
        part = None
        for a, s in group:
            term = weight(TAPS - 1 - s) * buf[pl.ds(row0 + 8 * a, CONV_RS + 8), lanes]
            part = term if part is None else part + term
        if b:
            part = pltpu.roll(part, CONV_RS + 8 - b, 0)
        acc = part[:CONV_RS, :] if acc is None else acc + part[:CONV_RS, :]
    return acc


def _layer_norm_parts(cv):
    mu = jnp.mean(cv, axis=-1, keepdims=True)
    cen = cv - mu
    rstd = lax.rsqrt(jnp.mean(cen * cen, axis=-1, keepdims=True) + LN_EPS)
    return cen * rstd, rstd


def _conv_fwd(proj, dw, dw_bias, ln_g, ln_b, w_conv_out, token):
    seq = proj.shape[0]
    tm = CONV_TM

    def body(a_ref, gate_ref, dw_ref, bias_ref, lg_ref, lb_ref, w_ref, _token, cv_ref, y_ref, ug):
        i = pl.program_id(0)

        @pl.when(i == 0)
        def _():
            ug[0:HALO, :] = jnp.zeros((HALO, D), F32)

        @pl.when(i > 0)
        def _():
            ug[0:HALO, :] = ug[tm:tm + HALO, :]
        ug[HALO:HALO + tm, :] = a_ref[...].astype(F32) * _sigmoid(gate_ref[...].astype(F32))

        def channel_block(cb, carry):
            lanes = pl.ds(pl.multiple_of(cb * 128, 128), 128)
            for r0 in range(0, tm, CONV_RS):
                taps = _taps_looking_back(ug, HALO + r0, lanes, lambda k: dw_ref[cb, k:k + 1, :])
                cv_ref[pl.ds(r0, CONV_RS), lanes] = taps + bias_ref[:, lanes]
            return carry
        lax.fori_loop(0, D // 128, channel_block, 0)

        n, _ = _layer_norm_parts(cv_ref[...])
        ln = n * lg_ref[...] + lb_ref[...]
        y_ref[...] = _dot((ln * _sigmoid(ln)).astype(BF16), w_ref[...]).astype(BF16)

    vec = pl.BlockSpec((1, D), lambda i: (0, 0))
    tile = pl.BlockSpec((tm, D), lambda i: (i, 0))
    return pl.pallas_call(
        body, name="conv_fwd", grid=(seq // tm,),
        in_specs=[pl.BlockSpec((tm, D), lambda i: (i, 0)), pl.BlockSpec((tm, D), lambda i: (i, 1)),
                  pl.BlockSpec((N_DEV, TAPS, 128), lambda i: (0, 0, 0)), vec, vec, vec,
                  pl.BlockSpec((D, D), lambda i: (0, 0)), ANY],
        out_specs=[tile, tile],
        out_shape=[pltpu.HBM((seq, D), F32), pltpu.HBM((seq, D), BF16)],
        scratch_shapes=[pltpu.VMEM((HALO + tm, D), F32)],
        compiler_params=_params(("arbitrary",), 32),
    )(*_pinned(proj, proj, dw, dw_bias, ln_g, ln_b, w_conv_out, token))


def _window_sums(rows, window, back):
    n = rows.shape[0]
    span = 1
    while span < window:
        rows = rows + pltpu.roll(rows, span if back else n - span, 0)
        span *= 2
    return rows


def _pool_counts(tile_index, tm, window):
    t = tile_index * tm + lax.broadcasted_iota(jnp.int32, (tm, 1), 0)
    return 1.0 / jnp.minimum(t + 1, window).astype(F32)


def _pool_merge_fwd(proj, y_conv, x, pool_w, pool_scale, w_pool_out, w_o, g_post, token):
    seq = proj.shape[0]
    tm = CONV_TM

    def body(p_ref, gc_ref, gp_ref, yc_ref, x_ref, pw_ref, ps_ref, wpo_ref, wo_ref, g_ref, _token,
             z_ref, zl_ref, yp_ref, mg_ref, o_ref, h1_ref, pbuf, zl_buf):
        i = pl.program_id(0)

        @pl.when(i == 0)
        def _():
            pbuf[0:HALO, :] = jnp.zeros((HALO, D), F32)

        @pl.when(i > 0)
        def _():
            pbuf[0:HALO, :] = pbuf[tm:tm + HALO, :]
        pbuf[HALO:HALO + tm, :] = p_ref[...].astype(F32)

        for g, window in enumerate(POOL_WINDOWS):
            lanes = pl.ds(g * PG, PG)
            acc = _window_sums(pbuf[:, lanes], window, back=True)[HALO:, :]
            zg = acc * _pool_counts(i, tm, window) - pbuf[pl.ds(HALO, tm), lanes]
            z_ref[:, lanes] = zg.astype(BF16)
            zl_buf[:, lanes] = _dot(zg.astype(BF16), pw_ref[g])
        zl = zl_buf[...]
        zl_ref[...] = zl.astype(BF16)
        y_pool = _dot((zl * ps_ref[...]).astype(BF16), wpo_ref[...])
        yp_ref[...] = y_pool.astype(BF16)
        merged = (_sigmoid(gc_ref[...].astype(F32)) * yc_ref[...].astype(F32)
                  + _sigmoid(gp_ref[...].astype(F32)) * y_pool).astype(BF16)
        mg_ref[...] = merged
        o = _dot(merged, wo_ref[...])
        o_ref[...] = o
        h1_ref[...] = x_ref[...] + o * _rms_scale(o) * g_ref[...]

    vec = pl.BlockSpec((1, D), lambda i: (0, 0))
    tile = pl.BlockSpec((tm, D), lambda i: (i, 0))
    mat = pl.BlockSpec((D, D), lambda i: (0, 0))
    return pl.pallas_call(
        body, name="pool_merge_fwd", grid=(seq // tm,),
        in_specs=[pl.BlockSpec((tm, D), lambda i: (i, 2)), pl.BlockSpec((tm, D), lambda i: (i, 3)),
                  pl.BlockSpec((tm, D), lambda i: (i, 4)), tile, tile,
                  pl.BlockSpec((4, PG, PG), lambda i: (0, 0, 0)), vec, mat, mat, vec, ANY],
        out_specs=[tile] * 6,
        out_shape=[pltpu.HBM((seq, D), dt) for dt in (BF16, BF16, BF16, BF16, F32, F32)],
        scratch_shapes=[pltpu.VMEM((HALO + tm, D), F32), pltpu.VMEM((tm, D), F32)],
        compiler_params=_params(("arbitrary",), 48),
    )(*_pinned(proj, proj, proj, y_conv, x, pool_w, pool_scale, w_pool_out, w_o, g_post, token))


def _mlp_fwd(h1, g_pre, w_ff1, w_ff2, g_post, target):
    seq = h1.shape[0]
    tm, tf = 1024, D_FF // N_DEV
    n_f = D_FF // tf

    def body(h1_ref, gpre_ref, w1_ref, w2_ref, gpost_ref, tgt_ref, v_ref, dm_ref, dh2_ref, sse_ref, ggrad_ref, macc):
        i, j = pl.program_id(0), pl.program_id(1)

        @pl.when(j == 0)
        def _():
            h = h1_ref[...]
            v_ref[...] = (h * _rms_scale(h) * gpre_ref[...]).astype(BF16)
        f = jnp.maximum(_dot(v_ref[...], w1_ref[...]), 0.0)
        part = _dot((f * f).astype(BF16), w2_ref[...])

        @pl.when(j == 0)
        def _():
            macc[...] = part

        @pl.when(j > 0)
        def _():
            macc[...] += part

        @pl.when(j == n_f - 1)
        def _():
            mo = macc[...]
            err = h1_ref[...] + mo * _rms_scale(mo) * gpost_ref[...] - tgt_ref[...]
            dh2 = err * (1.0 / D)
            dh2_ref[...] = dh2.astype(BF16)
            dm, ggrad = _rms_bwd(dh2, mo, gpost_ref[...])
            dm_ref[...] = dm.astype(BF16)
            _acc_out(ggrad_ref, i == 0, ggrad)
            _acc_out(sse_ref, i == 0, jnp.sum(jnp.sum(err * err, axis=1, keepdims=True), axis=0, keepdims=True))

    vec = pl.BlockSpec((1, D), lambda i, j: (0, 0))
    tile = pl.BlockSpec((tm, D), lambda i, j: (i, 0))
    return pl.pallas_call(
        body, name="mlp_fwd", grid=(seq // tm, n_f),
        in_specs=[tile, vec, pl.BlockSpec((D, tf), lambda i, j: (0, j)), pl.BlockSpec((tf, D), lambda i, j: (j, 0)), vec, tile],
        out_specs=[tile, tile, tile, pl.BlockSpec((1, 1), lambda i, j: (0, 0)), vec],
        out_shape=[pltpu.HBM((seq, D), BF16), pltpu.HBM((seq, D), BF16),
                   pltpu.HBM((seq, D), BF16), jax.ShapeDtypeStruct((1, 1), F32),
                   jax.ShapeDtypeStruct((1, D), F32)],
        scratch_shapes=[pltpu.VMEM((tm, D), F32)],
        compiler_params=_params(("arbitrary", "arbitrary"), 56),
    )(*_pinned(h1, g_pre, w_ff1, w_ff2, g_post, target))


def _mlp_bwd(v, dm, w_ff1, w_ff2):
    seq = v.shape[0]
    tm, tf = 1024, D_FF // N_DEV
    n_t = seq // tm

    def body(v_ref, dm_ref, w1_ref, w2_ref, dv_hbm, g1_ref, g2_ref, dv_acc, g1_acc, g2_acc, sems):
        j, i = pl.program_id(0), pl.program_id(1)
        vt, dmt = v_ref[...], dm_ref[...]
        f = jnp.maximum(_dot(vt, w1_ref[...]), 0.0)
        df = (_dot_nt(dmt, w2_ref[...]) * (2.0 * f)).astype(BF16)
        rows = pl.ds(pl.multiple_of(i * tm, tm), tm)
        dv_part = _dot_nt(df, w1_ref[...])

        @pl.when(j == 0)
        def _():
            dv_acc[rows, :] = dv_part

        @pl.when(j > 0)
        def _():
            dv_acc[rows, :] += dv_part
        g1_part = _dot_tn(vt, df)
        g2_part = _dot_tn((f * f).astype(BF16), dmt)

        @pl.when(i == 0)
        def _():
            g1_acc[...] = g1_part
            g2_acc[...] = g2_part

        @pl.when(i > 0)
        def _():
            g1_acc[...] += g1_part
            g2_acc[...] += g2_part

        @pl.when(i == n_t - 1)
        def _():
            g1_ref[...] = g1_acc[...].astype(BF16)
            g2_ref[...] = g2_acc[...].astype(BF16)

        last_pass = j == D_FF // tf - 1

        @pl.when(last_pass)
        def _():
            pltpu.make_async_copy(dv_acc.at[rows, :], dv_hbm.at[rows, :], sems.at[i]).start()

        @pl.when(jnp.logical_and(last_pass, i == n_t - 1))
        def _():
            for t in range(n_t):
                done = pl.ds(t * tm, tm)
                pltpu.make_async_copy(dv_acc.at[done, :], dv_hbm.at[done, :], sems.at[t]).wait()

    tile = pl.BlockSpec((tm, D), lambda j, i: (i, 0))
    return pl.pallas_call(
        body, name="mlp_bwd", grid=(D_FF // tf, n_t),
        in_specs=[tile, tile, pl.BlockSpec((D, tf), lambda j, i: (0, j)), pl.BlockSpec((tf, D), lambda j, i: (j, 0))],
        out_specs=[pl.BlockSpec(memory_space=pl.ANY), pl.BlockSpec((D, tf), lambda j, i: (0, j)),
                   pl.BlockSpec((tf, D), lambda j, i: (j, 0))],
        out_shape=[pltpu.HBM((seq, D), F32), pltpu.HBM((D, D_FF), BF16),
                   pltpu.HBM((D_FF, D), BF16)],
        scratch_shapes=[pltpu.VMEM((seq, D), F32), pltpu.VMEM((D, tf), F32), pltpu.VMEM((tf, D), F32),
                        pltpu.SemaphoreType.DMA((n_t,))],
        compiler_params=_params(("arbitrary", "arbitrary"), 52),
    )(*_pinned(v, dm, w_ff1, w_ff2))


def _merge_bwd(dh2, dv, h1, g_mlp_pre, o, g_mix_post, w_o, merged, proj, y_conv, y_pool, token):
    seq = dh2.shape[0]
    tm = 256
    n_t = seq // tm

    def body(dh2_ref, dv_ref, h1_ref, gpre_ref, o_ref, gpost_ref, wo_ref, mg_ref, gc_ref, gp_ref, yc_ref, yp_ref, _token,
             dh1_ref, dyc_ref, dyp_ref, dg_ref, gwo_ref, ggpre_ref, ggpost_ref, gwo_acc):
        i = pl.program_id(0)
        dnorm, ggpre = _rms_bwd(dv_ref[...], h1_ref[...], gpre_ref[...])
        dh1 = dh2_ref[...].astype(F32) + dnorm
        dh1_ref[...] = dh1
        do, ggpost = _rms_bwd(dh1, o_ref[...], gpost_ref[...])
        do = do.astype(BF16)
        _acc_out(ggpre_ref, i == 0, ggpre)
        _acc_out(ggpost_ref, i == 0, ggpost)
        _acc_out(gwo_acc, i == 0, _dot_tn(mg_ref[...], do))
        dmerged = _dot_nt(do, wo_ref[...])
        sc, sp = _sigmoid(gc_ref[...].astype(F32)), _sigmoid(gp_ref[...].astype(F32))
        dyc_ref[...] = (dmerged * sc).astype(BF16)
        dyp_ref[...] = (dmerged * sp).astype(BF16)
        dg_ref[:, 0:D] = (dmerged * yc_ref[...].astype(F32) * (sc * (1.0 - sc))).astype(BF16)
        dg_ref[:, D:2 * D] = (dmerged * yp_ref[...].astype(F32) * (sp * (1.0 - sp))).astype(BF16)

        @pl.when(i == n_t - 1)
        def _():
            gwo_ref[...] = gwo_acc[...].astype(BF16)

    vec = pl.BlockSpec((1, D), lambda i: (0, 0))
    tile = pl.BlockSpec((tm, D), lambda i: (i, 0))
    mat = pl.BlockSpec((D, D), lambda i: (0, 0))
    return pl.pallas_call(
        body, name="merge_bwd", grid=(n_t,),
        in_specs=[tile, tile, tile, vec, tile, vec, mat, tile,
                  pl.BlockSpec((tm, D), lambda i: (i, 3)), pl.BlockSpec((tm, D), lambda i: (i, 4)), tile, tile, ANY],
        out_specs=[tile, tile, tile, pl.BlockSpec((tm, 2 * D), lambda i: (i, 0)), mat, vec, vec],
        out_shape=[pltpu.HBM((seq, D), F32), pltpu.HBM((seq, D), BF16),
                   pltpu.HBM((seq, D), BF16), pltpu.HBM((seq, 2 * D), BF16),
                   pltpu.HBM((D, D), BF16), jax.ShapeDtypeStruct((1, D), F32),
                   jax.ShapeDtypeStruct((1, D), F32)],
        scratch_shapes=[pltpu.VMEM((D, D), F32)],
        compiler_params=_params(("arbitrary",), 48),
    )(*_pinned(dh2, dv, h1, g_mlp_pre, o, g_mix_post, w_o, merged, proj, proj, y_conv, y_pool, token))


def _pool_bwd(dy_pool, zl, z, pool_w, pool_scale, w_pool_out, token):
    seq = dy_pool.shape[0]
    tm = CONV_TM
    n_t = seq // tm

    def body(dy_ref, zl_ref, z_ref, pw_ref, ps_ref, wpo_ref, _token,
             dp_ref, gwpo_ref, gpw_ref, gps_ref, qbuf, gwpo_acc, gpw_acc):
        i = pl.program_id(0)
        tile_index = n_t - 1 - i
        first = i == 0
        dy = dy_ref[...]
        zl = zl_ref[...].astype(F32)
        dzs = _dot_nt(dy, wpo_ref[...])
        _acc_out(gwpo_acc, first, _dot_tn((zl * ps_ref[...]).astype(BF16), dy))
        _acc_out(gps_ref, first, jnp.sum(dzs * zl, axis=0, keepdims=True))
        dzl = (dzs * ps_ref[...]).astype(BF16)

        @pl.when(first)
        def _():
            qbuf[tm:tm + HALO, :] = jnp.zeros((HALO, D), F32)

        @pl.when(jnp.logical_not(first))
        def _():
            qbuf[tm:tm + HALO, :] = qbuf[0:HALO, :]

        dzs_list = []
        for g, window in enumerate(POOL_WINDOWS):
            lanes = pl.ds(g * PG, PG)
            dzl_g = dzl[:, g * PG:(g + 1) * PG]
            dz = _dot_nt(dzl_g, pw_ref[g])
            _acc_out(gpw_acc.at[g], first, _dot_tn(z_ref[:, lanes], dzl_g))
            qbuf[pl.ds(0, tm), lanes] = dz * _pool_counts(tile_index, tm, window)
            dzs_list.append(dz)
        for g, window in enumerate(POOL_WINDOWS):
            lanes = pl.ds(g * PG, PG)
            acc = _window_sums(qbuf[:, lanes], window, back=False)[:tm, :]
            dp_ref[:, lanes] = (acc - dzs_list[g]).astype(BF16)

        @pl.when(i == n_t - 1)
        def _():
            gwpo_ref[...] = gwpo_acc[...].astype(BF16)
            gpw_ref[...] = gpw_acc[...].astype(BF16)

    vec = pl.BlockSpec((1, D), lambda i: (0, 0))
    tile = pl.BlockSpec((tm, D), lambda i: (n_t - 1 - i, 0))
    mat = pl.BlockSpec((D, D), lambda i: (0, 0))
    pw = pl.BlockSpec((4, PG, PG), lambda i: (0, 0, 0))
    return pl.pallas_call(
        body, name="pool_bwd", grid=(n_t,),
        in_specs=[tile, tile, tile, pw, vec, mat, ANY],
        out_specs=[tile, mat, pw, vec],
        out_shape=[pltpu.HBM((seq, D), BF16), pltpu.HBM((D, D), BF16),
                   pltpu.HBM((4, PG, PG), BF16), jax.ShapeDtypeStruct((1, D), F32)],
        scratch_shapes=[pltpu.VMEM((tm + HALO, D), F32), pltpu.VMEM((D, D), F32), pltpu.VMEM((4, PG, PG), F32)],
        compiler_params=_params(("arbitrary",), 40),
    )(*_pinned(dy_pool, zl, z, pool_w, pool_scale, w_pool_out, token))


def _conv_bwd(dy_conv, cv, proj, dw, ln_g, ln_b, w_conv_out, token):
    seq = dy_conv.shape[0]
    tm = CONV_TM // 2
    n_t = seq // tm
    halo_blocks = tm // HALO

    def body(dy_ref, cv_ref, a_ref, gate_ref, ah_ref, gh_ref, dw_ref, lg_ref, lb_ref, w_ref, _token,
             dglu_ref, gw_ref, gdw_ref, gbias_ref, glg_ref, glb_ref, ug, dcv, dug, gw_acc):
        i = pl.program_id(0)
        tile_index = n_t - 1 - i
        first = i == 0
        dy = dy_ref[...]
        n, rstd = _layer_norm_parts(cv_ref[...])
        ln = n * lg_ref[...] + lb_ref[...]
        sg = _sigmoid(ln)
        _acc_out(gw_acc, first, _dot_tn((ln * sg).astype(BF16), dy))
        dln = _dot_nt(dy, w_ref[...]) * (sg * (1.0 + ln * (1.0 - sg)))
        _acc_out(glg_ref, first, jnp.sum(dln * n, axis=0, keepdims=True))
        _acc_out(glb_ref, first, jnp.sum(dln, axis=0, keepdims=True))
        dn = dln * lg_ref[...]
        dcv_tile = rstd * (dn - jnp.mean(dn, axis=-1, keepdims=True) - n * jnp.mean(dn * n, axis=-1, keepdims=True))
        _acc_out(gbias_ref, first, jnp.sum(dcv_tile, axis=0, keepdims=True))

        @pl.when(first)
        def _():
            dcv[tm:tm + HALO, :] = jnp.zeros((HALO, D), F32)

        @pl.when(jnp.logical_not(first))
        def _():
            dcv[tm:tm + HALO, :] = dcv[0:HALO, :]
        dcv[0:tm, :] = dcv_tile

        a, gate = a_ref[...].astype(F32), gate_ref[...].astype(F32)
        sgate = _sigmoid(gate)
        ug[HALO:HALO + tm, :] = a * sgate
        before = jnp.where(tile_index > 0, 1.0, 0.0)
        ug[0:HALO, :] = ah_ref[...].astype(F32) * _sigmoid(gh_ref[...].astype(F32)) * before

        @pl.when(first)
        def _():
            gdw_ref[...] = jnp.zeros((N_DEV, TAPS + 1, 128), F32)

        def channel_block(cb, carry):
            lanes = pl.ds(pl.multiple_of(cb * 128, 128), 128)
            for r0 in range(0, tm, CONV_RS):
                dug[pl.ds(r0, CONV_RS), lanes] = _taps_looking_ahead(dcv, r0, lanes, lambda k: dw_ref[cb, k:k + 1, :])
            for b, group in _shifts():
                sums = [jnp.zeros((8, 128), F32) for _ in group]
                for r0 in range(0, tm, CONV_RS):
                    window = ug[pl.ds(r0, CONV_RS + HALO), lanes]
                    if b:
                        window = pltpu.roll(window, b, 0)
                    d = dcv[pl.ds(r0, CONV_RS), lanes]
                    for n_a, (a, s) in enumerate(group):
                        prod = d * window[HALO - 8 * a:HALO - 8 * a + CONV_RS, :]
                        sums[n_a] = sums[n_a] + jnp.sum(prod.reshape(CONV_RS // 8, 8, 128), axis=0)
                for n_a, (a, s) in enumerate(group):
                    k = TAPS - 1 - s
                    gdw_ref[cb, k:k + 1, :] += jnp.sum(sums[n_a], axis=0, keepdims=True)
            return carry
        lax.fori_loop(0, D // 128, channel_block, 0)

        d_ug = dug[...]
        dglu_ref[:, 0:D] = (d_ug * sgate).astype(BF16)
        dglu_ref[:, D:2 * D] = (d_ug * a * (sgate * (1.0 - sgate))).astype(BF16)

        @pl.when(i == n_t - 1)
        def _():
            gw_ref[...] = gw_acc[...].astype(BF16)

    def halo_index(col):
        return lambda i: (jnp.maximum((n_t - 1 - i) * halo_blocks - 1, 0), col)

    vec = pl.BlockSpec((1, D), lambda i: (0, 0))
    tile = pl.BlockSpec((tm, D), lambda i: (n_t - 1 - i, 0))
    mat = pl.BlockSpec((D, D), lambda i: (0, 0))
    dwspec = pl.BlockSpec((N_DEV, TAPS, 128), lambda i: (0, 0, 0))
    return pl.pallas_call(
        body, name="conv_bwd", grid=(n_t,),
        in_specs=[tile, tile, pl.BlockSpec((tm, D), lambda i: (n_t - 1 - i, 0)), pl.BlockSpec((tm, D), lambda i: (n_t - 1 - i, 1)),
                  pl.BlockSpec((HALO, D), halo_index(0)), pl.BlockSpec((HALO, D), halo_index(1)), dwspec, vec, vec, mat,
                  ANY],
        out_specs=[pl.BlockSpec((tm, 2 * D), lambda i: (n_t - 1 - i, 0)), mat,
                   pl.BlockSpec((N_DEV, TAPS + 1, 128), lambda i: (0, 0, 0)), vec, vec, vec],
        out_shape=[pltpu.HBM((seq, 2 * D), BF16), pltpu.HBM((D, D), BF16),
                   pltpu.HBM((N_DEV, TAPS + 1, 128), F32), jax.ShapeDtypeStruct((1, D), F32),
                   jax.ShapeDtypeStruct((1, D), F32), jax.ShapeDtypeStruct((1, D), F32)],
        scratch_shapes=[pltpu.VMEM((HALO + tm, D), F32), pltpu.VMEM((tm + HALO, D), F32), pltpu.VMEM((tm, D), F32),
                        pltpu.VMEM((D, D), F32)],
        compiler_params=_params(("arbitrary",), 48),
    )(*_pinned(dy_conv, cv, proj, proj, proj, proj, dw, ln_g, ln_b, w_conv_out, token))


def _in_proj_bwd_x(d_glu, dp, dgates, w_in, x, g_pre, dh1, token):
    seq = x.shape[0]
    tm = 512

    def body(dglu_ref, dp_ref, dg_ref, w_ref, x_ref, g_ref, dh1_ref, _token, dx_ref, gg_ref):
        du = _dot_nt(dglu_ref[...], w_ref[:, 0:2 * D])
        du += _dot_nt(dp_ref[...], w_ref[:, 2 * D:3 * D])
        du += _dot_nt(dg_ref[...], w_ref[:, 3 * D:5 * D])
        dnorm, gg = _rms_bwd(du, x_ref[...], g_ref[...])
        dx_ref[...] = dh1_ref[...] + dnorm
        _acc_out(gg_ref, pl.program_id(0) == 0, gg)

    vec = pl.BlockSpec((1, D), lambda i: (0, 0))
    tile = pl.BlockSpec((tm, D), lambda i: (i, 0))
    wide = pl.BlockSpec((tm, 2 * D), lambda i: (i, 0))
    return pl.pallas_call(
        body, name="in_proj_bwd_x", grid=(seq // tm,),
        in_specs=[wide, tile, wide, pl.BlockSpec((D, IN_COLS), lambda i: (0, 0)), tile, vec, tile, ANY],
        out_specs=[tile, vec],
        out_shape=[pltpu.HBM((seq, D), F32), jax.ShapeDtypeStruct((1, D), F32)],
        compiler_params=_params(("arbitrary",), 48),
    )(*_pinned(d_glu, dp, dgates, w_in, x, g_pre, dh1, token))


def _in_proj_bwd_w(u_t, d_glu, dp, dgates, token):
    seq = u_t.shape[1]
    tm = 2048
    n_t = seq // tm

    def body(u_ref, dglu_ref, dp_ref, dg_ref, _token, out_ref, acc):
        b, i = pl.program_id(0), pl.program_id(1)
        ut = u_ref[...]

        def add(d_ref):
            _acc_out(acc, i == 0, _dot(ut, d_ref[...]))

        pl.when(b < 2)(lambda: add(dglu_ref))
        pl.when(b == 2)(lambda: add(dp_ref))
        pl.when(b > 2)(lambda: add(dg_ref))

        @pl.when(i == n_t - 1)
        def _():
            out_ref[...] = acc[...].astype(BF16)

    return pl.pallas_call(
        body, name="in_proj_bwd_w", grid=(IN_COLS // D, n_t),
        in_specs=[pl.BlockSpec((D, tm), lambda b, i: (0, i)),
                  pl.BlockSpec((tm, D), lambda b, i: (jnp.where(b < 2, i, 0), jnp.minimum(b, 1))),
                  pl.BlockSpec((tm, D), lambda b, i: (jnp.where(b == 2, i, 0), 0)),
                  pl.BlockSpec((tm, D), lambda b, i: (jnp.where(b > 2, i, 0), jnp.maximum(b - 3, 0))), ANY],
        out_specs=pl.BlockSpec((D, D), lambda b, i: (0, b)),
        out_shape=pltpu.HBM((D, IN_COLS), BF16),
        scratch_shapes=[pltpu.VMEM((D, D), F32)],
        compiler_params=_params(("arbitrary", "arbitrary"), 52),
    )(*_pinned(u_t, d_glu, dp, dgates, token))


VEC_NAMES = ("mix_pre_g", "dw_bias", "conv_ln_g", "conv_ln_b", "pool_scale", "mix_post_g", "mlp_pre_g", "mlp_post_g")
WEIGHT_ORDER = ("mix_pre_g", "w_in", "dw_kernel", "dw_bias", "conv_ln_g", "conv_ln_b", "w_conv_out", "pool_w",
                "pool_scale", "w_pool_out", "w_o", "mix_post_g", "mlp_pre_g", "w_ff1", "w_ff2", "mlp_post_g")


def _step(x, loss_target, w, m, v):
    row = lambda a: a.reshape(1, D)
    names = [s[0] for s in SHARDED]

    gather_groups = (("w_in",), ("w_conv_out", "dw_kernel"), ("pool_w", "w_pool_out", "w_o"), ("w_ff1", "w_ff2"))
    legs = [GATHER_ICI, GATHER_D2D]
    first_level, token = _exchange_start(
        "gather_start_w_in", [(gather_groups[0], None, _stage_shards(gather_groups[0], [w["w_in"]], x))], legs)
    later = [n for g in gather_groups[1:] for n in g]
    seeded = dict(zip(later, _stage_shards(later, [w[n] for n in later], token)))
    rest, token = _exchange_start("gather_start", [(g, None, [seeded[n] for n in g]) for g in gather_groups[1:]], legs)
    first_level = first_level + rest
    forwarded, full = {}, {}

    def forward(k, after):
        group, (ici_sems, d2d_sems), _, landed = first_level[k]
        _, landed = _exchange_wait("gather_ici_" + group[0], group, None, landed, [(ici_sems, GATHER_ICI)], after)
        (second,), tok = _exchange_start("gather_forward_" + group[0], [(group, None, landed)], [GATHER_FORWARD])
        forwarded[k] = (second[3], [(d2d_sems, GATHER_D2D), (second[1][0], GATHER_FORWARD)])
        return tok

    def gathered(k, after):
        group = gather_groups[k]
        landed, waits = forwarded[k]
        _, arrays = _exchange_wait("gather_wait_" + group[0], group, None, landed, waits, after)
        full.update(zip(group, arrays))

    u, u_t = _input_norm(x, row(w["mix_pre_g"]), token)
    group, (ici_sems, d2d_sems), _, landed = first_level[0]
    _, landed = _exchange_wait("gather_d2d_w_in", group, None, landed, [(d2d_sems, GATHER_D2D)], u)
    home = jnp.reshape(_my_slot() // 2, (1,)).astype(jnp.int32)
    proj = _in_proj_fwd("in_proj_fwd_home", u, landed[0], home, lax.empty((x.shape[0], IN_COLS), BF16), u)
    _, landed = _exchange_wait("gather_ici_w_in", group, None, landed, [(ici_sems, GATHER_ICI)], proj)
    (second,), tok = _exchange_start("gather_forward_w_in", [(group, None, landed)], [GATHER_FORWARD])
    (full["w_in"],) = _exchange_wait("gather_wait_w_in", group, None, second[3], [(second[1][0], GATHER_FORWARD)], tok)[1]
    tok = forward(1, full["w_in"])
    away = jnp.stack([home[0] ^ 1, home[0] ^ 2, home[0] ^ 3]).astype(jnp.int32)
    proj = _in_proj_fwd("in_proj_fwd", u, full["w_in"], away, proj, tok)
    gathered(1, proj)
    tok = forward(2, proj)
    cv, y_conv = _conv_fwd(proj, full["dw_kernel"], row(w["dw_bias"]), row(w["conv_ln_g"]), row(w["conv_ln_b"]),
                           full["w_conv_out"], tok)
    gathered(2, y_conv)
    tok = forward(3, y_conv)
    z, zl, y_pool, merged, o, h1 = _pool_merge_fwd(proj, y_conv, x, full["pool_w"], row(w["pool_scale"]),
                                                   full["w_pool_out"], full["w_o"], row(w["mix_post_g"]), tok)
    gathered(3, h1)
    vv, dm, dh2, sse, g_mlp_post = _mlp_fwd(h1, row(w["mlp_pre_g"]), full["w_ff1"], full["w_ff2"],
                                            row(w["mlp_post_g"]), loss_target)

    shard_of = {name: shard for name, _, shard, _ in SHARDED}

    def scatter_start(tag, group, grads):
        landings = [lax.empty((N_DEV - 1,) + shard_of[n], g.dtype) for n, g in zip(group, grads)]
        (handle,), tok = _exchange_start("scatter_start_" + tag, [(group, grads, landings)], [SCATTER])
        return handle, tok

    dv, g_ff1, g_ff2 = _mlp_bwd(vv, dm, full["w_ff1"], full["w_ff2"])
    h_ff, tok_ff = scatter_start("ff", ("w_ff1", "w_ff2"), [g_ff1, g_ff2])
    dh1, dy_conv, dy_pool, dgates, g_wo, g_mlp_pre, g_mix_post = _merge_bwd(
        dh2, dv, h1, row(w["mlp_pre_g"]), o, row(w["mix_post_g"]), full["w_o"], merged, proj, y_conv, y_pool, tok_ff)
    dp, g_wpo, g_pw, g_pool_scale = _pool_bwd(dy_pool, zl, z, full["pool_w"], row(w["pool_scale"]), full["w_pool_out"],
                                              dh1)
    h_pool, tok_pool = scatter_start("pool", ("w_o", "w_pool_out", "pool_w"), [g_wo, g_wpo, g_pw])
    d_glu, g_wco, g_dw, g_bias, g_ln_g, g_ln_b = _conv_bwd(dy_conv, cv, proj, full["dw_kernel"], row(w["conv_ln_g"]),
                                                            row(w["conv_ln_b"]), full["w_conv_out"], tok_pool)
    h_conv, tok_conv = scatter_start("conv", ("w_conv_out", "dw_kernel"), [g_wco, g_dw[:, :TAPS, :]])
    g_win = _in_proj_bwd_w(u_t, d_glu, dp, dgates, tok_conv)
    h_in, tok_in = scatter_start("in", ("w_in",), [g_win])
    grad_x, g_mix_pre = _in_proj_bwd_x(d_glu, dp, dgates, full["w_in"], x, row(w["mix_pre_g"]), dh1, tok_in)

    out = {}
    slot = jnp.reshape(_my_slot(), (1,)).astype(jnp.int32)
    after = grad_x
    for tag, (group, (sems,), grads, landings) in (("ff", h_ff), ("pool", h_pool), ("conv", h_conv), ("in", h_in)):
        if tag == "in":
            vec_parts = after = _vec_exchange(
                [g_mix_pre, g_bias, g_ln_g, g_ln_b, g_pool_scale, g_mix_post, g_mlp_pre, g_mlp_post,
                 jnp.broadcast_to(sse, (1, D))], after)
        mine, landed = _exchange_wait("scatter_wait_" + tag, group, grads, landings, [(sems, SCATTER)], after)
        for name, own, parts in zip(group, mine, landed):
            out[name] = _adamw(name, own, parts, w[name], m[name], v[name], slot, after)
            after = out[name][0]
    stack = lambda d: jnp.stack([d[n] for n in VEC_NAMES], axis=0)
    *res, loss_row = _adamw_vectors(vec_parts, stack(w), stack(m), stack(v))
    for k, name in enumerate(VEC_NAMES):
        out[name] = [r[k] for r in res]
    return loss_row[0, 0], grad_x, out


def kernel(x, mix_pre_g, w_in, dw_kernel, dw_bias, conv_ln_g, conv_ln_b, w_conv_out, pool_w, pool_scale, w_pool_out, w_o, mix_post_g, mlp_pre_g, w_ff1, w_ff2, mlp_post_g, loss_target, m_mix_pre_g, m_w_in, m_dw_kernel, m_dw_bias, m_conv_ln_g, m_conv_ln_b, m_w_conv_out, m_pool_w, m_pool_scale, m_w_pool_out, m_w_o, m_mix_post_g, m_mlp_pre_g, m_w_ff1, m_w_ff2, m_mlp_post_g, v_mix_pre_g, v_w_in, v_dw_kernel, v_dw_bias, v_conv_ln_g, v_conv_ln_b, v_w_conv_out, v_pool_w, v_pool_scale, v_w_pool_out, v_w_o, v_mix_post_g, v_mlp_pre_g, v_w_ff1, v_w_ff2, v_mlp_post_g):
    w = dict(mix_pre_g=mix_pre_g, w_in=w_in, dw_kernel=dw_kernel, dw_bias=dw_bias, conv_ln_g=conv_ln_g, conv_ln_b=conv_ln_b,
             w_conv_out=w_conv_out, pool_w=pool_w, pool_scale=pool_scale, w_pool_out=w_pool_out, w_o=w_o,
             mix_post_g=mix_post_g, mlp_pre_g=mlp_pre_g, w_ff1=w_ff1, w_ff2=w_ff2, mlp_post_g=mlp_post_g)
    m = dict(mix_pre_g=m_mix_pre_g, w_in=m_w_in, dw_kernel=m_dw_kernel, dw_bias=m_dw_bias, conv_ln_g=m_conv_ln_g,
             conv_ln_b=m_conv_ln_b, w_conv_out=m_w_conv_out, pool_w=m_pool_w, pool_scale=m_pool_scale,
             w_pool_out=m_w_pool_out, w_o=m_w_o, mix_post_g=m_mix_post_g, mlp_pre_g=m_mlp_pre_g, w_ff1=m_w_ff1,
             w_ff2=m_w_ff2, mlp_post_g=m_mlp_post_g)
    v = dict(mix_pre_g=v_mix_pre_g, w_in=v_w_in, dw_kernel=v_dw_kernel, dw_bias=v_dw_bias, conv_ln_g=v_conv_ln_g,
             conv_ln_b=v_conv_ln_b, w_conv_out=v_w_conv_out, pool_w=v_pool_w, pool_scale=v_pool_scale,
             w_pool_out=v_w_pool_out, w_o=v_w_o, mix_post_g=v_mix_post_g, mlp_pre_g=v_mlp_pre_g, w_ff1=v_w_ff1,
             w_ff2=v_w_ff2, mlp_post_g=v_mlp_post_g)
    seq = x.shape[1]
    loss, grad_x, out = _step(x.reshape(seq, D), loss_target.reshape(seq, D), w, m, v)
    grads, deltas, new_m, new_v = ([out[n][k] for n in WEIGHT_ORDER] for k in range(4))
    return (loss, grad_x.reshape(x.shape), *grads, *deltas, *new_m, *new_v)
```

```python
import collections

import jax
import jax.numpy as jnp
from jax import lax
from jax.experimental import pallas as pl
from jax.experimental.pallas import tpu as pltpu

D = 1024
D_FF = 4 * D
IN_COLS = 5 * D
TAPS = 31
HALO = 32
POOL_WINDOWS = (2, 4, 8, 16)
PG = D // 4
N_DEV = 8
RMS_EPS = 1e-6
LN_EPS = 1e-5
ADAM_LR, ADAM_B1, ADAM_B2, ADAM_EPS, ADAM_WD, ADAM_STEP = 0.001, 0.9, 0.999, 1e-08, 0.01, 10

BF16 = jnp.bfloat16
F32 = jnp.float32
MIB = 1 << 20
MESH = pl.DeviceIdType.MESH


def _params(sem, vmem_mib):
    return pltpu.CompilerParams(dimension_semantics=sem, vmem_limit_bytes=vmem_mib * MIB)


def _dot(a, b):
    return jnp.dot(a, b, preferred_element_type=F32)


def _dot_nt(a, b):
    return lax.dot_general(a, b, (((1,), (1,)), ((), ())), preferred_element_type=F32)


def _dot_tn(a, b):
    return lax.dot_general(a, b, (((0,), (0,)), ((), ())), preferred_element_type=F32)


def _rms_scale(x):
    return lax.rsqrt(jnp.mean(x * x, axis=-1, keepdims=True) + RMS_EPS)


def _rms_bwd(dy, x, g):
    xn = x * _rms_scale(x)
    dn = dy * g
    dx = _rms_scale(x) * (dn - xn * jnp.mean(dn * xn, axis=-1, keepdims=True))
    return dx, jnp.sum(dy * xn, axis=0, keepdims=True)


def _sigmoid(x):
    return jax.nn.sigmoid(x)


def _acc_out(ref, first, value):
    @pl.when(first)
    def _():
        ref[...] = value

    @pl.when(jnp.logical_not(first))
    def _():
        ref[...] += value


def _my_slot():
    return 4 * lax.axis_index("x") + 2 * lax.axis_index("y") + lax.axis_index("c")


def _peer(mask):
    x, y, c = lax.axis_index("x"), lax.axis_index("y"), lax.axis_index("c")
    return (x ^ ((mask >> 2) & 1), y ^ ((mask >> 1) & 1), c ^ (mask & 1))


def _cols(width):
    return lambda ref, slot: ref.at[:, pl.ds(pl.multiple_of(slot * width, 128), width)]


def _rows(height):
    return lambda ref, slot: ref.at[pl.ds(pl.multiple_of(slot * height, 8), height), :]


def _lead(ref, slot):
    return ref.at[slot]


def _pool_rows(ref, slot):
    return ref.at[:, pl.ds(pl.multiple_of(slot * (PG // N_DEV), 8), PG // N_DEV), :]


SHARDED = (
    ("w_in", (D, IN_COLS), (D, IN_COLS // N_DEV), _cols(IN_COLS // N_DEV)),
    ("w_ff1", (D, D_FF), (D, D_FF // N_DEV), _cols(D_FF // N_DEV)),
    ("w_ff2", (D_FF, D), (D_FF // N_DEV, D), _rows(D_FF // N_DEV)),
    ("w_conv_out", (D, D), (D // N_DEV, D), _rows(D // N_DEV)),
    ("w_pool_out", (D, D), (D // N_DEV, D), _rows(D // N_DEV)),
    ("w_o", (D, D), (D // N_DEV, D), _rows(D // N_DEV)),
    ("pool_w", (4, PG, PG), (4, PG // N_DEV, PG), _pool_rows),
    ("dw_kernel", (N_DEV, TAPS, D // N_DEV), (TAPS, D // N_DEV), _lead),
)
N_SHARDED = len(SHARDED)


SHARD_AT = {name: at for name, _, _, at in SHARDED}
HBM = pl.BlockSpec(memory_space=pltpu.HBM)
SEM = pl.BlockSpec(memory_space=pltpu.SEMAPHORE)
ANY = pl.BlockSpec(memory_space=pl.ANY)
EFFECT = pltpu.SideEffectType.DATAFLOW_SIDE_EFFECTING


def _in_hbm(a):
    return pltpu.with_memory_space_constraint(a, pltpu.HBM)


def _pinned(*arrays):
    return [_in_hbm(a) for a in arrays]


def _stage_shards(names, shards, token):
    n = len(names)
    specs = [s for name in names for s in SHARDED if s[0] == name]
    dtypes = [F32 if name == "dw_kernel" else BF16 for name in names]

    def body(*refs):
        ins = refs[:n]
        fulls = refs[n + 1:2 * n + 1]
        raw = refs[2 * n + 1:3 * n + 1]
        stage = refs[3 * n + 1:4 * n + 1]
        in_sems, out_sems = refs[4 * n + 1:]
        me = _my_slot()
        loads = [pltpu.make_async_copy(ins[a], raw[a], in_sems.at[a]) for a in range(n)]
        for cp in loads:
            cp.start()
        stores = []
        for a, (_, _, _, at) in enumerate(specs):
            loads[a].wait()
            stage[a][...] = raw[a][...].astype(dtypes[a])
            cp = pltpu.make_async_copy(stage[a], at(fulls[a], me), out_sems.at[a])
            cp.start()
            stores.append(cp)
        for cp in stores:
            cp.wait()

    return pl.pallas_call(
        body, name="stage_" + names[0],
        out_shape=[pltpu.HBM(full, dt) for (_, full, _, _), dt in zip(specs, dtypes)],
        in_specs=[ANY] * (n + 1),
        out_specs=[ANY] * n,
        scratch_shapes=[pltpu.VMEM(shard, F32) for _, _, shard, _ in specs]
        + [pltpu.VMEM(shard, dt) for (_, _, shard, _), dt in zip(specs, dtypes)]
        + [pltpu.SemaphoreType.DMA((n,)), pltpu.SemaphoreType.DMA((n,))],
        compiler_params=pltpu.CompilerParams(vmem_limit_bytes=40 * MIB),
    )(*_pinned(*shards, token))


Leg = collections.namedtuple("Leg", "routes src_of dst_of src_is_land")


def _sem_index(k, m):
    return k * (N_DEV - 1) + m - 1


def _exchange_start(name, groups, legs):
    sizes = [len(g[0]) for g in groups]
    names = [nm for g in groups for nm in g[0]]
    srcs = [s for g in groups if g[1] is not None for s in g[1]]
    lands = [l for g in groups for l in g[2]]
    n_src, n, n_g, n_l = len(srcs), len(names), len(groups), len(legs)

    def body(*refs):
        src_refs, land_refs = list(refs[:n_src]), refs[n_src:n_src + n]
        sems = refs[n_src + n:n_src + n + 2 * n_g * n_l]
        token = refs[-1]
        me = _my_slot()
        first = 0
        for g, size in enumerate(sizes):
            own_src = [src_refs.pop(0) for _ in range(size)] if groups[g][1] is not None else None
            for li, leg in enumerate(legs):
                send, recv = sems[2 * (g * n_l + li)], sems[2 * (g * n_l + li) + 1]
                for m, via in leg.routes:
                    for k in range(size):
                        land = land_refs[first + k]
                        src = land if leg.src_is_land else own_src[k]
                        pltpu.make_async_remote_copy(
                            src_ref=leg.src_of(names[first + k], src, me, m), dst_ref=leg.dst_of(names[first + k], land, me, m),
                            send_sem=send.at[_sem_index(k, m)], recv_sem=recv.at[_sem_index(k, m)],
                            device_id=_peer(via), device_id_type=MESH).start()
            first += size
        token[...] = jnp.zeros_like(token)

    sem_shapes = [pltpu.SemaphoreType.DMA((size * (N_DEV - 1),)) for size in sizes for _ in range(2 * n_l)]
    n_sem = len(sem_shapes)
    outs = pl.pallas_call(
        body, name=name,
        out_shape=sem_shapes + [pltpu.HBM(a.shape, a.dtype) for a in srcs + lands] + [jax.ShapeDtypeStruct((8, 128), F32)],
        in_specs=[HBM] * (n_src + n),
        out_specs=[SEM] * n_sem + [HBM] * (n_src + n) + [pl.BlockSpec(memory_space=pltpu.VMEM)],
        input_output_aliases={k: n_sem + k for k in range(n_src + n)},
        compiler_params=pltpu.CompilerParams(has_side_effects=EFFECT),
    )(*[_in_hbm(a) for a in srcs + lands])
    sems, thru, token = outs[:n_sem], list(outs[n_sem:-1]), outs[-1]
    src_thru, land_thru = thru[:n_src], thru[n_src:]
    handles, first = [], 0
    for g, size in enumerate(sizes):
        pairs = [(sems[2 * (g * n_l + li)], sems[2 * (g * n_l + li) + 1]) for li in range(n_l)]
        mine = [src_thru.pop(0) for _ in range(size)] if groups[g][1] is not None else None
        handles.append((groups[g][0], pairs, mine, land_thru[first:first + size]))
        first += size
    return handles, token


def _exchange_wait(name, names, srcs, lands, waits, after):
    n = len(names)
    n_src = n if srcs is not None else 0

    def body(*refs):
        src_refs, land_refs = refs[:n_src], refs[n_src:n_src + n]
        sems = refs[n_src + n:n_src + n + 2 * len(waits)]
        me = _my_slot()
        for wi, (_, leg) in enumerate(waits):
            for m, via in leg.routes:
                for k in range(n):
                    src = land_refs[k] if leg.src_is_land else src_refs[k]
                    cp = pltpu.make_async_remote_copy(
                        src_ref=leg.src_of(names[k], src, me, m), dst_ref=leg.dst_of(names[k], land_refs[k], me ^ via, m),
                        send_sem=sems[2 * wi].at[_sem_index(k, m)], recv_sem=sems[2 * wi + 1].at[_sem_index(k, m)],
                        device_id=_peer(via), device_id_type=MESH)
                    cp.wait_send()
                    cp.wait_recv()

    arrays = (list(srcs) if srcs is not None else []) + list(lands)
    outs = pl.pallas_call(
        body, name=name,
        out_shape=[pltpu.HBM(a.shape, a.dtype) for a in arrays],
        in_specs=[HBM] * len(arrays) + [SEM] * (2 * len(waits)) + [pl.BlockSpec(memory_space=pl.ANY)],
        out_specs=[HBM] * len(arrays),
        input_output_aliases={k: k for k in range(len(arrays))},
        compiler_params=pltpu.CompilerParams(has_side_effects=EFFECT),
    )(*arrays, *[s for pair, _ in waits for s in pair], after)
    return (outs[:n_src] if srcs is not None else None), outs[n_src:]


def _exchange_relay(name, names, lands, arrived, arrived_leg, onward_leg, after):
    n = len(names)

    def body(*refs):
        land_refs = refs[:n]
        was_send, was_recv = refs[n], refs[n + 1]
        send, recv = refs[n + 3], refs[n + 4]
        token = refs[-1]
        me = _my_slot()
        for m, via in arrived_leg.routes:
            for k in range(n):
                cp = pltpu.make_async_remote_copy(
                    src_ref=arrived_leg.src_of(names[k], land_refs[k], me, m),
                    dst_ref=arrived_leg.dst_of(names[k], land_refs[k], me ^ via, m),
                    send_sem=was_send.at[_sem_index(k, m)], recv_sem=was_recv.at[_sem_index(k, m)],
                    device_id=_peer(via), device_id_type=MESH)
                cp.wait_send()
                cp.wait_recv()
        for m, via in onward_leg.routes:
            for k in range(n):
                pltpu.make_async_remote_copy(
                    src_ref=onward_leg.src_of(names[k], land_refs[k], me, m),
                    dst_ref=onward_leg.dst_of(names[k], land_refs[k], me, m),
                    send_sem=send.at[_sem_index(k, m)], recv_sem=recv.at[_sem_index(k, m)],
                    device_id=_peer(via), device_id_type=MESH).start()
        token[...] = jnp.zeros_like(token)

    sem_shape = pltpu.SemaphoreType.DMA((n * (N_DEV - 1),))
    outs = pl.pallas_call(
        body, name=name,
        out_shape=[sem_shape, sem_shape] + [pltpu.HBM(a.shape, a.dtype) for a in lands] + [jax.ShapeDtypeStruct((8, 128), F32)],
        in_specs=[HBM] * n + [SEM, SEM, ANY],
        out_specs=[SEM, SEM] + [HBM] * n + [pl.BlockSpec(memory_space=pltpu.VMEM)],
        input_output_aliases={k: 2 + k for k in range(n)},
        compiler_params=pltpu.CompilerParams(has_side_effects=EFFECT),
    )(*lands, *arrived, after)
    return (outs[0], outs[1]), outs[2:2 + n], outs[-1]


def _shard_slot(name, ref, slot):
    return SHARD_AT[name](ref, slot)


GATHER_ICI = Leg(((2, 2), (4, 4), (6, 6)), lambda name, ref, me, m: _shard_slot(name, ref, me),
                 lambda name, ref, sender, m: _shard_slot(name, ref, sender), True)
GATHER_D2D = Leg(((1, 1),), GATHER_ICI.src_of, GATHER_ICI.dst_of, True)
GATHER_FORWARD = Leg(((2, 1), (4, 1), (6, 1)), lambda name, ref, me, m: _shard_slot(name, ref, me ^ m),
                     lambda name, ref, sender, m: _shard_slot(name, ref, sender ^ m), True)
SCATTER = Leg(tuple((m, m) for m in range(1, N_DEV)), lambda name, ref, me, m: _shard_slot(name, ref, me ^ m),
              lambda name, ref, sender, m: ref.at[m - 1], False)


def _vec_exchange(vectors, after):
    n = len(vectors)

    def body(*refs):
        vec, vec_out = refs[n + 2], refs[n + 1]
        send_sems, recv_sems, local_sem = refs[n + 3:]
        for k in range(n):
            vec[k:k + 1, :] = refs[k][...]
        me = _my_slot()
        local = pltpu.make_async_copy(vec, vec_out.at[me], local_sem)
        local.start()
        sends = []
        for mask in range(1, N_DEV):
            cp = pltpu.make_async_remote_copy(
                src_ref=vec, dst_ref=vec_out.at[me], send_sem=send_sems.at[mask - 1],
                recv_sem=recv_sems.at[mask - 1], device_id=_peer(mask), device_id_type=MESH)
            cp.start()
            sends.append(cp)
        for mask in range(1, N_DEV):
            pltpu.make_async_remote_copy(
                src_ref=vec, dst_ref=vec_out.at[me ^ mask], send_sem=send_sems.at[mask - 1],
                recv_sem=recv_sems.at[mask - 1], device_id=_peer(mask), device_id_type=MESH).wait_recv()
        for cp in sends:
            cp.wait_send()
        local.wait()

    return pl.pallas_call(
        body, name="vec_exchange",
        out_shape=jax.ShapeDtypeStruct((N_DEV, n, D), F32),
        in_specs=[pl.BlockSpec(memory_space=pltpu.VMEM)] * n + [ANY],
        out_specs=pl.BlockSpec(memory_space=pl.ANY),
        scratch_shapes=[pltpu.VMEM((n, D), F32), pltpu.SemaphoreType.DMA((N_DEV - 1,)),
                        pltpu.SemaphoreType.DMA((N_DEV - 1,)), pltpu.SemaphoreType.DMA],
    )(*vectors, after)


def _adamw_update(g, w_ref, m_ref, v_ref, g_out, d_out, m_out, v_out):
    m_new = ADAM_B1 * m_ref[...] + (1.0 - ADAM_B1) * g
    v_new = ADAM_B2 * v_ref[...] + (1.0 - ADAM_B2) * (g * g)
    m_hat = m_new / (1.0 - ADAM_B1 ** ADAM_STEP)
    v_hat = v_new / (1.0 - ADAM_B2 ** ADAM_STEP)
    g_out[...] = g
    d_out[...] = -ADAM_LR * (m_hat / (jnp.sqrt(v_hat) + ADAM_EPS) + ADAM_WD * w_ref[...])
    m_out[...] = m_new
    v_out[...] = v_new


ADAMW_ROWS = 256


def _adamw(name, own, parts, w, m, v, slot, token):
    shard = w.shape
    if name in ("w_in", "w_ff1"):
        tr = ADAMW_ROWS
        grid = (shard[0] // tr,)
        own_spec = pl.BlockSpec((tr, shard[1]), lambda i, s: (i, s[0]))
        blk = pl.BlockSpec((tr, shard[1]), lambda i, s: (i, 0))
        parts_spec = pl.BlockSpec((N_DEV - 1, tr, shard[1]), lambda i, s: (0, i, 0))
    elif name == "pool_w":
        grid = (shard[0],)
        own_spec = pl.BlockSpec((None,) + shard[1:], lambda g, s: (g, s[0], 0))
        blk = pl.BlockSpec((None,) + shard[1:], lambda g, s: (g, 0, 0))
        parts_spec = pl.BlockSpec((N_DEV - 1, None) + shard[1:], lambda g, s: (0, g, 0, 0))
    elif name == "dw_kernel":
        grid = (1,)
        own_spec = pl.BlockSpec((None,) + shard, lambda i, s: (s[0], 0, 0))
        blk = pl.BlockSpec(shard, lambda i, s: (0, 0))
        parts_spec = pl.BlockSpec((N_DEV - 1,) + shard, lambda i, s: (0, 0, 0))
    else:
        tr = min(ADAMW_ROWS, shard[0])
        grid = (shard[0] // tr,)
        own_spec = pl.BlockSpec((tr, shard[1]), lambda i, s: (s[0] * grid[0] + i, 0))
        blk = pl.BlockSpec((tr, shard[1]), lambda i, s: (i, 0))
        parts_spec = pl.BlockSpec((N_DEV - 1, tr, shard[1]), lambda i, s: (0, i, 0))

    def body(slot_ref, own_ref, p_ref, w_ref, m_ref, v_ref, _token, g_out, d_out, m_out, v_out):
        g = own_ref[...].astype(F32)
        for k in range(N_DEV - 1):
            g = g + p_ref[k].astype(F32)
        _adamw_update(g, w_ref, m_ref, v_ref, g_out, d_out, m_out, v_out)

    return pl.pallas_call(
        body, name="adamw_" + name,
        grid_spec=pltpu.PrefetchScalarGridSpec(
            num_scalar_prefetch=1, grid=grid, in_specs=[own_spec, parts_spec, blk, blk, blk, ANY], out_specs=[blk] * 4),
        out_shape=[jax.ShapeDtypeStruct(shard, F32)] * 4,
        compiler_params=_params(("arbitrary",), 32),
    )(slot, *_pinned(own, parts, w, m, v, token))


def _adamw_vectors(parts, w, m, v):
    n_vec = w.shape[0]

    def body(p_ref, w_ref, m_ref, v_ref, g_out, d_out, m_out, v_out, loss_out):
        total = p_ref[0]
        for s in range(1, N_DEV):
            total = total + p_ref[s]
        _adamw_update(total[0:n_vec], w_ref, m_ref, v_ref, g_out, d_out, m_out, v_out)
        loss_out[...] = total[n_vec:n_vec + 1] * (0.5 / D)

    return pl.pallas_call(
        body, name="adamw_vectors",
        out_shape=[jax.ShapeDtypeStruct(w.shape, F32)] * 4 + [jax.ShapeDtypeStruct((1, D), F32)],
    )(parts, w, m, v)


def _input_norm(x, g_pre, token):
    seq = x.shape[0]
    tm = 1024

    def body(x_ref, g_ref, _token, u_ref, ut_ref):
        xf = x_ref[...]
        u = (xf * _rms_scale(xf) * g_ref[...]).astype(BF16)
        u_ref[...] = u
        ut_ref[...] = u.T

    return pl.pallas_call(
        body, name="input_norm", grid=(seq // tm,),
        in_specs=[pl.BlockSpec((tm, D), lambda i: (i, 0)), pl.BlockSpec((1, D), lambda i: (0, 0)), ANY],
        out_specs=[pl.BlockSpec((tm, D), lambda i: (i, 0)), pl.BlockSpec((D, tm), lambda i: (0, i))],
        out_shape=[pltpu.HBM((seq, D), BF16), pltpu.HBM((D, seq), BF16)],
        compiler_params=_params(("arbitrary",), 40),
    )(*_pinned(x, g_pre, token))


IN_PROJ_TN = IN_COLS // 4


def _in_proj_fwd(name, u, w_in, blocks, proj, token):
    seq = u.shape[0]
    tm, tn = 1024, IN_PROJ_TN

    def body(_blocks, u_ref, w_ref, _token, _proj_in, proj_ref):
        proj_ref[...] = _dot(u_ref[...], w_ref[...]).astype(BF16)

    return pl.pallas_call(
        body, name=name,
        grid_spec=pltpu.PrefetchScalarGridSpec(
            num_scalar_prefetch=1, grid=(seq // tm, blocks.shape[0]),
            in_specs=[pl.BlockSpec((tm, D), lambda i, j, b: (i, 0)), pl.BlockSpec((D, tn), lambda i, j, b: (0, b[j])),
                      ANY, ANY],
            out_specs=pl.BlockSpec((tm, tn), lambda i, j, b: (i, b[j]))),
        out_shape=pltpu.HBM((seq, IN_COLS), BF16),
        input_output_aliases={4: 0},
        compiler_params=_params(("arbitrary", "arbitrary"), 40),
    )(blocks, *_pinned(u, w_in, token, proj))


CONV_TM = 512
CONV_RS = 128


def _shifts():
    return [(b, [(a, 8 * a + b) for a in range(4) if 8 * a + b < TAPS]) for b in range(8)]


def _taps_looking_back(buf, row0, lanes, weight):
    acc = None
    for b, group in _shifts():
        part = None
        for a, s in group:
            term = weight(TAPS - 1 - s) * buf[pl.ds(row0 - 8 - 8 * a, CONV_RS + 8), lanes]
            part = term if part is None else part + term
        if b:
            part = pltpu.roll(part, b, 0)
        acc = part[8:, :] if acc is None else acc + part[8:, :]
    return acc


def _taps_looking_ahead(buf, row0, lanes, weight):
    acc = None
    for b, group in _shifts():
        part = None
        for a, s in group:
            term = weight(TAPS - 1 - s) * buf[pl.ds(row0 + 8 * a, CONV_RS + 8), lanes]
            part = term if part is None else part + term
        if b:
            part = pltpu.roll(part, CONV_RS + 8 - b, 0)
        acc = part[:CONV_RS, :] if acc is None else acc + part[:CONV_RS, :]
    return acc


def _layer_norm_parts(cv):
    mu = jnp.mean(cv, axis=-1, keepdims=True)
    cen = cv - mu
    rstd = lax.rsqrt(jnp.mean(cen * cen, axis=-1, keepdims=True) + LN_EPS)
    return cen * rstd, rstd


def _conv_fwd(proj, dw, dw_bias, ln_g, ln_b, w_conv_out, token):
    seq = proj.shape[0]
    tm = CONV_TM

    def body(a_ref, gate_ref, dw_ref, bias_ref, lg_ref, lb_ref, w_ref, _token, cv_ref, y_ref, ug):
        i = pl.program_id(0)

        @pl.when(i == 0)
        def _():
            ug[0:HALO, :] = jnp.zeros((HALO, D), F32)

        @pl.when(i > 0)
        def _():
            ug[0:HALO, :] = ug[tm:tm + HALO, :]
        ug[HALO:HALO + tm, :] = a_ref[...].astype(F32) * _sigmoid(gate_ref[...].astype(F32))

        def channel_block(cb, carry):
            lanes = pl.ds(pl.multiple_of(cb * 128, 128), 128)
            for r0 in range(0, tm, CONV_RS):
                taps = _taps_looking_back(ug, HALO + r0, lanes, lambda k: dw_ref[cb, k:k + 1, :])
                cv_ref[pl.ds(r0, CONV_RS), lanes] = taps + bias_ref[:, lanes]
            return carry
        lax.fori_loop(0, D // 128, channel_block, 0)

        n, _ = _layer_norm_parts(cv_ref[...])
        ln = n * lg_ref[...] + lb_ref[...]
        y_ref[...] = _dot((ln * _sigmoid(ln)).astype(BF16), w_ref[...]).astype(BF16)

    vec = pl.BlockSpec((1, D), lambda i: (0, 0))
    tile = pl.BlockSpec((tm, D), lambda i: (i, 0))
    return pl.pallas_call(
        body, name="conv_fwd", grid=(seq // tm,),
        in_specs=[pl.BlockSpec((tm, D), lambda i: (i, 0)), pl.BlockSpec((tm, D), lambda i: (i, 1)),
                  pl.BlockSpec((N_DEV, TAPS, 128), lambda i: (0, 0, 0)), vec, vec, vec,
                  pl.BlockSpec((D, D), lambda i: (0, 0)), ANY],
        out_specs=[tile, tile],
        out_shape=[pltpu.HBM((seq, D), F32), pltpu.HBM((seq, D), BF16)],
        scratch_shapes=[pltpu.VMEM((HALO + tm, D), F32)],
        compiler_params=_params(("arbitrary",), 32),
    )(*_pinned(proj, proj, dw, dw_bias, ln_g, ln_b, w_conv_out, token))


def _window_sums(rows, window, back):
    n = rows.shape[0]
    span = 1
    while span < window:
        rows = rows + pltpu.roll(rows, span if back else n - span, 0)
        span *= 2
    return rows


def _pool_counts(tile_index, tm, window):
    t = tile_index * tm + lax.broadcasted_iota(jnp.int32, (tm, 1), 0)
    return 1.0 / jnp.minimum(t + 1, window).astype(F32)


def _pool_merge_fwd(proj, y_conv, x, pool_w, pool_scale, w_pool_out, w_o, g_post, token):
    seq = proj.shape[0]
    tm = CONV_TM

    def body(p_ref, gc_ref, gp_ref, yc_ref, x_ref, pw_ref, ps_ref, wpo_ref, wo_ref, g_ref, _token,
             z_ref, zl_ref, yp_ref, mg_ref, o_ref, h1_ref, pbuf, zl_buf):
        i = pl.program_id(0)

        @pl.when(i == 0)
        def _():
            pbuf[0:HALO, :] = jnp.zeros((HALO, D), F32)

        @pl.when(i > 0)
        def _():
            pbuf[0:HALO, :] = pbuf[tm:tm + HALO, :]
        pbuf[HALO:HALO + tm, :] = p_ref[...].astype(F32)

        for g, window in enumerate(POOL_WINDOWS):
            lanes = pl.ds(g * PG, PG)
            acc = _window_sums(pbuf[:, lanes], window, back=True)[HALO:, :]
            zg = acc * _pool_counts(i, tm, window) - pbuf[pl.ds(HALO, tm), lanes]
            z_ref[:, lanes] = zg.astype(BF16)
            zl_buf[:, lanes] = _dot(zg.astype(BF16), pw_ref[g])
        zl = zl_buf[...]
        zl_ref[...] = zl.astype(BF16)
        y_pool = _dot((zl * ps_ref[...]).astype(BF16), wpo_ref[...])
        yp_ref[...] = y_pool.astype(BF16)
        merged = (_sigmoid(gc_ref[...].astype(F32)) * yc_ref[...].astype(F32)
                  + _sigmoid(gp_ref[...].astype(F32)) * y_pool).astype(BF16)
        mg_ref[...] = merged
        o = _dot(merged, wo_ref[...])
        o_ref[...] = o
        h1_ref[...] = x_ref[...] + o * _rms_scale(o) * g_ref[...]

    vec = pl.BlockSpec((1, D), lambda i: (0, 0))
    tile = pl.BlockSpec((tm, D), lambda i: (i, 0))
    mat = pl.BlockSpec((D, D), lambda i: (0, 0))
    return pl.pallas_call(
        body, name="pool_merge_fwd", grid=(seq // tm,),
        in_specs=[pl.BlockSpec((tm, D), lambda i: (i, 2)), pl.BlockSpec((tm, D), lambda i: (i, 3)),
                  pl.BlockSpec((tm, D), lambda i: (i, 4)), tile, tile,
                  pl.BlockSpec((4, PG, PG), lambda i: (0, 0, 0)), vec, mat, mat, vec, ANY],
        out_specs=[tile] * 6,
        out_shape=[pltpu.HBM((seq, D), dt) for dt in (BF16, BF16, BF16, BF16, F32, F32)],
        scratch_shapes=[pltpu.VMEM((HALO + tm, D), F32), pltpu.VMEM((tm, D), F32)],
        compiler_params=_params(("arbitrary",), 48),
    )(*_pinned(proj, proj, proj, y_conv, x, pool_w, pool_scale, w_pool_out, w_o, g_post, token))


def _mlp_fwd(h1, g_pre, w_ff1, w_ff2, g_post, target):
    seq = h1.shape[0]
    tm, tf = 1024, D_FF // N_DEV
    n_f = D_FF // tf

    def body(h1_ref, gpre_ref, w1_ref, w2_ref, gpost_ref, tgt_ref, v_ref, dm_ref, dh2_ref, sse_ref, ggrad_ref, macc):
        i, j = pl.program_id(0), pl.program_id(1)

        @pl.when(j == 0)
        def _():
            h = h1_ref[...]
            v_ref[...] = (h * _rms_scale(h) * gpre_ref[...]).astype(BF16)
        f = jnp.maximum(_dot(v_ref[...], w1_ref[...]), 0.0)
        part = _dot((f * f).astype(BF16), w2_ref[...])

        @pl.when(j == 0)
        def _():
            macc[...] = part

        @pl.when(j > 0)
        def _():
            macc[...] += part

        @pl.when(j == n_f - 1)
        def _():
            mo = macc[...]
            err = h1_ref[...] + mo * _rms_scale(mo) * gpost_ref[...] - tgt_ref[...]
            dh2 = err * (1.0 / D)
            dh2_ref[...] = dh2.astype(BF16)
            dm, ggrad = _rms_bwd(dh2, mo, gpost_ref[...])
            dm_ref[...] = dm.astype(BF16)
            _acc_out(ggrad_ref, i == 0, ggrad)
            _acc_out(sse_ref, i == 0, jnp.sum(jnp.sum(err * err, axis=1, keepdims=True), axis=0, keepdims=True))

    vec = pl.BlockSpec((1, D), lambda i, j: (0, 0))
    tile = pl.BlockSpec((tm, D), lambda i, j: (i, 0))
    return pl.pallas_call(
        body, name="mlp_fwd", grid=(seq // tm, n_f),
        in_specs=[tile, vec, pl.BlockSpec((D, tf), lambda i, j: (0, j)), pl.BlockSpec((tf, D), lambda i, j: (j, 0)), vec, tile],
        out_specs=[tile, tile, tile, pl.BlockSpec((1, 1), lambda i, j: (0, 0)), vec],
        out_shape=[pltpu.HBM((seq, D), BF16), pltpu.HBM((seq, D), BF16),
                   pltpu.HBM((seq, D), BF16), jax.ShapeDtypeStruct((1, 1), F32),
                   jax.ShapeDtypeStruct((1, D), F32)],
        scratch_shapes=[pltpu.VMEM((tm, D), F32)],
        compiler_params=_params(("arbitrary", "arbitrary"), 56),
    )(*_pinned(h1, g_pre, w_ff1, w_ff2, g_post, target))


def _mlp_bwd(v, dm, w_ff1, w_ff2):
    seq = v.shape[0]
    tm, tf = 1024, D_FF // N_DEV
    n_t = seq // tm

    def body(v_ref, dm_ref, w1_ref, w2_ref, dv_hbm, g1_ref, g2_ref, dv_acc, g1_acc, g2_acc, sems):
        j, i = pl.program_id(0), pl.program_id(1)
        vt, dmt = v_ref[...], dm_ref[...]
        f = jnp.maximum(_dot(vt, w1_ref[...]), 0.0)
        df = (_dot_nt(dmt, w2_ref[...]) * (2.0 * f)).astype(BF16)
        rows = pl.ds(pl.multiple_of(i * tm, tm), tm)
        dv_part = _dot_nt(df, w1_ref[...])

        @pl.when(j == 0)
        def _():
            dv_acc[rows, :] = dv_part

        @pl.when(j > 0)
        def _():
            dv_acc[rows, :] += dv_part
        g1_part = _dot_tn(vt, df)
        g2_part = _dot_tn((f * f).astype(BF16), dmt)

        @pl.when(i == 0)
        def _():
            g1_acc[...] = g1_part
            g2_acc[...] = g2_part

        @pl.when(i > 0)
        def _():
            g1_acc[...] += g1_part
            g2_acc[...] += g2_part

        @pl.when(i == n_t - 1)
        def _():
            g1_ref[...] = g1_acc[...].astype(BF16)
            g2_ref[...] = g2_acc[...].astype(BF16)

        last_pass = j == D_FF // tf - 1

        @pl.when(last_pass)
        def _():
            pltpu.make_async_copy(dv_acc.at[rows, :], dv_hbm.at[rows, :], sems.at[i]).start()

        @pl.when(jnp.logical_and(last_pass, i == n_t - 1))
        def _():
            for t in range(n_t):
                done = pl.ds(t * tm, tm)
                pltpu.make_async_copy(dv_acc.at[done, :], dv_hbm.at[done, :], sems.at[t]).wait()

    tile = pl.BlockSpec((tm, D), lambda j, i: (i, 0))
    return pl.pallas_call(
        body, name="mlp_bwd", grid=(D_FF // tf, n_t),
        in_specs=[tile, tile, pl.BlockSpec((D, tf), lambda j, i: (0, j)), pl.BlockSpec((tf, D), lambda j, i: (j, 0))],
        out_specs=[pl.BlockSpec(memory_space=pl.ANY), pl.BlockSpec((D, tf), lambda j, i: (0, j)),
                   pl.BlockSpec((tf, D), lambda j, i: (j, 0))],
        out_shape=[pltpu.HBM((seq, D), F32), pltpu.HBM((D, D_FF), BF16),
                   pltpu.HBM((D_FF, D), BF16)],
        scratch_shapes=[pltpu.VMEM((seq, D), F32), pltpu.VMEM((D, tf), F32), pltpu.VMEM((tf, D), F32),
                        pltpu.SemaphoreType.DMA((n_t,))],
        compiler_params=_params(("arbitrary", "arbitrary"), 52),
    )(*_pinned(v, dm, w_ff1, w_ff2))


def _merge_bwd(dh2, dv, h1, g_mlp_pre, o, g_mix_post, w_o, merged, proj, y_conv, y_pool, token):
    seq = dh2.shape[0]
    tm = 256
    n_t = seq // tm

    def body(dh2_ref, dv_ref, h1_ref, gpre_ref, o_ref, gpost_ref, wo_ref, mg_ref, gc_ref, gp_ref, yc_ref, yp_ref, _token,
             dh1_ref, dyc_ref, dyp_ref, dg_ref, gwo_ref, ggpre_ref, ggpost_ref, gwo_acc):
        i = pl.program_id(0)
        dnorm, ggpre = _rms_bwd(dv_ref[...], h1_ref[...], gpre_ref[...])
        dh1 = dh2_ref[...].astype(F32) + dnorm
        dh1_ref[...] = dh1
        do, ggpost = _rms_bwd(dh1, o_ref[...], gpost_ref[...])
        do = do.astype(BF16)
        _acc_out(ggpre_ref, i == 0, ggpre)
        _acc_out(ggpost_ref, i == 0, ggpost)
        _acc_out(gwo_acc, i == 0, _dot_tn(mg_ref[...], do))
        dmerged = _dot_nt(do, wo_ref[...])
        sc, sp = _sigmoid(gc_ref[...].astype(F32)), _sigmoid(gp_ref[...].astype(F32))
        dyc_ref[...] = (dmerged * sc).astype(BF16)
        dyp_ref[...] = (dmerged * sp).astype(BF16)
        dg_ref[:, 0:D] = (dmerged * yc_ref[...].astype(F32) * (sc * (1.0 - sc))).astype(BF16)
        dg_ref[:, D:2 * D] = (dmerged * yp_ref[...].astype(F32) * (sp * (1.0 - sp))).astype(BF16)

        @pl.when(i == n_t - 1)
        def _():
            gwo_ref[...] = gwo_acc[...].astype(BF16)

    vec = pl.BlockSpec((1, D), lambda i: (0, 0))
    tile = pl.BlockSpec((tm, D), lambda i: (i, 0))
    mat = pl.BlockSpec((D, D), lambda i: (0, 0))
    return pl.pallas_call(
        body, name="merge_bwd", grid=(n_t,),
        in_specs=[tile, tile, tile, vec, tile, vec, mat, tile,
                  pl.BlockSpec((tm, D), lambda i: (i, 3)), pl.BlockSpec((tm, D), lambda i: (i, 4)), tile, tile, ANY],
        out_specs=[tile, tile, tile, pl.BlockSpec((tm, 2 * D), lambda i: (i, 0)), mat, vec, vec],
        out_shape=[pltpu.HBM((seq, D), F32), pltpu.HBM((seq, D), BF16),
                   pltpu.HBM((seq, D), BF16), pltpu.HBM((seq, 2 * D), BF16),
                   pltpu.HBM((D, D), BF16), jax.ShapeDtypeStruct((1, D), F32),
                   jax.ShapeDtypeStruct((1, D), F32)],
        scratch_shapes=[pltpu.VMEM((D, D), F32)],
        compiler_params=_params(("arbitrary",), 48),
    )(*_pinned(dh2, dv, h1, g_mlp_pre, o, g_mix_post, w_o, merged, proj, proj, y_conv, y_pool, token))


def _pool_bwd(dy_pool, zl, z, pool_w, pool_scale, w_pool_out, token):
    seq = dy_pool.shape[0]
    tm = CONV_TM
    n_t = seq // tm

    def body(dy_ref, zl_ref, z_ref, pw_ref, ps_ref, wpo_ref, _token,
             dp_ref, gwpo_ref, gpw_ref, gps_ref, qbuf, gwpo_acc, gpw_acc):
        i = pl.program_id(0)
        tile_index = n_t - 1 - i
        first = i == 0
        dy = dy_ref[...]
        zl = zl_ref[...].astype(F32)
        dzs = _dot_nt(dy, wpo_ref[...])
        _acc_out(gwpo_acc, first, _dot_tn((zl * ps_ref[...]).astype(BF16), dy))
        _acc_out(gps_ref, first, jnp.sum(dzs * zl, axis=0, keepdims=True))
        dzl = (dzs * ps_ref[...]).astype(BF16)

        @pl.when(first)
        def _():
            qbuf[tm:tm + HALO, :] = jnp.zeros((HALO, D), F32)

        @pl.when(jnp.logical_not(first))
        def _():
            qbuf[tm:tm + HALO, :] = qbuf[0:HALO, :]

        dzs_list = []
        for g, window in enumerate(POOL_WINDOWS):
            lanes = pl.ds(g * PG, PG)
            dzl_g = dzl[:, g * PG:(g + 1) * PG]
            dz = _dot_nt(dzl_g, pw_ref[g])
            _acc_out(gpw_acc.at[g], first, _dot_tn(z_ref[:, lanes], dzl_g))
            qbuf[pl.ds(0, tm), lanes] = dz * _pool_counts(tile_index, tm, window)
            dzs_list.append(dz)
        for g, window in enumerate(POOL_WINDOWS):
            lanes = pl.ds(g * PG, PG)
            acc = _window_sums(qbuf[:, lanes], window, back=False)[:tm, :]
            dp_ref[:, lanes] = (acc - dzs_list[g]).astype(BF16)

        @pl.when(i == n_t - 1)
        def _():
            gwpo_ref[...] = gwpo_acc[...].astype(BF16)
            gpw_ref[...] = gpw_acc[...].astype(BF16)

    vec = pl.BlockSpec((1, D), lambda i: (0, 0))
    tile = pl.BlockSpec((tm, D), lambda i: (n_t - 1 - i, 0))
    mat = pl.BlockSpec((D, D), lambda i: (0, 0))
    pw = pl.BlockSpec((4, PG, PG), lambda i: (0, 0, 0))
    return pl.pallas_call(
        body, name="pool_bwd", grid=(n_t,),
        in_specs=[tile, tile, tile, pw, vec, mat, ANY],
        out_specs=[tile, mat, pw, vec],
        out_shape=[pltpu.HBM((seq, D), BF16), pltpu.HBM((D, D), BF16),
                   pltpu.HBM((4, PG, PG), BF16), jax.ShapeDtypeStruct((1, D), F32)],
        scratch_shapes=[pltpu.VMEM((tm + HALO, D), F32), pltpu.VMEM((D, D), F32), pltpu.VMEM((4, PG, PG), F32)],
        compiler_params=_params(("arbitrary",), 40),
    )(*_pinned(dy_pool, zl, z, pool_w, pool_scale, w_pool_out, token))


def _conv_bwd(dy_conv, cv, proj, dw, ln_g, ln_b, w_conv_out, token):
    seq = dy_conv.shape[0]
    tm = CONV_TM // 2
    n_t = seq // tm
    halo_blocks = tm // HALO

    def body(dy_ref, cv_ref, a_ref, gate_ref, ah_ref, gh_ref, dw_ref, lg_ref, lb_ref, w_ref, _token,
             dglu_ref, gw_ref, gdw_ref, gbias_ref, glg_ref, glb_ref, ug, dcv, dug, gw_acc):
        i = pl.program_id(0)
        tile_index = n_t - 1 - i
        first = i == 0
        dy = dy_ref[...]
        n, rstd = _layer_norm_parts(cv_ref[...])
        ln = n * lg_ref[...] + lb_ref[...]
        sg = _sigmoid(ln)
        _acc_out(gw_acc, first, _dot_tn((ln * sg).astype(BF16), dy))
        dln = _dot_nt(dy, w_ref[...]) * (sg * (1.0 + ln * (1.0 - sg)))
        _acc_out(glg_ref, first, jnp.sum(dln * n, axis=0, keepdims=True))
        _acc_out(glb_ref, first, jnp.sum(dln, axis=0, keepdims=True))
        dn = dln * lg_ref[...]
        dcv_tile = rstd * (dn - jnp.mean(dn, axis=-1, keepdims=True) - n * jnp.mean(dn * n, axis=-1, keepdims=True))
        _acc_out(gbias_ref, first, jnp.sum(dcv_tile, axis=0, keepdims=True))

        @pl.when(first)
        def _():
            dcv[tm:tm + HALO, :] = jnp.zeros((HALO, D), F32)

        @pl.when(jnp.logical_not(first))
        def _():
            dcv[tm:tm + HALO, :] = dcv[0:HALO, :]
        dcv[0:tm, :] = dcv_tile

        a, gate = a_ref[...].astype(F32), gate_ref[...].astype(F32)
        sgate = _sigmoid(gate)
        ug[HALO:HALO + tm, :] = a * sgate
        before = jnp.where(tile_index > 0, 1.0, 0.0)
        ug[0:HALO, :] = ah_ref[...].astype(F32) * _sigmoid(gh_ref[...].astype(F32)) * before

        @pl.when(first)
        def _():
            gdw_ref[...] = jnp.zeros((N_DEV, TAPS + 1, 128), F32)

        def channel_block(cb, carry):
            lanes = pl.ds(pl.multiple_of(cb * 128, 128), 128)
            for r0 in range(0, tm, CONV_RS):
                dug[pl.ds(r0, CONV_RS), lanes] = _taps_looking_ahead(dcv, r0, lanes, lambda k: dw_ref[cb, k:k + 1, :])
            for b, group in _shifts():
                sums = [jnp.zeros((8, 128), F32) for _ in group]
                for r0 in range(0, tm, CONV_RS):
                    window = ug[pl.ds(r0, CONV_RS + HALO), lanes]
                    if b:
                        window = pltpu.roll(window, b, 0)
                    d = dcv[pl.ds(r0, CONV_RS), lanes]
                    for n_a, (a, s) in enumerate(group):
                        prod = d * window[HALO - 8 * a:HALO - 8 * a + CONV_RS, :]
                        sums[n_a] = sums[n_a] + jnp.sum(prod.reshape(CONV_RS // 8, 8, 128), axis=0)
                for n_a, (a, s) in enumerate(group):
                    k = TAPS - 1 - s
                    gdw_ref[cb, k:k + 1, :] += jnp.sum(sums[n_a], axis=0, keepdims=True)
            return carry
        lax.fori_loop(0, D // 128, channel_block, 0)

        d_ug = dug[...]
        dglu_ref[:, 0:D] = (d_ug * sgate).astype(BF16)
        dglu_ref[:, D:2 * D] = (d_ug * a * (sgate * (1.0 - sgate))).astype(BF16)

        @pl.when(i == n_t - 1)
        def _():
            gw_ref[...] = gw_acc[...].astype(BF16)

    def halo_index(col):
        return lambda i: (jnp.maximum((n_t - 1 - i) * halo_blocks - 1, 0), col)

    vec = pl.BlockSpec((1, D), lambda i: (0, 0))
    tile = pl.BlockSpec((tm, D), lambda i: (n_t - 1 - i, 0))
    mat = pl.BlockSpec((D, D), lambda i: (0, 0))
    dwspec = pl.BlockSpec((N_DEV, TAPS, 128), lambda i: (0, 0, 0))
    return pl.pallas_call(
        body, name="conv_bwd", grid=(n_t,),
        in_specs=[tile, tile, pl.BlockSpec((tm, D), lambda i: (n_t - 1 - i, 0)), pl.BlockSpec((tm, D), lambda i: (n_t - 1 - i, 1)),
                  pl.BlockSpec((HALO, D), halo_index(0)), pl.BlockSpec((HALO, D), halo_index(1)), dwspec, vec, vec, mat,
                  ANY],
        out_specs=[pl.BlockSpec((tm, 2 * D), lambda i: (n_t - 1 - i, 0)), mat,
                   pl.BlockSpec((N_DEV, TAPS + 1, 128), lambda i: (0, 0, 0)), vec, vec, vec],
        out_shape=[pltpu.HBM((seq, 2 * D), BF16), pltpu.HBM((D, D), BF16),
                   pltpu.HBM((N_DEV, TAPS + 1, 128), F32), jax.ShapeDtypeStruct((1, D), F32),
                   jax.ShapeDtypeStruct((1, D), F32), jax.ShapeDtypeStruct((1, D), F32)],
        scratch_shapes=[pltpu.VMEM((HALO + tm, D), F32), pltpu.VMEM((tm + HALO, D), F32), pltpu.VMEM((tm, D), F32),
                        pltpu.VMEM((D, D), F32)],
        compiler_params=_params(("arbitrary",), 48),
    )(*_pinned(dy_conv, cv, proj, proj, proj, proj, dw, ln_g, ln_b, w_conv_out, token))


def _in_proj_bwd_x(d_glu, dp, dgates, w_in, x, g_pre, dh1, token):
    seq = x.shape[0]
    tm = 512

    def body(dglu_ref, dp_ref, dg_ref, w_ref, x_ref, g_ref, dh1_ref, _token, dx_ref, gg_ref):
        du = _dot_nt(dglu_ref[...], w_ref[:, 0:2 * D])
        du += _dot_nt(dp_ref[...], w_ref[:, 2 * D:3 * D])
        du += _dot_nt(dg_ref[...], w_ref[:, 3 * D:5 * D])
        dnorm, gg = _rms_bwd(du, x_ref[...], g_ref[...])
        dx_ref[...] = dh1_ref[...] + dnorm
        _acc_out(gg_ref, pl.program_id(0) == 0, gg)

    vec = pl.BlockSpec((1, D), lambda i: (0, 0))
    tile = pl.BlockSpec((tm, D), lambda i: (i, 0))
    wide = pl.BlockSpec((tm, 2 * D), lambda i: (i, 0))
    return pl.pallas_call(
        body, name="in_proj_bwd_x", grid=(seq // tm,),
        in_specs=[wide, tile, wide, pl.BlockSpec((D, IN_COLS), lambda i: (0, 0)), tile, vec, tile, ANY],
        out_specs=[tile, vec],
        out_shape=[pltpu.HBM((seq, D), F32), jax.ShapeDtypeStruct((1, D), F32)],
        compiler_params=_params(("arbitrary",), 48),
    )(*_pinned(d_glu, dp, dgates, w_in, x, g_pre, dh1, token))


def _in_proj_bwd_w(u_t, d_glu, dp, dgates, token):
    seq = u_t.shape[1]
    tm = 2048
    n_t = seq // tm

    def body(u_ref, dglu_ref, dp_ref, dg_ref, _token, out_ref, acc):
        b, i = pl.program_id(0), pl.program_id(1)
        ut = u_ref[...]

        def add(d_ref):
            _acc_out(acc, i == 0, _dot(ut, d_ref[...]))

        pl.when(b < 2)(lambda: add(dglu_ref))
        pl.when(b == 2)(lambda: add(dp_ref))
        pl.when(b > 2)(lambda: add(dg_ref))

        @pl.when(i == n_t - 1)
        def _():
            out_ref[...] = acc[...].astype(BF16)

    return pl.pallas_call(
        body, name="in_proj_bwd_w", grid=(IN_COLS // D, n_t),
        in_specs=[pl.BlockSpec((D, tm), lambda b, i: (0, i)),
                  pl.BlockSpec((tm, D), lambda b, i: (jnp.where(b < 2, i, 0), jnp.minimum(b, 1))),
                  pl.BlockSpec((tm, D), lambda b, i: (jnp.where(b == 2, i, 0), 0)),
                  pl.BlockSpec((tm, D), lambda b, i: (jnp.where(b > 2, i, 0), jnp.maximum(b - 3, 0))), ANY],
        out_specs=pl.BlockSpec((D, D), lambda b, i: (0, b)),
        out_shape=pltpu.HBM((D, IN_COLS), BF16),
        scratch_shapes=[pltpu.VMEM((D, D), F32)],
        compiler_params=_params(("arbitrary", "arbitrary"), 52),
    )(*_pinned(u_t, d_glu, dp, dgates, token))


VEC_NAMES = ("mix_pre_g", "dw_bias", "conv_ln_g", "conv_ln_b", "pool_scale", "mix_post_g", "mlp_pre_g", "mlp_post_g")
WEIGHT_ORDER = ("mix_pre_g", "w_in", "dw_kernel", "dw_bias", "conv_ln_g", "conv_ln_b", "w_conv_out", "pool_w",
                "pool_scale", "w_pool_out", "w_o", "mix_post_g", "mlp_pre_g", "w_ff1", "w_ff2", "mlp_post_g")


def _step(x, loss_target, w, m, v):
    row = lambda a: a.reshape(1, D)
    names = [s[0] for s in SHARDED]

    gather_groups = (("w_in",), ("w_conv_out", "dw_kernel"), ("pool_w", "w_pool_out", "w_o"), ("w_ff1", "w_ff2"))
    legs = [GATHER_ICI, GATHER_D2D]
    first_level, token = _exchange_start(
        "gather_start_w_in", [(gather_groups[0], None, _stage_shards(gather_groups[0], [w["w_in"]], x))], legs)
    later = [n for g in gather_groups[1:] for n in g]
    seeded = dict(zip(later, _stage_shards(later, [w[n] for n in later], token)))
    rest, token = _exchange_start("gather_start", [(g, None, [seeded[n] for n in g]) for g in gather_groups[1:]], legs)
    first_level = first_level + rest
    forwarded, full = {}, {}

    def forward(k, after):
        group, (ici_sems, d2d_sems), _, landed = first_level[k]
        onward, landed, tok = _exchange_relay("gather_forward_" + group[0], group, landed, ici_sems, GATHER_ICI,
                                              GATHER_FORWARD, after)
        forwarded[k] = (landed, [(d2d_sems, GATHER_D2D), (onward, GATHER_FORWARD)])
        return tok

    def gathered(k, after):
        group = gather_groups[k]
        landed, waits = forwarded[k]
        _, arrays = _exchange_wait("gather_wait_" + group[0], group, None, landed, waits, after)
        full.update(zip(group, arrays))

    u, u_t = _input_norm(x, row(w["mix_pre_g"]), token)
    group, (ici_sems, d2d_sems), _, landed = first_level[0]
    _, landed = _exchange_wait("gather_d2d_w_in", group, None, landed, [(d2d_sems, GATHER_D2D)], u)
    home = jnp.reshape(_my_slot() // 2, (1,)).astype(jnp.int32)
    proj = _in_proj_fwd("in_proj_fwd_home", u, landed[0], home, lax.empty((x.shape[0], IN_COLS), BF16), u)
    onward, landed, tok = _exchange_relay("gather_forward_w_in", group, landed, ici_sems, GATHER_ICI, GATHER_FORWARD, proj)
    (full["w_in"],) = _exchange_wait("gather_wait_w_in", group, None, landed, [(onward, GATHER_FORWARD)], tok)[1]
    tok = forward(1, full["w_in"])
    away = jnp.stack([home[0] ^ 1, home[0] ^ 2, home[0] ^ 3]).astype(jnp.int32)
    proj = _in_proj_fwd("in_proj_fwd", u, full["w_in"], away, proj, tok)
    gathered(1, proj)
    tok = forward(2, proj)
    cv, y_conv = _conv_fwd(proj, full["dw_kernel"], row(w["dw_bias"]), row(w["conv_ln_g"]), row(w["conv_ln_b"]),
                           full["w_conv_out"], tok)
    gathered(2, y_conv)
    tok = forward(3, y_conv)
    z, zl, y_pool, merged, o, h1 = _pool_merge_fwd(proj, y_conv, x, full["pool_w"], row(w["pool_scale"]),
                                                   full["w_pool_out"], full["w_o"], row(w["mix_post_g"]), tok)
    gathered(3, h1)
    vv, dm, dh2, sse, g_mlp_post = _mlp_fwd(h1, row(w["mlp_pre_g"]), full["w_ff1"], full["w_ff2"],
                                            row(w["mlp_post_g"]), loss_target)

    shard_of = {name: shard for name, _, shard, _ in SHARDED}

    def scatter_start(tag, group, grads):
        landings = [lax.empty((N_DEV - 1,) + shard_of[n], g.dtype) for n, g in zip(group, grads)]
        (handle,), tok = _exchange_start("scatter_start_" + tag, [(group, grads, landings)], [SCATTER])
        return handle, tok

    dv, g_ff1, g_ff2 = _mlp_bwd(vv, dm, full["w_ff1"], full["w_ff2"])
    h_ff, tok_ff = scatter_start("ff", ("w_ff1", "w_ff2"), [g_ff1, g_ff2])
    dh1, dy_conv, dy_pool, dgates, g_wo, g_mlp_pre, g_mix_post = _merge_bwd(
        dh2, dv, h1, row(w["mlp_pre_g"]), o, row(w["mix_post_g"]), full["w_o"], merged, proj, y_conv, y_pool, tok_ff)
    dp, g_wpo, g_pw, g_pool_scale = _pool_bwd(dy_pool, zl, z, full["pool_w"], row(w["pool_scale"]), full["w_pool_out"],
                                              dh1)
    h_pool, tok_pool = scatter_start("pool", ("w_o", "w_pool_out", "pool_w"), [g_wo, g_wpo, g_pw])
    d_glu, g_wco, g_dw, g_bias, g_ln_g, g_ln_b = _conv_bwd(dy_conv, cv, proj, full["dw_kernel"], row(w["conv_ln_g"]),
                                                            row(w["conv_ln_b"]), full["w_conv_out"], tok_pool)
    h_conv, tok_conv = scatter_start("conv", ("w_conv_out", "dw_kernel"), [g_wco, g_dw[:, :TAPS, :]])
    g_win = _in_proj_bwd_w(u_t, d_glu, dp, dgates, tok_conv)
    h_in, tok_in = scatter_start("in", ("w_in",), [g_win])
    grad_x, g_mix_pre = _in_proj_bwd_x(d_glu, dp, dgates, full["w_in"], x, row(w["mix_pre_g"]), dh1, tok_in)

    out = {}
    slot = jnp.reshape(_my_slot(), (1,)).astype(jnp.int32)
    after = grad_x
    for tag, (group, (sems,), grads, landings) in (("ff", h_ff), ("pool", h_pool), ("conv", h_conv), ("in", h_in)):
        if tag == "in":
            vec_parts = after = _vec_exchange(
                [g_mix_pre, g_bias, g_ln_g, g_ln_b, g_pool_scale, g_mix_post, g_mlp_pre, g_mlp_post,
                 jnp.broadcast_to(sse, (1, D))], after)
        mine, landed = _exchange_wait("scatter_wait_" + tag, group, grads, landings, [(sems, SCATTER)], after)
        for name, own, parts in zip(group, mine, landed):
            out[name] = _adamw(name, own, parts, w[name], m[name], v[name], slot, after)
            after = out[name][0]
    stack = lambda d: jnp.stack([d[n] for n in VEC_NAMES], axis=0)
    *res, loss_row = _adamw_vectors(vec_parts, stack(w), stack(m), stack(v))
    for k, name in enumerate(VEC_NAMES):
        out[name] = [r[k] for r in res]
    return loss_row[0, 0], grad_x, out


def kernel(x, mix_pre_g, w_in, dw_kernel, dw_bias, conv_ln_g, conv_ln_b, w_conv_out, pool_w, pool_scale, w_pool_out, w_o, mix_post_g, mlp_pre_g, w_ff1, w_ff2, mlp_post_g, loss_target, m_mix_pre_g, m_w_in, m_dw_kernel, m_dw_bias, m_conv_ln_g, m_conv_ln_b, m_w_conv_out, m_pool_w, m_pool_scale, m_w_pool_out, m_w_o, m_mix_post_g, m_mlp_pre_g, m_w_ff1, m_w_ff2, m_mlp_post_g, v_mix_pre_g, v_w_in, v_dw_kernel, v_dw_bias, v_conv_ln_g, v_conv_ln_b, v_w_conv_out, v_pool_w, v_pool_scale, v_w_pool_out, v_w_o, v_mix_post_g, v_mlp_pre_g, v_w_ff1, v_w_ff2, v_mlp_post_g):
    w = dict(mix_pre_g=mix_pre_g, w_in=w_in, dw_kernel=dw_kernel, dw_bias=dw_bias, conv_ln_g=conv_ln_g, conv_ln_b=conv_ln_b,
             w_conv_out=w_conv_out, pool_w=pool_w, pool_scale=pool_scale, w_pool_out=w_pool_out, w_o=w_o,
             mix_post_g=mix_post_g, mlp_pre_g=mlp_pre_g, w_ff1=w_ff1, w_ff2=w_ff2, mlp_post_g=mlp_post_g)
    m = dict(mix_pre_g=m_mix_pre_g, w_in=m_w_in, dw_kernel=m_dw_kernel, dw_bias=m_dw_bias, conv_ln_g=m_conv_ln_g,
             conv_ln_b=m_conv_ln_b, w_conv_out=m_w_conv_out, pool_w=m_pool_w, pool_scale=m_pool_scale,
             w_pool_out=m_w_pool_out, w_o=m_w_o, mix_post_g=m_mix_post_g, mlp_pre_g=m_mlp_pre_g, w_ff1=m_w_ff1,
             w_ff2=m_w_ff2, mlp_post_g=m_mlp_post_g)
    v = dict(mix_pre_g=v_mix_pre_g, w_in=v_w_in, dw_kernel=v_dw_kernel, dw_bias=v_dw_bias, conv_ln_g=v_conv_ln_g,
             conv_ln_b=v_conv_ln_b, w_conv_out=v_w_conv_out, pool_w=v_pool_w, pool_scale=v_pool_scale,
             w_pool_out=v_w_pool_out, w_o=v_w_o, mix_post_g=v_mix_post_g, mlp_pre_g=v_mlp_pre_g, w_ff1=v_w_ff1,
             w_ff2=v_w_ff2, mlp_post_g=v_mlp_post_g)
    seq = x.shape[1]
    loss, grad_x, out = _step(x.reshape(seq, D), loss_target.reshape(seq, D), w, m, v)
    grads, deltas, new_m, new_v = ([out[n][k] for n in WEIGHT_ORDER] for k in range(4))
    return (loss, grad_x.reshape(x.shape), *grads, *deltas, *new_m, *new_v)
```

```python
import collections

import jax
import jax.numpy as jnp
from jax import lax
from jax.experimental import pallas as pl
from jax.experimental.pallas import tpu as pltpu

D = 1024
D_FF = 4 * D
IN_COLS = 5 * D
TAPS = 31
HALO = 32
POOL_WINDOWS = (2, 4, 8, 16)
PG = D // 4
N_DEV = 8
RMS_EPS = 1e-6
LN_EPS = 1e-5
ADAM_LR, ADAM_B1, ADAM_B2, ADAM_EPS, ADAM_WD, ADAM_STEP = 0.001, 0.9, 0.999, 1e-08, 0.01, 10

BF16 = jnp.bfloat16
F32 = jnp.float32
MIB = 1 << 20
MESH = pl.DeviceIdType.MESH


def _params(sem, vmem_mib):
    return pltpu.CompilerParams(dimension_semantics=sem, vmem_limit_bytes=vmem_mib * MIB)


def _dot(a, b):
    return jnp.dot(a, b, preferred_element_type=F32)


def _dot_nt(a, b):
    return lax.dot_general(a, b, (((1,), (1,)), ((), ())), preferred_element_type=F32)


def _dot_tn(a, b):
    return lax.dot_general(a, b, (((0,), (0,)), ((), ())), preferred_element_type=F32)


def _rms_scale(x):
    return lax.rsqrt(jnp.mean(x * x, axis=-1, keepdims=True) + RMS_EPS)


def _rms_bwd(dy, x, g):
    xn = x * _rms_scale(x)
    dn = dy * g
    dx = _rms_scale(x) * (dn - xn * jnp.mean(dn * xn, axis=-1, keepdims=True))
    return dx, jnp.sum(dy * xn, axis=0, keepdims=True)


def _sigmoid(x):
    return jax.nn.sigmoid(x)


def _acc_out(ref, first, value):
    @pl.when(first)
    def _():
        ref[...] = value

    @pl.when(jnp.logical_not(first))
    def _():
        ref[...] += value


def _my_slot():
    return 4 * lax.axis_index("x") + 2 * lax.axis_index("y") + lax.axis_index("c")


def _peer(mask):
    x, y, c = lax.axis_index("x"), lax.axis_index("y"), lax.axis_index("c")
    return (x ^ ((mask >> 2) & 1), y ^ ((mask >> 1) & 1), c ^ (mask & 1))


def _cols(width):
    return lambda ref, slot: ref.at[:, pl.ds(pl.multiple_of(slot * width, 128), width)]


def _rows(height):
    return lambda ref, slot: ref.at[pl.ds(pl.multiple_of(slot * height, 8), height), :]


def _lead(ref, slot):
    return ref.at[slot]


def _pool_rows(ref, slot):
    return ref.at[:, pl.ds(pl.multiple_of(slot * (PG // N_DEV), 8), PG // N_DEV), :]


SHARDED = (
    ("w_in", (D, IN_COLS), (D, IN_COLS // N_DEV), _cols(IN_COLS // N_DEV)),
    ("w_ff1", (D, D_FF), (D, D_FF // N_DEV), _cols(D_FF // N_DEV)),
    ("w_ff2", (D_FF, D), (D_FF // N_DEV, D), _rows(D_FF // N_DEV)),
    ("w_conv_out", (D, D), (D // N_DEV, D), _rows(D // N_DEV)),
    ("w_pool_out", (D, D), (D // N_DEV, D), _rows(D // N_DEV)),
    ("w_o", (D, D), (D // N_DEV, D), _rows(D // N_DEV)),
    ("pool_w", (4, PG, PG), (4, PG // N_DEV, PG), _pool_rows),
    ("dw_kernel", (N_DEV, TAPS, D // N_DEV), (TAPS, D // N_DEV), _lead),
)
N_SHARDED = len(SHARDED)


SHARD_AT = {name: at for name, _, _, at in SHARDED}
HBM = pl.BlockSpec(memory_space=pltpu.HBM)
SEM = pl.BlockSpec(memory_space=pltpu.SEMAPHORE)
ANY = pl.BlockSpec(memory_space=pl.ANY)
EFFECT = pltpu.SideEffectType.DATAFLOW_SIDE_EFFECTING


def _in_hbm(a):
    return pltpu.with_memory_space_constraint(a, pltpu.HBM)


def _pinned(*arrays):
    return [_in_hbm(a) for a in arrays]


def _stage_shards(names, shards, token):
    n = len(names)
    specs = [s for name in names for s in SHARDED if s[0] == name]
    dtypes = [F32 if name == "dw_kernel" else BF16 for name in names]

    def body(*refs):
        ins = refs[:n]
        fulls = refs[n + 1:2 * n + 1]
        raw = refs[2 * n + 1:3 * n + 1]
        stage = refs[3 * n + 1:4 * n + 1]
        in_sems, out_sems = refs[4 * n + 1:]
        me = _my_slot()
        loads = [pltpu.make_async_copy(ins[a], raw[a], in_sems.at[a]) for a in range(n)]
        for cp in loads:
            cp.start()
        stores = []
        for a, (_, _, _, at) in enumerate(specs):
            loads[a].wait()
            stage[a][...] = raw[a][...].astype(dtypes[a])
            cp = pltpu.make_async_copy(stage[a], at(fulls[a], me), out_sems.at[a])
            cp.start()
            stores.append(cp)
        for cp in stores:
            cp.wait()

    return pl.pallas_call(
        body, name="stage_" + names[0],
        out_shape=[pltpu.HBM(full, dt) for (_, full, _, _), dt in zip(specs, dtypes)],
        in_specs=[ANY] * (n + 1),
        out_specs=[ANY] * n,
        scratch_shapes=[pltpu.VMEM(shard, F32) for _, _, shard, _ in specs]
        + [pltpu.VMEM(shard, dt) for (_, _, shard, _), dt in zip(specs, dtypes)]
        + [pltpu.SemaphoreType.DMA((n,)), pltpu.SemaphoreType.DMA((n,))],
        compiler_params=pltpu.CompilerParams(vmem_limit_bytes=40 * MIB),
    )(*_pinned(*shards, token))


Leg = collections.namedtuple("Leg", "routes src_of dst_of src_is_land")


def _sem_index(k, m):
    return k * (N_DEV - 1) + m - 1


def _exchange_start(name, groups, legs):
    sizes = [len(g[0]) for g in groups]
    names = [nm for g in groups for nm in g[0]]
    srcs = [s for g in groups if g[1] is not None for s in g[1]]
    lands = [l for g in groups for l in g[2]]
    n_src, n, n_g, n_l = len(srcs), len(names), len(groups), len(legs)

    def body(*refs):
        src_refs, land_refs = list(refs[:n_src]), refs[n_src:n_src + n]
        sems = refs[n_src + n:n_src + n + 2 * n_g * n_l]
        token = refs[-1]
        me = _my_slot()
        first = 0
        for g, size in enumerate(sizes):
            own_src = [src_refs.pop(0) for _ in range(size)] if groups[g][1] is not None else None
            for li, leg in enumerate(legs):
                send, recv = sems[2 * (g * n_l + li)], sems[2 * (g * n_l + li) + 1]
                for m, via in leg.routes:
                    for k in range(size):
                        land = land_refs[first + k]
                        src = land if leg.src_is_land else own_src[k]
                        pltpu.make_async_remote_copy(
                            src_ref=leg.src_of(names[first + k], src, me, m), dst_ref=leg.dst_of(names[first + k], land, me, m),
                            send_sem=send.at[_sem_index(k, m)], recv_sem=recv.at[_sem_index(k, m)],
                            device_id=_peer(via), device_id_type=MESH).start()
            first += size
        token[...] = jnp.zeros_like(token)

    sem_shapes = [pltpu.SemaphoreType.DMA((size * (N_DEV - 1),)) for size in sizes for _ in range(2 * n_l)]
    n_sem = len(sem_shapes)
    outs = pl.pallas_call(
        body, name=name,
        out_shape=sem_shapes + [pltpu.HBM(a.shape, a.dtype) for a in srcs + lands] + [jax.ShapeDtypeStruct((8, 128), F32)],
        in_specs=[HBM] * (n_src + n),
        out_specs=[SEM] * n_sem + [HBM] * (n_src + n) + [pl.BlockSpec(memory_space=pltpu.VMEM)],
        input_output_aliases={k: n_sem + k for k in range(n_src + n)},
        compiler_params=pltpu.CompilerParams(has_side_effects=EFFECT),
    )(*[_in_hbm(a) for a in srcs + lands])
    sems, thru, token = outs[:n_sem], list(outs[n_sem:-1]), outs[-1]
    src_thru, land_thru = thru[:n_src], thru[n_src:]
    handles, first = [], 0
    for g, size in enumerate(sizes):
        pairs = [(sems[2 * (g * n_l + li)], sems[2 * (g * n_l + li) + 1]) for li in range(n_l)]
        mine = [src_thru.pop(0) for _ in range(size)] if groups[g][1] is not None else None
        handles.append((groups[g][0], pairs, mine, land_thru[first:first + size]))
        first += size
    return handles, token


def _exchange_wait(name, names, srcs, lands, waits, after):
    n = len(names)
    n_src = n if srcs is not None else 0

    def body(*refs):
        src_refs, land_refs = refs[:n_src], refs[n_src:n_src + n]
        sems = refs[n_src + n:n_src + n + 2 * len(waits)]
        me = _my_slot()
        for wi, (_, leg) in enumerate(waits):
            for m, via in leg.routes:
                for k in range(n):
                    src = land_refs[k] if leg.src_is_land else src_refs[k]
                    cp = pltpu.make_async_remote_copy(
                        src_ref=leg.src_of(names[k], src, me, m), dst_ref=leg.dst_of(names[k], land_refs[k], me ^ via, m),
                        send_sem=sems[2 * wi].at[_sem_index(k, m)], recv_sem=sems[2 * wi + 1].at[_sem_index(k, m)],
                        device_id=_peer(via), device_id_type=MESH)
                    cp.wait_send()
                    cp.wait_recv()

    arrays = (list(srcs) if srcs is not None else []) + list(lands)
    outs = pl.pallas_call(
        body, name=name,
        out_shape=[pltpu.HBM(a.shape, a.dtype) for a in arrays],
        in_specs=[HBM] * len(arrays) + [SEM] * (2 * len(waits)) + [pl.BlockSpec(memory_space=pl.ANY)],
        out_specs=[HBM] * len(arrays),
        input_output_aliases={k: k for k in range(len(arrays))},
        compiler_params=pltpu.CompilerParams(has_side_effects=EFFECT),
    )(*arrays, *[s for pair, _ in waits for s in pair], after)
    return (outs[:n_src] if srcs is not None else None), outs[n_src:]


def _shard_slot(name, ref, slot):
    return SHARD_AT[name](ref, slot)


GATHER_ICI = Leg(((2, 2), (4, 4), (6, 6)), lambda name, ref, me, m: _shard_slot(name, ref, me),
                 lambda name, ref, sender, m: _shard_slot(name, ref, sender), True)
GATHER_D2D = Leg(((1, 1),), GATHER_ICI.src_of, GATHER_ICI.dst_of, True)
GATHER_FORWARD = Leg(((2, 1), (4, 1), (6, 1)), lambda name, ref, me, m: _shard_slot(name, ref, me ^ m),
                     lambda name, ref, sender, m: _shard_slot(name, ref, sender ^ m), True)
SCATTER = Leg(tuple((m, m) for m in range(1, N_DEV)), lambda name, ref, me, m: _shard_slot(name, ref, me ^ m),
              lambda name, ref, sender, m: ref.at[m - 1], False)


def _vec_exchange(vectors, after):
    n = len(vectors)

    def body(*refs):
        vec, vec_out = refs[n + 2], refs[n + 1]
        send_sems, recv_sems, local_sem = refs[n + 3:]
        for k in range(n):
            vec[k:k + 1, :] = refs[k][...]
        me = _my_slot()
        local = pltpu.make_async_copy(vec, vec_out.at[me], local_sem)
        local.start()
        sends = []
        for mask in range(1, N_DEV):
            cp = pltpu.make_async_remote_copy(
                src_ref=vec, dst_ref=vec_out.at[me], send_sem=send_sems.at[mask - 1],
                recv_sem=recv_sems.at[mask - 1], device_id=_peer(mask), device_id_type=MESH)
            cp.start()
            sends.append(cp)
        for mask in range(1, N_DEV):
            pltpu.make_async_remote_copy(
                src_ref=vec, dst_ref=vec_out.at[me ^ mask], send_sem=send_sems.at[mask - 1],
                recv_sem=recv_sems.at[mask - 1], device_id=_peer(mask), device_id_type=MESH).wait_recv()
        for cp in sends:
            cp.wait_send()
        local.wait()

    return pl.pallas_call(
        body, name="vec_exchange",
        out_shape=jax.ShapeDtypeStruct((N_DEV, n, D), F32),
        in_specs=[pl.BlockSpec(memory_space=pltpu.VMEM)] * n + [ANY],
        out_specs=pl.BlockSpec(memory_space=pl.ANY),
        scratch_shapes=[pltpu.VMEM((n, D), F32), pltpu.SemaphoreType.DMA((N_DEV - 1,)),
                        pltpu.SemaphoreType.DMA((N_DEV - 1,)), pltpu.SemaphoreType.DMA],
    )(*vectors, after)


def _adamw_update(g, w_ref, m_ref, v_ref, g_out, d_out, m_out, v_out):
    m_new = ADAM_B1 * m_ref[...] + (1.0 - ADAM_B1) * g
    v_new = ADAM_B2 * v_ref[...] + (1.0 - ADAM_B2) * (g * g)
    m_hat = m_new / (1.0 - ADAM_B1 ** ADAM_STEP)
    v_hat = v_new / (1.0 - ADAM_B2 ** ADAM_STEP)
    g_out[...] = g
    d_out[...] = -ADAM_LR * (m_hat / (jnp.sqrt(v_hat) + ADAM_EPS) + ADAM_WD * w_ref[...])
    m_out[...] = m_new
    v_out[...] = v_new


ADAMW_ROWS = 256


def _adamw(name, own, parts, w, m, v, slot, token):
    shard = w.shape
    if name in ("w_in", "w_ff1"):
        tr = ADAMW_ROWS
        grid = (shard[0] // tr,)
        own_spec = pl.BlockSpec((tr, shard[1]), lambda i, s: (i, s[0]))
        blk = pl.BlockSpec((tr, shard[1]), lambda i, s: (i, 0))
        parts_spec = pl.BlockSpec((N_DEV - 1, tr, shard[1]), lambda i, s: (0, i, 0))
    elif name == "pool_w":
        grid = (shard[0],)
        own_spec = pl.BlockSpec((None,) + shard[1:], lambda g, s: (g, s[0], 0))
        blk = pl.BlockSpec((None,) + shard[1:], lambda g, s: (g, 0, 0))
        parts_spec = pl.BlockSpec((N_DEV - 1, None) + shard[1:], lambda g, s: (0, g, 0, 0))
    elif name == "dw_kernel":
        grid = (1,)
        own_spec = pl.BlockSpec((None,) + shard, lambda i, s: (s[0], 0, 0))
        blk = pl.BlockSpec(shard, lambda i, s: (0, 0))
        parts_spec = pl.BlockSpec((N_DEV - 1,) + shard, lambda i, s: (0, 0, 0))
    else:
        tr = min(ADAMW_ROWS, shard[0])
        grid = (shard[0] // tr,)
        own_spec = pl.BlockSpec((tr, shard[1]), lambda i, s: (s[0] * grid[0] + i, 0))
        blk = pl.BlockSpec((tr, shard[1]), lambda i, s: (i, 0))
        parts_spec = pl.BlockSpec((N_DEV - 1, tr, shard[1]), lambda i, s: (0, i, 0))

    def body(slot_ref, own_ref, p_ref, w_ref, m_ref, v_ref, _token, g_out, d_out, m_out, v_out):
        g = own_ref[...].astype(F32)
        for k in range(N_DEV - 1):
            g = g + p_ref[k].astype(F32)
        _adamw_update(g, w_ref, m_ref, v_ref, g_out, d_out, m_out, v_out)

    return pl.pallas_call(
        body, name="adamw_" + name,
        grid_spec=pltpu.PrefetchScalarGridSpec(
            num_scalar_prefetch=1, grid=grid, in_specs=[own_spec, parts_spec, blk, blk, blk, ANY], out_specs=[blk] * 4),
        out_shape=[jax.ShapeDtypeStruct(shard, F32)] * 4,
        compiler_params=_params(("arbitrary",), 32),
    )(slot, *_pinned(own, parts, w, m, v, token))


def _adamw_vectors(parts, w, m, v):
    n_vec = w.shape[0]

    def body(p_ref, w_ref, m_ref, v_ref, g_out, d_out, m_out, v_out, loss_out):
        total = p_ref[0]
        for s in range(1, N_DEV):
            total = total + p_ref[s]
        _adamw_update(total[0:n_vec], w_ref, m_ref, v_ref, g_out, d_out, m_out, v_out)
        loss_out[...] = total[n_vec:n_vec + 1] * (0.5 / D)

    return pl.pallas_call(
        body, name="adamw_vectors",
        out_shape=[jax.ShapeDtypeStruct(w.shape, F32)] * 4 + [jax.ShapeDtypeStruct((1, D), F32)],
    )(parts, w, m, v)


def _input_norm(x, g_pre, token):
    seq = x.shape[0]
    tm = 1024

    def body(x_ref, g_ref, _token, u_ref, ut_ref):
        xf = x_ref[...]
        u = (xf * _rms_scale(xf) * g_ref[...]).astype(BF16)
        u_ref[...] = u
        ut_ref[...] = u.T

    return pl.pallas_call(
        body, name="input_norm", grid=(seq // tm,),
        in_specs=[pl.BlockSpec((tm, D), lambda i: (i, 0)), pl.BlockSpec((1, D), lambda i: (0, 0)), ANY],
        out_specs=[pl.BlockSpec((tm, D), lambda i: (i, 0)), pl.BlockSpec((D, tm), lambda i: (0, i))],
        out_shape=[pltpu.HBM((seq, D), BF16), pltpu.HBM((D, seq), BF16)],
        compiler_params=_params(("arbitrary",), 40),
    )(*_pinned(x, g_pre, token))


IN_PROJ_TN = IN_COLS // 4


def _in_proj_fwd(name, u, w_in, blocks, proj, token):
    seq = u.shape[0]
    tm, tn = 1024, IN_PROJ_TN

    def body(_blocks, u_ref, w_ref, _token, _proj_in, proj_ref):
        proj_ref[...] = _dot(u_ref[...], w_ref[...]).astype(BF16)

    return pl.pallas_call(
        body, name=name,
        grid_spec=pltpu.PrefetchScalarGridSpec(
            num_scalar_prefetch=1, grid=(blocks.shape[0], seq // tm),
            in_specs=[pl.BlockSpec((tm, D), lambda j, i, b: (i, 0)), pl.BlockSpec((D, tn), lambda j, i, b: (0, b[j])),
                      ANY, ANY],
            out_specs=pl.BlockSpec((tm, tn), lambda j, i, b: (i, b[j]))),
        out_shape=pltpu.HBM((seq, IN_COLS), BF16),
        input_output_aliases={4: 0},
        compiler_params=_params(("arbitrary", "arbitrary"), 40),
    )(blocks, *_pinned(u, w_in, token, proj))


CONV_TM = 512
CONV_RS = 128


def _shifts():
    return [(b, [(a, 8 * a + b) for a in range(4) if 8 * a + b < TAPS]) for b in range(8)]


def _taps_looking_back(buf, row0, lanes, weight):
    acc = None
    for b, group in _shifts():
        part = None
        for a, s in group:
            term = weight(TAPS - 1 - s) * buf[pl.ds(row0 - 8 - 8 * a, CONV_RS + 8), lanes]
            part = term if part is None else part + term
        if b:
            part = pltpu.roll(part, b, 0)
        acc = part[8:, :] if acc is None else acc + part[8:, :]
    return acc


def _taps_looking_ahead(buf, row0, lanes, weight):
    acc = None
    for b, group in _shifts():
        part = None
        for a, s in group:
            term = weight(TAPS - 1 - s) * buf[pl.ds(row0 + 8 * a, CONV_RS + 8), lanes]
            part = term if part is None else part + term
        if b:
            part = pltpu.roll(part, CONV_RS + 8 - b, 0)
        acc = part[:CONV_RS, :] if acc is None else acc + part[:CONV_RS, :]
    return acc


def _layer_norm_parts(cv):
    mu = jnp.mean(cv, axis=-1, keepdims=True)
    cen = cv - mu
    rstd = lax.rsqrt(jnp.mean(cen * cen, axis=-1, keepdims=True) + LN_EPS)
    return cen * rstd, rstd


def _conv_fwd(proj, dw, dw_bias, ln_g, ln_b, w_conv_out, token):
    seq = proj.shape[0]
    tm = CONV_TM

    def body(a_ref, gate_ref, dw_ref, bias_ref, lg_ref, lb_ref, w_ref, _token, cv_ref, y_ref, ug):
        i = pl.program_id(0)

        @pl.when(i == 0)
        def _():
            ug[0:HALO, :] = jnp.zeros((HALO, D), F32)

        @pl.when(i > 0)
        def _():
            ug[0:HALO, :] = ug[tm:tm + HALO, :]
        ug[HALO:HALO + tm, :] = a_ref[...].astype(F32) * _sigmoid(gate_ref[...].astype(F32))

        def channel_block(cb, carry):
            lanes = pl.ds(pl.multiple_of(cb * 128, 128), 128)
            for r0 in range(0, tm, CONV_RS):
                taps = _taps_looking_back(ug, HALO + r0, lanes, lambda k: dw_ref[cb, k:k + 1, :])
                cv_ref[pl.ds(r0, CONV_RS), lanes] = taps + bias_ref[:, lanes]
            return carry
        lax.fori_loop(0, D // 128, channel_block, 0)

        n, _ = _layer_norm_parts(cv_ref[...])
        ln = n * lg_ref[...] + lb_ref[...]
        y_ref[...] = _dot((ln * _sigmoid(ln)).astype(BF16), w_ref[...]).astype(BF16)

    vec = pl.BlockSpec((1, D), lambda i: (0, 0))
    tile = pl.BlockSpec((tm, D), lambda i: (i, 0))
    return pl.pallas_call(
        body, name="conv_fwd", grid=(seq // tm,),
        in_specs=[pl.BlockSpec((tm, D), lambda i: (i, 0)), pl.BlockSpec((tm, D), lambda i: (i, 1)),
                  pl.BlockSpec((N_DEV, TAPS, 128), lambda i: (0, 0, 0)), vec, vec, vec,
                  pl.BlockSpec((D, D), lambda i: (0, 0)), ANY],
        out_specs=[tile, tile],
        out_shape=[pltpu.HBM((seq, D), F32), pltpu.HBM((seq, D), BF16)],
        scratch_shapes=[pltpu.VMEM((HALO + tm, D), F32)],
        compiler_params=_params(("arbitrary",), 32),
    )(*_pinned(proj, proj, dw, dw_bias, ln_g, ln_b, w_conv_out, token))


def _window_sums(rows, window, back):
    n = rows.shape[0]
    span = 1
    while span < window:
        rows = rows + pltpu.roll(rows, span if back else n - span, 0)
        span *= 2
    return rows


def _pool_counts(tile_index, tm, window):
    t = tile_index * tm + lax.broadcasted_iota(jnp.int32, (tm, 1), 0)
    return 1.0 / jnp.minimum(t + 1, window).astype(F32)


def _pool_merge_fwd(proj, y_conv, x, pool_w, pool_scale, w_pool_out, w_o, g_post, token):
    seq = proj.shape[0]
    tm = CONV_TM

    def body(p_ref, gc_ref, gp_ref, yc_ref, x_ref, pw_ref, ps_ref, wpo_ref, wo_ref, g_ref, _token,
             z_ref, zl_ref, yp_ref, mg_ref, o_ref, h1_ref, pbuf, zl_buf):
        i = pl.program_id(0)

        @pl.when(i == 0)
        def _():
            pbuf[0:HALO, :] = jnp.zeros((HALO, D), F32)

        @pl.when(i > 0)
        def _():
            pbuf[0:HALO, :] = pbuf[tm:tm + HALO, :]
        pbuf[HALO:HALO + tm, :] = p_ref[...].astype(F32)

        for g, window in enumerate(POOL_WINDOWS):
            lanes = pl.ds(g * PG, PG)
            acc = _window_sums(pbuf[:, lanes], window, back=True)[HALO:, :]
            zg = acc * _pool_counts(i, tm, window) - pbuf[pl.ds(HALO, tm), lanes]
            z_ref[:, lanes] = zg.astype(BF16)
            zl_buf[:, lanes] = _dot(zg.astype(BF16), pw_ref[g])
        zl = zl_buf[...]
        zl_ref[...] = zl.astype(BF16)
        y_pool = _dot((zl * ps_ref[...]).astype(BF16), wpo_ref[...])
        yp_ref[...] = y_pool.astype(BF16)
        merged = (_sigmoid(gc_ref[...].astype(F32)) * yc_ref[...].astype(F32)
                  + _sigmoid(gp_ref[...].astype(F32)) * y_pool).astype(BF16)
        mg_ref[...] = merged
        o = _dot(merged, wo_ref[...])
        o_ref[...] = o
        h1_ref[...] = x_ref[...] + o * _rms_scale(o) * g_ref[...]

    vec = pl.BlockSpec((1, D), lambda i: (0, 0))
    tile = pl.BlockSpec((tm, D), lambda i: (i, 0))
    mat = pl.BlockSpec((D, D), lambda i: (0, 0))
    return pl.pallas_call(
        body, name="pool_merge_fwd", grid=(seq // tm,),
        in_specs=[pl.BlockSpec((tm, D), lambda i: (i, 2)), pl.BlockSpec((tm, D), lambda i: (i, 3)),
                  pl.BlockSpec((tm, D), lambda i: (i, 4)), tile, tile,
                  pl.BlockSpec((4, PG, PG), lambda i: (0, 0, 0)), vec, mat, mat, vec, ANY],
        out_specs=[tile] * 6,
        out_shape=[pltpu.HBM((seq, D), dt) for dt in (BF16, BF16, BF16, BF16, F32, F32)],
        scratch_shapes=[pltpu.VMEM((HALO + tm, D), F32), pltpu.VMEM((tm, D), F32)],
        compiler_params=_params(("arbitrary",), 48),
    )(*_pinned(proj, proj, proj, y_conv, x, pool_w, pool_scale, w_pool_out, w_o, g_post, token))


def _mlp_fwd(h1, g_pre, w_ff1, w_ff2, g_post, target):
    seq = h1.shape[0]
    tm, tf = 1024, D_FF // N_DEV
    n_f = D_FF // tf

    def body(h1_ref, gpre_ref, w1_ref, w2_ref, gpost_ref, tgt_ref, v_ref, dm_ref, dh2_ref, sse_ref, ggrad_ref, macc):
        i, j = pl.program_id(0), pl.program_id(1)

        @pl.when(j == 0)
        def _():
            h = h1_ref[...]
            v_ref[...] = (h * _rms_scale(h) * gpre_ref[...]).astype(BF16)
        f = jnp.maximum(_dot(v_ref[...], w1_ref[...]), 0.0)
        part = _dot((f * f).astype(BF16), w2_ref[...])

        @pl.when(j == 0)
        def _():
            macc[...] = part

        @pl.when(j > 0)
        def _():
            macc[...] += part

        @pl.when(j == n_f - 1)
        def _():
            mo = macc[...]
            err = h1_ref[...] + mo * _rms_scale(mo) * gpost_ref[...] - tgt_ref[...]
            dh2 = err * (1.0 / D)
            dh2_ref[...] = dh2.astype(BF16)
            dm, ggrad = _rms_bwd(dh2, mo, gpost_ref[...])
            dm_ref[...] = dm.astype(BF16)
            _acc_out(ggrad_ref, i == 0, ggrad)
            _acc_out(sse_ref, i == 0, jnp.sum(jnp.sum(err * err, axis=1, keepdims=True), axis=0, keepdims=True))

    vec = pl.BlockSpec((1, D), lambda i, j: (0, 0))
    tile = pl.BlockSpec((tm, D), lambda i, j: (i, 0))
    return pl.pallas_call(
        body, name="mlp_fwd", grid=(seq // tm, n_f),
        in_specs=[tile, vec, pl.BlockSpec((D, tf), lambda i, j: (0, j)), pl.BlockSpec((tf, D), lambda i, j: (j, 0)), vec, tile],
        out_specs=[tile, tile, tile, pl.BlockSpec((1, 1), lambda i, j: (0, 0)), vec],
        out_shape=[pltpu.HBM((seq, D), BF16), pltpu.HBM((seq, D), BF16),
                   pltpu.HBM((seq, D), BF16), jax.ShapeDtypeStruct((1, 1), F32),
                   jax.ShapeDtypeStruct((1, D), F32)],
        scratch_shapes=[pltpu.VMEM((tm, D), F32)],
        compiler_params=_params(("arbitrary", "arbitrary"), 56),
    )(*_pinned(h1, g_pre, w_ff1, w_ff2, g_post, target))


def _mlp_bwd(v, dm, w_ff1, w_ff2):
    seq = v.shape[0]
    tm, tf = 1024, D_FF // N_DEV
    n_t = seq // tm

    def body(v_ref, dm_ref, w1_ref, w2_ref, dv_hbm, g1_ref, g2_ref, dv_acc, g1_acc, g2_acc, sems):
        j, i = pl.program_id(0), pl.program_id(1)
        vt, dmt = v_ref[...], dm_ref[...]
        f = jnp.maximum(_dot(vt, w1_ref[...]), 0.0)
        df = (_dot_nt(dmt, w2_ref[...]) * (2.0 * f)).astype(BF16)
        rows = pl.ds(pl.multiple_of(i * tm, tm), tm)
        dv_part = _dot_nt(df, w1_ref[...])

        @pl.when(j == 0)
        def _():
            dv_acc[rows, :] = dv_part

        @pl.when(j > 0)
        def _():
            dv_acc[rows, :] += dv_part
        g1_part = _dot_tn(vt, df)
        g2_part = _dot_tn((f * f).astype(BF16), dmt)

        @pl.when(i == 0)
        def _():
            g1_acc[...] = g1_part
            g2_acc[...] = g2_part

        @pl.when(i > 0)
        def _():
            g1_acc[...] += g1_part
            g2_acc[...] += g2_part

        @pl.when(i == n_t - 1)
        def _():
            g1_ref[...] = g1_acc[...].astype(BF16)
            g2_ref[...] = g2_acc[...].astype(BF16)

        last_pass = j == D_FF // tf - 1

        @pl.when(last_pass)
        def _():
            pltpu.make_async_copy(dv_acc.at[rows, :], dv_hbm.at[rows, :], sems.at[i]).start()

        @pl.when(jnp.logical_and(last_pass, i == n_t - 1))
        def _():
            for t in range(n_t):
                done = pl.ds(t * tm, tm)
                pltpu.make_async_copy(dv_acc.at[done, :], dv_hbm.at[done, :], sems.at[t]).wait()

    tile = pl.BlockSpec((tm, D), lambda j, i: (i, 0))
    return pl.pallas_call(
        body, name="mlp_bwd", grid=(D_FF // tf, n_t),
        in_specs=[tile, tile, pl.BlockSpec((D, tf), lambda j, i: (0, j)), pl.BlockSpec((tf, D), lambda j, i: (j, 0))],
        out_specs=[pl.BlockSpec(memory_space=pl.ANY), pl.BlockSpec((D, tf), lambda j, i: (0, j)),
                   pl.BlockSpec((tf, D), lambda j, i: (j, 0))],
        out_shape=[pltpu.HBM((seq, D), F32), pltpu.HBM((D, D_FF), BF16),
                   pltpu.HBM((D_FF, D), BF16)],
        scratch_shapes=[pltpu.VMEM((seq, D), F32), pltpu.VMEM((D, tf), F32), pltpu.VMEM((tf, D), F32),
                        pltpu.SemaphoreType.DMA((n_t,))],
        compiler_params=_params(("arbitrary", "arbitrary"), 52),
    )(*_pinned(v, dm, w_ff1, w_ff2))


def _merge_bwd(dh2, dv, h1, g_mlp_pre, o, g_mix_post, w_o, merged, proj, y_conv, y_pool, token):
    seq = dh2.shape[0]
    tm = 256
    n_t = seq // tm

    def body(dh2_ref, dv_ref, h1_ref, gpre_ref, o_ref, gpost_ref, wo_ref, mg_ref, gc_ref, gp_ref, yc_ref, yp_ref, _token,
             dh1_ref, dyc_ref, dyp_ref, dg_ref, gwo_ref, ggpre_ref, ggpost_ref, gwo_acc):
        i = pl.program_id(0)
        dnorm, ggpre = _rms_bwd(dv_ref[...], h1_ref[...], gpre_ref[...])
        dh1 = dh2_ref[...].astype(F32) + dnorm
        dh1_ref[...] = dh1
        do, ggpost = _rms_bwd(dh1, o_ref[...], gpost_ref[...])
        do = do.astype(BF16)
        _acc_out(ggpre_ref, i == 0, ggpre)
        _acc_out(ggpost_ref, i == 0, ggpost)
        _acc_out(gwo_acc, i == 0, _dot_tn(mg_ref[...], do))
        dmerged = _dot_nt(do, wo_ref[...])
        sc, sp = _sigmoid(gc_ref[...].astype(F32)), _sigmoid(gp_ref[...].astype(F32))
        dyc_ref[...] = (dmerged * sc).astype(BF16)
        dyp_ref[...] = (dmerged * sp).astype(BF16)
        dg_ref[:, 0:D] = (dmerged * yc_ref[...].astype(F32) * (sc * (1.0 - sc))).astype(BF16)
        dg_ref[:, D:2 * D] = (dmerged * yp_ref[...].astype(F32) * (sp * (1.0 - sp))).astype(BF16)

        @pl.when(i == n_t - 1)
        def _():
            gwo_ref[...] = gwo_acc[...].astype(BF16)

    vec = pl.BlockSpec((1, D), lambda i: (0, 0))
    tile = pl.BlockSpec((tm, D), lambda i: (i, 0))
    mat = pl.BlockSpec((D, D), lambda i: (0, 0))
    return pl.pallas_call(
        body, name="merge_bwd", grid=(n_t,),
        in_specs=[tile, tile, tile, vec, tile, vec, mat, tile,
                  pl.BlockSpec((tm, D), lambda i: (i, 3)), pl.BlockSpec((tm, D), lambda i: (i, 4)), tile, tile, ANY],
        out_specs=[tile, tile, tile, pl.BlockSpec((tm, 2 * D), lambda i: (i, 0)), mat, vec, vec],
        out_shape=[pltpu.HBM((seq, D), F32), pltpu.HBM((seq, D), BF16),
                   pltpu.HBM((seq, D), BF16), pltpu.HBM((seq, 2 * D), BF16),
                   pltpu.HBM((D, D), BF16), jax.ShapeDtypeStruct((1, D), F32),
                   jax.ShapeDtypeStruct((1, D), F32)],
        scratch_shapes=[pltpu.VMEM((D, D), F32)],
        compiler_params=_params(("arbitrary",), 48),
    )(*_pinned(dh2, dv, h1, g_mlp_pre, o, g_mix_post, w_o, merged, proj, proj, y_conv, y_pool, token))


def _pool_bwd(dy_pool, zl, z, pool_w, pool_scale, w_pool_out, token):
    seq = dy_pool.shape[0]
    tm = CONV_TM
    n_t = seq // tm

    def body(dy_ref, zl_ref, z_ref, pw_ref, ps_ref, wpo_ref, _token,
             dp_ref, gwpo_ref, gpw_ref, gps_ref, qbuf, gwpo_acc, gpw_acc):
        i = pl.program_id(0)
        tile_index = n_t - 1 - i
        first = i == 0
        dy = dy_ref[...]
        zl = zl_ref[...].astype(F32)
        dzs = _dot_nt(dy, wpo_ref[...])
        _acc_out(gwpo_acc, first, _dot_tn((zl * ps_ref[...]).astype(BF16), dy))
        _acc_out(gps_ref, first, jnp.sum(dzs * zl, axis=0, keepdims=True))
        dzl = (dzs * ps_ref[...]).astype(BF16)

        @pl.when(first)
        def _():
            qbuf[tm:tm + HALO, :] = jnp.zeros((HALO, D), F32)

        @pl.when(jnp.logical_not(first))
        def _():
            qbuf[tm:tm + HALO, :] = qbuf[0:HALO, :]

        dzs_list = []
        for g, window in enumerate(POOL_WINDOWS):
            lanes = pl.ds(g * PG, PG)
            dzl_g = dzl[:, g * PG:(g + 1) * PG]
            dz = _dot_nt(dzl_g, pw_ref[g])
            _acc_out(gpw_acc.at[g], first, _dot_tn(z_ref[:, lanes], dzl_g))
            qbuf[pl.ds(0, tm), lanes] = dz * _pool_counts(tile_index, tm, window)
            dzs_list.append(dz)
        for g, window in enumerate(POOL_WINDOWS):
            lanes = pl.ds(g * PG, PG)
            acc = _window_sums(qbuf[:, lanes], window, back=False)[:tm, :]
            dp_ref[:, lanes] = (acc - dzs_list[g]).astype(BF16)

        @pl.when(i == n_t - 1)
        def _():
            gwpo_ref[...] = gwpo_acc[...].astype(BF16)
            gpw_ref[...] = gpw_acc[...].astype(BF16)

    vec = pl.BlockSpec((1, D), lambda i: (0, 0))
    tile = pl.BlockSpec((tm, D), lambda i: (n_t - 1 - i, 0))
    mat = pl.BlockSpec((D, D), lambda i: (0, 0))
    pw = pl.BlockSpec((4, PG, PG), lambda i: (0, 0, 0))
    return pl.pallas_call(
        body, name="pool_bwd", grid=(n_t,),
        in_specs=[tile, tile, tile, pw, vec, mat, ANY],
        out_specs=[tile, mat, pw, vec],
        out_shape=[pltpu.HBM((seq, D), BF16), pltpu.HBM((D, D), BF16),
                   pltpu.HBM((4, PG, PG), BF16), jax.ShapeDtypeStruct((1, D), F32)],
        scratch_shapes=[pltpu.VMEM((tm + HALO, D), F32), pltpu.VMEM((D, D), F32), pltpu.VMEM((4, PG, PG), F32)],
        compiler_params=_params(("arbitrary",), 40),
    )(*_pinned(dy_pool, zl, z, pool_w, pool_scale, w_pool_out, token))


def _conv_bwd(dy_conv, cv, proj, dw, ln_g, ln_b, w_conv_out, token):
    seq = dy_conv.shape[0]
    tm = CONV_TM // 2
    n_t = seq // tm
    halo_blocks = tm // HALO

    def body(dy_ref, cv_ref, a_ref, gate_ref, ah_ref, gh_ref, dw_ref, lg_ref, lb_ref, w_ref, _token,
             dglu_ref, gw_ref, gdw_ref, gbias_ref, glg_ref, glb_ref, ug, dcv, dug, gw_acc):
        i = pl.program_id(0)
        tile_index = n_t - 1 - i
        first = i == 0
        dy = dy_ref[...]
        n, rstd = _layer_norm_parts(cv_ref[...])
        ln = n * lg_ref[...] + lb_ref[...]
        sg = _sigmoid(ln)
        _acc_out(gw_acc, first, _dot_tn((ln * sg).astype(BF16), dy))
        dln = _dot_nt(dy, w_ref[...]) * (sg * (1.0 + ln * (1.0 - sg)))
        _acc_out(glg_ref, first, jnp.sum(dln * n, axis=0, keepdims=True))
        _acc_out(glb_ref, first, jnp.sum(dln, axis=0, keepdims=True))
        dn = dln * lg_ref[...]
        dcv_tile = rstd * (dn - jnp.mean(dn, axis=-1, keepdims=True) - n * jnp.mean(dn * n, axis=-1, keepdims=True))
        _acc_out(gbias_ref, first, jnp.sum(dcv_tile, axis=0, keepdims=True))

        @pl.when(first)
        def _():
            dcv[tm:tm + HALO, :] = jnp.zeros((HALO, D), F32)

        @pl.when(jnp.logical_not(first))
        def _():
            dcv[tm:tm + HALO, :] = dcv[0:HALO, :]
        dcv[0:tm, :] = dcv_tile

        a, gate = a_ref[...].astype(F32), gate_ref[...].astype(F32)
        sgate = _sigmoid(gate)
        ug[HALO:HALO + tm, :] = a * sgate
        before = jnp.where(tile_index > 0, 1.0, 0.0)
        ug[0:HALO, :] = ah_ref[...].astype(F32) * _sigmoid(gh_ref[...].astype(F32)) * before

        @pl.when(first)
        def _():
            gdw_ref[...] = jnp.zeros((N_DEV, TAPS + 1, 128), F32)

        def channel_block(cb, carry):
            lanes = pl.ds(pl.multiple_of(cb * 128, 128), 128)
            for r0 in range(0, tm, CONV_RS):
                dug[pl.ds(r0, CONV_RS), lanes] = _taps_looking_ahead(dcv, r0, lanes, lambda k: dw_ref[cb, k:k + 1, :])
            for b, group in _shifts():
                sums = [jnp.zeros((8, 128), F32) for _ in group]
                for r0 in range(0, tm, CONV_RS):
                    window = ug[pl.ds(r0, CONV_RS + HALO), lanes]
                    if b:
                        window = pltpu.roll(window, b, 0)
                    d = dcv[pl.ds(r0, CONV_RS), lanes]
                    for n_a, (a, s) in enumerate(group):
                        prod = d * window[HALO - 8 * a:HALO - 8 * a + CONV_RS, :]
                        sums[n_a] = sums[n_a] + jnp.sum(prod.reshape(CONV_RS // 8, 8, 128), axis=0)
                for n_a, (a, s) in enumerate(group):
                    k = TAPS - 1 - s
                    gdw_ref[cb, k:k + 1, :] += jnp.sum(sums[n_a], axis=0, keepdims=True)
            return carry
        lax.fori_loop(0, D // 128, channel_block, 0)

        d_ug = dug[...]
        dglu_ref[:, 0:D] = (d_ug * sgate).astype(BF16)
        dglu_ref[:, D:2 * D] = (d_ug * a * (sgate * (1.0 - sgate))).astype(BF16)

        @pl.when(i == n_t - 1)
        def _():
            gw_ref[...] = gw_acc[...].astype(BF16)

    def halo_index(col):
        return lambda i: (jnp.maximum((n_t - 1 - i) * halo_blocks - 1, 0), col)

    vec = pl.BlockSpec((1, D), lambda i: (0, 0))
    tile = pl.BlockSpec((tm, D), lambda i: (n_t - 1 - i, 0))
    mat = pl.BlockSpec((D, D), lambda i: (0, 0))
    dwspec = pl.BlockSpec((N_DEV, TAPS, 128), lambda i: (0, 0, 0))
    return pl.pallas_call(
        body, name="conv_bwd", grid=(n_t,),
        in_specs=[tile, tile, pl.BlockSpec((tm, D), lambda i: (n_t - 1 - i, 0)), pl.BlockSpec((tm, D), lambda i: (n_t - 1 - i, 1)),
                  pl.BlockSpec((HALO, D), halo_index(0)), pl.BlockSpec((HALO, D), halo_index(1)), dwspec, vec, vec, mat,
                  ANY],
        out_specs=[pl.BlockSpec((tm, 2 * D), lambda i: (n_t - 1 - i, 0)), mat,
                   pl.BlockSpec((N_DEV, TAPS + 1, 128), lambda i: (0, 0, 0)), vec, vec, vec],
        out_shape=[pltpu.HBM((seq, 2 * D), BF16), pltpu.HBM((D, D), BF16),
                   pltpu.HBM((N_DEV, TAPS + 1, 128), F32), jax.ShapeDtypeStruct((1, D), F32),
                   jax.ShapeDtypeStruct((1, D), F32), jax.ShapeDtypeStruct((1, D), F32)],
        scratch_shapes=[pltpu.VMEM((HALO + tm, D), F32), pltpu.VMEM((tm + HALO, D), F32), pltpu.VMEM((tm, D), F32),
                        pltpu.VMEM((D, D), F32)],
        compiler_params=_params(("arbitrary",), 48),
    )(*_pinned(dy_conv, cv, proj, proj, proj, proj, dw, ln_g, ln_b, w_conv_out, token))


def _in_proj_bwd_x(d_glu, dp, dgates, w_in, x, g_pre, dh1, token):
    seq = x.shape[0]
    tm = 512

    def body(dglu_ref, dp_ref, dg_ref, w_ref, x_ref, g_ref, dh1_ref, _token, dx_ref, gg_ref):
        du = _dot_nt(dglu_ref[...], w_ref[:, 0:2 * D])
        du += _dot_nt(dp_ref[...], w_ref[:, 2 * D:3 * D])
        du += _dot_nt(dg_ref[...], w_ref[:, 3 * D:5 * D])
        dnorm, gg = _rms_bwd(du, x_ref[...], g_ref[...])
        dx_ref[...] = dh1_ref[...] + dnorm
        _acc_out(gg_ref, pl.program_id(0) == 0, gg)

    vec = pl.BlockSpec((1, D), lambda i: (0, 0))
    tile = pl.BlockSpec((tm, D), lambda i: (i, 0))
    wide = pl.BlockSpec((tm, 2 * D), lambda i: (i, 0))
    return pl.pallas_call(
        body, name="in_proj_bwd_x", grid=(seq // tm,),
        in_specs=[wide, tile, wide, pl.BlockSpec((D, IN_COLS), lambda i: (0, 0)), tile, vec, tile, ANY],
        out_specs=[tile, vec],
        out_shape=[pltpu.HBM((seq, D), F32), jax.ShapeDtypeStruct((1, D), F32)],
        compiler_params=_params(("arbitrary",), 48),
    )(*_pinned(d_glu, dp, dgates, w_in, x, g_pre, dh1, token))


def _in_proj_bwd_w(u_t, d_glu, dp, dgates, token):
    seq = u_t.shape[1]
    tm = 2048
    n_t = seq // tm

    def body(u_ref, dglu_ref, dp_ref, dg_ref, _token, out_ref, acc):
        b, i = pl.program_id(0), pl.program_id(1)
        ut = u_ref[...]

        def add(d_ref):
            _acc_out(acc, i == 0, _dot(ut, d_ref[...]))

        pl.when(b < 2)(lambda: add(dglu_ref))
        pl.when(b == 2)(lambda: add(dp_ref))
        pl.when(b > 2)(lambda: add(dg_ref))

        @pl.when(i == n_t - 1)
        def _():
            out_ref[...] = acc[...].astype(BF16)

    return pl.pallas_call(
        body, name="in_proj_bwd_w", grid=(IN_COLS // D, n_t),
        in_specs=[pl.BlockSpec((D, tm), lambda b, i: (0, i)),
                  pl.BlockSpec((tm, D), lambda b, i: (jnp.where(b < 2, i, 0), jnp.minimum(b, 1))),
                  pl.BlockSpec((tm, D), lambda b, i: (jnp.where(b == 2, i, 0), 0)),
                  pl.BlockSpec((tm, D), lambda b, i: (jnp.where(b > 2, i, 0), jnp.maximum(b - 3, 0))), ANY],
        out_specs=pl.BlockSpec((D, D), lambda b, i: (0, b)),
        out_shape=pltpu.HBM((D, IN_COLS), BF16),
        scratch_shapes=[pltpu.VMEM((D, D), F32)],
        compiler_params=_params(("arbitrary", "arbitrary"), 52),
    )(*_pinned(u_t, d_glu, dp, dgates, token))


VEC_NAMES = ("mix_pre_g", "dw_bias", "conv_ln_g", "conv_ln_b", "pool_scale", "mix_post_g", "mlp_pre_g", "mlp_post_g")
WEIGHT_ORDER = ("mix_pre_g", "w_in", "dw_kernel", "dw_bias", "conv_ln_g", "conv_ln_b", "w_conv_out", "pool_w",
                "pool_scale", "w_pool_out", "w_o", "mix_post_g", "mlp_pre_g", "w_ff1", "w_ff2", "mlp_post_g")


def _step(x, loss_target, w, m, v):
    row = lambda a: a.reshape(1, D)
    names = [s[0] for s in SHARDED]

    gather_groups = (("w_in",), ("w_conv_out", "dw_kernel"), ("pool_w", "w_pool_out", "w_o"), ("w_ff1", "w_ff2"))
    legs = [GATHER_ICI, GATHER_D2D]
    first_level, token = _exchange_start(
        "gather_start_w_in", [(gather_groups[0], None, _stage_shards(gather_groups[0], [w["w_in"]], x))], legs)
    later = [n for g in gather_groups[1:] for n in g]
    seeded = dict(zip(later, _stage_shards(later, [w[n] for n in later], token)))
    rest, token = _exchange_start("gather_start", [(g, None, [seeded[n] for n in g]) for g in gather_groups[1:]], legs)
    first_level = first_level + rest
    forwarded, full = {}, {}

    def forward(k, after):
        group, (ici_sems, d2d_sems), _, landed = first_level[k]
        _, landed = _exchange_wait("gather_ici_" + group[0], group, None, landed, [(ici_sems, GATHER_ICI)], after)
        (second,), tok = _exchange_start("gather_forward_" + group[0], [(group, None, landed)], [GATHER_FORWARD])
        forwarded[k] = (second[3], [(d2d_sems, GATHER_D2D), (second[1][0], GATHER_FORWARD)])
        return tok

    def gathered(k, after):
        group = gather_groups[k]
        landed, waits = forwarded[k]
        _, arrays = _exchange_wait("gather_wait_" + group[0], group, None, landed, waits, after)
        full.update(zip(group, arrays))

    u, u_t = _input_norm(x, row(w["mix_pre_g"]), token)
    group, (ici_sems, d2d_sems), _, landed = first_level[0]
    _, landed = _exchange_wait("gather_d2d_w_in", group, None, landed, [(d2d_sems, GATHER_D2D)], u)
    home = jnp.reshape(_my_slot() // 2, (1,)).astype(jnp.int32)
    proj = _in_proj_fwd("in_proj_fwd_home", u, landed[0], home, lax.empty((x.shape[0], IN_COLS), BF16), u)
    _, landed = _exchange_wait("gather_ici_w_in", group, None, landed, [(ici_sems, GATHER_ICI)], proj)
    (second,), tok = _exchange_start("gather_forward_w_in", [(group, None, landed)], [GATHER_FORWARD])
    (full["w_in"],) = _exchange_wait("gather_wait_w_in", group, None, second[3], [(second[1][0], GATHER_FORWARD)], tok)[1]
    tok = forward(1, full["w_in"])
    away = jnp.stack([home[0] ^ 1, home[0] ^ 2, home[0] ^ 3]).astype(jnp.int32)
    proj = _in_proj_fwd("in_proj_fwd", u, full["w_in"], away, proj, tok)
    gathered(1, proj)
    tok = forward(2, proj)
    cv, y_conv = _conv_fwd(proj, full["dw_kernel"], row(w["dw_bias"]), row(w["conv_ln_g"]), row(w["conv_ln_b"]),
                           full["w_conv_out"], tok)
    gathered(2, y_conv)
    tok = forward(3, y_conv)
    z, zl, y_pool, merged, o, h1 = _pool_merge_fwd(proj, y_conv, x, full["pool_w"], row(w["pool_scale"]),
                                                   full["w_pool_out"], full["w_o"], row(w["mix_post_g"]), tok)
    gathered(3, h1)
    vv, dm, dh2, sse, g_mlp_post = _mlp_fwd(h1, row(w["mlp_pre_g"]), full["w_ff1"], full["w_ff2"],
                                            row(w["mlp_post_g"]), loss_target)

    shard_of = {name: shard for name, _, shard, _ in SHARDED}

    def scatter_start(tag, group, grads):
        landings = [lax.empty((N_DEV - 1,) + shard_of[n], g.dtype) for n, g in zip(group, grads)]
        (handle,), tok = _exchange_start("scatter_start_" + tag, [(group, grads, landings)], [SCATTER])
        return handle, tok

    dv, g_ff1, g_ff2 = _mlp_bwd(vv, dm, full["w_ff1"], full["w_ff2"])
    h_ff, tok_ff = scatter_start("ff", ("w_ff1", "w_ff2"), [g_ff1, g_ff2])
    dh1, dy_conv, dy_pool, dgates, g_wo, g_mlp_pre, g_mix_post = _merge_bwd(
        dh2, dv, h1, row(w["mlp_pre_g"]), o, row(w["mix_post_g"]), full["w_o"], merged, proj, y_conv, y_pool, tok_ff)
    dp, g_wpo, g_pw, g_pool_scale = _pool_bwd(dy_pool, zl, z, full["pool_w"], row(w["pool_scale"]), full["w_pool_out"],
                                              dh1)
    h_pool, tok_pool = scatter_start("pool", ("w_o", "w_pool_out", "pool_w"), [g_wo, g_wpo, g_pw])
    d_glu, g_wco, g_dw, g_bias, g_ln_g, g_ln_b = _conv_bwd(dy_conv, cv, proj, full["dw_kernel"], row(w["conv_ln_g"]),
                                                            row(w["conv_ln_b"]), full["w_conv_out"], tok_pool)
    h_conv, tok_conv = scatter_start("conv", ("w_conv_out", "dw_kernel"), [g_wco, g_dw[:, :TAPS, :]])
    g_win = _in_proj_bwd_w(u_t, d_glu, dp, dgates, tok_conv)
    h_in, tok_in = scatter_start("in", ("w_in",), [g_win])
    grad_x, g_mix_pre = _in_proj_bwd_x(d_glu, dp, dgates, full["w_in"], x, row(w["mix_pre_g"]), dh1, tok_in)

    out = {}
    slot = jnp.reshape(_my_slot(), (1,)).astype(jnp.int32)
    after = grad_x
    for tag, (group, (sems,), grads, landings) in (("ff", h_ff), ("pool", h_pool), ("conv", h_conv), ("in", h_in)):
        if tag == "in":
            vec_parts = after = _vec_exchange(
                [g_mix_pre, g_bias, g_ln_g, g_ln_b, g_pool_scale, g_mix_post, g_mlp_pre, g_mlp_post,
                 jnp.broadcast_to(sse, (1, D))], after)
        mine, landed = _exchange_wait("scatter_wait_" + tag, group, grads, landings, [(sems, SCATTER)], after)
        for name, own, parts in zip(group, mine, landed):
            out[name] = _adamw(name, own, parts, w[name], m[name], v[name], slot, after)
            after = out[name][0]
    stack = lambda d: jnp.stack([d[n] for n in VEC_NAMES], axis=0)
    *res, loss_row = _adamw_vectors(vec_parts, stack(w), stack(m), stack(v))
    for k, name in enumerate(VEC_NAMES):
        out[name] = [r[k] for r in res]
    return loss_row[0, 0], grad_x, out


def kernel(x, mix_pre_g, w_in, dw_kernel, dw_bias, conv_ln_g, conv_ln_b, w_conv_out, pool_w, pool_scale, w_pool_out, w_o, mix_post_g, mlp_pre_g, w_ff1, w_ff2, mlp_post_g, loss_target, m_mix_pre_g, m_w_in, m_dw_kernel, m_dw_bias, m_conv_ln_g, m_conv_ln_b, m_w_conv_out, m_pool_w, m_pool_scale, m_w_pool_out, m_w_o, m_mix_post_g, m_mlp_pre_g, m_w_ff1, m_w_ff2, m_mlp_post_g, v_mix_pre_g, v_w_in, v_dw_kernel, v_dw_bias, v_conv_ln_g, v_conv_ln_b, v_w_conv_out, v_pool_w, v_pool_scale, v_w_pool_out, v_w_o, v_mix_post_g, v_mlp_pre_g, v_w_ff1, v_w_ff2, v_mlp_post_g):
    w = dict(mix_pre_g=mix_pre_g, w_in=w_in, dw_kernel=dw_kernel, dw_bias=dw_bias, conv_ln_g=conv_ln_g, conv_ln_b=conv_ln_b,
             w_conv_out=w_conv_out, pool_w=pool_w, pool_scale=pool_scale, w_pool_out=w_pool_out, w_o=w_o,
             mix_post_g=mix_post_g, mlp_pre_g=mlp_pre_g, w_ff1=w_ff1, w_ff2=w_ff2, mlp_post_g=mlp_post_g)
    m = dict(mix_pre_g=m_mix_pre_g, w_in=m_w_in, dw_kernel=m_dw_kernel, dw_bias=m_dw_bias, conv_ln_g=m_conv_ln_g,
             conv_ln_b=m_conv_ln_b, w_conv_out=m_w_conv_out, pool_w=m_pool_w, pool_scale=m_pool_scale,
             w_pool_out=m_w_pool_out, w_o=m_w_o, mix_post_g=m_mix_post_g, mlp_pre_g=m_mlp_pre_g, w_ff1=m_w_ff1,
             w_ff2=m_w_ff2, mlp_post_g=m_mlp_post_g)
    v = dict(mix_pre_g=v_mix_pre_g, w_in=v_w_in, dw_kernel=v_dw_kernel, dw_bias=v_dw_bias, conv_ln_g=v_conv_ln_g,
             conv_ln_b=v_conv_ln_b, w_conv_out=v_w_conv_out, pool_w=v_pool_w, pool_scale=v_pool_scale,
             w_pool_out=v_w_pool_out, w_o=v_w_o, mix_post_g=v_mix_post_g, mlp_pre_g=v_mlp_pre_g, w_ff1=v_w_ff1,
             w_ff2=v_w_ff2, mlp_post_g=v_mlp_post_g)
    seq = x.shape[1]
    loss, grad_x, out = _step(x.reshape(seq, D), loss_target.reshape(seq, D), w, m, v)
    grads, deltas, new_m, new_v = ([out[n][k] for n in WEIGHT_ORDER] for k in range(4))
    return (loss, grad_x.reshape(x.shape), *grads, *deltas, *new_m, *new_v)
```
